```python
import jax, jax.numpy as jnp
from jax import lax
import numpy as np

D_MODEL = 1024
BATCH = 16
SEQ = 4096
DEPTH = 2

N_MIXERS = 2
N_A = (DEPTH + 1) // 2
N_B = DEPTH // 2

CHUNK = 128
SGU_WIDTH = D_MODEL
SGU_GROUPS = 8
SGU_GROUP_DIM = SGU_WIDTH // SGU_GROUPS

RWKV_HEAD = 64
RWKV_HEADS = D_MODEL // RWKV_HEAD
DECAY_LORA = 64
AAA_LORA = 64
GATE_LORA = 160

D_FF = 4 * D_MODEL
N_MOD = 6

RMS_EPS = 1e-6
LN_EPS = 1e-5
GN_EPS = RWKV_HEAD * 1e-5
L2_EPS = 1e-12

kernel_name = "hybrid_sgu_rwkv7_adaln_trunk"


def rms_norm(x):
    xf = x.astype(jnp.float32)
    y = xf * lax.rsqrt(jnp.mean(xf * xf, axis=-1, keepdims=True) + RMS_EPS)
    return y.astype(x.dtype)


def layer_norm(x, g, b):
    xf = x.astype(jnp.float32)
    mu = jnp.mean(xf, axis=-1, keepdims=True)
    var = jnp.mean(jnp.square(xf - mu), axis=-1, keepdims=True)
    y = (xf - mu) * lax.rsqrt(var + LN_EPS)
    return y.astype(x.dtype) * g + b


def modulate(h, shift, scale):
    return h * (1 + scale[:, None, :]) + shift[:, None, :]


def token_shift(x):
    return jnp.pad(x[:, :-1], ((0, 0), (1, 0), (0, 0)))


def sgu_mixer(h, w_in, ln_g, ln_b, w_s, b_s, w_out):
    B, T, _ = h.shape
    uv = jax.nn.gelu(h @ w_in, approximate=False)
    u, v = jnp.split(uv, 2, axis=-1)
    v = layer_norm(v, ln_g, ln_b)
    vc = v.reshape(B, T // CHUNK, CHUNK, SGU_GROUPS, SGU_GROUP_DIM)
    mask = jnp.tril(jnp.ones((CHUNK, CHUNK), dtype=w_s.dtype))
    sv = jnp.einsum('gts,bcsgd->bctgd', w_s * mask, vc) + b_s.T[:, :, None]
    return (u * sv.reshape(B, T, SGU_WIDTH)) @ w_out


def wkv7_scan(r, w, k, v, a_, b_):
    B, T, H, N = r.shape
    seq = tuple(jnp.swapaxes(z.astype(jnp.float32), 0, 1) for z in (r, w, k, v, a_, b_))

    def step(S, inp):
        r_t, w_t, k_t, v_t, a_t, b_t = inp
        sa = jnp.einsum('bhij,bhj->bhi', S, a_t)
        S = S * w_t[:, :, None, :] + sa[..., None] * b_t[:, :, None, :] + v_t[..., None] * k_t[:, :, None, :]
        y = jnp.einsum('bhij,bhj->bhi', S, r_t)
        return S, y

    S0 = jnp.zeros((B, H, N, N), jnp.float32)
    _, ys = lax.scan(step, S0, seq)
    return jnp.swapaxes(ys, 0, 1).astype(r.dtype)


def rwkv7_mixer(h, mu, w_in, w0, w1, w2, a0, a1, a2, g1, g2, k_k, k_a, r_k, ln_g, ln_b, w_out):
    B, T, D = h.shape
    H, N = RWKV_HEADS, RWKV_HEAD
    xx = token_shift(h) - h
    xr, xw, xk, xv, xa, xg = [h + xx * mu[i] for i in range(6)]
    rkv = jnp.einsum('nbtd,dne->nbte', jnp.stack([xr, xk, xv]), w_in.reshape(D, 3, D))
    r, k, v = rkv[0], rkv[1], rkv[2]
    w_log = -jax.nn.softplus(-(w0 + jnp.tanh(xw @ w1) @ w2)) - 0.5
    decay = jnp.exp(-jnp.exp(w_log))
    a = jax.nn.sigmoid(a0 + (xa @ a1) @ a2)
    g = jax.nn.sigmoid(xg @ g1) @ g2
    kk = (k * k_k).reshape(B, T, H, N)
    kkf = kk.astype(jnp.float32)
    kk = (kkf / jnp.maximum(jnp.sqrt(jnp.sum(kkf * kkf, axis=-1, keepdims=True)), L2_EPS)).astype(h.dtype)
    k = k * (1 + (a - 1) * k_a)
    rh = r.reshape(B, T, H, N)
    kh = k.reshape(B, T, H, N)
    vh = v.reshape(B, T, H, N)
    ah = a.reshape(B, T, H, N)
    y = wkv7_scan(rh, decay.reshape(B, T, H, N), kh, vh, -kk, kk * ah)
    yf = y.astype(jnp.float32)
    m = jnp.mean(yf, axis=-1, keepdims=True)
    var = jnp.mean(jnp.square(yf - m), axis=-1, keepdims=True)
    y = ((yf - m) * lax.rsqrt(var + GN_EPS)).astype(h.dtype).reshape(B, T, D) * ln_g + ln_b
    bonus = jnp.sum(rh * kh * r_k, axis=-1, keepdims=True) * vh
    y = y + bonus.reshape(B, T, D)
    return (y * g) @ w_out


def _fwd_setup_inputs(seed: int = 0) -> dict:
    key = jax.random.key(seed)
    ks = iter(jax.random.split(key, 48))
    D = D_MODEL

    def nrm(shape, scale):
        return jax.random.normal(next(ks), shape, jnp.float32) * scale

    def unif(shape, lo, hi):
        return jax.random.uniform(next(ks), shape, jnp.float32, lo, hi)

    return {
        "x": nrm((BATCH, SEQ, D), 1.0),
        "c": nrm((BATCH, D), 1.0),
        "ada_w": nrm((DEPTH, D, N_MOD * D), 0.3 * D ** -0.5),
        "ada_b": nrm((DEPTH, N_MOD * D), 0.02),
        "mlp_w1": nrm((DEPTH, D, D_FF), D ** -0.5),
        "mlp_w2": nrm((DEPTH, D_FF, D), D_FF ** -0.5),
        "a_w_in": nrm((N_A, D, 2 * SGU_WIDTH), D ** -0.5),
        "a_ln_g": 1.0 + nrm((N_A, SGU_WIDTH), 0.02),
        "a_ln_b": nrm((N_A, SGU_WIDTH), 0.02),
        "a_w_s": nrm((N_A, SGU_GROUPS, CHUNK, CHUNK), CHUNK ** -0.5),
        "a_b_s": 1.0 + nrm((N_A, SGU_GROUPS, CHUNK), 0.02),
        "a_w_out": nrm((N_A, SGU_WIDTH, D), SGU_WIDTH ** -0.5),
        "b_mu": unif((N_B, 6, D), 0.0, 1.0),
        "b_w_in": nrm((N_B, D, 3 * D), D ** -0.5),
        "b_w0": unif((N_B, D), -7.0, -2.0),
        "b_w1": nrm((N_B, D, DECAY_LORA), D ** -0.5),
        "b_w2": nrm((N_B, DECAY_LORA, D), 0.1 * DECAY_LORA ** -0.5),
        "b_a0": nrm((N_B, D), 0.5),
        "b_a1": nrm((N_B, D, AAA_LORA), D ** -0.5),
        "b_a2": nrm((N_B, AAA_LORA, D), 0.5 * AAA_LORA ** -0.5),
        "b_g1": nrm((N_B, D, GATE_LORA), D ** -0.5),
        "b_g2": nrm((N_B, GATE_LORA, D), GATE_LORA ** -0.5),
        "b_k_k": 0.85 + nrm((N_B, D), 0.02),
        "b_k_a": 1.0 + nrm((N_B, D), 0.02),
        "b_r_k": -0.04 + nrm((N_B, RWKV_HEADS, RWKV_HEAD), 0.02),
        "b_ln_g": 1.0 + nrm((N_B, D), 0.02),
        "b_ln_b": nrm((N_B, D), 0.02),
        "b_w_out": nrm((N_B, D, D), D ** -0.5),
        "final_g": 1.0 + nrm((D,), 0.02),
    }


def _fwd_reference(x, c, ada_w, ada_b, mlp_w1, mlp_w2,
              a_w_in, a_ln_g, a_ln_b, a_w_s, a_b_s, a_w_out,
              b_mu, b_w_in, b_w0, b_w1, b_w2, b_a0, b_a1, b_a2, b_g1, b_g2,
              b_k_k, b_k_a, b_r_k, b_ln_g, b_ln_b, b_w_out, final_g):
    cond = jax.nn.silu(c)
    for i in range(DEPTH):
        mod = cond @ ada_w[i] + ada_b[i]
        shift1, scale1, gate1, shift2, scale2, gate2 = jnp.split(mod, N_MOD, axis=-1)
        h = modulate(rms_norm(x), shift1, scale1)
        j = i // N_MIXERS
        if i % N_MIXERS == 0:
            mix = sgu_mixer(h, a_w_in[j], a_ln_g[j], a_ln_b[j], a_w_s[j], a_b_s[j], a_w_out[j])
        else:
            mix = rwkv7_mixer(h, b_mu[j], b_w_in[j], b_w0[j], b_w1[j], b_w2[j],
                              b_a0[j], b_a1[j], b_a2[j], b_g1[j], b_g2[j],
                              b_k_k[j], b_k_a[j], b_r_k[j], b_ln_g[j], b_ln_b[j], b_w_out[j])
        x = x + gate1[:, None, :] * mix
        h = modulate(rms_norm(x), shift2, scale2)
        ff = jnp.square(jax.nn.relu(h @ mlp_w1[i])) @ mlp_w2[i]
        x = x + gate2[:, None, :] * ff
    return rms_norm(x) * final_g


import jax as _jax
import jax.numpy as _jnp

TWIN_FORMAT = 'train_step'
FWD_PARAMS = ['x', 'c', 'ada_w', 'ada_b', 'mlp_w1', 'mlp_w2', 'a_w_in', 'a_ln_g', 'a_ln_b', 'a_w_s', 'a_b_s', 'a_w_out', 'b_mu', 'b_w_in', 'b_w0', 'b_w1', 'b_w2', 'b_a0', 'b_a1', 'b_a2', 'b_g1', 'b_g2', 'b_k_k', 'b_k_a', 'b_r_k', 'b_ln_g', 'b_ln_b', 'b_w_out', 'final_g']
TWIN_WEIGHTS = ['ada_w', 'ada_b', 'mlp_w1', 'mlp_w2', 'a_w_in', 'a_ln_g', 'a_ln_b', 'a_w_s', 'a_b_s', 'a_w_out', 'b_mu', 'b_w_in', 'b_w0', 'b_w1', 'b_w2', 'b_a0', 'b_a1', 'b_a2', 'b_g1', 'b_g2', 'b_k_k', 'b_k_a', 'b_r_k', 'b_ln_g', 'b_ln_b', 'b_w_out', 'final_g']
TWIN_DIFF_INPUT = 'x'
TWIN_INPUTS = ['x', 'c', 'ada_w', 'ada_b', 'mlp_w1', 'mlp_w2', 'a_w_in', 'a_ln_g', 'a_ln_b', 'a_w_s', 'a_b_s', 'a_w_out', 'b_mu', 'b_w_in', 'b_w0', 'b_w1', 'b_w2', 'b_a0', 'b_a1', 'b_a2', 'b_g1', 'b_g2', 'b_k_k', 'b_k_a', 'b_r_k', 'b_ln_g', 'b_ln_b', 'b_w_out', 'final_g', 'loss_target', 'm_ada_w', 'm_ada_b', 'm_mlp_w1', 'm_mlp_w2', 'm_a_w_in', 'm_a_ln_g', 'm_a_ln_b', 'm_a_w_s', 'm_a_b_s', 'm_a_w_out', 'm_b_mu', 'm_b_w_in', 'm_b_w0', 'm_b_w1', 'm_b_w2', 'm_b_a0', 'm_b_a1', 'm_b_a2', 'm_b_g1', 'm_b_g2', 'm_b_k_k', 'm_b_k_a', 'm_b_r_k', 'm_b_ln_g', 'm_b_ln_b', 'm_b_w_out', 'm_final_g', 'v_ada_w', 'v_ada_b', 'v_mlp_w1', 'v_mlp_w2', 'v_a_w_in', 'v_a_ln_g', 'v_a_ln_b', 'v_a_w_s', 'v_a_b_s', 'v_a_w_out', 'v_b_mu', 'v_b_w_in', 'v_b_w0', 'v_b_w1', 'v_b_w2', 'v_b_a0', 'v_b_a1', 'v_b_a2', 'v_b_g1', 'v_b_g2', 'v_b_k_k', 'v_b_k_a', 'v_b_r_k', 'v_b_ln_g', 'v_b_ln_b', 'v_b_w_out', 'v_final_g']
TWIN_OUTPUTS = ['loss', 'grad_x', 'grad_ada_w', 'grad_ada_b', 'grad_mlp_w1', 'grad_mlp_w2', 'grad_a_w_in', 'grad_a_ln_g', 'grad_a_ln_b', 'grad_a_w_s', 'grad_a_b_s', 'grad_a_w_out', 'grad_b_mu', 'grad_b_w_in', 'grad_b_w0', 'grad_b_w1', 'grad_b_w2', 'grad_b_a0', 'grad_b_a1', 'grad_b_a2', 'grad_b_g1', 'grad_b_g2', 'grad_b_k_k', 'grad_b_k_a', 'grad_b_r_k', 'grad_b_ln_g', 'grad_b_ln_b', 'grad_b_w_out', 'grad_final_g', 'delta_ada_w', 'delta_ada_b', 'delta_mlp_w1', 'delta_mlp_w2', 'delta_a_w_in', 'delta_a_ln_g', 'delta_a_ln_b', 'delta_a_w_s', 'delta_a_b_s', 'delta_a_w_out', 'delta_b_mu', 'delta_b_w_in', 'delta_b_w0', 'delta_b_w1', 'delta_b_w2', 'delta_b_a0', 'delta_b_a1', 'delta_b_a2', 'delta_b_g1', 'delta_b_g2', 'delta_b_k_k', 'delta_b_k_a', 'delta_b_r_k', 'delta_b_ln_g', 'delta_b_ln_b', 'delta_b_w_out', 'delta_final_g', 'new_m_ada_w', 'new_m_ada_b', 'new_m_mlp_w1', 'new_m_mlp_w2', 'new_m_a_w_in', 'new_m_a_ln_g', 'new_m_a_ln_b', 'new_m_a_w_s', 'new_m_a_b_s', 'new_m_a_w_out', 'new_m_b_mu', 'new_m_b_w_in', 'new_m_b_w0', 'new_m_b_w1', 'new_m_b_w2', 'new_m_b_a0', 'new_m_b_a1', 'new_m_b_a2', 'new_m_b_g1', 'new_m_b_g2', 'new_m_b_k_k', 'new_m_b_k_a', 'new_m_b_r_k', 'new_m_b_ln_g', 'new_m_b_ln_b', 'new_m_b_w_out', 'new_m_final_g', 'new_v_ada_w', 'new_v_ada_b', 'new_v_mlp_w1', 'new_v_mlp_w2', 'new_v_a_w_in', 'new_v_a_ln_g', 'new_v_a_ln_b', 'new_v_a_w_s', 'new_v_a_b_s', 'new_v_a_w_out', 'new_v_b_mu', 'new_v_b_w_in', 'new_v_b_w0', 'new_v_b_w1', 'new_v_b_w2', 'new_v_b_a0', 'new_v_b_a1', 'new_v_b_a2', 'new_v_b_g1', 'new_v_b_g2', 'new_v_b_k_k', 'new_v_b_k_a', 'new_v_b_r_k', 'new_v_b_ln_g', 'new_v_b_ln_b', 'new_v_b_w_out', 'new_v_final_g']
TWIN_LEAF_KINDS = {'loss': 'loss', 'grad_x': 'grad_x', 'grad_ada_w': 'grad_w', 'grad_ada_b': 'grad_w', 'grad_mlp_w1': 'grad_w', 'grad_mlp_w2': 'grad_w', 'grad_a_w_in': 'grad_w', 'grad_a_ln_g': 'grad_w', 'grad_a_ln_b': 'grad_w', 'grad_a_w_s': 'grad_w', 'grad_a_b_s': 'grad_w', 'grad_a_w_out': 'grad_w', 'grad_b_mu': 'grad_w', 'grad_b_w_in': 'grad_w', 'grad_b_w0': 'grad_w', 'grad_b_w1': 'grad_w', 'grad_b_w2': 'grad_w', 'grad_b_a0': 'grad_w', 'grad_b_a1': 'grad_w', 'grad_b_a2': 'grad_w', 'grad_b_g1': 'grad_w', 'grad_b_g2': 'grad_w', 'grad_b_k_k': 'grad_w', 'grad_b_k_a': 'grad_w', 'grad_b_r_k': 'grad_w', 'grad_b_ln_g': 'grad_w', 'grad_b_ln_b': 'grad_w', 'grad_b_w_out': 'grad_w', 'grad_final_g': 'grad_w', 'delta_ada_w': 'delta_w', 'delta_ada_b': 'delta_w', 'delta_mlp_w1': 'delta_w', 'delta_mlp_w2': 'delta_w', 'delta_a_w_in': 'delta_w', 'delta_a_ln_g': 'delta_w', 'delta_a_ln_b': 'delta_w', 'delta_a_w_s': 'delta_w', 'delta_a_b_s': 'delta_w', 'delta_a_w_out': 'delta_w', 'delta_b_mu': 'delta_w', 'delta_b_w_in': 'delta_w', 'delta_b_w0': 'delta_w', 'delta_b_w1': 'delta_w', 'delta_b_w2': 'delta_w', 'delta_b_a0': 'delta_w', 'delta_b_a1': 'delta_w', 'delta_b_a2': 'delta_w', 'delta_b_g1': 'delta_w', 'delta_b_g2': 'delta_w', 'delta_b_k_k': 'delta_w', 'delta_b_k_a': 'delta_w', 'delta_b_r_k': 'delta_w', 'delta_b_ln_g': 'delta_w', 'delta_b_ln_b': 'delta_w', 'delta_b_w_out': 'delta_w', 'delta_final_g': 'delta_w', 'new_m_ada_w': 'new_m', 'new_m_ada_b': 'new_m', 'new_m_mlp_w1': 'new_m', 'new_m_mlp_w2': 'new_m', 'new_m_a_w_in': 'new_m', 'new_m_a_ln_g': 'new_m', 'new_m_a_ln_b': 'new_m', 'new_m_a_w_s': 'new_m', 'new_m_a_b_s': 'new_m', 'new_m_a_w_out': 'new_m', 'new_m_b_mu': 'new_m', 'new_m_b_w_in': 'new_m', 'new_m_b_w0': 'new_m', 'new_m_b_w1': 'new_m', 'new_m_b_w2': 'new_m', 'new_m_b_a0': 'new_m', 'new_m_b_a1': 'new_m', 'new_m_b_a2': 'new_m', 'new_m_b_g1': 'new_m', 'new_m_b_g2': 'new_m', 'new_m_b_k_k': 'new_m', 'new_m_b_k_a': 'new_m', 'new_m_b_r_k': 'new_m', 'new_m_b_ln_g': 'new_m', 'new_m_b_ln_b': 'new_m', 'new_m_b_w_out': 'new_m', 'new_m_final_g': 'new_m', 'new_v_ada_w': 'new_v', 'new_v_ada_b': 'new_v', 'new_v_mlp_w1': 'new_v', 'new_v_mlp_w2': 'new_v', 'new_v_a_w_in': 'new_v', 'new_v_a_ln_g': 'new_v', 'new_v_a_ln_b': 'new_v', 'new_v_a_w_s': 'new_v', 'new_v_a_b_s': 'new_v', 'new_v_a_w_out': 'new_v', 'new_v_b_mu': 'new_v', 'new_v_b_w_in': 'new_v', 'new_v_b_w0': 'new_v', 'new_v_b_w1': 'new_v', 'new_v_b_w2': 'new_v', 'new_v_b_a0': 'new_v', 'new_v_b_a1': 'new_v', 'new_v_b_a2': 'new_v', 'new_v_b_g1': 'new_v', 'new_v_b_g2': 'new_v', 'new_v_b_k_k': 'new_v', 'new_v_b_k_a': 'new_v', 'new_v_b_r_k': 'new_v', 'new_v_b_ln_g': 'new_v', 'new_v_b_ln_b': 'new_v', 'new_v_b_w_out': 'new_v', 'new_v_final_g': 'new_v'}


def _forward(args):
    return _fwd_reference(*[args[k] for k in FWD_PARAMS])


def _output_shape():
    out = _jax.eval_shape(lambda: _forward(_fwd_setup_inputs(0)))
    return out.shape, out.dtype

N_MICROBATCH = 1
ADAM_LR = 0.001
ADAM_B1 = 0.9
ADAM_B2 = 0.999
ADAM_EPS = 1e-08
ADAM_WD = 0.01
ADAM_STEP = 10
PER_EXAMPLE_BATCH_AXIS = {'x': 0, 'c': 0, 'loss_target': 0}
SHARED_INPUTS = []
_WEIGHT_DTYPES = {'ada_w': _jnp.float32, 'ada_b': _jnp.float32, 'mlp_w1': _jnp.float32, 'mlp_w2': _jnp.float32, 'a_w_in': _jnp.float32, 'a_ln_g': _jnp.float32, 'a_ln_b': _jnp.float32, 'a_w_s': _jnp.float32, 'a_b_s': _jnp.float32, 'a_w_out': _jnp.float32, 'b_mu': _jnp.float32, 'b_w_in': _jnp.float32, 'b_w0': _jnp.float32, 'b_w1': _jnp.float32, 'b_w2': _jnp.float32, 'b_a0': _jnp.float32, 'b_a1': _jnp.float32, 'b_a2': _jnp.float32, 'b_g1': _jnp.float32, 'b_g2': _jnp.float32, 'b_k_k': _jnp.float32, 'b_k_a': _jnp.float32, 'b_r_k': _jnp.float32, 'b_ln_g': _jnp.float32, 'b_ln_b': _jnp.float32, 'b_w_out': _jnp.float32, 'final_g': _jnp.float32}
MOMENT_SCALE = {'ada_w': 1.030354e-01, 'ada_b': 1.752695e-01, 'mlp_w1': 3.344517e-02, 'mlp_w2': 5.857515e-02, 'a_w_in': 3.398604e-02, 'a_ln_g': 2.294233e-02, 'a_ln_b': 2.309107e-02, 'a_w_s': 2.292798e-02, 'a_b_s': 3.145364e-02, 'a_w_out': 3.894810e-02, 'b_mu': 2.986565e-02, 'b_w_in': 2.653943e-02, 'b_w0': 9.271156e-03, 'b_w1': 6.796733e-04, 'b_w2': 1.521113e-03, 'b_a0': 9.976400e-03, 'b_a1': 1.921443e-02, 'b_a2': 9.685203e-03, 'b_g1': 2.158100e-02, 'b_g2': 2.570909e-02, 'b_k_k': 7.598379e-02, 'b_k_a': 4.052490e-02, 'b_r_k': 6.583739e-02, 'b_ln_g': 2.611031e-02, 'b_ln_b': 2.806022e-02, 'b_w_out': 2.466473e-02, 'final_g': 6.402083e+01}


def _to_microbatches(a, axis):
    t = _jnp.moveaxis(a, axis, 0)
    t = t.reshape((N_MICROBATCH, t.shape[0] // N_MICROBATCH) + t.shape[1:])
    return _jnp.moveaxis(t, 1, axis + 1)


def setup_inputs(seed: int = 0) -> dict:
    inp = _fwd_setup_inputs(seed)
    key = _jax.random.fold_in(_jax.random.key(seed), 7919)
    shape, _ = _output_shape()
    out = dict(inp)
    out["loss_target"] = _jax.random.normal(_jax.random.fold_in(key, 0), shape, _jnp.float32)
    for i, name in enumerate(TWIN_WEIGHTS):
        w = inp[name].astype(_jnp.float32)
        if MOMENT_SCALE is None:
            s = _jnp.sqrt(_jnp.mean(_jnp.square(w)) + 1e-30)
        else:
            s = MOMENT_SCALE[name]
        km, kv = _jax.random.split(_jax.random.fold_in(key, i + 1))
        out[name] = w
        out["m_" + name] = s * _jax.random.normal(km, w.shape, _jnp.float32)
        out["v_" + name] = (s * s) * _jax.random.uniform(kv, w.shape, _jnp.float32, 0.5, 1.5)
    if N_MICROBATCH > 1:
        for name, axis in PER_EXAMPLE_BATCH_AXIS.items():
            out[name] = _to_microbatches(out[name], axis)
    return {'x': out['x'], 'c': out['c'], 'ada_w': out['ada_w'], 'ada_b': out['ada_b'], 'mlp_w1': out['mlp_w1'], 'mlp_w2': out['mlp_w2'], 'a_w_in': out['a_w_in'], 'a_ln_g': out['a_ln_g'], 'a_ln_b': out['a_ln_b'], 'a_w_s': out['a_w_s'], 'a_b_s': out['a_b_s'], 'a_w_out': out['a_w_out'], 'b_mu': out['b_mu'], 'b_w_in': out['b_w_in'], 'b_w0': out['b_w0'], 'b_w1': out['b_w1'], 'b_w2': out['b_w2'], 'b_a0': out['b_a0'], 'b_a1': out['b_a1'], 'b_a2': out['b_a2'], 'b_g1': out['b_g1'], 'b_g2': out['b_g2'], 'b_k_k': out['b_k_k'], 'b_k_a': out['b_k_a'], 'b_r_k': out['b_r_k'], 'b_ln_g': out['b_ln_g'], 'b_ln_b': out['b_ln_b'], 'b_w_out': out['b_w_out'], 'final_g': out['final_g'], 'loss_target': out['loss_target'], 'm_ada_w': out['m_ada_w'], 'm_ada_b': out['m_ada_b'], 'm_mlp_w1': out['m_mlp_w1'], 'm_mlp_w2': out['m_mlp_w2'], 'm_a_w_in': out['m_a_w_in'], 'm_a_ln_g': out['m_a_ln_g'], 'm_a_ln_b': out['m_a_ln_b'], 'm_a_w_s': out['m_a_w_s'], 'm_a_b_s': out['m_a_b_s'], 'm_a_w_out': out['m_a_w_out'], 'm_b_mu': out['m_b_mu'], 'm_b_w_in': out['m_b_w_in'], 'm_b_w0': out['m_b_w0'], 'm_b_w1': out['m_b_w1'], 'm_b_w2': out['m_b_w2'], 'm_b_a0': out['m_b_a0'], 'm_b_a1': out['m_b_a1'], 'm_b_a2': out['m_b_a2'], 'm_b_g1': out['m_b_g1'], 'm_b_g2': out['m_b_g2'], 'm_b_k_k': out['m_b_k_k'], 'm_b_k_a': out['m_b_k_a'], 'm_b_r_k': out['m_b_r_k'], 'm_b_ln_g': out['m_b_ln_g'], 'm_b_ln_b': out['m_b_ln_b'], 'm_b_w_out': out['m_b_w_out'], 'm_final_g': out['m_final_g'], 'v_ada_w': out['v_ada_w'], 'v_ada_b': out['v_ada_b'], 'v_mlp_w1': out['v_mlp_w1'], 'v_mlp_w2': out['v_mlp_w2'], 'v_a_w_in': out['v_a_w_in'], 'v_a_ln_g': out['v_a_ln_g'], 'v_a_ln_b': out['v_a_ln_b'], 'v_a_w_s': out['v_a_w_s'], 'v_a_b_s': out['v_a_b_s'], 'v_a_w_out': out['v_a_w_out'], 'v_b_mu': out['v_b_mu'], 'v_b_w_in': out['v_b_w_in'], 'v_b_w0': out['v_b_w0'], 'v_b_w1': out['v_b_w1'], 'v_b_w2': out['v_b_w2'], 'v_b_a0': out['v_b_a0'], 'v_b_a1': out['v_b_a1'], 'v_b_a2': out['v_b_a2'], 'v_b_g1': out['v_b_g1'], 'v_b_g2': out['v_b_g2'], 'v_b_k_k': out['v_b_k_k'], 'v_b_k_a': out['v_b_k_a'], 'v_b_r_k': out['v_b_r_k'], 'v_b_ln_g': out['v_b_ln_g'], 'v_b_ln_b': out['v_b_ln_b'], 'v_b_w_out': out['v_b_w_out'], 'v_final_g': out['v_final_g']}


def _loss(weights, diff, rest, loss_target):
    with _jax.named_scope("forward"):
        args = {**rest, TWIN_DIFF_INPUT: diff, **{k: w.astype(_WEIGHT_DTYPES[k]) for k, w in weights.items()}}
        y = _forward(args)
    with _jax.named_scope("loss_head"):
        err = _jnp.square(y.astype(_jnp.float32) - loss_target)
        return 0.5 * _jnp.sum(_jnp.mean(err, axis=-1)) if err.ndim else 0.5 * err


def _adamw(w, g, m, v):
    m = ADAM_B1 * m + (1.0 - ADAM_B1) * g
    v = ADAM_B2 * v + (1.0 - ADAM_B2) * _jnp.square(g)
    m_hat = m / (1.0 - ADAM_B1 ** ADAM_STEP)
    v_hat = v / (1.0 - ADAM_B2 ** ADAM_STEP)
    delta = -ADAM_LR * (m_hat / (_jnp.sqrt(v_hat) + ADAM_EPS) + ADAM_WD * w)
    return delta, m, v


def reference(x, c, ada_w, ada_b, mlp_w1, mlp_w2, a_w_in, a_ln_g, a_ln_b, a_w_s, a_b_s, a_w_out, b_mu, b_w_in, b_w0, b_w1, b_w2, b_a0, b_a1, b_a2, b_g1, b_g2, b_k_k, b_k_a, b_r_k, b_ln_g, b_ln_b, b_w_out, final_g, loss_target, m_ada_w, m_ada_b, m_mlp_w1, m_mlp_w2, m_a_w_in, m_a_ln_g, m_a_ln_b, m_a_w_s, m_a_b_s, m_a_w_out, m_b_mu, m_b_w_in, m_b_w0, m_b_w1, m_b_w2, m_b_a0, m_b_a1, m_b_a2, m_b_g1, m_b_g2, m_b_k_k, m_b_k_a, m_b_r_k, m_b_ln_g, m_b_ln_b, m_b_w_out, m_final_g, v_ada_w, v_ada_b, v_mlp_w1, v_mlp_w2, v_a_w_in, v_a_ln_g, v_a_ln_b, v_a_w_s, v_a_b_s, v_a_w_out, v_b_mu, v_b_w_in, v_b_w0, v_b_w1, v_b_w2, v_b_a0, v_b_a1, v_b_a2, v_b_g1, v_b_g2, v_b_k_k, v_b_k_a, v_b_r_k, v_b_ln_g, v_b_ln_b, v_b_w_out, v_final_g):
    given = dict(x=x, c=c, ada_w=ada_w, ada_b=ada_b, mlp_w1=mlp_w1, mlp_w2=mlp_w2, a_w_in=a_w_in, a_ln_g=a_ln_g, a_ln_b=a_ln_b, a_w_s=a_w_s, a_b_s=a_b_s, a_w_out=a_w_out, b_mu=b_mu, b_w_in=b_w_in, b_w0=b_w0, b_w1=b_w1, b_w2=b_w2, b_a0=b_a0, b_a1=b_a1, b_a2=b_a2, b_g1=b_g1, b_g2=b_g2, b_k_k=b_k_k, b_k_a=b_k_a, b_r_k=b_r_k, b_ln_g=b_ln_g, b_ln_b=b_ln_b, b_w_out=b_w_out, final_g=final_g, loss_target=loss_target, m_ada_w=m_ada_w, m_ada_b=m_ada_b, m_mlp_w1=m_mlp_w1, m_mlp_w2=m_mlp_w2, m_a_w_in=m_a_w_in, m_a_ln_g=m_a_ln_g, m_a_ln_b=m_a_ln_b, m_a_w_s=m_a_w_s, m_a_b_s=m_a_b_s, m_a_w_out=m_a_w_out, m_b_mu=m_b_mu, m_b_w_in=m_b_w_in, m_b_w0=m_b_w0, m_b_w1=m_b_w1, m_b_w2=m_b_w2, m_b_a0=m_b_a0, m_b_a1=m_b_a1, m_b_a2=m_b_a2, m_b_g1=m_b_g1, m_b_g2=m_b_g2, m_b_k_k=m_b_k_k, m_b_k_a=m_b_k_a, m_b_r_k=m_b_r_k, m_b_ln_g=m_b_ln_g, m_b_ln_b=m_b_ln_b, m_b_w_out=m_b_w_out, m_final_g=m_final_g, v_ada_w=v_ada_w, v_ada_b=v_ada_b, v_mlp_w1=v_mlp_w1, v_mlp_w2=v_mlp_w2, v_a_w_in=v_a_w_in, v_a_ln_g=v_a_ln_g, v_a_ln_b=v_a_ln_b, v_a_w_s=v_a_w_s, v_a_b_s=v_a_b_s, v_a_w_out=v_a_w_out, v_b_mu=v_b_mu, v_b_w_in=v_b_w_in, v_b_w0=v_b_w0, v_b_w1=v_b_w1, v_b_w2=v_b_w2, v_b_a0=v_b_a0, v_b_a1=v_b_a1, v_b_a2=v_b_a2, v_b_g1=v_b_g1, v_b_g2=v_b_g2, v_b_k_k=v_b_k_k, v_b_k_a=v_b_k_a, v_b_r_k=v_b_r_k, v_b_ln_g=v_b_ln_g, v_b_ln_b=v_b_ln_b, v_b_w_out=v_b_w_out, v_final_g=v_final_g)
    weights = {n: given[n] for n in TWIN_WEIGHTS}
    shared = {n: given[n] for n in SHARED_INPUTS}
    per_example = {n: given[n] for n in ['x', 'c']}
    grad_fn = _jax.value_and_grad(_loss, argnums=(0, 1))

    def one_microbatch(ex, loss_target):
        ex = dict(ex)
        diff = ex.pop(TWIN_DIFF_INPUT)
        return grad_fn(weights, diff, {**shared, **ex}, loss_target)

    if N_MICROBATCH == 1:
        loss, (grad_w, grad_x) = one_microbatch(per_example, given["loss_target"])
    else:
        def body(carry, xs):
            loss_sum, grad_sum = carry
            l_k, (gw_k, gx_k) = one_microbatch(xs[0], xs[1])
            with _jax.named_scope("update"):
                return (loss_sum + l_k, _jax.tree.map(_jnp.add, grad_sum, gw_k)), gx_k

        init = (_jnp.zeros((), _jnp.float32), _jax.tree.map(_jnp.zeros_like, weights))
        (loss, grad_w), grad_x = _jax.lax.scan(body, init, (per_example, given["loss_target"]))
    with _jax.named_scope("update"):
        delta_w, new_m, new_v = {}, {}, {}
        for n in TWIN_WEIGHTS:
            delta_w[n], new_m[n], new_v[n] = _adamw(weights[n], grad_w[n], given["m_" + n], given["v_" + n])
    return (loss, grad_x, *[grad_w[n] for n in TWIN_WEIGHTS], *[delta_w[n] for n in TWIN_WEIGHTS],
            *[new_m[n] for n in TWIN_WEIGHTS], *[new_v[n] for n in TWIN_WEIGHTS])
```

```python
import functools

import jax
import jax.numpy as jnp
from jax import lax
from jax.experimental import pallas as pl
from jax.experimental.pallas import tpu as pltpu

F32 = jnp.float32
BF16 = jnp.bfloat16
HIGHEST = lax.Precision.HIGHEST

N_DEV = 8
RMS_EPS = 1e-6
LN_EPS = 1e-5
HEAD = 64
GN_EPS = HEAD * 1e-5
L2_EPS = 1e-12
SGU_CHUNK = 128
SGU_GROUPS = 8
WKV_CHUNK = 64
WKV_HEADS_PER_STEP = 4
LORA_PAD = 128
GATE_PAD = 256
ADAM_LR, ADAM_B1, ADAM_B2, ADAM_EPS, ADAM_WD, ADAM_STEP = 0.001, 0.9, 0.999, 1e-08, 0.01, 10
VMEM_LIMIT = 56 * 1024 * 1024


def _cparams(sem=None, **kw):
    if sem is not None:
        kw["dimension_semantics"] = sem
    return pltpu.CompilerParams(vmem_limit_bytes=VMEM_LIMIT, **kw)


def _dot(a, b):
    return jnp.dot(a.astype(BF16), b.astype(BF16), preferred_element_type=F32)


def _dot_nt(a, b):
    return lax.dot_general(a.astype(BF16), b.astype(BF16), (((1,), (1,)), ((), ())), preferred_element_type=F32)


def _dot_tn(a, b):
    return lax.dot_general(a.astype(BF16), b.astype(BF16), (((0,), (0,)), ((), ())), preferred_element_type=F32)


def _bdot(a, b, dims):
    return lax.dot_general(a, b, (dims, ((0,), (0,))), precision=HIGHEST, preferred_element_type=F32)


@jax.custom_vjp
def _bmm_nn(a, b):
    return _bdot(a, b, ((2,), (1,)))


@jax.custom_vjp
def _bmm_nt(a, b):
    return _bdot(a, b, ((2,), (2,)))


@jax.custom_vjp
def _bmm_tn(a, b):
    return _bdot(a, b, ((1,), (1,)))


_bmm_nn.defvjp(lambda a, b: (_bmm_nn(a, b), (a, b)), lambda res, g: (_bmm_nt(g, res[1]), _bmm_tn(res[0], g)))
_bmm_nt.defvjp(lambda a, b: (_bmm_nt(a, b), (a, b)), lambda res, g: (_bmm_nn(g, res[1]), _bmm_tn(g, res[0])))
_bmm_tn.defvjp(lambda a, b: (_bmm_tn(a, b), (a, b)), lambda res, g: (_bmm_nt(res[1], g), _bmm_nn(res[0], g)))


def _wkv_chunk(s0, r, ld, k, v, a, b):
    nh, n, _ = r.shape
    row = lax.broadcasted_iota(jnp.int32, (n, n), 0)
    col = lax.broadcasted_iota(jnp.int32, (n, n), 1)
    incl = (row >= col)[None]
    strict = (row > col)[None]
    ones_incl = jnp.broadcast_to(jnp.where(incl, 1.0, 0.0).astype(F32), (nh, n, n))
    eye = jnp.where(row == col, 1.0, 0.0).astype(F32)[None]
    c = _bmm_nn(ones_incl, ld)
    c_end = c[:, n - 1:n, :]
    ec, enc, ecx, eend = jnp.exp(c), jnp.exp(-c), jnp.exp(c - ld), jnp.exp(c_end - c)
    at, rt, bt, kt = a * ecx, r * ec, b * enc, k * enc
    a_ab = jnp.where(strict, _bmm_nt(at, bt), 0.0)
    a_ak = jnp.where(strict, _bmm_nt(at, kt), 0.0)
    a_rb = jnp.where(incl, _bmm_nt(rt, bt), 0.0)
    a_rk = jnp.where(incl, _bmm_nt(rt, kt), 0.0)
    p = a_ab
    tinv = eye + p
    steps = max(1, (n - 1).bit_length()) - 1
    for _ in range(steps):
        p = _bmm_nn(p, p)
        tinv = tinv + _bmm_nn(tinv, p)
    u = _bmm_nn(tinv, _bmm_nt(at, s0) + _bmm_nn(a_ak, v))
    y = _bmm_nt(rt, s0) + _bmm_nn(a_rb, u) + _bmm_nn(a_rk, v)
    s1 = s0 * jnp.exp(c_end) + _bmm_tn(u, b * eend) + _bmm_tn(v, k * eend)
    return y, s1


def _wkv_specs(bl, nh, t):
    hb, lc = WKV_HEADS_PER_STEP, WKV_CHUNK
    return hb, lc, (bl, nh // hb, t // lc)


def _wkv_fwd(r, ld, k, v, a, b):
    bl, nh, t, n = r.shape
    hb, lc, grid = _wkv_specs(bl, nh, t)
    nc = t // lc

    def body(r_ref, ld_ref, k_ref, v_ref, a_ref, b_ref, y_ref, s0_ref, s_scr):
        @pl.when(pl.program_id(2) == 0)
        def _():
            s_scr[...] = jnp.zeros_like(s_scr)

        s0 = s_scr[...]
        s0_ref[0, :, 0] = s0
        y, s1 = _wkv_chunk(s0, r_ref[0], ld_ref[0], k_ref[0], v_ref[0], a_ref[0], b_ref[0])
        y_ref[0] = y
        s_scr[...] = s1

    seq = pl.BlockSpec((1, hb, lc, n), lambda e, h, c: (e, h, c, 0))
    return pl.pallas_call(
        body, name="wkv_fwd", grid=grid,
        in_specs=[seq] * 6,
        out_specs=(seq, pl.BlockSpec((1, hb, 1, n, n), lambda e, h, c: (e, h, c, 0, 0))),
        out_shape=(jax.ShapeDtypeStruct((bl, nh, t, n), F32), jax.ShapeDtypeStruct((bl, nh, nc, n, n), F32)),
        scratch_shapes=[pltpu.VMEM((hb, n, n), F32)],
        compiler_params=_cparams(("arbitrary", "arbitrary", "arbitrary")),
    )(r, ld, k, v, a, b)


def _wkv_bwd(r, ld, k, v, a, b, s0_all, dy):
    bl, nh, t, n = r.shape
    hb, lc, grid = _wkv_specs(bl, nh, t)
    nc = t // lc

    def body(r_ref, ld_ref, k_ref, v_ref, a_ref, b_ref, s0_ref, dy_ref,
             dr_ref, dld_ref, dk_ref, dv_ref, da_ref, db_ref, ds_scr):
        @pl.when(pl.program_id(2) == 0)
        def _():
            ds_scr[...] = jnp.zeros_like(ds_scr)

        args = (s0_ref[0, :, 0], r_ref[0], ld_ref[0], k_ref[0], v_ref[0], a_ref[0], b_ref[0])
        _, vjp = jax.vjp(_wkv_chunk, *args)
        ds0, dr, dld, dk, dv, da, db = vjp((dy_ref[0], ds_scr[...]))
        ds_scr[...] = ds0
        dr_ref[0], dld_ref[0], dk_ref[0], dv_ref[0], da_ref[0], db_ref[0] = dr, dld, dk, dv, da, db

    seq = pl.BlockSpec((1, hb, lc, n), lambda e, h, c: (e, h, nc - 1 - c, 0))
    st = pl.BlockSpec((1, hb, 1, n, n), lambda e, h, c: (e, h, nc - 1 - c, 0, 0))
    out = jax.ShapeDtypeStruct((bl, nh, t, n), F32)
    return pl.pallas_call(
        body, name="wkv_bwd", grid=grid,
        in_specs=[seq] * 6 + [st, seq],
        out_specs=(seq,) * 6, out_shape=(out,) * 6,
        scratch_shapes=[pltpu.VMEM((hb, n, n), F32)],
        compiler_params=_cparams(("arbitrary", "arbitrary", "arbitrary")),
    )(r, ld, k, v, a, b, s0_all, dy)


def _exchange(x, name, scatter):
    blk = x.shape[1:] if scatter else x.shape

    def body(x_ref, o_ref, send_sems, recv_sems, local_sem):
        pos = (lax.axis_index("x"), lax.axis_index("y"), lax.axis_index("c"))
        me = 4 * pos[0] + 2 * pos[1] + pos[2]

        def peer_of(m):
            p = tuple(1 - pos[i] if (m >> (2 - i)) & 1 else pos[i] for i in range(3))
            return p, 4 * p[0] + 2 * p[1] + p[2]

        def copy(m):
            p, pidx = peer_of(m)
            return pltpu.make_async_remote_copy(
                src_ref=x_ref.at[pidx] if scatter else x_ref, dst_ref=o_ref.at[me],
                send_sem=send_sems.at[m - 1], recv_sem=recv_sems.at[m - 1],
                device_id=p, device_id_type=pl.DeviceIdType.MESH)

        def arrival(m):
            p, pidx = peer_of(m)
            return pltpu.make_async_remote_copy(
                src_ref=x_ref.at[pidx] if scatter else x_ref, dst_ref=o_ref.at[pidx],
                send_sem=send_sems.at[m - 1], recv_sem=recv_sems.at[m - 1],
                device_id=p, device_id_type=pl.DeviceIdType.MESH)

        mine = pltpu.make_async_copy(x_ref.at[me] if scatter else x_ref, o_ref.at[me], local_sem)
        mine.start()
        sends = [copy(m) for m in range(1, N_DEV)]
        for cp in sends:
            cp.start()
        for m in range(1, N_DEV):
            arrival(m).wait_recv()
        for cp in sends:
            cp.wait_send()
        mine.wait()

    return pl.pallas_call(
        body, name=name,
        out_shape=jax.ShapeDtypeStruct((N_DEV,) + tuple(blk), x.dtype),
        in_specs=[pl.BlockSpec(memory_space=pl.ANY)],
        out_specs=pl.BlockSpec(memory_space=pl.ANY),
        scratch_shapes=[pltpu.SemaphoreType.DMA((N_DEV - 1,)), pltpu.SemaphoreType.DMA((N_DEV - 1,)),
                        pltpu.SemaphoreType.DMA],
    )(x)


def _rms(x):
    inv = lax.rsqrt(jnp.mean(x * x, axis=-1, keepdims=True) + RMS_EPS)
    return x * inv, inv


def _rms_bwd(xn, inv, dxn):
    return inv * (dxn - xn * jnp.mean(dxn * xn, axis=-1, keepdims=True))


def _colsum(x):
    return jnp.sum(x, axis=0, keepdims=True)


def _sigmoid(x):
    return 0.5 * (jnp.tanh(0.5 * x) + 1.0)


def _split_bf16(x):
    hi = x.astype(BF16)
    return hi, (x - hi.astype(F32)).astype(BF16)


def _dot_split(x, e):
    hi, lo = _split_bf16(x)
    return jnp.dot(hi, e, preferred_element_type=F32) + jnp.dot(lo, e, preferred_element_type=F32)


@jax.custom_vjp
def _headsum(x, e, et):
    return _dot_split(_dot_split(x, e), et)


_headsum.defvjp(lambda x, e, et: (_headsum(x, e, et), (e, et)),
                lambda res, g: (_headsum(g, *res), jnp.zeros_like(res[0]), jnp.zeros_like(res[1])))


def _make_headsum(e, et):
    return lambda x: _headsum(x, e, et)


def _head_indicators(d):
    e = (jnp.arange(d)[:, None] // HEAD == jnp.arange(128)[None, :]).astype(BF16)
    return e, e.T


def _rwkv_elem(r, k, lw, la, w0, a0, k_k, k_a, headsum):
    z = w0 + lw
    w_log = -(jnp.maximum(-z, 0.0) + jnp.log(1.0 + jnp.exp(-jnp.abs(z)))) - 0.5
    ld = -jnp.exp(w_log)
    a = _sigmoid(a0 + la)
    kkp = k * k_k
    kk = kkp / jnp.maximum(jnp.sqrt(headsum(kkp * kkp)), L2_EPS)
    k2 = k * (1.0 + (a - 1.0) * k_a)
    del r
    return ld, k2, -kk, kk * a


def _rwkv_post(y, r, k2, v, g, ln_g, ln_b, r_k, headsum):
    m = headsum(y) * (1.0 / HEAD)
    yc = y - m
    var = headsum(yc * yc) * (1.0 / HEAD)
    yn = yc * lax.rsqrt(var + GN_EPS)
    bonus = headsum(r * k2 * r_k) * v
    return (yn * ln_g + ln_b + bonus) * g


def _shift_down(h, first_row):
    rolled = pltpu.roll(h, 1, 0)
    row = lax.broadcasted_iota(jnp.int32, h.shape, 0)
    return jnp.where(row == 0, first_row, rolled)


def _shift_up(h, last_row):
    n = h.shape[0]
    rolled = pltpu.roll(h, n - 1, 0)
    row = lax.broadcasted_iota(jnp.int32, h.shape, 0)
    return jnp.where(row == n - 1, last_row, rolled)


def _gelu(p):
    return 0.5 * p * (1.0 + lax.erf(p * 0.7071067811865476))


def _gelu_grad(p):
    return 0.5 * (1.0 + lax.erf(p * 0.7071067811865476)) + p * jnp.exp(-0.5 * p * p) * 0.3989422804014327


def _tok(tm, d):
    return pl.BlockSpec((1, tm, d), lambda e, t, *_: (e, t, 0))


def _per_example(rows, d):
    return pl.BlockSpec((1, rows, d), lambda e, t, *_: (e, 0, 0))


def _whole(shape):
    nd = len(shape)
    return pl.BlockSpec(tuple(shape), lambda *_: (0,) * nd)


def _heads(nh, tm):
    return pl.BlockSpec((1, nh, tm, HEAD), lambda e, t, *_: (e, 0, t, 0))


def _sds(shape, dtype=F32):
    return jax.ShapeDtypeStruct(tuple(shape), dtype)


def _add_rows(ref, first, rows):
    @pl.when(first)
    def _():
        ref[0] = jnp.zeros(ref.shape[1:], ref.dtype)

    for i, r in enumerate(rows):
        ref[0, i:i + 1] += r


def _first(e, t):
    return jnp.logical_and(e == 0, t == 0)


def _ada_fwd(c_all, ada_w, ada_b_cols):
    nl, d, cols = ada_w.shape
    nb = c_all.shape[0]

    def body(c_ref, w_ref, b_ref, o_ref):
        c = c_ref[...]
        cond = c * _sigmoid(c)
        for i in range(nl):
            o_ref[i] = _dot(cond, w_ref[i]) + b_ref[i]

    return pl.pallas_call(
        body, name="ada_fwd", out_shape=_sds((nl, nb, cols)),
        compiler_params=_cparams(),
    )(c_all, ada_w, ada_b_cols)


def _ada_bwd(c_all, dmod_cols, dmod_full):
    nl, nb, cols = dmod_cols.shape
    d = c_all.shape[1]

    def body(c_ref, g_ref, f_ref, o_ref, b_ref):
        c = c_ref[...]
        cond = c * _sigmoid(c)
        for i in range(nl):
            o_ref[i] = _dot_tn(cond, g_ref[i])
            b_ref[i:i + 1] = jnp.sum(f_ref[i], axis=0, keepdims=True)

    return pl.pallas_call(
        body, name="ada_bwd", out_shape=(_sds((nl, d, cols)), _sds((nl, dmod_full.shape[2]))),
        compiler_params=_cparams(),
    )(c_all, dmod_cols, dmod_full)


def _matmul_tn(a, b, name):
    m, ka = a.shape
    n = b.shape[1]
    tm = min(m, 512)
    tk = min(ka, 1024)
    tn = min(n, 512)
    steps = m // tm

    def body(a_ref, b_ref, o_ref):
        @pl.when(pl.program_id(2) == 0)
        def _():
            o_ref[...] = jnp.zeros_like(o_ref)

        o_ref[...] += _dot_tn(a_ref[...], b_ref[...])

    return pl.pallas_call(
        body, name=name, grid=(ka // tk, n // tn, steps),
        in_specs=[pl.BlockSpec((tm, tk), lambda i, j, s: (s, i)), pl.BlockSpec((tm, tn), lambda i, j, s: (s, j))],
        out_specs=pl.BlockSpec((tk, tn), lambda i, j, s: (i, j)),
        out_shape=_sds((ka, n)),
        compiler_params=_cparams(("parallel", "parallel", "arbitrary")),
    )(a, b)


MLP_TM = 512
MLP_FJ = 1024


def _mlp_fwd(x, mod, w1, w2):
    bl, t, d = x.shape
    f = w1.shape[1]
    tm, fj = min(t, MLP_TM), min(f, MLP_FJ)
    nj = f // fj

    def body(x_ref, mod_ref, w1_ref, w2_ref, xo_ref, ff_ref, h_scr, acc):
        j = pl.program_id(2)

        @pl.when(j == 0)
        def _():
            xn, _ = _rms(x_ref[0])
            h_scr[...] = (xn * (1.0 + mod_ref[0, 1:2]) + mod_ref[0, 0:1]).astype(BF16)
            acc[...] = jnp.zeros_like(acc)

        p = jnp.dot(h_scr[...], w1_ref[...], preferred_element_type=F32)
        q = jnp.square(jnp.maximum(p, 0.0))
        acc[...] += _dot(q, w2_ref[...])

        @pl.when(j == nj - 1)
        def _():
            ff_ref[0] = acc[...]
            xo_ref[0] = x_ref[0] + mod_ref[0, 2:3] * acc[...]

    return pl.pallas_call(
        body, name="mlp_fwd", grid=(bl, t // tm, nj),
        in_specs=[_tok(tm, d), _per_example(8, d),
                  pl.BlockSpec((d, fj), lambda e, i, j: (0, j)), pl.BlockSpec((fj, d), lambda e, i, j: (j, 0))],
        out_specs=(_tok(tm, d), _tok(tm, d)),
        out_shape=(_sds(x.shape), _sds(x.shape)),
        scratch_shapes=[pltpu.VMEM((tm, d), BF16), pltpu.VMEM((tm, d), F32)],
        compiler_params=_cparams(("arbitrary", "arbitrary", "arbitrary")),
    )(x, mod, w1, w2)


def _mlp_bwd(x, dxo, ff, mod, w1, w2):
    bl, t, d = x.shape
    f = w1.shape[1]
    tm, fj = min(t, MLP_TM), min(f, MLP_FJ)
    nj = f // fj

    def body(x_ref, dxo_ref, ff_ref, mod_ref, w1_ref, w2_ref,
             dx_ref, dmod_ref, h_ref, dff_ref, q_ref, dp_ref, acc):
        ti, j = pl.program_id(1), pl.program_id(2)

        @pl.when(j == 0)
        def _():
            xn, _ = _rms(x_ref[0])
            h_ref[0] = (xn * (1.0 + mod_ref[0, 1:2]) + mod_ref[0, 0:1]).astype(BF16)
            dff_ref[0] = (mod_ref[0, 2:3] * dxo_ref[0]).astype(BF16)
            acc[...] = jnp.zeros_like(acc)

        p = jnp.dot(h_ref[0], w1_ref[...], preferred_element_type=F32)
        rl = jnp.maximum(p, 0.0)
        q_ref[0] = jnp.square(rl).astype(BF16)
        dp = (_dot_nt(dff_ref[0], w2_ref[...]) * (2.0 * rl)).astype(BF16)
        dp_ref[0] = dp
        acc[...] += _dot_nt(dp, w1_ref[...])

        @pl.when(j == nj - 1)
        def _():
            xn, inv = _rms(x_ref[0])
            dh = acc[...]
            dx_ref[0] = dxo_ref[0] + _rms_bwd(xn, inv, dh * (1.0 + mod_ref[0, 1:2]))
            _add_rows(dmod_ref, ti == 0, [_colsum(dh), _colsum(dh * xn), _colsum(dxo_ref[0] * ff_ref[0])])

    big = lambda: pl.BlockSpec((1, tm, fj), lambda e, i, j: (e, i, j))
    return pl.pallas_call(
        body, name="mlp_bwd", grid=(bl, t // tm, nj),
        in_specs=[_tok(tm, d), _tok(tm, d), _tok(tm, d), _per_example(8, d),
                  pl.BlockSpec((d, fj), lambda e, i, j: (0, j)), pl.BlockSpec((fj, d), lambda e, i, j: (j, 0))],
        out_specs=(_tok(tm, d), _per_example(8, d), _tok(tm, d), _tok(tm, d), big(), big()),
        out_shape=(_sds(x.shape), _sds((bl, 8, d)), _sds(x.shape, BF16), _sds(x.shape, BF16),
                   _sds((bl, t, f), BF16), _sds((bl, t, f), BF16)),
        scratch_shapes=[pltpu.VMEM((tm, d), F32)],
        compiler_params=_cparams(("arbitrary", "arbitrary", "arbitrary")),
    )(x, dxo, ff, mod, w1, w2)


SGU_TM = 256


def _sgu_core(x, mod_ref, win_ref, lng, lnb, ws_ref, bias_ref):
    tm, d = x.shape
    xn, inv = _rms(x)
    h = (xn * (1.0 + mod_ref[0, 1:2]) + mod_ref[0, 0:1]).astype(BF16)
    pre = jnp.dot(h, win_ref[...], preferred_element_type=F32)
    uv = _gelu(pre)
    u, v = uv[:, :d], uv[:, d:]
    mu = jnp.mean(v, axis=-1, keepdims=True)
    vc = v - mu
    rstd = lax.rsqrt(jnp.mean(vc * vc, axis=-1, keepdims=True) + LN_EPS)
    vhat = vc * rstd
    vln = vhat * lng + lnb
    gd = d // SGU_GROUPS
    rows = []
    for c in range(tm // SGU_CHUNK):
        cols = []
        for g in range(SGU_GROUPS):
            cols.append(_dot(ws_ref[g], vln[c * SGU_CHUNK:(c + 1) * SGU_CHUNK, g * gd:(g + 1) * gd]))
        rows.append(jnp.concatenate(cols, axis=1) + bias_ref[...])
    sv = jnp.concatenate(rows, axis=0)
    return xn, inv, h, pre, u, vhat, rstd, vln, sv


def _sgu_masked(ws_ref, wm_scr):
    row = lax.broadcasted_iota(jnp.int32, (SGU_CHUNK, SGU_CHUNK), 0)
    col = lax.broadcasted_iota(jnp.int32, (SGU_CHUNK, SGU_CHUNK), 1)
    for g in range(SGU_GROUPS):
        wm_scr[g] = jnp.where(row >= col, ws_ref[g], 0.0).astype(BF16)


def _sgu_fwd(x, mod, w_in, ln_g, ln_b, w_s, bias_full, w_out):
    bl, t, d = x.shape
    tm = min(t, SGU_TM)

    def body(x_ref, mod_ref, win_ref, lng_ref, lnb_ref, ws_ref, bias_ref, wout_ref, xo_ref, mix_ref, wm_scr):
        _sgu_masked(ws_ref, wm_scr)
        xt = x_ref[0]
        *_, u, _, _, _, sv = _sgu_core(xt, mod_ref, win_ref, lng_ref[...], lnb_ref[...], wm_scr, bias_ref)
        mix = _dot(u * sv, wout_ref[...])
        mix_ref[0] = mix
        xo_ref[0] = xt + mod_ref[0, 2:3] * mix

    return pl.pallas_call(
        body, name="sgu_fwd", grid=(bl, t // tm),
        in_specs=[_tok(tm, d), _per_example(8, d), _whole(w_in.shape), _whole(ln_g.shape), _whole(ln_b.shape),
                  _whole(w_s.shape), _whole(bias_full.shape), _whole(w_out.shape)],
        out_specs=(_tok(tm, d), _tok(tm, d)),
        out_shape=(_sds(x.shape), _sds(x.shape)),
        scratch_shapes=[pltpu.VMEM(w_s.shape, BF16)],
        compiler_params=_cparams(("arbitrary", "arbitrary")),
    )(x, mod, w_in, ln_g, ln_b, w_s, bias_full, w_out)


def _sgu_bwd(x, dxo, mix, mod, w_in, ln_g, ln_b, w_s, bias_full, w_out, group_ind):
    bl, t, d = x.shape
    tm = min(t, SGU_TM)
    gd = d // SGU_GROUPS

    def body(x_ref, dxo_ref, mix_ref, mod_ref, win_ref, lng_ref, lnb_ref, ws_ref, bias_ref, wout_ref, ind_ref,
             dx_ref, dmod_ref, h_ref, dpre_ref, z_ref, dmix_ref, small_ref, dws_ref, dbs_ref, wm_scr, dbias_scr):
        e, ti = pl.program_id(0), pl.program_id(1)
        _sgu_masked(ws_ref, wm_scr)
        xt, dxo = x_ref[0], dxo_ref[0]
        lng = lng_ref[...]
        xn, inv, h, pre, u, vhat, rstd, vln, sv = _sgu_core(xt, mod_ref, win_ref, lng, lnb_ref[...], wm_scr, bias_ref)
        h_ref[0] = h
        z_ref[0] = (u * sv).astype(BF16)
        dmix = mod_ref[0, 2:3] * dxo
        dmix_ref[0] = dmix.astype(BF16)
        dz = _dot_nt(dmix, wout_ref[...])
        du, dsv = dz * sv, dz * u

        @pl.when(_first(e, ti))
        def _():
            dws_ref[...] = jnp.zeros_like(dws_ref)
            dbias_scr[...] = jnp.zeros_like(dbias_scr)
            small_ref[...] = jnp.zeros_like(small_ref)

        row = lax.broadcasted_iota(jnp.int32, (SGU_CHUNK, SGU_CHUNK), 0)
        col = lax.broadcasted_iota(jnp.int32, (SGU_CHUNK, SGU_CHUNK), 1)
        rows = []
        for c in range(tm // SGU_CHUNK):
            rs = slice(c * SGU_CHUNK, (c + 1) * SGU_CHUNK)
            dbias_scr[...] += dsv[rs]
            cols = []
            for g in range(SGU_GROUPS):
                cs = slice(g * gd, (g + 1) * gd)
                cols.append(_dot_tn(wm_scr[g], dsv[rs, cs]))
                dws_ref[g] += jnp.where(row >= col, _dot_nt(dsv[rs, cs], vln[rs, cs]), 0.0)
            rows.append(jnp.concatenate(cols, axis=1))
        dvln = jnp.concatenate(rows, axis=0)
        small_ref[0:1] += _colsum(dvln * vhat)
        small_ref[1:2] += _colsum(dvln)
        dvhat = dvln * lng
        dv = rstd * (dvhat - jnp.mean(dvhat, axis=-1, keepdims=True)
                     - vhat * jnp.mean(dvhat * vhat, axis=-1, keepdims=True))
        dpre = (jnp.concatenate([du, dv], axis=1) * _gelu_grad(pre)).astype(BF16)
        dpre_ref[0] = dpre
        dh = _dot_nt(dpre, win_ref[...])
        dx_ref[0] = dxo + _rms_bwd(xn, inv, dh * (1.0 + mod_ref[0, 1:2]))
        _add_rows(dmod_ref, ti == 0, [_colsum(dh), _colsum(dh * xn), _colsum(dxo * mix_ref[0])])

        @pl.when(jnp.logical_and(e == bl - 1, ti == t // tm - 1))
        def _():
            hi, lo = _split_bf16(dbias_scr[...])
            ind = ind_ref[...]
            dbs_ref[...] = (lax.dot_general(ind, hi, (((1,), (1,)), ((), ())), preferred_element_type=F32)
                            + lax.dot_general(ind, lo, (((1,), (1,)), ((), ())), preferred_element_type=F32))

    return pl.pallas_call(
        body, name="sgu_bwd", grid=(bl, t // tm),
        in_specs=[_tok(tm, d), _tok(tm, d), _tok(tm, d), _per_example(8, d), _whole(w_in.shape), _whole(ln_g.shape),
                  _whole(ln_b.shape), _whole(w_s.shape), _whole(bias_full.shape), _whole(w_out.shape),
                  _whole(group_ind.shape)],
        out_specs=(_tok(tm, d), _per_example(8, d), _tok(tm, d), _tok(tm, 2 * d), _tok(tm, d), _tok(tm, d),
                   _whole((8, d)), _whole(w_s.shape), _whole((SGU_GROUPS, SGU_CHUNK))),
        out_shape=(_sds(x.shape), _sds((bl, 8, d)), _sds(x.shape, BF16), _sds((bl, t, 2 * d), BF16),
                   _sds(x.shape, BF16), _sds(x.shape, BF16), _sds((8, d)), _sds(w_s.shape),
                   _sds((SGU_GROUPS, SGU_CHUNK))),
        scratch_shapes=[pltpu.VMEM(w_s.shape, BF16), pltpu.VMEM((SGU_CHUNK, d), F32)],
        compiler_params=_cparams(("arbitrary", "arbitrary")),
    )(x, dxo, mix, mod, w_in, ln_g, ln_b, w_s, bias_full, w_out, group_ind)


RWKV_TM = 256
N_VEC = 16


def _rwkv_pre_core(x_ref, halo_ref, mod_ref, vec_ref, ti):
    xn, inv = _rms(x_ref[0])
    scale1, shift = 1.0 + mod_ref[0, 1:2], mod_ref[0, 0:1]
    h = xn * scale1 + shift
    hn, _ = _rms(halo_ref[0])
    hh = hn * scale1 + shift
    first = jnp.where(ti == 0, 0.0, hh[7:8])
    xx = _shift_down(h, first) - h
    xs = [h + xx * vec_ref[i:i + 1] for i in range(6)]
    return xn, inv, xx, xs


def _rwkv_proj(xs, wrkv_ref, w1_ref, a1_ref, g1_ref, w2_ref, a2_ref, g2_ref):
    d = xs[0].shape[1]
    xr, xw, xk, xv, xa, xg = [z.astype(BF16) for z in xs]
    r = jnp.dot(xr, wrkv_ref[:, 0:d], preferred_element_type=F32)
    k = jnp.dot(xk, wrkv_ref[:, d:2 * d], preferred_element_type=F32)
    v = jnp.dot(xv, wrkv_ref[:, 2 * d:3 * d], preferred_element_type=F32)
    tw2 = jnp.tanh(jnp.dot(xw, w1_ref[...], preferred_element_type=F32))
    ta = jnp.dot(xa, a1_ref[...], preferred_element_type=F32)
    sg = _sigmoid(jnp.dot(xg, g1_ref[...], preferred_element_type=F32))
    lw, la, g = _dot(tw2, w2_ref[...]), _dot(ta, a2_ref[...]), _dot(sg, g2_ref[...])
    return (xr, xw, xk, xv, xa, xg), r, k, v, tw2, ta, sg, lw, la, g


def _to_heads(ref, val, nh):
    for hd in range(nh):
        ref[0, hd] = val[:, hd * HEAD:(hd + 1) * HEAD]


def _from_heads(ref, scr, nh):
    for hd in range(nh):
        scr[:, hd * HEAD:(hd + 1) * HEAD] = ref[0, hd]
    return scr[...]


def _rwkv_weight_specs(ws):
    return [_whole(w.shape) for w in ws]


def _rwkv_pre_fwd(x, mod, vec, e_ind, et_ind, weights):
    bl, t, d = x.shape
    tm = min(t, RWKV_TM)
    nh = d // HEAD
    hb = tm // 8

    def body(x_ref, halo_ref, mod_ref, vec_ref, e_ref, et_ref, wrkv, w1, a1, g1, w2, a2, g2,
             r_ref, ld_ref, k2_ref, v_ref, as_ref, bs_ref, g_ref):
        ti = pl.program_id(1)
        _, _, _, xs = _rwkv_pre_core(x_ref, halo_ref, mod_ref, vec_ref, ti)
        _, r, k, v, _, _, _, lw, la, g = _rwkv_proj(xs, wrkv, w1, a1, g1, w2, a2, g2)
        headsum = _make_headsum(e_ref[...], et_ref[...])
        ld, k2, a_s, b_s = _rwkv_elem(r, k, lw, la, vec_ref[6:7], vec_ref[7:8], vec_ref[8:9], vec_ref[9:10], headsum)
        g_ref[0] = g
        for ref, val in ((r_ref, r), (ld_ref, ld), (k2_ref, k2), (v_ref, v), (as_ref, a_s), (bs_ref, b_s)):
            _to_heads(ref, val, nh)

    halo = pl.BlockSpec((1, 8, d), lambda e, i: (e, jnp.maximum(i * hb - 1, 0), 0))
    hs = _sds((bl, nh, t, HEAD))
    return pl.pallas_call(
        body, name="rwkv_pre_fwd", grid=(bl, t // tm),
        in_specs=[_tok(tm, d), halo, _per_example(8, d), _whole(vec.shape), _whole(e_ind.shape), _whole(et_ind.shape)]
        + _rwkv_weight_specs(weights),
        out_specs=(_heads(nh, tm),) * 6 + (_tok(tm, d),),
        out_shape=(hs,) * 6 + (_sds(x.shape),),
        compiler_params=_cparams(("arbitrary", "arbitrary")),
    )(x, x, mod, vec, e_ind, et_ind, *weights)


def _rwkv_post_fwd(x, y, r, k2, v, g, mod, vec, e_ind, et_ind, w_out):
    bl, t, d = x.shape
    tm = min(t, RWKV_TM)
    nh = d // HEAD

    def body(x_ref, y_ref, r_ref, k2_ref, v_ref, g_ref, mod_ref, vec_ref, e_ref, et_ref, wout_ref,
             xo_ref, mix_ref, s0, s1, s2, s3):
        headsum = _make_headsum(e_ref[...], et_ref[...])
        yv, rv, kv, vv = (_from_heads(ref, scr, nh) for ref, scr in
                          ((y_ref, s0), (r_ref, s1), (k2_ref, s2), (v_ref, s3)))
        o = _rwkv_post(yv, rv, kv, vv, g_ref[0], vec_ref[10:11], vec_ref[11:12], vec_ref[12:13], headsum)
        mix = _dot(o, wout_ref[...])
        mix_ref[0] = mix
        xo_ref[0] = x_ref[0] + mod_ref[0, 2:3] * mix

    return pl.pallas_call(
        body, name="rwkv_post_fwd", grid=(bl, t // tm),
        in_specs=[_tok(tm, d)] + [_heads(nh, tm)] * 4 + [_tok(tm, d), _per_example(8, d), _whole(vec.shape),
                                                         _whole(e_ind.shape), _whole(et_ind.shape), _whole(w_out.shape)],
        out_specs=(_tok(tm, d), _tok(tm, d)),
        out_shape=(_sds(x.shape), _sds(x.shape)),
        scratch_shapes=[pltpu.VMEM((tm, d), F32)] * 4,
        compiler_params=_cparams(("arbitrary", "arbitrary")),
    )(x, y, r, k2, v, g, mod, vec, e_ind, et_ind, w_out)


def _rwkv_post_bwd(dxo, mix, y, r, k2, v, g, mod, vec, e_ind, et_ind, w_out):
    bl, t, d = dxo.shape
    tm = min(t, RWKV_TM)
    nh = d // HEAD

    def body(dxo_ref, mix_ref, y_ref, r_ref, k2_ref, v_ref, g_ref, mod_ref, vec_ref, e_ref, et_ref, wout_ref,
             dy_ref, dr_ref, dk2_ref, dv_ref, dg_ref, o_ref, dmix_ref, dgate_ref, small_ref, s0, s1, s2, s3):
        e, ti = pl.program_id(0), pl.program_id(1)
        headsum = _make_headsum(e_ref[...], et_ref[...])
        yv, rv, kv, vv = (_from_heads(ref, scr, nh) for ref, scr in
                          ((y_ref, s0), (r_ref, s1), (k2_ref, s2), (v_ref, s3)))
        dxo = dxo_ref[0]
        dmix = mod_ref[0, 2:3] * dxo
        dmix_ref[0] = dmix.astype(BF16)
        do = _dot_nt(dmix, wout_ref[...])
        post = functools.partial(_rwkv_post, headsum=headsum)
        o, vjp = jax.vjp(post, yv, rv, kv, vv, g_ref[0], vec_ref[10:11], vec_ref[11:12], vec_ref[12:13])
        o_ref[0] = o.astype(BF16)
        dy, dr, dk2, dv, dg, dlng, dlnb, drk = vjp(do)
        _to_heads(dy_ref, dy, nh)
        dr_ref[0], dk2_ref[0], dv_ref[0], dg_ref[0] = dr, dk2, dv, dg
        zero = jnp.zeros((1, d), F32)
        _add_rows(dgate_ref, ti == 0, [zero, zero, _colsum(dxo * mix_ref[0])])

        @pl.when(_first(e, ti))
        def _():
            small_ref[...] = jnp.zeros_like(small_ref)

        small_ref[0:1] += dlng
        small_ref[1:2] += dlnb
        small_ref[2:3] += drk

    return pl.pallas_call(
        body, name="rwkv_post_bwd", grid=(bl, t // tm),
        in_specs=[_tok(tm, d), _tok(tm, d)] + [_heads(nh, tm)] * 4
        + [_tok(tm, d), _per_example(8, d), _whole(vec.shape), _whole(e_ind.shape), _whole(et_ind.shape),
           _whole(w_out.shape)],
        out_specs=(_heads(nh, tm),) + (_tok(tm, d),) * 6 + (_per_example(8, d), _whole((8, d))),
        out_shape=(_sds((bl, nh, t, HEAD)),) + (_sds(dxo.shape),) * 4 + (_sds(dxo.shape, BF16),) * 2
        + (_sds((bl, 8, d)), _sds((8, d))),
        scratch_shapes=[pltpu.VMEM((tm, d), F32)] * 4,
        compiler_params=_cparams(("arbitrary", "arbitrary")),
    )(dxo, mix, y, r, k2, v, g, mod, vec, e_ind, et_ind, w_out)


RWKV_BWD_TM = 128


def _rwkv_pre_bwd(x, mod, vec, e_ind, et_ind, weights, dr_p, dk2_p, dv_p, dg, dr_s, dld, dk2_s, dv_s, das, dbs):
    bl, t, d = x.shape
    tm = min(t, RWKV_BWD_TM)
    nh = d // HEAD
    hb = tm // 8
    lp, gp = LORA_PAD, GATE_PAD

    def body(x_ref, halo_ref, mod_ref, vec_ref, e_ref, et_ref, wrkv, w1, a1, g1, w2, a2, g2,
             drp_ref, dk2p_ref, dvp_ref, dg_ref, drs_ref, dld_ref, dk2s_ref, dvs_ref, das_ref, dbs_ref,
             dh_ref, dhp_ref, xr_ref, xw_ref, xk_ref, xv_ref, xa_ref, xg_ref, dr_ref, dk_ref, dv_ref,
             dtw_ref, dta_ref, dtg_ref, tw2_ref, ta_ref, sg_ref, dlw_ref, dla_ref, dgb_ref, small_ref,
             s0, s1, s2, s3, s4, s5):
        e, ti = pl.program_id(0), pl.program_id(1)
        _, _, xx, xs = _rwkv_pre_core(x_ref, halo_ref, mod_ref, vec_ref, ti)
        xb, r, k, v, tw2, ta, sg, lw, la, _ = _rwkv_proj(xs, wrkv, w1, a1, g1, w2, a2, g2)
        for ref, val in zip((xr_ref, xw_ref, xk_ref, xv_ref, xa_ref, xg_ref), xb):
            ref[0] = val
        headsum = _make_headsum(e_ref[...], et_ref[...])
        drs, dld, dk2s, dvs, das, dbs_ = (_from_heads(ref, scr, nh) for ref, scr in
                                          ((drs_ref, s0), (dld_ref, s1), (dk2s_ref, s2), (dvs_ref, s3),
                                           (das_ref, s4), (dbs_ref, s5)))
        elem = functools.partial(_rwkv_elem, r, headsum=headsum)
        _, vjp = jax.vjp(elem, k, lw, la, vec_ref[6:7], vec_ref[7:8], vec_ref[8:9], vec_ref[9:10])
        dk, dlw, dla, dw0, da0, dkk, dka = vjp((dld, dk2p_ref[0] + dk2s, das, dbs_))
        dr = drp_ref[0] + drs
        dv = dvp_ref[0] + dvs
        dgv = dg_ref[0]
        dtg = _dot_nt(dgv, g2[...]) * sg * (1.0 - sg)
        dtw = _dot_nt(dlw, w2[...]) * (1.0 - tw2 * tw2)
        dta = _dot_nt(dla, a2[...])
        dr_ref[0], dk_ref[0], dv_ref[0] = dr.astype(BF16), dk.astype(BF16), dv.astype(BF16)
        dtw_ref[0], dta_ref[0], dtg_ref[0] = dtw.astype(BF16), dta.astype(BF16), dtg.astype(BF16)
        tw2_ref[0], ta_ref[0], sg_ref[0] = tw2.astype(BF16), ta.astype(BF16), sg.astype(BF16)
        dlw_ref[0], dla_ref[0], dgb_ref[0] = dlw.astype(BF16), dla.astype(BF16), dgv.astype(BF16)
        dxs = (_dot_nt(dr, wrkv[:, 0:d]), _dot_nt(dtw, w1[...]), _dot_nt(dk, wrkv[:, d:2 * d]),
               _dot_nt(dv, wrkv[:, 2 * d:3 * d]), _dot_nt(dta, a1[...]), _dot_nt(dtg, g1[...]))

        @pl.when(_first(e, ti))
        def _():
            small_ref[...] = jnp.zeros_like(small_ref)

        dh = jnp.zeros((tm, d), F32)
        dhp = jnp.zeros((tm, d), F32)
        for i, dxi in enumerate(dxs):
            mu = vec_ref[i:i + 1]
            dh += dxi * (1.0 - mu)
            dhp += dxi * mu
            small_ref[i:i + 1] += _colsum(dxi * xx)
        dh_ref[0], dhp_ref[0] = dh, dhp
        small_ref[6:7] += dw0
        small_ref[7:8] += da0
        small_ref[8:9] += dkk
        small_ref[9:10] += dka

    halo = pl.BlockSpec((1, 8, d), lambda e, i: (e, jnp.maximum(i * hb - 1, 0), 0))
    tokd, tokl, tokg = _tok(tm, d), _tok(tm, lp), _tok(tm, gp)
    bf = lambda w: _sds((bl, t, w), BF16)
    return pl.pallas_call(
        body, name="rwkv_pre_bwd", grid=(bl, t // tm),
        in_specs=[tokd, halo, _per_example(8, d), _whole(vec.shape), _whole(e_ind.shape), _whole(et_ind.shape)]
        + _rwkv_weight_specs(weights) + [tokd] * 4 + [_heads(nh, tm)] * 6,
        out_specs=(tokd, tokd) + (tokd,) * 6 + (tokd,) * 3 + (tokl, tokl, tokg, tokl, tokl, tokg)
        + (tokd, tokd, tokd, _whole((N_VEC, d))),
        out_shape=(_sds(x.shape), _sds(x.shape)) + (bf(d),) * 9 + (bf(lp), bf(lp), bf(gp), bf(lp), bf(lp), bf(gp))
        + (bf(d), bf(d), bf(d), _sds((N_VEC, d))),
        scratch_shapes=[pltpu.VMEM((tm, d), F32)] * 6,
        compiler_params=_cparams(("arbitrary", "arbitrary")),
    )(x, x, mod, vec, e_ind, et_ind, *weights, dr_p, dk2_p, dv_p, dg, dr_s, dld, dk2_s, dv_s, das, dbs)


def _norm_bwd(x, dxo, dh, dhprev, mod, dgate):
    bl, t, d = x.shape
    tm = min(t, RWKV_TM)
    hb = tm // 8
    last_blk = t // 8 - 1

    def body(x_ref, dxo_ref, dh_ref, dhp_ref, nxt_ref, mod_ref, dgate_ref, dx_ref, dmod_ref):
        ti = pl.program_id(1)
        xn, inv = _rms(x_ref[0])
        last = jnp.where(ti == t // tm - 1, 0.0, nxt_ref[0, 0:1])
        dh = dh_ref[0] + _shift_up(dhp_ref[0], last)
        dx_ref[0] = dxo_ref[0] + _rms_bwd(xn, inv, dh * (1.0 + mod_ref[0, 1:2]))

        @pl.when(ti == 0)
        def _():
            dmod_ref[0] = dgate_ref[0]

        dmod_ref[0, 0:1] += _colsum(dh)
        dmod_ref[0, 1:2] += _colsum(dh * xn)

    nxt = pl.BlockSpec((1, 8, d), lambda e, i: (e, jnp.minimum((i + 1) * hb, last_blk), 0))
    return pl.pallas_call(
        body, name="norm_bwd", grid=(bl, t // tm),
        in_specs=[_tok(tm, d)] * 4 + [nxt, _per_example(8, d), _per_example(8, d)],
        out_specs=(_tok(tm, d), _per_example(8, d)),
        out_shape=(_sds(x.shape), _sds((bl, 8, d))),
        compiler_params=_cparams(("arbitrary", "arbitrary")),
    )(x, dxo, dh, dhprev, dhprev, mod, dgate)


def _final(x, target, final_g):
    bl, t, d = x.shape
    tm = min(t, 512)

    def body(x_ref, tgt_ref, g_ref, dx_ref, loss_ref, dg_ref):
        e, ti = pl.program_id(0), pl.program_id(1)

        @pl.when(_first(e, ti))
        def _():
            loss_ref[...] = jnp.zeros_like(loss_ref)
            dg_ref[...] = jnp.zeros_like(dg_ref)

        xn, inv = _rms(x_ref[0])
        err = xn * g_ref[...] - tgt_ref[0]
        loss_ref[...] += (0.5 / d) * jnp.sum(err * err)
        dy = err * (1.0 / d)
        dg_ref[0:1] += _colsum(dy * xn)
        dx_ref[0] = _rms_bwd(xn, inv, dy * g_ref[...])

    return pl.pallas_call(
        body, name="final_loss", grid=(bl, t // tm),
        in_specs=[_tok(tm, d), _tok(tm, d), _whole(final_g.shape)],
        out_specs=(_tok(tm, d), _whole((8, 128)), _whole((8, d))),
        out_shape=(_sds(x.shape), _sds((8, 128)), _sds((8, d))),
        compiler_params=_cparams(("arbitrary", "arbitrary")),
    )(x, target, final_g)


def _adamw_math(w, g, m, v):
    m = ADAM_B1 * m + (1.0 - ADAM_B1) * g
    v = ADAM_B2 * v + (1.0 - ADAM_B2) * jnp.square(g)
    m_hat = m / (1.0 - ADAM_B1 ** ADAM_STEP)
    v_hat = v / (1.0 - ADAM_B2 ** ADAM_STEP)
    return -ADAM_LR * (m_hat / (jnp.sqrt(v_hat) + ADAM_EPS) + ADAM_WD * w), m, v


def _sum_parts(ref, n):
    g = ref[0]
    for s in range(1, n):
        g = g + ref[s]
    return g


def _adamw_rows(w, m, v, parts, row0, name):
    rows, c = w.shape
    n = parts.shape[0]
    tr = 128
    blk0 = row0 // tr

    def body(w_ref, m_ref, v_ref, p_ref, g_ref, d_ref, mo_ref, vo_ref):
        g = _sum_parts(p_ref, n)
        g_ref[...] = g
        d_ref[...], mo_ref[...], vo_ref[...] = _adamw_math(w_ref[...], g, m_ref[...], v_ref[...])

    row = pl.BlockSpec((tr, c), lambda i: (i, 0))
    return pl.pallas_call(
        body, name=name, grid=(rows // tr,),
        in_specs=[row, row, row, pl.BlockSpec((n, tr, c), lambda i: (0, blk0 + i, 0))],
        out_specs=(row,) * 4, out_shape=(_sds(w.shape),) * 4,
        compiler_params=_cparams(("parallel",)),
    )(w, m, v, parts)


def _adamw_small(items, name):
    k = len(items)
    ns = [it[3].shape[0] for it in items]

    def body(*refs):
        ins, outs = refs[:4 * k], refs[4 * k:]
        for i in range(k):
            w_ref, m_ref, v_ref, p_ref = ins[4 * i:4 * i + 4]
            g = _sum_parts(p_ref, ns[i])
            outs[4 * i][...] = g
            outs[4 * i + 1][...], outs[4 * i + 2][...], outs[4 * i + 3][...] = _adamw_math(
                w_ref[...], g, m_ref[...], v_ref[...])

    flat = [a for it in items for a in it]
    res = pl.pallas_call(
        body, name=name,
        out_shape=tuple(_sds(it[0].shape) for it in items for _ in range(4)),
        compiler_params=_cparams(),
    )(*flat)
    return [tuple(res[4 * i:4 * i + 4]) for i in range(k)]


WEIGHTS = ['ada_w', 'ada_b', 'mlp_w1', 'mlp_w2', 'a_w_in', 'a_ln_g', 'a_ln_b', 'a_w_s', 'a_b_s', 'a_w_out', 'b_mu',
           'b_w_in', 'b_w0', 'b_w1', 'b_w2', 'b_a0', 'b_a1', 'b_a2', 'b_g1', 'b_g2', 'b_k_k', 'b_k_a', 'b_r_k',
           'b_ln_g', 'b_ln_b', 'b_w_out', 'final_g']
GATHERED = [('mlp_w1', 2), ('mlp_w2', 1), ('a_w_in', 2), ('a_w_out', 1), ('b_w_in', 2), ('b_w_out', 1),
            ('b_w1', 1), ('b_a1', 1), ('b_g1', 1), ('b_w2', 2), ('b_a2', 2), ('b_g2', 2)]
VECTORS = ['b_mu', 'b_w0', 'b_a0', 'b_k_k', 'b_k_a', 'b_ln_g', 'b_ln_b']
REPLICATED = ['a_ln_g', 'a_ln_b', 'a_w_s', 'a_b_s', 'b_r_k', 'final_g']
ROW_ALIGN = 16


def _pad_rows(a, mult):
    pad = (-a.shape[-2]) % mult
    return jnp.pad(a, [(0, 0)] * (a.ndim - 2) + [(0, pad), (0, 0)]) if pad else a


def _as2d(a):
    if a.ndim == 1:
        return a.reshape(1, -1)
    lead = 1
    for s in a.shape[:-1]:
        lead *= s
    return a.reshape(lead, a.shape[-1])


def kernel(x, c, ada_w, ada_b, mlp_w1, mlp_w2, a_w_in, a_ln_g, a_ln_b, a_w_s, a_b_s, a_w_out, b_mu, b_w_in, b_w0, b_w1, b_w2, b_a0, b_a1, b_a2, b_g1, b_g2, b_k_k, b_k_a, b_r_k, b_ln_g, b_ln_b, b_w_out, final_g, loss_target, m_ada_w, m_ada_b, m_mlp_w1, m_mlp_w2, m_a_w_in, m_a_ln_g, m_a_ln_b, m_a_w_s, m_a_b_s, m_a_w_out, m_b_mu, m_b_w_in, m_b_w0, m_b_w1, m_b_w2, m_b_a0, m_b_a1, m_b_a2, m_b_g1, m_b_g2, m_b_k_k, m_b_k_a, m_b_r_k, m_b_ln_g, m_b_ln_b, m_b_w_out, m_final_g, v_ada_w, v_ada_b, v_mlp_w1, v_mlp_w2, v_a_w_in, v_a_ln_g, v_a_ln_b, v_a_w_s, v_a_b_s, v_a_w_out, v_b_mu, v_b_w_in, v_b_w0, v_b_w1, v_b_w2, v_b_a0, v_b_a1, v_b_a2, v_b_g1, v_b_g2, v_b_k_k, v_b_k_a, v_b_r_k, v_b_ln_g, v_b_ln_b, v_b_w_out, v_final_g):
    given = dict(locals())
    w = {n: given[n] for n in WEIGHTS}
    bl, t, d = x.shape
    nl = ada_w.shape[0]
    nb = N_DEV * bl
    m_tok = bl * t
    me = 4 * lax.axis_index("x") + 2 * lax.axis_index("y") + lax.axis_index("c")

    c_all = _exchange(c, "gather_c", False).reshape(nb, d)
    cols = ada_w.shape[2]
    ada_b_cols = lax.dynamic_slice(ada_b, (0, me * cols), (nl, cols)).reshape(nl, 1, cols)
    mod_cols = _ada_fwd(c_all, ada_w, ada_b_cols)
    mod_full = jnp.moveaxis(_exchange(mod_cols, "gather_mod", False), 0, 2).reshape(nl, nb, 6 * d)
    mod_mine = lax.dynamic_slice(mod_full, (0, me * bl, 0), (nl, bl, 6 * d)).reshape(nl, bl, 6, d)
    mod_mix = jnp.pad(mod_mine[:, :, 0:3], ((0, 0), (0, 0), (0, 5), (0, 0)))
    mod_mlp = jnp.pad(mod_mine[:, :, 3:6], ((0, 0), (0, 0), (0, 5), (0, 0)))

    rows = [w[n].size // d for n, _ in GATHERED]
    offs = [sum(rows[:i]) for i in range(len(rows))]
    n_rows = sum(rows)
    pack = _pad_rows(jnp.concatenate([w[n].reshape(-1, d) for n, _ in GATHERED], axis=0).astype(BF16), ROW_ALIGN)
    gathered = _exchange(pack, "gather_weights", False)
    full = {}
    for (n, ax), off, nr in zip(GATHERED, offs, rows):
        loc = w[n].shape
        g = jnp.moveaxis(gathered[:, off:off + nr].reshape((N_DEV,) + loc), 0, ax)
        full[n] = g.reshape(loc[:ax] + (N_DEV * loc[ax],) + loc[ax + 1:])
    vec_loc = _pad_rows(jnp.concatenate([_as2d(w[n]) for n in VECTORS], axis=0), ROW_ALIGN)
    n_vec_rows = sum(_as2d(w[n]).shape[0] for n in VECTORS)
    vec = jnp.moveaxis(_exchange(vec_loc, "gather_vectors", False), 0, 1).reshape(N_VEC, d)
    vec = vec.at[n_vec_rows].set(b_r_k.reshape(d))

    e_ind, et_ind = _head_indicators(d)
    gd = d // SGU_GROUPS
    group_ind = (jnp.arange(SGU_GROUPS)[:, None] == jnp.arange(d)[None, :] // gd).astype(BF16)
    bias_full = jnp.repeat(a_b_s[0].T, gd, axis=1)
    pad_c = lambda a, n: jnp.pad(a, ((0, 0), (0, n - a.shape[1])))
    pad_r = lambda a, n: jnp.pad(a, ((0, n - a.shape[0]), (0, 0)))
    rwkv_w = (full['b_w_in'][0], pad_c(full['b_w1'][0], LORA_PAD), pad_c(full['b_a1'][0], LORA_PAD),
              pad_c(full['b_g1'][0], GATE_PAD), pad_r(full['b_w2'][0], LORA_PAD), pad_r(full['b_a2'][0], LORA_PAD),
              pad_r(full['b_g2'][0], GATE_PAD))
    sgu_args = (full['a_w_in'][0], a_ln_g, a_ln_b, a_w_s[0], bias_full, full['a_w_out'][0])

    x0 = x
    x1, mix_a = _sgu_fwd(x0, mod_mix[0], *sgu_args)
    x2, ff0 = _mlp_fwd(x1, mod_mlp[0], full['mlp_w1'][0], full['mlp_w2'][0])
    r, ld, k2, v, a_s, b_s, gate = _rwkv_pre_fwd(x2, mod_mix[1], vec, e_ind, et_ind, rwkv_w)
    y, s0 = _wkv_fwd(r, ld, k2, v, a_s, b_s)
    x3, mix_b = _rwkv_post_fwd(x2, y, r, k2, v, gate, mod_mix[1], vec, e_ind, et_ind, full['b_w_out'][0])
    x4, ff1 = _mlp_fwd(x3, mod_mlp[1], full['mlp_w1'][1], full['mlp_w2'][1])
    dx4, loss_blk, dfinal = _final(x4, loss_target, final_g.reshape(1, d))
    loss = lax.psum(loss_blk[0, 0], ("x", "y", "c"))

    tok = lambda a: a.reshape(m_tok, a.shape[-1])
    grads = {}
    dx3, dmod_mlp1, h_b, dff_b, q_b, dp_b = _mlp_bwd(x3, dx4, ff1, mod_mlp[1], full['mlp_w1'][1], full['mlp_w2'][1])
    gw1_1 = _matmul_tn(tok(h_b), tok(dp_b), "grad_mlp_w1_l1")
    gw2_1 = _matmul_tn(tok(q_b), tok(dff_b), "grad_mlp_w2_l1")
    dy, dr_p, dk2_p, dv_p, dgate_act, o_b, dmix_b, dgate_b, small_post = _rwkv_post_bwd(
        dx3, mix_b, y, r, k2, v, gate, mod_mix[1], vec, e_ind, et_ind, full['b_w_out'][0])
    grads['b_w_out'] = _matmul_tn(tok(o_b), tok(dmix_b), "grad_b_w_out")[None]
    dr_s, dld, dk2_s, dv_s, das, dbs = _wkv_bwd(r, ld, k2, v, a_s, b_s, s0, dy)
    (dh, dhp, xr_b, xw_b, xk_b, xv_b, xa_b, xg_b, dr_b, dk_b, dv_b, dtw_b, dta_b, dtg_b, tw2_b, ta_b, sg_b,
     dlw_b, dla_b, dg_b, small_pre) = _rwkv_pre_bwd(x2, mod_mix[1], vec, e_ind, et_ind, rwkv_w,
                                                    dr_p, dk2_p, dv_p, dgate_act, dr_s, dld, dk2_s, dv_s, das, dbs)
    grads['b_w_in'] = jnp.concatenate([_matmul_tn(tok(xr_b), tok(dr_b), "grad_b_w_r"),
                                       _matmul_tn(tok(xk_b), tok(dk_b), "grad_b_w_k"),
                                       _matmul_tn(tok(xv_b), tok(dv_b), "grad_b_w_v")], axis=1)[None]
    lw_, lg_ = b_w1.shape[2], b_g1.shape[2]
    grads['b_w1'] = _matmul_tn(tok(xw_b), tok(dtw_b), "grad_b_w1")[None, :, :lw_]
    grads['b_a1'] = _matmul_tn(tok(xa_b), tok(dta_b), "grad_b_a1")[None, :, :lw_]
    grads['b_g1'] = _matmul_tn(tok(xg_b), tok(dtg_b), "grad_b_g1")[None, :, :lg_]
    grads['b_w2'] = _matmul_tn(tok(tw2_b), tok(dlw_b), "grad_b_w2")[None, :lw_]
    grads['b_a2'] = _matmul_tn(tok(ta_b), tok(dla_b), "grad_b_a2")[None, :lw_]
    grads['b_g2'] = _matmul_tn(tok(sg_b), tok(dg_b), "grad_b_g2")[None, :lg_]
    dx2, dmod_mix1 = _norm_bwd(x2, dx3, dh, dhp, mod_mix[1], dgate_b)
    dx1, dmod_mlp0, h_b, dff_b, q_b, dp_b = _mlp_bwd(x1, dx2, ff0, mod_mlp[0], full['mlp_w1'][0], full['mlp_w2'][0])
    gw1_0 = _matmul_tn(tok(h_b), tok(dp_b), "grad_mlp_w1_l0")
    gw2_0 = _matmul_tn(tok(q_b), tok(dff_b), "grad_mlp_w2_l0")
    grads['mlp_w1'] = jnp.stack([gw1_0, gw1_1])
    grads['mlp_w2'] = jnp.stack([gw2_0, gw2_1])
    dx0, dmod_mix0, h_b, dpre_b, z_b, dmix_b, small_sgu, d_ws, d_bs = _sgu_bwd(
        x0, dx1, mix_a, mod_mix[0], *sgu_args, group_ind)
    grads['a_w_in'] = _matmul_tn(tok(h_b), tok(dpre_b), "grad_a_w_in")[None]
    grads['a_w_out'] = _matmul_tn(tok(z_b), tok(dmix_b), "grad_a_w_out")[None]

    dmod_mine = jnp.stack([jnp.concatenate([dmod_mix0[:, 0:3], dmod_mlp0[:, 0:3]], axis=1),
                           jnp.concatenate([dmod_mix1[:, 0:3], dmod_mlp1[:, 0:3]], axis=1)], axis=1)
    dmod_all = _exchange(dmod_mine.reshape(bl, nl * 6 * d), "gather_dmod", False)
    dmod_all = jnp.moveaxis(dmod_all.reshape(nb, nl, 6 * d), 0, 1)
    dmod_cols = lax.dynamic_slice(dmod_all, (0, 0, me * cols), (nl, nb, cols))
    g_ada_w, g_ada_b = _ada_bwd(c_all, dmod_cols, dmod_all)

    vec_g = jnp.concatenate([small_pre[0:10], small_post[0:2]], axis=0)
    vec_parts = jnp.moveaxis(vec_g.reshape(n_vec_rows, N_DEV, d // N_DEV), 1, 0).reshape(N_DEV, -1)
    vec_rows = -(-vec_parts.shape[1] // d)
    vec_parts = jnp.pad(vec_parts, ((0, 0), (0, vec_rows * d - vec_parts.shape[1]))).reshape(N_DEV, vec_rows, d)
    parts = []
    for n, ax in GATHERED:
        loc = w[n].shape
        g = grads[n].reshape(loc[:ax] + (N_DEV, loc[ax]) + loc[ax + 1:])
        parts.append(jnp.moveaxis(g, ax, 0).reshape(N_DEV, -1, d))
    gpack = _pad_rows(jnp.concatenate(parts + [vec_parts], axis=1), ROW_ALIGN)
    recv = _exchange(gpack, "scatter_grads", True)

    rep_g = {'a_ln_g': small_sgu[0:1], 'a_ln_b': small_sgu[1:2], 'a_w_s': d_ws.reshape(-1, d), 'a_b_s': d_bs.reshape(1, d),
             'b_r_k': small_post[2:3], 'final_g': dfinal[0:1]}
    rep_rows = [rep_g[n].shape[0] for n in REPLICATED]
    rep_pack = _pad_rows(jnp.concatenate([rep_g[n] for n in REPLICATED], axis=0), 8)
    rep_all = _exchange(rep_pack, "gather_replicated_grads", False)

    mom = {n: given['m_' + n] for n in WEIGHTS}
    var = {n: given['v_' + n] for n in WEIGHTS}
    out = {}
    big = GATHERED[:6]
    for (n, _), off in zip(big, offs):
        res = _adamw_rows(w[n].reshape(-1, d), mom[n].reshape(-1, d), var[n].reshape(-1, d), recv, off, "adamw_" + n)
        out[n] = tuple(a.reshape(w[n].shape) for a in res)
    res = _adamw_rows(ada_w.reshape(-1, d), m_ada_w.reshape(-1, d), v_ada_w.reshape(-1, d),
                      g_ada_w.reshape(1, -1, d), 0, "adamw_ada_w")
    out['ada_w'] = tuple(a.reshape(ada_w.shape) for a in res)

    items, names = [], []

    def add(n, part):
        s2 = _as2d(w[n]).shape
        items.append((_as2d(w[n]), _as2d(mom[n]), _as2d(var[n]), part.reshape((part.shape[0],) + s2)))
        names.append(n)

    for (n, _), off, nr in list(zip(GATHERED, offs, rows))[6:]:
        add(n, recv[:, off:off + nr])
    vflat = recv[:, n_rows:n_rows + vec_rows].reshape(N_DEV, -1)
    vo = 0
    for n in VECTORS:
        sz = w[n].size
        add(n, vflat[:, vo:vo + sz])
        vo += sz
    ro = 0
    for n, nr in zip(REPLICATED, rep_rows):
        add(n, rep_all[:, ro:ro + nr])
        ro += nr
    add('ada_b', g_ada_b[None])
    for n, res in zip(names, _adamw_small(items, "adamw_small")):
        out[n] = tuple(a.reshape(w[n].shape) for a in res)

    return (loss, dx0, *[out[n][0] for n in WEIGHTS], *[out[n][1] for n in WEIGHTS],
            *[out[n][2] for n in WEIGHTS], *[out[n][3] for n in WEIGHTS])
```

```python
import functools

import jax
import jax.numpy as jnp
from jax import lax
from jax.experimental import pallas as pl
from jax.experimental.pallas import tpu as pltpu

F32 = jnp.float32
BF16 = jnp.bfloat16
HIGHEST = lax.Precision.HIGHEST

N_DEV = 8
RMS_EPS = 1e-6
LN_EPS = 1e-5
HEAD = 64
GN_EPS = HEAD * 1e-5
L2_EPS = 1e-12
SGU_CHUNK = 128
SGU_GROUPS = 8
WKV_CHUNK = 64
WKV_HEADS_PER_STEP = 8
LORA_PAD = 128
GATE_PAD = 256
ADAM_LR, ADAM_B1, ADAM_B2, ADAM_EPS, ADAM_WD, ADAM_STEP = 0.001, 0.9, 0.999, 1e-08, 0.01, 10
VMEM_LIMIT = 56 * 1024 * 1024


def _cparams(sem=None, **kw):
    if sem is not None:
        kw["dimension_semantics"] = sem
    return pltpu.CompilerParams(vmem_limit_bytes=VMEM_LIMIT, **kw)


def _dot(a, b):
    return jnp.dot(a.astype(BF16), b.astype(BF16), preferred_element_type=F32)


def _dot_nt(a, b):
    return lax.dot_general(a.astype(BF16), b.astype(BF16), (((1,), (1,)), ((), ())), preferred_element_type=F32)


def _dot_tn(a, b):
    return lax.dot_general(a.astype(BF16), b.astype(BF16), (((0,), (0,)), ((), ())), preferred_element_type=F32)


def _bdot(a, b, dims):
    dn = (dims, ((0,), (0,)))
    ah, bh = a.astype(BF16), b.astype(BF16)
    al, bl = (a - ah.astype(F32)).astype(BF16), (b - bh.astype(F32)).astype(BF16)
    dot = functools.partial(lax.dot_general, dimension_numbers=dn, preferred_element_type=F32)
    return dot(ah, bh) + (dot(ah, bl) + dot(al, bh))


@jax.custom_vjp
def _bmm_nn(a, b):
    return _bdot(a, b, ((2,), (1,)))


@jax.custom_vjp
def _bmm_nt(a, b):
    return _bdot(a, b, ((2,), (2,)))


@jax.custom_vjp
def _bmm_tn(a, b):
    return _bdot(a, b, ((1,), (1,)))


_bmm_nn.defvjp(lambda a, b: (_bmm_nn(a, b), (a, b)), lambda res, g: (_bmm_nt(g, res[1]), _bmm_tn(res[0], g)))
_bmm_nt.defvjp(lambda a, b: (_bmm_nt(a, b), (a, b)), lambda res, g: (_bmm_nn(g, res[1]), _bmm_tn(g, res[0])))
_bmm_tn.defvjp(lambda a, b: (_bmm_tn(a, b), (a, b)), lambda res, g: (_bmm_nt(res[1], g), _bmm_nn(res[0], g)))


def _wkv_chunk(s0, r, ld, k, v, a, b):
    nh, n, _ = r.shape
    row = lax.broadcasted_iota(jnp.int32, (n, n), 0)
    col = lax.broadcasted_iota(jnp.int32, (n, n), 1)
    incl = row >= col
    strict = row > col
    ones_incl = jnp.broadcast_to(jnp.where(incl, 1.0, 0.0).astype(F32), (nh, n, n))
    eye = jnp.where(row == col, 1.0, 0.0).astype(F32)[None]
    c = _bmm_nn(ones_incl, ld)
    c_end = c[:, n - 1:n, :]
    ec, enc, ecx, eend = jnp.exp(c), jnp.exp(-c), jnp.exp(c - ld), jnp.exp(c_end - c)
    ar = jnp.concatenate([a * ecx, r * ec], axis=1)
    mask = jnp.concatenate([strict, incl], axis=0)[None]
    m_b = jnp.where(mask, _bmm_nt(ar, b * enc), 0.0)
    m_k = jnp.where(mask, _bmm_nt(ar, k * enc), 0.0)
    a_ab, a_rb = m_b[:, :n], m_b[:, n:]
    p = a_ab
    tinv = eye + p
    steps = max(1, (n - 1).bit_length()) - 1
    for _ in range(steps):
        p = _bmm_nn(p, p)
        tinv = tinv + _bmm_nn(tinv, p)
    base = _bmm_nt(ar, s0) + _bmm_nn(m_k, v)
    u = _bmm_nn(tinv, base[:, :n])
    y = base[:, n:] + _bmm_nn(a_rb, u)
    s1 = s0 * jnp.exp(c_end) + _bmm_tn(jnp.concatenate([u, v], axis=1), jnp.concatenate([b * eend, k * eend], axis=1))
    return y, s1


def _wkv_specs(bl, nh, t):
    hb, lc = WKV_HEADS_PER_STEP, WKV_CHUNK
    return hb, lc, (bl, nh // hb, t // lc)


def _wkv_fwd(r, ld, k, v, a, b):
    bl, nh, t, n = r.shape
    hb, lc, grid = _wkv_specs(bl, nh, t)
    nc = t // lc

    def body(r_ref, ld_ref, k_ref, v_ref, a_ref, b_ref, y_ref, s0_ref, s_scr):
        @pl.when(pl.program_id(2) == 0)
        def _():
            s_scr[...] = jnp.zeros_like(s_scr)

        s0 = s_scr[...]
        s0_ref[0, :, 0] = s0
        y, s1 = _wkv_chunk(s0, r_ref[0], ld_ref[0], k_ref[0], v_ref[0], a_ref[0], b_ref[0])
        y_ref[0] = y
        s_scr[...] = s1

    seq = pl.BlockSpec((1, hb, lc, n), lambda e, h, c: (e, h, c, 0))
    return pl.pallas_call(
        body, name="wkv_fwd", grid=grid,
        in_specs=[seq] * 6,
        out_specs=(seq, pl.BlockSpec((1, hb, 1, n, n), lambda e, h, c: (e, h, c, 0, 0))),
        out_shape=(jax.ShapeDtypeStruct((bl, nh, t, n), F32), jax.ShapeDtypeStruct((bl, nh, nc, n, n), F32)),
        scratch_shapes=[pltpu.VMEM((hb, n, n), F32)],
        compiler_params=_cparams(("arbitrary", "arbitrary", "arbitrary")),
    )(r, ld, k, v, a, b)


def _wkv_bwd(r, ld, k, v, a, b, s0_all, dy):
    bl, nh, t, n = r.shape
    hb, lc, grid = _wkv_specs(bl, nh, t)
    nc = t // lc

    def body(r_ref, ld_ref, k_ref, v_ref, a_ref, b_ref, s0_ref, dy_ref,
             dr_ref, dld_ref, dk_ref, dv_ref, da_ref, db_ref, ds_scr):
        @pl.when(pl.program_id(2) == 0)
        def _():
            ds_scr[...] = jnp.zeros_like(ds_scr)

        args = (s0_ref[0, :, 0], r_ref[0], ld_ref[0], k_ref[0], v_ref[0], a_ref[0], b_ref[0])
        _, vjp = jax.vjp(_wkv_chunk, *args)
        ds0, dr, dld, dk, dv, da, db = vjp((dy_ref[0], ds_scr[...]))
        ds_scr[...] = ds0
        dr_ref[0], dld_ref[0], dk_ref[0], dv_ref[0], da_ref[0], db_ref[0] = dr, dld, dk, dv, da, db

    seq = pl.BlockSpec((1, hb, lc, n), lambda e, h, c: (e, h, nc - 1 - c, 0))
    st = pl.BlockSpec((1, hb, 1, n, n), lambda e, h, c: (e, h, nc - 1 - c, 0, 0))
    out = jax.ShapeDtypeStruct((bl, nh, t, n), F32)
    return pl.pallas_call(
        body, name="wkv_bwd", grid=grid,
        in_specs=[seq] * 6 + [st, seq],
        out_specs=(seq,) * 6, out_shape=(out,) * 6,
        scratch_shapes=[pltpu.VMEM((hb, n, n), F32)],
        compiler_params=_cparams(("arbitrary", "arbitrary", "arbitrary")),
    )(r, ld, k, v, a, b, s0_all, dy)


def _exchange(x, name, scatter):
    blk = x.shape[1:] if scatter else x.shape

    def body(x_ref, o_ref, send_sems, recv_sems, local_sem):
        pos = (lax.axis_index("x"), lax.axis_index("y"), lax.axis_index("c"))
        me = 4 * pos[0] + 2 * pos[1] + pos[2]

        def peer_of(m):
            p = tuple(1 - pos[i] if (m >> (2 - i)) & 1 else pos[i] for i in range(3))
            return p, 4 * p[0] + 2 * p[1] + p[2]

        def copy(m):
            p, pidx = peer_of(m)
            return pltpu.make_async_remote_copy(
                src_ref=x_ref.at[pidx] if scatter else x_ref, dst_ref=o_ref.at[me],
                send_sem=send_sems.at[m - 1], recv_sem=recv_sems.at[m - 1],
                device_id=p, device_id_type=pl.DeviceIdType.MESH)

        def arrival(m):
            p, pidx = peer_of(m)
            return pltpu.make_async_remote_copy(
                src_ref=x_ref.at[pidx] if scatter else x_ref, dst_ref=o_ref.at[pidx],
                send_sem=send_sems.at[m - 1], recv_sem=recv_sems.at[m - 1],
                device_id=p, device_id_type=pl.DeviceIdType.MESH)

        mine = pltpu.make_async_copy(x_ref.at[me] if scatter else x_ref, o_ref.at[me], local_sem)
        mine.start()
        sends = [copy(m) for m in range(1, N_DEV)]
        for cp in sends:
            cp.start()
        for m in range(1, N_DEV):
            arrival(m).wait_recv()
        for cp in sends:
            cp.wait_send()
        mine.wait()

    return pl.pallas_call(
        body, name=name,
        out_shape=jax.ShapeDtypeStruct((N_DEV,) + tuple(blk), x.dtype),
        in_specs=[pl.BlockSpec(memory_space=pl.ANY)],
        out_specs=pl.BlockSpec(memory_space=pl.ANY),
        scratch_shapes=[pltpu.SemaphoreType.DMA((N_DEV - 1,)), pltpu.SemaphoreType.DMA((N_DEV - 1,)),
                        pltpu.SemaphoreType.DMA],
    )(x)


def _rms(x):
    inv = lax.rsqrt(jnp.mean(x * x, axis=-1, keepdims=True) + RMS_EPS)
    return x * inv, inv


def _rms_bwd(xn, inv, dxn):
    return inv * (dxn - xn * jnp.mean(dxn * xn, axis=-1, keepdims=True))


def _colsum(x):
    return jnp.sum(x, axis=0, keepdims=True)


def _sigmoid(x):
    return 0.5 * (jnp.tanh(0.5 * x) + 1.0)


def _split_bf16(x):
    hi = x.astype(BF16)
    return hi, (x - hi.astype(F32)).astype(BF16)


def _dot_split(x, e):
    hi, lo = _split_bf16(x)
    return jnp.dot(hi, e, preferred_element_type=F32) + jnp.dot(lo, e, preferred_element_type=F32)


@jax.custom_vjp
def _headsum(x, e, et):
    return _dot_split(_dot_split(x, e), et)


_headsum.defvjp(lambda x, e, et: (_headsum(x, e, et), (e, et)),
                lambda res, g: (_headsum(g, *res), jnp.zeros_like(res[0]), jnp.zeros_like(res[1])))


def _make_headsum(e, et):
    return lambda x: _headsum(x, e, et)


def _head_indicators(d):
    e = (jnp.arange(d)[:, None] // HEAD == jnp.arange(128)[None, :]).astype(BF16)
    return e, e.T


def _rwkv_elem(r, k, lw, la, w0, a0, k_k, k_a, headsum):
    z = w0 + lw
    w_log = -(jnp.maximum(-z, 0.0) + jnp.log(1.0 + jnp.exp(-jnp.abs(z)))) - 0.5
    ld = -jnp.exp(w_log)
    a = _sigmoid(a0 + la)
    kkp = k * k_k
    kk = kkp / jnp.maximum(jnp.sqrt(headsum(kkp * kkp)), L2_EPS)
    k2 = k * (1.0 + (a - 1.0) * k_a)
    del r
    return ld, k2, -kk, kk * a


def _rwkv_post(y, r, k2, v, g, ln_g, ln_b, r_k, headsum):
    m = headsum(y) * (1.0 / HEAD)
    yc = y - m
    var = headsum(yc * yc) * (1.0 / HEAD)
    yn = yc * lax.rsqrt(var + GN_EPS)
    bonus = headsum(r * k2 * r_k) * v
    return (yn * ln_g + ln_b + bonus) * g


def _shift_down(h, first_row):
    rolled = pltpu.roll(h, 1, 0)
    row = lax.broadcasted_iota(jnp.int32, h.shape, 0)
    return jnp.where(row == 0, first_row, rolled)


def _shift_up(h, last_row):
    n = h.shape[0]
    rolled = pltpu.roll(h, n - 1, 0)
    row = lax.broadcasted_iota(jnp.int32, h.shape, 0)
    return jnp.where(row == n - 1, last_row, rolled)


def _gelu(p):
    return 0.5 * p * (1.0 + lax.erf(p * 0.7071067811865476))


def _gelu_grad(p):
    return 0.5 * (1.0 + lax.erf(p * 0.7071067811865476)) + p * jnp.exp(-0.5 * p * p) * 0.3989422804014327


def _tok(tm, d):
    return pl.BlockSpec((1, tm, d), lambda e, t, *_: (e, t, 0))


def _per_example(rows, d):
    return pl.BlockSpec((1, rows, d), lambda e, t, *_: (e, 0, 0))


def _whole(shape):
    nd = len(shape)
    return pl.BlockSpec(tuple(shape), lambda *_: (0,) * nd)


def _heads(nh, tm):
    return pl.BlockSpec((1, nh, tm, HEAD), lambda e, t, *_: (e, 0, t, 0))


def _sds(shape, dtype=F32):
    return jax.ShapeDtypeStruct(tuple(shape), dtype)


def _add_rows(ref, first, rows):
    @pl.when(first)
    def _():
        ref[0] = jnp.zeros(ref.shape[1:], ref.dtype)

    for i, r in enumerate(rows):
        ref[0, i:i + 1] += r


def _first(e, t):
    return jnp.logical_and(e == 0, t == 0)


def _ada_fwd(c_all, ada_w, ada_b_cols):
    nl, d, cols = ada_w.shape
    nb = c_all.shape[0]

    def body(c_ref, w_ref, b_ref, o_ref):
        c = c_ref[...]
        cond = c * _sigmoid(c)
        for i in range(nl):
            o_ref[i] = _dot(cond, w_ref[i]) + b_ref[i]

    return pl.pallas_call(
        body, name="ada_fwd", out_shape=_sds((nl, nb, cols)),
        compiler_params=_cparams(),
    )(c_all, ada_w, ada_b_cols)


def _ada_bwd(c_all, dmod_cols, dmod_full):
    nl, nb, cols = dmod_cols.shape
    d = c_all.shape[1]

    def body(c_ref, g_ref, f_ref, o_ref, b_ref):
        c = c_ref[...]
        cond = c * _sigmoid(c)
        for i in range(nl):
            o_ref[i] = _dot_tn(cond, g_ref[i])
            b_ref[i:i + 1] = jnp.sum(f_ref[i], axis=0, keepdims=True)

    return pl.pallas_call(
        body, name="ada_bwd", out_shape=(_sds((nl, d, cols)), _sds((nl, dmod_full.shape[2]))),
        compiler_params=_cparams(),
    )(c_all, dmod_cols, dmod_full)


def _matmul_tn(a, b, name):
    m, ka = a.shape
    n = b.shape[1]
    tm = min(m, 512)
    tk = min(ka, 1024)
    tn = min(n, 512)
    steps = m // tm

    def body(a_ref, b_ref, o_ref, acc):
        s = pl.program_id(2)

        @pl.when(s == 0)
        def _():
            acc[...] = jnp.zeros_like(acc)

        acc[...] += _dot_tn(a_ref[...], b_ref[...])

        @pl.when(s == steps - 1)
        def _():
            o_ref[...] = acc[...].astype(BF16)

    return pl.pallas_call(
        body, name=name, grid=(ka // tk, n // tn, steps),
        in_specs=[pl.BlockSpec((tm, tk), lambda i, j, s: (s, i)), pl.BlockSpec((tm, tn), lambda i, j, s: (s, j))],
        out_specs=pl.BlockSpec((tk, tn), lambda i, j, s: (i, j)),
        out_shape=_sds((ka, n), BF16),
        scratch_shapes=[pltpu.VMEM((tk, tn), F32)],
        compiler_params=_cparams(("parallel", "parallel", "arbitrary")),
    )(a, b)


MLP_TM = 512
MLP_FJ = 1024


def _mlp_fwd(x, mod, w1, w2):
    bl, t, d = x.shape
    f = w1.shape[1]
    tm, fj = min(t, MLP_TM), min(f, MLP_FJ)
    nj = f // fj

    def body(x_ref, mod_ref, w1_ref, w2_ref, xo_ref, ff_ref, h_scr, acc):
        j = pl.program_id(2)

        @pl.when(j == 0)
        def _():
            xn, _ = _rms(x_ref[0])
            h_scr[...] = (xn * (1.0 + mod_ref[0, 1:2]) + mod_ref[0, 0:1]).astype(BF16)
            acc[...] = jnp.zeros_like(acc)

        p = jnp.dot(h_scr[...], w1_ref[...], preferred_element_type=F32)
        q = jnp.square(jnp.maximum(p, 0.0))
        acc[...] += _dot(q, w2_ref[...])

        @pl.when(j == nj - 1)
        def _():
            ff_ref[0] = acc[...]
            xo_ref[0] = x_ref[0] + mod_ref[0, 2:3] * acc[...]

    return pl.pallas_call(
        body, name="mlp_fwd", grid=(bl, t // tm, nj),
        in_specs=[_tok(tm, d), _per_example(8, d),
                  pl.BlockSpec((d, fj), lambda e, i, j: (0, j)), pl.BlockSpec((fj, d), lambda e, i, j: (j, 0))],
        out_specs=(_tok(tm, d), _tok(tm, d)),
        out_shape=(_sds(x.shape), _sds(x.shape)),
        scratch_shapes=[pltpu.VMEM((tm, d), BF16), pltpu.VMEM((tm, d), F32)],
        compiler_params=_cparams(("arbitrary", "arbitrary", "arbitrary")),
    )(x, mod, w1, w2)


def _mlp_bwd(x, dxo, ff, mod, w1, w2):
    bl, t, d = x.shape
    f = w1.shape[1]
    tm, fj = min(t, MLP_TM), min(f, MLP_FJ)
    nj = f // fj

    def body(x_ref, dxo_ref, ff_ref, mod_ref, w1_ref, w2_ref,
             dx_ref, dmod_ref, h_ref, dff_ref, q_ref, dp_ref, acc):
        ti, j = pl.program_id(1), pl.program_id(2)

        @pl.when(j == 0)
        def _():
            xn, _ = _rms(x_ref[0])
            h_ref[0] = (xn * (1.0 + mod_ref[0, 1:2]) + mod_ref[0, 0:1]).astype(BF16)
            dff_ref[0] = (mod_ref[0, 2:3] * dxo_ref[0]).astype(BF16)
            acc[...] = jnp.zeros_like(acc)

        p = jnp.dot(h_ref[0], w1_ref[...], preferred_element_type=F32)
        rl = jnp.maximum(p, 0.0)
        q_ref[0] = jnp.square(rl).astype(BF16)
        dp = (_dot_nt(dff_ref[0], w2_ref[...]) * (2.0 * rl)).astype(BF16)
        dp_ref[0] = dp
        acc[...] += _dot_nt(dp, w1_ref[...])

        @pl.when(j == nj - 1)
        def _():
            xn, inv = _rms(x_ref[0])
            dh = acc[...]
            dx_ref[0] = dxo_ref[0] + _rms_bwd(xn, inv, dh * (1.0 + mod_ref[0, 1:2]))
            _add_rows(dmod_ref, ti == 0, [_colsum(dh), _colsum(dh * xn), _colsum(dxo_ref[0] * ff_ref[0])])

    big = lambda: pl.BlockSpec((1, tm, fj), lambda e, i, j: (e, i, j))
    return pl.pallas_call(
        body, name="mlp_bwd", grid=(bl, t // tm, nj),
        in_specs=[_tok(tm, d), _tok(tm, d), _tok(tm, d), _per_example(8, d),
                  pl.BlockSpec((d, fj), lambda e, i, j: (0, j)), pl.BlockSpec((fj, d), lambda e, i, j: (j, 0))],
        out_specs=(_tok(tm, d), _per_example(8, d), _tok(tm, d), _tok(tm, d), big(), big()),
        out_shape=(_sds(x.shape), _sds((bl, 8, d)), _sds(x.shape, BF16), _sds(x.shape, BF16),
                   _sds((bl, t, f), BF16), _sds((bl, t, f), BF16)),
        scratch_shapes=[pltpu.VMEM((tm, d), F32)],
        compiler_params=_cparams(("arbitrary", "arbitrary", "arbitrary")),
    )(x, dxo, ff, mod, w1, w2)


SGU_TM = 256


def _sgu_core(x, mod_ref, win_ref, lng, lnb, ws_ref, bias_ref):
    tm, d = x.shape
    xn, inv = _rms(x)
    h = (xn * (1.0 + mod_ref[0, 1:2]) + mod_ref[0, 0:1]).astype(BF16)
    pre = jnp.dot(h, win_ref[...], preferred_element_type=F32)
    uv = _gelu(pre)
    u, v = uv[:, :d], uv[:, d:]
    mu = jnp.mean(v, axis=-1, keepdims=True)
    vc = v - mu
    rstd = lax.rsqrt(jnp.mean(vc * vc, axis=-1, keepdims=True) + LN_EPS)
    vhat = vc * rstd
    vln = vhat * lng + lnb
    gd = d // SGU_GROUPS
    rows = []
    for c in range(tm // SGU_CHUNK):
        cols = []
        for g in range(SGU_GROUPS):
            cols.append(_dot(ws_ref[g], vln[c * SGU_CHUNK:(c + 1) * SGU_CHUNK, g * gd:(g + 1) * gd]))
        rows.append(jnp.concatenate(cols, axis=1) + bias_ref[...])
    sv = jnp.concatenate(rows, axis=0)
    return xn, inv, h, pre, u, vhat, rstd, vln, sv


def _sgu_masked(ws_ref, wm_scr):
    row = lax.broadcasted_iota(jnp.int32, (SGU_CHUNK, SGU_CHUNK), 0)
    col = lax.broadcasted_iota(jnp.int32, (SGU_CHUNK, SGU_CHUNK), 1)
    for g in range(SGU_GROUPS):
        wm_scr[g] = jnp.where(row >= col, ws_ref[g], 0.0).astype(BF16)


def _sgu_fwd(x, mod, w_in, ln_g, ln_b, w_s, bias_full, w_out):
    bl, t, d = x.shape
    tm = min(t, SGU_TM)

    def body(x_ref, mod_ref, win_ref, lng_ref, lnb_ref, ws_ref, bias_ref, wout_ref, xo_ref, mix_ref, wm_scr):
        _sgu_masked(ws_ref, wm_scr)
        xt = x_ref[0]
        *_, u, _, _, _, sv = _sgu_core(xt, mod_ref, win_ref, lng_ref[...], lnb_ref[...], wm_scr, bias_ref)
        mix = _dot(u * sv, wout_ref[...])
        mix_ref[0] = mix
        xo_ref[0] = xt + mod_ref[0, 2:3] * mix

    return pl.pallas_call(
        body, name="sgu_fwd", grid=(bl, t // tm),
        in_specs=[_tok(tm, d), _per_example(8, d), _whole(w_in.shape), _whole(ln_g.shape), _whole(ln_b.shape),
                  _whole(w_s.shape), _whole(bias_full.shape), _whole(w_out.shape)],
        out_specs=(_tok(tm, d), _tok(tm, d)),
        out_shape=(_sds(x.shape), _sds(x.shape)),
        scratch_shapes=[pltpu.VMEM(w_s.shape, BF16)],
        compiler_params=_cparams(("arbitrary", "arbitrary")),
    )(x, mod, w_in, ln_g, ln_b, w_s, bias_full, w_out)


def _sgu_bwd(x, dxo, mix, mod, w_in, ln_g, ln_b, w_s, bias_full, w_out, group_ind):
    bl, t, d = x.shape
    tm = min(t, SGU_TM)
    gd = d // SGU_GROUPS

    def body(x_ref, dxo_ref, mix_ref, mod_ref, win_ref, lng_ref, lnb_ref, ws_ref, bias_ref, wout_ref, ind_ref,
             dx_ref, dmod_ref, h_ref, dpre_ref, z_ref, dmix_ref, small_ref, dws_ref, dbs_ref, wm_scr, dbias_scr):
        e, ti = pl.program_id(0), pl.program_id(1)
        _sgu_masked(ws_ref, wm_scr)
        xt, dxo = x_ref[0], dxo_ref[0]
        lng = lng_ref[...]
        xn, inv, h, pre, u, vhat, rstd, vln, sv = _sgu_core(xt, mod_ref, win_ref, lng, lnb_ref[...], wm_scr, bias_ref)
        h_ref[0] = h
        z_ref[0] = (u * sv).astype(BF16)
        dmix = mod_ref[0, 2:3] * dxo
        dmix_ref[0] = dmix.astype(BF16)
        dz = _dot_nt(dmix, wout_ref[...])
        du, dsv = dz * sv, dz * u

        @pl.when(_first(e, ti))
        def _():
            dws_ref[...] = jnp.zeros_like(dws_ref)
            dbias_scr[...] = jnp.zeros_like(dbias_scr)
            small_ref[...] = jnp.zeros_like(small_ref)

        row = lax.broadcasted_iota(jnp.int32, (SGU_CHUNK, SGU_CHUNK), 0)
        col = lax.broadcasted_iota(jnp.int32, (SGU_CHUNK, SGU_CHUNK), 1)
        rows = []
        for c in range(tm // SGU_CHUNK):
            rs = slice(c * SGU_CHUNK, (c + 1) * SGU_CHUNK)
            dbias_scr[...] += dsv[rs]
            cols = []
            for g in range(SGU_GROUPS):
                cs = slice(g * gd, (g + 1) * gd)
                cols.append(_dot_tn(wm_scr[g], dsv[rs, cs]))
                dws_ref[g] += jnp.where(row >= col, _dot_nt(dsv[rs, cs], vln[rs, cs]), 0.0)
            rows.append(jnp.concatenate(cols, axis=1))
        dvln = jnp.concatenate(rows, axis=0)
        small_ref[0:1] += _colsum(dvln * vhat)
        small_ref[1:2] += _colsum(dvln)
        dvhat = dvln * lng
        dv = rstd * (dvhat - jnp.mean(dvhat, axis=-1, keepdims=True)
                     - vhat * jnp.mean(dvhat * vhat, axis=-1, keepdims=True))
        dpre = (jnp.concatenate([du, dv], axis=1) * _gelu_grad(pre)).astype(BF16)
        dpre_ref[0] = dpre
        dh = _dot_nt(dpre, win_ref[...])
        dx_ref[0] = dxo + _rms_bwd(xn, inv, dh * (1.0 + mod_ref[0, 1:2]))
        _add_rows(dmod_ref, ti == 0, [_colsum(dh), _colsum(dh * xn), _colsum(dxo * mix_ref[0])])

        @pl.when(jnp.logical_and(e == bl - 1, ti == t // tm - 1))
        def _():
            hi, lo = _split_bf16(dbias_scr[...])
            ind = ind_ref[...]
            dbs_ref[...] = (lax.dot_general(ind, hi, (((1,), (1,)), ((), ())), preferred_element_type=F32)
                            + lax.dot_general(ind, lo, (((1,), (1,)), ((), ())), preferred_element_type=F32))

    return pl.pallas_call(
        body, name="sgu_bwd", grid=(bl, t // tm),
        in_specs=[_tok(tm, d), _tok(tm, d), _tok(tm, d), _per_example(8, d), _whole(w_in.shape), _whole(ln_g.shape),
                  _whole(ln_b.shape), _whole(w_s.shape), _whole(bias_full.shape), _whole(w_out.shape),
                  _whole(group_ind.shape)],
        out_specs=(_tok(tm, d), _per_example(8, d), _tok(tm, d), _tok(tm, 2 * d), _tok(tm, d), _tok(tm, d),
                   _whole((8, d)), _whole(w_s.shape), _whole((SGU_GROUPS, SGU_CHUNK))),
        out_shape=(_sds(x.shape), _sds((bl, 8, d)), _sds(x.shape, BF16), _sds((bl, t, 2 * d), BF16),
                   _sds(x.shape, BF16), _sds(x.shape, BF16), _sds((8, d)), _sds(w_s.shape),
                   _sds((SGU_GROUPS, SGU_CHUNK))),
        scratch_shapes=[pltpu.VMEM(w_s.shape, BF16), pltpu.VMEM((SGU_CHUNK, d), F32)],
        compiler_params=_cparams(("arbitrary", "arbitrary")),
    )(x, dxo, mix, mod, w_in, ln_g, ln_b, w_s, bias_full, w_out, group_ind)


RWKV_TM = 256
N_VEC = 16


def _rwkv_pre_core(x_ref, halo_ref, mod_ref, vec_ref, ti):
    xn, inv = _rms(x_ref[0])
    scale1, shift = 1.0 + mod_ref[0, 1:2], mod_ref[0, 0:1]
    h = xn * scale1 + shift
    hn, _ = _rms(halo_ref[0])
    hh = hn * scale1 + shift
    first = jnp.where(ti == 0, 0.0, hh[7:8])
    xx = _shift_down(h, first) - h
    xs = [h + xx * vec_ref[i:i + 1] for i in range(6)]
    return xn, inv, xx, xs


def _rwkv_proj(xs, wrkv_ref, w1_ref, a1_ref, g1_ref, w2_ref, a2_ref, g2_ref):
    d = xs[0].shape[1]
    xr, xw, xk, xv, xa, xg = [z.astype(BF16) for z in xs]
    r = jnp.dot(xr, wrkv_ref[:, 0:d], preferred_element_type=F32)
    k = jnp.dot(xk, wrkv_ref[:, d:2 * d], preferred_element_type=F32)
    v = jnp.dot(xv, wrkv_ref[:, 2 * d:3 * d], preferred_element_type=F32)
    tw2 = jnp.tanh(jnp.dot(xw, w1_ref[...], preferred_element_type=F32))
    ta = jnp.dot(xa, a1_ref[...], preferred_element_type=F32)
    sg = _sigmoid(jnp.dot(xg, g1_ref[...], preferred_element_type=F32))
    lw, la, g = _dot(tw2, w2_ref[...]), _dot(ta, a2_ref[...]), _dot(sg, g2_ref[...])
    return (xr, xw, xk, xv, xa, xg), r, k, v, tw2, ta, sg, lw, la, g


def _to_heads(ref, val, nh):
    for hd in range(nh):
        ref[0, hd] = val[:, hd * HEAD:(hd + 1) * HEAD]


def _from_heads(ref, scr, nh):
    for hd in range(nh):
        scr[:, hd * HEAD:(hd + 1) * HEAD] = ref[0, hd]
    return scr[...]


def _rwkv_weight_specs(ws):
    return [_whole(w.shape) for w in ws]


def _rwkv_pre_fwd(x, mod, vec, e_ind, et_ind, weights):
    bl, t, d = x.shape
    tm = min(t, RWKV_TM)
    nh = d // HEAD
    hb = tm // 8

    def body(x_ref, halo_ref, mod_ref, vec_ref, e_ref, et_ref, wrkv, w1, a1, g1, w2, a2, g2,
             r_ref, ld_ref, k2_ref, v_ref, as_ref, bs_ref, g_ref):
        ti = pl.program_id(1)
        _, _, _, xs = _rwkv_pre_core(x_ref, halo_ref, mod_ref, vec_ref, ti)
        _, r, k, v, _, _, _, lw, la, g = _rwkv_proj(xs, wrkv, w1, a1, g1, w2, a2, g2)
        headsum = _make_headsum(e_ref[...], et_ref[...])
        ld, k2, a_s, b_s = _rwkv_elem(r, k, lw, la, vec_ref[6:7], vec_ref[7:8], vec_ref[8:9], vec_ref[9:10], headsum)
        g_ref[0] = g
        for ref, val in ((r_ref, r), (ld_ref, ld), (k2_ref, k2), (v_ref, v), (as_ref, a_s), (bs_ref, b_s)):
            _to_heads(ref, val, nh)

    halo = pl.BlockSpec((1, 8, d), lambda e, i: (e, jnp.maximum(i * hb - 1, 0), 0))
    hs = _sds((bl, nh, t, HEAD))
    return pl.pallas_call(
        body, name="rwkv_pre_fwd", grid=(bl, t // tm),
        in_specs=[_tok(tm, d), halo, _per_example(8, d), _whole(vec.shape), _whole(e_ind.shape), _whole(et_ind.shape)]
        + _rwkv_weight_specs(weights),
        out_specs=(_heads(nh, tm),) * 6 + (_tok(tm, d),),
        out_shape=(hs,) * 6 + (_sds(x.shape),),
        compiler_params=_cparams(("arbitrary", "arbitrary")),
    )(x, x, mod, vec, e_ind, et_ind, *weights)


def _rwkv_post_fwd(x, y, r, k2, v, g, mod, vec, e_ind, et_ind, w_out):
    bl, t, d = x.shape
    tm = min(t, RWKV_TM)
    nh = d // HEAD

    def body(x_ref, y_ref, r_ref, k2_ref, v_ref, g_ref, mod_ref, vec_ref, e_ref, et_ref, wout_ref,
             xo_ref, mix_ref, s0, s1, s2, s3):
        headsum = _make_headsum(e_ref[...], et_ref[...])
        yv, rv, kv, vv = (_from_heads(ref, scr, nh) for ref, scr in
                          ((y_ref, s0), (r_ref, s1), (k2_ref, s2), (v_ref, s3)))
        o = _rwkv_post(yv, rv, kv, vv, g_ref[0], vec_ref[10:11], vec_ref[11:12], vec_ref[12:13], headsum)
        mix = _dot(o, wout_ref[...])
        mix_ref[0] = mix
        xo_ref[0] = x_ref[0] + mod_ref[0, 2:3] * mix

    return pl.pallas_call(
        body, name="rwkv_post_fwd", grid=(bl, t // tm),
        in_specs=[_tok(tm, d)] + [_heads(nh, tm)] * 4 + [_tok(tm, d), _per_example(8, d), _whole(vec.shape),
                                                         _whole(e_ind.shape), _whole(et_ind.shape), _whole(w_out.shape)],
        out_specs=(_tok(tm, d), _tok(tm, d)),
        out_shape=(_sds(x.shape), _sds(x.shape)),
        scratch_shapes=[pltpu.VMEM((tm, d), F32)] * 4,
        compiler_params=_cparams(("arbitrary", "arbitrary")),
    )(x, y, r, k2, v, g, mod, vec, e_ind, et_ind, w_out)


def _rwkv_post_bwd(dxo, mix, y, r, k2, v, g, mod, vec, e_ind, et_ind, w_out):
    bl, t, d = dxo.shape
    tm = min(t, RWKV_TM)
    nh = d // HEAD

    def body(dxo_ref, mix_ref, y_ref, r_ref, k2_ref, v_ref, g_ref, mod_ref, vec_ref, e_ref, et_ref, wout_ref,
             dy_ref, dr_ref, dk2_ref, dv_ref, dg_ref, o_ref, dmix_ref, dgate_ref, small_ref, s0, s1, s2, s3):
        e, ti = pl.program_id(0), pl.program_id(1)
        headsum = _make_headsum(e_ref[...], et_ref[...])
        yv, rv, kv, vv = (_from_heads(ref, scr, nh) for ref, scr in
                          ((y_ref, s0), (r_ref, s1), (k2_ref, s2), (v_ref, s3)))
        dxo = dxo_ref[0]
        dmix = mod_ref[0, 2:3] * dxo
        dmix_ref[0] = dmix.astype(BF16)
        do = _dot_nt(dmix, wout_ref[...])
        post = functools.partial(_rwkv_post, headsum=headsum)
        o, vjp = jax.vjp(post, yv, rv, kv, vv, g_ref[0], vec_ref[10:11], vec_ref[11:12], vec_ref[12:13])
        o_ref[0] = o.astype(BF16)
        dy, dr, dk2, dv, dg, dlng, dlnb, drk = vjp(do)
        _to_heads(dy_ref, dy, nh)
        dr_ref[0], dk2_ref[0], dv_ref[0], dg_ref[0] = dr, dk2, dv, dg
        zero = jnp.zeros((1, d), F32)
        _add_rows(dgate_ref, ti == 0, [zero, zero, _colsum(dxo * mix_ref[0])])

        @pl.when(_first(e, ti))
        def _():
            small_ref[...] = jnp.zeros_like(small_ref)

        small_ref[0:1] += dlng
        small_ref[1:2] += dlnb
        small_ref[2:3] += drk

    return pl.pallas_call(
        body, name="rwkv_post_bwd", grid=(bl, t // tm),
        in_specs=[_tok(tm, d), _tok(tm, d)] + [_heads(nh, tm)] * 4
        + [_tok(tm, d), _per_example(8, d), _whole(vec.shape), _whole(e_ind.shape), _whole(et_ind.shape),
           _whole(w_out.shape)],
        out_specs=(_heads(nh, tm),) + (_tok(tm, d),) * 6 + (_per_example(8, d), _whole((8, d))),
        out_shape=(_sds((bl, nh, t, HEAD)),) + (_sds(dxo.shape),) * 4 + (_sds(dxo.shape, BF16),) * 2
        + (_sds((bl, 8, d)), _sds((8, d))),
        scratch_shapes=[pltpu.VMEM((tm, d), F32)] * 4,
        compiler_params=_cparams(("arbitrary", "arbitrary")),
    )(dxo, mix, y, r, k2, v, g, mod, vec, e_ind, et_ind, w_out)


RWKV_BWD_TM = 128


def _rwkv_pre_bwd(x, mod, vec, e_ind, et_ind, weights, dr_p, dk2_p, dv_p, dg, dr_s, dld, dk2_s, dv_s, das, dbs):
    bl, t, d = x.shape
    tm = min(t, RWKV_BWD_TM)
    nh = d // HEAD
    hb = tm // 8
    lp, gp = LORA_PAD, GATE_PAD

    def body(x_ref, halo_ref, mod_ref, vec_ref, e_ref, et_ref, wrkv, w1, a1, g1, w2, a2, g2,
             drp_ref, dk2p_ref, dvp_ref, dg_ref, drs_ref, dld_ref, dk2s_ref, dvs_ref, das_ref, dbs_ref,
             dh_ref, dhp_ref, xr_ref, xw_ref, xk_ref, xv_ref, xa_ref, xg_ref, dr_ref, dk_ref, dv_ref,
             dtw_ref, dta_ref, dtg_ref, tw2_ref, ta_ref, sg_ref, dlw_ref, dla_ref, dgb_ref, small_ref,
             s0, s1, s2, s3, s4, s5):
        e, ti = pl.program_id(0), pl.program_id(1)
        _, _, xx, xs = _rwkv_pre_core(x_ref, halo_ref, mod_ref, vec_ref, ti)
        xb, r, k, v, tw2, ta, sg, lw, la, _ = _rwkv_proj(xs, wrkv, w1, a1, g1, w2, a2, g2)
        for ref, val in zip((xr_ref, xw_ref, xk_ref, xv_ref, xa_ref, xg_ref), xb):
            ref[0] = val
        headsum = _make_headsum(e_ref[...], et_ref[...])
        drs, dld, dk2s, dvs, das, dbs_ = (_from_heads(ref, scr, nh) for ref, scr in
                                          ((drs_ref, s0), (dld_ref, s1), (dk2s_ref, s2), (dvs_ref, s3),
                                           (das_ref, s4), (dbs_ref, s5)))
        elem = functools.partial(_rwkv_elem, r, headsum=headsum)
        _, vjp = jax.vjp(elem, k, lw, la, vec_ref[6:7], vec_ref[7:8], vec_ref[8:9], vec_ref[9:10])
        dk, dlw, dla, dw0, da0, dkk, dka = vjp((dld, dk2p_ref[0] + dk2s, das, dbs_))
        dr = drp_ref[0] + drs
        dv = dvp_ref[0] + dvs
        dgv = dg_ref[0]
        dtg = _dot_nt(dgv, g2[...]) * sg * (1.0 - sg)
        dtw = _dot_nt(dlw, w2[...]) * (1.0 - tw2 * tw2)
        dta = _dot_nt(dla, a2[...])
        dr_ref[0], dk_ref[0], dv_ref[0] = dr.astype(BF16), dk.astype(BF16), dv.astype(BF16)
        dtw_ref[0], dta_ref[0], dtg_ref[0] = dtw.astype(BF16), dta.astype(BF16), dtg.astype(BF16)
        tw2_ref[0], ta_ref[0], sg_ref[0] = tw2.astype(BF16), ta.astype(BF16), sg.astype(BF16)
        dlw_ref[0], dla_ref[0], dgb_ref[0] = dlw.astype(BF16), dla.astype(BF16), dgv.astype(BF16)
        dxs = (_dot_nt(dr, wrkv[:, 0:d]), _dot_nt(dtw, w1[...]), _dot_nt(dk, wrkv[:, d:2 * d]),
               _dot_nt(dv, wrkv[:, 2 * d:3 * d]), _dot_nt(dta, a1[...]), _dot_nt(dtg, g1[...]))

        @pl.when(_first(e, ti))
        def _():
            small_ref[...] = jnp.zeros_like(small_ref)

        dh = jnp.zeros((tm, d), F32)
        dhp = jnp.zeros((tm, d), F32)
        for i, dxi in enumerate(dxs):
            mu = vec_ref[i:i + 1]
            dh += dxi * (1.0 - mu)
            dhp += dxi * mu
            small_ref[i:i + 1] += _colsum(dxi * xx)
        dh_ref[0], dhp_ref[0] = dh, dhp
        small_ref[6:7] += dw0
        small_ref[7:8] += da0
        small_ref[8:9] += dkk
        small_ref[9:10] += dka

    halo = pl.BlockSpec((1, 8, d), lambda e, i: (e, jnp.maximum(i * hb - 1, 0), 0))
    tokd, tokl, tokg = _tok(tm, d), _tok(tm, lp), _tok(tm, gp)
    bf = lambda w: _sds((bl, t, w), BF16)
    return pl.pallas_call(
        body, name="rwkv_pre_bwd", grid=(bl, t // tm),
        in_specs=[tokd, halo, _per_example(8, d), _whole(vec.shape), _whole(e_ind.shape), _whole(et_ind.shape)]
        + _rwkv_weight_specs(weights) + [tokd] * 4 + [_heads(nh, tm)] * 6,
        out_specs=(tokd, tokd) + (tokd,) * 6 + (tokd,) * 3 + (tokl, tokl, tokg, tokl, tokl, tokg)
        + (tokd, tokd, tokd, _whole((N_VEC, d))),
        out_shape=(_sds(x.shape), _sds(x.shape)) + (bf(d),) * 9 + (bf(lp), bf(lp), bf(gp), bf(lp), bf(lp), bf(gp))
        + (bf(d), bf(d), bf(d), _sds((N_VEC, d))),
        scratch_shapes=[pltpu.VMEM((tm, d), F32)] * 6,
        compiler_params=_cparams(("arbitrary", "arbitrary")),
    )(x, x, mod, vec, e_ind, et_ind, *weights, dr_p, dk2_p, dv_p, dg, dr_s, dld, dk2_s, dv_s, das, dbs)


def _norm_bwd(x, dxo, dh, dhprev, mod, dgate):
    bl, t, d = x.shape
    tm = min(t, RWKV_TM)
    hb = tm // 8
    last_blk = t // 8 - 1

    def body(x_ref, dxo_ref, dh_ref, dhp_ref, nxt_ref, mod_ref, dgate_ref, dx_ref, dmod_ref):
        ti = pl.program_id(1)
        xn, inv = _rms(x_ref[0])
        last = jnp.where(ti == t // tm - 1, 0.0, nxt_ref[0, 0:1])
        dh = dh_ref[0] + _shift_up(dhp_ref[0], last)
        dx_ref[0] = dxo_ref[0] + _rms_bwd(xn, inv, dh * (1.0 + mod_ref[0, 1:2]))

        @pl.when(ti == 0)
        def _():
            dmod_ref[0] = dgate_ref[0]

        dmod_ref[0, 0:1] += _colsum(dh)
        dmod_ref[0, 1:2] += _colsum(dh * xn)

    nxt = pl.BlockSpec((1, 8, d), lambda e, i: (e, jnp.minimum((i + 1) * hb, last_blk), 0))
    return pl.pallas_call(
        body, name="norm_bwd", grid=(bl, t // tm),
        in_specs=[_tok(tm, d)] * 4 + [nxt, _per_example(8, d), _per_example(8, d)],
        out_specs=(_tok(tm, d), _per_example(8, d)),
        out_shape=(_sds(x.shape), _sds((bl, 8, d))),
        compiler_params=_cparams(("arbitrary", "arbitrary")),
    )(x, dxo, dh, dhprev, dhprev, mod, dgate)


def _final(x, target, final_g):
    bl, t, d = x.shape
    tm = min(t, 512)

    def body(x_ref, tgt_ref, g_ref, dx_ref, loss_ref, dg_ref):
        e, ti = pl.program_id(0), pl.program_id(1)

        @pl.when(_first(e, ti))
        def _():
            loss_ref[...] = jnp.zeros_like(loss_ref)
            dg_ref[...] = jnp.zeros_like(dg_ref)

        xn, inv = _rms(x_ref[0])
        err = xn * g_ref[...] - tgt_ref[0]
        loss_ref[...] += (0.5 / d) * jnp.sum(err * err)
        dy = err * (1.0 / d)
        dg_ref[0:1] += _colsum(dy * xn)
        dx_ref[0] = _rms_bwd(xn, inv, dy * g_ref[...])

    return pl.pallas_call(
        body, name="final_loss", grid=(bl, t // tm),
        in_specs=[_tok(tm, d), _tok(tm, d), _whole(final_g.shape)],
        out_specs=(_tok(tm, d), _whole((8, 128)), _whole((8, d))),
        out_shape=(_sds(x.shape), _sds((8, 128)), _sds((8, d))),
        compiler_params=_cparams(("arbitrary", "arbitrary")),
    )(x, target, final_g)


def _adamw_math(w, g, m, v):
    m = ADAM_B1 * m + (1.0 - ADAM_B1) * g
    v = ADAM_B2 * v + (1.0 - ADAM_B2) * jnp.square(g)
    m_hat = m / (1.0 - ADAM_B1 ** ADAM_STEP)
    v_hat = v / (1.0 - ADAM_B2 ** ADAM_STEP)
    return -ADAM_LR * (m_hat / (jnp.sqrt(v_hat) + ADAM_EPS) + ADAM_WD * w), m, v


def _sum_parts(ref, n):
    g = ref[0].astype(F32)
    for s in range(1, n):
        g = g + ref[s].astype(F32)
    return g


def _adamw_rows(w, m, v, parts, row0, name):
    rows, c = w.shape
    n = parts.shape[0]
    tr = 128
    blk0 = row0 // tr

    def body(w_ref, m_ref, v_ref, p_ref, g_ref, d_ref, mo_ref, vo_ref):
        g = _sum_parts(p_ref, n)
        g_ref[...] = g
        d_ref[...], mo_ref[...], vo_ref[...] = _adamw_math(w_ref[...], g, m_ref[...], v_ref[...])

    row = pl.BlockSpec((tr, c), lambda i: (i, 0))
    return pl.pallas_call(
        body, name=name, grid=(rows // tr,),
        in_specs=[row, row, row, pl.BlockSpec((n, tr, c), lambda i: (0, blk0 + i, 0))],
        out_specs=(row,) * 4, out_shape=(_sds(w.shape),) * 4,
        compiler_params=_cparams(("parallel",)),
    )(w, m, v, parts)


def _adamw_small(items, name):
    k = len(items)
    ns = [it[3].shape[0] for it in items]

    def body(*refs):
        ins, outs = refs[:4 * k], refs[4 * k:]
        for i in range(k):
            w_ref, m_ref, v_ref, p_ref = ins[4 * i:4 * i + 4]
            g = _sum_parts(p_ref, ns[i])
            outs[4 * i][...] = g
            outs[4 * i + 1][...], outs[4 * i + 2][...], outs[4 * i + 3][...] = _adamw_math(
                w_ref[...], g, m_ref[...], v_ref[...])

    flat = [a for it in items for a in it]
    res = pl.pallas_call(
        body, name=name,
        out_shape=tuple(_sds(it[0].shape) for it in items for _ in range(4)),
        compiler_params=_cparams(),
    )(*flat)
    return [tuple(res[4 * i:4 * i + 4]) for i in range(k)]


WEIGHTS = ['ada_w', 'ada_b', 'mlp_w1', 'mlp_w2', 'a_w_in', 'a_ln_g', 'a_ln_b', 'a_w_s', 'a_b_s', 'a_w_out', 'b_mu',
           'b_w_in', 'b_w0', 'b_w1', 'b_w2', 'b_a0', 'b_a1', 'b_a2', 'b_g1', 'b_g2', 'b_k_k', 'b_k_a', 'b_r_k',
           'b_ln_g', 'b_ln_b', 'b_w_out', 'final_g']
GATHERED = [('mlp_w1', 2), ('mlp_w2', 1), ('a_w_in', 2), ('a_w_out', 1), ('b_w_in', 2), ('b_w_out', 1),
            ('b_w1', 1), ('b_a1', 1), ('b_g1', 1), ('b_w2', 2), ('b_a2', 2), ('b_g2', 2)]
VECTORS = ['b_mu', 'b_w0', 'b_a0', 'b_k_k', 'b_k_a', 'b_ln_g', 'b_ln_b']
REPLICATED = ['a_ln_g', 'a_ln_b', 'a_w_s', 'a_b_s', 'b_r_k', 'final_g']
ROW_ALIGN = 16


def _pad_rows(a, mult):
    pad = (-a.shape[-2]) % mult
    return jnp.pad(a, [(0, 0)] * (a.ndim - 2) + [(0, pad), (0, 0)]) if pad else a


def _as2d(a):
    if a.ndim == 1:
        return a.reshape(1, -1)
    lead = 1
    for s in a.shape[:-1]:
        lead *= s
    return a.reshape(lead, a.shape[-1])


def kernel(x, c, ada_w, ada_b, mlp_w1, mlp_w2, a_w_in, a_ln_g, a_ln_b, a_w_s, a_b_s, a_w_out, b_mu, b_w_in, b_w0, b_w1, b_w2, b_a0, b_a1, b_a2, b_g1, b_g2, b_k_k, b_k_a, b_r_k, b_ln_g, b_ln_b, b_w_out, final_g, loss_target, m_ada_w, m_ada_b, m_mlp_w1, m_mlp_w2, m_a_w_in, m_a_ln_g, m_a_ln_b, m_a_w_s, m_a_b_s, m_a_w_out, m_b_mu, m_b_w_in, m_b_w0, m_b_w1, m_b_w2, m_b_a0, m_b_a1, m_b_a2, m_b_g1, m_b_g2, m_b_k_k, m_b_k_a, m_b_r_k, m_b_ln_g, m_b_ln_b, m_b_w_out, m_final_g, v_ada_w, v_ada_b, v_mlp_w1, v_mlp_w2, v_a_w_in, v_a_ln_g, v_a_ln_b, v_a_w_s, v_a_b_s, v_a_w_out, v_b_mu, v_b_w_in, v_b_w0, v_b_w1, v_b_w2, v_b_a0, v_b_a1, v_b_a2, v_b_g1, v_b_g2, v_b_k_k, v_b_k_a, v_b_r_k, v_b_ln_g, v_b_ln_b, v_b_w_out, v_final_g):
    given = dict(locals())
    w = {n: given[n] for n in WEIGHTS}
    bl, t, d = x.shape
    nl = ada_w.shape[0]
    nb = N_DEV * bl
    m_tok = bl * t
    me = 4 * lax.axis_index("x") + 2 * lax.axis_index("y") + lax.axis_index("c")

    c_all = _exchange(c, "gather_c", False).reshape(nb, d)
    cols = ada_w.shape[2]
    ada_b_cols = lax.dynamic_slice(ada_b, (0, me * cols), (nl, cols)).reshape(nl, 1, cols)
    mod_cols = _ada_fwd(c_all, ada_w, ada_b_cols)
    mod_full = jnp.moveaxis(_exchange(mod_cols, "gather_mod", False), 0, 2).reshape(nl, nb, 6 * d)
    mod_mine = lax.dynamic_slice(mod_full, (0, me * bl, 0), (nl, bl, 6 * d)).reshape(nl, bl, 6, d)
    mod_mix = jnp.pad(mod_mine[:, :, 0:3], ((0, 0), (0, 0), (0, 5), (0, 0)))
    mod_mlp = jnp.pad(mod_mine[:, :, 3:6], ((0, 0), (0, 0), (0, 5), (0, 0)))

    rows = [w[n].size // d for n, _ in GATHERED]
    offs = [sum(rows[:i]) for i in range(len(rows))]
    n_rows = sum(rows)
    pack = _pad_rows(jnp.concatenate([w[n].reshape(-1, d) for n, _ in GATHERED], axis=0).astype(BF16), ROW_ALIGN)
    gathered = _exchange(pack, "gather_weights", False)
    full = {}
    for (n, ax), off, nr in zip(GATHERED, offs, rows):
        loc = w[n].shape
        g = jnp.moveaxis(gathered[:, off:off + nr].reshape((N_DEV,) + loc), 0, ax)
        full[n] = g.reshape(loc[:ax] + (N_DEV * loc[ax],) + loc[ax + 1:])
    vec_loc = _pad_rows(jnp.concatenate([_as2d(w[n]) for n in VECTORS], axis=0), ROW_ALIGN)
    n_vec_rows = sum(_as2d(w[n]).shape[0] for n in VECTORS)
    vec = jnp.moveaxis(_exchange(vec_loc, "gather_vectors", False), 0, 1).reshape(N_VEC, d)
    vec = vec.at[n_vec_rows].set(b_r_k.reshape(d))

    e_ind, et_ind = _head_indicators(d)
    gd = d // SGU_GROUPS
    group_ind = (jnp.arange(SGU_GROUPS)[:, None] == jnp.arange(d)[None, :] // gd).astype(BF16)
    bias_full = jnp.repeat(a_b_s[0].T, gd, axis=1)
    pad_c = lambda a, n: jnp.pad(a, ((0, 0), (0, n - a.shape[1])))
    pad_r = lambda a, n: jnp.pad(a, ((0, n - a.shape[0]), (0, 0)))
    rwkv_w = (full['b_w_in'][0], pad_c(full['b_w1'][0], LORA_PAD), pad_c(full['b_a1'][0], LORA_PAD),
              pad_c(full['b_g1'][0], GATE_PAD), pad_r(full['b_w2'][0], LORA_PAD), pad_r(full['b_a2'][0], LORA_PAD),
              pad_r(full['b_g2'][0], GATE_PAD))
    sgu_args = (full['a_w_in'][0], a_ln_g, a_ln_b, a_w_s[0], bias_full, full['a_w_out'][0])

    x0 = x
    x1, mix_a = _sgu_fwd(x0, mod_mix[0], *sgu_args)
    x2, ff0 = _mlp_fwd(x1, mod_mlp[0], full['mlp_w1'][0], full['mlp_w2'][0])
    r, ld, k2, v, a_s, b_s, gate = _rwkv_pre_fwd(x2, mod_mix[1], vec, e_ind, et_ind, rwkv_w)
    y, s0 = _wkv_fwd(r, ld, k2, v, a_s, b_s)
    x3, mix_b = _rwkv_post_fwd(x2, y, r, k2, v, gate, mod_mix[1], vec, e_ind, et_ind, full['b_w_out'][0])
    x4, ff1 = _mlp_fwd(x3, mod_mlp[1], full['mlp_w1'][1], full['mlp_w2'][1])
    dx4, loss_blk, dfinal = _final(x4, loss_target, final_g.reshape(1, d))
    loss = lax.psum(loss_blk[0, 0], ("x", "y", "c"))

    tok = lambda a: a.reshape(m_tok, a.shape[-1])
    grads = {}
    dx3, dmod_mlp1, h_b, dff_b, q_b, dp_b = _mlp_bwd(x3, dx4, ff1, mod_mlp[1], full['mlp_w1'][1], full['mlp_w2'][1])
    gw1_1 = _matmul_tn(tok(h_b), tok(dp_b), "grad_mlp_w1_l1")
    gw2_1 = _matmul_tn(tok(q_b), tok(dff_b), "grad_mlp_w2_l1")
    dy, dr_p, dk2_p, dv_p, dgate_act, o_b, dmix_b, dgate_b, small_post = _rwkv_post_bwd(
        dx3, mix_b, y, r, k2, v, gate, mod_mix[1], vec, e_ind, et_ind, full['b_w_out'][0])
    grads['b_w_out'] = _matmul_tn(tok(o_b), tok(dmix_b), "grad_b_w_out")[None]
    dr_s, dld, dk2_s, dv_s, das, dbs = _wkv_bwd(r, ld, k2, v, a_s, b_s, s0, dy)
    (dh, dhp, xr_b, xw_b, xk_b, xv_b, xa_b, xg_b, dr_b, dk_b, dv_b, dtw_b, dta_b, dtg_b, tw2_b, ta_b, sg_b,
     dlw_b, dla_b, dg_b, small_pre) = _rwkv_pre_bwd(x2, mod_mix[1], vec, e_ind, et_ind, rwkv_w,
                                                    dr_p, dk2_p, dv_p, dgate_act, dr_s, dld, dk2_s, dv_s, das, dbs)
    grads['b_w_in'] = jnp.concatenate([_matmul_tn(tok(xr_b), tok(dr_b), "grad_b_w_r"),
                                       _matmul_tn(tok(xk_b), tok(dk_b), "grad_b_w_k"),
                                       _matmul_tn(tok(xv_b), tok(dv_b), "grad_b_w_v")], axis=1)[None]
    lw_, lg_ = b_w1.shape[2], b_g1.shape[2]
    grads['b_w1'] = _matmul_tn(tok(xw_b), tok(dtw_b), "grad_b_w1")[None, :, :lw_]
    grads['b_a1'] = _matmul_tn(tok(xa_b), tok(dta_b), "grad_b_a1")[None, :, :lw_]
    grads['b_g1'] = _matmul_tn(tok(xg_b), tok(dtg_b), "grad_b_g1")[None, :, :lg_]
    grads['b_w2'] = _matmul_tn(tok(tw2_b), tok(dlw_b), "grad_b_w2")[None, :lw_]
    grads['b_a2'] = _matmul_tn(tok(ta_b), tok(dla_b), "grad_b_a2")[None, :lw_]
    grads['b_g2'] = _matmul_tn(tok(sg_b), tok(dg_b), "grad_b_g2")[None, :lg_]
    dx2, dmod_mix1 = _norm_bwd(x2, dx3, dh, dhp, mod_mix[1], dgate_b)
    dx1, dmod_mlp0, h_b, dff_b, q_b, dp_b = _mlp_bwd(x1, dx2, ff0, mod_mlp[0], full['mlp_w1'][0], full['mlp_w2'][0])
    gw1_0 = _matmul_tn(tok(h_b), tok(dp_b), "grad_mlp_w1_l0")
    gw2_0 = _matmul_tn(tok(q_b), tok(dff_b), "grad_mlp_w2_l0")
    grads['mlp_w1'] = jnp.stack([gw1_0, gw1_1])
    grads['mlp_w2'] = jnp.stack([gw2_0, gw2_1])
    dx0, dmod_mix0, h_b, dpre_b, z_b, dmix_b, small_sgu, d_ws, d_bs = _sgu_bwd(
        x0, dx1, mix_a, mod_mix[0], *sgu_args, group_ind)
    grads['a_w_in'] = _matmul_tn(tok(h_b), tok(dpre_b), "grad_a_w_in")[None]
    grads['a_w_out'] = _matmul_tn(tok(z_b), tok(dmix_b), "grad_a_w_out")[None]

    dmod_mine = jnp.stack([jnp.concatenate([dmod_mix0[:, 0:3], dmod_mlp0[:, 0:3]], axis=1),
                           jnp.concatenate([dmod_mix1[:, 0:3], dmod_mlp1[:, 0:3]], axis=1)], axis=1)
    dmod_all = _exchange(dmod_mine.reshape(bl, nl * 6 * d), "gather_dmod", False)
    dmod_all = jnp.moveaxis(dmod_all.reshape(nb, nl, 6 * d), 0, 1)
    dmod_cols = lax.dynamic_slice(dmod_all, (0, 0, me * cols), (nl, nb, cols))
    g_ada_w, g_ada_b = _ada_bwd(c_all, dmod_cols, dmod_all)

    vec_g = jnp.concatenate([small_pre[0:10], small_post[0:2]], axis=0)
    vec_parts = jnp.moveaxis(vec_g.reshape(n_vec_rows, N_DEV, d // N_DEV), 1, 0).reshape(N_DEV, -1)
    vec_rows = -(-vec_parts.shape[1] // d)
    vec_parts = jnp.pad(vec_parts, ((0, 0), (0, vec_rows * d - vec_parts.shape[1]))).reshape(N_DEV, vec_rows, d)
    vec_parts = vec_parts.astype(BF16)
    parts = []
    for n, ax in GATHERED:
        loc = w[n].shape
        g = grads[n].reshape(loc[:ax] + (N_DEV, loc[ax]) + loc[ax + 1:])
        parts.append(jnp.moveaxis(g, ax, 0).reshape(N_DEV, -1, d))
    gpack = _pad_rows(jnp.concatenate(parts + [vec_parts], axis=1), ROW_ALIGN)
    recv = _exchange(gpack, "scatter_grads", True)

    rep_g = {'a_ln_g': small_sgu[0:1], 'a_ln_b': small_sgu[1:2], 'a_w_s': d_ws.reshape(-1, d), 'a_b_s': d_bs.reshape(1, d),
             'b_r_k': small_post[2:3], 'final_g': dfinal[0:1]}
    rep_rows = [rep_g[n].shape[0] for n in REPLICATED]
    rep_pack = _pad_rows(jnp.concatenate([rep_g[n] for n in REPLICATED], axis=0), 8)
    rep_all = _exchange(rep_pack, "gather_replicated_grads", False)

    mom = {n: given['m_' + n] for n in WEIGHTS}
    var = {n: given['v_' + n] for n in WEIGHTS}
    out = {}
    big = GATHERED[:6]
    for (n, _), off in zip(big, offs):
        res = _adamw_rows(w[n].reshape(-1, d), mom[n].reshape(-1, d), var[n].reshape(-1, d), recv, off, "adamw_" + n)
        out[n] = tuple(a.reshape(w[n].shape) for a in res)
    res = _adamw_rows(ada_w.reshape(-1, d), m_ada_w.reshape(-1, d), v_ada_w.reshape(-1, d),
                      g_ada_w.reshape(1, -1, d), 0, "adamw_ada_w")
    out['ada_w'] = tuple(a.reshape(ada_w.shape) for a in res)

    items, names = [], []

    def add(n, part):
        s2 = _as2d(w[n]).shape
        items.append((_as2d(w[n]), _as2d(mom[n]), _as2d(var[n]), part.reshape((part.shape[0],) + s2)))
        names.append(n)

    for (n, _), off, nr in list(zip(GATHERED, offs, rows))[6:]:
        add(n, recv[:, off:off + nr])
    vflat = recv[:, n_rows:n_rows + vec_rows].reshape(N_DEV, -1)
    vo = 0
    for n in VECTORS:
        sz = w[n].size
        add(n, vflat[:, vo:vo + sz])
        vo += sz
    ro = 0
    for n, nr in zip(REPLICATED, rep_rows):
        add(n, rep_all[:, ro:ro + nr])
        ro += nr
    add('ada_b', g_ada_b[None])
    for n, res in zip(names, _adamw_small(items, "adamw_small")):
        out[n] = tuple(a.reshape(w[n].shape) for a in res)

    return (loss, dx0, *[out[n][0] for n in WEIGHTS], *[out[n][1] for n in WEIGHTS],
            *[out[n][2] for n in WEIGHTS], *[out[n][3] for n in WEIGHTS])
```

```python
import functools

import jax
import jax.numpy as jnp
from jax import lax
from jax.experimental import pallas as pl
from jax.experimental.pallas import tpu as pltpu

F32 = jnp.float32
BF16 = jnp.bfloat16

N_DEV = 8
RMS_EPS = 1e-6
LN_EPS = 1e-5
HEAD = 64
GN_EPS = HEAD * 1e-5
L2_EPS = 1e-12
SGU_CHUNK = 128
SGU_GROUPS = 8
WKV_CHUNK = 64
WKV_HEADS_PER_STEP = 8
LORA_PAD = 128
GATE_PAD = 256
ADAM_LR, ADAM_B1, ADAM_B2, ADAM_EPS, ADAM_WD, ADAM_STEP = 0.001, 0.9, 0.999, 1e-08, 0.01, 10
VMEM_LIMIT = 56 * 1024 * 1024


def _cparams(sem=None, **kw):
    if sem is not None:
        kw["dimension_semantics"] = sem
    return pltpu.CompilerParams(vmem_limit_bytes=VMEM_LIMIT, **kw)


def _dot(a, b):
    return jnp.dot(a.astype(BF16), b.astype(BF16), preferred_element_type=F32)


def _dot_nt(a, b):
    return lax.dot_general(a.astype(BF16), b.astype(BF16), (((1,), (1,)), ((), ())), preferred_element_type=F32)


def _dot_tn(a, b):
    return lax.dot_general(a.astype(BF16), b.astype(BF16), (((0,), (0,)), ((), ())), preferred_element_type=F32)


def _bdot(a, b, dims):
    return lax.dot_general(a.astype(BF16), b.astype(BF16), (dims, ((0,), (0,))), preferred_element_type=F32)


@jax.custom_vjp
def _tri_sum(tri, tri_t, x):
    hi = x.astype(BF16)
    lo = (x - hi.astype(F32)).astype(BF16)
    dn = (((2,), (1,)), ((0,), (0,)))
    return (lax.dot_general(tri, hi, dn, preferred_element_type=F32)
            + lax.dot_general(tri, lo, dn, preferred_element_type=F32))


_tri_sum.defvjp(lambda tri, tri_t, x: (_tri_sum(tri, tri_t, x), (tri, tri_t)),
                lambda res, g: (jnp.zeros_like(res[0]), jnp.zeros_like(res[1]), _tri_sum(res[1], res[0], g)))


@jax.custom_vjp
def _bmm_nn(a, b):
    return _bdot(a, b, ((2,), (1,)))


@jax.custom_vjp
def _bmm_nt(a, b):
    return _bdot(a, b, ((2,), (2,)))


@jax.custom_vjp
def _bmm_tn(a, b):
    return _bdot(a, b, ((1,), (1,)))


_bmm_nn.defvjp(lambda a, b: (_bmm_nn(a, b), (a, b)), lambda res, g: (_bmm_nt(g, res[1]), _bmm_tn(res[0], g)))
_bmm_nt.defvjp(lambda a, b: (_bmm_nt(a, b), (a, b)), lambda res, g: (_bmm_nn(g, res[1]), _bmm_tn(g, res[0])))
_bmm_tn.defvjp(lambda a, b: (_bmm_tn(a, b), (a, b)), lambda res, g: (_bmm_nt(res[1], g), _bmm_nn(res[0], g)))


def _wkv_chunk(s0, r, ld, k, v, a, b):
    nh, n, _ = r.shape
    row = lax.broadcasted_iota(jnp.int32, (n, n), 0)
    col = lax.broadcasted_iota(jnp.int32, (n, n), 1)
    incl = row >= col
    strict = row > col
    lower = jnp.broadcast_to(jnp.where(incl, 1.0, 0.0).astype(BF16), (nh, n, n))
    upper = jnp.broadcast_to(jnp.where(row <= col, 1.0, 0.0).astype(BF16), (nh, n, n))
    eye = jnp.where(row == col, 1.0, 0.0).astype(F32)[None]
    c = _tri_sum(lower, upper, ld)
    c_end = c[:, n - 1:n, :]
    ec, enc, ecx, eend = jnp.exp(c), jnp.exp(-c), jnp.exp(c - ld), jnp.exp(c_end - c)
    ar = jnp.concatenate([a * ecx, r * ec], axis=1)
    mask = jnp.concatenate([strict, incl], axis=0)[None]
    m_b = jnp.where(mask, _bmm_nt(ar, b * enc), 0.0)
    m_k = jnp.where(mask, _bmm_nt(ar, k * enc), 0.0)
    a_ab, a_rb = m_b[:, :n], m_b[:, n:]
    p = a_ab
    tinv = eye + p
    steps = max(1, (n - 1).bit_length()) - 1
    for _ in range(steps):
        p = _bmm_nn(p, p)
        tinv = tinv + _bmm_nn(tinv, p)
    base = _bmm_nt(ar, s0) + _bmm_nn(m_k, v)
    u = _bmm_nn(tinv, base[:, :n])
    y = base[:, n:] + _bmm_nn(a_rb, u)
    s1 = s0 * jnp.exp(c_end) + _bmm_tn(jnp.concatenate([u, v], axis=1), jnp.concatenate([b * eend, k * eend], axis=1))
    return y, s1


def _wkv_specs(bl, nh, t):
    hb, lc = WKV_HEADS_PER_STEP, WKV_CHUNK
    return hb, lc, (bl, nh // hb, t // lc)


def _wkv_fwd(r, ld, k, v, a, b):
    bl, nh, t, n = r.shape
    hb, lc, grid = _wkv_specs(bl, nh, t)
    nc = t // lc

    def body(r_ref, ld_ref, k_ref, v_ref, a_ref, b_ref, y_ref, s0_ref, s_scr):
        @pl.when(pl.program_id(2) == 0)
        def _():
            s_scr[...] = jnp.zeros_like(s_scr)

        s0 = s_scr[...]
        s0_ref[0, :, 0] = s0
        y, s1 = _wkv_chunk(s0, r_ref[0], ld_ref[0], k_ref[0], v_ref[0], a_ref[0], b_ref[0])
        y_ref[0] = y
        s_scr[...] = s1

    seq = pl.BlockSpec((1, hb, lc, n), lambda e, h, c: (e, h, c, 0))
    return pl.pallas_call(
        body, name="wkv_fwd", grid=grid,
        in_specs=[seq] * 6,
        out_specs=(seq, pl.BlockSpec((1, hb, 1, n, n), lambda e, h, c: (e, h, c, 0, 0))),
        out_shape=(jax.ShapeDtypeStruct((bl, nh, t, n), F32), jax.ShapeDtypeStruct((bl, nh, nc, n, n), F32)),
        scratch_shapes=[pltpu.VMEM((hb, n, n), F32)],
        compiler_params=_cparams(("arbitrary", "arbitrary", "arbitrary")),
    )(r, ld, k, v, a, b)


def _wkv_bwd(r, ld, k, v, a, b, s0_all, dy):
    bl, nh, t, n = r.shape
    hb, lc, grid = _wkv_specs(bl, nh, t)
    nc = t // lc

    def body(r_ref, ld_ref, k_ref, v_ref, a_ref, b_ref, s0_ref, dy_ref,
             dr_ref, dld_ref, dk_ref, dv_ref, da_ref, db_ref, ds_scr):
        @pl.when(pl.program_id(2) == 0)
        def _():
            ds_scr[...] = jnp.zeros_like(ds_scr)

        args = (s0_ref[0, :, 0], r_ref[0], ld_ref[0], k_ref[0], v_ref[0], a_ref[0], b_ref[0])
        _, vjp = jax.vjp(_wkv_chunk, *args)
        ds0, dr, dld, dk, dv, da, db = vjp((dy_ref[0], ds_scr[...]))
        ds_scr[...] = ds0
        dr_ref[0], dld_ref[0], dk_ref[0], dv_ref[0], da_ref[0], db_ref[0] = dr, dld, dk, dv, da, db

    seq = pl.BlockSpec((1, hb, lc, n), lambda e, h, c: (e, h, nc - 1 - c, 0))
    st = pl.BlockSpec((1, hb, 1, n, n), lambda e, h, c: (e, h, nc - 1 - c, 0, 0))
    out = jax.ShapeDtypeStruct((bl, nh, t, n), F32)
    return pl.pallas_call(
        body, name="wkv_bwd", grid=grid,
        in_specs=[seq] * 6 + [st, seq],
        out_specs=(seq,) * 6, out_shape=(out,) * 6,
        scratch_shapes=[pltpu.VMEM((hb, n, n), F32)],
        compiler_params=_cparams(("arbitrary", "arbitrary", "arbitrary")),
    )(r, ld, k, v, a, b, s0_all, dy)


def _exchange(x, name, scatter, relay=False):
    assert not (relay and scatter)
    blk = x.shape[1:] if scatter else x.shape

    def body(x_ref, o_ref, send_sems, recv_sems, local_sem):
        pos = (lax.axis_index("x"), lax.axis_index("y"), lax.axis_index("c"))
        me = 4 * pos[0] + 2 * pos[1] + pos[2]

        def peer_of(m):
            p = tuple(1 - pos[i] if (m >> (2 - i)) & 1 else pos[i] for i in range(3))
            return p, 4 * p[0] + 2 * p[1] + p[2]

        def copy(m):
            p, pidx = peer_of(m)
            return pltpu.make_async_remote_copy(
                src_ref=x_ref.at[pidx] if scatter else x_ref, dst_ref=o_ref.at[me],
                send_sem=send_sems.at[m - 1], recv_sem=recv_sems.at[m - 1],
                device_id=p, device_id_type=pl.DeviceIdType.MESH)

        def arrival(m):
            p, pidx = peer_of(m)
            return pltpu.make_async_remote_copy(
                src_ref=x_ref.at[pidx] if scatter else x_ref, dst_ref=o_ref.at[pidx],
                send_sem=send_sems.at[m - 1], recv_sem=recv_sems.at[m - 1],
                device_id=p, device_id_type=pl.DeviceIdType.MESH)

        mine = pltpu.make_async_copy(x_ref.at[me] if scatter else x_ref, o_ref.at[me], local_sem)
        mine.start()
        if not relay:
            sends = [copy(m) for m in range(1, N_DEV)]
            for cp in sends:
                cp.start()
            for m in range(1, N_DEV):
                arrival(m).wait_recv()
        else:
            sibling, _ = peer_of(1)
            far = (2, 4, 6)

            def relay_copy(m, origin_idx):
                return pltpu.make_async_remote_copy(
                    src_ref=o_ref.at[origin_idx], dst_ref=o_ref.at[origin_idx],
                    send_sem=send_sems.at[m], recv_sem=recv_sems.at[m],
                    device_id=sibling, device_id_type=pl.DeviceIdType.MESH)

            sends = [copy(1)] + [copy(m) for m in far]
            for cp in sends:
                cp.start()
            for m in far:
                arrival(m).wait_recv()
                fwd = relay_copy(m, peer_of(m)[1])
                fwd.start()
                sends.append(fwd)
            arrival(1).wait_recv()
            for m in far:
                relay_copy(m, peer_of(m ^ 1)[1]).wait_recv()
        for cp in sends:
            cp.wait_send()
        mine.wait()

    return pl.pallas_call(
        body, name=name,
        out_shape=jax.ShapeDtypeStruct((N_DEV,) + tuple(blk), x.dtype),
        in_specs=[pl.BlockSpec(memory_space=pl.ANY)],
        out_specs=pl.BlockSpec(memory_space=pl.ANY),
        scratch_shapes=[pltpu.SemaphoreType.DMA((N_DEV - 1,)), pltpu.SemaphoreType.DMA((N_DEV - 1,)),
                        pltpu.SemaphoreType.DMA],
    )(x)


def _rms(x):
    inv = lax.rsqrt(jnp.mean(x * x, axis=-1, keepdims=True) + RMS_EPS)
    return x * inv, inv


def _rms_bwd(xn, inv, dxn):
    return inv * (dxn - xn * jnp.mean(dxn * xn, axis=-1, keepdims=True))


def _colsum(x):
    return jnp.sum(x, axis=0, keepdims=True)


def _sigmoid(x):
    return 0.5 * (jnp.tanh(0.5 * x) + 1.0)


def _split_bf16(x):
    hi = x.astype(BF16)
    return hi, (x - hi.astype(F32)).astype(BF16)


def _dot_split(x, e):
    hi, lo = _split_bf16(x)
    return jnp.dot(hi, e, preferred_element_type=F32) + jnp.dot(lo, e, preferred_element_type=F32)


@jax.custom_vjp
def _headsum(x, e, et):
    return _dot_split(_dot_split(x, e), et)


_headsum.defvjp(lambda x, e, et: (_headsum(x, e, et), (e, et)),
                lambda res, g: (_headsum(g, *res), jnp.zeros_like(res[0]), jnp.zeros_like(res[1])))


def _make_headsum(e, et):
    return lambda x: _headsum(x, e, et)


def _head_indicators(d):
    e = (jnp.arange(d)[:, None] // HEAD == jnp.arange(128)[None, :]).astype(BF16)
    return e, e.T


def _rwkv_elem(r, k, lw, la, w0, a0, k_k, k_a, headsum):
    z = w0 + lw
    w_log = -(jnp.maximum(-z, 0.0) + jnp.log(1.0 + jnp.exp(-jnp.abs(z)))) - 0.5
    ld = -jnp.exp(w_log)
    a = _sigmoid(a0 + la)
    kkp = k * k_k
    kk = kkp / jnp.maximum(jnp.sqrt(headsum(kkp * kkp)), L2_EPS)
    k2 = k * (1.0 + (a - 1.0) * k_a)
    del r
    return ld, k2, -kk, kk * a


def _rwkv_post(y, r, k2, v, g, ln_g, ln_b, r_k, headsum):
    m = headsum(y) * (1.0 / HEAD)
    yc = y - m
    var = headsum(yc * yc) * (1.0 / HEAD)
    yn = yc * lax.rsqrt(var + GN_EPS)
    bonus = headsum(r * k2 * r_k) * v
    return (yn * ln_g + ln_b + bonus) * g


def _shift_down(h, first_row):
    rolled = pltpu.roll(h, 1, 0)
    row = lax.broadcasted_iota(jnp.int32, h.shape, 0)
    return jnp.where(row == 0, first_row, rolled)


def _shift_up(h, last_row):
    n = h.shape[0]
    rolled = pltpu.roll(h, n - 1, 0)
    row = lax.broadcasted_iota(jnp.int32, h.shape, 0)
    return jnp.where(row == n - 1, last_row, rolled)


def _gelu(p):
    return 0.5 * p * (1.0 + lax.erf(p * 0.7071067811865476))


def _gelu_grad(p):
    return 0.5 * (1.0 + lax.erf(p * 0.7071067811865476)) + p * jnp.exp(-0.5 * p * p) * 0.3989422804014327


def _tok(tm, d):
    return pl.BlockSpec((1, tm, d), lambda e, t, *_: (e, t, 0))


def _per_example(rows, d):
    return pl.BlockSpec((1, rows, d), lambda e, t, *_: (e, 0, 0))


def _whole(shape):
    nd = len(shape)
    return pl.BlockSpec(tuple(shape), lambda *_: (0,) * nd)


def _heads(nh, tm):
    return pl.BlockSpec((1, nh, tm, HEAD), lambda e, t, *_: (e, 0, t, 0))


def _sds(shape, dtype=F32):
    return jax.ShapeDtypeStruct(tuple(shape), dtype)


def _add_rows(ref, first, rows):
    @pl.when(first)
    def _():
        ref[0] = jnp.zeros(ref.shape[1:], ref.dtype)

    for i, r in enumerate(rows):
        ref[0, i:i + 1] += r


def _first(e, t):
    return jnp.logical_and(e == 0, t == 0)


def _ada_fwd(c_all, ada_w, ada_b_cols):
    nl, d, cols = ada_w.shape
    nb = c_all.shape[0]

    def body(c_ref, w_ref, b_ref, o_ref):
        c = c_ref[...]
        cond = c * _sigmoid(c)
        for i in range(nl):
            o_ref[i] = _dot(cond, w_ref[i]) + b_ref[i]

    return pl.pallas_call(
        body, name="ada_fwd", out_shape=_sds((nl, nb, cols)),
        compiler_params=_cparams(),
    )(c_all, ada_w, ada_b_cols)


def _ada_bwd(c_all, dmod_cols, dmod_full):
    nl, nb, cols = dmod_cols.shape
    d = c_all.shape[1]

    def body(c_ref, g_ref, f_ref, o_ref, b_ref):
        c = c_ref[...]
        cond = c * _sigmoid(c)
        for i in range(nl):
            o_ref[i] = _dot_tn(cond, g_ref[i])
            b_ref[i:i + 1] = jnp.sum(f_ref[i], axis=0, keepdims=True)

    return pl.pallas_call(
        body, name="ada_bwd", out_shape=(_sds((nl, d, cols)), _sds((nl, dmod_full.shape[2]))),
        compiler_params=_cparams(),
    )(c_all, dmod_cols, dmod_full)


def _matmul_tn(a, b, name):
    m, ka = a.shape
    n = b.shape[1]
    tm = min(m, 512)
    tk = min(ka, 1024)
    tn = min(n, 512)
    steps = m // tm

    def body(a_ref, b_ref, o_ref, acc):
        s = pl.program_id(2)

        @pl.when(s == 0)
        def _():
            acc[...] = jnp.zeros_like(acc)

        acc[...] += _dot_tn(a_ref[...], b_ref[...])

        @pl.when(s == steps - 1)
        def _():
            o_ref[...] = acc[...].astype(BF16)

    return pl.pallas_call(
        body, name=name, grid=(ka // tk, n // tn, steps),
        in_specs=[pl.BlockSpec((tm, tk), lambda i, j, s: (s, i)), pl.BlockSpec((tm, tn), lambda i, j, s: (s, j))],
        out_specs=pl.BlockSpec((tk, tn), lambda i, j, s: (i, j)),
        out_shape=_sds((ka, n), BF16),
        scratch_shapes=[pltpu.VMEM((tk, tn), F32)],
        compiler_params=_cparams(("parallel", "parallel", "arbitrary")),
    )(a, b)


MLP_TM = 512
MLP_FJ = 1024


def _mlp_fwd(x, mod, w1, w2):
    bl, t, d = x.shape
    f = w1.shape[1]
    tm, fj = min(t, MLP_TM), min(f, MLP_FJ)
    nj = f // fj

    def body(x_ref, mod_ref, w1_ref, w2_ref, xo_ref, ff_ref, h_scr, acc):
        j = pl.program_id(2)

        @pl.when(j == 0)
        def _():
            xn, _ = _rms(x_ref[0])
            h_scr[...] = (xn * (1.0 + mod_ref[0, 1:2]) + mod_ref[0, 0:1]).astype(BF16)
            acc[...] = jnp.zeros_like(acc)

        p = jnp.dot(h_scr[...], w1_ref[...], preferred_element_type=F32)
        q = jnp.square(jnp.maximum(p, 0.0))
        acc[...] += _dot(q, w2_ref[...])

        @pl.when(j == nj - 1)
        def _():
            ff_ref[0] = acc[...]
            xo_ref[0] = x_ref[0] + mod_ref[0, 2:3] * acc[...]

    return pl.pallas_call(
        body, name="mlp_fwd", grid=(bl, t // tm, nj),
        in_specs=[_tok(tm, d), _per_example(8, d),
                  pl.BlockSpec((d, fj), lambda e, i, j: (0, j)), pl.BlockSpec((fj, d), lambda e, i, j: (j, 0))],
        out_specs=(_tok(tm, d), _tok(tm, d)),
        out_shape=(_sds(x.shape), _sds(x.shape)),
        scratch_shapes=[pltpu.VMEM((tm, d), BF16), pltpu.VMEM((tm, d), F32)],
        compiler_params=_cparams(("arbitrary", "arbitrary", "arbitrary")),
    )(x, mod, w1, w2)


def _mlp_bwd(x, dxo, ff, mod, w1, w2):
    bl, t, d = x.shape
    f = w1.shape[1]
    tm, fj = min(t, MLP_TM), min(f, MLP_FJ)
    nj = f // fj

    def body(x_ref, dxo_ref, ff_ref, mod_ref, w1_ref, w2_ref,
             dx_ref, dmod_ref, h_ref, dff_ref, q_ref, dp_ref, acc):
        ti, j = pl.program_id(1), pl.program_id(2)

        @pl.when(j == 0)
        def _():
            xn, _ = _rms(x_ref[0])
            h_ref[0] = (xn * (1.0 + mod_ref[0, 1:2]) + mod_ref[0, 0:1]).astype(BF16)
            dff_ref[0] = (mod_ref[0, 2:3] * dxo_ref[0]).astype(BF16)
            acc[...] = jnp.zeros_like(acc)

        p = jnp.dot(h_ref[0], w1_ref[...], preferred_element_type=F32)
        rl = jnp.maximum(p, 0.0)
        q_ref[0] = jnp.square(rl).astype(BF16)
        dp = (_dot_nt(dff_ref[0], w2_ref[...]) * (2.0 * rl)).astype(BF16)
        dp_ref[0] = dp
        acc[...] += _dot_nt(dp, w1_ref[...])

        @pl.when(j == nj - 1)
        def _():
            xn, inv = _rms(x_ref[0])
            dh = acc[...]
            dx_ref[0] = dxo_ref[0] + _rms_bwd(xn, inv, dh * (1.0 + mod_ref[0, 1:2]))
            _add_rows(dmod_ref, ti == 0, [_colsum(dh), _colsum(dh * xn), _colsum(dxo_ref[0] * ff_ref[0])])

    big = lambda: pl.BlockSpec((1, tm, fj), lambda e, i, j: (e, i, j))
    return pl.pallas_call(
        body, name="mlp_bwd", grid=(bl, t // tm, nj),
        in_specs=[_tok(tm, d), _tok(tm, d), _tok(tm, d), _per_example(8, d),
                  pl.BlockSpec((d, fj), lambda e, i, j: (0, j)), pl.BlockSpec((fj, d), lambda e, i, j: (j, 0))],
        out_specs=(_tok(tm, d), _per_example(8, d), _tok(tm, d), _tok(tm, d), big(), big()),
        out_shape=(_sds(x.shape), _sds((bl, 8, d)), _sds(x.shape, BF16), _sds(x.shape, BF16),
                   _sds((bl, t, f), BF16), _sds((bl, t, f), BF16)),
        scratch_shapes=[pltpu.VMEM((tm, d), F32)],
        compiler_params=_cparams(("arbitrary", "arbitrary", "arbitrary")),
    )(x, dxo, ff, mod, w1, w2)


SGU_TM = 256


def _sgu_core(x, mod_ref, win_ref, lng, lnb, ws_ref, bias_ref):
    tm, d = x.shape
    xn, inv = _rms(x)
    h = (xn * (1.0 + mod_ref[0, 1:2]) + mod_ref[0, 0:1]).astype(BF16)
    pre = jnp.dot(h, win_ref[...], preferred_element_type=F32)
    uv = _gelu(pre)
    u, v = uv[:, :d], uv[:, d:]
    mu = jnp.mean(v, axis=-1, keepdims=True)
    vc = v - mu
    rstd = lax.rsqrt(jnp.mean(vc * vc, axis=-1, keepdims=True) + LN_EPS)
    vhat = vc * rstd
    vln = vhat * lng + lnb
    gd = d // SGU_GROUPS
    rows = []
    for c in range(tm // SGU_CHUNK):
        cols = []
        for g in range(SGU_GROUPS):
            cols.append(_dot(ws_ref[g], vln[c * SGU_CHUNK:(c + 1) * SGU_CHUNK, g * gd:(g + 1) * gd]))
        rows.append(jnp.concatenate(cols, axis=1) + bias_ref[...])
    sv = jnp.concatenate(rows, axis=0)
    return xn, inv, h, pre, u, vhat, rstd, vln, sv


def _sgu_masked(ws_ref, wm_scr):
    row = lax.broadcasted_iota(jnp.int32, (SGU_CHUNK, SGU_CHUNK), 0)
    col = lax.broadcasted_iota(jnp.int32, (SGU_CHUNK, SGU_CHUNK), 1)
    for g in range(SGU_GROUPS):
        wm_scr[g] = jnp.where(row >= col, ws_ref[g], 0.0).astype(BF16)


def _sgu_fwd(x, mod, w_in, ln_g, ln_b, w_s, bias_full, w_out):
    bl, t, d = x.shape
    tm = min(t, SGU_TM)

    def body(x_ref, mod_ref, win_ref, lng_ref, lnb_ref, ws_ref, bias_ref, wout_ref, xo_ref, mix_ref, wm_scr):
        _sgu_masked(ws_ref, wm_scr)
        xt = x_ref[0]
        *_, u, _, _, _, sv = _sgu_core(xt, mod_ref, win_ref, lng_ref[...], lnb_ref[...], wm_scr, bias_ref)
        mix = _dot(u * sv, wout_ref[...])
        mix_ref[0] = mix
        xo_ref[0] = xt + mod_ref[0, 2:3] * mix

    return pl.pallas_call(
        body, name="sgu_fwd", grid=(bl, t // tm),
        in_specs=[_tok(tm, d), _per_example(8, d), _whole(w_in.shape), _whole(ln_g.shape), _whole(ln_b.shape),
                  _whole(w_s.shape), _whole(bias_full.shape), _whole(w_out.shape)],
        out_specs=(_tok(tm, d), _tok(tm, d)),
        out_shape=(_sds(x.shape), _sds(x.shape)),
        scratch_shapes=[pltpu.VMEM(w_s.shape, BF16)],
        compiler_params=_cparams(("arbitrary", "arbitrary")),
    )(x, mod, w_in, ln_g, ln_b, w_s, bias_full, w_out)


def _sgu_bwd(x, dxo, mix, mod, w_in, ln_g, ln_b, w_s, bias_full, w_out, group_ind):
    bl, t, d = x.shape
    tm = min(t, SGU_TM)
    gd = d // SGU_GROUPS

    def body(x_ref, dxo_ref, mix_ref, mod_ref, win_ref, lng_ref, lnb_ref, ws_ref, bias_ref, wout_ref, ind_ref,
             dx_ref, dmod_ref, h_ref, dpre_ref, z_ref, dmix_ref, small_ref, dws_ref, dbs_ref, wm_scr, dbias_scr):
        e, ti = pl.program_id(0), pl.program_id(1)
        _sgu_masked(ws_ref, wm_scr)
        xt, dxo = x_ref[0], dxo_ref[0]
        lng = lng_ref[...]
        xn, inv, h, pre, u, vhat, rstd, vln, sv = _sgu_core(xt, mod_ref, win_ref, lng, lnb_ref[...], wm_scr, bias_ref)
        h_ref[0] = h
        z_ref[0] = (u * sv).astype(BF16)
        dmix = mod_ref[0, 2:3] * dxo
        dmix_ref[0] = dmix.astype(BF16)
        dz = _dot_nt(dmix, wout_ref[...])
        du, dsv = dz * sv, dz * u

        @pl.when(_first(e, ti))
        def _():
            dws_ref[...] = jnp.zeros_like(dws_ref)
            dbias_scr[...] = jnp.zeros_like(dbias_scr)
            small_ref[...] = jnp.zeros_like(small_ref)

        row = lax.broadcasted_iota(jnp.int32, (SGU_CHUNK, SGU_CHUNK), 0)
        col = lax.broadcasted_iota(jnp.int32, (SGU_CHUNK, SGU_CHUNK), 1)
        rows = []
        for c in range(tm // SGU_CHUNK):
            rs = slice(c * SGU_CHUNK, (c + 1) * SGU_CHUNK)
            dbias_scr[...] += dsv[rs]
            cols = []
            for g in range(SGU_GROUPS):
                cs = slice(g * gd, (g + 1) * gd)
                cols.append(_dot_tn(wm_scr[g], dsv[rs, cs]))
                dws_ref[g] += jnp.where(row >= col, _dot_nt(dsv[rs, cs], vln[rs, cs]), 0.0)
            rows.append(jnp.concatenate(cols, axis=1))
        dvln = jnp.concatenate(rows, axis=0)
        small_ref[0:1] += _colsum(dvln * vhat)
        small_ref[1:2] += _colsum(dvln)
        dvhat = dvln * lng
        dv = rstd * (dvhat - jnp.mean(dvhat, axis=-1, keepdims=True)
                     - vhat * jnp.mean(dvhat * vhat, axis=-1, keepdims=True))
        dpre = (jnp.concatenate([du, dv], axis=1) * _gelu_grad(pre)).astype(BF16)
        dpre_ref[0] = dpre
        dh = _dot_nt(dpre, win_ref[...])
        dx_ref[0] = dxo + _rms_bwd(xn, inv, dh * (1.0 + mod_ref[0, 1:2]))
        _add_rows(dmod_ref, ti == 0, [_colsum(dh), _colsum(dh * xn), _colsum(dxo * mix_ref[0])])

        @pl.when(jnp.logical_and(e == bl - 1, ti == t // tm - 1))
        def _():
            hi, lo = _split_bf16(dbias_scr[...])
            ind = ind_ref[...]
            dbs_ref[...] = (lax.dot_general(ind, hi, (((1,), (1,)), ((), ())), preferred_element_type=F32)
                            + lax.dot_general(ind, lo, (((1,), (1,)), ((), ())), preferred_element_type=F32))

    return pl.pallas_call(
        body, name="sgu_bwd", grid=(bl, t // tm),
        in_specs=[_tok(tm, d), _tok(tm, d), _tok(tm, d), _per_example(8, d), _whole(w_in.shape), _whole(ln_g.shape),
                  _whole(ln_b.shape), _whole(w_s.shape), _whole(bias_full.shape), _whole(w_out.shape),
                  _whole(group_ind.shape)],
        out_specs=(_tok(tm, d), _per_example(8, d), _tok(tm, d), _tok(tm, 2 * d), _tok(tm, d), _tok(tm, d),
                   _whole((8, d)), _whole(w_s.shape), _whole((SGU_GROUPS, SGU_CHUNK))),
        out_shape=(_sds(x.shape), _sds((bl, 8, d)), _sds(x.shape, BF16), _sds((bl, t, 2 * d), BF16),
                   _sds(x.shape, BF16), _sds(x.shape, BF16), _sds((8, d)), _sds(w_s.shape),
                   _sds((SGU_GROUPS, SGU_CHUNK))),
        scratch_shapes=[pltpu.VMEM(w_s.shape, BF16), pltpu.VMEM((SGU_CHUNK, d), F32)],
        compiler_params=_cparams(("arbitrary", "arbitrary")),
    )(x, dxo, mix, mod, w_in, ln_g, ln_b, w_s, bias_full, w_out, group_ind)


RWKV_TM = 256
N_VEC = 16


def _rwkv_pre_core(x_ref, halo_ref, mod_ref, vec_ref, ti):
    xn, inv = _rms(x_ref[0])
    scale1, shift = 1.0 + mod_ref[0, 1:2], mod_ref[0, 0:1]
    h = xn * scale1 + shift
    hn, _ = _rms(halo_ref[0])
    hh = hn * scale1 + shift
    first = jnp.where(ti == 0, 0.0, hh[7:8])
    xx = _shift_down(h, first) - h
    xs = [h + xx * vec_ref[i:i + 1] for i in range(6)]
    return xn, inv, xx, xs


def _rwkv_proj(xs, wrkv_ref, w1_ref, a1_ref, g1_ref, w2_ref, a2_ref, g2_ref):
    d = xs[0].shape[1]
    xr, xw, xk, xv, xa, xg = [z.astype(BF16) for z in xs]
    r = jnp.dot(xr, wrkv_ref[:, 0:d], preferred_element_type=F32)
    k = jnp.dot(xk, wrkv_ref[:, d:2 * d], preferred_element_type=F32)
    v = jnp.dot(xv, wrkv_ref[:, 2 * d:3 * d], preferred_element_type=F32)
    tw2 = jnp.tanh(jnp.dot(xw, w1_ref[...], preferred_element_type=F32))
    ta = jnp.dot(xa, a1_ref[...], preferred_element_type=F32)
    sg = _sigmoid(jnp.dot(xg, g1_ref[...], preferred_element_type=F32))
    lw, la, g = _dot(tw2, w2_ref[...]), _dot(ta, a2_ref[...]), _dot(sg, g2_ref[...])
    return (xr, xw, xk, xv, xa, xg), r, k, v, tw2, ta, sg, lw, la, g


def _to_heads(ref, val, nh):
    for hd in range(nh):
        ref[0, hd] = val[:, hd * HEAD:(hd + 1) * HEAD]


def _from_heads(ref, scr, nh):
    for hd in range(nh):
        scr[:, hd * HEAD:(hd + 1) * HEAD] = ref[0, hd]
    return scr[...]


def _rwkv_weight_specs(ws):
    return [_whole(w.shape) for w in ws]


def _rwkv_pre_fwd(x, mod, vec, e_ind, et_ind, weights):
    bl, t, d = x.shape
    tm = min(t, RWKV_TM)
    nh = d // HEAD
    hb = tm // 8

    def body(x_ref, halo_ref, mod_ref, vec_ref, e_ref, et_ref, wrkv, w1, a1, g1, w2, a2, g2,
             r_ref, ld_ref, k2_ref, v_ref, as_ref, bs_ref, g_ref):
        ti = pl.program_id(1)
        _, _, _, xs = _rwkv_pre_core(x_ref, halo_ref, mod_ref, vec_ref, ti)
        _, r, k, v, _, _, _, lw, la, g = _rwkv_proj(xs, wrkv, w1, a1, g1, w2, a2, g2)
        headsum = _make_headsum(e_ref[...], et_ref[...])
        ld, k2, a_s, b_s = _rwkv_elem(r, k, lw, la, vec_ref[6:7], vec_ref[7:8], vec_ref[8:9], vec_ref[9:10], headsum)
        g_ref[0] = g
        for ref, val in ((r_ref, r), (ld_ref, ld), (k2_ref, k2), (v_ref, v), (as_ref, a_s), (bs_ref, b_s)):
            _to_heads(ref, val, nh)

    halo = pl.BlockSpec((1, 8, d), lambda e, i: (e, jnp.maximum(i * hb - 1, 0), 0))
    hs = _sds((bl, nh, t, HEAD))
    return pl.pallas_call(
        body, name="rwkv_pre_fwd", grid=(bl, t // tm),
        in_specs=[_tok(tm, d), halo, _per_example(8, d), _whole(vec.shape), _whole(e_ind.shape), _whole(et_ind.shape)]
        + _rwkv_weight_specs(weights),
        out_specs=(_heads(nh, tm),) * 6 + (_tok(tm, d),),
        out_shape=(hs,) * 6 + (_sds(x.shape),),
        compiler_params=_cparams(("arbitrary", "arbitrary")),
    )(x, x, mod, vec, e_ind, et_ind, *weights)


def _rwkv_post_fwd(x, y, r, k2, v, g, mod, vec, e_ind, et_ind, w_out):
    bl, t, d = x.shape
    tm = min(t, RWKV_TM)
    nh = d // HEAD

    def body(x_ref, y_ref, r_ref, k2_ref, v_ref, g_ref, mod_ref, vec_ref, e_ref, et_ref, wout_ref,
             xo_ref, mix_ref, s0, s1, s2, s3):
        headsum = _make_headsum(e_ref[...], et_ref[...])
        yv, rv, kv, vv = (_from_heads(ref, scr, nh) for ref, scr in
                          ((y_ref, s0), (r_ref, s1), (k2_ref, s2), (v_ref, s3)))
        o = _rwkv_post(yv, rv, kv, vv, g_ref[0], vec_ref[10:11], vec_ref[11:12], vec_ref[12:13], headsum)
        mix = _dot(o, wout_ref[...])
        mix_ref[0] = mix
        xo_ref[0] = x_ref[0] + mod_ref[0, 2:3] * mix

    return pl.pallas_call(
        body, name="rwkv_post_fwd", grid=(bl, t // tm),
        in_specs=[_tok(tm, d)] + [_heads(nh, tm)] * 4 + [_tok(tm, d), _per_example(8, d), _whole(vec.shape),
                                                         _whole(e_ind.shape), _whole(et_ind.shape), _whole(w_out.shape)],
        out_specs=(_tok(tm, d), _tok(tm, d)),
        out_shape=(_sds(x.shape), _sds(x.shape)),
        scratch_shapes=[pltpu.VMEM((tm, d), F32)] * 4,
        compiler_params=_cparams(("arbitrary", "arbitrary")),
    )(x, y, r, k2, v, g, mod, vec, e_ind, et_ind, w_out)


def _rwkv_post_bwd(dxo, mix, y, r, k2, v, g, mod, vec, e_ind, et_ind, w_out):
    bl, t, d = dxo.shape
    tm = min(t, RWKV_TM)
    nh = d // HEAD

    def body(dxo_ref, mix_ref, y_ref, r_ref, k2_ref, v_ref, g_ref, mod_ref, vec_ref, e_ref, et_ref, wout_ref,
             dy_ref, dr_ref, dk2_ref, dv_ref, dg_ref, o_ref, dmix_ref, dgate_ref, small_ref, s0, s1, s2, s3):
        e, ti = pl.program_id(0), pl.program_id(1)
        headsum = _make_headsum(e_ref[...], et_ref[...])
        yv, rv, kv, vv = (_from_heads(ref, scr, nh) for ref, scr in
                          ((y_ref, s0), (r_ref, s1), (k2_ref, s2), (v_ref, s3)))
        dxo = dxo_ref[0]
        dmix = mod_ref[0, 2:3] * dxo
        dmix_ref[0] = dmix.astype(BF16)
        do = _dot_nt(dmix, wout_ref[...])
        post = functools.partial(_rwkv_post, headsum=headsum)
        o, vjp = jax.vjp(post, yv, rv, kv, vv, g_ref[0], vec_ref[10:11], vec_ref[11:12], vec_ref[12:13])
        o_ref[0] = o.astype(BF16)
        dy, dr, dk2, dv, dg, dlng, dlnb, drk = vjp(do)
        _to_heads(dy_ref, dy, nh)
        dr_ref[0], dk2_ref[0], dv_ref[0], dg_ref[0] = dr, dk2, dv, dg
        zero = jnp.zeros((1, d), F32)
        _add_rows(dgate_ref, ti == 0, [zero, zero, _colsum(dxo * mix_ref[0])])

        @pl.when(_first(e, ti))
        def _():
            small_ref[...] = jnp.zeros_like(small_ref)

        small_ref[0:1] += dlng
        small_ref[1:2] += dlnb
        small_ref[2:3] += drk

    return pl.pallas_call(
        body, name="rwkv_post_bwd", grid=(bl, t // tm),
        in_specs=[_tok(tm, d), _tok(tm, d)] + [_heads(nh, tm)] * 4
        + [_tok(tm, d), _per_example(8, d), _whole(vec.shape), _whole(e_ind.shape), _whole(et_ind.shape),
           _whole(w_out.shape)],
        out_specs=(_heads(nh, tm),) + (_tok(tm, d),) * 6 + (_per_example(8, d), _whole((8, d))),
        out_shape=(_sds((bl, nh, t, HEAD)),) + (_sds(dxo.shape),) * 4 + (_sds(dxo.shape, BF16),) * 2
        + (_sds((bl, 8, d)), _sds((8, d))),
        scratch_shapes=[pltpu.VMEM((tm, d), F32)] * 4,
        compiler_params=_cparams(("arbitrary", "arbitrary")),
    )(dxo, mix, y, r, k2, v, g, mod, vec, e_ind, et_ind, w_out)


RWKV_BWD_TM = 128


def _rwkv_pre_bwd(x, mod, vec, e_ind, et_ind, weights, dr_p, dk2_p, dv_p, dg, dr_s, dld, dk2_s, dv_s, das, dbs):
    bl, t, d = x.shape
    tm = min(t, RWKV_BWD_TM)
    nh = d // HEAD
    hb = tm // 8
    lp, gp = LORA_PAD, GATE_PAD

    def body(x_ref, halo_ref, mod_ref, vec_ref, e_ref, et_ref, wrkv, w1, a1, g1, w2, a2, g2,
             drp_ref, dk2p_ref, dvp_ref, dg_ref, drs_ref, dld_ref, dk2s_ref, dvs_ref, das_ref, dbs_ref,
             dh_ref, dhp_ref, xr_ref, xw_ref, xk_ref, xv_ref, xa_ref, xg_ref, dr_ref, dk_ref, dv_ref,
             dtw_ref, dta_ref, dtg_ref, tw2_ref, ta_ref, sg_ref, dlw_ref, dla_ref, dgb_ref, small_ref,
             s0, s1, s2, s3, s4, s5):
        e, ti = pl.program_id(0), pl.program_id(1)
        _, _, xx, xs = _rwkv_pre_core(x_ref, halo_ref, mod_ref, vec_ref, ti)
        xb, r, k, v, tw2, ta, sg, lw, la, _ = _rwkv_proj(xs, wrkv, w1, a1, g1, w2, a2, g2)
        for ref, val in zip((xr_ref, xw_ref, xk_ref, xv_ref, xa_ref, xg_ref), xb):
            ref[0] = val
        headsum = _make_headsum(e_ref[...], et_ref[...])
        drs, dld, dk2s, dvs, das, dbs_ = (_from_heads(ref, scr, nh) for ref, scr in
                                          ((drs_ref, s0), (dld_ref, s1), (dk2s_ref, s2), (dvs_ref, s3),
                                           (das_ref, s4), (dbs_ref, s5)))
        elem = functools.partial(_rwkv_elem, r, headsum=headsum)
        _, vjp = jax.vjp(elem, k, lw, la, vec_ref[6:7], vec_ref[7:8], vec_ref[8:9], vec_ref[9:10])
        dk, dlw, dla, dw0, da0, dkk, dka = vjp((dld, dk2p_ref[0] + dk2s, das, dbs_))
        dr = drp_ref[0] + drs
        dv = dvp_ref[0] + dvs
        dgv = dg_ref[0]
        dtg = _dot_nt(dgv, g2[...]) * sg * (1.0 - sg)
        dtw = _dot_nt(dlw, w2[...]) * (1.0 - tw2 * tw2)
        dta = _dot_nt(dla, a2[...])
        dr_ref[0], dk_ref[0], dv_ref[0] = dr.astype(BF16), dk.astype(BF16), dv.astype(BF16)
        dtw_ref[0], dta_ref[0], dtg_ref[0] = dtw.astype(BF16), dta.astype(BF16), dtg.astype(BF16)
        tw2_ref[0], ta_ref[0], sg_ref[0] = tw2.astype(BF16), ta.astype(BF16), sg.astype(BF16)
        dlw_ref[0], dla_ref[0], dgb_ref[0] = dlw.astype(BF16), dla.astype(BF16), dgv.astype(BF16)
        dxs = (_dot_nt(dr, wrkv[:, 0:d]), _dot_nt(dtw, w1[...]), _dot_nt(dk, wrkv[:, d:2 * d]),
               _dot_nt(dv, wrkv[:, 2 * d:3 * d]), _dot_nt(dta, a1[...]), _dot_nt(dtg, g1[...]))

        @pl.when(_first(e, ti))
        def _():
            small_ref[...] = jnp.zeros_like(small_ref)

        dh = jnp.zeros((tm, d), F32)
        dhp = jnp.zeros((tm, d), F32)
        for i, dxi in enumerate(dxs):
            mu = vec_ref[i:i + 1]
            dh += dxi * (1.0 - mu)
            dhp += dxi * mu
            small_ref[i:i + 1] += _colsum(dxi * xx)
        dh_ref[0], dhp_ref[0] = dh, dhp
        small_ref[6:7] += dw0
        small_ref[7:8] += da0
        small_ref[8:9] += dkk
        small_ref[9:10] += dka

    halo = pl.BlockSpec((1, 8, d), lambda e, i: (e, jnp.maximum(i * hb - 1, 0), 0))
    tokd, tokl, tokg = _tok(tm, d), _tok(tm, lp), _tok(tm, gp)
    bf = lambda w: _sds((bl, t, w), BF16)
    return pl.pallas_call(
        body, name="rwkv_pre_bwd", grid=(bl, t // tm),
        in_specs=[tokd, halo, _per_example(8, d), _whole(vec.shape), _whole(e_ind.shape), _whole(et_ind.shape)]
        + _rwkv_weight_specs(weights) + [tokd] * 4 + [_heads(nh, tm)] * 6,
        out_specs=(tokd, tokd) + (tokd,) * 6 + (tokd,) * 3 + (tokl, tokl, tokg, tokl, tokl, tokg)
        + (tokd, tokd, tokd, _whole((N_VEC, d))),
        out_shape=(_sds(x.shape), _sds(x.shape)) + (bf(d),) * 9 + (bf(lp), bf(lp), bf(gp), bf(lp), bf(lp), bf(gp))
        + (bf(d), bf(d), bf(d), _sds((N_VEC, d))),
        scratch_shapes=[pltpu.VMEM((tm, d), F32)] * 6,
        compiler_params=_cparams(("arbitrary", "arbitrary")),
    )(x, x, mod, vec, e_ind, et_ind, *weights, dr_p, dk2_p, dv_p, dg, dr_s, dld, dk2_s, dv_s, das, dbs)


def _norm_bwd(x, dxo, dh, dhprev, mod, dgate):
    bl, t, d = x.shape
    tm = min(t, RWKV_TM)
    hb = tm // 8
    last_blk = t // 8 - 1

    def body(x_ref, dxo_ref, dh_ref, dhp_ref, nxt_ref, mod_ref, dgate_ref, dx_ref, dmod_ref):
        ti = pl.program_id(1)
        xn, inv = _rms(x_ref[0])
        last = jnp.where(ti == t // tm - 1, 0.0, nxt_ref[0, 0:1])
        dh = dh_ref[0] + _shift_up(dhp_ref[0], last)
        dx_ref[0] = dxo_ref[0] + _rms_bwd(xn, inv, dh * (1.0 + mod_ref[0, 1:2]))

        @pl.when(ti == 0)
        def _():
            dmod_ref[0] = dgate_ref[0]

        dmod_ref[0, 0:1] += _colsum(dh)
        dmod_ref[0, 1:2] += _colsum(dh * xn)

    nxt = pl.BlockSpec((1, 8, d), lambda e, i: (e, jnp.minimum((i + 1) * hb, last_blk), 0))
    return pl.pallas_call(
        body, name="norm_bwd", grid=(bl, t // tm),
        in_specs=[_tok(tm, d)] * 4 + [nxt, _per_example(8, d), _per_example(8, d)],
        out_specs=(_tok(tm, d), _per_example(8, d)),
        out_shape=(_sds(x.shape), _sds((bl, 8, d))),
        compiler_params=_cparams(("arbitrary", "arbitrary")),
    )(x, dxo, dh, dhprev, dhprev, mod, dgate)


def _final(x, target, final_g):
    bl, t, d = x.shape
    tm = min(t, 512)

    def body(x_ref, tgt_ref, g_ref, dx_ref, loss_ref, dg_ref):
        e, ti = pl.program_id(0), pl.program_id(1)

        @pl.when(_first(e, ti))
        def _():
            loss_ref[...] = jnp.zeros_like(loss_ref)
            dg_ref[...] = jnp.zeros_like(dg_ref)

        xn, inv = _rms(x_ref[0])
        err = xn * g_ref[...] - tgt_ref[0]
        loss_ref[...] += (0.5 / d) * jnp.sum(err * err)
        dy = err * (1.0 / d)
        dg_ref[0:1] += _colsum(dy * xn)
        dx_ref[0] = _rms_bwd(xn, inv, dy * g_ref[...])

    return pl.pallas_call(
        body, name="final_loss", grid=(bl, t // tm),
        in_specs=[_tok(tm, d), _tok(tm, d), _whole(final_g.shape)],
        out_specs=(_tok(tm, d), _whole((8, 128)), _whole((8, d))),
        out_shape=(_sds(x.shape), _sds((8, 128)), _sds((8, d))),
        compiler_params=_cparams(("arbitrary", "arbitrary")),
    )(x, target, final_g)


def _adamw_math(w, g, m, v):
    m = ADAM_B1 * m + (1.0 - ADAM_B1) * g
    v = ADAM_B2 * v + (1.0 - ADAM_B2) * jnp.square(g)
    m_hat = m / (1.0 - ADAM_B1 ** ADAM_STEP)
    v_hat = v / (1.0 - ADAM_B2 ** ADAM_STEP)
    return -ADAM_LR * (m_hat / (jnp.sqrt(v_hat) + ADAM_EPS) + ADAM_WD * w), m, v


def _sum_parts(ref, n):
    g = ref[0].astype(F32)
    for s in range(1, n):
        g = g + ref[s].astype(F32)
    return g


def _adamw_rows(w, m, v, parts, row0, name):
    rows, c = w.shape
    n = parts.shape[0]
    tr = 128
    blk0 = row0 // tr

    def body(w_ref, m_ref, v_ref, p_ref, g_ref, d_ref, mo_ref, vo_ref):
        g = _sum_parts(p_ref, n)
        g_ref[...] = g
        d_ref[...], mo_ref[...], vo_ref[...] = _adamw_math(w_ref[...], g, m_ref[...], v_ref[...])

    row = pl.BlockSpec((tr, c), lambda i: (i, 0))
    return pl.pallas_call(
        body, name=name, grid=(rows // tr,),
        in_specs=[row, row, row, pl.BlockSpec((n, tr, c), lambda i: (0, blk0 + i, 0))],
        out_specs=(row,) * 4, out_shape=(_sds(w.shape),) * 4,
        compiler_params=_cparams(("parallel",)),
    )(w, m, v, parts)


def _adamw_small(items, name):
    k = len(items)
    ns = [it[3].shape[0] for it in items]

    def body(*refs):
        ins, outs = refs[:4 * k], refs[4 * k:]
        for i in range(k):
            w_ref, m_ref, v_ref, p_ref = ins[4 * i:4 * i + 4]
            g = _sum_parts(p_ref, ns[i])
            outs[4 * i][...] = g
            outs[4 * i + 1][...], outs[4 * i + 2][...], outs[4 * i + 3][...] = _adamw_math(
                w_ref[...], g, m_ref[...], v_ref[...])

    flat = [a for it in items for a in it]
    res = pl.pallas_call(
        body, name=name,
        out_shape=tuple(_sds(it[0].shape) for it in items for _ in range(4)),
        compiler_params=_cparams(),
    )(*flat)
    return [tuple(res[4 * i:4 * i + 4]) for i in range(k)]


WEIGHTS = ['ada_w', 'ada_b', 'mlp_w1', 'mlp_w2', 'a_w_in', 'a_ln_g', 'a_ln_b', 'a_w_s', 'a_b_s', 'a_w_out', 'b_mu',
           'b_w_in', 'b_w0', 'b_w1', 'b_w2', 'b_a0', 'b_a1', 'b_a2', 'b_g1', 'b_g2', 'b_k_k', 'b_k_a', 'b_r_k',
           'b_ln_g', 'b_ln_b', 'b_w_out', 'final_g']
GATHERED = [('mlp_w1', 2), ('mlp_w2', 1), ('a_w_in', 2), ('a_w_out', 1), ('b_w_in', 2), ('b_w_out', 1),
            ('b_w1', 1), ('b_a1', 1), ('b_g1', 1), ('b_w2', 2), ('b_a2', 2), ('b_g2', 2)]
VECTORS = ['b_mu', 'b_w0', 'b_a0', 'b_k_k', 'b_k_a', 'b_ln_g', 'b_ln_b']
REPLICATED = ['a_ln_g', 'a_ln_b', 'a_w_s', 'a_b_s', 'b_r_k', 'final_g']
ROW_ALIGN = 16


def _pad_rows(a, mult):
    pad = (-a.shape[-2]) % mult
    return jnp.pad(a, [(0, 0)] * (a.ndim - 2) + [(0, pad), (0, 0)]) if pad else a


def _as2d(a):
    if a.ndim == 1:
        return a.reshape(1, -1)
    lead = 1
    for s in a.shape[:-1]:
        lead *= s
    return a.reshape(lead, a.shape[-1])


def kernel(x, c, ada_w, ada_b, mlp_w1, mlp_w2, a_w_in, a_ln_g, a_ln_b, a_w_s, a_b_s, a_w_out, b_mu, b_w_in, b_w0, b_w1, b_w2, b_a0, b_a1, b_a2, b_g1, b_g2, b_k_k, b_k_a, b_r_k, b_ln_g, b_ln_b, b_w_out, final_g, loss_target, m_ada_w, m_ada_b, m_mlp_w1, m_mlp_w2, m_a_w_in, m_a_ln_g, m_a_ln_b, m_a_w_s, m_a_b_s, m_a_w_out, m_b_mu, m_b_w_in, m_b_w0, m_b_w1, m_b_w2, m_b_a0, m_b_a1, m_b_a2, m_b_g1, m_b_g2, m_b_k_k, m_b_k_a, m_b_r_k, m_b_ln_g, m_b_ln_b, m_b_w_out, m_final_g, v_ada_w, v_ada_b, v_mlp_w1, v_mlp_w2, v_a_w_in, v_a_ln_g, v_a_ln_b, v_a_w_s, v_a_b_s, v_a_w_out, v_b_mu, v_b_w_in, v_b_w0, v_b_w1, v_b_w2, v_b_a0, v_b_a1, v_b_a2, v_b_g1, v_b_g2, v_b_k_k, v_b_k_a, v_b_r_k, v_b_ln_g, v_b_ln_b, v_b_w_out, v_final_g):
    given = dict(locals())
    w = {n: given[n] for n in WEIGHTS}
    bl, t, d = x.shape
    nl = ada_w.shape[0]
    nb = N_DEV * bl
    m_tok = bl * t
    me = 4 * lax.axis_index("x") + 2 * lax.axis_index("y") + lax.axis_index("c")

    c_all = _exchange(c, "gather_c", False).reshape(nb, d)
    cols = ada_w.shape[2]
    ada_b_cols = lax.dynamic_slice(ada_b, (0, me * cols), (nl, cols)).reshape(nl, 1, cols)
    mod_cols = _ada_fwd(c_all, ada_w, ada_b_cols)
    mod_full = jnp.moveaxis(_exchange(mod_cols, "gather_mod", False), 0, 2).reshape(nl, nb, 6 * d)
    mod_mine = lax.dynamic_slice(mod_full, (0, me * bl, 0), (nl, bl, 6 * d)).reshape(nl, bl, 6, d)
    mod_mix = jnp.pad(mod_mine[:, :, 0:3], ((0, 0), (0, 0), (0, 5), (0, 0)))
    mod_mlp = jnp.pad(mod_mine[:, :, 3:6], ((0, 0), (0, 0), (0, 5), (0, 0)))

    rows = [w[n].size // d for n, _ in GATHERED]
    offs = [sum(rows[:i]) for i in range(len(rows))]
    n_rows = sum(rows)
    pack = _pad_rows(jnp.concatenate([w[n].reshape(-1, d) for n, _ in GATHERED], axis=0).astype(BF16), ROW_ALIGN)
    gathered = _exchange(pack, "gather_weights", False, relay=True)
    full = {}
    for (n, ax), off, nr in zip(GATHERED, offs, rows):
        loc = w[n].shape
        g = jnp.moveaxis(gathered[:, off:off + nr].reshape((N_DEV,) + loc), 0, ax)
        full[n] = g.reshape(loc[:ax] + (N_DEV * loc[ax],) + loc[ax + 1:])
    vec_loc = _pad_rows(jnp.concatenate([_as2d(w[n]) for n in VECTORS], axis=0), ROW_ALIGN)
    n_vec_rows = sum(_as2d(w[n]).shape[0] for n in VECTORS)
    vec = jnp.moveaxis(_exchange(vec_loc, "gather_vectors", False), 0, 1).reshape(N_VEC, d)
    vec = vec.at[n_vec_rows].set(b_r_k.reshape(d))

    e_ind, et_ind = _head_indicators(d)
    gd = d // SGU_GROUPS
    group_ind = (jnp.arange(SGU_GROUPS)[:, None] == jnp.arange(d)[None, :] // gd).astype(BF16)
    bias_full = jnp.repeat(a_b_s[0].T, gd, axis=1)
    pad_c = lambda a, n: jnp.pad(a, ((0, 0), (0, n - a.shape[1])))
    pad_r = lambda a, n: jnp.pad(a, ((0, n - a.shape[0]), (0, 0)))
    rwkv_w = (full['b_w_in'][0], pad_c(full['b_w1'][0], LORA_PAD), pad_c(full['b_a1'][0], LORA_PAD),
              pad_c(full['b_g1'][0], GATE_PAD), pad_r(full['b_w2'][0], LORA_PAD), pad_r(full['b_a2'][0], LORA_PAD),
              pad_r(full['b_g2'][0], GATE_PAD))
    sgu_args = (full['a_w_in'][0], a_ln_g, a_ln_b, a_w_s[0], bias_full, full['a_w_out'][0])

    x0 = x
    x1, mix_a = _sgu_fwd(x0, mod_mix[0], *sgu_args)
    x2, ff0 = _mlp_fwd(x1, mod_mlp[0], full['mlp_w1'][0], full['mlp_w2'][0])
    r, ld, k2, v, a_s, b_s, gate = _rwkv_pre_fwd(x2, mod_mix[1], vec, e_ind, et_ind, rwkv_w)
    y, s0 = _wkv_fwd(r, ld, k2, v, a_s, b_s)
    x3, mix_b = _rwkv_post_fwd(x2, y, r, k2, v, gate, mod_mix[1], vec, e_ind, et_ind, full['b_w_out'][0])
    x4, ff1 = _mlp_fwd(x3, mod_mlp[1], full['mlp_w1'][1], full['mlp_w2'][1])
    dx4, loss_blk, dfinal = _final(x4, loss_target, final_g.reshape(1, d))
    loss = lax.psum(loss_blk[0, 0], ("x", "y", "c"))

    tok = lambda a: a.reshape(m_tok, a.shape[-1])
    grads = {}
    dx3, dmod_mlp1, h_b, dff_b, q_b, dp_b = _mlp_bwd(x3, dx4, ff1, mod_mlp[1], full['mlp_w1'][1], full['mlp_w2'][1])
    gw1_1 = _matmul_tn(tok(h_b), tok(dp_b), "grad_mlp_w1_l1")
    gw2_1 = _matmul_tn(tok(q_b), tok(dff_b), "grad_mlp_w2_l1")
    dy, dr_p, dk2_p, dv_p, dgate_act, o_b, dmix_b, dgate_b, small_post = _rwkv_post_bwd(
        dx3, mix_b, y, r, k2, v, gate, mod_mix[1], vec, e_ind, et_ind, full['b_w_out'][0])
    grads['b_w_out'] = _matmul_tn(tok(o_b), tok(dmix_b), "grad_b_w_out")[None]
    dr_s, dld, dk2_s, dv_s, das, dbs = _wkv_bwd(r, ld, k2, v, a_s, b_s, s0, dy)
    (dh, dhp, xr_b, xw_b, xk_b, xv_b, xa_b, xg_b, dr_b, dk_b, dv_b, dtw_b, dta_b, dtg_b, tw2_b, ta_b, sg_b,
     dlw_b, dla_b, dg_b, small_pre) = _rwkv_pre_bwd(x2, mod_mix[1], vec, e_ind, et_ind, rwkv_w,
                                                    dr_p, dk2_p, dv_p, dgate_act, dr_s, dld, dk2_s, dv_s, das, dbs)
    grads['b_w_in'] = jnp.concatenate([_matmul_tn(tok(xr_b), tok(dr_b), "grad_b_w_r"),
                                       _matmul_tn(tok(xk_b), tok(dk_b), "grad_b_w_k"),
                                       _matmul_tn(tok(xv_b), tok(dv_b), "grad_b_w_v")], axis=1)[None]
    lw_, lg_ = b_w1.shape[2], b_g1.shape[2]
    grads['b_w1'] = _matmul_tn(tok(xw_b), tok(dtw_b), "grad_b_w1")[None, :, :lw_]
    grads['b_a1'] = _matmul_tn(tok(xa_b), tok(dta_b), "grad_b_a1")[None, :, :lw_]
    grads['b_g1'] = _matmul_tn(tok(xg_b), tok(dtg_b), "grad_b_g1")[None, :, :lg_]
    grads['b_w2'] = _matmul_tn(tok(tw2_b), tok(dlw_b), "grad_b_w2")[None, :lw_]
    grads['b_a2'] = _matmul_tn(tok(ta_b), tok(dla_b), "grad_b_a2")[None, :lw_]
    grads['b_g2'] = _matmul_tn(tok(sg_b), tok(dg_b), "grad_b_g2")[None, :lg_]
    dx2, dmod_mix1 = _norm_bwd(x2, dx3, dh, dhp, mod_mix[1], dgate_b)
    dx1, dmod_mlp0, h_b, dff_b, q_b, dp_b = _mlp_bwd(x1, dx2, ff0, mod_mlp[0], full['mlp_w1'][0], full['mlp_w2'][0])
    gw1_0 = _matmul_tn(tok(h_b), tok(dp_b), "grad_mlp_w1_l0")
    gw2_0 = _matmul_tn(tok(q_b), tok(dff_b), "grad_mlp_w2_l0")
    grads['mlp_w1'] = jnp.stack([gw1_0, gw1_1])
    grads['mlp_w2'] = jnp.stack([gw2_0, gw2_1])
    dx0, dmod_mix0, h_b, dpre_b, z_b, dmix_b, small_sgu, d_ws, d_bs = _sgu_bwd(
        x0, dx1, mix_a, mod_mix[0], *sgu_args, group_ind)
    grads['a_w_in'] = _matmul_tn(tok(h_b), tok(dpre_b), "grad_a_w_in")[None]
    grads['a_w_out'] = _matmul_tn(tok(z_b), tok(dmix_b), "grad_a_w_out")[None]

    dmod_mine = jnp.stack([jnp.concatenate([dmod_mix0[:, 0:3], dmod_mlp0[:, 0:3]], axis=1),
                           jnp.concatenate([dmod_mix1[:, 0:3], dmod_mlp1[:, 0:3]], axis=1)], axis=1)
    dmod_all = _exchange(dmod_mine.reshape(bl, nl * 6 * d), "gather_dmod", False)
    dmod_all = jnp.moveaxis(dmod_all.reshape(nb, nl, 6 * d), 0, 1)
    dmod_cols = lax.dynamic_slice(dmod_all, (0, 0, me * cols), (nl, nb, cols))
    g_ada_w, g_ada_b = _ada_bwd(c_all, dmod_cols, dmod_all)

    vec_g = jnp.concatenate([small_pre[0:10], small_post[0:2]], axis=0)
    vec_parts = jnp.moveaxis(vec_g.reshape(n_vec_rows, N_DEV, d // N_DEV), 1, 0).reshape(N_DEV, -1)
    vec_rows = -(-vec_parts.shape[1] // d)
    vec_parts = jnp.pad(vec_parts, ((0, 0), (0, vec_rows * d - vec_parts.shape[1]))).reshape(N_DEV, vec_rows, d)
    vec_parts = vec_parts.astype(BF16)
    parts = []
    for n, ax in GATHERED:
        loc = w[n].shape
        g = grads[n].reshape(loc[:ax] + (N_DEV, loc[ax]) + loc[ax + 1:])
        parts.append(jnp.moveaxis(g, ax, 0).reshape(N_DEV, -1, d))
    gpack = _pad_rows(jnp.concatenate(parts + [vec_parts], axis=1), ROW_ALIGN)
    recv = _exchange(gpack, "scatter_grads", True)

    rep_g = {'a_ln_g': small_sgu[0:1], 'a_ln_b': small_sgu[1:2], 'a_w_s': d_ws.reshape(-1, d), 'a_b_s': d_bs.reshape(1, d),
             'b_r_k': small_post[2:3], 'final_g': dfinal[0:1]}
    rep_rows = [rep_g[n].shape[0] for n in REPLICATED]
    rep_pack = _pad_rows(jnp.concatenate([rep_g[n] for n in REPLICATED], axis=0), 8)
    rep_all = _exchange(rep_pack, "gather_replicated_grads", False, relay=True)

    mom = {n: given['m_' + n] for n in WEIGHTS}
    var = {n: given['v_' + n] for n in WEIGHTS}
    out = {}
    big = GATHERED[:6]
    for (n, _), off in zip(big, offs):
        res = _adamw_rows(w[n].reshape(-1, d), mom[n].reshape(-1, d), var[n].reshape(-1, d), recv, off, "adamw_" + n)
        out[n] = tuple(a.reshape(w[n].shape) for a in res)
    res = _adamw_rows(ada_w.reshape(-1, d), m_ada_w.reshape(-1, d), v_ada_w.reshape(-1, d),
                      g_ada_w.reshape(1, -1, d), 0, "adamw_ada_w")
    out['ada_w'] = tuple(a.reshape(ada_w.shape) for a in res)

    items, names = [], []

    def add(n, part):
        s2 = _as2d(w[n]).shape
        items.append((_as2d(w[n]), _as2d(mom[n]), _as2d(var[n]), part.reshape((part.shape[0],) + s2)))
        names.append(n)

    for (n, _), off, nr in list(zip(GATHERED, offs, rows))[6:]:
        add(n, recv[:, off:off + nr])
    vflat = recv[:, n_rows:n_rows + vec_rows].reshape(N_DEV, -1)
    vo = 0
    for n in VECTORS:
        sz = w[n].size
        add(n, vflat[:, vo:vo + sz])
        vo += sz
    ro = 0
    for n, nr in zip(REPLICATED, rep_rows):
        add(n, rep_all[:, ro:ro + nr])
        ro += nr
    add('ada_b', g_ada_b[None])
    for n, res in zip(names, _adamw_small(items, "adamw_small")):
        out[n] = tuple(a.reshape(w[n].shape) for a in res)

    return (loss, dx0, *[out[n][0] for n in WEIGHTS], *[out[n][1] for n in WEIGHTS],
            *[out[n][2] for n in WEIGHTS], *[out[n][3] for n in WEIGHTS])
```

```python
import functools

import jax
import jax.numpy as jnp
from jax import lax
from jax.experimental import pallas as pl
from jax.experimental.pallas import tpu as pltpu

F32 = jnp.float32
BF16 = jnp.bfloat16

N_DEV = 8
RMS_EPS = 1e-6
LN_EPS = 1e-5
HEAD = 64
GN_EPS = HEAD * 1e-5
L2_EPS = 1e-12
SGU_CHUNK = 128
SGU_GROUPS = 8
WKV_CHUNK = 64
WKV_HEADS_PER_STEP = 8
LORA_PAD = 128
GATE_PAD = 256
ADAM_LR, ADAM_B1, ADAM_B2, ADAM_EPS, ADAM_WD, ADAM_STEP = 0.001, 0.9, 0.999, 1e-08, 0.01, 10
VMEM_LIMIT = 56 * 1024 * 1024


def _cparams(sem=None, **kw):
    if sem is not None:
        kw["dimension_semantics"] = sem
    return pltpu.CompilerParams(vmem_limit_bytes=VMEM_LIMIT, **kw)


def _dot(a, b):
    return jnp.dot(a.astype(BF16), b.astype(BF16), preferred_element_type=F32)


def _dot_nt(a, b):
    return lax.dot_general(a.astype(BF16), b.astype(BF16), (((1,), (1,)), ((), ())), preferred_element_type=F32)


def _dot_tn(a, b):
    return lax.dot_general(a.astype(BF16), b.astype(BF16), (((0,), (0,)), ((), ())), preferred_element_type=F32)


def _bdot(a, b, dims):
    return lax.dot_general(a.astype(BF16), b.astype(BF16), (dims, ((0,), (0,))), preferred_element_type=F32)


@jax.custom_vjp
def _tri_sum(tri, tri_t, x):
    hi = x.astype(BF16)
    lo = (x - hi.astype(F32)).astype(BF16)
    dn = (((2,), (1,)), ((0,), (0,)))
    return (lax.dot_general(tri, hi, dn, preferred_element_type=F32)
            + lax.dot_general(tri, lo, dn, preferred_element_type=F32))


_tri_sum.defvjp(lambda tri, tri_t, x: (_tri_sum(tri, tri_t, x), (tri, tri_t)),
                lambda res, g: (jnp.zeros_like(res[0]), jnp.zeros_like(res[1]), _tri_sum(res[1], res[0], g)))


@jax.custom_vjp
def _bmm_nn(a, b):
    return _bdot(a, b, ((2,), (1,)))


@jax.custom_vjp
def _bmm_nt(a, b):
    return _bdot(a, b, ((2,), (2,)))


@jax.custom_vjp
def _bmm_tn(a, b):
    return _bdot(a, b, ((1,), (1,)))


_bmm_nn.defvjp(lambda a, b: (_bmm_nn(a, b), (a, b)), lambda res, g: (_bmm_nt(g, res[1]), _bmm_tn(res[0], g)))
_bmm_nt.defvjp(lambda a, b: (_bmm_nt(a, b), (a, b)), lambda res, g: (_bmm_nn(g, res[1]), _bmm_tn(g, res[0])))
_bmm_tn.defvjp(lambda a, b: (_bmm_tn(a, b), (a, b)), lambda res, g: (_bmm_nt(res[1], g), _bmm_nn(res[0], g)))


def _tri_solve_fwd(p, rhs):
    n = p.shape[1]
    row = lax.broadcasted_iota(jnp.int32, (n, n), 0)
    col = lax.broadcasted_iota(jnp.int32, (n, n), 1)
    tinv = jnp.where(row == col, 1.0, 0.0).astype(F32)[None] + p
    for _ in range(max(1, (n - 1).bit_length()) - 1):
        p = _bmm_nn(p, p)
        tinv = tinv + _bmm_nn(tinv, p)
    u = _bmm_nn(tinv, rhs)
    return u, (tinv, u)


def _tri_solve_bwd(res, du):
    tinv, u = res
    drhs = _bmm_tn(tinv, du)
    return _bmm_nt(drhs, u), drhs


@jax.custom_vjp
def _tri_solve(p, rhs):
    return _tri_solve_fwd(p, rhs)[0]


_tri_solve.defvjp(_tri_solve_fwd, _tri_solve_bwd)


def _wkv_chunk(s0, r, ld, k, v, a, b):
    nh, n, _ = r.shape
    row = lax.broadcasted_iota(jnp.int32, (n, n), 0)
    col = lax.broadcasted_iota(jnp.int32, (n, n), 1)
    incl = row >= col
    strict = row > col
    lower = jnp.broadcast_to(jnp.where(incl, 1.0, 0.0).astype(BF16), (nh, n, n))
    upper = jnp.broadcast_to(jnp.where(row <= col, 1.0, 0.0).astype(BF16), (nh, n, n))
    c = _tri_sum(lower, upper, ld)
    c_end = c[:, n - 1:n, :]
    ec, enc, ecx, eend = jnp.exp(c), jnp.exp(-c), jnp.exp(c - ld), jnp.exp(c_end - c)
    ar = jnp.concatenate([a * ecx, r * ec], axis=1)
    mask = jnp.concatenate([strict, incl], axis=0)[None]
    m_b = jnp.where(mask, _bmm_nt(ar, b * enc), 0.0)
    m_k = jnp.where(mask, _bmm_nt(ar, k * enc), 0.0)
    a_ab, a_rb = m_b[:, :n], m_b[:, n:]
    base = _bmm_nt(ar, s0) + _bmm_nn(m_k, v)
    u = _tri_solve(a_ab, base[:, :n])
    y = base[:, n:] + _bmm_nn(a_rb, u)
    s1 = s0 * jnp.exp(c_end) + _bmm_tn(jnp.concatenate([u, v], axis=1), jnp.concatenate([b * eend, k * eend], axis=1))
    return y, s1


def _wkv_specs(bl, nh, t):
    hb, lc = WKV_HEADS_PER_STEP, WKV_CHUNK
    return hb, lc, (bl, nh // hb, t // lc)


def _wkv_fwd(r, ld, k, v, a, b):
    bl, nh, t, n = r.shape
    hb, lc, grid = _wkv_specs(bl, nh, t)
    nc = t // lc

    def body(r_ref, ld_ref, k_ref, v_ref, a_ref, b_ref, y_ref, s0_ref, s_scr):
        @pl.when(pl.program_id(2) == 0)
        def _():
            s_scr[...] = jnp.zeros_like(s_scr)

        s0 = s_scr[...]
        s0_ref[0, :, 0] = s0
        y, s1 = _wkv_chunk(s0, r_ref[0], ld_ref[0], k_ref[0], v_ref[0], a_ref[0], b_ref[0])
        y_ref[0] = y
        s_scr[...] = s1

    seq = pl.BlockSpec((1, hb, lc, n), lambda e, h, c: (e, h, c, 0))
    return pl.pallas_call(
        body, name="wkv_fwd", grid=grid,
        in_specs=[seq] * 6,
        out_specs=(seq, pl.BlockSpec((1, hb, 1, n, n), lambda e, h, c: (e, h, c, 0, 0))),
        out_shape=(jax.ShapeDtypeStruct((bl, nh, t, n), F32), jax.ShapeDtypeStruct((bl, nh, nc, n, n), F32)),
        scratch_shapes=[pltpu.VMEM((hb, n, n), F32)],
        compiler_params=_cparams(("arbitrary", "arbitrary", "arbitrary")),
    )(r, ld, k, v, a, b)


def _wkv_bwd(r, ld, k, v, a, b, s0_all, dy):
    bl, nh, t, n = r.shape
    hb, lc, grid = _wkv_specs(bl, nh, t)
    nc = t // lc

    def body(r_ref, ld_ref, k_ref, v_ref, a_ref, b_ref, s0_ref, dy_ref,
             dr_ref, dld_ref, dk_ref, dv_ref, da_ref, db_ref, ds_scr):
        @pl.when(pl.program_id(2) == 0)
        def _():
            ds_scr[...] = jnp.zeros_like(ds_scr)

        args = (s0_ref[0, :, 0], r_ref[0], ld_ref[0], k_ref[0], v_ref[0], a_ref[0], b_ref[0])
        _, vjp = jax.vjp(_wkv_chunk, *args)
        ds0, dr, dld, dk, dv, da, db = vjp((dy_ref[0], ds_scr[...]))
        ds_scr[...] = ds0
        dr_ref[0], dld_ref[0], dk_ref[0], dv_ref[0], da_ref[0], db_ref[0] = dr, dld, dk, dv, da, db

    seq = pl.BlockSpec((1, hb, lc, n), lambda e, h, c: (e, h, nc - 1 - c, 0))
    st = pl.BlockSpec((1, hb, 1, n, n), lambda e, h, c: (e, h, nc - 1 - c, 0, 0))
    out = jax.ShapeDtypeStruct((bl, nh, t, n), F32)
    return pl.pallas_call(
        body, name="wkv_bwd", grid=grid,
        in_specs=[seq] * 6 + [st, seq],
        out_specs=(seq,) * 6, out_shape=(out,) * 6,
        scratch_shapes=[pltpu.VMEM((hb, n, n), F32)],
        compiler_params=_cparams(("arbitrary", "arbitrary", "arbitrary")),
    )(r, ld, k, v, a, b, s0_all, dy)


def _exchange(x, name, scatter, relay=False):
    assert not (relay and scatter)
    blk = x.shape[1:] if scatter else x.shape

    def body(x_ref, o_ref, send_sems, recv_sems, local_sem):
        pos = (lax.axis_index("x"), lax.axis_index("y"), lax.axis_index("c"))
        me = 4 * pos[0] + 2 * pos[1] + pos[2]

        def peer_of(m):
            p = tuple(1 - pos[i] if (m >> (2 - i)) & 1 else pos[i] for i in range(3))
            return p, 4 * p[0] + 2 * p[1] + p[2]

        def copy(m):
            p, pidx = peer_of(m)
            return pltpu.make_async_remote_copy(
                src_ref=x_ref.at[pidx] if scatter else x_ref, dst_ref=o_ref.at[me],
                send_sem=send_sems.at[m - 1], recv_sem=recv_sems.at[m - 1],
                device_id=p, device_id_type=pl.DeviceIdType.MESH)

        def arrival(m):
            p, pidx = peer_of(m)
            return pltpu.make_async_remote_copy(
                src_ref=x_ref.at[pidx] if scatter else x_ref, dst_ref=o_ref.at[pidx],
                send_sem=send_sems.at[m - 1], recv_sem=recv_sems.at[m - 1],
                device_id=p, device_id_type=pl.DeviceIdType.MESH)

        mine = pltpu.make_async_copy(x_ref.at[me] if scatter else x_ref, o_ref.at[me], local_sem)
        mine.start()
        if not relay:
            sends = [copy(m) for m in range(1, N_DEV)]
            for cp in sends:
                cp.start()
            for m in range(1, N_DEV):
                arrival(m).wait_recv()
        else:
            sibling, _ = peer_of(1)
            far = (2, 4, 6)

            def relay_copy(m, origin_idx):
                return pltpu.make_async_remote_copy(
                    src_ref=o_ref.at[origin_idx], dst_ref=o_ref.at[origin_idx],
                    send_sem=send_sems.at[m], recv_sem=recv_sems.at[m],
                    device_id=sibling, device_id_type=pl.DeviceIdType.MESH)

            sends = [copy(1)] + [copy(m) for m in far]
            for cp in sends:
                cp.start()
            for m in far:
                arrival(m).wait_recv()
                fwd = relay_copy(m, peer_of(m)[1])
                fwd.start()
                sends.append(fwd)
            arrival(1).wait_recv()
            for m in far:
                relay_copy(m, peer_of(m ^ 1)[1]).wait_recv()
        for cp in sends:
            cp.wait_send()
        mine.wait()

    return pl.pallas_call(
        body, name=name,
        out_shape=jax.ShapeDtypeStruct((N_DEV,) + tuple(blk), x.dtype),
        in_specs=[pl.BlockSpec(memory_space=pl.ANY)],
        out_specs=pl.BlockSpec(memory_space=pl.ANY),
        scratch_shapes=[pltpu.SemaphoreType.DMA((N_DEV - 1,)), pltpu.SemaphoreType.DMA((N_DEV - 1,)),
                        pltpu.SemaphoreType.DMA],
    )(x)


def _rms(x):
    inv = lax.rsqrt(jnp.mean(x * x, axis=-1, keepdims=True) + RMS_EPS)
    return x * inv, inv


def _rms_bwd(xn, inv, dxn):
    return inv * (dxn - xn * jnp.mean(dxn * xn, axis=-1, keepdims=True))


def _colsum(x):
    return jnp.sum(x, axis=0, keepdims=True)


def _sigmoid(x):
    return 0.5 * (jnp.tanh(0.5 * x) + 1.0)


def _split_bf16(x):
    hi = x.astype(BF16)
    return hi, (x - hi.astype(F32)).astype(BF16)


def _dot_split(x, e):
    hi, lo = _split_bf16(x)
    return jnp.dot(hi, e, preferred_element_type=F32) + jnp.dot(lo, e, preferred_element_type=F32)


@jax.custom_vjp
def _headsum(x, e, et):
    return _dot_split(_dot_split(x, e), et)


_headsum.defvjp(lambda x, e, et: (_headsum(x, e, et), (e, et)),
                lambda res, g: (_headsum(g, *res), jnp.zeros_like(res[0]), jnp.zeros_like(res[1])))


def _make_headsum(e, et):
    return lambda x: _headsum(x, e, et)


def _head_indicators(d):
    e = (jnp.arange(d)[:, None] // HEAD == jnp.arange(128)[None, :]).astype(BF16)
    return e, e.T


def _rwkv_elem(r, k, lw, la, w0, a0, k_k, k_a, headsum):
    z = w0 + lw
    w_log = -(jnp.maximum(-z, 0.0) + jnp.log(1.0 + jnp.exp(-jnp.abs(z)))) - 0.5
    ld = -jnp.exp(w_log)
    a = _sigmoid(a0 + la)
    kkp = k * k_k
    kk = kkp / jnp.maximum(jnp.sqrt(headsum(kkp * kkp)), L2_EPS)
    k2 = k * (1.0 + (a - 1.0) * k_a)
    del r
    return ld, k2, -kk, kk * a


def _rwkv_post(y, r, k2, v, g, ln_g, ln_b, r_k, headsum):
    m = headsum(y) * (1.0 / HEAD)
    yc = y - m
    var = headsum(yc * yc) * (1.0 / HEAD)
    yn = yc * lax.rsqrt(var + GN_EPS)
    bonus = headsum(r * k2 * r_k) * v
    return (yn * ln_g + ln_b + bonus) * g


def _shift_down(h, first_row):
    rolled = pltpu.roll(h, 1, 0)
    row = lax.broadcasted_iota(jnp.int32, h.shape, 0)
    return jnp.where(row == 0, first_row, rolled)


def _shift_up(h, last_row):
    n = h.shape[0]
    rolled = pltpu.roll(h, n - 1, 0)
    row = lax.broadcasted_iota(jnp.int32, h.shape, 0)
    return jnp.where(row == n - 1, last_row, rolled)


def _gelu(p):
    return 0.5 * p * (1.0 + lax.erf(p * 0.7071067811865476))


def _gelu_grad(p):
    return 0.5 * (1.0 + lax.erf(p * 0.7071067811865476)) + p * jnp.exp(-0.5 * p * p) * 0.3989422804014327


def _tok(tm, d):
    return pl.BlockSpec((1, tm, d), lambda e, t, *_: (e, t, 0))


def _per_example(rows, d):
    return pl.BlockSpec((1, rows, d), lambda e, t, *_: (e, 0, 0))


def _whole(shape):
    nd = len(shape)
    return pl.BlockSpec(tuple(shape), lambda *_: (0,) * nd)


def _heads(nh, tm):
    return pl.BlockSpec((1, nh, tm, HEAD), lambda e, t, *_: (e, 0, t, 0))


def _sds(shape, dtype=F32):
    return jax.ShapeDtypeStruct(tuple(shape), dtype)


def _add_rows(ref, first, rows):
    @pl.when(first)
    def _():
        ref[0] = jnp.zeros(ref.shape[1:], ref.dtype)

    for i, r in enumerate(rows):
        ref[0, i:i + 1] += r


def _first(e, t):
    return jnp.logical_and(e == 0, t == 0)


def _ada_fwd(c_all, ada_w, ada_b_cols):
    nl, d, cols = ada_w.shape
    nb = c_all.shape[0]

    def body(c_ref, w_ref, b_ref, o_ref):
        c = c_ref[...]
        cond = c * _sigmoid(c)
        for i in range(nl):
            o_ref[i] = _dot(cond, w_ref[i]) + b_ref[i]

    return pl.pallas_call(
        body, name="ada_fwd", out_shape=_sds((nl, nb, cols)),
        compiler_params=_cparams(),
    )(c_all, ada_w, ada_b_cols)


def _ada_bwd(c_all, dmod_cols, dmod_full):
    nl, nb, cols = dmod_cols.shape
    d = c_all.shape[1]

    def body(c_ref, g_ref, f_ref, o_ref, b_ref):
        c = c_ref[...]
        cond = c * _sigmoid(c)
        for i in range(nl):
            o_ref[i] = _dot_tn(cond, g_ref[i])
            b_ref[i:i + 1] = jnp.sum(f_ref[i], axis=0, keepdims=True)

    return pl.pallas_call(
        body, name="ada_bwd", out_shape=(_sds((nl, d, cols)), _sds((nl, dmod_full.shape[2]))),
        compiler_params=_cparams(),
    )(c_all, dmod_cols, dmod_full)


def _matmul_tn(a, b, name):
    m, ka = a.shape
    n = b.shape[1]
    tm = min(m, 2048)
    tk = min(ka, 1024)
    tn = min(n, 512)
    steps = m // tm

    def body(a_ref, b_ref, o_ref, acc):
        s = pl.program_id(2)

        @pl.when(s == 0)
        def _():
            acc[...] = jnp.zeros_like(acc)

        acc[...] += _dot_tn(a_ref[...], b_ref[...])

        @pl.when(s == steps - 1)
        def _():
            o_ref[...] = acc[...].astype(BF16)

    return pl.pallas_call(
        body, name=name, grid=(ka // tk, n // tn, steps),
        in_specs=[pl.BlockSpec((tm, tk), lambda i, j, s: (s, i)), pl.BlockSpec((tm, tn), lambda i, j, s: (s, j))],
        out_specs=pl.BlockSpec((tk, tn), lambda i, j, s: (i, j)),
        out_shape=_sds((ka, n), BF16),
        scratch_shapes=[pltpu.VMEM((tk, tn), F32)],
        compiler_params=_cparams(("parallel", "parallel", "arbitrary")),
    )(a, b)


MLP_TM = 512
MLP_FJ = 1024


def _mlp_fwd(x, mod, w1, w2):
    bl, t, d = x.shape
    f = w1.shape[1]
    tm, fj = min(t, MLP_TM), min(f, MLP_FJ)
    nj = f // fj

    def body(x_ref, mod_ref, w1_ref, w2_ref, xo_ref, ff_ref, h_scr, acc):
        j = pl.program_id(2)

        @pl.when(j == 0)
        def _():
            xn, _ = _rms(x_ref[0])
            h_scr[...] = (xn * (1.0 + mod_ref[0, 1:2]) + mod_ref[0, 0:1]).astype(BF16)
            acc[...] = jnp.zeros_like(acc)

        p = jnp.dot(h_scr[...], w1_ref[...], preferred_element_type=F32)
        q = jnp.square(jnp.maximum(p, 0.0))
        acc[...] += _dot(q, w2_ref[...])

        @pl.when(j == nj - 1)
        def _():
            ff_ref[0] = acc[...]
            xo_ref[0] = x_ref[0] + mod_ref[0, 2:3] * acc[...]

    return pl.pallas_call(
        body, name="mlp_fwd", grid=(bl, t // tm, nj),
        in_specs=[_tok(tm, d), _per_example(8, d),
                  pl.BlockSpec((d, fj), lambda e, i, j: (0, j)), pl.BlockSpec((fj, d), lambda e, i, j: (j, 0))],
        out_specs=(_tok(tm, d), _tok(tm, d)),
        out_shape=(_sds(x.shape), _sds(x.shape)),
        scratch_shapes=[pltpu.VMEM((tm, d), BF16), pltpu.VMEM((tm, d), F32)],
        compiler_params=_cparams(("arbitrary", "arbitrary", "arbitrary")),
    )(x, mod, w1, w2)


def _mlp_bwd(x, dxo, ff, mod, w1, w2):
    bl, t, d = x.shape
    f = w1.shape[1]
    tm, fj = min(t, MLP_TM), min(f, MLP_FJ)
    nj = f // fj

    def body(x_ref, dxo_ref, ff_ref, mod_ref, w1_ref, w2_ref,
             dx_ref, dmod_ref, h_ref, dff_ref, q_ref, dp_ref, acc):
        ti, j = pl.program_id(1), pl.program_id(2)

        @pl.when(j == 0)
        def _():
            xn, _ = _rms(x_ref[0])
            h_ref[0] = (xn * (1.0 + mod_ref[0, 1:2]) + mod_ref[0, 0:1]).astype(BF16)
            dff_ref[0] = (mod_ref[0, 2:3] * dxo_ref[0]).astype(BF16)
            acc[...] = jnp.zeros_like(acc)

        p = jnp.dot(h_ref[0], w1_ref[...], preferred_element_type=F32)
        rl = jnp.maximum(p, 0.0)
        q_ref[0] = jnp.square(rl).astype(BF16)
        dp = (_dot_nt(dff_ref[0], w2_ref[...]) * (2.0 * rl)).astype(BF16)
        dp_ref[0] = dp
        acc[...] += _dot_nt(dp, w1_ref[...])

        @pl.when(j == nj - 1)
        def _():
            xn, inv = _rms(x_ref[0])
            dh = acc[...]
            dx_ref[0] = dxo_ref[0] + _rms_bwd(xn, inv, dh * (1.0 + mod_ref[0, 1:2]))
            _add_rows(dmod_ref, ti == 0, [_colsum(dh), _colsum(dh * xn), _colsum(dxo_ref[0] * ff_ref[0])])

    big = lambda: pl.BlockSpec((1, tm, fj), lambda e, i, j: (e, i, j))
    return pl.pallas_call(
        body, name="mlp_bwd", grid=(bl, t // tm, nj),
        in_specs=[_tok(tm, d), _tok(tm, d), _tok(tm, d), _per_example(8, d),
                  pl.BlockSpec((d, fj), lambda e, i, j: (0, j)), pl.BlockSpec((fj, d), lambda e, i, j: (j, 0))],
        out_specs=(_tok(tm, d), _per_example(8, d), _tok(tm, d), _tok(tm, d), big(), big()),
        out_shape=(_sds(x.shape), _sds((bl, 8, d)), _sds(x.shape, BF16), _sds(x.shape, BF16),
                   _sds((bl, t, f), BF16), _sds((bl, t, f), BF16)),
        scratch_shapes=[pltpu.VMEM((tm, d), F32)],
        compiler_params=_cparams(("arbitrary", "arbitrary", "arbitrary")),
    )(x, dxo, ff, mod, w1, w2)


SGU_TM = 256


def _sgu_core(x, mod_ref, win_ref, lng, lnb, ws_ref, bias_ref):
    tm, d = x.shape
    xn, inv = _rms(x)
    h = (xn * (1.0 + mod_ref[0, 1:2]) + mod_ref[0, 0:1]).astype(BF16)
    pre = jnp.dot(h, win_ref[...], preferred_element_type=F32)
    uv = _gelu(pre)
    u, v = uv[:, :d], uv[:, d:]
    mu = jnp.mean(v, axis=-1, keepdims=True)
    vc = v - mu
    rstd = lax.rsqrt(jnp.mean(vc * vc, axis=-1, keepdims=True) + LN_EPS)
    vhat = vc * rstd
    vln = vhat * lng + lnb
    gd = d // SGU_GROUPS
    rows = []
    for c in range(tm // SGU_CHUNK):
        cols = []
        for g in range(SGU_GROUPS):
            cols.append(_dot(ws_ref[g], vln[c * SGU_CHUNK:(c + 1) * SGU_CHUNK, g * gd:(g + 1) * gd]))
        rows.append(jnp.concatenate(cols, axis=1) + bias_ref[...])
    sv = jnp.concatenate(rows, axis=0)
    return xn, inv, h, pre, u, vhat, rstd, vln, sv


def _sgu_masked(ws_ref, wm_scr):
    row = lax.broadcasted_iota(jnp.int32, (SGU_CHUNK, SGU_CHUNK), 0)
    col = lax.broadcasted_iota(jnp.int32, (SGU_CHUNK, SGU_CHUNK), 1)
    for g in range(SGU_GROUPS):
        wm_scr[g] = jnp.where(row >= col, ws_ref[g], 0.0).astype(BF16)


def _sgu_fwd(x, mod, w_in, ln_g, ln_b, w_s, bias_full, w_out):
    bl, t, d = x.shape
    tm = min(t, SGU_TM)

    def body(x_ref, mod_ref, win_ref, lng_ref, lnb_ref, ws_ref, bias_ref, wout_ref, xo_ref, mix_ref, wm_scr):
        _sgu_masked(ws_ref, wm_scr)
        xt = x_ref[0]
        *_, u, _, _, _, sv = _sgu_core(xt, mod_ref, win_ref, lng_ref[...], lnb_ref[...], wm_scr, bias_ref)
        mix = _dot(u * sv, wout_ref[...])
        mix_ref[0] = mix
        xo_ref[0] = xt + mod_ref[0, 2:3] * mix

    return pl.pallas_call(
        body, name="sgu_fwd", grid=(bl, t // tm),
        in_specs=[_tok(tm, d), _per_example(8, d), _whole(w_in.shape), _whole(ln_g.shape), _whole(ln_b.shape),
                  _whole(w_s.shape), _whole(bias_full.shape), _whole(w_out.shape)],
        out_specs=(_tok(tm, d), _tok(tm, d)),
        out_shape=(_sds(x.shape), _sds(x.shape)),
        scratch_shapes=[pltpu.VMEM(w_s.shape, BF16)],
        compiler_params=_cparams(("arbitrary", "arbitrary")),
    )(x, mod, w_in, ln_g, ln_b, w_s, bias_full, w_out)


def _sgu_bwd(x, dxo, mix, mod, w_in, ln_g, ln_b, w_s, bias_full, w_out, group_ind):
    bl, t, d = x.shape
    tm = min(t, SGU_TM)
    gd = d // SGU_GROUPS

    def body(x_ref, dxo_ref, mix_ref, mod_ref, win_ref, lng_ref, lnb_ref, ws_ref, bias_ref, wout_ref, ind_ref,
             dx_ref, dmod_ref, h_ref, dpre_ref, z_ref, dmix_ref, small_ref, dws_ref, dbs_ref, wm_scr, dbias_scr):
        e, ti = pl.program_id(0), pl.program_id(1)
        _sgu_masked(ws_ref, wm_scr)
        xt, dxo = x_ref[0], dxo_ref[0]
        lng = lng_ref[...]
        xn, inv, h, pre, u, vhat, rstd, vln, sv = _sgu_core(xt, mod_ref, win_ref, lng, lnb_ref[...], wm_scr, bias_ref)
        h_ref[0] = h
        z_ref[0] = (u * sv).astype(BF16)
        dmix = mod_ref[0, 2:3] * dxo
        dmix_ref[0] = dmix.astype(BF16)
        dz = _dot_nt(dmix, wout_ref[...])
        du, dsv = dz * sv, dz * u

        @pl.when(_first(e, ti))
        def _():
            dws_ref[...] = jnp.zeros_like(dws_ref)
            dbias_scr[...] = jnp.zeros_like(dbias_scr)
            small_ref[...] = jnp.zeros_like(small_ref)

        row = lax.broadcasted_iota(jnp.int32, (SGU_CHUNK, SGU_CHUNK), 0)
        col = lax.broadcasted_iota(jnp.int32, (SGU_CHUNK, SGU_CHUNK), 1)
        rows = []
        for c in range(tm // SGU_CHUNK):
            rs = slice(c * SGU_CHUNK, (c + 1) * SGU_CHUNK)
            dbias_scr[...] += dsv[rs]
            cols = []
            for g in range(SGU_GROUPS):
                cs = slice(g * gd, (g + 1) * gd)
                cols.append(_dot_tn(wm_scr[g], dsv[rs, cs]))
                dws_ref[g] += jnp.where(row >= col, _dot_nt(dsv[rs, cs], vln[rs, cs]), 0.0)
            rows.append(jnp.concatenate(cols, axis=1))
        dvln = jnp.concatenate(rows, axis=0)
        small_ref[0:1] += _colsum(dvln * vhat)
        small_ref[1:2] += _colsum(dvln)
        dvhat = dvln * lng
        dv = rstd * (dvhat - jnp.mean(dvhat, axis=-1, keepdims=True)
                     - vhat * jnp.mean(dvhat * vhat, axis=-1, keepdims=True))
        dpre = (jnp.concatenate([du, dv], axis=1) * _gelu_grad(pre)).astype(BF16)
        dpre_ref[0] = dpre
        dh = _dot_nt(dpre, win_ref[...])
        dx_ref[0] = dxo + _rms_bwd(xn, inv, dh * (1.0 + mod_ref[0, 1:2]))
        _add_rows(dmod_ref, ti == 0, [_colsum(dh), _colsum(dh * xn), _colsum(dxo * mix_ref[0])])

        @pl.when(jnp.logical_and(e == bl - 1, ti == t // tm - 1))
        def _():
            hi, lo = _split_bf16(dbias_scr[...])
            ind = ind_ref[...]
            dbs_ref[...] = (lax.dot_general(ind, hi, (((1,), (1,)), ((), ())), preferred_element_type=F32)
                            + lax.dot_general(ind, lo, (((1,), (1,)), ((), ())), preferred_element_type=F32))

    return pl.pallas_call(
        body, name="sgu_bwd", grid=(bl, t // tm),
        in_specs=[_tok(tm, d), _tok(tm, d), _tok(tm, d), _per_example(8, d), _whole(w_in.shape), _whole(ln_g.shape),
                  _whole(ln_b.shape), _whole(w_s.shape), _whole(bias_full.shape), _whole(w_out.shape),
                  _whole(group_ind.shape)],
        out_specs=(_tok(tm, d), _per_example(8, d), _tok(tm, d), _tok(tm, 2 * d), _tok(tm, d), _tok(tm, d),
                   _whole((8, d)), _whole(w_s.shape), _whole((SGU_GROUPS, SGU_CHUNK))),
        out_shape=(_sds(x.shape), _sds((bl, 8, d)), _sds(x.shape, BF16), _sds((bl, t, 2 * d), BF16),
                   _sds(x.shape, BF16), _sds(x.shape, BF16), _sds((8, d)), _sds(w_s.shape),
                   _sds((SGU_GROUPS, SGU_CHUNK))),
        scratch_shapes=[pltpu.VMEM(w_s.shape, BF16), pltpu.VMEM((SGU_CHUNK, d), F32)],
        compiler_params=_cparams(("arbitrary", "arbitrary")),
    )(x, dxo, mix, mod, w_in, ln_g, ln_b, w_s, bias_full, w_out, group_ind)


RWKV_TM = 256
N_VEC = 16


def _rwkv_pre_core(x_ref, halo_ref, mod_ref, vec_ref, ti):
    xn, inv = _rms(x_ref[0])
    scale1, shift = 1.0 + mod_ref[0, 1:2], mod_ref[0, 0:1]
    h = xn * scale1 + shift
    hn, _ = _rms(halo_ref[0])
    hh = hn * scale1 + shift
    first = jnp.where(ti == 0, 0.0, hh[7:8])
    xx = _shift_down(h, first) - h
    xs = [h + xx * vec_ref[i:i + 1] for i in range(6)]
    return xn, inv, xx, xs


def _rwkv_proj(xs, wrkv_ref, w1_ref, a1_ref, g1_ref, w2_ref, a2_ref, g2_ref):
    d = xs[0].shape[1]
    xr, xw, xk, xv, xa, xg = [z.astype(BF16) for z in xs]
    r = jnp.dot(xr, wrkv_ref[:, 0:d], preferred_element_type=F32)
    k = jnp.dot(xk, wrkv_ref[:, d:2 * d], preferred_element_type=F32)
    v = jnp.dot(xv, wrkv_ref[:, 2 * d:3 * d], preferred_element_type=F32)
    tw2 = jnp.tanh(jnp.dot(xw, w1_ref[...], preferred_element_type=F32))
    ta = jnp.dot(xa, a1_ref[...], preferred_element_type=F32)
    sg = _sigmoid(jnp.dot(xg, g1_ref[...], preferred_element_type=F32))
    lw, la, g = _dot(tw2, w2_ref[...]), _dot(ta, a2_ref[...]), _dot(sg, g2_ref[...])
    return (xr, xw, xk, xv, xa, xg), r, k, v, tw2, ta, sg, lw, la, g


def _to_heads(ref, val, nh):
    for hd in range(nh):
        ref[0, hd] = val[:, hd * HEAD:(hd + 1) * HEAD]


def _from_heads(ref, scr, nh):
    for hd in range(nh):
        scr[:, hd * HEAD:(hd + 1) * HEAD] = ref[0, hd]
    return scr[...]


def _rwkv_weight_specs(ws):
    return [_whole(w.shape) for w in ws]


def _rwkv_pre_fwd(x, mod, vec, e_ind, et_ind, weights):
    bl, t, d = x.shape
    tm = min(t, RWKV_TM)
    nh = d // HEAD
    hb = tm // 8

    def body(x_ref, halo_ref, mod_ref, vec_ref, e_ref, et_ref, wrkv, w1, a1, g1, w2, a2, g2,
             r_ref, ld_ref, k2_ref, v_ref, as_ref, bs_ref, g_ref):
        ti = pl.program_id(1)
        _, _, _, xs = _rwkv_pre_core(x_ref, halo_ref, mod_ref, vec_ref, ti)
        _, r, k, v, _, _, _, lw, la, g = _rwkv_proj(xs, wrkv, w1, a1, g1, w2, a2, g2)
        headsum = _make_headsum(e_ref[...], et_ref[...])
        ld, k2, a_s, b_s = _rwkv_elem(r, k, lw, la, vec_ref[6:7], vec_ref[7:8], vec_ref[8:9], vec_ref[9:10], headsum)
        g_ref[0] = g
        for ref, val in ((r_ref, r), (ld_ref, ld), (k2_ref, k2), (v_ref, v), (as_ref, a_s), (bs_ref, b_s)):
            _to_heads(ref, val, nh)

    halo = pl.BlockSpec((1, 8, d), lambda e, i: (e, jnp.maximum(i * hb - 1, 0), 0))
    hs = _sds((bl, nh, t, HEAD))
    return pl.pallas_call(
        body, name="rwkv_pre_fwd", grid=(bl, t // tm),
        in_specs=[_tok(tm, d), halo, _per_example(8, d), _whole(vec.shape), _whole(e_ind.shape), _whole(et_ind.shape)]
        + _rwkv_weight_specs(weights),
        out_specs=(_heads(nh, tm),) * 6 + (_tok(tm, d),),
        out_shape=(hs,) * 6 + (_sds(x.shape),),
        compiler_params=_cparams(("arbitrary", "arbitrary")),
    )(x, x, mod, vec, e_ind, et_ind, *weights)


def _rwkv_post_fwd(x, y, r, k2, v, g, mod, vec, e_ind, et_ind, w_out):
    bl, t, d = x.shape
    tm = min(t, RWKV_TM)
    nh = d // HEAD

    def body(x_ref, y_ref, r_ref, k2_ref, v_ref, g_ref, mod_ref, vec_ref, e_ref, et_ref, wout_ref,
             xo_ref, mix_ref, s0, s1, s2, s3):
        headsum = _make_headsum(e_ref[...], et_ref[...])
        yv, rv, kv, vv = (_from_heads(ref, scr, nh) for ref, scr in
                          ((y_ref, s0), (r_ref, s1), (k2_ref, s2), (v_ref, s3)))
        o = _rwkv_post(yv, rv, kv, vv, g_ref[0], vec_ref[10:11], vec_ref[11:12], vec_ref[12:13], headsum)
        mix = _dot(o, wout_ref[...])
        mix_ref[0] = mix
        xo_ref[0] = x_ref[0] + mod_ref[0, 2:3] * mix

    return pl.pallas_call(
        body, name="rwkv_post_fwd", grid=(bl, t // tm),
        in_specs=[_tok(tm, d)] + [_heads(nh, tm)] * 4 + [_tok(tm, d), _per_example(8, d), _whole(vec.shape),
                                                         _whole(e_ind.shape), _whole(et_ind.shape), _whole(w_out.shape)],
        out_specs=(_tok(tm, d), _tok(tm, d)),
        out_shape=(_sds(x.shape), _sds(x.shape)),
        scratch_shapes=[pltpu.VMEM((tm, d), F32)] * 4,
        compiler_params=_cparams(("arbitrary", "arbitrary")),
    )(x, y, r, k2, v, g, mod, vec, e_ind, et_ind, w_out)


def _rwkv_post_bwd(dxo, mix, y, r, k2, v, g, mod, vec, e_ind, et_ind, w_out):
    bl, t, d = dxo.shape
    tm = min(t, RWKV_TM)
    nh = d // HEAD

    def body(dxo_ref, mix_ref, y_ref, r_ref, k2_ref, v_ref, g_ref, mod_ref, vec_ref, e_ref, et_ref, wout_ref,
             dy_ref, dr_ref, dk2_ref, dv_ref, dg_ref, o_ref, dmix_ref, dgate_ref, small_ref, s0, s1, s2, s3):
        e, ti = pl.program_id(0), pl.program_id(1)
        headsum = _make_headsum(e_ref[...], et_ref[...])
        yv, rv, kv, vv = (_from_heads(ref, scr, nh) for ref, scr in
                          ((y_ref, s0), (r_ref, s1), (k2_ref, s2), (v_ref, s3)))
        dxo = dxo_ref[0]
        dmix = mod_ref[0, 2:3] * dxo
        dmix_ref[0] = dmix.astype(BF16)
        do = _dot_nt(dmix, wout_ref[...])
        post = functools.partial(_rwkv_post, headsum=headsum)
        o, vjp = jax.vjp(post, yv, rv, kv, vv, g_ref[0], vec_ref[10:11], vec_ref[11:12], vec_ref[12:13])
        o_ref[0] = o.astype(BF16)
        dy, dr, dk2, dv, dg, dlng, dlnb, drk = vjp(do)
        _to_heads(dy_ref, dy, nh)
        dr_ref[0], dk2_ref[0], dv_ref[0], dg_ref[0] = dr, dk2, dv, dg
        zero = jnp.zeros((1, d), F32)
        _add_rows(dgate_ref, ti == 0, [zero, zero, _colsum(dxo * mix_ref[0])])

        @pl.when(_first(e, ti))
        def _():
            small_ref[...] = jnp.zeros_like(small_ref)

        small_ref[0:1] += dlng
        small_ref[1:2] += dlnb
        small_ref[2:3] += drk

    return pl.pallas_call(
        body, name="rwkv_post_bwd", grid=(bl, t // tm),
        in_specs=[_tok(tm, d), _tok(tm, d)] + [_heads(nh, tm)] * 4
        + [_tok(tm, d), _per_example(8, d), _whole(vec.shape), _whole(e_ind.shape), _whole(et_ind.shape),
           _whole(w_out.shape)],
        out_specs=(_heads(nh, tm),) + (_tok(tm, d),) * 6 + (_per_example(8, d), _whole((8, d))),
        out_shape=(_sds((bl, nh, t, HEAD)),) + (_sds(dxo.shape),) * 4 + (_sds(dxo.shape, BF16),) * 2
        + (_sds((bl, 8, d)), _sds((8, d))),
        scratch_shapes=[pltpu.VMEM((tm, d), F32)] * 4,
        compiler_params=_cparams(("arbitrary", "arbitrary")),
    )(dxo, mix, y, r, k2, v, g, mod, vec, e_ind, et_ind, w_out)


RWKV_BWD_TM = 128


def _rwkv_pre_bwd(x, mod, vec, e_ind, et_ind, weights, dr_p, dk2_p, dv_p, dg, dr_s, dld, dk2_s, dv_s, das, dbs):
    bl, t, d = x.shape
    tm = min(t, RWKV_BWD_TM)
    nh = d // HEAD
    hb = tm // 8
    lp, gp = LORA_PAD, GATE_PAD

    def body(x_ref, halo_ref, mod_ref, vec_ref, e_ref, et_ref, wrkv, w1, a1, g1, w2, a2, g2,
             drp_ref, dk2p_ref, dvp_ref, dg_ref, drs_ref, dld_ref, dk2s_ref, dvs_ref, das_ref, dbs_ref,
             dh_ref, dhp_ref, xr_ref, xw_ref, xk_ref, xv_ref, xa_ref, xg_ref, dr_ref, dk_ref, dv_ref,
             dtw_ref, dta_ref, dtg_ref, tw2_ref, ta_ref, sg_ref, dlw_ref, dla_ref, dgb_ref, small_ref,
             s0, s1, s2, s3, s4, s5):
        e, ti = pl.program_id(0), pl.program_id(1)
        _, _, xx, xs = _rwkv_pre_core(x_ref, halo_ref, mod_ref, vec_ref, ti)
        xb, r, k, v, tw2, ta, sg, lw, la, _ = _rwkv_proj(xs, wrkv, w1, a1, g1, w2, a2, g2)
        for ref, val in zip((xr_ref, xw_ref, xk_ref, xv_ref, xa_ref, xg_ref), xb):
            ref[0] = val
        headsum = _make_headsum(e_ref[...], et_ref[...])
        drs, dld, dk2s, dvs, das, dbs_ = (_from_heads(ref, scr, nh) for ref, scr in
                                          ((drs_ref, s0), (dld_ref, s1), (dk2s_ref, s2), (dvs_ref, s3),
                                           (das_ref, s4), (dbs_ref, s5)))
        elem = functools.partial(_rwkv_elem, r, headsum=headsum)
        _, vjp = jax.vjp(elem, k, lw, la, vec_ref[6:7], vec_ref[7:8], vec_ref[8:9], vec_ref[9:10])
        dk, dlw, dla, dw0, da0, dkk, dka = vjp((dld, dk2p_ref[0] + dk2s, das, dbs_))
        dr = drp_ref[0] + drs
        dv = dvp_ref[0] + dvs
        dgv = dg_ref[0]
        dtg = _dot_nt(dgv, g2[...]) * sg * (1.0 - sg)
        dtw = _dot_nt(dlw, w2[...]) * (1.0 - tw2 * tw2)
        dta = _dot_nt(dla, a2[...])
        dr_ref[0], dk_ref[0], dv_ref[0] = dr.astype(BF16), dk.astype(BF16), dv.astype(BF16)
        dtw_ref[0], dta_ref[0], dtg_ref[0] = dtw.astype(BF16), dta.astype(BF16), dtg.astype(BF16)
        tw2_ref[0], ta_ref[0], sg_ref[0] = tw2.astype(BF16), ta.astype(BF16), sg.astype(BF16)
        dlw_ref[0], dla_ref[0], dgb_ref[0] = dlw.astype(BF16), dla.astype(BF16), dgv.astype(BF16)
        dxs = (_dot_nt(dr, wrkv[:, 0:d]), _dot_nt(dtw, w1[...]), _dot_nt(dk, wrkv[:, d:2 * d]),
               _dot_nt(dv, wrkv[:, 2 * d:3 * d]), _dot_nt(dta, a1[...]), _dot_nt(dtg, g1[...]))

        @pl.when(_first(e, ti))
        def _():
            small_ref[...] = jnp.zeros_like(small_ref)

        dh = jnp.zeros((tm, d), F32)
        dhp = jnp.zeros((tm, d), F32)
        for i, dxi in enumerate(dxs):
            mu = vec_ref[i:i + 1]
            dh += dxi * (1.0 - mu)
            dhp += dxi * mu
            small_ref[i:i + 1] += _colsum(dxi * xx)
        dh_ref[0], dhp_ref[0] = dh, dhp
        small_ref[6:7] += dw0
        small_ref[7:8] += da0
        small_ref[8:9] += dkk
        small_ref[9:10] += dka

    halo = pl.BlockSpec((1, 8, d), lambda e, i: (e, jnp.maximum(i * hb - 1, 0), 0))
    tokd, tokl, tokg = _tok(tm, d), _tok(tm, lp), _tok(tm, gp)
    bf = lambda w: _sds((bl, t, w), BF16)
    return pl.pallas_call(
        body, name="rwkv_pre_bwd", grid=(bl, t // tm),
        in_specs=[tokd, halo, _per_example(8, d), _whole(vec.shape), _whole(e_ind.shape), _whole(et_ind.shape)]
        + _rwkv_weight_specs(weights) + [tokd] * 4 + [_heads(nh, tm)] * 6,
        out_specs=(tokd, tokd) + (tokd,) * 6 + (tokd,) * 3 + (tokl, tokl, tokg, tokl, tokl, tokg)
        + (tokd, tokd, tokd, _whole((N_VEC, d))),
        out_shape=(_sds(x.shape), _sds(x.shape)) + (bf(d),) * 9 + (bf(lp), bf(lp), bf(gp), bf(lp), bf(lp), bf(gp))
        + (bf(d), bf(d), bf(d), _sds((N_VEC, d))),
        scratch_shapes=[pltpu.VMEM((tm, d), F32)] * 6,
        compiler_params=_cparams(("arbitrary", "arbitrary")),
    )(x, x, mod, vec, e_ind, et_ind, *weights, dr_p, dk2_p, dv_p, dg, dr_s, dld, dk2_s, dv_s, das, dbs)


def _norm_bwd(x, dxo, dh, dhprev, mod, dgate):
    bl, t, d = x.shape
    tm = min(t, RWKV_TM)
    hb = tm // 8
    last_blk = t // 8 - 1

    def body(x_ref, dxo_ref, dh_ref, dhp_ref, nxt_ref, mod_ref, dgate_ref, dx_ref, dmod_ref):
        ti = pl.program_id(1)
        xn, inv = _rms(x_ref[0])
        last = jnp.where(ti == t // tm - 1, 0.0, nxt_ref[0, 0:1])
        dh = dh_ref[0] + _shift_up(dhp_ref[0], last)
        dx_ref[0] = dxo_ref[0] + _rms_bwd(xn, inv, dh * (1.0 + mod_ref[0, 1:2]))

        @pl.when(ti == 0)
        def _():
            dmod_ref[0] = dgate_ref[0]

        dmod_ref[0, 0:1] += _colsum(dh)
        dmod_ref[0, 1:2] += _colsum(dh * xn)

    nxt = pl.BlockSpec((1, 8, d), lambda e, i: (e, jnp.minimum((i + 1) * hb, last_blk), 0))
    return pl.pallas_call(
        body, name="norm_bwd", grid=(bl, t // tm),
        in_specs=[_tok(tm, d)] * 4 + [nxt, _per_example(8, d), _per_example(8, d)],
        out_specs=(_tok(tm, d), _per_example(8, d)),
        out_shape=(_sds(x.shape), _sds((bl, 8, d))),
        compiler_params=_cparams(("arbitrary", "arbitrary")),
    )(x, dxo, dh, dhprev, dhprev, mod, dgate)


def _final(x, target, final_g):
    bl, t, d = x.shape
    tm = min(t, 512)

    def body(x_ref, tgt_ref, g_ref, dx_ref, loss_ref, dg_ref):
        e, ti = pl.program_id(0), pl.program_id(1)

        @pl.when(_first(e, ti))
        def _():
            loss_ref[...] = jnp.zeros_like(loss_ref)
            dg_ref[...] = jnp.zeros_like(dg_ref)

        xn, inv = _rms(x_ref[0])
        err = xn * g_ref[...] - tgt_ref[0]
        loss_ref[...] += (0.5 / d) * jnp.sum(err * err)
        dy = err * (1.0 / d)
        dg_ref[0:1] += _colsum(dy * xn)
        dx_ref[0] = _rms_bwd(xn, inv, dy * g_ref[...])

    return pl.pallas_call(
        body, name="final_loss", grid=(bl, t // tm),
        in_specs=[_tok(tm, d), _tok(tm, d), _whole(final_g.shape)],
        out_specs=(_tok(tm, d), _whole((8, 128)), _whole((8, d))),
        out_shape=(_sds(x.shape), _sds((8, 128)), _sds((8, d))),
        compiler_params=_cparams(("arbitrary", "arbitrary")),
    )(x, target, final_g)


def _adamw_math(w, g, m, v):
    m = ADAM_B1 * m + (1.0 - ADAM_B1) * g
    v = ADAM_B2 * v + (1.0 - ADAM_B2) * jnp.square(g)
    m_hat = m / (1.0 - ADAM_B1 ** ADAM_STEP)
    v_hat = v / (1.0 - ADAM_B2 ** ADAM_STEP)
    return -ADAM_LR * (m_hat / (jnp.sqrt(v_hat) + ADAM_EPS) + ADAM_WD * w), m, v


def _sum_parts(ref, n):
    g = ref[0].astype(F32)
    for s in range(1, n):
        g = g + ref[s].astype(F32)
    return g


def _adamw_rows(w, m, v, parts, row0, name):
    rows, c = w.shape
    n = parts.shape[0]
    tr = 128
    blk0 = row0 // tr

    def body(w_ref, m_ref, v_ref, p_ref, g_ref, d_ref, mo_ref, vo_ref):
        g = _sum_parts(p_ref, n)
        g_ref[...] = g
        d_ref[...], mo_ref[...], vo_ref[...] = _adamw_math(w_ref[...], g, m_ref[...], v_ref[...])

    row = pl.BlockSpec((tr, c), lambda i: (i, 0))
    return pl.pallas_call(
        body, name=name, grid=(rows // tr,),
        in_specs=[row, row, row, pl.BlockSpec((n, tr, c), lambda i: (0, blk0 + i, 0))],
        out_specs=(row,) * 4, out_shape=(_sds(w.shape),) * 4,
        compiler_params=_cparams(("parallel",)),
    )(w, m, v, parts)


def _adamw_small(items, name):
    k = len(items)
    ns = [it[3].shape[0] for it in items]

    def body(*refs):
        ins, outs = refs[:4 * k], refs[4 * k:]
        for i in range(k):
            w_ref, m_ref, v_ref, p_ref = ins[4 * i:4 * i + 4]
            g = _sum_parts(p_ref, ns[i])
            outs[4 * i][...] = g
            outs[4 * i + 1][...], outs[4 * i + 2][...], outs[4 * i + 3][...] = _adamw_math(
                w_ref[...], g, m_ref[...], v_ref[...])

    flat = [a for it in items for a in it]
    res = pl.pallas_call(
        body, name=name,
        out_shape=tuple(_sds(it[0].shape) for it in items for _ in range(4)),
        compiler_params=_cparams(),
    )(*flat)
    return [tuple(res[4 * i:4 * i + 4]) for i in range(k)]


WEIGHTS = ['ada_w', 'ada_b', 'mlp_w1', 'mlp_w2', 'a_w_in', 'a_ln_g', 'a_ln_b', 'a_w_s', 'a_b_s', 'a_w_out', 'b_mu',
           'b_w_in', 'b_w0', 'b_w1', 'b_w2', 'b_a0', 'b_a1', 'b_a2', 'b_g1', 'b_g2', 'b_k_k', 'b_k_a', 'b_r_k',
           'b_ln_g', 'b_ln_b', 'b_w_out', 'final_g']
GATHERED = [('mlp_w1', 2), ('mlp_w2', 1), ('a_w_in', 2), ('a_w_out', 1), ('b_w_in', 2), ('b_w_out', 1),
            ('b_w1', 1), ('b_a1', 1), ('b_g1', 1), ('b_w2', 2), ('b_a2', 2), ('b_g2', 2)]
VECTORS = ['b_mu', 'b_w0', 'b_a0', 'b_k_k', 'b_k_a', 'b_ln_g', 'b_ln_b']
REPLICATED = ['a_ln_g', 'a_ln_b', 'a_w_s', 'a_b_s', 'b_r_k', 'final_g']
ROW_ALIGN = 16


def _pad_rows(a, mult):
    pad = (-a.shape[-2]) % mult
    return jnp.pad(a, [(0, 0)] * (a.ndim - 2) + [(0, pad), (0, 0)]) if pad else a


def _as2d(a):
    if a.ndim == 1:
        return a.reshape(1, -1)
    lead = 1
    for s in a.shape[:-1]:
        lead *= s
    return a.reshape(lead, a.shape[-1])


def kernel(x, c, ada_w, ada_b, mlp_w1, mlp_w2, a_w_in, a_ln_g, a_ln_b, a_w_s, a_b_s, a_w_out, b_mu, b_w_in, b_w0, b_w1, b_w2, b_a0, b_a1, b_a2, b_g1, b_g2, b_k_k, b_k_a, b_r_k, b_ln_g, b_ln_b, b_w_out, final_g, loss_target, m_ada_w, m_ada_b, m_mlp_w1, m_mlp_w2, m_a_w_in, m_a_ln_g, m_a_ln_b, m_a_w_s, m_a_b_s, m_a_w_out, m_b_mu, m_b_w_in, m_b_w0, m_b_w1, m_b_w2, m_b_a0, m_b_a1, m_b_a2, m_b_g1, m_b_g2, m_b_k_k, m_b_k_a, m_b_r_k, m_b_ln_g, m_b_ln_b, m_b_w_out, m_final_g, v_ada_w, v_ada_b, v_mlp_w1, v_mlp_w2, v_a_w_in, v_a_ln_g, v_a_ln_b, v_a_w_s, v_a_b_s, v_a_w_out, v_b_mu, v_b_w_in, v_b_w0, v_b_w1, v_b_w2, v_b_a0, v_b_a1, v_b_a2, v_b_g1, v_b_g2, v_b_k_k, v_b_k_a, v_b_r_k, v_b_ln_g, v_b_ln_b, v_b_w_out, v_final_g):
    given = dict(locals())
    w = {n: given[n] for n in WEIGHTS}
    bl, t, d = x.shape
    nl = ada_w.shape[0]
    nb = N_DEV * bl
    m_tok = bl * t
    me = 4 * lax.axis_index("x") + 2 * lax.axis_index("y") + lax.axis_index("c")

    c_all = _exchange(c, "gather_c", False).reshape(nb, d)
    cols = ada_w.shape[2]
    ada_b_cols = lax.dynamic_slice(ada_b, (0, me * cols), (nl, cols)).reshape(nl, 1, cols)
    mod_cols = _ada_fwd(c_all, ada_w, ada_b_cols)
    mod_full = jnp.moveaxis(_exchange(mod_cols, "gather_mod", False), 0, 2).reshape(nl, nb, 6 * d)
    mod_mine = lax.dynamic_slice(mod_full, (0, me * bl, 0), (nl, bl, 6 * d)).reshape(nl, bl, 6, d)
    mod_mix = jnp.pad(mod_mine[:, :, 0:3], ((0, 0), (0, 0), (0, 5), (0, 0)))
    mod_mlp = jnp.pad(mod_mine[:, :, 3:6], ((0, 0), (0, 0), (0, 5), (0, 0)))

    rows = [w[n].size // d for n, _ in GATHERED]
    offs = [sum(rows[:i]) for i in range(len(rows))]
    n_rows = sum(rows)
    pack = _pad_rows(jnp.concatenate([w[n].reshape(-1, d) for n, _ in GATHERED], axis=0).astype(BF16), ROW_ALIGN)
    gathered = _exchange(pack, "gather_weights", False, relay=True)
    full = {}
    for (n, ax), off, nr in zip(GATHERED, offs, rows):
        loc = w[n].shape
        g = jnp.moveaxis(gathered[:, off:off + nr].reshape((N_DEV,) + loc), 0, ax)
        full[n] = g.reshape(loc[:ax] + (N_DEV * loc[ax],) + loc[ax + 1:])
    vec_loc = _pad_rows(jnp.concatenate([_as2d(w[n]) for n in VECTORS], axis=0), ROW_ALIGN)
    n_vec_rows = sum(_as2d(w[n]).shape[0] for n in VECTORS)
    vec = jnp.moveaxis(_exchange(vec_loc, "gather_vectors", False), 0, 1).reshape(N_VEC, d)
    vec = vec.at[n_vec_rows].set(b_r_k.reshape(d))

    e_ind, et_ind = _head_indicators(d)
    gd = d // SGU_GROUPS
    group_ind = (jnp.arange(SGU_GROUPS)[:, None] == jnp.arange(d)[None, :] // gd).astype(BF16)
    bias_full = jnp.repeat(a_b_s[0].T, gd, axis=1)
    pad_c = lambda a, n: jnp.pad(a, ((0, 0), (0, n - a.shape[1])))
    pad_r = lambda a, n: jnp.pad(a, ((0, n - a.shape[0]), (0, 0)))
    rwkv_w = (full['b_w_in'][0], pad_c(full['b_w1'][0], LORA_PAD), pad_c(full['b_a1'][0], LORA_PAD),
              pad_c(full['b_g1'][0], GATE_PAD), pad_r(full['b_w2'][0], LORA_PAD), pad_r(full['b_a2'][0], LORA_PAD),
              pad_r(full['b_g2'][0], GATE_PAD))
    sgu_args = (full['a_w_in'][0], a_ln_g, a_ln_b, a_w_s[0], bias_full, full['a_w_out'][0])

    x0 = x
    x1, mix_a = _sgu_fwd(x0, mod_mix[0], *sgu_args)
    x2, ff0 = _mlp_fwd(x1, mod_mlp[0], full['mlp_w1'][0], full['mlp_w2'][0])
    r, ld, k2, v, a_s, b_s, gate = _rwkv_pre_fwd(x2, mod_mix[1], vec, e_ind, et_ind, rwkv_w)
    y, s0 = _wkv_fwd(r, ld, k2, v, a_s, b_s)
    x3, mix_b = _rwkv_post_fwd(x2, y, r, k2, v, gate, mod_mix[1], vec, e_ind, et_ind, full['b_w_out'][0])
    x4, ff1 = _mlp_fwd(x3, mod_mlp[1], full['mlp_w1'][1], full['mlp_w2'][1])
    dx4, loss_blk, dfinal = _final(x4, loss_target, final_g.reshape(1, d))
    loss = lax.psum(loss_blk[0, 0], ("x", "y", "c"))

    tok = lambda a: a.reshape(m_tok, a.shape[-1])
    grads = {}
    dx3, dmod_mlp1, h_b, dff_b, q_b, dp_b = _mlp_bwd(x3, dx4, ff1, mod_mlp[1], full['mlp_w1'][1], full['mlp_w2'][1])
    gw1_1 = _matmul_tn(tok(h_b), tok(dp_b), "grad_mlp_w1_l1")
    gw2_1 = _matmul_tn(tok(q_b), tok(dff_b), "grad_mlp_w2_l1")
    dy, dr_p, dk2_p, dv_p, dgate_act, o_b, dmix_b, dgate_b, small_post = _rwkv_post_bwd(
        dx3, mix_b, y, r, k2, v, gate, mod_mix[1], vec, e_ind, et_ind, full['b_w_out'][0])
    grads['b_w_out'] = _matmul_tn(tok(o_b), tok(dmix_b), "grad_b_w_out")[None]
    dr_s, dld, dk2_s, dv_s, das, dbs = _wkv_bwd(r, ld, k2, v, a_s, b_s, s0, dy)
    (dh, dhp, xr_b, xw_b, xk_b, xv_b, xa_b, xg_b, dr_b, dk_b, dv_b, dtw_b, dta_b, dtg_b, tw2_b, ta_b, sg_b,
     dlw_b, dla_b, dg_b, small_pre) = _rwkv_pre_bwd(x2, mod_mix[1], vec, e_ind, et_ind, rwkv_w,
                                                    dr_p, dk2_p, dv_p, dgate_act, dr_s, dld, dk2_s, dv_s, das, dbs)
    grads['b_w_in'] = jnp.concatenate([_matmul_tn(tok(xr_b), tok(dr_b), "grad_b_w_r"),
                                       _matmul_tn(tok(xk_b), tok(dk_b), "grad_b_w_k"),
                                       _matmul_tn(tok(xv_b), tok(dv_b), "grad_b_w_v")], axis=1)[None]
    lw_, lg_ = b_w1.shape[2], b_g1.shape[2]
    grads['b_w1'] = _matmul_tn(tok(xw_b), tok(dtw_b), "grad_b_w1")[None, :, :lw_]
    grads['b_a1'] = _matmul_tn(tok(xa_b), tok(dta_b), "grad_b_a1")[None, :, :lw_]
    grads['b_g1'] = _matmul_tn(tok(xg_b), tok(dtg_b), "grad_b_g1")[None, :, :lg_]
    grads['b_w2'] = _matmul_tn(tok(tw2_b), tok(dlw_b), "grad_b_w2")[None, :lw_]
    grads['b_a2'] = _matmul_tn(tok(ta_b), tok(dla_b), "grad_b_a2")[None, :lw_]
    grads['b_g2'] = _matmul_tn(tok(sg_b), tok(dg_b), "grad_b_g2")[None, :lg_]
    dx2, dmod_mix1 = _norm_bwd(x2, dx3, dh, dhp, mod_mix[1], dgate_b)
    dx1, dmod_mlp0, h_b, dff_b, q_b, dp_b = _mlp_bwd(x1, dx2, ff0, mod_mlp[0], full['mlp_w1'][0], full['mlp_w2'][0])
    gw1_0 = _matmul_tn(tok(h_b), tok(dp_b), "grad_mlp_w1_l0")
    gw2_0 = _matmul_tn(tok(q_b), tok(dff_b), "grad_mlp_w2_l0")
    grads['mlp_w1'] = jnp.stack([gw1_0, gw1_1])
    grads['mlp_w2'] = jnp.stack([gw2_0, gw2_1])
    dx0, dmod_mix0, h_b, dpre_b, z_b, dmix_b, small_sgu, d_ws, d_bs = _sgu_bwd(
        x0, dx1, mix_a, mod_mix[0], *sgu_args, group_ind)
    grads['a_w_in'] = _matmul_tn(tok(h_b), tok(dpre_b), "grad_a_w_in")[None]
    grads['a_w_out'] = _matmul_tn(tok(z_b), tok(dmix_b), "grad_a_w_out")[None]

    dmod_mine = jnp.stack([jnp.concatenate([dmod_mix0[:, 0:3], dmod_mlp0[:, 0:3]], axis=1),
                           jnp.concatenate([dmod_mix1[:, 0:3], dmod_mlp1[:, 0:3]], axis=1)], axis=1)
    dmod_all = _exchange(dmod_mine.reshape(bl, nl * 6 * d), "gather_dmod", False)
    dmod_all = jnp.moveaxis(dmod_all.reshape(nb, nl, 6 * d), 0, 1)
    dmod_cols = lax.dynamic_slice(dmod_all, (0, 0, me * cols), (nl, nb, cols))
    g_ada_w, g_ada_b = _ada_bwd(c_all, dmod_cols, dmod_all)

    vec_g = jnp.concatenate([small_pre[0:10], small_post[0:2]], axis=0)
    vec_parts = jnp.moveaxis(vec_g.reshape(n_vec_rows, N_DEV, d // N_DEV), 1, 0).reshape(N_DEV, -1)
    vec_rows = -(-vec_parts.shape[1] // d)
    vec_parts = jnp.pad(vec_parts, ((0, 0), (0, vec_rows * d - vec_parts.shape[1]))).reshape(N_DEV, vec_rows, d)
    vec_parts = vec_parts.astype(BF16)
    parts = []
    for n, ax in GATHERED:
        loc = w[n].shape
        g = grads[n].reshape(loc[:ax] + (N_DEV, loc[ax]) + loc[ax + 1:])
        parts.append(jnp.moveaxis(g, ax, 0).reshape(N_DEV, -1, d))
    gpack = _pad_rows(jnp.concatenate(parts + [vec_parts], axis=1), ROW_ALIGN)
    recv = _exchange(gpack, "scatter_grads", True)

    rep_g = {'a_ln_g': small_sgu[0:1], 'a_ln_b': small_sgu[1:2], 'a_w_s': d_ws.reshape(-1, d), 'a_b_s': d_bs.reshape(1, d),
             'b_r_k': small_post[2:3], 'final_g': dfinal[0:1]}
    rep_rows = [rep_g[n].shape[0] for n in REPLICATED]
    rep_pack = _pad_rows(jnp.concatenate([rep_g[n] for n in REPLICATED], axis=0), 8)
    rep_all = _exchange(rep_pack, "gather_replicated_grads", False, relay=True)

    mom = {n: given['m_' + n] for n in WEIGHTS}
    var = {n: given['v_' + n] for n in WEIGHTS}
    out = {}
    big = GATHERED[:6]
    for (n, _), off in zip(big, offs):
        res = _adamw_rows(w[n].reshape(-1, d), mom[n].reshape(-1, d), var[n].reshape(-1, d), recv, off, "adamw_" + n)
        out[n] = tuple(a.reshape(w[n].shape) for a in res)
    res = _adamw_rows(ada_w.reshape(-1, d), m_ada_w.reshape(-1, d), v_ada_w.reshape(-1, d),
                      g_ada_w.reshape(1, -1, d), 0, "adamw_ada_w")
    out['ada_w'] = tuple(a.reshape(ada_w.shape) for a in res)

    items, names = [], []

    def add(n, part):
        s2 = _as2d(w[n]).shape
        items.append((_as2d(w[n]), _as2d(mom[n]), _as2d(var[n]), part.reshape((part.shape[0],) + s2)))
        names.append(n)

    for (n, _), off, nr in list(zip(GATHERED, offs, rows))[6:]:
        add(n, recv[:, off:off + nr])
    vflat = recv[:, n_rows:n_rows + vec_rows].reshape(N_DEV, -1)
    vo = 0
    for n in VECTORS:
        sz = w[n].size
        add(n, vflat[:, vo:vo + sz])
        vo += sz
    ro = 0
    for n, nr in zip(REPLICATED, rep_rows):
        add(n, rep_all[:, ro:ro + nr])
        ro += nr
    add('ada_b', g_ada_b[None])
    for n, res in zip(names, _adamw_small(items, "adamw_small")):
        out[n] = tuple(a.reshape(w[n].shape) for a in res)

    return (loss, dx0, *[out[n][0] for n in WEIGHTS], *[out[n][1] for n in WEIGHTS],
            *[out[n][2] for n in WEIGHTS], *[out[n][3] for n in WEIGHTS])
```

```python
import functools

import jax
import jax.numpy as jnp
from jax import lax
from jax.experimental import pallas as pl
from jax.experimental.pallas import tpu as pltpu

F32 = jnp.float32
BF16 = jnp.bfloat16

N_DEV = 8
RMS_EPS = 1e-6
LN_EPS = 1e-5
HEAD = 64
GN_EPS = HEAD * 1e-5
L2_EPS = 1e-12
SGU_CHUNK = 128
SGU_GROUPS = 8
WKV_CHUNK = 64
WKV_HEADS_PER_STEP = 8
LORA_PAD = 128
GATE_PAD = 256
ADAM_LR, ADAM_B1, ADAM_B2, ADAM_EPS, ADAM_WD, ADAM_STEP = 0.001, 0.9, 0.999, 1e-08, 0.01, 10
VMEM_LIMIT = 56 * 1024 * 1024


def _cparams(sem=None, **kw):
    if sem is not None:
        kw["dimension_semantics"] = sem
    return pltpu.CompilerParams(vmem_limit_bytes=VMEM_LIMIT, **kw)


def _dot(a, b):
    return jnp.dot(a.astype(BF16), b.astype(BF16), preferred_element_type=F32)


def _dot_nt(a, b):
    return lax.dot_general(a.astype(BF16), b.astype(BF16), (((1,), (1,)), ((), ())), preferred_element_type=F32)


def _dot_tn(a, b):
    return lax.dot_general(a.astype(BF16), b.astype(BF16), (((0,), (0,)), ((), ())), preferred_element_type=F32)


def _bdot(a, b, dims):
    return lax.dot_general(a.astype(BF16), b.astype(BF16), (dims, ((0,), (0,))), preferred_element_type=F32)


@jax.custom_vjp
def _tri_sum(tri, tri_t, x):
    hi = x.astype(BF16)
    lo = (x - hi.astype(F32)).astype(BF16)
    dn = (((2,), (1,)), ((0,), (0,)))
    return (lax.dot_general(tri, hi, dn, preferred_element_type=F32)
            + lax.dot_general(tri, lo, dn, preferred_element_type=F32))


_tri_sum.defvjp(lambda tri, tri_t, x: (_tri_sum(tri, tri_t, x), (tri, tri_t)),
                lambda res, g: (jnp.zeros_like(res[0]), jnp.zeros_like(res[1]), _tri_sum(res[1], res[0], g)))


@jax.custom_vjp
def _bmm_nn(a, b):
    return _bdot(a, b, ((2,), (1,)))


@jax.custom_vjp
def _bmm_nt(a, b):
    return _bdot(a, b, ((2,), (2,)))


@jax.custom_vjp
def _bmm_tn(a, b):
    return _bdot(a, b, ((1,), (1,)))


_bmm_nn.defvjp(lambda a, b: (_bmm_nn(a, b), (a, b)), lambda res, g: (_bmm_nt(g, res[1]), _bmm_tn(res[0], g)))
_bmm_nt.defvjp(lambda a, b: (_bmm_nt(a, b), (a, b)), lambda res, g: (_bmm_nn(g, res[1]), _bmm_tn(g, res[0])))
_bmm_tn.defvjp(lambda a, b: (_bmm_tn(a, b), (a, b)), lambda res, g: (_bmm_nt(res[1], g), _bmm_nn(res[0], g)))


def _tri_solve_fwd(p, rhs):
    n = p.shape[1]
    row = lax.broadcasted_iota(jnp.int32, (n, n), 0)
    col = lax.broadcasted_iota(jnp.int32, (n, n), 1)
    tinv = jnp.where(row == col, 1.0, 0.0).astype(F32)[None] + p
    for _ in range(max(1, (n - 1).bit_length()) - 1):
        p = _bmm_nn(p, p)
        tinv = tinv + _bmm_nn(tinv, p)
    u = _bmm_nn(tinv, rhs)
    return u, (tinv, u)


def _tri_solve_bwd(res, du):
    tinv, u = res
    drhs = _bmm_tn(tinv, du)
    return _bmm_nt(drhs, u), drhs


@jax.custom_vjp
def _tri_solve(p, rhs):
    return _tri_solve_fwd(p, rhs)[0]


_tri_solve.defvjp(_tri_solve_fwd, _tri_solve_bwd)


def _wkv_chunk(s0, r, ld, k, v, a, b):
    nh, n, _ = r.shape
    row = lax.broadcasted_iota(jnp.int32, (n, n), 0)
    col = lax.broadcasted_iota(jnp.int32, (n, n), 1)
    incl = row >= col
    strict = row > col
    lower = jnp.broadcast_to(jnp.where(incl, 1.0, 0.0).astype(BF16), (nh, n, n))
    upper = jnp.broadcast_to(jnp.where(row <= col, 1.0, 0.0).astype(BF16), (nh, n, n))
    c = _tri_sum(lower, upper, ld)
    c_end = c[:, n - 1:n, :]
    ec, enc, ecx, eend = jnp.exp(c), jnp.exp(-c), jnp.exp(c - ld), jnp.exp(c_end - c)
    ar = jnp.concatenate([a * ecx, r * ec], axis=1)
    mask = jnp.concatenate([strict, incl], axis=0)[None]
    m_b = jnp.where(mask, _bmm_nt(ar, b * enc), 0.0)
    m_k = jnp.where(mask, _bmm_nt(ar, k * enc), 0.0)
    a_ab, a_rb = m_b[:, :n], m_b[:, n:]
    base = _bmm_nt(ar, s0) + _bmm_nn(m_k, v)
    u = _tri_solve(a_ab, base[:, :n])
    y = base[:, n:] + _bmm_nn(a_rb, u)
    s1 = s0 * jnp.exp(c_end) + _bmm_tn(jnp.concatenate([u, v], axis=1), jnp.concatenate([b * eend, k * eend], axis=1))
    return y, s1


def _wkv_specs(bl, nh, t):
    hb, lc = WKV_HEADS_PER_STEP, WKV_CHUNK
    return hb, lc, (bl, nh // hb, t // lc)


def _wkv_fwd(r, ld, k, v, a, b):
    bl, nh, t, n = r.shape
    hb, lc, grid = _wkv_specs(bl, nh, t)
    nc = t // lc

    def body(r_ref, ld_ref, k_ref, v_ref, a_ref, b_ref, y_ref, s0_ref, s_scr):
        @pl.when(pl.program_id(2) == 0)
        def _():
            s_scr[...] = jnp.zeros_like(s_scr)

        s0 = s_scr[...]
        s0_ref[0, :, 0] = s0
        y, s1 = _wkv_chunk(s0, r_ref[0], ld_ref[0], k_ref[0], v_ref[0], a_ref[0], b_ref[0])
        y_ref[0] = y
        s_scr[...] = s1

    seq = pl.BlockSpec((1, hb, lc, n), lambda e, h, c: (e, h, c, 0))
    return pl.pallas_call(
        body, name="wkv_fwd", grid=grid,
        in_specs=[seq] * 6,
        out_specs=(seq, pl.BlockSpec((1, hb, 1, n, n), lambda e, h, c: (e, h, c, 0, 0))),
        out_shape=(jax.ShapeDtypeStruct((bl, nh, t, n), F32), jax.ShapeDtypeStruct((bl, nh, nc, n, n), F32)),
        scratch_shapes=[pltpu.VMEM((hb, n, n), F32)],
        compiler_params=_cparams(("arbitrary", "arbitrary", "arbitrary")),
    )(r, ld, k, v, a, b)


def _wkv_bwd(r, ld, k, v, a, b, s0_all, dy):
    bl, nh, t, n = r.shape
    hb, lc, grid = _wkv_specs(bl, nh, t)
    nc = t // lc

    def body(r_ref, ld_ref, k_ref, v_ref, a_ref, b_ref, s0_ref, dy_ref,
             dr_ref, dld_ref, dk_ref, dv_ref, da_ref, db_ref, ds_scr):
        @pl.when(pl.program_id(2) == 0)
        def _():
            ds_scr[...] = jnp.zeros_like(ds_scr)

        args = (s0_ref[0, :, 0], r_ref[0], ld_ref[0], k_ref[0], v_ref[0], a_ref[0], b_ref[0])
        _, vjp = jax.vjp(_wkv_chunk, *args)
        ds0, dr, dld, dk, dv, da, db = vjp((dy_ref[0], ds_scr[...]))
        ds_scr[...] = ds0
        dr_ref[0], dld_ref[0], dk_ref[0], dv_ref[0], da_ref[0], db_ref[0] = dr, dld, dk, dv, da, db

    seq = pl.BlockSpec((1, hb, lc, n), lambda e, h, c: (e, h, nc - 1 - c, 0))
    st = pl.BlockSpec((1, hb, 1, n, n), lambda e, h, c: (e, h, nc - 1 - c, 0, 0))
    out = jax.ShapeDtypeStruct((bl, nh, t, n), F32)
    return pl.pallas_call(
        body, name="wkv_bwd", grid=grid,
        in_specs=[seq] * 6 + [st, seq],
        out_specs=(seq,) * 6, out_shape=(out,) * 6,
        scratch_shapes=[pltpu.VMEM((hb, n, n), F32)],
        compiler_params=_cparams(("arbitrary", "arbitrary", "arbitrary")),
    )(r, ld, k, v, a, b, s0_all, dy)


def _exchange(x, name, scatter, relay=False):
    assert not (relay and scatter)
    blk = x.shape[1:] if scatter else x.shape

    def body(x_ref, o_ref, send_sems, recv_sems, local_sem):
        pos = (lax.axis_index("x"), lax.axis_index("y"), lax.axis_index("c"))
        me = 4 * pos[0] + 2 * pos[1] + pos[2]

        def peer_of(m):
            p = tuple(1 - pos[i] if (m >> (2 - i)) & 1 else pos[i] for i in range(3))
            return p, 4 * p[0] + 2 * p[1] + p[2]

        def copy(m):
            p, pidx = peer_of(m)
            return pltpu.make_async_remote_copy(
                src_ref=x_ref.at[pidx] if scatter else x_ref, dst_ref=o_ref.at[me],
                send_sem=send_sems.at[m - 1], recv_sem=recv_sems.at[m - 1],
                device_id=p, device_id_type=pl.DeviceIdType.MESH)

        def arrival(m):
            p, pidx = peer_of(m)
            return pltpu.make_async_remote_copy(
                src_ref=x_ref.at[pidx] if scatter else x_ref, dst_ref=o_ref.at[pidx],
                send_sem=send_sems.at[m - 1], recv_sem=recv_sems.at[m - 1],
                device_id=p, device_id_type=pl.DeviceIdType.MESH)

        mine = pltpu.make_async_copy(x_ref.at[me] if scatter else x_ref, o_ref.at[me], local_sem)
        mine.start()
        if not relay:
            sends = [copy(m) for m in range(1, N_DEV)]
            for cp in sends:
                cp.start()
            for m in range(1, N_DEV):
                arrival(m).wait_recv()
        else:
            sibling, _ = peer_of(1)
            far = (2, 4, 6)

            def relay_copy(m, origin_idx):
                return pltpu.make_async_remote_copy(
                    src_ref=o_ref.at[origin_idx], dst_ref=o_ref.at[origin_idx],
                    send_sem=send_sems.at[m], recv_sem=recv_sems.at[m],
                    device_id=sibling, device_id_type=pl.DeviceIdType.MESH)

            sends = [copy(1)] + [copy(m) for m in far]
            for cp in sends:
                cp.start()
            for m in far:
                arrival(m).wait_recv()
                fwd = relay_copy(m, peer_of(m)[1])
                fwd.start()
                sends.append(fwd)
            arrival(1).wait_recv()
            for m in far:
                relay_copy(m, peer_of(m ^ 1)[1]).wait_recv()
        for cp in sends:
            cp.wait_send()
        mine.wait()

    return pl.pallas_call(
        body, name=name,
        out_shape=jax.ShapeDtypeStruct((N_DEV,) + tuple(blk), x.dtype),
        in_specs=[pl.BlockSpec(memory_space=pl.ANY)],
        out_specs=pl.BlockSpec(memory_space=pl.ANY),
        scratch_shapes=[pltpu.SemaphoreType.DMA((N_DEV - 1,)), pltpu.SemaphoreType.DMA((N_DEV - 1,)),
                        pltpu.SemaphoreType.DMA],
    )(x)


def _scatter_copies(x_refs, o_refs, send_sems, recv_sems, local_sems):
    pos = (lax.axis_index("x"), lax.axis_index("y"), lax.axis_index("c"))
    me = 4 * pos[0] + 2 * pos[1] + pos[2]

    def descriptors():
        sends, arrivals, local = [], [], []
        for i, (x_ref, o_ref) in enumerate(zip(x_refs, o_refs)):
            for m in range(1, N_DEV):
                p = tuple(1 - pos[a] if (m >> (2 - a)) & 1 else pos[a] for a in range(3))
                pidx = 4 * p[0] + 2 * p[1] + p[2]
                k = (N_DEV - 1) * i + m - 1
                for dst, out in ((o_ref.at[me], sends), (o_ref.at[pidx], arrivals)):
                    out.append(pltpu.make_async_remote_copy(
                        src_ref=x_ref.at[pidx], dst_ref=dst, send_sem=send_sems.at[k], recv_sem=recv_sems.at[k],
                        device_id=p, device_id_type=pl.DeviceIdType.MESH))
            local.append(pltpu.make_async_copy(x_ref.at[me], o_ref.at[me], local_sems.at[i]))
        return sends, arrivals, local

    def start():
        sends, _, local = descriptors()
        for cp in local + sends:
            cp.start()

    def finish():
        sends, arrivals, local = descriptors()
        for cp in arrivals:
            cp.wait_recv()
        for cp in sends:
            cp.wait_send()
        for cp in local:
            cp.wait()

    return start, finish


def _scatter_scratch(n):
    return [pltpu.SemaphoreType.DMA(((N_DEV - 1) * n,)), pltpu.SemaphoreType.DMA(((N_DEV - 1) * n,)),
            pltpu.SemaphoreType.DMA((n,))]


_ANY = pl.BlockSpec(memory_space=pl.ANY)


def _scatter_call(arrays, name):
    n = len(arrays)

    def body(*refs):
        start, finish = _scatter_copies(refs[:n], refs[n:2 * n], *refs[2 * n:])
        start()
        finish()

    return pl.pallas_call(
        body, name=name, in_specs=[_ANY] * n, out_specs=(_ANY,) * n,
        out_shape=tuple(_sds(a.shape, a.dtype) for a in arrays), scratch_shapes=_scatter_scratch(n),
    )(*arrays)


def _call_with_scatter(body, *, name, grid, in_specs, out_specs, out_shape, scratch_shapes, operands, scatter):
    nc, n_in, n_out, n_scr = len(scatter), len(in_specs), len(out_specs), len(scratch_shapes)
    if nc == 0:
        return pl.pallas_call(
            body, name=name, grid=grid, in_specs=list(in_specs), out_specs=tuple(out_specs),
            out_shape=tuple(out_shape), scratch_shapes=list(scratch_shapes),
            compiler_params=_cparams(("arbitrary",) * len(grid)))(*operands), ()

    def wrapped(*refs):
        ins, refs = refs[:n_in], refs[n_in:]
        c_in, refs = refs[:nc], refs[nc:]
        outs, refs = refs[:n_out], refs[n_out:]
        c_out, refs = refs[:nc], refs[nc:]
        scr, sems = refs[:n_scr], refs[n_scr:]
        ids = [pl.program_id(a) for a in range(len(grid))]
        first = functools.reduce(jnp.logical_and, [i == 0 for i in ids])
        last = functools.reduce(jnp.logical_and, [i == g - 1 for i, g in zip(ids, grid)])
        start, finish = _scatter_copies(c_in, c_out, *sems)
        pl.when(first)(start)
        body(*ins, *outs, *scr)
        pl.when(last)(finish)

    res = pl.pallas_call(
        wrapped, name=name, grid=grid,
        in_specs=list(in_specs) + [_ANY] * nc, out_specs=tuple(out_specs) + (_ANY,) * nc,
        out_shape=tuple(out_shape) + tuple(_sds(a.shape, a.dtype) for a in scatter),
        scratch_shapes=list(scratch_shapes) + _scatter_scratch(nc),
        compiler_params=_cparams(("arbitrary",) * len(grid)),
    )(*operands, *scatter)
    return res[:n_out], res[n_out:]


def _rms(x):
    inv = lax.rsqrt(jnp.mean(x * x, axis=-1, keepdims=True) + RMS_EPS)
    return x * inv, inv


def _rms_bwd(xn, inv, dxn):
    return inv * (dxn - xn * jnp.mean(dxn * xn, axis=-1, keepdims=True))


def _colsum(x):
    return jnp.sum(x, axis=0, keepdims=True)


def _sigmoid(x):
    return 0.5 * (jnp.tanh(0.5 * x) + 1.0)


def _split_bf16(x):
    hi = x.astype(BF16)
    return hi, (x - hi.astype(F32)).astype(BF16)


def _dot_split(x, e):
    hi, lo = _split_bf16(x)
    return jnp.dot(hi, e, preferred_element_type=F32) + jnp.dot(lo, e, preferred_element_type=F32)


@jax.custom_vjp
def _headsum(x, e, et):
    return _dot_split(_dot_split(x, e), et)


_headsum.defvjp(lambda x, e, et: (_headsum(x, e, et), (e, et)),
                lambda res, g: (_headsum(g, *res), jnp.zeros_like(res[0]), jnp.zeros_like(res[1])))


def _make_headsum(e, et):
    return lambda x: _headsum(x, e, et)


def _head_indicators(d):
    e = (jnp.arange(d)[:, None] // HEAD == jnp.arange(128)[None, :]).astype(BF16)
    return e, e.T


def _rwkv_elem(r, k, lw, la, w0, a0, k_k, k_a, headsum):
    z = w0 + lw
    w_log = -(jnp.maximum(-z, 0.0) + jnp.log(1.0 + jnp.exp(-jnp.abs(z)))) - 0.5
    ld = -jnp.exp(w_log)
    a = _sigmoid(a0 + la)
    kkp = k * k_k
    kk = kkp / jnp.maximum(jnp.sqrt(headsum(kkp * kkp)), L2_EPS)
    k2 = k * (1.0 + (a - 1.0) * k_a)
    del r
    return ld, k2, -kk, kk * a


def _rwkv_post(y, r, k2, v, g, ln_g, ln_b, r_k, headsum):
    m = headsum(y) * (1.0 / HEAD)
    yc = y - m
    var = headsum(yc * yc) * (1.0 / HEAD)
    yn = yc * lax.rsqrt(var + GN_EPS)
    bonus = headsum(r * k2 * r_k) * v
    return (yn * ln_g + ln_b + bonus) * g


def _shift_down(h, first_row):
    rolled = pltpu.roll(h, 1, 0)
    row = lax.broadcasted_iota(jnp.int32, h.shape, 0)
    return jnp.where(row == 0, first_row, rolled)


def _shift_up(h, last_row):
    n = h.shape[0]
    rolled = pltpu.roll(h, n - 1, 0)
    row = lax.broadcasted_iota(jnp.int32, h.shape, 0)
    return jnp.where(row == n - 1, last_row, rolled)


def _gelu(p):
    return 0.5 * p * (1.0 + lax.erf(p * 0.7071067811865476))


def _gelu_grad(p):
    return 0.5 * (1.0 + lax.erf(p * 0.7071067811865476)) + p * jnp.exp(-0.5 * p * p) * 0.3989422804014327


def _tok(tm, d):
    return pl.BlockSpec((1, tm, d), lambda e, t, *_: (e, t, 0))


def _per_example(rows, d):
    return pl.BlockSpec((1, rows, d), lambda e, t, *_: (e, 0, 0))


def _whole(shape):
    nd = len(shape)
    return pl.BlockSpec(tuple(shape), lambda *_: (0,) * nd)


def _heads(nh, tm):
    return pl.BlockSpec((1, nh, tm, HEAD), lambda e, t, *_: (e, 0, t, 0))


def _sds(shape, dtype=F32):
    return jax.ShapeDtypeStruct(tuple(shape), dtype)


def _add_rows(ref, first, rows):
    @pl.when(first)
    def _():
        ref[0] = jnp.zeros(ref.shape[1:], ref.dtype)

    for i, r in enumerate(rows):
        ref[0, i:i + 1] += r


def _first(e, t):
    return jnp.logical_and(e == 0, t == 0)


def _ada_fwd(c_all, ada_w, ada_b_cols):
    nl, d, cols = ada_w.shape
    nb = c_all.shape[0]

    def body(c_ref, w_ref, b_ref, o_ref):
        c = c_ref[...]
        cond = c * _sigmoid(c)
        for i in range(nl):
            o_ref[i] = _dot(cond, w_ref[i]) + b_ref[i]

    return pl.pallas_call(
        body, name="ada_fwd", out_shape=_sds((nl, nb, cols)),
        compiler_params=_cparams(),
    )(c_all, ada_w, ada_b_cols)


def _ada_bwd(c_all, dmod_cols, dmod_full):
    nl, nb, cols = dmod_cols.shape
    d = c_all.shape[1]

    def body(c_ref, g_ref, f_ref, b_ref, *o_refs):
        c = c_ref[...]
        cond = c * _sigmoid(c)
        for i in range(nl):
            o_refs[i][0] = _dot_tn(cond, g_ref[i])
            b_ref[i:i + 1] = jnp.sum(f_ref[i], axis=0, keepdims=True)

    res = pl.pallas_call(
        body, name="ada_bwd", out_shape=(_sds((nl, dmod_full.shape[2])),) + (_sds((1, d, cols)),) * nl,
        compiler_params=_cparams(),
    )(c_all, dmod_cols, dmod_full)
    return res[1:], res[0]


def _matmul_tn(a, b, name, col_shards=None):
    m, ka = a.shape
    n = b.shape[1]
    tm = min(m, 2048)
    tk = min(ka, 1024)
    tn = n // col_shards if col_shards else min(n, 512)
    steps = m // tm
    if col_shards:
        out_spec = pl.BlockSpec((None, tk, tn), lambda i, j, s: (j, i, 0))
        out_shape = _sds((col_shards, ka, tn), BF16)
    else:
        out_spec = pl.BlockSpec((tk, tn), lambda i, j, s: (i, j))
        out_shape = _sds((ka, n), BF16)

    def body(a_ref, b_ref, o_ref, acc):
        s = pl.program_id(2)

        @pl.when(s == 0)
        def _():
            acc[...] = jnp.zeros_like(acc)

        acc[...] += _dot_tn(a_ref[...], b_ref[...])

        @pl.when(s == steps - 1)
        def _():
            o_ref[...] = acc[...].astype(BF16)

    return pl.pallas_call(
        body, name=name, grid=(ka // tk, n // tn, steps),
        in_specs=[pl.BlockSpec((tm, tk), lambda i, j, s: (s, i)), pl.BlockSpec((tm, tn), lambda i, j, s: (s, j))],
        out_specs=out_spec, out_shape=out_shape,
        scratch_shapes=[pltpu.VMEM((tk, tn), F32)],
        compiler_params=_cparams(("parallel", "parallel", "arbitrary")),
    )(a, b)


MLP_TM = 512
MLP_FJ = 1024


def _mlp_fwd(x, mod, w1, w2):
    bl, t, d = x.shape
    f = w1.shape[1]
    tm, fj = min(t, MLP_TM), min(f, MLP_FJ)
    nj = f // fj

    def body(x_ref, mod_ref, w1_ref, w2_ref, xo_ref, ff_ref, h_scr, acc):
        j = pl.program_id(2)

        @pl.when(j == 0)
        def _():
            xn, _ = _rms(x_ref[0])
            h_scr[...] = (xn * (1.0 + mod_ref[0, 1:2]) + mod_ref[0, 0:1]).astype(BF16)
            acc[...] = jnp.zeros_like(acc)

        p = jnp.dot(h_scr[...], w1_ref[...], preferred_element_type=F32)
        q = jnp.square(jnp.maximum(p, 0.0))
        acc[...] += _dot(q, w2_ref[...])

        @pl.when(j == nj - 1)
        def _():
            ff_ref[0] = acc[...]
            xo_ref[0] = x_ref[0] + mod_ref[0, 2:3] * acc[...]

    return pl.pallas_call(
        body, name="mlp_fwd", grid=(bl, t // tm, nj),
        in_specs=[_tok(tm, d), _per_example(8, d),
                  pl.BlockSpec((d, fj), lambda e, i, j: (0, j)), pl.BlockSpec((fj, d), lambda e, i, j: (j, 0))],
        out_specs=(_tok(tm, d), _tok(tm, d)),
        out_shape=(_sds(x.shape), _sds(x.shape)),
        scratch_shapes=[pltpu.VMEM((tm, d), BF16), pltpu.VMEM((tm, d), F32)],
        compiler_params=_cparams(("arbitrary", "arbitrary", "arbitrary")),
    )(x, mod, w1, w2)


def _mlp_bwd(x, dxo, ff, mod, w1, w2, scatter=()):
    bl, t, d = x.shape
    f = w1.shape[1]
    tm, fj = min(t, MLP_TM), min(f, MLP_FJ)
    nj = f // fj

    def body(x_ref, dxo_ref, ff_ref, mod_ref, w1_ref, w2_ref,
             dx_ref, dmod_ref, h_ref, dff_ref, q_ref, dp_ref, acc):
        ti, j = pl.program_id(1), pl.program_id(2)

        @pl.when(j == 0)
        def _():
            xn, _ = _rms(x_ref[0])
            h_ref[0] = (xn * (1.0 + mod_ref[0, 1:2]) + mod_ref[0, 0:1]).astype(BF16)
            dff_ref[0] = (mod_ref[0, 2:3] * dxo_ref[0]).astype(BF16)
            acc[...] = jnp.zeros_like(acc)

        p = jnp.dot(h_ref[0], w1_ref[...], preferred_element_type=F32)
        rl = jnp.maximum(p, 0.0)
        q_ref[0] = jnp.square(rl).astype(BF16)
        dp = (_dot_nt(dff_ref[0], w2_ref[...]) * (2.0 * rl)).astype(BF16)
        dp_ref[0] = dp
        acc[...] += _dot_nt(dp, w1_ref[...])

        @pl.when(j == nj - 1)
        def _():
            xn, inv = _rms(x_ref[0])
            dh = acc[...]
            dx_ref[0] = dxo_ref[0] + _rms_bwd(xn, inv, dh * (1.0 + mod_ref[0, 1:2]))
            _add_rows(dmod_ref, ti == 0, [_colsum(dh), _colsum(dh * xn), _colsum(dxo_ref[0] * ff_ref[0])])

    big = lambda: pl.BlockSpec((1, tm, fj), lambda e, i, j: (e, i, j))
    return _call_with_scatter(
        body, name="mlp_bwd", grid=(bl, t // tm, nj),
        in_specs=[_tok(tm, d), _tok(tm, d), _tok(tm, d), _per_example(8, d),
                  pl.BlockSpec((d, fj), lambda e, i, j: (0, j)), pl.BlockSpec((fj, d), lambda e, i, j: (j, 0))],
        out_specs=(_tok(tm, d), _per_example(8, d), _tok(tm, d), _tok(tm, d), big(), big()),
        out_shape=(_sds(x.shape), _sds((bl, 8, d)), _sds(x.shape, BF16), _sds(x.shape, BF16),
                   _sds((bl, t, f), BF16), _sds((bl, t, f), BF16)),
        scratch_shapes=[pltpu.VMEM((tm, d), F32)],
        operands=(x, dxo, ff, mod, w1, w2), scatter=scatter)


SGU_TM = 256


def _sgu_core(x, mod_ref, win_ref, lng, lnb, ws_ref, bias_ref):
    tm, d = x.shape
    xn, inv = _rms(x)
    h = (xn * (1.0 + mod_ref[0, 1:2]) + mod_ref[0, 0:1]).astype(BF16)
    pre = jnp.dot(h, win_ref[...], preferred_element_type=F32)
    uv = _gelu(pre)
    u, v = uv[:, :d], uv[:, d:]
    mu = jnp.mean(v, axis=-1, keepdims=True)
    vc = v - mu
    rstd = lax.rsqrt(jnp.mean(vc * vc, axis=-1, keepdims=True) + LN_EPS)
    vhat = vc * rstd
    vln = vhat * lng + lnb
    gd = d // SGU_GROUPS
    rows = []
    for c in range(tm // SGU_CHUNK):
        cols = []
        for g in range(SGU_GROUPS):
            cols.append(_dot(ws_ref[g], vln[c * SGU_CHUNK:(c + 1) * SGU_CHUNK, g * gd:(g + 1) * gd]))
        rows.append(jnp.concatenate(cols, axis=1) + bias_ref[...])
    sv = jnp.concatenate(rows, axis=0)
    return xn, inv, h, pre, u, vhat, rstd, vln, sv


def _sgu_masked(ws_ref, wm_scr):
    row = lax.broadcasted_iota(jnp.int32, (SGU_CHUNK, SGU_CHUNK), 0)
    col = lax.broadcasted_iota(jnp.int32, (SGU_CHUNK, SGU_CHUNK), 1)
    for g in range(SGU_GROUPS):
        wm_scr[g] = jnp.where(row >= col, ws_ref[g], 0.0).astype(BF16)


def _sgu_fwd(x, mod, w_in, ln_g, ln_b, w_s, bias_full, w_out):
    bl, t, d = x.shape
    tm = min(t, SGU_TM)

    def body(x_ref, mod_ref, win_ref, lng_ref, lnb_ref, ws_ref, bias_ref, wout_ref, xo_ref, mix_ref, wm_scr):
        _sgu_masked(ws_ref, wm_scr)
        xt = x_ref[0]
        *_, u, _, _, _, sv = _sgu_core(xt, mod_ref, win_ref, lng_ref[...], lnb_ref[...], wm_scr, bias_ref)
        mix = _dot(u * sv, wout_ref[...])
        mix_ref[0] = mix
        xo_ref[0] = xt + mod_ref[0, 2:3] * mix

    return pl.pallas_call(
        body, name="sgu_fwd", grid=(bl, t // tm),
        in_specs=[_tok(tm, d), _per_example(8, d), _whole(w_in.shape), _whole(ln_g.shape), _whole(ln_b.shape),
                  _whole(w_s.shape), _whole(bias_full.shape), _whole(w_out.shape)],
        out_specs=(_tok(tm, d), _tok(tm, d)),
        out_shape=(_sds(x.shape), _sds(x.shape)),
        scratch_shapes=[pltpu.VMEM(w_s.shape, BF16)],
        compiler_params=_cparams(("arbitrary", "arbitrary")),
    )(x, mod, w_in, ln_g, ln_b, w_s, bias_full, w_out)


def _sgu_bwd(x, dxo, mix, mod, w_in, ln_g, ln_b, w_s, bias_full, w_out, group_ind, scatter=()):
    bl, t, d = x.shape
    tm = min(t, SGU_TM)
    gd = d // SGU_GROUPS

    def body(x_ref, dxo_ref, mix_ref, mod_ref, win_ref, lng_ref, lnb_ref, ws_ref, bias_ref, wout_ref, ind_ref,
             dx_ref, dmod_ref, h_ref, dpre_ref, z_ref, dmix_ref, small_ref, dws_ref, dbs_ref, wm_scr, dbias_scr):
        e, ti = pl.program_id(0), pl.program_id(1)
        _sgu_masked(ws_ref, wm_scr)
        xt, dxo = x_ref[0], dxo_ref[0]
        lng = lng_ref[...]
        xn, inv, h, pre, u, vhat, rstd, vln, sv = _sgu_core(xt, mod_ref, win_ref, lng, lnb_ref[...], wm_scr, bias_ref)
        h_ref[0] = h
        z_ref[0] = (u * sv).astype(BF16)
        dmix = mod_ref[0, 2:3] * dxo
        dmix_ref[0] = dmix.astype(BF16)
        dz = _dot_nt(dmix, wout_ref[...])
        du, dsv = dz * sv, dz * u

        @pl.when(_first(e, ti))
        def _():
            dws_ref[...] = jnp.zeros_like(dws_ref)
            dbias_scr[...] = jnp.zeros_like(dbias_scr)
            small_ref[...] = jnp.zeros_like(small_ref)

        row = lax.broadcasted_iota(jnp.int32, (SGU_CHUNK, SGU_CHUNK), 0)
        col = lax.broadcasted_iota(jnp.int32, (SGU_CHUNK, SGU_CHUNK), 1)
        rows = []
        for c in range(tm // SGU_CHUNK):
            rs = slice(c * SGU_CHUNK, (c + 1) * SGU_CHUNK)
            dbias_scr[...] += dsv[rs]
            cols = []
            for g in range(SGU_GROUPS):
                cs = slice(g * gd, (g + 1) * gd)
                cols.append(_dot_tn(wm_scr[g], dsv[rs, cs]))
                dws_ref[g] += jnp.where(row >= col, _dot_nt(dsv[rs, cs], vln[rs, cs]), 0.0)
            rows.append(jnp.concatenate(cols, axis=1))
        dvln = jnp.concatenate(rows, axis=0)
        small_ref[0:1] += _colsum(dvln * vhat)
        small_ref[1:2] += _colsum(dvln)
        dvhat = dvln * lng
        dv = rstd * (dvhat - jnp.mean(dvhat, axis=-1, keepdims=True)
                     - vhat * jnp.mean(dvhat * vhat, axis=-1, keepdims=True))
        dpre = (jnp.concatenate([du, dv], axis=1) * _gelu_grad(pre)).astype(BF16)
        dpre_ref[0] = dpre
        dh = _dot_nt(dpre, win_ref[...])
        dx_ref[0] = dxo + _rms_bwd(xn, inv, dh * (1.0 + mod_ref[0, 1:2]))
        _add_rows(dmod_ref, ti == 0, [_colsum(dh), _colsum(dh * xn), _colsum(dxo * mix_ref[0])])

        @pl.when(jnp.logical_and(e == bl - 1, ti == t // tm - 1))
        def _():
            hi, lo = _split_bf16(dbias_scr[...])
            ind = ind_ref[...]
            dbs_ref[...] = (lax.dot_general(ind, hi, (((1,), (1,)), ((), ())), preferred_element_type=F32)
                            + lax.dot_general(ind, lo, (((1,), (1,)), ((), ())), preferred_element_type=F32))

    return _call_with_scatter(
        body, name="sgu_bwd", grid=(bl, t // tm),
        in_specs=[_tok(tm, d), _tok(tm, d), _tok(tm, d), _per_example(8, d), _whole(w_in.shape), _whole(ln_g.shape),
                  _whole(ln_b.shape), _whole(w_s.shape), _whole(bias_full.shape), _whole(w_out.shape),
                  _whole(group_ind.shape)],
        out_specs=(_tok(tm, d), _per_example(8, d), _tok(tm, d), _tok(tm, 2 * d), _tok(tm, d), _tok(tm, d),
                   _whole((8, d)), _whole(w_s.shape), _whole((SGU_GROUPS, SGU_CHUNK))),
        out_shape=(_sds(x.shape), _sds((bl, 8, d)), _sds(x.shape, BF16), _sds((bl, t, 2 * d), BF16),
                   _sds(x.shape, BF16), _sds(x.shape, BF16), _sds((8, d)), _sds(w_s.shape),
                   _sds((SGU_GROUPS, SGU_CHUNK))),
        scratch_shapes=[pltpu.VMEM(w_s.shape, BF16), pltpu.VMEM((SGU_CHUNK, d), F32)],
        operands=(x, dxo, mix, mod, w_in, ln_g, ln_b, w_s, bias_full, w_out, group_ind), scatter=scatter)


RWKV_TM = 256
N_VEC = 16


def _rwkv_pre_core(x_ref, halo_ref, mod_ref, vec_ref, ti):
    xn, inv = _rms(x_ref[0])
    scale1, shift = 1.0 + mod_ref[0, 1:2], mod_ref[0, 0:1]
    h = xn * scale1 + shift
    hn, _ = _rms(halo_ref[0])
    hh = hn * scale1 + shift
    first = jnp.where(ti == 0, 0.0, hh[7:8])
    xx = _shift_down(h, first) - h
    xs = [h + xx * vec_ref[i:i + 1] for i in range(6)]
    return xn, inv, xx, xs


def _rwkv_proj(xs, wrkv_ref, w1_ref, a1_ref, g1_ref, w2_ref, a2_ref, g2_ref):
    d = xs[0].shape[1]
    xr, xw, xk, xv, xa, xg = [z.astype(BF16) for z in xs]
    r = jnp.dot(xr, wrkv_ref[:, 0:d], preferred_element_type=F32)
    k = jnp.dot(xk, wrkv_ref[:, d:2 * d], preferred_element_type=F32)
    v = jnp.dot(xv, wrkv_ref[:, 2 * d:3 * d], preferred_element_type=F32)
    tw2 = jnp.tanh(jnp.dot(xw, w1_ref[...], preferred_element_type=F32))
    ta = jnp.dot(xa, a1_ref[...], preferred_element_type=F32)
    sg = _sigmoid(jnp.dot(xg, g1_ref[...], preferred_element_type=F32))
    lw, la, g = _dot(tw2, w2_ref[...]), _dot(ta, a2_ref[...]), _dot(sg, g2_ref[...])
    return (xr, xw, xk, xv, xa, xg), r, k, v, tw2, ta, sg, lw, la, g


def _to_heads(ref, val, nh):
    for hd in range(nh):
        ref[0, hd] = val[:, hd * HEAD:(hd + 1) * HEAD]


def _from_heads(ref, scr, nh):
    for hd in range(nh):
        scr[:, hd * HEAD:(hd + 1) * HEAD] = ref[0, hd]
    return scr[...]


def _rwkv_weight_specs(ws):
    return [_whole(w.shape) for w in ws]


def _rwkv_pre_fwd(x, mod, vec, e_ind, et_ind, weights):
    bl, t, d = x.shape
    tm = min(t, RWKV_TM)
    nh = d // HEAD
    hb = tm // 8

    def body(x_ref, halo_ref, mod_ref, vec_ref, e_ref, et_ref, wrkv, w1, a1, g1, w2, a2, g2,
             r_ref, ld_ref, k2_ref, v_ref, as_ref, bs_ref, g_ref):
        ti = pl.program_id(1)
        _, _, _, xs = _rwkv_pre_core(x_ref, halo_ref, mod_ref, vec_ref, ti)
        _, r, k, v, _, _, _, lw, la, g = _rwkv_proj(xs, wrkv, w1, a1, g1, w2, a2, g2)
        headsum = _make_headsum(e_ref[...], et_ref[...])
        ld, k2, a_s, b_s = _rwkv_elem(r, k, lw, la, vec_ref[6:7], vec_ref[7:8], vec_ref[8:9], vec_ref[9:10], headsum)
        g_ref[0] = g
        for ref, val in ((r_ref, r), (ld_ref, ld), (k2_ref, k2), (v_ref, v), (as_ref, a_s), (bs_ref, b_s)):
            _to_heads(ref, val, nh)

    halo = pl.BlockSpec((1, 8, d), lambda e, i: (e, jnp.maximum(i * hb - 1, 0), 0))
    hs = _sds((bl, nh, t, HEAD))
    return pl.pallas_call(
        body, name="rwkv_pre_fwd", grid=(bl, t // tm),
        in_specs=[_tok(tm, d), halo, _per_example(8, d), _whole(vec.shape), _whole(e_ind.shape), _whole(et_ind.shape)]
        + _rwkv_weight_specs(weights),
        out_specs=(_heads(nh, tm),) * 6 + (_tok(tm, d),),
        out_shape=(hs,) * 6 + (_sds(x.shape),),
        compiler_params=_cparams(("arbitrary", "arbitrary")),
    )(x, x, mod, vec, e_ind, et_ind, *weights)


def _rwkv_post_fwd(x, y, r, k2, v, g, mod, vec, e_ind, et_ind, w_out):
    bl, t, d = x.shape
    tm = min(t, RWKV_TM)
    nh = d // HEAD

    def body(x_ref, y_ref, r_ref, k2_ref, v_ref, g_ref, mod_ref, vec_ref, e_ref, et_ref, wout_ref,
             xo_ref, mix_ref, s0, s1, s2, s3):
        headsum = _make_headsum(e_ref[...], et_ref[...])
        yv, rv, kv, vv = (_from_heads(ref, scr, nh) for ref, scr in
                          ((y_ref, s0), (r_ref, s1), (k2_ref, s2), (v_ref, s3)))
        o = _rwkv_post(yv, rv, kv, vv, g_ref[0], vec_ref[10:11], vec_ref[11:12], vec_ref[12:13], headsum)
        mix = _dot(o, wout_ref[...])
        mix_ref[0] = mix
        xo_ref[0] = x_ref[0] + mod_ref[0, 2:3] * mix

    return pl.pallas_call(
        body, name="rwkv_post_fwd", grid=(bl, t // tm),
        in_specs=[_tok(tm, d)] + [_heads(nh, tm)] * 4 + [_tok(tm, d), _per_example(8, d), _whole(vec.shape),
                                                         _whole(e_ind.shape), _whole(et_ind.shape), _whole(w_out.shape)],
        out_specs=(_tok(tm, d), _tok(tm, d)),
        out_shape=(_sds(x.shape), _sds(x.shape)),
        scratch_shapes=[pltpu.VMEM((tm, d), F32)] * 4,
        compiler_params=_cparams(("arbitrary", "arbitrary")),
    )(x, y, r, k2, v, g, mod, vec, e_ind, et_ind, w_out)


def _rwkv_post_bwd(dxo, mix, y, r, k2, v, g, mod, vec, e_ind, et_ind, w_out, scatter=()):
    bl, t, d = dxo.shape
    tm = min(t, RWKV_TM)
    nh = d // HEAD

    def body(dxo_ref, mix_ref, y_ref, r_ref, k2_ref, v_ref, g_ref, mod_ref, vec_ref, e_ref, et_ref, wout_ref,
             dy_ref, dr_ref, dk2_ref, dv_ref, dg_ref, o_ref, dmix_ref, dgate_ref, small_ref, s0, s1, s2, s3):
        e, ti = pl.program_id(0), pl.program_id(1)
        headsum = _make_headsum(e_ref[...], et_ref[...])
        yv, rv, kv, vv = (_from_heads(ref, scr, nh) for ref, scr in
                          ((y_ref, s0), (r_ref, s1), (k2_ref, s2), (v_ref, s3)))
        dxo = dxo_ref[0]
        dmix = mod_ref[0, 2:3] * dxo
        dmix_ref[0] = dmix.astype(BF16)
        do = _dot_nt(dmix, wout_ref[...])
        post = functools.partial(_rwkv_post, headsum=headsum)
        o, vjp = jax.vjp(post, yv, rv, kv, vv, g_ref[0], vec_ref[10:11], vec_ref[11:12], vec_ref[12:13])
        o_ref[0] = o.astype(BF16)
        dy, dr, dk2, dv, dg, dlng, dlnb, drk = vjp(do)
        _to_heads(dy_ref, dy, nh)
        dr_ref[0], dk2_ref[0], dv_ref[0], dg_ref[0] = dr, dk2, dv, dg
        zero = jnp.zeros((1, d), F32)
        _add_rows(dgate_ref, ti == 0, [zero, zero, _colsum(dxo * mix_ref[0])])

        @pl.when(_first(e, ti))
        def _():
            small_ref[...] = jnp.zeros_like(small_ref)

        small_ref[0:1] += dlng
        small_ref[1:2] += dlnb
        small_ref[2:3] += drk

    return _call_with_scatter(
        body, name="rwkv_post_bwd", grid=(bl, t // tm),
        in_specs=[_tok(tm, d), _tok(tm, d)] + [_heads(nh, tm)] * 4
        + [_tok(tm, d), _per_example(8, d), _whole(vec.shape), _whole(e_ind.shape), _whole(et_ind.shape),
           _whole(w_out.shape)],
        out_specs=(_heads(nh, tm),) + (_tok(tm, d),) * 6 + (_per_example(8, d), _whole((8, d))),
        out_shape=(_sds((bl, nh, t, HEAD)),) + (_sds(dxo.shape),) * 4 + (_sds(dxo.shape, BF16),) * 2
        + (_sds((bl, 8, d)), _sds((8, d))),
        scratch_shapes=[pltpu.VMEM((tm, d), F32)] * 4,
        operands=(dxo, mix, y, r, k2, v, g, mod, vec, e_ind, et_ind, w_out), scatter=scatter)


RWKV_BWD_TM = 128


def _rwkv_pre_bwd(x, mod, vec, e_ind, et_ind, weights, dr_p, dk2_p, dv_p, dg, dr_s, dld, dk2_s, dv_s, das, dbs):
    bl, t, d = x.shape
    tm = min(t, RWKV_BWD_TM)
    nh = d // HEAD
    hb = tm // 8
    lp, gp = LORA_PAD, GATE_PAD

    def body(x_ref, halo_ref, mod_ref, vec_ref, e_ref, et_ref, wrkv, w1, a1, g1, w2, a2, g2,
             drp_ref, dk2p_ref, dvp_ref, dg_ref, drs_ref, dld_ref, dk2s_ref, dvs_ref, das_ref, dbs_ref,
             dh_ref, dhp_ref, xr_ref, xw_ref, xk_ref, xv_ref, xa_ref, xg_ref, dr_ref, dk_ref, dv_ref,
             dtw_ref, dta_ref, dtg_ref, tw2_ref, ta_ref, sg_ref, dlw_ref, dla_ref, dgb_ref, small_ref,
             s0, s1, s2, s3, s4, s5):
        e, ti = pl.program_id(0), pl.program_id(1)
        _, _, xx, xs = _rwkv_pre_core(x_ref, halo_ref, mod_ref, vec_ref, ti)
        xb, r, k, v, tw2, ta, sg, lw, la, _ = _rwkv_proj(xs, wrkv, w1, a1, g1, w2, a2, g2)
        for ref, val in zip((xr_ref, xw_ref, xk_ref, xv_ref, xa_ref, xg_ref), xb):
            ref[0] = val
        headsum = _make_headsum(e_ref[...], et_ref[...])
        drs, dld, dk2s, dvs, das, dbs_ = (_from_heads(ref, scr, nh) for ref, scr in
                                          ((drs_ref, s0), (dld_ref, s1), (dk2s_ref, s2), (dvs_ref, s3),
                                           (das_ref, s4), (dbs_ref, s5)))
        elem = functools.partial(_rwkv_elem, r, headsum=headsum)
        _, vjp = jax.vjp(elem, k, lw, la, vec_ref[6:7], vec_ref[7:8], vec_ref[8:9], vec_ref[9:10])
        dk, dlw, dla, dw0, da0, dkk, dka = vjp((dld, dk2p_ref[0] + dk2s, das, dbs_))
        dr = drp_ref[0] + drs
        dv = dvp_ref[0] + dvs
        dgv = dg_ref[0]
        dtg = _dot_nt(dgv, g2[...]) * sg * (1.0 - sg)
        dtw = _dot_nt(dlw, w2[...]) * (1.0 - tw2 * tw2)
        dta = _dot_nt(dla, a2[...])
        dr_ref[0], dk_ref[0], dv_ref[0] = dr.astype(BF16), dk.astype(BF16), dv.astype(BF16)
        dtw_ref[0], dta_ref[0], dtg_ref[0] = dtw.astype(BF16), dta.astype(BF16), dtg.astype(BF16)
        tw2_ref[0], ta_ref[0], sg_ref[0] = tw2.astype(BF16), ta.astype(BF16), sg.astype(BF16)
        dlw_ref[0], dla_ref[0], dgb_ref[0] = dlw.astype(BF16), dla.astype(BF16), dgv.astype(BF16)
        dxs = (_dot_nt(dr, wrkv[:, 0:d]), _dot_nt(dtw, w1[...]), _dot_nt(dk, wrkv[:, d:2 * d]),
               _dot_nt(dv, wrkv[:, 2 * d:3 * d]), _dot_nt(dta, a1[...]), _dot_nt(dtg, g1[...]))

        @pl.when(_first(e, ti))
        def _():
            small_ref[...] = jnp.zeros_like(small_ref)

        dh = jnp.zeros((tm, d), F32)
        dhp = jnp.zeros((tm, d), F32)
        for i, dxi in enumerate(dxs):
            mu = vec_ref[i:i + 1]
            dh += dxi * (1.0 - mu)
            dhp += dxi * mu
            small_ref[i:i + 1] += _colsum(dxi * xx)
        dh_ref[0], dhp_ref[0] = dh, dhp
        small_ref[6:7] += dw0
        small_ref[7:8] += da0
        small_ref[8:9] += dkk
        small_ref[9:10] += dka

    halo = pl.BlockSpec((1, 8, d), lambda e, i: (e, jnp.maximum(i * hb - 1, 0), 0))
    tokd, tokl, tokg = _tok(tm, d), _tok(tm, lp), _tok(tm, gp)
    bf = lambda w: _sds((bl, t, w), BF16)
    return pl.pallas_call(
        body, name="rwkv_pre_bwd", grid=(bl, t // tm),
        in_specs=[tokd, halo, _per_example(8, d), _whole(vec.shape), _whole(e_ind.shape), _whole(et_ind.shape)]
        + _rwkv_weight_specs(weights) + [tokd] * 4 + [_heads(nh, tm)] * 6,
        out_specs=(tokd, tokd) + (tokd,) * 6 + (tokd,) * 3 + (tokl, tokl, tokg, tokl, tokl, tokg)
        + (tokd, tokd, tokd, _whole((N_VEC, d))),
        out_shape=(_sds(x.shape), _sds(x.shape)) + (bf(d),) * 9 + (bf(lp), bf(lp), bf(gp), bf(lp), bf(lp), bf(gp))
        + (bf(d), bf(d), bf(d), _sds((N_VEC, d))),
        scratch_shapes=[pltpu.VMEM((tm, d), F32)] * 6,
        compiler_params=_cparams(("arbitrary", "arbitrary")),
    )(x, x, mod, vec, e_ind, et_ind, *weights, dr_p, dk2_p, dv_p, dg, dr_s, dld, dk2_s, dv_s, das, dbs)


def _norm_bwd(x, dxo, dh, dhprev, mod, dgate):
    bl, t, d = x.shape
    tm = min(t, RWKV_TM)
    hb = tm // 8
    last_blk = t // 8 - 1

    def body(x_ref, dxo_ref, dh_ref, dhp_ref, nxt_ref, mod_ref, dgate_ref, dx_ref, dmod_ref):
        ti = pl.program_id(1)
        xn, inv = _rms(x_ref[0])
        last = jnp.where(ti == t // tm - 1, 0.0, nxt_ref[0, 0:1])
        dh = dh_ref[0] + _shift_up(dhp_ref[0], last)
        dx_ref[0] = dxo_ref[0] + _rms_bwd(xn, inv, dh * (1.0 + mod_ref[0, 1:2]))

        @pl.when(ti == 0)
        def _():
            dmod_ref[0] = dgate_ref[0]

        dmod_ref[0, 0:1] += _colsum(dh)
        dmod_ref[0, 1:2] += _colsum(dh * xn)

    nxt = pl.BlockSpec((1, 8, d), lambda e, i: (e, jnp.minimum((i + 1) * hb, last_blk), 0))
    return pl.pallas_call(
        body, name="norm_bwd", grid=(bl, t // tm),
        in_specs=[_tok(tm, d)] * 4 + [nxt, _per_example(8, d), _per_example(8, d)],
        out_specs=(_tok(tm, d), _per_example(8, d)),
        out_shape=(_sds(x.shape), _sds((bl, 8, d))),
        compiler_params=_cparams(("arbitrary", "arbitrary")),
    )(x, dxo, dh, dhprev, dhprev, mod, dgate)


def _final(x, target, final_g):
    bl, t, d = x.shape
    tm = min(t, 512)

    def body(x_ref, tgt_ref, g_ref, dx_ref, loss_ref, dg_ref):
        e, ti = pl.program_id(0), pl.program_id(1)

        @pl.when(_first(e, ti))
        def _():
            loss_ref[...] = jnp.zeros_like(loss_ref)
            dg_ref[...] = jnp.zeros_like(dg_ref)

        xn, inv = _rms(x_ref[0])
        err = xn * g_ref[...] - tgt_ref[0]
        loss_ref[...] += (0.5 / d) * jnp.sum(err * err)
        dy = err * (1.0 / d)
        dg_ref[0:1] += _colsum(dy * xn)
        dx_ref[0] = _rms_bwd(xn, inv, dy * g_ref[...])

    return pl.pallas_call(
        body, name="final_loss", grid=(bl, t // tm),
        in_specs=[_tok(tm, d), _tok(tm, d), _whole(final_g.shape)],
        out_specs=(_tok(tm, d), _whole((8, 128)), _whole((8, d))),
        out_shape=(_sds(x.shape), _sds((8, 128)), _sds((8, d))),
        compiler_params=_cparams(("arbitrary", "arbitrary")),
    )(x, target, final_g)


def _adamw_math(w, g, m, v):
    m = ADAM_B1 * m + (1.0 - ADAM_B1) * g
    v = ADAM_B2 * v + (1.0 - ADAM_B2) * jnp.square(g)
    m_hat = m / (1.0 - ADAM_B1 ** ADAM_STEP)
    v_hat = v / (1.0 - ADAM_B2 ** ADAM_STEP)
    return -ADAM_LR * (m_hat / (jnp.sqrt(v_hat) + ADAM_EPS) + ADAM_WD * w), m, v


def _sum_parts(ref, n):
    g = ref[0].astype(F32)
    for s in range(1, n):
        g = g + ref[s].astype(F32)
    return g


def _adamw_layers(w, m, v, parts, name):
    nl, rows, c = w.shape
    tr = min(rows, 128)

    def body(w_ref, m_ref, v_ref, *refs):
        p_refs, (g_ref, d_ref, mo_ref, vo_ref) = refs[:nl], refs[nl:]
        for layer in range(nl):
            @pl.when(pl.program_id(0) == layer)
            def _(p_ref=p_refs[layer]):
                g = _sum_parts(p_ref, p_ref.shape[0])
                g_ref[...] = g
                d_ref[...], mo_ref[...], vo_ref[...] = _adamw_math(w_ref[...], g, m_ref[...], v_ref[...])

    row = pl.BlockSpec((None, tr, c), lambda l, i: (l, i, 0))
    return pl.pallas_call(
        body, name=name, grid=(nl, rows // tr),
        in_specs=[row, row, row] + [pl.BlockSpec((p.shape[0], tr, c), lambda l, i: (0, i, 0)) for p in parts],
        out_specs=(row,) * 4, out_shape=(_sds(w.shape),) * 4,
        compiler_params=_cparams(("arbitrary", "arbitrary")),
    )(w, m, v, *parts)


def _adamw_small(items, name):
    k = len(items)
    ns = [it[3].shape[0] for it in items]

    def body(*refs):
        ins, outs = refs[:4 * k], refs[4 * k:]
        for i in range(k):
            w_ref, m_ref, v_ref, p_ref = ins[4 * i:4 * i + 4]
            g = _sum_parts(p_ref, ns[i])
            outs[4 * i][...] = g
            outs[4 * i + 1][...], outs[4 * i + 2][...], outs[4 * i + 3][...] = _adamw_math(
                w_ref[...], g, m_ref[...], v_ref[...])

    flat = [a for it in items for a in it]
    res = pl.pallas_call(
        body, name=name,
        out_shape=tuple(_sds(it[0].shape) for it in items for _ in range(4)),
        compiler_params=_cparams(),
    )(*flat)
    return [tuple(res[4 * i:4 * i + 4]) for i in range(k)]


WEIGHTS = ['ada_w', 'ada_b', 'mlp_w1', 'mlp_w2', 'a_w_in', 'a_ln_g', 'a_ln_b', 'a_w_s', 'a_b_s', 'a_w_out', 'b_mu',
           'b_w_in', 'b_w0', 'b_w1', 'b_w2', 'b_a0', 'b_a1', 'b_a2', 'b_g1', 'b_g2', 'b_k_k', 'b_k_a', 'b_r_k',
           'b_ln_g', 'b_ln_b', 'b_w_out', 'final_g']
GATHERED = [('mlp_w1', 2), ('mlp_w2', 1), ('a_w_in', 2), ('a_w_out', 1), ('b_w_in', 2), ('b_w_out', 1),
            ('b_w1', 1), ('b_a1', 1), ('b_g1', 1), ('b_w2', 2), ('b_a2', 2), ('b_g2', 2)]
VECTORS = ['b_mu', 'b_w0', 'b_a0', 'b_k_k', 'b_k_a', 'b_ln_g', 'b_ln_b']
REPLICATED = ['a_ln_g', 'a_ln_b', 'a_w_s', 'a_b_s', 'b_r_k', 'final_g']
ROW_ALIGN = 16


def _pad_rows(a, mult):
    pad = (-a.shape[-2]) % mult
    return jnp.pad(a, [(0, 0)] * (a.ndim - 2) + [(0, pad), (0, 0)]) if pad else a


def _as2d(a):
    if a.ndim == 1:
        return a.reshape(1, -1)
    lead = 1
    for s in a.shape[:-1]:
        lead *= s
    return a.reshape(lead, a.shape[-1])


def kernel(x, c, ada_w, ada_b, mlp_w1, mlp_w2, a_w_in, a_ln_g, a_ln_b, a_w_s, a_b_s, a_w_out, b_mu, b_w_in, b_w0, b_w1, b_w2, b_a0, b_a1, b_a2, b_g1, b_g2, b_k_k, b_k_a, b_r_k, b_ln_g, b_ln_b, b_w_out, final_g, loss_target, m_ada_w, m_ada_b, m_mlp_w1, m_mlp_w2, m_a_w_in, m_a_ln_g, m_a_ln_b, m_a_w_s, m_a_b_s, m_a_w_out, m_b_mu, m_b_w_in, m_b_w0, m_b_w1, m_b_w2, m_b_a0, m_b_a1, m_b_a2, m_b_g1, m_b_g2, m_b_k_k, m_b_k_a, m_b_r_k, m_b_ln_g, m_b_ln_b, m_b_w_out, m_final_g, v_ada_w, v_ada_b, v_mlp_w1, v_mlp_w2, v_a_w_in, v_a_ln_g, v_a_ln_b, v_a_w_s, v_a_b_s, v_a_w_out, v_b_mu, v_b_w_in, v_b_w0, v_b_w1, v_b_w2, v_b_a0, v_b_a1, v_b_a2, v_b_g1, v_b_g2, v_b_k_k, v_b_k_a, v_b_r_k, v_b_ln_g, v_b_ln_b, v_b_w_out, v_final_g):
    given = dict(locals())
    w = {n: given[n] for n in WEIGHTS}
    bl, t, d = x.shape
    nl = ada_w.shape[0]
    nb = N_DEV * bl
    m_tok = bl * t
    me = 4 * lax.axis_index("x") + 2 * lax.axis_index("y") + lax.axis_index("c")

    c_all = _exchange(c, "gather_c", False).reshape(nb, d)
    cols = ada_w.shape[2]
    ada_b_cols = lax.dynamic_slice(ada_b, (0, me * cols), (nl, cols)).reshape(nl, 1, cols)
    mod_cols = _ada_fwd(c_all, ada_w, ada_b_cols)
    mod_full = jnp.moveaxis(_exchange(mod_cols, "gather_mod", False), 0, 2).reshape(nl, nb, 6 * d)
    mod_mine = lax.dynamic_slice(mod_full, (0, me * bl, 0), (nl, bl, 6 * d)).reshape(nl, bl, 6, d)
    mod_mix = jnp.pad(mod_mine[:, :, 0:3], ((0, 0), (0, 0), (0, 5), (0, 0)))
    mod_mlp = jnp.pad(mod_mine[:, :, 3:6], ((0, 0), (0, 0), (0, 5), (0, 0)))

    rows = [w[n].size // d for n, _ in GATHERED]
    offs = [sum(rows[:i]) for i in range(len(rows))]
    n_rows = sum(rows)
    pack = _pad_rows(jnp.concatenate([w[n].reshape(-1, d) for n, _ in GATHERED], axis=0).astype(BF16), ROW_ALIGN)
    gathered = _exchange(pack, "gather_weights", False, relay=True)
    full = {}
    for (n, ax), off, nr in zip(GATHERED, offs, rows):
        loc = w[n].shape
        g = jnp.moveaxis(gathered[:, off:off + nr].reshape((N_DEV,) + loc), 0, ax)
        full[n] = g.reshape(loc[:ax] + (N_DEV * loc[ax],) + loc[ax + 1:])
    vec_loc = _pad_rows(jnp.concatenate([_as2d(w[n]) for n in VECTORS], axis=0), ROW_ALIGN)
    n_vec_rows = sum(_as2d(w[n]).shape[0] for n in VECTORS)
    vec = jnp.moveaxis(_exchange(vec_loc, "gather_vectors", False), 0, 1).reshape(N_VEC, d)
    vec = vec.at[n_vec_rows].set(b_r_k.reshape(d))

    e_ind, et_ind = _head_indicators(d)
    gd = d // SGU_GROUPS
    group_ind = (jnp.arange(SGU_GROUPS)[:, None] == jnp.arange(d)[None, :] // gd).astype(BF16)
    bias_full = jnp.repeat(a_b_s[0].T, gd, axis=1)
    pad_c = lambda a, n: jnp.pad(a, ((0, 0), (0, n - a.shape[1])))
    pad_r = lambda a, n: jnp.pad(a, ((0, n - a.shape[0]), (0, 0)))
    rwkv_w = (full['b_w_in'][0], pad_c(full['b_w1'][0], LORA_PAD), pad_c(full['b_a1'][0], LORA_PAD),
              pad_c(full['b_g1'][0], GATE_PAD), pad_r(full['b_w2'][0], LORA_PAD), pad_r(full['b_a2'][0], LORA_PAD),
              pad_r(full['b_g2'][0], GATE_PAD))
    sgu_args = (full['a_w_in'][0], a_ln_g, a_ln_b, a_w_s[0], bias_full, full['a_w_out'][0])

    x0 = x
    x1, mix_a = _sgu_fwd(x0, mod_mix[0], *sgu_args)
    x2, ff0 = _mlp_fwd(x1, mod_mlp[0], full['mlp_w1'][0], full['mlp_w2'][0])
    r, ld, k2, v, a_s, b_s, gate = _rwkv_pre_fwd(x2, mod_mix[1], vec, e_ind, et_ind, rwkv_w)
    y, s0 = _wkv_fwd(r, ld, k2, v, a_s, b_s)
    x3, mix_b = _rwkv_post_fwd(x2, y, r, k2, v, gate, mod_mix[1], vec, e_ind, et_ind, full['b_w_out'][0])
    x4, ff1 = _mlp_fwd(x3, mod_mlp[1], full['mlp_w1'][1], full['mlp_w2'][1])
    dx4, loss_blk, dfinal = _final(x4, loss_target, final_g.reshape(1, d))
    loss = lax.psum(loss_blk[0, 0], ("x", "y", "c"))

    tok = lambda a: a.reshape(m_tok, a.shape[-1])
    shard_rows = lambda g: g.reshape((N_DEV, g.shape[0] // N_DEV) + g.shape[1:])
    (dx3, dmod_mlp1, h_b, dff_b, q_b, dp_b), _ = _mlp_bwd(x3, dx4, ff1, mod_mlp[1], full['mlp_w1'][1],
                                                          full['mlp_w2'][1])
    gw1_1 = _matmul_tn(tok(h_b), tok(dp_b), "grad_mlp_w1_l1", col_shards=N_DEV)
    gw2_1 = shard_rows(_matmul_tn(tok(q_b), tok(dff_b), "grad_mlp_w2_l1"))
    (dy, dr_p, dk2_p, dv_p, dgate_act, o_b, dmix_b, dgate_b, small_post), (rw1_1, rw2_1) = _rwkv_post_bwd(
        dx3, mix_b, y, r, k2, v, gate, mod_mix[1], vec, e_ind, et_ind, full['b_w_out'][0], scatter=(gw1_1, gw2_1))
    g_b_w_out = shard_rows(_matmul_tn(tok(o_b), tok(dmix_b), "grad_b_w_out"))
    dr_s, dld, dk2_s, dv_s, das, dbs = _wkv_bwd(r, ld, k2, v, a_s, b_s, s0, dy)
    (dh, dhp, xr_b, xw_b, xk_b, xv_b, xa_b, xg_b, dr_b, dk_b, dv_b, dtw_b, dta_b, dtg_b, tw2_b, ta_b, sg_b,
     dlw_b, dla_b, dg_b, small_pre) = _rwkv_pre_bwd(x2, mod_mix[1], vec, e_ind, et_ind, rwkv_w,
                                                    dr_p, dk2_p, dv_p, dgate_act, dr_s, dld, dk2_s, dv_s, das, dbs)
    g_b_w_in = jnp.concatenate([_matmul_tn(tok(xr_b), tok(dr_b), "grad_b_w_r"),
                                _matmul_tn(tok(xk_b), tok(dk_b), "grad_b_w_k"),
                                _matmul_tn(tok(xv_b), tok(dv_b), "grad_b_w_v")], axis=1)
    shard_cols = lambda g: jnp.moveaxis(g.reshape(g.shape[0], N_DEV, g.shape[1] // N_DEV), 1, 0)
    g_b_w_in = shard_cols(g_b_w_in)
    lw_, lg_ = b_w1.shape[2], b_g1.shape[2]
    small_names = ['b_w1', 'b_a1', 'b_g1', 'b_w2', 'b_a2', 'b_g2'] + VECTORS
    small_parts = [
        shard_rows(_matmul_tn(tok(xw_b), tok(dtw_b), "grad_b_w1")[:, :lw_]),
        shard_rows(_matmul_tn(tok(xa_b), tok(dta_b), "grad_b_a1")[:, :lw_]),
        shard_rows(_matmul_tn(tok(xg_b), tok(dtg_b), "grad_b_g1")[:, :lg_]),
        shard_cols(_matmul_tn(tok(tw2_b), tok(dlw_b), "grad_b_w2")[:lw_]),
        shard_cols(_matmul_tn(tok(ta_b), tok(dla_b), "grad_b_a2")[:lw_]),
        shard_cols(_matmul_tn(tok(sg_b), tok(dg_b), "grad_b_g2")[:lg_]),
        shard_cols(jnp.concatenate([small_pre[0:10], small_post[0:2]], axis=0).astype(BF16)),
    ]
    small_flat = jnp.concatenate([p.reshape(N_DEV, -1) for p in small_parts], axis=1)
    lane = 128
    small_rows = -(-small_flat.shape[1] // (lane * ROW_ALIGN)) * ROW_ALIGN
    small_pack = jnp.pad(small_flat, ((0, 0), (0, small_rows * lane - small_flat.shape[1]))).reshape(
        N_DEV, small_rows, lane)
    dx2, dmod_mix1 = _norm_bwd(x2, dx3, dh, dhp, mod_mix[1], dgate_b)
    (dx1, dmod_mlp0, h_b, dff_b, q_b, dp_b), (r_b_w_in, r_b_w_out, r_small) = _mlp_bwd(
        x1, dx2, ff0, mod_mlp[0], full['mlp_w1'][0], full['mlp_w2'][0], scatter=(g_b_w_in, g_b_w_out, small_pack))
    gw1_0 = _matmul_tn(tok(h_b), tok(dp_b), "grad_mlp_w1_l0", col_shards=N_DEV)
    gw2_0 = shard_rows(_matmul_tn(tok(q_b), tok(dff_b), "grad_mlp_w2_l0"))
    (dx0, dmod_mix0, h_b, dpre_b, z_b, dmix_b, small_sgu, d_ws, d_bs), (rw1_0, rw2_0) = _sgu_bwd(
        x0, dx1, mix_a, mod_mix[0], *sgu_args, group_ind, scatter=(gw1_0, gw2_0))
    g_a_w_in = _matmul_tn(tok(h_b), tok(dpre_b), "grad_a_w_in", col_shards=N_DEV)
    g_a_w_out = shard_rows(_matmul_tn(tok(z_b), tok(dmix_b), "grad_a_w_out"))
    r_a_w_in, r_a_w_out = _scatter_call((g_a_w_in, g_a_w_out), "scatter_sgu_grads")

    dmod_mine = jnp.stack([jnp.concatenate([dmod_mix0[:, 0:3], dmod_mlp0[:, 0:3]], axis=1),
                           jnp.concatenate([dmod_mix1[:, 0:3], dmod_mlp1[:, 0:3]], axis=1)], axis=1)
    dmod_all = _exchange(dmod_mine.reshape(bl, nl * 6 * d), "gather_dmod", False)
    dmod_all = jnp.moveaxis(dmod_all.reshape(nb, nl, 6 * d), 0, 1)
    dmod_cols = lax.dynamic_slice(dmod_all, (0, 0, me * cols), (nl, nb, cols))
    g_ada_w, g_ada_b = _ada_bwd(c_all, dmod_cols, dmod_all)

    rep_g = {'a_ln_g': small_sgu[0:1], 'a_ln_b': small_sgu[1:2], 'a_w_s': d_ws.reshape(-1, d), 'a_b_s': d_bs.reshape(1, d),
             'b_r_k': small_post[2:3], 'final_g': dfinal[0:1]}
    rep_rows = [rep_g[n].shape[0] for n in REPLICATED]
    rep_pack = _pad_rows(jnp.concatenate([rep_g[n] for n in REPLICATED], axis=0), 8)
    rep_all = _exchange(rep_pack, "gather_replicated_grads", False, relay=True)

    mom = {n: given['m_' + n] for n in WEIGHTS}
    var = {n: given['v_' + n] for n in WEIGHTS}
    out = {}
    as3d = lambda a: a.reshape((-1,) + a.shape[-2:])
    for n, parts in (('mlp_w1', [rw1_0, rw1_1]), ('mlp_w2', [rw2_0, rw2_1]), ('a_w_in', [r_a_w_in]),
                     ('a_w_out', [r_a_w_out]), ('b_w_in', [r_b_w_in]), ('b_w_out', [r_b_w_out]),
                     ('ada_w', list(g_ada_w))):
        res = _adamw_layers(as3d(w[n]), as3d(mom[n]), as3d(var[n]), parts, "adamw_" + n)
        out[n] = tuple(a.reshape(w[n].shape) for a in res)

    items, names = [], []

    def add(n, part):
        s2 = _as2d(w[n]).shape
        items.append((_as2d(w[n]), _as2d(mom[n]), _as2d(var[n]), part.reshape((part.shape[0],) + s2)))
        names.append(n)

    sflat = r_small.reshape(N_DEV, -1)
    so = 0
    for n in small_names:
        sz = w[n].size
        add(n, sflat[:, so:so + sz])
        so += sz
    ro = 0
    for n, nr in zip(REPLICATED, rep_rows):
        add(n, rep_all[:, ro:ro + nr])
        ro += nr
    add('ada_b', g_ada_b[None])
    for n, res in zip(names, _adamw_small(items, "adamw_small")):
        out[n] = tuple(a.reshape(w[n].shape) for a in res)

    return (loss, dx0, *[out[n][0] for n in WEIGHTS], *[out[n][1] for n in WEIGHTS],
            *[out[n][2] for n in WEIGHTS], *[out[n][3] for n in WEIGHTS])
```

```python
import functools

import jax
import jax.numpy as jnp
from jax import lax
from jax.experimental import pallas as pl
from jax.experimental.pallas import tpu as pltpu

F32 = jnp.float32
BF16 = jnp.bfloat16

N_DEV = 8
RMS_EPS = 1e-6
LN_EPS = 1e-5
HEAD = 64
GN_EPS = HEAD * 1e-5
L2_EPS = 1e-12
SGU_CHUNK = 128
SGU_GROUPS = 8
WKV_CHUNK = 64
WKV_HEADS_PER_STEP = 16
WKV_EXAMPLES_PER_STEP = 2
LORA_PAD = 128
GATE_PAD = 256
ADAM_LR, ADAM_B1, ADAM_B2, ADAM_EPS, ADAM_WD, ADAM_STEP = 0.001, 0.9, 0.999, 1e-08, 0.01, 10
VMEM_LIMIT = 56 * 1024 * 1024


def _cparams(sem=None, **kw):
    if sem is not None:
        kw["dimension_semantics"] = sem
    return pltpu.CompilerParams(vmem_limit_bytes=VMEM_LIMIT, **kw)


def _dot(a, b):
    return jnp.dot(a.astype(BF16), b.astype(BF16), preferred_element_type=F32)


def _dot_nt(a, b):
    return lax.dot_general(a.astype(BF16), b.astype(BF16), (((1,), (1,)), ((), ())), preferred_element_type=F32)


def _dot_tn(a, b):
    return lax.dot_general(a.astype(BF16), b.astype(BF16), (((0,), (0,)), ((), ())), preferred_element_type=F32)


def _bdot(a, b, dims):
    return lax.dot_general(a.astype(BF16), b.astype(BF16), (dims, ((0,), (0,))), preferred_element_type=F32)


@jax.custom_vjp
def _tri_sum(tri, tri_t, x):
    hi = x.astype(BF16)
    lo = (x - hi.astype(F32)).astype(BF16)
    dn = (((2,), (1,)), ((0,), (0,)))
    return (lax.dot_general(tri, hi, dn, preferred_element_type=F32)
            + lax.dot_general(tri, lo, dn, preferred_element_type=F32))


_tri_sum.defvjp(lambda tri, tri_t, x: (_tri_sum(tri, tri_t, x), (tri, tri_t)),
                lambda res, g: (jnp.zeros_like(res[0]), jnp.zeros_like(res[1]), _tri_sum(res[1], res[0], g)))


@jax.custom_vjp
def _bmm_nn(a, b):
    return _bdot(a, b, ((2,), (1,)))


@jax.custom_vjp
def _bmm_nt(a, b):
    return _bdot(a, b, ((2,), (2,)))


@jax.custom_vjp
def _bmm_tn(a, b):
    return _bdot(a, b, ((1,), (1,)))


_bmm_nn.defvjp(lambda a, b: (_bmm_nn(a, b), (a, b)), lambda res, g: (_bmm_nt(g, res[1]), _bmm_tn(res[0], g)))
_bmm_nt.defvjp(lambda a, b: (_bmm_nt(a, b), (a, b)), lambda res, g: (_bmm_nn(g, res[1]), _bmm_tn(g, res[0])))
_bmm_tn.defvjp(lambda a, b: (_bmm_tn(a, b), (a, b)), lambda res, g: (_bmm_nt(res[1], g), _bmm_nn(res[0], g)))


def _tri_solve_fwd(p, rhs):
    n = p.shape[1]
    row = lax.broadcasted_iota(jnp.int32, (n, n), 0)
    col = lax.broadcasted_iota(jnp.int32, (n, n), 1)
    tinv = jnp.where(row == col, 1.0, 0.0).astype(F32)[None] + p
    for _ in range(max(1, (n - 1).bit_length()) - 1):
        p = _bmm_nn(p, p)
        tinv = tinv + _bmm_nn(tinv, p)
    u = _bmm_nn(tinv, rhs)
    return u, (tinv, u)


def _tri_solve_bwd(res, du):
    tinv, u = res
    drhs = _bmm_tn(tinv, du)
    return _bmm_nt(drhs, u), drhs


@jax.custom_vjp
def _tri_solve(p, rhs):
    return _tri_solve_fwd(p, rhs)[0]


_tri_solve.defvjp(_tri_solve_fwd, _tri_solve_bwd)


def _wkv_chunk(s0, r, ld, k, v, a, b):
    nh, n, _ = r.shape
    row = lax.broadcasted_iota(jnp.int32, (n, n), 0)
    col = lax.broadcasted_iota(jnp.int32, (n, n), 1)
    incl = row >= col
    strict = row > col
    lower = jnp.broadcast_to(jnp.where(incl, 1.0, 0.0).astype(BF16), (nh, n, n))
    upper = jnp.broadcast_to(jnp.where(row <= col, 1.0, 0.0).astype(BF16), (nh, n, n))
    c = _tri_sum(lower, upper, ld)
    c_end = c[:, n - 1:n, :]
    ec, enc, ecx, eend = jnp.exp(c), jnp.exp(-c), jnp.exp(c - ld), jnp.exp(c_end - c)
    ar = jnp.concatenate([a * ecx, r * ec], axis=1)
    mask = jnp.concatenate([strict, incl], axis=0)[None]
    m_b = jnp.where(mask, _bmm_nt(ar, b * enc), 0.0)
    m_k = jnp.where(mask, _bmm_nt(ar, k * enc), 0.0)
    a_ab, a_rb = m_b[:, :n], m_b[:, n:]
    base = _bmm_nt(ar, s0) + _bmm_nn(m_k, v)
    u = _tri_solve(a_ab, base[:, :n])
    y = base[:, n:] + _bmm_nn(a_rb, u)
    s1 = s0 * jnp.exp(c_end) + _bmm_tn(jnp.concatenate([u, v], axis=1), jnp.concatenate([b * eend, k * eend], axis=1))
    return y, s1


def _wkv_specs(bl, nh, t):
    eb, hb, lc = min(bl, WKV_EXAMPLES_PER_STEP), min(nh, WKV_HEADS_PER_STEP), WKV_CHUNK
    return eb, hb, lc, (bl // eb, nh // hb, t // lc)


def _wkv_fwd(r, ld, k, v, a, b):
    bl, nh, t, n = r.shape
    eb, hb, lc, grid = _wkv_specs(bl, nh, t)
    nc = t // lc
    nb = eb * hb

    def body(r_ref, ld_ref, k_ref, v_ref, a_ref, b_ref, y_ref, s0_ref, s_scr):
        @pl.when(pl.program_id(2) == 0)
        def _():
            s_scr[...] = jnp.zeros_like(s_scr)

        s0 = s_scr[...]
        s0_ref[:, :, 0] = s0.reshape(eb, hb, n, n)
        y, s1 = _wkv_chunk(s0, *(ref[...].reshape(nb, lc, n) for ref in (r_ref, ld_ref, k_ref, v_ref, a_ref, b_ref)))
        y_ref[...] = y.reshape(eb, hb, lc, n)
        s_scr[...] = s1

    seq = pl.BlockSpec((eb, hb, lc, n), lambda e, h, c: (e, h, c, 0))
    return pl.pallas_call(
        body, name="wkv_fwd", grid=grid,
        in_specs=[seq] * 6,
        out_specs=(seq, pl.BlockSpec((eb, hb, 1, n, n), lambda e, h, c: (e, h, c, 0, 0))),
        out_shape=(jax.ShapeDtypeStruct((bl, nh, t, n), F32), jax.ShapeDtypeStruct((bl, nh, nc, n, n), F32)),
        scratch_shapes=[pltpu.VMEM((nb, n, n), F32)],
        compiler_params=_cparams(("arbitrary", "arbitrary", "arbitrary")),
    )(r, ld, k, v, a, b)


def _wkv_bwd(r, ld, k, v, a, b, s0_all, dy):
    bl, nh, t, n = r.shape
    eb, hb, lc, grid = _wkv_specs(bl, nh, t)
    nc = t // lc
    nb = eb * hb

    def body(r_ref, ld_ref, k_ref, v_ref, a_ref, b_ref, s0_ref, dy_ref,
             dr_ref, dld_ref, dk_ref, dv_ref, da_ref, db_ref, ds_scr):
        @pl.when(pl.program_id(2) == 0)
        def _():
            ds_scr[...] = jnp.zeros_like(ds_scr)

        args = (s0_ref[:, :, 0].reshape(nb, n, n),) + tuple(
            ref[...].reshape(nb, lc, n) for ref in (r_ref, ld_ref, k_ref, v_ref, a_ref, b_ref))
        _, vjp = jax.vjp(_wkv_chunk, *args)
        ds0, *dseq = vjp((dy_ref[...].reshape(nb, lc, n), ds_scr[...]))
        ds_scr[...] = ds0
        for ref, val in zip((dr_ref, dld_ref, dk_ref, dv_ref, da_ref, db_ref), dseq):
            ref[...] = val.reshape(eb, hb, lc, n)

    seq = pl.BlockSpec((eb, hb, lc, n), lambda e, h, c: (e, h, nc - 1 - c, 0))
    st = pl.BlockSpec((eb, hb, 1, n, n), lambda e, h, c: (e, h, nc - 1 - c, 0, 0))
    out = jax.ShapeDtypeStruct((bl, nh, t, n), F32)
    return pl.pallas_call(
        body, name="wkv_bwd", grid=grid,
        in_specs=[seq] * 6 + [st, seq],
        out_specs=(seq,) * 6, out_shape=(out,) * 6,
        scratch_shapes=[pltpu.VMEM((nb, n, n), F32)],
        compiler_params=_cparams(("arbitrary", "arbitrary", "arbitrary")),
    )(r, ld, k, v, a, b, s0_all, dy)


def _exchange(x, name, scatter, relay=False):
    assert not (relay and scatter)
    blk = x.shape[1:] if scatter else x.shape

    def body(x_ref, o_ref, send_sems, recv_sems, local_sem):
        pos = (lax.axis_index("x"), lax.axis_index("y"), lax.axis_index("c"))
        me = 4 * pos[0] + 2 * pos[1] + pos[2]

        def peer_of(m):
            p = tuple(1 - pos[i] if (m >> (2 - i)) & 1 else pos[i] for i in range(3))
            return p, 4 * p[0] + 2 * p[1] + p[2]

        def copy(m):
            p, pidx = peer_of(m)
            return pltpu.make_async_remote_copy(
                src_ref=x_ref.at[pidx] if scatter else x_ref, dst_ref=o_ref.at[me],
                send_sem=send_sems.at[m - 1], recv_sem=recv_sems.at[m - 1],
                device_id=p, device_id_type=pl.DeviceIdType.MESH)

        def arrival(m):
            p, pidx = peer_of(m)
            return pltpu.make_async_remote_copy(
                src_ref=x_ref.at[pidx] if scatter else x_ref, dst_ref=o_ref.at[pidx],
                send_sem=send_sems.at[m - 1], recv_sem=recv_sems.at[m - 1],
                device_id=p, device_id_type=pl.DeviceIdType.MESH)

        mine = pltpu.make_async_copy(x_ref.at[me] if scatter else x_ref, o_ref.at[me], local_sem)
        mine.start()
        if not relay:
            sends = [copy(m) for m in range(1, N_DEV)]
            for cp in sends:
                cp.start()
            for m in range(1, N_DEV):
                arrival(m).wait_recv()
        else:
            sibling, _ = peer_of(1)
            far = (2, 4, 6)

            def relay_copy(m, origin_idx):
                return pltpu.make_async_remote_copy(
                    src_ref=o_ref.at[origin_idx], dst_ref=o_ref.at[origin_idx],
                    send_sem=send_sems.at[m], recv_sem=recv_sems.at[m],
                    device_id=sibling, device_id_type=pl.DeviceIdType.MESH)

            sends = [copy(1)] + [copy(m) for m in far]
            for cp in sends:
                cp.start()
            for m in far:
                arrival(m).wait_recv()
                fwd = relay_copy(m, peer_of(m)[1])
                fwd.start()
                sends.append(fwd)
            arrival(1).wait_recv()
            for m in far:
                relay_copy(m, peer_of(m ^ 1)[1]).wait_recv()
        for cp in sends:
            cp.wait_send()
        mine.wait()

    return pl.pallas_call(
        body, name=name,
        out_shape=jax.ShapeDtypeStruct((N_DEV,) + tuple(blk), x.dtype),
        in_specs=[pl.BlockSpec(memory_space=pl.ANY)],
        out_specs=pl.BlockSpec(memory_space=pl.ANY),
        scratch_shapes=[pltpu.SemaphoreType.DMA((N_DEV - 1,)), pltpu.SemaphoreType.DMA((N_DEV - 1,)),
                        pltpu.SemaphoreType.DMA],
    )(x)


def _scatter_copies(x_refs, o_refs, send_sems, recv_sems, local_sems):
    pos = (lax.axis_index("x"), lax.axis_index("y"), lax.axis_index("c"))
    me = 4 * pos[0] + 2 * pos[1] + pos[2]

    def descriptors():
        sends, arrivals, local = [], [], []
        for i, (x_ref, o_ref) in enumerate(zip(x_refs, o_refs)):
            for m in range(1, N_DEV):
                p = tuple(1 - pos[a] if (m >> (2 - a)) & 1 else pos[a] for a in range(3))
                pidx = 4 * p[0] + 2 * p[1] + p[2]
                k = (N_DEV - 1) * i + m - 1
                for dst, out in ((o_ref.at[me], sends), (o_ref.at[pidx], arrivals)):
                    out.append(pltpu.make_async_remote_copy(
                        src_ref=x_ref.at[pidx], dst_ref=dst, send_sem=send_sems.at[k], recv_sem=recv_sems.at[k],
                        device_id=p, device_id_type=pl.DeviceIdType.MESH))
            local.append(pltpu.make_async_copy(x_ref.at[me], o_ref.at[me], local_sems.at[i]))
        return sends, arrivals, local

    def start():
        sends, _, local = descriptors()
        for cp in local + sends:
            cp.start()

    def finish():
        sends, arrivals, local = descriptors()
        for cp in arrivals:
            cp.wait_recv()
        for cp in sends:
            cp.wait_send()
        for cp in local:
            cp.wait()

    return start, finish


def _gather_copies(x_refs, o_refs, send_sems, recv_sems, local_sems):
    pos = (lax.axis_index("x"), lax.axis_index("y"), lax.axis_index("c"))
    me = 4 * pos[0] + 2 * pos[1] + pos[2]
    far = (2, 4, 6)

    def peer_of(m):
        p = tuple(1 - pos[a] if (m >> (2 - a)) & 1 else pos[a] for a in range(3))
        return p, 4 * p[0] + 2 * p[1] + p[2]

    sibling, _ = peer_of(1)

    def copy(i, k, src, slot, to):
        return pltpu.make_async_remote_copy(
            src_ref=src, dst_ref=o_refs[i].at[slot], send_sem=send_sems.at[(N_DEV - 1) * i + k],
            recv_sem=recv_sems.at[(N_DEV - 1) * i + k], device_id=to, device_id_type=pl.DeviceIdType.MESH)

    def direct(i):
        return [copy(i, m - 1, x_refs[i], me, peer_of(m)[0]) for m in (1,) + far]

    def local(i):
        return pltpu.make_async_copy(x_refs[i], o_refs[i].at[me], local_sems.at[i])

    def start():
        for i in range(len(x_refs)):
            local(i).start()
            for cp in direct(i):
                cp.start()

    def finish():
        n = len(x_refs)
        relays = []
        for i in range(n):
            for m in far:
                origin = peer_of(m)[1]
                copy(i, m - 1, x_refs[i], origin, sibling).wait_recv()
                fwd = copy(i, m, o_refs[i].at[origin], origin, sibling)
                fwd.start()
                relays.append(fwd)
        for i in range(n):
            copy(i, 0, x_refs[i], peer_of(1)[1], sibling).wait_recv()
            for m in far:
                copy(i, m, x_refs[i], peer_of(m ^ 1)[1], sibling).wait_recv()
        for i in range(n):
            for cp in direct(i):
                cp.wait_send()
            local(i).wait()
        for cp in relays:
            cp.wait_send()

    return start, finish


def _scatter_scratch(n):
    return [pltpu.SemaphoreType.DMA(((N_DEV - 1) * n,)), pltpu.SemaphoreType.DMA(((N_DEV - 1) * n,)),
            pltpu.SemaphoreType.DMA((n,))]


_ANY = pl.BlockSpec(memory_space=pl.ANY)


def _scatter_call(arrays, name):
    n = len(arrays)

    def body(*refs):
        start, finish = _scatter_copies(refs[:n], refs[n:2 * n], *refs[2 * n:])
        start()
        finish()

    return pl.pallas_call(
        body, name=name, in_specs=[_ANY] * n, out_specs=(_ANY,) * n,
        out_shape=tuple(_sds(a.shape, a.dtype) for a in arrays), scratch_shapes=_scatter_scratch(n),
    )(*arrays)


def _gather_call(arrays, name):
    n = len(arrays)

    def body(*refs):
        start, finish = _gather_copies(refs[:n], refs[n:2 * n], *refs[2 * n:])
        start()
        finish()

    return pl.pallas_call(
        body, name=name, in_specs=[_ANY] * n, out_specs=(_ANY,) * n,
        out_shape=tuple(_sds((N_DEV,) + a.shape, a.dtype) for a in arrays), scratch_shapes=_scatter_scratch(n),
    )(*arrays)


def _call_with_scatter(body, *, name, grid, in_specs, out_specs, out_shape, scratch_shapes, operands,
                       scatter=(), gather=()):
    assert not (scatter and gather)
    carried = tuple(scatter) or tuple(gather)
    copies = _scatter_copies if scatter else _gather_copies
    recv_shapes = tuple(_sds(a.shape if scatter else (N_DEV,) + a.shape, a.dtype) for a in carried)
    nc, n_in, n_out, n_scr = len(carried), len(in_specs), len(out_specs), len(scratch_shapes)
    if nc == 0:
        return pl.pallas_call(
            body, name=name, grid=grid, in_specs=list(in_specs), out_specs=tuple(out_specs),
            out_shape=tuple(out_shape), scratch_shapes=list(scratch_shapes),
            compiler_params=_cparams(("arbitrary",) * len(grid)))(*operands), ()

    def wrapped(*refs):
        ins, refs = refs[:n_in], refs[n_in:]
        c_in, refs = refs[:nc], refs[nc:]
        outs, refs = refs[:n_out], refs[n_out:]
        c_out, refs = refs[:nc], refs[nc:]
        scr, sems = refs[:n_scr], refs[n_scr:]
        ids = [pl.program_id(a) for a in range(len(grid))]
        first = functools.reduce(jnp.logical_and, [i == 0 for i in ids])
        last = functools.reduce(jnp.logical_and, [i == g - 1 for i, g in zip(ids, grid)])
        start, finish = copies(c_in, c_out, *sems)
        pl.when(first)(start)
        body(*ins, *outs, *scr)
        pl.when(last)(finish)

    res = pl.pallas_call(
        wrapped, name=name, grid=grid,
        in_specs=list(in_specs) + [_ANY] * nc, out_specs=tuple(out_specs) + (_ANY,) * nc,
        out_shape=tuple(out_shape) + recv_shapes,
        scratch_shapes=list(scratch_shapes) + _scatter_scratch(nc),
        compiler_params=_cparams(("arbitrary",) * len(grid)),
    )(*operands, *carried)
    return res[:n_out], res[n_out:]


def _rms(x):
    inv = lax.rsqrt(jnp.mean(x * x, axis=-1, keepdims=True) + RMS_EPS)
    return x * inv, inv


def _rms_bwd(xn, inv, dxn):
    return inv * (dxn - xn * jnp.mean(dxn * xn, axis=-1, keepdims=True))


def _colsum(x):
    return jnp.sum(x, axis=0, keepdims=True)


def _sigmoid(x):
    return 0.5 * (jnp.tanh(0.5 * x) + 1.0)


def _split_bf16(x):
    hi = x.astype(BF16)
    return hi, (x - hi.astype(F32)).astype(BF16)


def _dot_split(x, e):
    hi, lo = _split_bf16(x)
    return jnp.dot(hi, e, preferred_element_type=F32) + jnp.dot(lo, e, preferred_element_type=F32)


@jax.custom_vjp
def _headsum(x, e, et):
    return _dot_split(_dot_split(x, e), et)


_headsum.defvjp(lambda x, e, et: (_headsum(x, e, et), (e, et)),
                lambda res, g: (_headsum(g, *res), jnp.zeros_like(res[0]), jnp.zeros_like(res[1])))


def _make_headsum(e, et):
    return lambda x: _headsum(x, e, et)


def _head_indicators(d):
    e = (jnp.arange(d)[:, None] // HEAD == jnp.arange(128)[None, :]).astype(BF16)
    return e, e.T


def _rwkv_elem(r, k, lw, la, w0, a0, k_k, k_a, headsum):
    z = w0 + lw
    w_log = -(jnp.maximum(-z, 0.0) + jnp.log(1.0 + jnp.exp(-jnp.abs(z)))) - 0.5
    ld = -jnp.exp(w_log)
    a = _sigmoid(a0 + la)
    kkp = k * k_k
    kk = kkp / jnp.maximum(jnp.sqrt(headsum(kkp * kkp)), L2_EPS)
    k2 = k * (1.0 + (a - 1.0) * k_a)
    del r
    return ld, k2, -kk, kk * a


def _rwkv_post(y, r, k2, v, g, ln_g, ln_b, r_k, headsum):
    m = headsum(y) * (1.0 / HEAD)
    yc = y - m
    var = headsum(yc * yc) * (1.0 / HEAD)
    yn = yc * lax.rsqrt(var + GN_EPS)
    bonus = headsum(r * k2 * r_k) * v
    return (yn * ln_g + ln_b + bonus) * g


def _shift_down(h, first_row):
    rolled = pltpu.roll(h, 1, 0)
    row = lax.broadcasted_iota(jnp.int32, h.shape, 0)
    return jnp.where(row == 0, first_row, rolled)


def _shift_up(h, last_row):
    n = h.shape[0]
    rolled = pltpu.roll(h, n - 1, 0)
    row = lax.broadcasted_iota(jnp.int32, h.shape, 0)
    return jnp.where(row == n - 1, last_row, rolled)


def _gelu(p):
    return 0.5 * p * (1.0 + lax.erf(p * 0.7071067811865476))


def _gelu_grad(p):
    return 0.5 * (1.0 + lax.erf(p * 0.7071067811865476)) + p * jnp.exp(-0.5 * p * p) * 0.3989422804014327


def _tok(tm, d):
    return pl.BlockSpec((1, tm, d), lambda e, t, *_: (e, t, 0))


def _per_example(rows, d):
    return pl.BlockSpec((1, rows, d), lambda e, t, *_: (e, 0, 0))


def _whole(shape):
    nd = len(shape)
    return pl.BlockSpec(tuple(shape), lambda *_: (0,) * nd)


def _heads(nh, tm):
    return pl.BlockSpec((1, nh, tm, HEAD), lambda e, t, *_: (e, 0, t, 0))


def _sds(shape, dtype=F32):
    return jax.ShapeDtypeStruct(tuple(shape), dtype)


def _add_rows(ref, first, rows):
    @pl.when(first)
    def _():
        ref[0] = jnp.zeros(ref.shape[1:], ref.dtype)

    for i, r in enumerate(rows):
        ref[0, i:i + 1] += r


def _first(e, t):
    return jnp.logical_and(e == 0, t == 0)


def _ada_fwd(c_all, ada_w, ada_b_cols):
    nl, d, cols = ada_w.shape
    nb = c_all.shape[0]

    def body(c_ref, w_ref, b_ref, o_ref):
        c = c_ref[...]
        cond = c * _sigmoid(c)
        for i in range(nl):
            o_ref[i] = _dot(cond, w_ref[i]) + b_ref[i]

    return pl.pallas_call(
        body, name="ada_fwd", out_shape=_sds((nl, nb, cols)),
        compiler_params=_cparams(),
    )(c_all, ada_w, ada_b_cols)


def _ada_bwd(c_all, dmod_cols, dmod_full):
    nl, nb, cols = dmod_cols.shape
    d = c_all.shape[1]

    def body(c_ref, g_ref, f_ref, b_ref, *o_refs):
        c = c_ref[...]
        cond = c * _sigmoid(c)
        for i in range(nl):
            o_refs[i][0] = _dot_tn(cond, g_ref[i])
            b_ref[i:i + 1] = jnp.sum(f_ref[i], axis=0, keepdims=True)

    res = pl.pallas_call(
        body, name="ada_bwd", out_shape=(_sds((nl, dmod_full.shape[2])),) + (_sds((1, d, cols)),) * nl,
        compiler_params=_cparams(),
    )(c_all, dmod_cols, dmod_full)
    return res[1:], res[0]


def _matmul_tn(a, b, name, col_shards=None):
    m, ka = a.shape
    n = b.shape[1]
    tm = min(m, 2048)
    tk = min(ka, 1024)
    tn = n // col_shards if col_shards else min(n, 512)
    steps = m // tm
    if col_shards:
        out_spec = pl.BlockSpec((None, tk, tn), lambda i, j, s: (j, i, 0))
        out_shape = _sds((col_shards, ka, tn), BF16)
    else:
        out_spec = pl.BlockSpec((tk, tn), lambda i, j, s: (i, j))
        out_shape = _sds((ka, n), BF16)

    def body(a_ref, b_ref, o_ref, acc):
        s = pl.program_id(2)

        @pl.when(s == 0)
        def _():
            acc[...] = jnp.zeros_like(acc)

        acc[...] += _dot_tn(a_ref[...], b_ref[...])

        @pl.when(s == steps - 1)
        def _():
            o_ref[...] = acc[...].astype(BF16)

    return pl.pallas_call(
        body, name=name, grid=(ka // tk, n // tn, steps),
        in_specs=[pl.BlockSpec((tm, tk), lambda i, j, s: (s, i)), pl.BlockSpec((tm, tn), lambda i, j, s: (s, j))],
        out_specs=out_spec, out_shape=out_shape,
        scratch_shapes=[pltpu.VMEM((tk, tn), F32)],
        compiler_params=_cparams(("parallel", "parallel", "arbitrary")),
    )(a, b)


MLP_TM = 512
MLP_FJ = 1024


def _mlp_fwd(x, mod, w1, w2, gather=()):
    bl, t, d = x.shape
    f = w1.shape[1]
    tm, fj = min(t, MLP_TM), min(f, MLP_FJ)
    nj = f // fj

    def body(x_ref, mod_ref, w1_ref, w2_ref, xo_ref, ff_ref, h_scr, acc):
        j = pl.program_id(2)

        @pl.when(j == 0)
        def _():
            xn, _ = _rms(x_ref[0])
            h_scr[...] = (xn * (1.0 + mod_ref[0, 1:2]) + mod_ref[0, 0:1]).astype(BF16)
            acc[...] = jnp.zeros_like(acc)

        p = jnp.dot(h_scr[...], w1_ref[...], preferred_element_type=F32)
        q = jnp.square(jnp.maximum(p, 0.0))
        acc[...] += _dot(q, w2_ref[...])

        @pl.when(j == nj - 1)
        def _():
            ff_ref[0] = acc[...]
            xo_ref[0] = x_ref[0] + mod_ref[0, 2:3] * acc[...]

    return _call_with_scatter(
        body, name="mlp_fwd", grid=(bl, t // tm, nj),
        in_specs=[_tok(tm, d), _per_example(8, d),
                  pl.BlockSpec((d, fj), lambda e, i, j: (0, j)), pl.BlockSpec((fj, d), lambda e, i, j: (j, 0))],
        out_specs=(_tok(tm, d), _tok(tm, d)),
        out_shape=(_sds(x.shape), _sds(x.shape)),
        scratch_shapes=[pltpu.VMEM((tm, d), BF16), pltpu.VMEM((tm, d), F32)],
        operands=(x, mod, w1, w2), gather=gather)


def _mlp_bwd(x, dxo, ff, mod, w1, w2, scatter=()):
    bl, t, d = x.shape
    f = w1.shape[1]
    tm, fj = min(t, MLP_TM), min(f, MLP_FJ)
    nj = f // fj

    def body(x_ref, dxo_ref, ff_ref, mod_ref, w1_ref, w2_ref,
             dx_ref, dmod_ref, h_ref, dff_ref, q_ref, dp_ref, acc):
        ti, j = pl.program_id(1), pl.program_id(2)

        @pl.when(j == 0)
        def _():
            xn, _ = _rms(x_ref[0])
            h_ref[0] = (xn * (1.0 + mod_ref[0, 1:2]) + mod_ref[0, 0:1]).astype(BF16)
            dff_ref[0] = (mod_ref[0, 2:3] * dxo_ref[0]).astype(BF16)
            acc[...] = jnp.zeros_like(acc)

        p = jnp.dot(h_ref[0], w1_ref[...], preferred_element_type=F32)
        rl = jnp.maximum(p, 0.0)
        q_ref[0] = jnp.square(rl).astype(BF16)
        dp = (_dot_nt(dff_ref[0], w2_ref[...]) * (2.0 * rl)).astype(BF16)
        dp_ref[0] = dp
        acc[...] += _dot_nt(dp, w1_ref[...])

        @pl.when(j == nj - 1)
        def _():
            xn, inv = _rms(x_ref[0])
            dh = acc[...]
            dx_ref[0] = dxo_ref[0] + _rms_bwd(xn, inv, dh * (1.0 + mod_ref[0, 1:2]))
            _add_rows(dmod_ref, ti == 0, [_colsum(dh), _colsum(dh * xn), _colsum(dxo_ref[0] * ff_ref[0])])

    big = lambda: pl.BlockSpec((1, tm, fj), lambda e, i, j: (e, i, j))
    return _call_with_scatter(
        body, name="mlp_bwd", grid=(bl, t // tm, nj),
        in_specs=[_tok(tm, d), _tok(tm, d), _tok(tm, d), _per_example(8, d),
                  pl.BlockSpec((d, fj), lambda e, i, j: (0, j)), pl.BlockSpec((fj, d), lambda e, i, j: (j, 0))],
        out_specs=(_tok(tm, d), _per_example(8, d), _tok(tm, d), _tok(tm, d), big(), big()),
        out_shape=(_sds(x.shape), _sds((bl, 8, d)), _sds(x.shape, BF16), _sds(x.shape, BF16),
                   _sds((bl, t, f), BF16), _sds((bl, t, f), BF16)),
        scratch_shapes=[pltpu.VMEM((tm, d), F32)],
        operands=(x, dxo, ff, mod, w1, w2), scatter=scatter)


SGU_TM = 256


def _sgu_core(x, mod_ref, win_ref, lng, lnb, ws_ref, bias_ref):
    tm, d = x.shape
    xn, inv = _rms(x)
    h = (xn * (1.0 + mod_ref[0, 1:2]) + mod_ref[0, 0:1]).astype(BF16)
    pre = jnp.dot(h, win_ref[...], preferred_element_type=F32)
    uv = _gelu(pre)
    u, v = uv[:, :d], uv[:, d:]
    mu = jnp.mean(v, axis=-1, keepdims=True)
    vc = v - mu
    rstd = lax.rsqrt(jnp.mean(vc * vc, axis=-1, keepdims=True) + LN_EPS)
    vhat = vc * rstd
    vln = vhat * lng + lnb
    gd = d // SGU_GROUPS
    rows = []
    for c in range(tm // SGU_CHUNK):
        cols = []
        for g in range(SGU_GROUPS):
            cols.append(_dot(ws_ref[g], vln[c * SGU_CHUNK:(c + 1) * SGU_CHUNK, g * gd:(g + 1) * gd]))
        rows.append(jnp.concatenate(cols, axis=1) + bias_ref[...])
    sv = jnp.concatenate(rows, axis=0)
    return xn, inv, h, pre, u, vhat, rstd, vln, sv


def _sgu_masked(ws_ref, wm_scr):
    row = lax.broadcasted_iota(jnp.int32, (SGU_CHUNK, SGU_CHUNK), 0)
    col = lax.broadcasted_iota(jnp.int32, (SGU_CHUNK, SGU_CHUNK), 1)
    for g in range(SGU_GROUPS):
        wm_scr[g] = jnp.where(row >= col, ws_ref[g], 0.0).astype(BF16)


def _sgu_fwd(x, mod, w_in, ln_g, ln_b, w_s, bias_full, w_out, gather=()):
    bl, t, d = x.shape
    tm = min(t, SGU_TM)

    def body(x_ref, mod_ref, win_ref, lng_ref, lnb_ref, ws_ref, bias_ref, wout_ref, xo_ref, mix_ref, wm_scr):
        _sgu_masked(ws_ref, wm_scr)
        xt = x_ref[0]
        *_, u, _, _, _, sv = _sgu_core(xt, mod_ref, win_ref, lng_ref[...], lnb_ref[...], wm_scr, bias_ref)
        mix = _dot(u * sv, wout_ref[...])
        mix_ref[0] = mix
        xo_ref[0] = xt + mod_ref[0, 2:3] * mix

    return _call_with_scatter(
        body, name="sgu_fwd", grid=(bl, t // tm),
        in_specs=[_tok(tm, d), _per_example(8, d), _whole(w_in.shape), _whole(ln_g.shape), _whole(ln_b.shape),
                  _whole(w_s.shape), _whole(bias_full.shape), _whole(w_out.shape)],
        out_specs=(_tok(tm, d), _tok(tm, d)),
        out_shape=(_sds(x.shape), _sds(x.shape)),
        scratch_shapes=[pltpu.VMEM(w_s.shape, BF16)],
        operands=(x, mod, w_in, ln_g, ln_b, w_s, bias_full, w_out), gather=gather)


def _sgu_bwd(x, dxo, mix, mod, w_in, ln_g, ln_b, w_s, bias_full, w_out, group_ind, scatter=()):
    bl, t, d = x.shape
    tm = min(t, SGU_TM)
    gd = d // SGU_GROUPS

    def body(x_ref, dxo_ref, mix_ref, mod_ref, win_ref, lng_ref, lnb_ref, ws_ref, bias_ref, wout_ref, ind_ref,
             dx_ref, dmod_ref, h_ref, dpre_ref, z_ref, dmix_ref, small_ref, dws_ref, dbs_ref, wm_scr, dbias_scr):
        e, ti = pl.program_id(0), pl.program_id(1)
        _sgu_masked(ws_ref, wm_scr)
        xt, dxo = x_ref[0], dxo_ref[0]
        lng = lng_ref[...]
        xn, inv, h, pre, u, vhat, rstd, vln, sv = _sgu_core(xt, mod_ref, win_ref, lng, lnb_ref[...], wm_scr, bias_ref)
        h_ref[0] = h
        z_ref[0] = (u * sv).astype(BF16)
        dmix = mod_ref[0, 2:3] * dxo
        dmix_ref[0] = dmix.astype(BF16)
        dz = _dot_nt(dmix, wout_ref[...])
        du, dsv = dz * sv, dz * u

        @pl.when(_first(e, ti))
        def _():
            dws_ref[...] = jnp.zeros_like(dws_ref)
            dbias_scr[...] = jnp.zeros_like(dbias_scr)
            small_ref[...] = jnp.zeros_like(small_ref)

        row = lax.broadcasted_iota(jnp.int32, (SGU_CHUNK, SGU_CHUNK), 0)
        col = lax.broadcasted_iota(jnp.int32, (SGU_CHUNK, SGU_CHUNK), 1)
        rows = []
        for c in range(tm // SGU_CHUNK):
            rs = slice(c * SGU_CHUNK, (c + 1) * SGU_CHUNK)
            dbias_scr[...] += dsv[rs]
            cols = []
            for g in range(SGU_GROUPS):
                cs = slice(g * gd, (g + 1) * gd)
                cols.append(_dot_tn(wm_scr[g], dsv[rs, cs]))
                dws_ref[g] += jnp.where(row >= col, _dot_nt(dsv[rs, cs], vln[rs, cs]), 0.0)
            rows.append(jnp.concatenate(cols, axis=1))
        dvln = jnp.concatenate(rows, axis=0)
        small_ref[0:1] += _colsum(dvln * vhat)
        small_ref[1:2] += _colsum(dvln)
        dvhat = dvln * lng
        dv = rstd * (dvhat - jnp.mean(dvhat, axis=-1, keepdims=True)
                     - vhat * jnp.mean(dvhat * vhat, axis=-1, keepdims=True))
        dpre = (jnp.concatenate([du, dv], axis=1) * _gelu_grad(pre)).astype(BF16)
        dpre_ref[0] = dpre
        dh = _dot_nt(dpre, win_ref[...])
        dx_ref[0] = dxo + _rms_bwd(xn, inv, dh * (1.0 + mod_ref[0, 1:2]))
        _add_rows(dmod_ref, ti == 0, [_colsum(dh), _colsum(dh * xn), _colsum(dxo * mix_ref[0])])

        @pl.when(jnp.logical_and(e == bl - 1, ti == t // tm - 1))
        def _():
            hi, lo = _split_bf16(dbias_scr[...])
            ind = ind_ref[...]
            dbs_ref[...] = (lax.dot_general(ind, hi, (((1,), (1,)), ((), ())), preferred_element_type=F32)
                            + lax.dot_general(ind, lo, (((1,), (1,)), ((), ())), preferred_element_type=F32))

    return _call_with_scatter(
        body, name="sgu_bwd", grid=(bl, t // tm),
        in_specs=[_tok(tm, d), _tok(tm, d), _tok(tm, d), _per_example(8, d), _whole(w_in.shape), _whole(ln_g.shape),
                  _whole(ln_b.shape), _whole(w_s.shape), _whole(bias_full.shape), _whole(w_out.shape),
                  _whole(group_ind.shape)],
        out_specs=(_tok(tm, d), _per_example(8, d), _tok(tm, d), _tok(tm, 2 * d), _tok(tm, d), _tok(tm, d),
                   _whole((8, d)), _whole(w_s.shape), _whole((SGU_GROUPS, SGU_CHUNK))),
        out_shape=(_sds(x.shape), _sds((bl, 8, d)), _sds(x.shape, BF16), _sds((bl, t, 2 * d), BF16),
                   _sds(x.shape, BF16), _sds(x.shape, BF16), _sds((8, d)), _sds(w_s.shape),
                   _sds((SGU_GROUPS, SGU_CHUNK))),
        scratch_shapes=[pltpu.VMEM(w_s.shape, BF16), pltpu.VMEM((SGU_CHUNK, d), F32)],
        operands=(x, dxo, mix, mod, w_in, ln_g, ln_b, w_s, bias_full, w_out, group_ind), scatter=scatter)


RWKV_TM = 256
N_VEC = 16


def _rwkv_pre_core(x_ref, halo_ref, mod_ref, vec_ref, ti):
    xn, inv = _rms(x_ref[0])
    scale1, shift = 1.0 + mod_ref[0, 1:2], mod_ref[0, 0:1]
    h = xn * scale1 + shift
    hn, _ = _rms(halo_ref[0])
    hh = hn * scale1 + shift
    first = jnp.where(ti == 0, 0.0, hh[7:8])
    xx = _shift_down(h, first) - h
    xs = [h + xx * vec_ref[i:i + 1] for i in range(6)]
    return xn, inv, xx, xs


def _rwkv_proj(xs, wrkv_ref, w1_ref, a1_ref, g1_ref, w2_ref, a2_ref, g2_ref):
    d = xs[0].shape[1]
    xr, xw, xk, xv, xa, xg = [z.astype(BF16) for z in xs]
    r = jnp.dot(xr, wrkv_ref[:, 0:d], preferred_element_type=F32)
    k = jnp.dot(xk, wrkv_ref[:, d:2 * d], preferred_element_type=F32)
    v = jnp.dot(xv, wrkv_ref[:, 2 * d:3 * d], preferred_element_type=F32)
    tw2 = jnp.tanh(jnp.dot(xw, w1_ref[...], preferred_element_type=F32))
    ta = jnp.dot(xa, a1_ref[...], preferred_element_type=F32)
    sg = _sigmoid(jnp.dot(xg, g1_ref[...], preferred_element_type=F32))
    lw, la, g = _dot(tw2, w2_ref[...]), _dot(ta, a2_ref[...]), _dot(sg, g2_ref[...])
    return (xr, xw, xk, xv, xa, xg), r, k, v, tw2, ta, sg, lw, la, g


def _to_heads(ref, val, nh):
    for hd in range(nh):
        ref[0, hd] = val[:, hd * HEAD:(hd + 1) * HEAD]


def _from_heads(ref, scr, nh):
    for hd in range(nh):
        scr[:, hd * HEAD:(hd + 1) * HEAD] = ref[0, hd]
    return scr[...]


def _rwkv_weight_specs(ws):
    return [_whole(w.shape) for w in ws]


def _rwkv_pre_fwd(x, mod, vec, e_ind, et_ind, weights):
    bl, t, d = x.shape
    tm = min(t, RWKV_TM)
    nh = d // HEAD
    hb = tm // 8

    def body(x_ref, halo_ref, mod_ref, vec_ref, e_ref, et_ref, wrkv, w1, a1, g1, w2, a2, g2,
             r_ref, ld_ref, k2_ref, v_ref, as_ref, bs_ref, g_ref):
        ti = pl.program_id(1)
        _, _, _, xs = _rwkv_pre_core(x_ref, halo_ref, mod_ref, vec_ref, ti)
        _, r, k, v, _, _, _, lw, la, g = _rwkv_proj(xs, wrkv, w1, a1, g1, w2, a2, g2)
        headsum = _make_headsum(e_ref[...], et_ref[...])
        ld, k2, a_s, b_s = _rwkv_elem(r, k, lw, la, vec_ref[6:7], vec_ref[7:8], vec_ref[8:9], vec_ref[9:10], headsum)
        g_ref[0] = g
        for ref, val in ((r_ref, r), (ld_ref, ld), (k2_ref, k2), (v_ref, v), (as_ref, a_s), (bs_ref, b_s)):
            _to_heads(ref, val, nh)

    halo = pl.BlockSpec((1, 8, d), lambda e, i: (e, jnp.maximum(i * hb - 1, 0), 0))
    hs = _sds((bl, nh, t, HEAD))
    return pl.pallas_call(
        body, name="rwkv_pre_fwd", grid=(bl, t // tm),
        in_specs=[_tok(tm, d), halo, _per_example(8, d), _whole(vec.shape), _whole(e_ind.shape), _whole(et_ind.shape)]
        + _rwkv_weight_specs(weights),
        out_specs=(_heads(nh, tm),) * 6 + (_tok(tm, d),),
        out_shape=(hs,) * 6 + (_sds(x.shape),),
        compiler_params=_cparams(("arbitrary", "arbitrary")),
    )(x, x, mod, vec, e_ind, et_ind, *weights)


def _rwkv_post_fwd(x, y, r, k2, v, g, mod, vec, e_ind, et_ind, w_out):
    bl, t, d = x.shape
    tm = min(t, RWKV_TM)
    nh = d // HEAD

    def body(x_ref, y_ref, r_ref, k2_ref, v_ref, g_ref, mod_ref, vec_ref, e_ref, et_ref, wout_ref,
             xo_ref, mix_ref, s0, s1, s2, s3):
        headsum = _make_headsum(e_ref[...], et_ref[...])
        yv, rv, kv, vv = (_from_heads(ref, scr, nh) for ref, scr in
                          ((y_ref, s0), (r_ref, s1), (k2_ref, s2), (v_ref, s3)))
        o = _rwkv_post(yv, rv, kv, vv, g_ref[0], vec_ref[10:11], vec_ref[11:12], vec_ref[12:13], headsum)
        mix = _dot(o, wout_ref[...])
        mix_ref[0] = mix
        xo_ref[0] = x_ref[0] + mod_ref[0, 2:3] * mix

    return pl.pallas_call(
        body, name="rwkv_post_fwd", grid=(bl, t // tm),
        in_specs=[_tok(tm, d)] + [_heads(nh, tm)] * 4 + [_tok(tm, d), _per_example(8, d), _whole(vec.shape),
                                                         _whole(e_ind.shape), _whole(et_ind.shape), _whole(w_out.shape)],
        out_specs=(_tok(tm, d), _tok(tm, d)),
        out_shape=(_sds(x.shape), _sds(x.shape)),
        scratch_shapes=[pltpu.VMEM((tm, d), F32)] * 4,
        compiler_params=_cparams(("arbitrary", "arbitrary")),
    )(x, y, r, k2, v, g, mod, vec, e_ind, et_ind, w_out)


def _rwkv_post_bwd(dxo, mix, y, r, k2, v, g, mod, vec, e_ind, et_ind, w_out, scatter=()):
    bl, t, d = dxo.shape
    tm = min(t, RWKV_TM)
    nh = d // HEAD

    def body(dxo_ref, mix_ref, y_ref, r_ref, k2_ref, v_ref, g_ref, mod_ref, vec_ref, e_ref, et_ref, wout_ref,
             dy_ref, dr_ref, dk2_ref, dv_ref, dg_ref, o_ref, dmix_ref, dgate_ref, small_ref, s0, s1, s2, s3):
        e, ti = pl.program_id(0), pl.program_id(1)
        headsum = _make_headsum(e_ref[...], et_ref[...])
        yv, rv, kv, vv = (_from_heads(ref, scr, nh) for ref, scr in
                          ((y_ref, s0), (r_ref, s1), (k2_ref, s2), (v_ref, s3)))
        dxo = dxo_ref[0]
        dmix = mod_ref[0, 2:3] * dxo
        dmix_ref[0] = dmix.astype(BF16)
        do = _dot_nt(dmix, wout_ref[...])
        post = functools.partial(_rwkv_post, headsum=headsum)
        o, vjp = jax.vjp(post, yv, rv, kv, vv, g_ref[0], vec_ref[10:11], vec_ref[11:12], vec_ref[12:13])
        o_ref[0] = o.astype(BF16)
        dy, dr, dk2, dv, dg, dlng, dlnb, drk = vjp(do)
        _to_heads(dy_ref, dy, nh)
        dr_ref[0], dk2_ref[0], dv_ref[0], dg_ref[0] = dr, dk2, dv, dg
        zero = jnp.zeros((1, d), F32)
        _add_rows(dgate_ref, ti == 0, [zero, zero, _colsum(dxo * mix_ref[0])])

        @pl.when(_first(e, ti))
        def _():
            small_ref[...] = jnp.zeros_like(small_ref)

        small_ref[0:1] += dlng
        small_ref[1:2] += dlnb
        small_ref[2:3] += drk

    return _call_with_scatter(
        body, name="rwkv_post_bwd", grid=(bl, t // tm),
        in_specs=[_tok(tm, d), _tok(tm, d)] + [_heads(nh, tm)] * 4
        + [_tok(tm, d), _per_example(8, d), _whole(vec.shape), _whole(e_ind.shape), _whole(et_ind.shape),
           _whole(w_out.shape)],
        out_specs=(_heads(nh, tm),) + (_tok(tm, d),) * 6 + (_per_example(8, d), _whole((8, d))),
        out_shape=(_sds((bl, nh, t, HEAD)),) + (_sds(dxo.shape),) * 4 + (_sds(dxo.shape, BF16),) * 2
        + (_sds((bl, 8, d)), _sds((8, d))),
        scratch_shapes=[pltpu.VMEM((tm, d), F32)] * 4,
        operands=(dxo, mix, y, r, k2, v, g, mod, vec, e_ind, et_ind, w_out), scatter=scatter)


RWKV_BWD_TM = 128


def _rwkv_pre_bwd(x, mod, vec, e_ind, et_ind, weights, dr_p, dk2_p, dv_p, dg, dr_s, dld, dk2_s, dv_s, das, dbs):
    bl, t, d = x.shape
    tm = min(t, RWKV_BWD_TM)
    nh = d // HEAD
    hb = tm // 8
    lp, gp = LORA_PAD, GATE_PAD

    def body(x_ref, halo_ref, mod_ref, vec_ref, e_ref, et_ref, wrkv, w1, a1, g1, w2, a2, g2,
             drp_ref, dk2p_ref, dvp_ref, dg_ref, drs_ref, dld_ref, dk2s_ref, dvs_ref, das_ref, dbs_ref,
             dh_ref, dhp_ref, xr_ref, xw_ref, xk_ref, xv_ref, xa_ref, xg_ref, dr_ref, dk_ref, dv_ref,
             dtw_ref, dta_ref, dtg_ref, tw2_ref, ta_ref, sg_ref, dlw_ref, dla_ref, dgb_ref, small_ref,
             s0, s1, s2, s3, s4, s5):
        e, ti = pl.program_id(0), pl.program_id(1)
        _, _, xx, xs = _rwkv_pre_core(x_ref, halo_ref, mod_ref, vec_ref, ti)
        xb, r, k, v, tw2, ta, sg, lw, la, _ = _rwkv_proj(xs, wrkv, w1, a1, g1, w2, a2, g2)
        for ref, val in zip((xr_ref, xw_ref, xk_ref, xv_ref, xa_ref, xg_ref), xb):
            ref[0] = val
        headsum = _make_headsum(e_ref[...], et_ref[...])
        drs, dld, dk2s, dvs, das, dbs_ = (_from_heads(ref, scr, nh) for ref, scr in
                                          ((drs_ref, s0), (dld_ref, s1), (dk2s_ref, s2), (dvs_ref, s3),
                                           (das_ref, s4), (dbs_ref, s5)))
        elem = functools.partial(_rwkv_elem, r, headsum=headsum)
        _, vjp = jax.vjp(elem, k, lw, la, vec_ref[6:7], vec_ref[7:8], vec_ref[8:9], vec_ref[9:10])
        dk, dlw, dla, dw0, da0, dkk, dka = vjp((dld, dk2p_ref[0] + dk2s, das, dbs_))
        dr = drp_ref[0] + drs
        dv = dvp_ref[0] + dvs
        dgv = dg_ref[0]
        dtg = _dot_nt(dgv, g2[...]) * sg * (1.0 - sg)
        dtw = _dot_nt(dlw, w2[...]) * (1.0 - tw2 * tw2)
        dta = _dot_nt(dla, a2[...])
        dr_ref[0], dk_ref[0], dv_ref[0] = dr.astype(BF16), dk.astype(BF16), dv.astype(BF16)
        dtw_ref[0], dta_ref[0], dtg_ref[0] = dtw.astype(BF16), dta.astype(BF16), dtg.astype(BF16)
        tw2_ref[0], ta_ref[0], sg_ref[0] = tw2.astype(BF16), ta.astype(BF16), sg.astype(BF16)
        dlw_ref[0], dla_ref[0], dgb_ref[0] = dlw.astype(BF16), dla.astype(BF16), dgv.astype(BF16)
        dxs = (_dot_nt(dr, wrkv[:, 0:d]), _dot_nt(dtw, w1[...]), _dot_nt(dk, wrkv[:, d:2 * d]),
               _dot_nt(dv, wrkv[:, 2 * d:3 * d]), _dot_nt(dta, a1[...]), _dot_nt(dtg, g1[...]))

        @pl.when(_first(e, ti))
        def _():
            small_ref[...] = jnp.zeros_like(small_ref)

        dh = jnp.zeros((tm, d), F32)
        dhp = jnp.zeros((tm, d), F32)
        for i, dxi in enumerate(dxs):
            mu = vec_ref[i:i + 1]
            dh += dxi * (1.0 - mu)
            dhp += dxi * mu
            small_ref[i:i + 1] += _colsum(dxi * xx)
        dh_ref[0], dhp_ref[0] = dh, dhp
        small_ref[6:7] += dw0
        small_ref[7:8] += da0
        small_ref[8:9] += dkk
        small_ref[9:10] += dka

    halo = pl.BlockSpec((1, 8, d), lambda e, i: (e, jnp.maximum(i * hb - 1, 0), 0))
    tokd, tokl, tokg = _tok(tm, d), _tok(tm, lp), _tok(tm, gp)
    bf = lambda w: _sds((bl, t, w), BF16)
    return pl.pallas_call(
        body, name="rwkv_pre_bwd", grid=(bl, t // tm),
        in_specs=[tokd, halo, _per_example(8, d), _whole(vec.shape), _whole(e_ind.shape), _whole(et_ind.shape)]
        + _rwkv_weight_specs(weights) + [tokd] * 4 + [_heads(nh, tm)] * 6,
        out_specs=(tokd, tokd) + (tokd,) * 6 + (tokd,) * 3 + (tokl, tokl, tokg, tokl, tokl, tokg)
        + (tokd, tokd, tokd, _whole((N_VEC, d))),
        out_shape=(_sds(x.shape), _sds(x.shape)) + (bf(d),) * 9 + (bf(lp), bf(lp), bf(gp), bf(lp), bf(lp), bf(gp))
        + (bf(d), bf(d), bf(d), _sds((N_VEC, d))),
        scratch_shapes=[pltpu.VMEM((tm, d), F32)] * 6,
        compiler_params=_cparams(("arbitrary", "arbitrary")),
    )(x, x, mod, vec, e_ind, et_ind, *weights, dr_p, dk2_p, dv_p, dg, dr_s, dld, dk2_s, dv_s, das, dbs)


def _norm_bwd(x, dxo, dh, dhprev, mod, dgate):
    bl, t, d = x.shape
    tm = min(t, RWKV_TM)
    hb = tm // 8
    last_blk = t // 8 - 1

    def body(x_ref, dxo_ref, dh_ref, dhp_ref, nxt_ref, mod_ref, dgate_ref, dx_ref, dmod_ref):
        ti = pl.program_id(1)
        xn, inv = _rms(x_ref[0])
        last = jnp.where(ti == t // tm - 1, 0.0, nxt_ref[0, 0:1])
        dh = dh_ref[0] + _shift_up(dhp_ref[0], last)
        dx_ref[0] = dxo_ref[0] + _rms_bwd(xn, inv, dh * (1.0 + mod_ref[0, 1:2]))

        @pl.when(ti == 0)
        def _():
            dmod_ref[0] = dgate_ref[0]

        dmod_ref[0, 0:1] += _colsum(dh)
        dmod_ref[0, 1:2] += _colsum(dh * xn)

    nxt = pl.BlockSpec((1, 8, d), lambda e, i: (e, jnp.minimum((i + 1) * hb, last_blk), 0))
    return pl.pallas_call(
        body, name="norm_bwd", grid=(bl, t // tm),
        in_specs=[_tok(tm, d)] * 4 + [nxt, _per_example(8, d), _per_example(8, d)],
        out_specs=(_tok(tm, d), _per_example(8, d)),
        out_shape=(_sds(x.shape), _sds((bl, 8, d))),
        compiler_params=_cparams(("arbitrary", "arbitrary")),
    )(x, dxo, dh, dhprev, dhprev, mod, dgate)


def _final(x, target, final_g):
    bl, t, d = x.shape
    tm = min(t, 512)

    def body(x_ref, tgt_ref, g_ref, dx_ref, loss_ref, dg_ref):
        e, ti = pl.program_id(0), pl.program_id(1)

        @pl.when(_first(e, ti))
        def _():
            loss_ref[...] = jnp.zeros_like(loss_ref)
            dg_ref[...] = jnp.zeros_like(dg_ref)

        xn, inv = _rms(x_ref[0])
        err = xn * g_ref[...] - tgt_ref[0]
        loss_ref[...] += (0.5 / d) * jnp.sum(err * err)
        dy = err * (1.0 / d)
        dg_ref[0:1] += _colsum(dy * xn)
        dx_ref[0] = _rms_bwd(xn, inv, dy * g_ref[...])

    return pl.pallas_call(
        body, name="final_loss", grid=(bl, t // tm),
        in_specs=[_tok(tm, d), _tok(tm, d), _whole(final_g.shape)],
        out_specs=(_tok(tm, d), _whole((8, 128)), _whole((8, d))),
        out_shape=(_sds(x.shape), _sds((8, 128)), _sds((8, d))),
        compiler_params=_cparams(("arbitrary", "arbitrary")),
    )(x, target, final_g)


def _adamw_math(w, g, m, v):
    m = ADAM_B1 * m + (1.0 - ADAM_B1) * g
    v = ADAM_B2 * v + (1.0 - ADAM_B2) * jnp.square(g)
    m_hat = m / (1.0 - ADAM_B1 ** ADAM_STEP)
    v_hat = v / (1.0 - ADAM_B2 ** ADAM_STEP)
    return -ADAM_LR * (m_hat / (jnp.sqrt(v_hat) + ADAM_EPS) + ADAM_WD * w), m, v


def _sum_parts(ref, n):
    g = ref[0].astype(F32)
    for s in range(1, n):
        g = g + ref[s].astype(F32)
    return g


def _adamw_layers(w, m, v, parts, name):
    nl, rows, c = w.shape
    tr = min(rows, 128)

    def body(w_ref, m_ref, v_ref, *refs):
        p_refs, (g_ref, d_ref, mo_ref, vo_ref) = refs[:nl], refs[nl:]
        for layer in range(nl):
            @pl.when(pl.program_id(0) == layer)
            def _(p_ref=p_refs[layer]):
                g = _sum_parts(p_ref, p_ref.shape[0])
                g_ref[...] = g
                d_ref[...], mo_ref[...], vo_ref[...] = _adamw_math(w_ref[...], g, m_ref[...], v_ref[...])

    row = pl.BlockSpec((None, tr, c), lambda l, i: (l, i, 0))
    return pl.pallas_call(
        body, name=name, grid=(nl, rows // tr),
        in_specs=[row, row, row] + [pl.BlockSpec((p.shape[0], tr, c), lambda l, i: (0, i, 0)) for p in parts],
        out_specs=(row,) * 4, out_shape=(_sds(w.shape),) * 4,
        compiler_params=_cparams(("arbitrary", "arbitrary")),
    )(w, m, v, *parts)


def _adamw_small(items, name):
    k = len(items)
    ns = [it[3].shape[0] for it in items]

    def body(*refs):
        ins, outs = refs[:4 * k], refs[4 * k:]
        for i in range(k):
            w_ref, m_ref, v_ref, p_ref = ins[4 * i:4 * i + 4]
            g = _sum_parts(p_ref, ns[i])
            outs[4 * i][...] = g
            outs[4 * i + 1][...], outs[4 * i + 2][...], outs[4 * i + 3][...] = _adamw_math(
                w_ref[...], g, m_ref[...], v_ref[...])

    flat = [a for it in items for a in it]
    res = pl.pallas_call(
        body, name=name,
        out_shape=tuple(_sds(it[0].shape) for it in items for _ in range(4)),
        compiler_params=_cparams(),
    )(*flat)
    return [tuple(res[4 * i:4 * i + 4]) for i in range(k)]


WEIGHTS = ['ada_w', 'ada_b', 'mlp_w1', 'mlp_w2', 'a_w_in', 'a_ln_g', 'a_ln_b', 'a_w_s', 'a_b_s', 'a_w_out', 'b_mu',
           'b_w_in', 'b_w0', 'b_w1', 'b_w2', 'b_a0', 'b_a1', 'b_a2', 'b_g1', 'b_g2', 'b_k_k', 'b_k_a', 'b_r_k',
           'b_ln_g', 'b_ln_b', 'b_w_out', 'final_g']
GATHERED = [('mlp_w1', 2), ('mlp_w2', 1), ('a_w_in', 2), ('a_w_out', 1), ('b_w_in', 2), ('b_w_out', 1),
            ('b_w1', 1), ('b_a1', 1), ('b_g1', 1), ('b_w2', 2), ('b_a2', 2), ('b_g2', 2)]
VECTORS = ['b_mu', 'b_w0', 'b_a0', 'b_k_k', 'b_k_a', 'b_ln_g', 'b_ln_b']
REPLICATED = ['a_ln_g', 'a_ln_b', 'a_w_s', 'a_b_s', 'b_r_k', 'final_g']
ROW_ALIGN = 16


def _pad_rows(a, mult):
    pad = (-a.shape[-2]) % mult
    return jnp.pad(a, [(0, 0)] * (a.ndim - 2) + [(0, pad), (0, 0)]) if pad else a


def _as2d(a):
    if a.ndim == 1:
        return a.reshape(1, -1)
    lead = 1
    for s in a.shape[:-1]:
        lead *= s
    return a.reshape(lead, a.shape[-1])


def kernel(x, c, ada_w, ada_b, mlp_w1, mlp_w2, a_w_in, a_ln_g, a_ln_b, a_w_s, a_b_s, a_w_out, b_mu, b_w_in, b_w0, b_w1, b_w2, b_a0, b_a1, b_a2, b_g1, b_g2, b_k_k, b_k_a, b_r_k, b_ln_g, b_ln_b, b_w_out, final_g, loss_target, m_ada_w, m_ada_b, m_mlp_w1, m_mlp_w2, m_a_w_in, m_a_ln_g, m_a_ln_b, m_a_w_s, m_a_b_s, m_a_w_out, m_b_mu, m_b_w_in, m_b_w0, m_b_w1, m_b_w2, m_b_a0, m_b_a1, m_b_a2, m_b_g1, m_b_g2, m_b_k_k, m_b_k_a, m_b_r_k, m_b_ln_g, m_b_ln_b, m_b_w_out, m_final_g, v_ada_w, v_ada_b, v_mlp_w1, v_mlp_w2, v_a_w_in, v_a_ln_g, v_a_ln_b, v_a_w_s, v_a_b_s, v_a_w_out, v_b_mu, v_b_w_in, v_b_w0, v_b_w1, v_b_w2, v_b_a0, v_b_a1, v_b_a2, v_b_g1, v_b_g2, v_b_k_k, v_b_k_a, v_b_r_k, v_b_ln_g, v_b_ln_b, v_b_w_out, v_final_g):
    given = dict(locals())
    w = {n: given[n] for n in WEIGHTS}
    bl, t, d = x.shape
    nl = ada_w.shape[0]
    nb = N_DEV * bl
    m_tok = bl * t
    me = 4 * lax.axis_index("x") + 2 * lax.axis_index("y") + lax.axis_index("c")

    c_all = _exchange(c, "gather_c", False).reshape(nb, d)
    cols = ada_w.shape[2]
    ada_b_cols = lax.dynamic_slice(ada_b, (0, me * cols), (nl, cols)).reshape(nl, 1, cols)
    mod_cols = _ada_fwd(c_all, ada_w, ada_b_cols)
    mod_full = jnp.moveaxis(_exchange(mod_cols, "gather_mod", False), 0, 2).reshape(nl, nb, 6 * d)
    mod_mine = lax.dynamic_slice(mod_full, (0, me * bl, 0), (nl, bl, 6 * d)).reshape(nl, bl, 6, d)
    mod_mix = jnp.pad(mod_mine[:, :, 0:3], ((0, 0), (0, 0), (0, 5), (0, 0)))
    mod_mlp = jnp.pad(mod_mine[:, :, 3:6], ((0, 0), (0, 0), (0, 5), (0, 0)))

    def unshard(g, ax):
        g = jnp.moveaxis(g, 0, ax)
        return g.reshape(g.shape[:ax] + (g.shape[ax] * g.shape[ax + 1],) + g.shape[ax + 2:])

    bf = lambda a: a.astype(BF16)
    lora_names = ['b_w1', 'b_a1', 'b_g1', 'b_w2', 'b_a2', 'b_g2']
    lora_pack = jnp.concatenate([bf(w[n]).reshape(-1) for n in lora_names]).reshape(-1, 128)
    g_a_in, g_a_out = _gather_call((bf(a_w_in[0]), bf(a_w_out[0])), "gather_sgu_weights")
    full = {'a_w_in': unshard(g_a_in, 1), 'a_w_out': unshard(g_a_out, 0)}
    vec_loc = _pad_rows(jnp.concatenate([_as2d(w[n]) for n in VECTORS], axis=0), ROW_ALIGN)
    n_vec_rows = sum(_as2d(w[n]).shape[0] for n in VECTORS)
    vec = jnp.moveaxis(_exchange(vec_loc, "gather_vectors", False), 0, 1).reshape(N_VEC, d)
    vec = vec.at[n_vec_rows].set(b_r_k.reshape(d))

    e_ind, et_ind = _head_indicators(d)
    gd = d // SGU_GROUPS
    group_ind = (jnp.arange(SGU_GROUPS)[:, None] == jnp.arange(d)[None, :] // gd).astype(BF16)
    bias_full = jnp.repeat(a_b_s[0].T, gd, axis=1)
    pad_c = lambda a, n: jnp.pad(a, ((0, 0), (0, n - a.shape[1])))
    pad_r = lambda a, n: jnp.pad(a, ((0, n - a.shape[0]), (0, 0)))
    sgu_args = (full['a_w_in'], a_ln_g, a_ln_b, a_w_s[0], bias_full, full['a_w_out'])

    x0 = x
    (x1, mix_a), (g_w1_0, g_w2_0) = _sgu_fwd(x0, mod_mix[0], *sgu_args, gather=(bf(mlp_w1[0]), bf(mlp_w2[0])))
    w1_full = [unshard(g_w1_0, 1), None]
    w2_full = [unshard(g_w2_0, 0), None]
    (x2, ff0), (g_w1_1, g_w2_1, g_b_in, g_b_out, g_lora) = _mlp_fwd(
        x1, mod_mlp[0], w1_full[0], w2_full[0],
        gather=(bf(mlp_w1[1]), bf(mlp_w2[1]), bf(b_w_in[0]), bf(b_w_out[0]), lora_pack))
    w1_full[1], w2_full[1] = unshard(g_w1_1, 1), unshard(g_w2_1, 0)
    full['b_w_in'], full['b_w_out'] = unshard(g_b_in, 1), unshard(g_b_out, 0)
    lora_flat, lo = g_lora.reshape(N_DEV, -1), 0
    for n, ax in zip(lora_names, (0, 0, 0, 1, 1, 1)):
        loc = w[n].shape[1:]
        full[n] = unshard(lora_flat[:, lo:lo + w[n].size].reshape((N_DEV,) + loc), ax)
        lo += w[n].size
    rwkv_w = (full['b_w_in'], pad_c(full['b_w1'], LORA_PAD), pad_c(full['b_a1'], LORA_PAD),
              pad_c(full['b_g1'], GATE_PAD), pad_r(full['b_w2'], LORA_PAD), pad_r(full['b_a2'], LORA_PAD),
              pad_r(full['b_g2'], GATE_PAD))
    r, ld, k2, v, a_s, b_s, gate = _rwkv_pre_fwd(x2, mod_mix[1], vec, e_ind, et_ind, rwkv_w)
    y, s0 = _wkv_fwd(r, ld, k2, v, a_s, b_s)
    x3, mix_b = _rwkv_post_fwd(x2, y, r, k2, v, gate, mod_mix[1], vec, e_ind, et_ind, full['b_w_out'])
    (x4, ff1), _ = _mlp_fwd(x3, mod_mlp[1], w1_full[1], w2_full[1])
    dx4, loss_blk, dfinal = _final(x4, loss_target, final_g.reshape(1, d))
    loss = lax.psum(loss_blk[0, 0], ("x", "y", "c"))

    tok = lambda a: a.reshape(m_tok, a.shape[-1])
    shard_rows = lambda g: g.reshape((N_DEV, g.shape[0] // N_DEV) + g.shape[1:])
    (dx3, dmod_mlp1, h_b, dff_b, q_b, dp_b), _ = _mlp_bwd(x3, dx4, ff1, mod_mlp[1], w1_full[1], w2_full[1])
    gw1_1 = _matmul_tn(tok(h_b), tok(dp_b), "grad_mlp_w1_l1", col_shards=N_DEV)
    gw2_1 = shard_rows(_matmul_tn(tok(q_b), tok(dff_b), "grad_mlp_w2_l1"))
    (dy, dr_p, dk2_p, dv_p, dgate_act, o_b, dmix_b, dgate_b, small_post), (rw1_1, rw2_1) = _rwkv_post_bwd(
        dx3, mix_b, y, r, k2, v, gate, mod_mix[1], vec, e_ind, et_ind, full['b_w_out'], scatter=(gw1_1, gw2_1))
    g_b_w_out = shard_rows(_matmul_tn(tok(o_b), tok(dmix_b), "grad_b_w_out"))
    dr_s, dld, dk2_s, dv_s, das, dbs = _wkv_bwd(r, ld, k2, v, a_s, b_s, s0, dy)
    (dh, dhp, xr_b, xw_b, xk_b, xv_b, xa_b, xg_b, dr_b, dk_b, dv_b, dtw_b, dta_b, dtg_b, tw2_b, ta_b, sg_b,
     dlw_b, dla_b, dg_b, small_pre) = _rwkv_pre_bwd(x2, mod_mix[1], vec, e_ind, et_ind, rwkv_w,
                                                    dr_p, dk2_p, dv_p, dgate_act, dr_s, dld, dk2_s, dv_s, das, dbs)
    g_b_w_in = jnp.concatenate([_matmul_tn(tok(xr_b), tok(dr_b), "grad_b_w_r"),
                                _matmul_tn(tok(xk_b), tok(dk_b), "grad_b_w_k"),
                                _matmul_tn(tok(xv_b), tok(dv_b), "grad_b_w_v")], axis=1)
    shard_cols = lambda g: jnp.moveaxis(g.reshape(g.shape[0], N_DEV, g.shape[1] // N_DEV), 1, 0)
    g_b_w_in = shard_cols(g_b_w_in)
    lw_, lg_ = b_w1.shape[2], b_g1.shape[2]
    small_names = ['b_w1', 'b_a1', 'b_g1', 'b_w2', 'b_a2', 'b_g2'] + VECTORS
    small_parts = [
        shard_rows(_matmul_tn(tok(xw_b), tok(dtw_b), "grad_b_w1")[:, :lw_]),
        shard_rows(_matmul_tn(tok(xa_b), tok(dta_b), "grad_b_a1")[:, :lw_]),
        shard_rows(_matmul_tn(tok(xg_b), tok(dtg_b), "grad_b_g1")[:, :lg_]),
        shard_cols(_matmul_tn(tok(tw2_b), tok(dlw_b), "grad_b_w2")[:lw_]),
        shard_cols(_matmul_tn(tok(ta_b), tok(dla_b), "grad_b_a2")[:lw_]),
        shard_cols(_matmul_tn(tok(sg_b), tok(dg_b), "grad_b_g2")[:lg_]),
        shard_cols(jnp.concatenate([small_pre[0:10], small_post[0:2]], axis=0).astype(BF16)),
    ]
    small_flat = jnp.concatenate([p.reshape(N_DEV, -1) for p in small_parts], axis=1)
    lane = 128
    small_rows = -(-small_flat.shape[1] // (lane * ROW_ALIGN)) * ROW_ALIGN
    small_pack = jnp.pad(small_flat, ((0, 0), (0, small_rows * lane - small_flat.shape[1]))).reshape(
        N_DEV, small_rows, lane)
    dx2, dmod_mix1 = _norm_bwd(x2, dx3, dh, dhp, mod_mix[1], dgate_b)
    (dx1, dmod_mlp0, h_b, dff_b, q_b, dp_b), (r_b_w_in, r_b_w_out, r_small) = _mlp_bwd(
        x1, dx2, ff0, mod_mlp[0], w1_full[0], w2_full[0], scatter=(g_b_w_in, g_b_w_out, small_pack))
    gw1_0 = _matmul_tn(tok(h_b), tok(dp_b), "grad_mlp_w1_l0", col_shards=N_DEV)
    gw2_0 = shard_rows(_matmul_tn(tok(q_b), tok(dff_b), "grad_mlp_w2_l0"))
    (dx0, dmod_mix0, h_b, dpre_b, z_b, dmix_b, small_sgu, d_ws, d_bs), (rw1_0, rw2_0) = _sgu_bwd(
        x0, dx1, mix_a, mod_mix[0], *sgu_args, group_ind, scatter=(gw1_0, gw2_0))
    g_a_w_in = _matmul_tn(tok(h_b), tok(dpre_b), "grad_a_w_in", col_shards=N_DEV)
    g_a_w_out = shard_rows(_matmul_tn(tok(z_b), tok(dmix_b), "grad_a_w_out"))
    r_a_w_in, r_a_w_out = _scatter_call((g_a_w_in, g_a_w_out), "scatter_sgu_grads")

    dmod_mine = jnp.stack([jnp.concatenate([dmod_mix0[:, 0:3], dmod_mlp0[:, 0:3]], axis=1),
                           jnp.concatenate([dmod_mix1[:, 0:3], dmod_mlp1[:, 0:3]], axis=1)], axis=1)
    dmod_all = _exchange(dmod_mine.reshape(bl, nl * 6 * d), "gather_dmod", False)
    dmod_all = jnp.moveaxis(dmod_all.reshape(nb, nl, 6 * d), 0, 1)
    dmod_cols = lax.dynamic_slice(dmod_all, (0, 0, me * cols), (nl, nb, cols))
    g_ada_w, g_ada_b = _ada_bwd(c_all, dmod_cols, dmod_all)

    rep_g = {'a_ln_g': small_sgu[0:1], 'a_ln_b': small_sgu[1:2], 'a_w_s': d_ws.reshape(-1, d), 'a_b_s': d_bs.reshape(1, d),
             'b_r_k': small_post[2:3], 'final_g': dfinal[0:1]}
    rep_rows = [rep_g[n].shape[0] for n in REPLICATED]
    rep_pack = _pad_rows(jnp.concatenate([rep_g[n] for n in REPLICATED], axis=0), 8)
    rep_all = _exchange(rep_pack, "gather_replicated_grads", False, relay=True)

    mom = {n: given['m_' + n] for n in WEIGHTS}
    var = {n: given['v_' + n] for n in WEIGHTS}
    out = {}
    as3d = lambda a: a.reshape((-1,) + a.shape[-2:])
    for n, parts in (('mlp_w1', [rw1_0, rw1_1]), ('mlp_w2', [rw2_0, rw2_1]), ('a_w_in', [r_a_w_in]),
                     ('a_w_out', [r_a_w_out]), ('b_w_in', [r_b_w_in]), ('b_w_out', [r_b_w_out]),
                     ('ada_w', list(g_ada_w))):
        res = _adamw_layers(as3d(w[n]), as3d(mom[n]), as3d(var[n]), parts, "adamw_" + n)
        out[n] = tuple(a.reshape(w[n].shape) for a in res)

    items, names = [], []

    def add(n, part):
        s2 = _as2d(w[n]).shape
        items.append((_as2d(w[n]), _as2d(mom[n]), _as2d(var[n]), part.reshape((part.shape[0],) + s2)))
        names.append(n)

    sflat = r_small.reshape(N_DEV, -1)
    so = 0
    for n in small_names:
        sz = w[n].size
        add(n, sflat[:, so:so + sz])
        so += sz
    ro = 0
    for n, nr in zip(REPLICATED, rep_rows):
        add(n, rep_all[:, ro:ro + nr])
        ro += nr
    add('ada_b', g_ada_b[None])
    for n, res in zip(names, _adamw_small(items, "adamw_small")):
        out[n] = tuple(a.reshape(w[n].shape) for a in res)

    return (loss, dx0, *[out[n][0] for n in WEIGHTS], *[out[n][1] for n in WEIGHTS],
            *[out[n][2] for n in WEIGHTS], *[out[n][3] for n in WEIGHTS])
```

```python
import functools

import jax
import jax.numpy as jnp
from jax import lax
from jax.experimental import pallas as pl
from jax.experimental.pallas import tpu as pltpu

F32 = jnp.float32
BF16 = jnp.bfloat16

N_DEV = 8
RMS_EPS = 1e-6
LN_EPS = 1e-5
HEAD = 64
GN_EPS = HEAD * 1e-5
L2_EPS = 1e-12
SGU_CHUNK = 128
SGU_GROUPS = 8
WKV_CHUNK = 64
WKV_HEADS_PER_STEP = 16
WKV_EXAMPLES_PER_STEP = 2
LORA_PAD = 128
GATE_PAD = 256
ADAM_LR, ADAM_B1, ADAM_B2, ADAM_EPS, ADAM_WD, ADAM_STEP = 0.001, 0.9, 0.999, 1e-08, 0.01, 10
VMEM_LIMIT = 56 * 1024 * 1024


def _cparams(sem=None, **kw):
    if sem is not None:
        kw["dimension_semantics"] = sem
    return pltpu.CompilerParams(vmem_limit_bytes=VMEM_LIMIT, **kw)


def _dot(a, b):
    return jnp.dot(a.astype(BF16), b.astype(BF16), preferred_element_type=F32)


def _dot_nt(a, b):
    return lax.dot_general(a.astype(BF16), b.astype(BF16), (((1,), (1,)), ((), ())), preferred_element_type=F32)


def _dot_tn(a, b):
    return lax.dot_general(a.astype(BF16), b.astype(BF16), (((0,), (0,)), ((), ())), preferred_element_type=F32)


def _bdot(a, b, dims):
    return lax.dot_general(a.astype(BF16), b.astype(BF16), (dims, ((0,), (0,))), preferred_element_type=F32)


@jax.custom_vjp
def _tri_sum(tri, tri_t, x):
    hi = x.astype(BF16)
    lo = (x - hi.astype(F32)).astype(BF16)
    dn = (((2,), (1,)), ((0,), (0,)))
    return (lax.dot_general(tri, hi, dn, preferred_element_type=F32)
            + lax.dot_general(tri, lo, dn, preferred_element_type=F32))


_tri_sum.defvjp(lambda tri, tri_t, x: (_tri_sum(tri, tri_t, x), (tri, tri_t)),
                lambda res, g: (jnp.zeros_like(res[0]), jnp.zeros_like(res[1]), _tri_sum(res[1], res[0], g)))


@jax.custom_vjp
def _bmm_nn(a, b):
    return _bdot(a, b, ((2,), (1,)))


@jax.custom_vjp
def _bmm_nt(a, b):
    return _bdot(a, b, ((2,), (2,)))


@jax.custom_vjp
def _bmm_tn(a, b):
    return _bdot(a, b, ((1,), (1,)))


_bmm_nn.defvjp(lambda a, b: (_bmm_nn(a, b), (a, b)), lambda res, g: (_bmm_nt(g, res[1]), _bmm_tn(res[0], g)))
_bmm_nt.defvjp(lambda a, b: (_bmm_nt(a, b), (a, b)), lambda res, g: (_bmm_nn(g, res[1]), _bmm_tn(g, res[0])))
_bmm_tn.defvjp(lambda a, b: (_bmm_tn(a, b), (a, b)), lambda res, g: (_bmm_nt(res[1], g), _bmm_nn(res[0], g)))


def _tri_inverse(p):
    n = p.shape[1]
    row = lax.broadcasted_iota(jnp.int32, (n, n), 0)
    col = lax.broadcasted_iota(jnp.int32, (n, n), 1)
    tinv = jnp.where(row == col, 1.0, 0.0).astype(F32)[None] + p
    for _ in range(max(1, (n - 1).bit_length()) - 1):
        p = _bmm_nn(p, p)
        tinv = tinv + _bmm_nn(tinv, p)
    return tinv.astype(BF16)


def _tri_solve_fwd(tinv, p, rhs):
    u = _bmm_nn(tinv, rhs)
    return u, (tinv, u)


def _tri_solve_bwd(res, du):
    tinv, u = res
    drhs = _bmm_tn(tinv, du)
    return jnp.zeros_like(tinv), _bmm_nt(drhs, u), drhs


@jax.custom_vjp
def _tri_solve(tinv, p, rhs):
    return _tri_solve_fwd(tinv, p, rhs)[0]


_tri_solve.defvjp(_tri_solve_fwd, _tri_solve_bwd)


def _wkv_chunk(s0, r, ld, k, v, a, b, tinv=None):
    nh, n, _ = r.shape
    row = lax.broadcasted_iota(jnp.int32, (n, n), 0)
    col = lax.broadcasted_iota(jnp.int32, (n, n), 1)
    incl = row >= col
    strict = row > col
    lower = jnp.broadcast_to(jnp.where(incl, 1.0, 0.0).astype(BF16), (nh, n, n))
    upper = jnp.broadcast_to(jnp.where(row <= col, 1.0, 0.0).astype(BF16), (nh, n, n))
    c = _tri_sum(lower, upper, ld)
    c_end = c[:, n - 1:n, :]
    ec, enc, ecx, eend = jnp.exp(c), jnp.exp(-c), jnp.exp(c - ld), jnp.exp(c_end - c)
    ar = jnp.concatenate([a * ecx, r * ec], axis=1)
    mask = jnp.concatenate([strict, incl], axis=0)[None]
    m_b = jnp.where(mask, _bmm_nt(ar, b * enc), 0.0)
    m_k = jnp.where(mask, _bmm_nt(ar, k * enc), 0.0)
    a_ab, a_rb = m_b[:, :n], m_b[:, n:]
    base = _bmm_nt(ar, s0) + _bmm_nn(m_k, v)
    if tinv is None:
        tinv = lax.stop_gradient(_tri_inverse(a_ab))
    u = _tri_solve(tinv, a_ab, base[:, :n])
    y = base[:, n:] + _bmm_nn(a_rb, u)
    s1 = s0 * jnp.exp(c_end) + _bmm_tn(jnp.concatenate([u, v], axis=1), jnp.concatenate([b * eend, k * eend], axis=1))
    return y, s1, tinv


def _wkv_specs(bl, nh, t):
    eb, hb, lc = min(bl, WKV_EXAMPLES_PER_STEP), min(nh, WKV_HEADS_PER_STEP), WKV_CHUNK
    return eb, hb, lc, (bl // eb, nh // hb, t // lc)


def _wkv_fwd(r, ld, k, v, a, b):
    bl, nh, t, n = r.shape
    eb, hb, lc, grid = _wkv_specs(bl, nh, t)
    nc = t // lc
    nb = eb * hb

    def body(r_ref, ld_ref, k_ref, v_ref, a_ref, b_ref, y_ref, s0_ref, tinv_ref, s_scr):
        @pl.when(pl.program_id(2) == 0)
        def _():
            s_scr[...] = jnp.zeros_like(s_scr)

        s0 = s_scr[...]
        s0_ref[:, :, 0] = s0.reshape(eb, hb, n, n)
        y, s1, tinv = _wkv_chunk(
            s0, *(ref[...].reshape(nb, lc, n) for ref in (r_ref, ld_ref, k_ref, v_ref, a_ref, b_ref)))
        y_ref[...] = y.reshape(eb, hb, lc, n)
        tinv_ref[:, :, 0] = tinv.reshape(eb, hb, lc, lc)
        s_scr[...] = s1

    seq = pl.BlockSpec((eb, hb, lc, n), lambda e, h, c: (e, h, c, 0))
    return pl.pallas_call(
        body, name="wkv_fwd", grid=grid,
        in_specs=[seq] * 6,
        out_specs=(seq, pl.BlockSpec((eb, hb, 1, n, n), lambda e, h, c: (e, h, c, 0, 0)),
                   pl.BlockSpec((eb, hb, 1, lc, lc), lambda e, h, c: (e, h, c, 0, 0))),
        out_shape=(jax.ShapeDtypeStruct((bl, nh, t, n), F32), jax.ShapeDtypeStruct((bl, nh, nc, n, n), F32),
                   jax.ShapeDtypeStruct((bl, nh, nc, lc, lc), BF16)),
        scratch_shapes=[pltpu.VMEM((nb, n, n), F32)],
        compiler_params=_cparams(("arbitrary", "arbitrary", "arbitrary")),
    )(r, ld, k, v, a, b)


def _wkv_bwd(r, ld, k, v, a, b, s0_all, tinv_all, dy):
    bl, nh, t, n = r.shape
    eb, hb, lc, grid = _wkv_specs(bl, nh, t)
    nc = t // lc
    nb = eb * hb

    def body(r_ref, ld_ref, k_ref, v_ref, a_ref, b_ref, s0_ref, tinv_ref, dy_ref,
             dr_ref, dld_ref, dk_ref, dv_ref, da_ref, db_ref, ds_scr):
        @pl.when(pl.program_id(2) == 0)
        def _():
            ds_scr[...] = jnp.zeros_like(ds_scr)

        args = (s0_ref[:, :, 0].reshape(nb, n, n),) + tuple(
            ref[...].reshape(nb, lc, n) for ref in (r_ref, ld_ref, k_ref, v_ref, a_ref, b_ref))
        tinv = tinv_ref[:, :, 0].reshape(nb, lc, lc)
        _, vjp = jax.vjp(lambda *xs: _wkv_chunk(*xs, tinv=tinv)[:2], *args)
        ds0, *dseq = vjp((dy_ref[...].reshape(nb, lc, n), ds_scr[...]))
        ds_scr[...] = ds0
        for ref, val in zip((dr_ref, dld_ref, dk_ref, dv_ref, da_ref, db_ref), dseq):
            ref[...] = val.reshape(eb, hb, lc, n)

    seq = pl.BlockSpec((eb, hb, lc, n), lambda e, h, c: (e, h, nc - 1 - c, 0))
    st = pl.BlockSpec((eb, hb, 1, n, n), lambda e, h, c: (e, h, nc - 1 - c, 0, 0))
    ti = pl.BlockSpec((eb, hb, 1, lc, lc), lambda e, h, c: (e, h, nc - 1 - c, 0, 0))
    out = jax.ShapeDtypeStruct((bl, nh, t, n), F32)
    return pl.pallas_call(
        body, name="wkv_bwd", grid=grid,
        in_specs=[seq] * 6 + [st, ti, seq],
        out_specs=(seq,) * 6, out_shape=(out,) * 6,
        scratch_shapes=[pltpu.VMEM((nb, n, n), F32)],
        compiler_params=_cparams(("arbitrary", "arbitrary", "arbitrary")),
    )(r, ld, k, v, a, b, s0_all, tinv_all, dy)


def _exchange(x, name, scatter, relay=False):
    assert not (relay and scatter)
    blk = x.shape[1:] if scatter else x.shape

    def body(x_ref, o_ref, send_sems, recv_sems, local_sem):
        pos = (lax.axis_index("x"), lax.axis_index("y"), lax.axis_index("c"))
        me = 4 * pos[0] + 2 * pos[1] + pos[2]

        def peer_of(m):
            p = tuple(1 - pos[i] if (m >> (2 - i)) & 1 else pos[i] for i in range(3))
            return p, 4 * p[0] + 2 * p[1] + p[2]

        def copy(m):
            p, pidx = peer_of(m)
            return pltpu.make_async_remote_copy(
                src_ref=x_ref.at[pidx] if scatter else x_ref, dst_ref=o_ref.at[me],
                send_sem=send_sems.at[m - 1], recv_sem=recv_sems.at[m - 1],
                device_id=p, device_id_type=pl.DeviceIdType.MESH)

        def arrival(m):
            p, pidx = peer_of(m)
            return pltpu.make_async_remote_copy(
                src_ref=x_ref.at[pidx] if scatter else x_ref, dst_ref=o_ref.at[pidx],
                send_sem=send_sems.at[m - 1], recv_sem=recv_sems.at[m - 1],
                device_id=p, device_id_type=pl.DeviceIdType.MESH)

        mine = pltpu.make_async_copy(x_ref.at[me] if scatter else x_ref, o_ref.at[me], local_sem)
        mine.start()
        if not relay:
            sends = [copy(m) for m in range(1, N_DEV)]
            for cp in sends:
                cp.start()
            for m in range(1, N_DEV):
                arrival(m).wait_recv()
        else:
            sibling, _ = peer_of(1)
            far = (2, 4, 6)

            def relay_copy(m, origin_idx):
                return pltpu.make_async_remote_copy(
                    src_ref=o_ref.at[origin_idx], dst_ref=o_ref.at[origin_idx],
                    send_sem=send_sems.at[m], recv_sem=recv_sems.at[m],
                    device_id=sibling, device_id_type=pl.DeviceIdType.MESH)

            sends = [copy(1)] + [copy(m) for m in far]
            for cp in sends:
                cp.start()
            for m in far:
                arrival(m).wait_recv()
                fwd = relay_copy(m, peer_of(m)[1])
                fwd.start()
                sends.append(fwd)
            arrival(1).wait_recv()
            for m in far:
                relay_copy(m, peer_of(m ^ 1)[1]).wait_recv()
        for cp in sends:
            cp.wait_send()
        mine.wait()

    return pl.pallas_call(
        body, name=name,
        out_shape=jax.ShapeDtypeStruct((N_DEV,) + tuple(blk), x.dtype),
        in_specs=[pl.BlockSpec(memory_space=pl.ANY)],
        out_specs=pl.BlockSpec(memory_space=pl.ANY),
        scratch_shapes=[pltpu.SemaphoreType.DMA((N_DEV - 1,)), pltpu.SemaphoreType.DMA((N_DEV - 1,)),
                        pltpu.SemaphoreType.DMA],
    )(x)


def _scatter_copies(x_refs, o_refs, send_sems, recv_sems, local_sems):
    pos = (lax.axis_index("x"), lax.axis_index("y"), lax.axis_index("c"))
    me = 4 * pos[0] + 2 * pos[1] + pos[2]

    def descriptors():
        sends, arrivals, local = [], [], []
        for i, (x_ref, o_ref) in enumerate(zip(x_refs, o_refs)):
            for m in range(1, N_DEV):
                p = tuple(1 - pos[a] if (m >> (2 - a)) & 1 else pos[a] for a in range(3))
                pidx = 4 * p[0] + 2 * p[1] + p[2]
                k = (N_DEV - 1) * i + m - 1
                for dst, out in ((o_ref.at[me], sends), (o_ref.at[pidx], arrivals)):
                    out.append(pltpu.make_async_remote_copy(
                        src_ref=x_ref.at[pidx], dst_ref=dst, send_sem=send_sems.at[k], recv_sem=recv_sems.at[k],
                        device_id=p, device_id_type=pl.DeviceIdType.MESH))
            local.append(pltpu.make_async_copy(x_ref.at[me], o_ref.at[me], local_sems.at[i]))
        return sends, arrivals, local

    def start():
        sends, _, local = descriptors()
        for cp in local + sends:
            cp.start()

    def finish():
        sends, arrivals, local = descriptors()
        for cp in arrivals:
            cp.wait_recv()
        for cp in sends:
            cp.wait_send()
        for cp in local:
            cp.wait()

    return start, finish


def _gather_copies(x_refs, o_refs, send_sems, recv_sems, local_sems):
    pos = (lax.axis_index("x"), lax.axis_index("y"), lax.axis_index("c"))
    me = 4 * pos[0] + 2 * pos[1] + pos[2]
    far = (2, 4, 6)

    def peer_of(m):
        p = tuple(1 - pos[a] if (m >> (2 - a)) & 1 else pos[a] for a in range(3))
        return p, 4 * p[0] + 2 * p[1] + p[2]

    sibling, _ = peer_of(1)

    def copy(i, k, src, slot, to):
        return pltpu.make_async_remote_copy(
            src_ref=src, dst_ref=o_refs[i].at[slot], send_sem=send_sems.at[(N_DEV - 1) * i + k],
            recv_sem=recv_sems.at[(N_DEV - 1) * i + k], device_id=to, device_id_type=pl.DeviceIdType.MESH)

    def direct(i):
        return [copy(i, m - 1, x_refs[i], me, peer_of(m)[0]) for m in (1,) + far]

    def local(i):
        return pltpu.make_async_copy(x_refs[i], o_refs[i].at[me], local_sems.at[i])

    def start():
        for i in range(len(x_refs)):
            local(i).start()
            for cp in direct(i):
                cp.start()

    def finish():
        n = len(x_refs)
        relays = []
        for i in range(n):
            for m in far:
                origin = peer_of(m)[1]
                copy(i, m - 1, x_refs[i], origin, sibling).wait_recv()
                fwd = copy(i, m, o_refs[i].at[origin], origin, sibling)
                fwd.start()
                relays.append(fwd)
        for i in range(n):
            copy(i, 0, x_refs[i], peer_of(1)[1], sibling).wait_recv()
            for m in far:
                copy(i, m, x_refs[i], peer_of(m ^ 1)[1], sibling).wait_recv()
        for i in range(n):
            for cp in direct(i):
                cp.wait_send()
            local(i).wait()
        for cp in relays:
            cp.wait_send()

    return start, finish


def _scatter_scratch(n):
    return [pltpu.SemaphoreType.DMA(((N_DEV - 1) * n,)), pltpu.SemaphoreType.DMA(((N_DEV - 1) * n,)),
            pltpu.SemaphoreType.DMA((n,))]


_ANY = pl.BlockSpec(memory_space=pl.ANY)


def _scatter_call(arrays, name):
    n = len(arrays)

    def body(*refs):
        start, finish = _scatter_copies(refs[:n], refs[n:2 * n], *refs[2 * n:])
        start()
        finish()

    return pl.pallas_call(
        body, name=name, in_specs=[_ANY] * n, out_specs=(_ANY,) * n,
        out_shape=tuple(_sds(a.shape, a.dtype) for a in arrays), scratch_shapes=_scatter_scratch(n),
    )(*arrays)


def _gather_call(arrays, name):
    n = len(arrays)

    def body(*refs):
        start, finish = _gather_copies(refs[:n], refs[n:2 * n], *refs[2 * n:])
        start()
        finish()

    return pl.pallas_call(
        body, name=name, in_specs=[_ANY] * n, out_specs=(_ANY,) * n,
        out_shape=tuple(_sds((N_DEV,) + a.shape, a.dtype) for a in arrays), scratch_shapes=_scatter_scratch(n),
    )(*arrays)


def _call_with_scatter(body, *, name, grid, in_specs, out_specs, out_shape, scratch_shapes, operands,
                       scatter=(), gather=()):
    assert not (scatter and gather)
    carried = tuple(scatter) or tuple(gather)
    copies = _scatter_copies if scatter else _gather_copies
    recv_shapes = tuple(_sds(a.shape if scatter else (N_DEV,) + a.shape, a.dtype) for a in carried)
    nc, n_in, n_out, n_scr = len(carried), len(in_specs), len(out_specs), len(scratch_shapes)
    if nc == 0:
        return pl.pallas_call(
            body, name=name, grid=grid, in_specs=list(in_specs), out_specs=tuple(out_specs),
            out_shape=tuple(out_shape), scratch_shapes=list(scratch_shapes),
            compiler_params=_cparams(("arbitrary",) * len(grid)))(*operands), ()

    def wrapped(*refs):
        ins, refs = refs[:n_in], refs[n_in:]
        c_in, refs = refs[:nc], refs[nc:]
        outs, refs = refs[:n_out], refs[n_out:]
        c_out, refs = refs[:nc], refs[nc:]
        scr, sems = refs[:n_scr], refs[n_scr:]
        ids = [pl.program_id(a) for a in range(len(grid))]
        first = functools.reduce(jnp.logical_and, [i == 0 for i in ids])
        last = functools.reduce(jnp.logical_and, [i == g - 1 for i, g in zip(ids, grid)])
        start, finish = copies(c_in, c_out, *sems)
        pl.when(first)(start)
        body(*ins, *outs, *scr)
        pl.when(last)(finish)

    res = pl.pallas_call(
        wrapped, name=name, grid=grid,
        in_specs=list(in_specs) + [_ANY] * nc, out_specs=tuple(out_specs) + (_ANY,) * nc,
        out_shape=tuple(out_shape) + recv_shapes,
        scratch_shapes=list(scratch_shapes) + _scatter_scratch(nc),
        compiler_params=_cparams(("arbitrary",) * len(grid)),
    )(*operands, *carried)
    return res[:n_out], res[n_out:]


def _rms(x):
    inv = lax.rsqrt(jnp.mean(x * x, axis=-1, keepdims=True) + RMS_EPS)
    return x * inv, inv


def _rms_bwd(xn, inv, dxn):
    return inv * (dxn - xn * jnp.mean(dxn * xn, axis=-1, keepdims=True))


def _colsum(x):
    return jnp.sum(x, axis=0, keepdims=True)


def _sigmoid(x):
    return 0.5 * (jnp.tanh(0.5 * x) + 1.0)


def _split_bf16(x):
    hi = x.astype(BF16)
    return hi, (x - hi.astype(F32)).astype(BF16)


def _dot_split(x, e):
    hi, lo = _split_bf16(x)
    return jnp.dot(hi, e, preferred_element_type=F32) + jnp.dot(lo, e, preferred_element_type=F32)


@jax.custom_vjp
def _headsum(x, e, et):
    return _dot_split(_dot_split(x, e), et)


_headsum.defvjp(lambda x, e, et: (_headsum(x, e, et), (e, et)),
                lambda res, g: (_headsum(g, *res), jnp.zeros_like(res[0]), jnp.zeros_like(res[1])))


def _make_headsum(e, et):
    return lambda x: _headsum(x, e, et)


def _head_indicators(d):
    e = (jnp.arange(d)[:, None] // HEAD == jnp.arange(128)[None, :]).astype(BF16)
    return e, e.T


def _rwkv_elem(r, k, lw, la, w0, a0, k_k, k_a, headsum):
    z = w0 + lw
    w_log = -(jnp.maximum(-z, 0.0) + jnp.log(1.0 + jnp.exp(-jnp.abs(z)))) - 0.5
    ld = -jnp.exp(w_log)
    a = _sigmoid(a0 + la)
    kkp = k * k_k
    kk = kkp / jnp.maximum(jnp.sqrt(headsum(kkp * kkp)), L2_EPS)
    k2 = k * (1.0 + (a - 1.0) * k_a)
    del r
    return ld, k2, -kk, kk * a


def _rwkv_post(y, r, k2, v, g, ln_g, ln_b, r_k, headsum):
    m = headsum(y) * (1.0 / HEAD)
    yc = y - m
    var = headsum(yc * yc) * (1.0 / HEAD)
    yn = yc * lax.rsqrt(var + GN_EPS)
    bonus = headsum(r * k2 * r_k) * v
    return (yn * ln_g + ln_b + bonus) * g


def _shift_down(h, first_row):
    rolled = pltpu.roll(h, 1, 0)
    row = lax.broadcasted_iota(jnp.int32, h.shape, 0)
    return jnp.where(row == 0, first_row, rolled)


def _shift_up(h, last_row):
    n = h.shape[0]
    rolled = pltpu.roll(h, n - 1, 0)
    row = lax.broadcasted_iota(jnp.int32, h.shape, 0)
    return jnp.where(row == n - 1, last_row, rolled)


def _gelu(p):
    return 0.5 * p * (1.0 + lax.erf(p * 0.7071067811865476))


def _gelu_grad(p):
    return 0.5 * (1.0 + lax.erf(p * 0.7071067811865476)) + p * jnp.exp(-0.5 * p * p) * 0.3989422804014327


def _tok(tm, d):
    return pl.BlockSpec((1, tm, d), lambda e, t, *_: (e, t, 0))


def _per_example(rows, d):
    return pl.BlockSpec((1, rows, d), lambda e, t, *_: (e, 0, 0))


def _whole(shape):
    nd = len(shape)
    return pl.BlockSpec(tuple(shape), lambda *_: (0,) * nd)


def _heads(nh, tm):
    return pl.BlockSpec((1, nh, tm, HEAD), lambda e, t, *_: (e, 0, t, 0))


def _sds(shape, dtype=F32):
    return jax.ShapeDtypeStruct(tuple(shape), dtype)


def _add_rows(ref, first, rows):
    @pl.when(first)
    def _():
        ref[0] = jnp.zeros(ref.shape[1:], ref.dtype)

    for i, r in enumerate(rows):
        ref[0, i:i + 1] += r


def _first(e, t):
    return jnp.logical_and(e == 0, t == 0)


def _ada_fwd(c_all, ada_w, ada_b_cols):
    nl, d, cols = ada_w.shape
    nb = c_all.shape[0]

    def body(c_ref, w_ref, b_ref, o_ref):
        c = c_ref[...]
        cond = c * _sigmoid(c)
        for i in range(nl):
            o_ref[i] = _dot(cond, w_ref[i]) + b_ref[i]

    return pl.pallas_call(
        body, name="ada_fwd", out_shape=_sds((nl, nb, cols)),
        compiler_params=_cparams(),
    )(c_all, ada_w, ada_b_cols)


def _ada_bwd(c_all, dmod_cols, dmod_full):
    nl, nb, cols = dmod_cols.shape
    d = c_all.shape[1]

    def body(c_ref, g_ref, f_ref, b_ref, *o_refs):
        c = c_ref[...]
        cond = c * _sigmoid(c)
        for i in range(nl):
            o_refs[i][0] = _dot_tn(cond, g_ref[i])
            b_ref[i:i + 1] = jnp.sum(f_ref[i], axis=0, keepdims=True)

    res = pl.pallas_call(
        body, name="ada_bwd", out_shape=(_sds((nl, dmod_full.shape[2])),) + (_sds((1, d, cols)),) * nl,
        compiler_params=_cparams(),
    )(c_all, dmod_cols, dmod_full)
    return res[1:], res[0]


def _matmul_tn(a, b, name, col_shards=None):
    m, ka = a.shape
    n = b.shape[1]
    tm = min(m, 2048)
    tk = min(ka, 1024)
    tn = n // col_shards if col_shards else min(n, 512)
    steps = m // tm
    if col_shards:
        out_spec = pl.BlockSpec((None, tk, tn), lambda i, j, s: (j, i, 0))
        out_shape = _sds((col_shards, ka, tn), BF16)
    else:
        out_spec = pl.BlockSpec((tk, tn), lambda i, j, s: (i, j))
        out_shape = _sds((ka, n), BF16)

    def body(a_ref, b_ref, o_ref, acc):
        s = pl.program_id(2)

        @pl.when(s == 0)
        def _():
            acc[...] = jnp.zeros_like(acc)

        acc[...] += _dot_tn(a_ref[...], b_ref[...])

        @pl.when(s == steps - 1)
        def _():
            o_ref[...] = acc[...].astype(BF16)

    return pl.pallas_call(
        body, name=name, grid=(ka // tk, n // tn, steps),
        in_specs=[pl.BlockSpec((tm, tk), lambda i, j, s: (s, i)), pl.BlockSpec((tm, tn), lambda i, j, s: (s, j))],
        out_specs=out_spec, out_shape=out_shape,
        scratch_shapes=[pltpu.VMEM((tk, tn), F32)],
        compiler_params=_cparams(("parallel", "parallel", "arbitrary")),
    )(a, b)


MLP_TM = 512
MLP_FWD_TM = 1024
MLP_FJ = 1024


def _mlp_fwd(x, mod, w1, w2, gather=()):
    bl, t, d = x.shape
    f = w1.shape[1]
    tm, fj = min(t, MLP_FWD_TM), min(f, MLP_FJ)
    nj = f // fj

    def body(x_ref, mod_ref, w1_ref, w2_ref, xo_ref, ff_ref, h_scr, acc):
        j = pl.program_id(2)

        @pl.when(j == 0)
        def _():
            xn, _ = _rms(x_ref[0])
            h_scr[...] = (xn * (1.0 + mod_ref[0, 1:2]) + mod_ref[0, 0:1]).astype(BF16)
            acc[...] = jnp.zeros_like(acc)

        p = jnp.dot(h_scr[...], w1_ref[...], preferred_element_type=F32)
        q = jnp.square(jnp.maximum(p, 0.0))
        acc[...] += _dot(q, w2_ref[...])

        @pl.when(j == nj - 1)
        def _():
            ff_ref[0] = acc[...]
            xo_ref[0] = x_ref[0] + mod_ref[0, 2:3] * acc[...]

    return _call_with_scatter(
        body, name="mlp_fwd", grid=(bl, t // tm, nj),
        in_specs=[_tok(tm, d), _per_example(8, d),
                  pl.BlockSpec((d, fj), lambda e, i, j: (0, j)), pl.BlockSpec((fj, d), lambda e, i, j: (j, 0))],
        out_specs=(_tok(tm, d), _tok(tm, d)),
        out_shape=(_sds(x.shape), _sds(x.shape)),
        scratch_shapes=[pltpu.VMEM((tm, d), BF16), pltpu.VMEM((tm, d), F32)],
        operands=(x, mod, w1, w2), gather=gather)


def _mlp_bwd(x, dxo, ff, mod, w1, w2, scatter=()):
    bl, t, d = x.shape
    f = w1.shape[1]
    tm, fj = min(t, MLP_TM), min(f, MLP_FJ)
    nj = f // fj

    def body(x_ref, dxo_ref, ff_ref, mod_ref, w1_ref, w2_ref,
             dx_ref, dmod_ref, h_ref, dff_ref, q_ref, dp_ref, acc):
        ti, j = pl.program_id(1), pl.program_id(2)

        @pl.when(j == 0)
        def _():
            xn, _ = _rms(x_ref[0])
            h_ref[0] = (xn * (1.0 + mod_ref[0, 1:2]) + mod_ref[0, 0:1]).astype(BF16)
            dff_ref[0] = (mod_ref[0, 2:3] * dxo_ref[0]).astype(BF16)
            acc[...] = jnp.zeros_like(acc)

        p = jnp.dot(h_ref[0], w1_ref[...], preferred_element_type=F32)
        rl = jnp.maximum(p, 0.0)
        q_ref[0] = jnp.square(rl).astype(BF16)
        dp = (_dot_nt(dff_ref[0], w2_ref[...]) * (2.0 * rl)).astype(BF16)
        dp_ref[0] = dp
        acc[...] += _dot_nt(dp, w1_ref[...])

        @pl.when(j == nj - 1)
        def _():
            xn, inv = _rms(x_ref[0])
            dh = acc[...]
            dx_ref[0] = dxo_ref[0] + _rms_bwd(xn, inv, dh * (1.0 + mod_ref[0, 1:2]))
            _add_rows(dmod_ref, ti == 0, [_colsum(dh), _colsum(dh * xn), _colsum(dxo_ref[0] * ff_ref[0])])

    big = lambda: pl.BlockSpec((1, tm, fj), lambda e, i, j: (e, i, j))
    return _call_with_scatter(
        body, name="mlp_bwd", grid=(bl, t // tm, nj),
        in_specs=[_tok(tm, d), _tok(tm, d), _tok(tm, d), _per_example(8, d),
                  pl.BlockSpec((d, fj), lambda e, i, j: (0, j)), pl.BlockSpec((fj, d), lambda e, i, j: (j, 0))],
        out_specs=(_tok(tm, d), _per_example(8, d), _tok(tm, d), _tok(tm, d), big(), big()),
        out_shape=(_sds(x.shape), _sds((bl, 8, d)), _sds(x.shape, BF16), _sds(x.shape, BF16),
                   _sds((bl, t, f), BF16), _sds((bl, t, f), BF16)),
        scratch_shapes=[pltpu.VMEM((tm, d), F32)],
        operands=(x, dxo, ff, mod, w1, w2), scatter=scatter)


SGU_TM = 256


def _sgu_core(x, mod_ref, win_ref, lng, lnb, ws_ref, bias_ref):
    tm, d = x.shape
    xn, inv = _rms(x)
    h = (xn * (1.0 + mod_ref[0, 1:2]) + mod_ref[0, 0:1]).astype(BF16)
    pre = jnp.dot(h, win_ref[...], preferred_element_type=F32)
    uv = _gelu(pre)
    u, v = uv[:, :d], uv[:, d:]
    mu = jnp.mean(v, axis=-1, keepdims=True)
    vc = v - mu
    rstd = lax.rsqrt(jnp.mean(vc * vc, axis=-1, keepdims=True) + LN_EPS)
    vhat = vc * rstd
    vln = vhat * lng + lnb
    gd = d // SGU_GROUPS
    rows = []
    for c in range(tm // SGU_CHUNK):
        cols = []
        for g in range(SGU_GROUPS):
            cols.append(_dot(ws_ref[g], vln[c * SGU_CHUNK:(c + 1) * SGU_CHUNK, g * gd:(g + 1) * gd]))
        rows.append(jnp.concatenate(cols, axis=1) + bias_ref[...])
    sv = jnp.concatenate(rows, axis=0)
    return xn, inv, h, pre, u, vhat, rstd, vln, sv


def _sgu_masked(ws_ref, wm_scr):
    row = lax.broadcasted_iota(jnp.int32, (SGU_CHUNK, SGU_CHUNK), 0)
    col = lax.broadcasted_iota(jnp.int32, (SGU_CHUNK, SGU_CHUNK), 1)
    for g in range(SGU_GROUPS):
        wm_scr[g] = jnp.where(row >= col, ws_ref[g], 0.0).astype(BF16)


def _sgu_fwd(x, mod, w_in, ln_g, ln_b, w_s, bias_full, w_out, gather=()):
    bl, t, d = x.shape
    tm = min(t, SGU_TM)

    def body(x_ref, mod_ref, win_ref, lng_ref, lnb_ref, ws_ref, bias_ref, wout_ref, xo_ref, mix_ref, wm_scr):
        _sgu_masked(ws_ref, wm_scr)
        xt = x_ref[0]
        *_, u, _, _, _, sv = _sgu_core(xt, mod_ref, win_ref, lng_ref[...], lnb_ref[...], wm_scr, bias_ref)
        mix = _dot(u * sv, wout_ref[...])
        mix_ref[0] = mix
        xo_ref[0] = xt + mod_ref[0, 2:3] * mix

    return _call_with_scatter(
        body, name="sgu_fwd", grid=(bl, t // tm),
        in_specs=[_tok(tm, d), _per_example(8, d), _whole(w_in.shape), _whole(ln_g.shape), _whole(ln_b.shape),
                  _whole(w_s.shape), _whole(bias_full.shape), _whole(w_out.shape)],
        out_specs=(_tok(tm, d), _tok(tm, d)),
        out_shape=(_sds(x.shape), _sds(x.shape)),
        scratch_shapes=[pltpu.VMEM(w_s.shape, BF16)],
        operands=(x, mod, w_in, ln_g, ln_b, w_s, bias_full, w_out), gather=gather)


def _sgu_bwd(x, dxo, mix, mod, w_in, ln_g, ln_b, w_s, bias_full, w_out, group_ind, scatter=()):
    bl, t, d = x.shape
    tm = min(t, SGU_TM)
    gd = d // SGU_GROUPS

    def body(x_ref, dxo_ref, mix_ref, mod_ref, win_ref, lng_ref, lnb_ref, ws_ref, bias_ref, wout_ref, ind_ref,
             dx_ref, dmod_ref, h_ref, dpre_ref, z_ref, dmix_ref, small_ref, dws_ref, dbs_ref, wm_scr, dbias_scr):
        e, ti = pl.program_id(0), pl.program_id(1)
        _sgu_masked(ws_ref, wm_scr)
        xt, dxo = x_ref[0], dxo_ref[0]
        lng = lng_ref[...]
        xn, inv, h, pre, u, vhat, rstd, vln, sv = _sgu_core(xt, mod_ref, win_ref, lng, lnb_ref[...], wm_scr, bias_ref)
        h_ref[0] = h
        z_ref[0] = (u * sv).astype(BF16)
        dmix = mod_ref[0, 2:3] * dxo
        dmix_ref[0] = dmix.astype(BF16)
        dz = _dot_nt(dmix, wout_ref[...])
        du, dsv = dz * sv, dz * u

        @pl.when(_first(e, ti))
        def _():
            dws_ref[...] = jnp.zeros_like(dws_ref)
            dbias_scr[...] = jnp.zeros_like(dbias_scr)
            small_ref[...] = jnp.zeros_like(small_ref)

        row = lax.broadcasted_iota(jnp.int32, (SGU_CHUNK, SGU_CHUNK), 0)
        col = lax.broadcasted_iota(jnp.int32, (SGU_CHUNK, SGU_CHUNK), 1)
        rows = []
        for c in range(tm // SGU_CHUNK):
            rs = slice(c * SGU_CHUNK, (c + 1) * SGU_CHUNK)
            dbias_scr[...] += dsv[rs]
            cols = []
            for g in range(SGU_GROUPS):
                cs = slice(g * gd, (g + 1) * gd)
                cols.append(_dot_tn(wm_scr[g], dsv[rs, cs]))
                dws_ref[g] += jnp.where(row >= col, _dot_nt(dsv[rs, cs], vln[rs, cs]), 0.0)
            rows.append(jnp.concatenate(cols, axis=1))
        dvln = jnp.concatenate(rows, axis=0)
        small_ref[0:1] += _colsum(dvln * vhat)
        small_ref[1:2] += _colsum(dvln)
        dvhat = dvln * lng
        dv = rstd * (dvhat - jnp.mean(dvhat, axis=-1, keepdims=True)
                     - vhat * jnp.mean(dvhat * vhat, axis=-1, keepdims=True))
        dpre = (jnp.concatenate([du, dv], axis=1) * _gelu_grad(pre)).astype(BF16)
        dpre_ref[0] = dpre
        dh = _dot_nt(dpre, win_ref[...])
        dx_ref[0] = dxo + _rms_bwd(xn, inv, dh * (1.0 + mod_ref[0, 1:2]))
        _add_rows(dmod_ref, ti == 0, [_colsum(dh), _colsum(dh * xn), _colsum(dxo * mix_ref[0])])

        @pl.when(jnp.logical_and(e == bl - 1, ti == t // tm - 1))
        def _():
            hi, lo = _split_bf16(dbias_scr[...])
            ind = ind_ref[...]
            dbs_ref[...] = (lax.dot_general(ind, hi, (((1,), (1,)), ((), ())), preferred_element_type=F32)
                            + lax.dot_general(ind, lo, (((1,), (1,)), ((), ())), preferred_element_type=F32))

    return _call_with_scatter(
        body, name="sgu_bwd", grid=(bl, t // tm),
        in_specs=[_tok(tm, d), _tok(tm, d), _tok(tm, d), _per_example(8, d), _whole(w_in.shape), _whole(ln_g.shape),
                  _whole(ln_b.shape), _whole(w_s.shape), _whole(bias_full.shape), _whole(w_out.shape),
                  _whole(group_ind.shape)],
        out_specs=(_tok(tm, d), _per_example(8, d), _tok(tm, d), _tok(tm, 2 * d), _tok(tm, d), _tok(tm, d),
                   _whole((8, d)), _whole(w_s.shape), _whole((SGU_GROUPS, SGU_CHUNK))),
        out_shape=(_sds(x.shape), _sds((bl, 8, d)), _sds(x.shape, BF16), _sds((bl, t, 2 * d), BF16),
                   _sds(x.shape, BF16), _sds(x.shape, BF16), _sds((8, d)), _sds(w_s.shape),
                   _sds((SGU_GROUPS, SGU_CHUNK))),
        scratch_shapes=[pltpu.VMEM(w_s.shape, BF16), pltpu.VMEM((SGU_CHUNK, d), F32)],
        operands=(x, dxo, mix, mod, w_in, ln_g, ln_b, w_s, bias_full, w_out, group_ind), scatter=scatter)


RWKV_TM = 256
N_VEC = 16


def _rwkv_pre_core(x_ref, halo_ref, mod_ref, vec_ref, ti):
    xn, inv = _rms(x_ref[0])
    scale1, shift = 1.0 + mod_ref[0, 1:2], mod_ref[0, 0:1]
    h = xn * scale1 + shift
    hn, _ = _rms(halo_ref[0])
    hh = hn * scale1 + shift
    first = jnp.where(ti == 0, 0.0, hh[7:8])
    xx = _shift_down(h, first) - h
    xs = [h + xx * vec_ref[i:i + 1] for i in range(6)]
    return xn, inv, xx, xs


def _rwkv_proj(xs, wrkv_ref, w1_ref, a1_ref, g1_ref, w2_ref, a2_ref, g2_ref):
    d = xs[0].shape[1]
    xr, xw, xk, xv, xa, xg = [z.astype(BF16) for z in xs]
    r = jnp.dot(xr, wrkv_ref[:, 0:d], preferred_element_type=F32)
    k = jnp.dot(xk, wrkv_ref[:, d:2 * d], preferred_element_type=F32)
    v = jnp.dot(xv, wrkv_ref[:, 2 * d:3 * d], preferred_element_type=F32)
    tw2 = jnp.tanh(jnp.dot(xw, w1_ref[...], preferred_element_type=F32))
    ta = jnp.dot(xa, a1_ref[...], preferred_element_type=F32)
    sg = _sigmoid(jnp.dot(xg, g1_ref[...], preferred_element_type=F32))
    lw, la, g = _dot(tw2, w2_ref[...]), _dot(ta, a2_ref[...]), _dot(sg, g2_ref[...])
    return (xr, xw, xk, xv, xa, xg), r, k, v, tw2, ta, sg, lw, la, g


def _to_heads(ref, val, nh):
    for hd in range(nh):
        ref[0, hd] = val[:, hd * HEAD:(hd + 1) * HEAD]


def _from_heads(ref, scr, nh):
    for hd in range(nh):
        scr[:, hd * HEAD:(hd + 1) * HEAD] = ref[0, hd]
    return scr[...]


def _rwkv_weight_specs(ws):
    return [_whole(w.shape) for w in ws]


def _rwkv_pre_fwd(x, mod, vec, e_ind, et_ind, weights):
    bl, t, d = x.shape
    tm = min(t, RWKV_TM)
    nh = d // HEAD
    hb = tm // 8

    def body(x_ref, halo_ref, mod_ref, vec_ref, e_ref, et_ref, wrkv, w1, a1, g1, w2, a2, g2,
             r_ref, ld_ref, k2_ref, v_ref, as_ref, bs_ref, g_ref):
        ti = pl.program_id(1)
        _, _, _, xs = _rwkv_pre_core(x_ref, halo_ref, mod_ref, vec_ref, ti)
        _, r, k, v, _, _, _, lw, la, g = _rwkv_proj(xs, wrkv, w1, a1, g1, w2, a2, g2)
        headsum = _make_headsum(e_ref[...], et_ref[...])
        ld, k2, a_s, b_s = _rwkv_elem(r, k, lw, la, vec_ref[6:7], vec_ref[7:8], vec_ref[8:9], vec_ref[9:10], headsum)
        g_ref[0] = g
        for ref, val in ((r_ref, r), (ld_ref, ld), (k2_ref, k2), (v_ref, v), (as_ref, a_s), (bs_ref, b_s)):
            _to_heads(ref, val, nh)

    halo = pl.BlockSpec((1, 8, d), lambda e, i: (e, jnp.maximum(i * hb - 1, 0), 0))
    hs = _sds((bl, nh, t, HEAD))
    return pl.pallas_call(
        body, name="rwkv_pre_fwd", grid=(bl, t // tm),
        in_specs=[_tok(tm, d), halo, _per_example(8, d), _whole(vec.shape), _whole(e_ind.shape), _whole(et_ind.shape)]
        + _rwkv_weight_specs(weights),
        out_specs=(_heads(nh, tm),) * 6 + (_tok(tm, d),),
        out_shape=(hs,) * 6 + (_sds(x.shape),),
        compiler_params=_cparams(("arbitrary", "arbitrary")),
    )(x, x, mod, vec, e_ind, et_ind, *weights)


def _rwkv_post_fwd(x, y, r, k2, v, g, mod, vec, e_ind, et_ind, w_out):
    bl, t, d = x.shape
    tm = min(t, RWKV_TM)
    nh = d // HEAD

    def body(x_ref, y_ref, r_ref, k2_ref, v_ref, g_ref, mod_ref, vec_ref, e_ref, et_ref, wout_ref,
             xo_ref, mix_ref, s0, s1, s2, s3):
        headsum = _make_headsum(e_ref[...], et_ref[...])
        yv, rv, kv, vv = (_from_heads(ref, scr, nh) for ref, scr in
                          ((y_ref, s0), (r_ref, s1), (k2_ref, s2), (v_ref, s3)))
        o = _rwkv_post(yv, rv, kv, vv, g_ref[0], vec_ref[10:11], vec_ref[11:12], vec_ref[12:13], headsum)
        mix = _dot(o, wout_ref[...])
        mix_ref[0] = mix
        xo_ref[0] = x_ref[0] + mod_ref[0, 2:3] * mix

    return pl.pallas_call(
        body, name="rwkv_post_fwd", grid=(bl, t // tm),
        in_specs=[_tok(tm, d)] + [_heads(nh, tm)] * 4 + [_tok(tm, d), _per_example(8, d), _whole(vec.shape),
                                                         _whole(e_ind.shape), _whole(et_ind.shape), _whole(w_out.shape)],
        out_specs=(_tok(tm, d), _tok(tm, d)),
        out_shape=(_sds(x.shape), _sds(x.shape)),
        scratch_shapes=[pltpu.VMEM((tm, d), F32)] * 4,
        compiler_params=_cparams(("arbitrary", "arbitrary")),
    )(x, y, r, k2, v, g, mod, vec, e_ind, et_ind, w_out)


def _rwkv_post_bwd(dxo, mix, y, r, k2, v, g, mod, vec, e_ind, et_ind, w_out, scatter=()):
    bl, t, d = dxo.shape
    tm = min(t, RWKV_TM)
    nh = d // HEAD

    def body(dxo_ref, mix_ref, y_ref, r_ref, k2_ref, v_ref, g_ref, mod_ref, vec_ref, e_ref, et_ref, wout_ref,
             dy_ref, dr_ref, dk2_ref, dv_ref, dg_ref, o_ref, dmix_ref, dgate_ref, small_ref, s0, s1, s2, s3):
        e, ti = pl.program_id(0), pl.program_id(1)
        headsum = _make_headsum(e_ref[...], et_ref[...])
        yv, rv, kv, vv = (_from_heads(ref, scr, nh) for ref, scr in
                          ((y_ref, s0), (r_ref, s1), (k2_ref, s2), (v_ref, s3)))
        dxo = dxo_ref[0]
        dmix = mod_ref[0, 2:3] * dxo
        dmix_ref[0] = dmix.astype(BF16)
        do = _dot_nt(dmix, wout_ref[...])
        post = functools.partial(_rwkv_post, headsum=headsum)
        o, vjp = jax.vjp(post, yv, rv, kv, vv, g_ref[0], vec_ref[10:11], vec_ref[11:12], vec_ref[12:13])
        o_ref[0] = o.astype(BF16)
        dy, dr, dk2, dv, dg, dlng, dlnb, drk = vjp(do)
        _to_heads(dy_ref, dy, nh)
        dr_ref[0], dk2_ref[0], dv_ref[0], dg_ref[0] = dr, dk2, dv, dg
        zero = jnp.zeros((1, d), F32)
        _add_rows(dgate_ref, ti == 0, [zero, zero, _colsum(dxo * mix_ref[0])])

        @pl.when(_first(e, ti))
        def _():
            small_ref[...] = jnp.zeros_like(small_ref)

        small_ref[0:1] += dlng
        small_ref[1:2] += dlnb
        small_ref[2:3] += drk

    return _call_with_scatter(
        body, name="rwkv_post_bwd", grid=(bl, t // tm),
        in_specs=[_tok(tm, d), _tok(tm, d)] + [_heads(nh, tm)] * 4
        + [_tok(tm, d), _per_example(8, d), _whole(vec.shape), _whole(e_ind.shape), _whole(et_ind.shape),
           _whole(w_out.shape)],
        out_specs=(_heads(nh, tm),) + (_tok(tm, d),) * 6 + (_per_example(8, d), _whole((8, d))),
        out_shape=(_sds((bl, nh, t, HEAD)),) + (_sds(dxo.shape),) * 4 + (_sds(dxo.shape, BF16),) * 2
        + (_sds((bl, 8, d)), _sds((8, d))),
        scratch_shapes=[pltpu.VMEM((tm, d), F32)] * 4,
        operands=(dxo, mix, y, r, k2, v, g, mod, vec, e_ind, et_ind, w_out), scatter=scatter)


RWKV_BWD_TM = 128


def _rwkv_pre_bwd(x, mod, vec, e_ind, et_ind, weights, dr_p, dk2_p, dv_p, dg, dr_s, dld, dk2_s, dv_s, das, dbs):
    bl, t, d = x.shape
    tm = min(t, RWKV_BWD_TM)
    nh = d // HEAD
    hb = tm // 8
    lp, gp = LORA_PAD, GATE_PAD

    def body(x_ref, halo_ref, mod_ref, vec_ref, e_ref, et_ref, wrkv, w1, a1, g1, w2, a2, g2,
             drp_ref, dk2p_ref, dvp_ref, dg_ref, drs_ref, dld_ref, dk2s_ref, dvs_ref, das_ref, dbs_ref,
             dh_ref, dhp_ref, xr_ref, xw_ref, xk_ref, xv_ref, xa_ref, xg_ref, dr_ref, dk_ref, dv_ref,
             dtw_ref, dta_ref, dtg_ref, tw2_ref, ta_ref, sg_ref, dlw_ref, dla_ref, dgb_ref, small_ref,
             s0, s1, s2, s3, s4, s5):
        e, ti = pl.program_id(0), pl.program_id(1)
        _, _, xx, xs = _rwkv_pre_core(x_ref, halo_ref, mod_ref, vec_ref, ti)
        xb, r, k, v, tw2, ta, sg, lw, la, _ = _rwkv_proj(xs, wrkv, w1, a1, g1, w2, a2, g2)
        for ref, val in zip((xr_ref, xw_ref, xk_ref, xv_ref, xa_ref, xg_ref), xb):
            ref[0] = val
        headsum = _make_headsum(e_ref[...], et_ref[...])
        drs, dld, dk2s, dvs, das, dbs_ = (_from_heads(ref, scr, nh) for ref, scr in
                                          ((drs_ref, s0), (dld_ref, s1), (dk2s_ref, s2), (dvs_ref, s3),
                                           (das_ref, s4), (dbs_ref, s5)))
        elem = functools.partial(_rwkv_elem, r, headsum=headsum)
        _, vjp = jax.vjp(elem, k, lw, la, vec_ref[6:7], vec_ref[7:8], vec_ref[8:9], vec_ref[9:10])
        dk, dlw, dla, dw0, da0, dkk, dka = vjp((dld, dk2p_ref[0] + dk2s, das, dbs_))
        dr = drp_ref[0] + drs
        dv = dvp_ref[0] + dvs
        dgv = dg_ref[0]
        dtg = _dot_nt(dgv, g2[...]) * sg * (1.0 - sg)
        dtw = _dot_nt(dlw, w2[...]) * (1.0 - tw2 * tw2)
        dta = _dot_nt(dla, a2[...])
        dr_ref[0], dk_ref[0], dv_ref[0] = dr.astype(BF16), dk.astype(BF16), dv.astype(BF16)
        dtw_ref[0], dta_ref[0], dtg_ref[0] = dtw.astype(BF16), dta.astype(BF16), dtg.astype(BF16)
        tw2_ref[0], ta_ref[0], sg_ref[0] = tw2.astype(BF16), ta.astype(BF16), sg.astype(BF16)
        dlw_ref[0], dla_ref[0], dgb_ref[0] = dlw.astype(BF16), dla.astype(BF16), dgv.astype(BF16)
        dxs = (_dot_nt(dr, wrkv[:, 0:d]), _dot_nt(dtw, w1[...]), _dot_nt(dk, wrkv[:, d:2 * d]),
               _dot_nt(dv, wrkv[:, 2 * d:3 * d]), _dot_nt(dta, a1[...]), _dot_nt(dtg, g1[...]))

        @pl.when(_first(e, ti))
        def _():
            small_ref[...] = jnp.zeros_like(small_ref)

        dh = jnp.zeros((tm, d), F32)
        dhp = jnp.zeros((tm, d), F32)
        for i, dxi in enumerate(dxs):
            mu = vec_ref[i:i + 1]
            dh += dxi * (1.0 - mu)
            dhp += dxi * mu
            small_ref[i:i + 1] += _colsum(dxi * xx)
        dh_ref[0], dhp_ref[0] = dh, dhp
        small_ref[6:7] += dw0
        small_ref[7:8] += da0
        small_ref[8:9] += dkk
        small_ref[9:10] += dka

    halo = pl.BlockSpec((1, 8, d), lambda e, i: (e, jnp.maximum(i * hb - 1, 0), 0))
    tokd, tokl, tokg = _tok(tm, d), _tok(tm, lp), _tok(tm, gp)
    bf = lambda w: _sds((bl, t, w), BF16)
    return pl.pallas_call(
        body, name="rwkv_pre_bwd", grid=(bl, t // tm),
        in_specs=[tokd, halo, _per_example(8, d), _whole(vec.shape), _whole(e_ind.shape), _whole(et_ind.shape)]
        + _rwkv_weight_specs(weights) + [tokd] * 4 + [_heads(nh, tm)] * 6,
        out_specs=(tokd, tokd) + (tokd,) * 6 + (tokd,) * 3 + (tokl, tokl, tokg, tokl, tokl, tokg)
        + (tokd, tokd, tokd, _whole((N_VEC, d))),
        out_shape=(_sds(x.shape), _sds(x.shape)) + (bf(d),) * 9 + (bf(lp), bf(lp), bf(gp), bf(lp), bf(lp), bf(gp))
        + (bf(d), bf(d), bf(d), _sds((N_VEC, d))),
        scratch_shapes=[pltpu.VMEM((tm, d), F32)] * 6,
        compiler_params=_cparams(("arbitrary", "arbitrary")),
    )(x, x, mod, vec, e_ind, et_ind, *weights, dr_p, dk2_p, dv_p, dg, dr_s, dld, dk2_s, dv_s, das, dbs)


def _norm_bwd(x, dxo, dh, dhprev, mod, dgate):
    bl, t, d = x.shape
    tm = min(t, RWKV_TM)
    hb = tm // 8
    last_blk = t // 8 - 1

    def body(x_ref, dxo_ref, dh_ref, dhp_ref, nxt_ref, mod_ref, dgate_ref, dx_ref, dmod_ref):
        ti = pl.program_id(1)
        xn, inv = _rms(x_ref[0])
        last = jnp.where(ti == t // tm - 1, 0.0, nxt_ref[0, 0:1])
        dh = dh_ref[0] + _shift_up(dhp_ref[0], last)
        dx_ref[0] = dxo_ref[0] + _rms_bwd(xn, inv, dh * (1.0 + mod_ref[0, 1:2]))

        @pl.when(ti == 0)
        def _():
            dmod_ref[0] = dgate_ref[0]

        dmod_ref[0, 0:1] += _colsum(dh)
        dmod_ref[0, 1:2] += _colsum(dh * xn)

    nxt = pl.BlockSpec((1, 8, d), lambda e, i: (e, jnp.minimum((i + 1) * hb, last_blk), 0))
    return pl.pallas_call(
        body, name="norm_bwd", grid=(bl, t // tm),
        in_specs=[_tok(tm, d)] * 4 + [nxt, _per_example(8, d), _per_example(8, d)],
        out_specs=(_tok(tm, d), _per_example(8, d)),
        out_shape=(_sds(x.shape), _sds((bl, 8, d))),
        compiler_params=_cparams(("arbitrary", "arbitrary")),
    )(x, dxo, dh, dhprev, dhprev, mod, dgate)


def _final(x, target, final_g):
    bl, t, d = x.shape
    tm = min(t, 512)

    def body(x_ref, tgt_ref, g_ref, dx_ref, loss_ref, dg_ref):
        e, ti = pl.program_id(0), pl.program_id(1)

        @pl.when(_first(e, ti))
        def _():
            loss_ref[...] = jnp.zeros_like(loss_ref)
            dg_ref[...] = jnp.zeros_like(dg_ref)

        xn, inv = _rms(x_ref[0])
        err = xn * g_ref[...] - tgt_ref[0]
        loss_ref[...] += (0.5 / d) * jnp.sum(err * err)
        dy = err * (1.0 / d)
        dg_ref[0:1] += _colsum(dy * xn)
        dx_ref[0] = _rms_bwd(xn, inv, dy * g_ref[...])

    return pl.pallas_call(
        body, name="final_loss", grid=(bl, t // tm),
        in_specs=[_tok(tm, d), _tok(tm, d), _whole(final_g.shape)],
        out_specs=(_tok(tm, d), _whole((8, 128)), _whole((8, d))),
        out_shape=(_sds(x.shape), _sds((8, 128)), _sds((8, d))),
        compiler_params=_cparams(("arbitrary", "arbitrary")),
    )(x, target, final_g)


def _adamw_math(w, g, m, v):
    m = ADAM_B1 * m + (1.0 - ADAM_B1) * g
    v = ADAM_B2 * v + (1.0 - ADAM_B2) * jnp.square(g)
    m_hat = m / (1.0 - ADAM_B1 ** ADAM_STEP)
    v_hat = v / (1.0 - ADAM_B2 ** ADAM_STEP)
    return -ADAM_LR * (m_hat / (jnp.sqrt(v_hat) + ADAM_EPS) + ADAM_WD * w), m, v


def _sum_parts(ref, n):
    g = ref[0].astype(F32)
    for s in range(1, n):
        g = g + ref[s].astype(F32)
    return g


def _adamw_layers(w, m, v, parts, name):
    nl, rows, c = w.shape
    tr = min(rows, 128)

    def body(w_ref, m_ref, v_ref, *refs):
        p_refs, (g_ref, d_ref, mo_ref, vo_ref) = refs[:nl], refs[nl:]
        for layer in range(nl):
            @pl.when(pl.program_id(0) == layer)
            def _(p_ref=p_refs[layer]):
                g = _sum_parts(p_ref, p_ref.shape[0])
                g_ref[...] = g
                d_ref[...], mo_ref[...], vo_ref[...] = _adamw_math(w_ref[...], g, m_ref[...], v_ref[...])

    row = pl.BlockSpec((None, tr, c), lambda l, i: (l, i, 0))
    return pl.pallas_call(
        body, name=name, grid=(nl, rows // tr),
        in_specs=[row, row, row] + [pl.BlockSpec((p.shape[0], tr, c), lambda l, i: (0, i, 0)) for p in parts],
        out_specs=(row,) * 4, out_shape=(_sds(w.shape),) * 4,
        compiler_params=_cparams(("arbitrary", "arbitrary")),
    )(w, m, v, *parts)


def _adamw_small(items, name):
    k = len(items)
    ns = [it[3].shape[0] for it in items]

    def body(*refs):
        ins, outs = refs[:4 * k], refs[4 * k:]
        for i in range(k):
            w_ref, m_ref, v_ref, p_ref = ins[4 * i:4 * i + 4]
            g = _sum_parts(p_ref, ns[i])
            outs[4 * i][...] = g
            outs[4 * i + 1][...], outs[4 * i + 2][...], outs[4 * i + 3][...] = _adamw_math(
                w_ref[...], g, m_ref[...], v_ref[...])

    flat = [a for it in items for a in it]
    res = pl.pallas_call(
        body, name=name,
        out_shape=tuple(_sds(it[0].shape) for it in items for _ in range(4)),
        compiler_params=_cparams(),
    )(*flat)
    return [tuple(res[4 * i:4 * i + 4]) for i in range(k)]


WEIGHTS = ['ada_w', 'ada_b', 'mlp_w1', 'mlp_w2', 'a_w_in', 'a_ln_g', 'a_ln_b', 'a_w_s', 'a_b_s', 'a_w_out', 'b_mu',
           'b_w_in', 'b_w0', 'b_w1', 'b_w2', 'b_a0', 'b_a1', 'b_a2', 'b_g1', 'b_g2', 'b_k_k', 'b_k_a', 'b_r_k',
           'b_ln_g', 'b_ln_b', 'b_w_out', 'final_g']
GATHERED = [('mlp_w1', 2), ('mlp_w2', 1), ('a_w_in', 2), ('a_w_out', 1), ('b_w_in', 2), ('b_w_out', 1),
            ('b_w1', 1), ('b_a1', 1), ('b_g1', 1), ('b_w2', 2), ('b_a2', 2), ('b_g2', 2)]
VECTORS = ['b_mu', 'b_w0', 'b_a0', 'b_k_k', 'b_k_a', 'b_ln_g', 'b_ln_b']
REPLICATED = ['a_ln_g', 'a_ln_b', 'a_w_s', 'a_b_s', 'b_r_k', 'final_g']
ROW_ALIGN = 16


def _pad_rows(a, mult):
    pad = (-a.shape[-2]) % mult
    return jnp.pad(a, [(0, 0)] * (a.ndim - 2) + [(0, pad), (0, 0)]) if pad else a


def _as2d(a):
    if a.ndim == 1:
        return a.reshape(1, -1)
    lead = 1
    for s in a.shape[:-1]:
        lead *= s
    return a.reshape(lead, a.shape[-1])


def kernel(x, c, ada_w, ada_b, mlp_w1, mlp_w2, a_w_in, a_ln_g, a_ln_b, a_w_s, a_b_s, a_w_out, b_mu, b_w_in, b_w0, b_w1, b_w2, b_a0, b_a1, b_a2, b_g1, b_g2, b_k_k, b_k_a, b_r_k, b_ln_g, b_ln_b, b_w_out, final_g, loss_target, m_ada_w, m_ada_b, m_mlp_w1, m_mlp_w2, m_a_w_in, m_a_ln_g, m_a_ln_b, m_a_w_s, m_a_b_s, m_a_w_out, m_b_mu, m_b_w_in, m_b_w0, m_b_w1, m_b_w2, m_b_a0, m_b_a1, m_b_a2, m_b_g1, m_b_g2, m_b_k_k, m_b_k_a, m_b_r_k, m_b_ln_g, m_b_ln_b, m_b_w_out, m_final_g, v_ada_w, v_ada_b, v_mlp_w1, v_mlp_w2, v_a_w_in, v_a_ln_g, v_a_ln_b, v_a_w_s, v_a_b_s, v_a_w_out, v_b_mu, v_b_w_in, v_b_w0, v_b_w1, v_b_w2, v_b_a0, v_b_a1, v_b_a2, v_b_g1, v_b_g2, v_b_k_k, v_b_k_a, v_b_r_k, v_b_ln_g, v_b_ln_b, v_b_w_out, v_final_g):
    given = dict(locals())
    w = {n: given[n] for n in WEIGHTS}
    bl, t, d = x.shape
    nl = ada_w.shape[0]
    nb = N_DEV * bl
    m_tok = bl * t
    me = 4 * lax.axis_index("x") + 2 * lax.axis_index("y") + lax.axis_index("c")

    c_all = _exchange(c, "gather_c", False).reshape(nb, d)
    cols = ada_w.shape[2]
    ada_b_cols = lax.dynamic_slice(ada_b, (0, me * cols), (nl, cols)).reshape(nl, 1, cols)
    mod_cols = _ada_fwd(c_all, ada_w, ada_b_cols)
    mod_full = jnp.moveaxis(_exchange(mod_cols, "gather_mod", False), 0, 2).reshape(nl, nb, 6 * d)
    mod_mine = lax.dynamic_slice(mod_full, (0, me * bl, 0), (nl, bl, 6 * d)).reshape(nl, bl, 6, d)
    mod_mix = jnp.pad(mod_mine[:, :, 0:3], ((0, 0), (0, 0), (0, 5), (0, 0)))
    mod_mlp = jnp.pad(mod_mine[:, :, 3:6], ((0, 0), (0, 0), (0, 5), (0, 0)))

    def unshard(g, ax):
        g = jnp.moveaxis(g, 0, ax)
        return g.reshape(g.shape[:ax] + (g.shape[ax] * g.shape[ax + 1],) + g.shape[ax + 2:])

    bf = lambda a: a.astype(BF16)
    lora_names = ['b_w1', 'b_a1', 'b_g1', 'b_w2', 'b_a2', 'b_g2']
    lora_pack = jnp.concatenate([bf(w[n]).reshape(-1) for n in lora_names]).reshape(-1, 128)
    g_a_in, g_a_out = _gather_call((bf(a_w_in[0]), bf(a_w_out[0])), "gather_sgu_weights")
    full = {'a_w_in': unshard(g_a_in, 1), 'a_w_out': unshard(g_a_out, 0)}
    vec_loc = _pad_rows(jnp.concatenate([_as2d(w[n]) for n in VECTORS], axis=0), ROW_ALIGN)
    n_vec_rows = sum(_as2d(w[n]).shape[0] for n in VECTORS)
    vec = jnp.moveaxis(_exchange(vec_loc, "gather_vectors", False), 0, 1).reshape(N_VEC, d)
    vec = vec.at[n_vec_rows].set(b_r_k.reshape(d))

    e_ind, et_ind = _head_indicators(d)
    gd = d // SGU_GROUPS
    group_ind = (jnp.arange(SGU_GROUPS)[:, None] == jnp.arange(d)[None, :] // gd).astype(BF16)
    bias_full = jnp.repeat(a_b_s[0].T, gd, axis=1)
    pad_c = lambda a, n: jnp.pad(a, ((0, 0), (0, n - a.shape[1])))
    pad_r = lambda a, n: jnp.pad(a, ((0, n - a.shape[0]), (0, 0)))
    sgu_args = (full['a_w_in'], a_ln_g, a_ln_b, a_w_s[0], bias_full, full['a_w_out'])

    x0 = x
    (x1, mix_a), (g_w1_0, g_w2_0) = _sgu_fwd(x0, mod_mix[0], *sgu_args, gather=(bf(mlp_w1[0]), bf(mlp_w2[0])))
    w1_full = [unshard(g_w1_0, 1), None]
    w2_full = [unshard(g_w2_0, 0), None]
    (x2, ff0), (g_w1_1, g_w2_1, g_b_in, g_b_out, g_lora) = _mlp_fwd(
        x1, mod_mlp[0], w1_full[0], w2_full[0],
        gather=(bf(mlp_w1[1]), bf(mlp_w2[1]), bf(b_w_in[0]), bf(b_w_out[0]), lora_pack))
    w1_full[1], w2_full[1] = unshard(g_w1_1, 1), unshard(g_w2_1, 0)
    full['b_w_in'], full['b_w_out'] = unshard(g_b_in, 1), unshard(g_b_out, 0)
    lora_flat, lo = g_lora.reshape(N_DEV, -1), 0
    for n, ax in zip(lora_names, (0, 0, 0, 1, 1, 1)):
        loc = w[n].shape[1:]
        full[n] = unshard(lora_flat[:, lo:lo + w[n].size].reshape((N_DEV,) + loc), ax)
        lo += w[n].size
    rwkv_w = (full['b_w_in'], pad_c(full['b_w1'], LORA_PAD), pad_c(full['b_a1'], LORA_PAD),
              pad_c(full['b_g1'], GATE_PAD), pad_r(full['b_w2'], LORA_PAD), pad_r(full['b_a2'], LORA_PAD),
              pad_r(full['b_g2'], GATE_PAD))
    r, ld, k2, v, a_s, b_s, gate = _rwkv_pre_fwd(x2, mod_mix[1], vec, e_ind, et_ind, rwkv_w)
    y, s0, tinv = _wkv_fwd(r, ld, k2, v, a_s, b_s)
    x3, mix_b = _rwkv_post_fwd(x2, y, r, k2, v, gate, mod_mix[1], vec, e_ind, et_ind, full['b_w_out'])
    (x4, ff1), _ = _mlp_fwd(x3, mod_mlp[1], w1_full[1], w2_full[1])
    dx4, loss_blk, dfinal = _final(x4, loss_target, final_g.reshape(1, d))
    loss = lax.psum(loss_blk[0, 0], ("x", "y", "c"))

    tok = lambda a: a.reshape(m_tok, a.shape[-1])
    shard_rows = lambda g: g.reshape((N_DEV, g.shape[0] // N_DEV) + g.shape[1:])
    (dx3, dmod_mlp1, h_b, dff_b, q_b, dp_b), _ = _mlp_bwd(x3, dx4, ff1, mod_mlp[1], w1_full[1], w2_full[1])
    gw1_1 = _matmul_tn(tok(h_b), tok(dp_b), "grad_mlp_w1_l1", col_shards=N_DEV)
    gw2_1 = shard_rows(_matmul_tn(tok(q_b), tok(dff_b), "grad_mlp_w2_l1"))
    (dy, dr_p, dk2_p, dv_p, dgate_act, o_b, dmix_b, dgate_b, small_post), (rw1_1, rw2_1) = _rwkv_post_bwd(
        dx3, mix_b, y, r, k2, v, gate, mod_mix[1], vec, e_ind, et_ind, full['b_w_out'], scatter=(gw1_1, gw2_1))
    g_b_w_out = shard_rows(_matmul_tn(tok(o_b), tok(dmix_b), "grad_b_w_out"))
    dr_s, dld, dk2_s, dv_s, das, dbs = _wkv_bwd(r, ld, k2, v, a_s, b_s, s0, tinv, dy)
    (dh, dhp, xr_b, xw_b, xk_b, xv_b, xa_b, xg_b, dr_b, dk_b, dv_b, dtw_b, dta_b, dtg_b, tw2_b, ta_b, sg_b,
     dlw_b, dla_b, dg_b, small_pre) = _rwkv_pre_bwd(x2, mod_mix[1], vec, e_ind, et_ind, rwkv_w,
                                                    dr_p, dk2_p, dv_p, dgate_act, dr_s, dld, dk2_s, dv_s, das, dbs)
    g_b_w_in = jnp.concatenate([_matmul_tn(tok(xr_b), tok(dr_b), "grad_b_w_r"),
                                _matmul_tn(tok(xk_b), tok(dk_b), "grad_b_w_k"),
                                _matmul_tn(tok(xv_b), tok(dv_b), "grad_b_w_v")], axis=1)
    shard_cols = lambda g: jnp.moveaxis(g.reshape(g.shape[0], N_DEV, g.shape[1] // N_DEV), 1, 0)
    g_b_w_in = shard_cols(g_b_w_in)
    lw_, lg_ = b_w1.shape[2], b_g1.shape[2]
    small_names = ['b_w1', 'b_a1', 'b_g1', 'b_w2', 'b_a2', 'b_g2'] + VECTORS
    small_parts = [
        shard_rows(_matmul_tn(tok(xw_b), tok(dtw_b), "grad_b_w1")[:, :lw_]),
        shard_rows(_matmul_tn(tok(xa_b), tok(dta_b), "grad_b_a1")[:, :lw_]),
        shard_rows(_matmul_tn(tok(xg_b), tok(dtg_b), "grad_b_g1")[:, :lg_]),
        shard_cols(_matmul_tn(tok(tw2_b), tok(dlw_b), "grad_b_w2")[:lw_]),
        shard_cols(_matmul_tn(tok(ta_b), tok(dla_b), "grad_b_a2")[:lw_]),
        shard_cols(_matmul_tn(tok(sg_b), tok(dg_b), "grad_b_g2")[:lg_]),
        shard_cols(jnp.concatenate([small_pre[0:10], small_post[0:2]], axis=0).astype(BF16)),
    ]
    small_flat = jnp.concatenate([p.reshape(N_DEV, -1) for p in small_parts], axis=1)
    lane = 128
    small_rows = -(-small_flat.shape[1] // (lane * ROW_ALIGN)) * ROW_ALIGN
    small_pack = jnp.pad(small_flat, ((0, 0), (0, small_rows * lane - small_flat.shape[1]))).reshape(
        N_DEV, small_rows, lane)
    dx2, dmod_mix1 = _norm_bwd(x2, dx3, dh, dhp, mod_mix[1], dgate_b)
    (dx1, dmod_mlp0, h_b, dff_b, q_b, dp_b), (r_b_w_in, r_b_w_out, r_small) = _mlp_bwd(
        x1, dx2, ff0, mod_mlp[0], w1_full[0], w2_full[0], scatter=(g_b_w_in, g_b_w_out, small_pack))
    gw1_0 = _matmul_tn(tok(h_b), tok(dp_b), "grad_mlp_w1_l0", col_shards=N_DEV)
    gw2_0 = shard_rows(_matmul_tn(tok(q_b), tok(dff_b), "grad_mlp_w2_l0"))
    (dx0, dmod_mix0, h_b, dpre_b, z_b, dmix_b, small_sgu, d_ws, d_bs), (rw1_0, rw2_0) = _sgu_bwd(
        x0, dx1, mix_a, mod_mix[0], *sgu_args, group_ind, scatter=(gw1_0, gw2_0))
    g_a_w_in = _matmul_tn(tok(h_b), tok(dpre_b), "grad_a_w_in", col_shards=N_DEV)
    g_a_w_out = shard_rows(_matmul_tn(tok(z_b), tok(dmix_b), "grad_a_w_out"))
    r_a_w_in, r_a_w_out = _scatter_call((g_a_w_in, g_a_w_out), "scatter_sgu_grads")

    dmod_mine = jnp.stack([jnp.concatenate([dmod_mix0[:, 0:3], dmod_mlp0[:, 0:3]], axis=1),
                           jnp.concatenate([dmod_mix1[:, 0:3], dmod_mlp1[:, 0:3]], axis=1)], axis=1)
    dmod_all = _exchange(dmod_mine.reshape(bl, nl * 6 * d), "gather_dmod", False)
    dmod_all = jnp.moveaxis(dmod_all.reshape(nb, nl, 6 * d), 0, 1)
    dmod_cols = lax.dynamic_slice(dmod_all, (0, 0, me * cols), (nl, nb, cols))
    g_ada_w, g_ada_b = _ada_bwd(c_all, dmod_cols, dmod_all)

    rep_g = {'a_ln_g': small_sgu[0:1], 'a_ln_b': small_sgu[1:2], 'a_w_s': d_ws.reshape(-1, d), 'a_b_s': d_bs.reshape(1, d),
             'b_r_k': small_post[2:3], 'final_g': dfinal[0:1]}
    rep_rows = [rep_g[n].shape[0] for n in REPLICATED]
    rep_pack = _pad_rows(jnp.concatenate([rep_g[n] for n in REPLICATED], axis=0), 8)
    rep_all = _exchange(rep_pack, "gather_replicated_grads", False, relay=True)

    mom = {n: given['m_' + n] for n in WEIGHTS}
    var = {n: given['v_' + n] for n in WEIGHTS}
    out = {}
    as3d = lambda a: a.reshape((-1,) + a.shape[-2:])
    for n, parts in (('mlp_w1', [rw1_0, rw1_1]), ('mlp_w2', [rw2_0, rw2_1]), ('a_w_in', [r_a_w_in]),
                     ('a_w_out', [r_a_w_out]), ('b_w_in', [r_b_w_in]), ('b_w_out', [r_b_w_out]),
                     ('ada_w', list(g_ada_w))):
        res = _adamw_layers(as3d(w[n]), as3d(mom[n]), as3d(var[n]), parts, "adamw_" + n)
        out[n] = tuple(a.reshape(w[n].shape) for a in res)

    items, names = [], []

    def add(n, part):
        s2 = _as2d(w[n]).shape
        items.append((_as2d(w[n]), _as2d(mom[n]), _as2d(var[n]), part.reshape((part.shape[0],) + s2)))
        names.append(n)

    sflat = r_small.reshape(N_DEV, -1)
    so = 0
    for n in small_names:
        sz = w[n].size
        add(n, sflat[:, so:so + sz])
        so += sz
    ro = 0
    for n, nr in zip(REPLICATED, rep_rows):
        add(n, rep_all[:, ro:ro + nr])
        ro += nr
    add('ada_b', g_ada_b[None])
    for n, res in zip(names, _adamw_small(items, "adamw_small")):
        out[n] = tuple(a.reshape(w[n].shape) for a in res)

    return (loss, dx0, *[out[n][0] for n in WEIGHTS], *[out[n][1] for n in WEIGHTS],
            *[out[n][2] for n in WEIGHTS], *[out[n][3] for n in WEIGHTS])
```

```python
import functools

import jax
import jax.numpy as jnp
from jax import lax
from jax.experimental import pallas as pl
from jax.experimental.pallas import tpu as pltpu

F32 = jnp.float32
BF16 = jnp.bfloat16

N_DEV = 8
RMS_EPS = 1e-6
LN_EPS = 1e-5
HEAD = 64
GN_EPS = HEAD * 1e-5
L2_EPS = 1e-12
SGU_CHUNK = 128
SGU_GROUPS = 8
WKV_CHUNK = 64
WKV_HEADS_PER_STEP = 16
WKV_EXAMPLES_PER_STEP = 2
LORA_PAD = 128
GATE_PAD = 256
ADAM_LR, ADAM_B1, ADAM_B2, ADAM_EPS, ADAM_WD, ADAM_STEP = 0.001, 0.9, 0.999, 1e-08, 0.01, 10
VMEM_LIMIT = 56 * 1024 * 1024


def _cparams(sem=None, **kw):
    if sem is not None:
        kw["dimension_semantics"] = sem
    return pltpu.CompilerParams(vmem_limit_bytes=VMEM_LIMIT, **kw)


def _dot(a, b):
    return jnp.dot(a.astype(BF16), b.astype(BF16), preferred_element_type=F32)


def _dot_nt(a, b):
    return lax.dot_general(a.astype(BF16), b.astype(BF16), (((1,), (1,)), ((), ())), preferred_element_type=F32)


def _dot_tn(a, b):
    return lax.dot_general(a.astype(BF16), b.astype(BF16), (((0,), (0,)), ((), ())), preferred_element_type=F32)


def _bdot(a, b, dims):
    return lax.dot_general(a.astype(BF16), b.astype(BF16), (dims, ((0,), (0,))), preferred_element_type=F32)


@jax.custom_vjp
def _tri_sum(tri, tri_t, x):
    hi = x.astype(BF16)
    lo = (x - hi.astype(F32)).astype(BF16)
    dn = (((2,), (1,)), ((0,), (0,)))
    return (lax.dot_general(tri, hi, dn, preferred_element_type=F32)
            + lax.dot_general(tri, lo, dn, preferred_element_type=F32))


_tri_sum.defvjp(lambda tri, tri_t, x: (_tri_sum(tri, tri_t, x), (tri, tri_t)),
                lambda res, g: (jnp.zeros_like(res[0]), jnp.zeros_like(res[1]), _tri_sum(res[1], res[0], g)))


@jax.custom_vjp
def _bmm_nn(a, b):
    return _bdot(a, b, ((2,), (1,)))


@jax.custom_vjp
def _bmm_nt(a, b):
    return _bdot(a, b, ((2,), (2,)))


@jax.custom_vjp
def _bmm_tn(a, b):
    return _bdot(a, b, ((1,), (1,)))


_bmm_nn.defvjp(lambda a, b: (_bmm_nn(a, b), (a, b)), lambda res, g: (_bmm_nt(g, res[1]), _bmm_tn(res[0], g)))
_bmm_nt.defvjp(lambda a, b: (_bmm_nt(a, b), (a, b)), lambda res, g: (_bmm_nn(g, res[1]), _bmm_tn(g, res[0])))
_bmm_tn.defvjp(lambda a, b: (_bmm_tn(a, b), (a, b)), lambda res, g: (_bmm_nt(res[1], g), _bmm_nn(res[0], g)))


def _tri_inverse(p):
    n = p.shape[1]
    row = lax.broadcasted_iota(jnp.int32, (n, n), 0)
    col = lax.broadcasted_iota(jnp.int32, (n, n), 1)
    tinv = jnp.where(row == col, 1.0, 0.0).astype(F32)[None] + p
    for _ in range(max(1, (n - 1).bit_length()) - 1):
        p = _bmm_nn(p, p)
        tinv = tinv + _bmm_nn(tinv, p)
    return tinv.astype(BF16)


def _tri_solve_fwd(tinv, p, rhs):
    u = _bmm_nn(tinv, rhs)
    return u, (tinv, u)


def _tri_solve_bwd(res, du):
    tinv, u = res
    drhs = _bmm_tn(tinv, du)
    return jnp.zeros_like(tinv), _bmm_nt(drhs, u), drhs


@jax.custom_vjp
def _tri_solve(tinv, p, rhs):
    return _tri_solve_fwd(tinv, p, rhs)[0]


_tri_solve.defvjp(_tri_solve_fwd, _tri_solve_bwd)


def _wkv_chunk(s0, r, ld, k, v, a, b, tinv=None):
    nh, n, _ = r.shape
    row = lax.broadcasted_iota(jnp.int32, (n, n), 0)
    col = lax.broadcasted_iota(jnp.int32, (n, n), 1)
    incl = row >= col
    strict = row > col
    lower = jnp.broadcast_to(jnp.where(incl, 1.0, 0.0).astype(BF16), (nh, n, n))
    upper = jnp.broadcast_to(jnp.where(row <= col, 1.0, 0.0).astype(BF16), (nh, n, n))
    c = _tri_sum(lower, upper, ld)
    c_end = c[:, n - 1:n, :]
    ec, enc, ecx, eend = jnp.exp(c), jnp.exp(-c), jnp.exp(c - ld), jnp.exp(c_end - c)
    ar = jnp.concatenate([a * ecx, r * ec], axis=1)
    mask = jnp.concatenate([strict, incl], axis=0)[None]
    m_b = jnp.where(mask, _bmm_nt(ar, b * enc), 0.0)
    m_k = jnp.where(mask, _bmm_nt(ar, k * enc), 0.0)
    a_ab, a_rb = m_b[:, :n], m_b[:, n:]
    base = _bmm_nt(ar, s0) + _bmm_nn(m_k, v)
    if tinv is None:
        tinv = lax.stop_gradient(_tri_inverse(a_ab))
    u = _tri_solve(tinv, a_ab, base[:, :n])
    y = base[:, n:] + _bmm_nn(a_rb, u)
    s1 = s0 * jnp.exp(c_end) + _bmm_tn(jnp.concatenate([u, v], axis=1), jnp.concatenate([b * eend, k * eend], axis=1))
    return y, s1, tinv


def _wkv_specs(bl, nh, t):
    eb, hb, lc = min(bl, WKV_EXAMPLES_PER_STEP), min(nh, WKV_HEADS_PER_STEP), WKV_CHUNK
    return eb, hb, lc, (bl // eb, nh // hb, t // lc)


def _wkv_fwd(r, ld, k, v, a, b):
    bl, nh, t, n = r.shape
    eb, hb, lc, grid = _wkv_specs(bl, nh, t)
    nc = t // lc
    nb = eb * hb

    def body(r_ref, ld_ref, k_ref, v_ref, a_ref, b_ref, y_ref, s0_ref, tinv_ref, s_scr):
        @pl.when(pl.program_id(2) == 0)
        def _():
            s_scr[...] = jnp.zeros_like(s_scr)

        s0 = s_scr[...]
        s0_ref[:, :, 0] = s0.reshape(eb, hb, n, n)
        y, s1, tinv = _wkv_chunk(
            s0, *(ref[...].reshape(nb, lc, n) for ref in (r_ref, ld_ref, k_ref, v_ref, a_ref, b_ref)))
        y_ref[...] = y.reshape(eb, hb, lc, n)
        tinv_ref[:, :, 0] = tinv.reshape(eb, hb, lc, lc)
        s_scr[...] = s1

    seq = pl.BlockSpec((eb, hb, lc, n), lambda e, h, c: (e, h, c, 0))
    return pl.pallas_call(
        body, name="wkv_fwd", grid=grid,
        in_specs=[seq] * 6,
        out_specs=(seq, pl.BlockSpec((eb, hb, 1, n, n), lambda e, h, c: (e, h, c, 0, 0)),
                   pl.BlockSpec((eb, hb, 1, lc, lc), lambda e, h, c: (e, h, c, 0, 0))),
        out_shape=(jax.ShapeDtypeStruct((bl, nh, t, n), F32), jax.ShapeDtypeStruct((bl, nh, nc, n, n), F32),
                   jax.ShapeDtypeStruct((bl, nh, nc, lc, lc), BF16)),
        scratch_shapes=[pltpu.VMEM((nb, n, n), F32)],
        compiler_params=_cparams(("arbitrary", "arbitrary", "arbitrary")),
    )(r, ld, k, v, a, b)


def _wkv_bwd(r, ld, k, v, a, b, s0_all, tinv_all, dy):
    bl, nh, t, n = r.shape
    eb, hb, lc, grid = _wkv_specs(bl, nh, t)
    nc = t // lc
    nb = eb * hb

    def body(r_ref, ld_ref, k_ref, v_ref, a_ref, b_ref, s0_ref, tinv_ref, dy_ref,
             dr_ref, dld_ref, dk_ref, dv_ref, da_ref, db_ref, ds_scr):
        @pl.when(pl.program_id(2) == 0)
        def _():
            ds_scr[...] = jnp.zeros_like(ds_scr)

        args = (s0_ref[:, :, 0].reshape(nb, n, n),) + tuple(
            ref[...].reshape(nb, lc, n) for ref in (r_ref, ld_ref, k_ref, v_ref, a_ref, b_ref))
        tinv = tinv_ref[:, :, 0].reshape(nb, lc, lc)
        _, vjp = jax.vjp(lambda *xs: _wkv_chunk(*xs, tinv=tinv)[:2], *args)
        ds0, *dseq = vjp((dy_ref[...].reshape(nb, lc, n), ds_scr[...]))
        ds_scr[...] = ds0
        for ref, val in zip((dr_ref, dld_ref, dk_ref, dv_ref, da_ref, db_ref), dseq):
            ref[...] = val.reshape(eb, hb, lc, n)

    seq = pl.BlockSpec((eb, hb, lc, n), lambda e, h, c: (e, h, nc - 1 - c, 0))
    st = pl.BlockSpec((eb, hb, 1, n, n), lambda e, h, c: (e, h, nc - 1 - c, 0, 0))
    ti = pl.BlockSpec((eb, hb, 1, lc, lc), lambda e, h, c: (e, h, nc - 1 - c, 0, 0))
    out = jax.ShapeDtypeStruct((bl, nh, t, n), F32)
    return pl.pallas_call(
        body, name="wkv_bwd", grid=grid,
        in_specs=[seq] * 6 + [st, ti, seq],
        out_specs=(seq,) * 6, out_shape=(out,) * 6,
        scratch_shapes=[pltpu.VMEM((nb, n, n), F32)],
        compiler_params=_cparams(("arbitrary", "arbitrary", "arbitrary")),
    )(r, ld, k, v, a, b, s0_all, tinv_all, dy)


def _scatter_copies(x_refs, o_refs, send_sems, recv_sems, local_sems):
    pos = (lax.axis_index("x"), lax.axis_index("y"), lax.axis_index("c"))
    me = 4 * pos[0] + 2 * pos[1] + pos[2]

    def descriptors():
        sends, arrivals, local = [], [], []
        for i, (x_ref, o_ref) in enumerate(zip(x_refs, o_refs)):
            for m in range(1, N_DEV):
                p = tuple(1 - pos[a] if (m >> (2 - a)) & 1 else pos[a] for a in range(3))
                pidx = 4 * p[0] + 2 * p[1] + p[2]
                k = (N_DEV - 1) * i + m - 1
                for dst, out in ((o_ref.at[me], sends), (o_ref.at[pidx], arrivals)):
                    out.append(pltpu.make_async_remote_copy(
                        src_ref=x_ref.at[pidx], dst_ref=dst, send_sem=send_sems.at[k], recv_sem=recv_sems.at[k],
                        device_id=p, device_id_type=pl.DeviceIdType.MESH))
            local.append(pltpu.make_async_copy(x_ref.at[me], o_ref.at[me], local_sems.at[i]))
        return sends, arrivals, local

    def start():
        sends, _, local = descriptors()
        for cp in local + sends:
            cp.start()

    def finish():
        sends, arrivals, local = descriptors()
        for cp in arrivals:
            cp.wait_recv()
        for cp in sends:
            cp.wait_send()
        for cp in local:
            cp.wait()

    return start, finish


def _gather_copies(x_refs, o_refs, send_sems, recv_sems, local_sems):
    pos = (lax.axis_index("x"), lax.axis_index("y"), lax.axis_index("c"))
    me = 4 * pos[0] + 2 * pos[1] + pos[2]
    far = (2, 4, 6)

    def peer_of(m):
        p = tuple(1 - pos[a] if (m >> (2 - a)) & 1 else pos[a] for a in range(3))
        return p, 4 * p[0] + 2 * p[1] + p[2]

    sibling, _ = peer_of(1)

    def copy(i, k, src, slot, to):
        return pltpu.make_async_remote_copy(
            src_ref=src, dst_ref=o_refs[i].at[slot], send_sem=send_sems.at[(N_DEV - 1) * i + k],
            recv_sem=recv_sems.at[(N_DEV - 1) * i + k], device_id=to, device_id_type=pl.DeviceIdType.MESH)

    def direct(i):
        return [copy(i, m - 1, x_refs[i], me, peer_of(m)[0]) for m in (1,) + far]

    def local(i):
        return pltpu.make_async_copy(x_refs[i], o_refs[i].at[me], local_sems.at[i])

    def start():
        for i in range(len(x_refs)):
            local(i).start()
            for cp in direct(i):
                cp.start()

    def finish():
        n = len(x_refs)
        relays = []
        for i in range(n):
            for m in far:
                origin = peer_of(m)[1]
                copy(i, m - 1, x_refs[i], origin, sibling).wait_recv()
                fwd = copy(i, m, o_refs[i].at[origin], origin, sibling)
                fwd.start()
                relays.append(fwd)
        for i in range(n):
            copy(i, 0, x_refs[i], peer_of(1)[1], sibling).wait_recv()
            for m in far:
                copy(i, m, x_refs[i], peer_of(m ^ 1)[1], sibling).wait_recv()
        for i in range(n):
            for cp in direct(i):
                cp.wait_send()
            local(i).wait()
        for cp in relays:
            cp.wait_send()

    return start, finish


def _scatter_scratch(n):
    return [pltpu.SemaphoreType.DMA(((N_DEV - 1) * n,)), pltpu.SemaphoreType.DMA(((N_DEV - 1) * n,)),
            pltpu.SemaphoreType.DMA((n,))]


_ANY = pl.BlockSpec(memory_space=pl.ANY)


def _scatter_call(arrays, name):
    n = len(arrays)

    def body(*refs):
        start, finish = _scatter_copies(refs[:n], refs[n:2 * n], *refs[2 * n:])
        start()
        finish()

    return pl.pallas_call(
        body, name=name, in_specs=[_ANY] * n, out_specs=(_ANY,) * n,
        out_shape=tuple(_sds(a.shape, a.dtype) for a in arrays), scratch_shapes=_scatter_scratch(n),
    )(*arrays)


def _gather_call(arrays, name):
    n = len(arrays)

    def body(*refs):
        start, finish = _gather_copies(refs[:n], refs[n:2 * n], *refs[2 * n:])
        start()
        finish()

    return pl.pallas_call(
        body, name=name, in_specs=[_ANY] * n, out_specs=(_ANY,) * n,
        out_shape=tuple(_sds((N_DEV,) + a.shape, a.dtype) for a in arrays), scratch_shapes=_scatter_scratch(n),
    )(*arrays)


def _call_with_scatter(body, *, name, grid, in_specs, out_specs, out_shape, scratch_shapes, operands,
                       scatter=(), gather=()):
    assert not (scatter and gather)
    carried = tuple(scatter) or tuple(gather)
    copies = _scatter_copies if scatter else _gather_copies
    recv_shapes = tuple(_sds(a.shape if scatter else (N_DEV,) + a.shape, a.dtype) for a in carried)
    nc, n_in, n_out, n_scr = len(carried), len(in_specs), len(out_specs), len(scratch_shapes)
    if nc == 0:
        return pl.pallas_call(
            body, name=name, grid=grid, in_specs=list(in_specs), out_specs=tuple(out_specs),
            out_shape=tuple(out_shape), scratch_shapes=list(scratch_shapes),
            compiler_params=_cparams(("arbitrary",) * len(grid)))(*operands), ()

    def wrapped(*refs):
        ins, refs = refs[:n_in], refs[n_in:]
        c_in, refs = refs[:nc], refs[nc:]
        outs, refs = refs[:n_out], refs[n_out:]
        c_out, refs = refs[:nc], refs[nc:]
        scr, sems = refs[:n_scr], refs[n_scr:]
        ids = [pl.program_id(a) for a in range(len(grid))]
        first = functools.reduce(jnp.logical_and, [i == 0 for i in ids])
        last = functools.reduce(jnp.logical_and, [i == g - 1 for i, g in zip(ids, grid)])
        start, finish = copies(c_in, c_out, *sems)
        pl.when(first)(start)
        body(*ins, *outs, *scr)
        pl.when(last)(finish)

    res = pl.pallas_call(
        wrapped, name=name, grid=grid,
        in_specs=list(in_specs) + [_ANY] * nc, out_specs=tuple(out_specs) + (_ANY,) * nc,
        out_shape=tuple(out_shape) + recv_shapes,
        scratch_shapes=list(scratch_shapes) + _scatter_scratch(nc),
        compiler_params=_cparams(("arbitrary",) * len(grid)),
    )(*operands, *carried)
    return res[:n_out], res[n_out:]


def _rms(x):
    inv = lax.rsqrt(jnp.mean(x * x, axis=-1, keepdims=True) + RMS_EPS)
    return x * inv, inv


def _rms_bwd(xn, inv, dxn):
    return inv * (dxn - xn * jnp.mean(dxn * xn, axis=-1, keepdims=True))


def _colsum(x):
    return jnp.sum(x, axis=0, keepdims=True)


def _sigmoid(x):
    return 0.5 * (jnp.tanh(0.5 * x) + 1.0)


def _split_bf16(x):
    hi = x.astype(BF16)
    return hi, (x - hi.astype(F32)).astype(BF16)


def _dot_split(x, e):
    hi, lo = _split_bf16(x)
    return jnp.dot(hi, e, preferred_element_type=F32) + jnp.dot(lo, e, preferred_element_type=F32)


@jax.custom_vjp
def _headsum(x, e, et):
    return _dot_split(_dot_split(x, e), et)


_headsum.defvjp(lambda x, e, et: (_headsum(x, e, et), (e, et)),
                lambda res, g: (_headsum(g, *res), jnp.zeros_like(res[0]), jnp.zeros_like(res[1])))


def _make_headsum(e, et):
    return lambda x: _headsum(x, e, et)


def _head_indicators(d):
    e = (jnp.arange(d)[:, None] // HEAD == jnp.arange(128)[None, :]).astype(BF16)
    return e, e.T


def _rwkv_elem(r, k, lw, la, w0, a0, k_k, k_a, headsum):
    z = w0 + lw
    w_log = -(jnp.maximum(-z, 0.0) + jnp.log(1.0 + jnp.exp(-jnp.abs(z)))) - 0.5
    ld = -jnp.exp(w_log)
    a = _sigmoid(a0 + la)
    kkp = k * k_k
    kk = kkp / jnp.maximum(jnp.sqrt(headsum(kkp * kkp)), L2_EPS)
    k2 = k * (1.0 + (a - 1.0) * k_a)
    del r
    return ld, k2, -kk, kk * a


def _rwkv_post(y, r, k2, v, g, ln_g, ln_b, r_k, headsum):
    m = headsum(y) * (1.0 / HEAD)
    yc = y - m
    var = headsum(yc * yc) * (1.0 / HEAD)
    yn = yc * lax.rsqrt(var + GN_EPS)
    bonus = headsum(r * k2 * r_k) * v
    return (yn * ln_g + ln_b + bonus) * g


def _shift_down(h, first_row):
    rolled = pltpu.roll(h, 1, 0)
    row = lax.broadcasted_iota(jnp.int32, h.shape, 0)
    return jnp.where(row == 0, first_row, rolled)


def _shift_up(h, last_row):
    n = h.shape[0]
    rolled = pltpu.roll(h, n - 1, 0)
    row = lax.broadcasted_iota(jnp.int32, h.shape, 0)
    return jnp.where(row == n - 1, last_row, rolled)


def _gelu(p):
    return 0.5 * p * (1.0 + lax.erf(p * 0.7071067811865476))


def _gelu_grad(p):
    return 0.5 * (1.0 + lax.erf(p * 0.7071067811865476)) + p * jnp.exp(-0.5 * p * p) * 0.3989422804014327


def _tok(tm, d):
    return pl.BlockSpec((1, tm, d), lambda e, t, *_: (e, t, 0))


def _per_example(rows, d):
    return pl.BlockSpec((1, rows, d), lambda e, t, *_: (e, 0, 0))


def _whole(shape):
    nd = len(shape)
    return pl.BlockSpec(tuple(shape), lambda *_: (0,) * nd)


def _heads(nh, tm):
    return pl.BlockSpec((1, nh, tm, HEAD), lambda e, t, *_: (e, 0, t, 0))


def _sds(shape, dtype=F32):
    return jax.ShapeDtypeStruct(tuple(shape), dtype)


def _add_rows(ref, first, rows):
    @pl.when(first)
    def _():
        ref[0] = jnp.zeros(ref.shape[1:], ref.dtype)

    for i, r in enumerate(rows):
        ref[0, i:i + 1] += r


def _first(e, t):
    return jnp.logical_and(e == 0, t == 0)


def _ada_fwd(c_all, ada_w, ada_b_cols):
    nl, d, cols = ada_w.shape
    nb = c_all.shape[0]

    def body(c_ref, w_ref, b_ref, o_ref):
        c = c_ref[...]
        cond = c * _sigmoid(c)
        for i in range(nl):
            o_ref[i] = _dot(cond, w_ref[i]) + b_ref[i]

    return pl.pallas_call(
        body, name="ada_fwd", out_shape=_sds((nl, nb, cols)),
        compiler_params=_cparams(),
    )(c_all, ada_w, ada_b_cols)


def _ada_bwd(c_all, dmod_cols, dmod_full):
    nl, nb, cols = dmod_cols.shape
    d = c_all.shape[1]

    def body(c_ref, g_ref, f_ref, b_ref, *o_refs):
        c = c_ref[...]
        cond = c * _sigmoid(c)
        for i in range(nl):
            o_refs[i][0] = _dot_tn(cond, g_ref[i])
            b_ref[i:i + 1] = jnp.sum(f_ref[i], axis=0, keepdims=True)

    res = pl.pallas_call(
        body, name="ada_bwd", out_shape=(_sds((nl, dmod_full.shape[2])),) + (_sds((1, d, cols)),) * nl,
        compiler_params=_cparams(),
    )(c_all, dmod_cols, dmod_full)
    return res[1:], res[0]


def _matmul_tn(a, b, name, col_shards=None, gather=()):
    m, ka = a.shape
    n = b.shape[1]
    tm = min(m, 2048)
    tk = min(ka, 1024)
    tn = min(n, 1024)
    steps = m // tm
    if col_shards:
        cs = n // col_shards
        spt = tn // cs
        out_spec = pl.BlockSpec((spt, tk, cs), lambda i, j, s: (j, i, 0))
        out_shape = _sds((col_shards, ka, cs), BF16)
    else:
        out_spec = pl.BlockSpec((tk, tn), lambda i, j, s: (i, j))
        out_shape = _sds((ka, n), BF16)

    def body(a_ref, b_ref, o_ref, acc):
        s = pl.program_id(2)

        @pl.when(s == 0)
        def _():
            acc[...] = jnp.zeros_like(acc)

        acc[...] += _dot_tn(a_ref[...], b_ref[...])

        @pl.when(s == steps - 1)
        def _():
            if col_shards:
                for q in range(spt):
                    o_ref[q] = acc[:, q * cs:(q + 1) * cs].astype(BF16)
            else:
                o_ref[...] = acc[...].astype(BF16)

    res, got = _call_with_scatter(
        body, name=name, grid=(ka // tk, n // tn, steps),
        in_specs=[pl.BlockSpec((tm, tk), lambda i, j, s: (s, i)), pl.BlockSpec((tm, tn), lambda i, j, s: (s, j))],
        out_specs=(out_spec,), out_shape=(out_shape,),
        scratch_shapes=[pltpu.VMEM((tk, tn), F32)], operands=(a, b), gather=gather)
    return (res[0], got) if gather else res[0]


MLP_FWD_TM = 1024
MLP_FJ = 1024
MLP_BWD_TM = 512
MLP_BWD_FJ = 1024


def _mlp_fwd(x, mod, w1, w2, gather=()):
    bl, t, d = x.shape
    f = w1.shape[1]
    tm, fj = min(t, MLP_FWD_TM), min(f, MLP_FJ)
    nj = f // fj

    def body(x_ref, mod_ref, w1_ref, w2_ref, xo_ref, ff_ref, h_scr, acc):
        j = pl.program_id(2)

        @pl.when(j == 0)
        def _():
            xn, _ = _rms(x_ref[0])
            h_scr[...] = (xn * (1.0 + mod_ref[0, 1:2]) + mod_ref[0, 0:1]).astype(BF16)
            acc[...] = jnp.zeros_like(acc)

        p = jnp.dot(h_scr[...], w1_ref[...], preferred_element_type=F32)
        q = jnp.square(jnp.maximum(p, 0.0))
        acc[...] += _dot(q, w2_ref[...])

        @pl.when(j == nj - 1)
        def _():
            ff_ref[0] = acc[...]
            xo_ref[0] = x_ref[0] + mod_ref[0, 2:3] * acc[...]

    return _call_with_scatter(
        body, name="mlp_fwd", grid=(bl, t // tm, nj),
        in_specs=[_tok(tm, d), _per_example(8, d),
                  pl.BlockSpec((d, fj), lambda e, i, j: (0, j)), pl.BlockSpec((fj, d), lambda e, i, j: (j, 0))],
        out_specs=(_tok(tm, d), _tok(tm, d)),
        out_shape=(_sds(x.shape), _sds(x.shape)),
        scratch_shapes=[pltpu.VMEM((tm, d), BF16), pltpu.VMEM((tm, d), F32)],
        operands=(x, mod, w1, w2), gather=gather)


def _mlp_bwd(x, dxo, ff, mod, w1, w2, scatter=()):
    bl, t, d = x.shape
    f = w1.shape[1]
    tm, fj = min(t, MLP_BWD_TM), min(f, MLP_BWD_FJ)
    nj = f // fj

    def body(x_ref, dxo_ref, ff_ref, mod_ref, w1_ref, w2_ref,
             dx_ref, dmod_ref, h_ref, dff_ref, q_ref, dp_ref, acc):
        ti, j = pl.program_id(1), pl.program_id(2)

        @pl.when(j == 0)
        def _():
            xn, _ = _rms(x_ref[0])
            h_ref[0] = (xn * (1.0 + mod_ref[0, 1:2]) + mod_ref[0, 0:1]).astype(BF16)
            dff_ref[0] = (mod_ref[0, 2:3] * dxo_ref[0]).astype(BF16)
            acc[...] = jnp.zeros_like(acc)

        p = jnp.dot(h_ref[0], w1_ref[...], preferred_element_type=F32)
        rl = jnp.maximum(p, 0.0)
        q_ref[0] = jnp.square(rl).astype(BF16)
        dp = (_dot_nt(dff_ref[0], w2_ref[...]) * (2.0 * rl)).astype(BF16)
        dp_ref[0] = dp
        acc[...] += _dot_nt(dp, w1_ref[...])

        @pl.when(j == nj - 1)
        def _():
            xn, inv = _rms(x_ref[0])
            dh = acc[...]
            dx_ref[0] = dxo_ref[0] + _rms_bwd(xn, inv, dh * (1.0 + mod_ref[0, 1:2]))
            _add_rows(dmod_ref, ti == 0, [_colsum(dh), _colsum(dh * xn), _colsum(dxo_ref[0] * ff_ref[0])])

    big = lambda: pl.BlockSpec((1, tm, fj), lambda e, i, j: (e, i, j))
    return _call_with_scatter(
        body, name="mlp_bwd", grid=(bl, t // tm, nj),
        in_specs=[_tok(tm, d), _tok(tm, d), _tok(tm, d), _per_example(8, d),
                  pl.BlockSpec((d, fj), lambda e, i, j: (0, j)), pl.BlockSpec((fj, d), lambda e, i, j: (j, 0))],
        out_specs=(_tok(tm, d), _per_example(8, d), _tok(tm, d), _tok(tm, d), big(), big()),
        out_shape=(_sds(x.shape), _sds((bl, 8, d)), _sds(x.shape, BF16), _sds(x.shape, BF16),
                   _sds((bl, t, f), BF16), _sds((bl, t, f), BF16)),
        scratch_shapes=[pltpu.VMEM((tm, d), F32)],
        operands=(x, dxo, ff, mod, w1, w2), scatter=scatter)


SGU_TM = 256


def _sgu_core(x, mod_ref, win_ref, lng, lnb, ws_ref, bias_ref):
    tm, d = x.shape
    xn, inv = _rms(x)
    h = (xn * (1.0 + mod_ref[0, 1:2]) + mod_ref[0, 0:1]).astype(BF16)
    pre = jnp.dot(h, win_ref[...], preferred_element_type=F32)
    uv = _gelu(pre)
    u, v = uv[:, :d], uv[:, d:]
    mu = jnp.mean(v, axis=-1, keepdims=True)
    vc = v - mu
    rstd = lax.rsqrt(jnp.mean(vc * vc, axis=-1, keepdims=True) + LN_EPS)
    vhat = vc * rstd
    vln = vhat * lng + lnb
    gd = d // SGU_GROUPS
    rows = []
    for c in range(tm // SGU_CHUNK):
        cols = []
        for g in range(SGU_GROUPS):
            cols.append(_dot(ws_ref[g], vln[c * SGU_CHUNK:(c + 1) * SGU_CHUNK, g * gd:(g + 1) * gd]))
        rows.append(jnp.concatenate(cols, axis=1) + bias_ref[...])
    sv = jnp.concatenate(rows, axis=0)
    return xn, inv, h, pre, u, vhat, rstd, vln, sv


def _sgu_masked(ws_ref, wm_scr):
    row = lax.broadcasted_iota(jnp.int32, (SGU_CHUNK, SGU_CHUNK), 0)
    col = lax.broadcasted_iota(jnp.int32, (SGU_CHUNK, SGU_CHUNK), 1)
    for g in range(SGU_GROUPS):
        wm_scr[g] = jnp.where(row >= col, ws_ref[g], 0.0).astype(BF16)


def _sgu_fwd(x, mod, w_in, ln_g, ln_b, w_s, bias_full, w_out, gather=()):
    bl, t, d = x.shape
    tm = min(t, SGU_TM)

    def body(x_ref, mod_ref, win_ref, lng_ref, lnb_ref, ws_ref, bias_ref, wout_ref, xo_ref, mix_ref, wm_scr):
        _sgu_masked(ws_ref, wm_scr)
        xt = x_ref[0]
        *_, u, _, _, _, sv = _sgu_core(xt, mod_ref, win_ref, lng_ref[...], lnb_ref[...], wm_scr, bias_ref)
        mix = _dot(u * sv, wout_ref[...])
        mix_ref[0] = mix
        xo_ref[0] = xt + mod_ref[0, 2:3] * mix

    return _call_with_scatter(
        body, name="sgu_fwd", grid=(bl, t // tm),
        in_specs=[_tok(tm, d), _per_example(8, d), _whole(w_in.shape), _whole(ln_g.shape), _whole(ln_b.shape),
                  _whole(w_s.shape), _whole(bias_full.shape), _whole(w_out.shape)],
        out_specs=(_tok(tm, d), _tok(tm, d)),
        out_shape=(_sds(x.shape), _sds(x.shape)),
        scratch_shapes=[pltpu.VMEM(w_s.shape, BF16)],
        operands=(x, mod, w_in, ln_g, ln_b, w_s, bias_full, w_out), gather=gather)


def _sgu_bwd(x, dxo, mix, mod, w_in, ln_g, ln_b, w_s, bias_full, w_out, group_ind, scatter=()):
    bl, t, d = x.shape
    tm = min(t, SGU_TM)
    gd = d // SGU_GROUPS

    def body(x_ref, dxo_ref, mix_ref, mod_ref, win_ref, lng_ref, lnb_ref, ws_ref, bias_ref, wout_ref, ind_ref,
             dx_ref, dmod_ref, h_ref, dpre_ref, z_ref, dmix_ref, small_ref, dws_ref, dbs_ref, wm_scr, dbias_scr):
        e, ti = pl.program_id(0), pl.program_id(1)
        _sgu_masked(ws_ref, wm_scr)
        xt, dxo = x_ref[0], dxo_ref[0]
        lng = lng_ref[...]
        xn, inv, h, pre, u, vhat, rstd, vln, sv = _sgu_core(xt, mod_ref, win_ref, lng, lnb_ref[...], wm_scr, bias_ref)
        h_ref[0] = h
        z_ref[0] = (u * sv).astype(BF16)
        dmix = mod_ref[0, 2:3] * dxo
        dmix_ref[0] = dmix.astype(BF16)
        dz = _dot_nt(dmix, wout_ref[...])
        du, dsv = dz * sv, dz * u

        @pl.when(_first(e, ti))
        def _():
            dws_ref[...] = jnp.zeros_like(dws_ref)
            dbias_scr[...] = jnp.zeros_like(dbias_scr)
            small_ref[...] = jnp.zeros_like(small_ref)

        row = lax.broadcasted_iota(jnp.int32, (SGU_CHUNK, SGU_CHUNK), 0)
        col = lax.broadcasted_iota(jnp.int32, (SGU_CHUNK, SGU_CHUNK), 1)
        rows = []
        for c in range(tm // SGU_CHUNK):
            rs = slice(c * SGU_CHUNK, (c + 1) * SGU_CHUNK)
            dbias_scr[...] += dsv[rs]
            cols = []
            for g in range(SGU_GROUPS):
                cs = slice(g * gd, (g + 1) * gd)
                cols.append(_dot_tn(wm_scr[g], dsv[rs, cs]))
                dws_ref[g] += jnp.where(row >= col, _dot_nt(dsv[rs, cs], vln[rs, cs]), 0.0)
            rows.append(jnp.concatenate(cols, axis=1))
        dvln = jnp.concatenate(rows, axis=0)
        small_ref[0:1] += _colsum(dvln * vhat)
        small_ref[1:2] += _colsum(dvln)
        dvhat = dvln * lng
        dv = rstd * (dvhat - jnp.mean(dvhat, axis=-1, keepdims=True)
                     - vhat * jnp.mean(dvhat * vhat, axis=-1, keepdims=True))
        dpre = (jnp.concatenate([du, dv], axis=1) * _gelu_grad(pre)).astype(BF16)
        dpre_ref[0] = dpre
        dh = _dot_nt(dpre, win_ref[...])
        dx_ref[0] = dxo + _rms_bwd(xn, inv, dh * (1.0 + mod_ref[0, 1:2]))
        _add_rows(dmod_ref, ti == 0, [_colsum(dh), _colsum(dh * xn), _colsum(dxo * mix_ref[0])])

        @pl.when(jnp.logical_and(e == bl - 1, ti == t // tm - 1))
        def _():
            hi, lo = _split_bf16(dbias_scr[...])
            ind = ind_ref[...]
            dbs_ref[...] = (lax.dot_general(ind, hi, (((1,), (1,)), ((), ())), preferred_element_type=F32)
                            + lax.dot_general(ind, lo, (((1,), (1,)), ((), ())), preferred_element_type=F32))

    return _call_with_scatter(
        body, name="sgu_bwd", grid=(bl, t // tm),
        in_specs=[_tok(tm, d), _tok(tm, d), _tok(tm, d), _per_example(8, d), _whole(w_in.shape), _whole(ln_g.shape),
                  _whole(ln_b.shape), _whole(w_s.shape), _whole(bias_full.shape), _whole(w_out.shape),
                  _whole(group_ind.shape)],
        out_specs=(_tok(tm, d), _per_example(8, d), _tok(tm, d), _tok(tm, 2 * d), _tok(tm, d), _tok(tm, d),
                   _whole((8, d)), _whole(w_s.shape), _whole((SGU_GROUPS, SGU_CHUNK))),
        out_shape=(_sds(x.shape), _sds((bl, 8, d)), _sds(x.shape, BF16), _sds((bl, t, 2 * d), BF16),
                   _sds(x.shape, BF16), _sds(x.shape, BF16), _sds((8, d)), _sds(w_s.shape),
                   _sds((SGU_GROUPS, SGU_CHUNK))),
        scratch_shapes=[pltpu.VMEM(w_s.shape, BF16), pltpu.VMEM((SGU_CHUNK, d), F32)],
        operands=(x, dxo, mix, mod, w_in, ln_g, ln_b, w_s, bias_full, w_out, group_ind), scatter=scatter)


RWKV_TM = 256
N_VEC = 16


def _rwkv_pre_core(x_ref, halo_ref, mod_ref, vec_ref, ti):
    xn, inv = _rms(x_ref[0])
    scale1, shift = 1.0 + mod_ref[0, 1:2], mod_ref[0, 0:1]
    h = xn * scale1 + shift
    hn, _ = _rms(halo_ref[0])
    hh = hn * scale1 + shift
    first = jnp.where(ti == 0, 0.0, hh[7:8])
    xx = _shift_down(h, first) - h
    xs = [h + xx * vec_ref[i:i + 1] for i in range(6)]
    return xn, inv, xx, xs


def _rwkv_proj(xs, wrkv_ref, w1_ref, a1_ref, g1_ref, w2_ref, a2_ref, g2_ref):
    d = xs[0].shape[1]
    xr, xw, xk, xv, xa, xg = [z.astype(BF16) for z in xs]
    r = jnp.dot(xr, wrkv_ref[:, 0:d], preferred_element_type=F32)
    k = jnp.dot(xk, wrkv_ref[:, d:2 * d], preferred_element_type=F32)
    v = jnp.dot(xv, wrkv_ref[:, 2 * d:3 * d], preferred_element_type=F32)
    tw2 = jnp.tanh(jnp.dot(xw, w1_ref[...], preferred_element_type=F32))
    ta = jnp.dot(xa, a1_ref[...], preferred_element_type=F32)
    sg = _sigmoid(jnp.dot(xg, g1_ref[...], preferred_element_type=F32))
    lw, la, g = _dot(tw2, w2_ref[...]), _dot(ta, a2_ref[...]), _dot(sg, g2_ref[...])
    return (xr, xw, xk, xv, xa, xg), r, k, v, tw2, ta, sg, lw, la, g


def _to_heads(ref, val, nh):
    for hd in range(nh):
        ref[0, hd] = val[:, hd * HEAD:(hd + 1) * HEAD]


def _from_heads(ref, scr, nh):
    for hd in range(nh):
        scr[:, hd * HEAD:(hd + 1) * HEAD] = ref[0, hd]
    return scr[...]


def _rwkv_weight_specs(ws):
    return [_whole(w.shape) for w in ws]


def _rwkv_pre_fwd(x, mod, vec, e_ind, et_ind, weights):
    bl, t, d = x.shape
    tm = min(t, RWKV_TM)
    nh = d // HEAD
    hb = tm // 8

    def body(x_ref, halo_ref, mod_ref, vec_ref, e_ref, et_ref, wrkv, w1, a1, g1, w2, a2, g2,
             r_ref, ld_ref, k2_ref, v_ref, as_ref, bs_ref, g_ref):
        ti = pl.program_id(1)
        _, _, _, xs = _rwkv_pre_core(x_ref, halo_ref, mod_ref, vec_ref, ti)
        _, r, k, v, _, _, _, lw, la, g = _rwkv_proj(xs, wrkv, w1, a1, g1, w2, a2, g2)
        headsum = _make_headsum(e_ref[...], et_ref[...])
        ld, k2, a_s, b_s = _rwkv_elem(r, k, lw, la, vec_ref[6:7], vec_ref[7:8], vec_ref[8:9], vec_ref[9:10], headsum)
        g_ref[0] = g
        for ref, val in ((r_ref, r), (ld_ref, ld), (k2_ref, k2), (v_ref, v), (as_ref, a_s), (bs_ref, b_s)):
            _to_heads(ref, val, nh)

    halo = pl.BlockSpec((1, 8, d), lambda e, i: (e, jnp.maximum(i * hb - 1, 0), 0))
    hs = _sds((bl, nh, t, HEAD))
    return pl.pallas_call(
        body, name="rwkv_pre_fwd", grid=(bl, t // tm),
        in_specs=[_tok(tm, d), halo, _per_example(8, d), _whole(vec.shape), _whole(e_ind.shape), _whole(et_ind.shape)]
        + _rwkv_weight_specs(weights),
        out_specs=(_heads(nh, tm),) * 6 + (_tok(tm, d),),
        out_shape=(hs,) * 6 + (_sds(x.shape),),
        compiler_params=_cparams(("arbitrary", "arbitrary")),
    )(x, x, mod, vec, e_ind, et_ind, *weights)


def _rwkv_post_fwd(x, y, r, k2, v, g, mod, vec, e_ind, et_ind, w_out):
    bl, t, d = x.shape
    tm = min(t, RWKV_TM)
    nh = d // HEAD

    def body(x_ref, y_ref, r_ref, k2_ref, v_ref, g_ref, mod_ref, vec_ref, e_ref, et_ref, wout_ref,
             xo_ref, mix_ref, s0, s1, s2, s3):
        headsum = _make_headsum(e_ref[...], et_ref[...])
        yv, rv, kv, vv = (_from_heads(ref, scr, nh) for ref, scr in
                          ((y_ref, s0), (r_ref, s1), (k2_ref, s2), (v_ref, s3)))
        o = _rwkv_post(yv, rv, kv, vv, g_ref[0], vec_ref[10:11], vec_ref[11:12], vec_ref[12:13], headsum)
        mix = _dot(o, wout_ref[...])
        mix_ref[0] = mix
        xo_ref[0] = x_ref[0] + mod_ref[0, 2:3] * mix

    return pl.pallas_call(
        body, name="rwkv_post_fwd", grid=(bl, t // tm),
        in_specs=[_tok(tm, d)] + [_heads(nh, tm)] * 4 + [_tok(tm, d), _per_example(8, d), _whole(vec.shape),
                                                         _whole(e_ind.shape), _whole(et_ind.shape), _whole(w_out.shape)],
        out_specs=(_tok(tm, d), _tok(tm, d)),
        out_shape=(_sds(x.shape), _sds(x.shape)),
        scratch_shapes=[pltpu.VMEM((tm, d), F32)] * 4,
        compiler_params=_cparams(("arbitrary", "arbitrary")),
    )(x, y, r, k2, v, g, mod, vec, e_ind, et_ind, w_out)


def _rwkv_post_bwd(dxo, mix, y, r, k2, v, g, mod, vec, e_ind, et_ind, w_out, scatter=()):
    bl, t, d = dxo.shape
    tm = min(t, RWKV_TM)
    nh = d // HEAD

    def body(dxo_ref, mix_ref, y_ref, r_ref, k2_ref, v_ref, g_ref, mod_ref, vec_ref, e_ref, et_ref, wout_ref,
             dy_ref, dr_ref, dk2_ref, dv_ref, dg_ref, o_ref, dmix_ref, dgate_ref, small_ref, s0, s1, s2, s3):
        e, ti = pl.program_id(0), pl.program_id(1)
        headsum = _make_headsum(e_ref[...], et_ref[...])
        yv, rv, kv, vv = (_from_heads(ref, scr, nh) for ref, scr in
                          ((y_ref, s0), (r_ref, s1), (k2_ref, s2), (v_ref, s3)))
        dxo = dxo_ref[0]
        dmix = mod_ref[0, 2:3] * dxo
        dmix_ref[0] = dmix.astype(BF16)
        do = _dot_nt(dmix, wout_ref[...])
        post = functools.partial(_rwkv_post, headsum=headsum)
        o, vjp = jax.vjp(post, yv, rv, kv, vv, g_ref[0], vec_ref[10:11], vec_ref[11:12], vec_ref[12:13])
        o_ref[0] = o.astype(BF16)
        dy, dr, dk2, dv, dg, dlng, dlnb, drk = vjp(do)
        _to_heads(dy_ref, dy, nh)
        dr_ref[0], dk2_ref[0], dv_ref[0], dg_ref[0] = dr, dk2, dv, dg
        zero = jnp.zeros((1, d), F32)
        _add_rows(dgate_ref, ti == 0, [zero, zero, _colsum(dxo * mix_ref[0])])

        @pl.when(_first(e, ti))
        def _():
            small_ref[...] = jnp.zeros_like(small_ref)

        small_ref[0:1] += dlng
        small_ref[1:2] += dlnb
        small_ref[2:3] += drk

    return _call_with_scatter(
        body, name="rwkv_post_bwd", grid=(bl, t // tm),
        in_specs=[_tok(tm, d), _tok(tm, d)] + [_heads(nh, tm)] * 4
        + [_tok(tm, d), _per_example(8, d), _whole(vec.shape), _whole(e_ind.shape), _whole(et_ind.shape),
           _whole(w_out.shape)],
        out_specs=(_heads(nh, tm),) + (_tok(tm, d),) * 6 + (_per_example(8, d), _whole((8, d))),
        out_shape=(_sds((bl, nh, t, HEAD)),) + (_sds(dxo.shape),) * 4 + (_sds(dxo.shape, BF16),) * 2
        + (_sds((bl, 8, d)), _sds((8, d))),
        scratch_shapes=[pltpu.VMEM((tm, d), F32)] * 4,
        operands=(dxo, mix, y, r, k2, v, g, mod, vec, e_ind, et_ind, w_out), scatter=scatter)


RWKV_BWD_TM = 128


def _rwkv_pre_bwd(x, mod, vec, e_ind, et_ind, weights, dr_p, dk2_p, dv_p, dg, dr_s, dld, dk2_s, dv_s, das, dbs):
    bl, t, d = x.shape
    tm = min(t, RWKV_BWD_TM)
    nh = d // HEAD
    hb = tm // 8
    lp, gp = LORA_PAD, GATE_PAD

    def body(x_ref, halo_ref, mod_ref, vec_ref, e_ref, et_ref, wrkv, w1, a1, g1, w2, a2, g2,
             drp_ref, dk2p_ref, dvp_ref, dg_ref, drs_ref, dld_ref, dk2s_ref, dvs_ref, das_ref, dbs_ref,
             dh_ref, dhp_ref, xr_ref, xw_ref, xk_ref, xv_ref, xa_ref, xg_ref, dr_ref, dk_ref, dv_ref,
             dtw_ref, dta_ref, dtg_ref, tw2_ref, ta_ref, sg_ref, dlw_ref, dla_ref, dgb_ref, small_ref,
             s0, s1, s2, s3, s4, s5):
        e, ti = pl.program_id(0), pl.program_id(1)
        _, _, xx, xs = _rwkv_pre_core(x_ref, halo_ref, mod_ref, vec_ref, ti)
        xb, r, k, v, tw2, ta, sg, lw, la, _ = _rwkv_proj(xs, wrkv, w1, a1, g1, w2, a2, g2)
        for ref, val in zip((xr_ref, xw_ref, xk_ref, xv_ref, xa_ref, xg_ref), xb):
            ref[0] = val
        headsum = _make_headsum(e_ref[...], et_ref[...])
        drs, dld, dk2s, dvs, das, dbs_ = (_from_heads(ref, scr, nh) for ref, scr in
                                          ((drs_ref, s0), (dld_ref, s1), (dk2s_ref, s2), (dvs_ref, s3),
                                           (das_ref, s4), (dbs_ref, s5)))
        elem = functools.partial(_rwkv_elem, r, headsum=headsum)
        _, vjp = jax.vjp(elem, k, lw, la, vec_ref[6:7], vec_ref[7:8], vec_ref[8:9], vec_ref[9:10])
        dk, dlw, dla, dw0, da0, dkk, dka = vjp((dld, dk2p_ref[0] + dk2s, das, dbs_))
        dr = drp_ref[0] + drs
        dv = dvp_ref[0] + dvs
        dgv = dg_ref[0]
        dtg = _dot_nt(dgv, g2[...]) * sg * (1.0 - sg)
        dtw = _dot_nt(dlw, w2[...]) * (1.0 - tw2 * tw2)
        dta = _dot_nt(dla, a2[...])
        dr_ref[0], dk_ref[0], dv_ref[0] = dr.astype(BF16), dk.astype(BF16), dv.astype(BF16)
        dtw_ref[0], dta_ref[0], dtg_ref[0] = dtw.astype(BF16), dta.astype(BF16), dtg.astype(BF16)
        tw2_ref[0], ta_ref[0], sg_ref[0] = tw2.astype(BF16), ta.astype(BF16), sg.astype(BF16)
        dlw_ref[0], dla_ref[0], dgb_ref[0] = dlw.astype(BF16), dla.astype(BF16), dgv.astype(BF16)
        dxs = (_dot_nt(dr, wrkv[:, 0:d]), _dot_nt(dtw, w1[...]), _dot_nt(dk, wrkv[:, d:2 * d]),
               _dot_nt(dv, wrkv[:, 2 * d:3 * d]), _dot_nt(dta, a1[...]), _dot_nt(dtg, g1[...]))

        @pl.when(_first(e, ti))
        def _():
            small_ref[...] = jnp.zeros_like(small_ref)

        total = jnp.zeros((tm, d), F32)
        dhp = jnp.zeros((tm, d), F32)
        for i, dxi in enumerate(dxs):
            total += dxi
            dhp += dxi * vec_ref[i:i + 1]
            small_ref[i:i + 1] += _colsum(dxi * xx)
        dh_ref[0], dhp_ref[0] = total - dhp, dhp
        small_ref[6:7] += dw0
        small_ref[7:8] += da0
        small_ref[8:9] += dkk
        small_ref[9:10] += dka

    halo = pl.BlockSpec((1, 8, d), lambda e, i: (e, jnp.maximum(i * hb - 1, 0), 0))
    tokd, tokl, tokg = _tok(tm, d), _tok(tm, lp), _tok(tm, gp)
    bf = lambda w: _sds((bl, t, w), BF16)
    return pl.pallas_call(
        body, name="rwkv_pre_bwd", grid=(bl, t // tm),
        in_specs=[tokd, halo, _per_example(8, d), _whole(vec.shape), _whole(e_ind.shape), _whole(et_ind.shape)]
        + _rwkv_weight_specs(weights) + [tokd] * 4 + [_heads(nh, tm)] * 6,
        out_specs=(tokd, tokd) + (tokd,) * 6 + (tokd,) * 3 + (tokl, tokl, tokg, tokl, tokl, tokg)
        + (tokd, tokd, tokd, _whole((N_VEC, d))),
        out_shape=(_sds(x.shape), _sds(x.shape)) + (bf(d),) * 9 + (bf(lp), bf(lp), bf(gp), bf(lp), bf(lp), bf(gp))
        + (bf(d), bf(d), bf(d), _sds((N_VEC, d))),
        scratch_shapes=[pltpu.VMEM((tm, d), F32)] * 6,
        compiler_params=_cparams(("arbitrary", "arbitrary")),
    )(x, x, mod, vec, e_ind, et_ind, *weights, dr_p, dk2_p, dv_p, dg, dr_s, dld, dk2_s, dv_s, das, dbs)


def _norm_bwd(x, dxo, dh, dhprev, mod, dgate):
    bl, t, d = x.shape
    tm = min(t, RWKV_TM)
    hb = tm // 8
    last_blk = t // 8 - 1

    def body(x_ref, dxo_ref, dh_ref, dhp_ref, nxt_ref, mod_ref, dgate_ref, dx_ref, dmod_ref):
        ti = pl.program_id(1)
        xn, inv = _rms(x_ref[0])
        last = jnp.where(ti == t // tm - 1, 0.0, nxt_ref[0, 0:1])
        dh = dh_ref[0] + _shift_up(dhp_ref[0], last)
        dx_ref[0] = dxo_ref[0] + _rms_bwd(xn, inv, dh * (1.0 + mod_ref[0, 1:2]))

        @pl.when(ti == 0)
        def _():
            dmod_ref[0] = dgate_ref[0]

        dmod_ref[0, 0:1] += _colsum(dh)
        dmod_ref[0, 1:2] += _colsum(dh * xn)

    nxt = pl.BlockSpec((1, 8, d), lambda e, i: (e, jnp.minimum((i + 1) * hb, last_blk), 0))
    return pl.pallas_call(
        body, name="norm_bwd", grid=(bl, t // tm),
        in_specs=[_tok(tm, d)] * 4 + [nxt, _per_example(8, d), _per_example(8, d)],
        out_specs=(_tok(tm, d), _per_example(8, d)),
        out_shape=(_sds(x.shape), _sds((bl, 8, d))),
        compiler_params=_cparams(("arbitrary", "arbitrary")),
    )(x, dxo, dh, dhprev, dhprev, mod, dgate)


def _final(x, target, final_g):
    bl, t, d = x.shape
    tm = min(t, 512)

    def body(x_ref, tgt_ref, g_ref, dx_ref, loss_ref, dg_ref):
        e, ti = pl.program_id(0), pl.program_id(1)

        @pl.when(_first(e, ti))
        def _():
            loss_ref[...] = jnp.zeros_like(loss_ref)
            dg_ref[...] = jnp.zeros_like(dg_ref)

        xn, inv = _rms(x_ref[0])
        err = xn * g_ref[...] - tgt_ref[0]
        loss_ref[...] += (0.5 / d) * jnp.sum(err * err)
        dy = err * (1.0 / d)
        dg_ref[0:1] += _colsum(dy * xn)
        dx_ref[0] = _rms_bwd(xn, inv, dy * g_ref[...])

    return pl.pallas_call(
        body, name="final_loss", grid=(bl, t // tm),
        in_specs=[_tok(tm, d), _tok(tm, d), _whole(final_g.shape)],
        out_specs=(_tok(tm, d), _whole((8, 128)), _whole((8, d))),
        out_shape=(_sds(x.shape), _sds((8, 128)), _sds((8, d))),
        compiler_params=_cparams(("arbitrary", "arbitrary")),
    )(x, target, final_g)


def _adamw_math(w, g, m, v):
    m = ADAM_B1 * m + (1.0 - ADAM_B1) * g
    v = ADAM_B2 * v + (1.0 - ADAM_B2) * jnp.square(g)
    m_hat = m / (1.0 - ADAM_B1 ** ADAM_STEP)
    v_hat = v / (1.0 - ADAM_B2 ** ADAM_STEP)
    return -ADAM_LR * (m_hat / (jnp.sqrt(v_hat) + ADAM_EPS) + ADAM_WD * w), m, v


def _sum_parts(ref, n):
    g = ref[0].astype(F32)
    for s in range(1, n):
        g = g + ref[s].astype(F32)
    return g


def _adamw_layers(w, m, v, parts, name):
    nl, rows, c = w.shape
    tr = min(rows, 128)

    def body(w_ref, m_ref, v_ref, *refs):
        p_refs, (g_ref, d_ref, mo_ref, vo_ref) = refs[:nl], refs[nl:]
        for layer in range(nl):
            @pl.when(pl.program_id(0) == layer)
            def _(p_ref=p_refs[layer]):
                g = _sum_parts(p_ref, p_ref.shape[0])
                g_ref[...] = g
                d_ref[...], mo_ref[...], vo_ref[...] = _adamw_math(w_ref[...], g, m_ref[...], v_ref[...])

    row = pl.BlockSpec((None, tr, c), lambda l, i: (l, i, 0))
    return pl.pallas_call(
        body, name=name, grid=(nl, rows // tr),
        in_specs=[row, row, row] + [pl.BlockSpec((p.shape[0], tr, c), lambda l, i: (0, i, 0)) for p in parts],
        out_specs=(row,) * 4, out_shape=(_sds(w.shape),) * 4,
        compiler_params=_cparams(("arbitrary", "arbitrary")),
    )(w, m, v, *parts)


def _adamw_small(items, name):
    k = len(items)
    ns = [it[3].shape[0] for it in items]

    def body(*refs):
        ins, outs = refs[:4 * k], refs[4 * k:]
        for i in range(k):
            w_ref, m_ref, v_ref, p_ref = ins[4 * i:4 * i + 4]
            g = _sum_parts(p_ref, ns[i])
            outs[4 * i][...] = g
            outs[4 * i + 1][...], outs[4 * i + 2][...], outs[4 * i + 3][...] = _adamw_math(
                w_ref[...], g, m_ref[...], v_ref[...])

    flat = [a for it in items for a in it]
    res = pl.pallas_call(
        body, name=name,
        out_shape=tuple(_sds(it[0].shape) for it in items for _ in range(4)),
        compiler_params=_cparams(),
    )(*flat)
    return [tuple(res[4 * i:4 * i + 4]) for i in range(k)]


WEIGHTS = ['ada_w', 'ada_b', 'mlp_w1', 'mlp_w2', 'a_w_in', 'a_ln_g', 'a_ln_b', 'a_w_s', 'a_b_s', 'a_w_out', 'b_mu',
           'b_w_in', 'b_w0', 'b_w1', 'b_w2', 'b_a0', 'b_a1', 'b_a2', 'b_g1', 'b_g2', 'b_k_k', 'b_k_a', 'b_r_k',
           'b_ln_g', 'b_ln_b', 'b_w_out', 'final_g']
VECTORS = ['b_mu', 'b_w0', 'b_a0', 'b_k_k', 'b_k_a', 'b_ln_g', 'b_ln_b']
REPLICATED = ['a_ln_g', 'a_ln_b', 'a_w_s', 'a_b_s', 'b_r_k', 'final_g']
ROW_ALIGN = 16


def _pad_rows(a, mult):
    pad = (-a.shape[-2]) % mult
    return jnp.pad(a, [(0, 0)] * (a.ndim - 2) + [(0, pad), (0, 0)]) if pad else a


def _as2d(a):
    if a.ndim == 1:
        return a.reshape(1, -1)
    lead = 1
    for s in a.shape[:-1]:
        lead *= s
    return a.reshape(lead, a.shape[-1])


def kernel(x, c, ada_w, ada_b, mlp_w1, mlp_w2, a_w_in, a_ln_g, a_ln_b, a_w_s, a_b_s, a_w_out, b_mu, b_w_in, b_w0, b_w1, b_w2, b_a0, b_a1, b_a2, b_g1, b_g2, b_k_k, b_k_a, b_r_k, b_ln_g, b_ln_b, b_w_out, final_g, loss_target, m_ada_w, m_ada_b, m_mlp_w1, m_mlp_w2, m_a_w_in, m_a_ln_g, m_a_ln_b, m_a_w_s, m_a_b_s, m_a_w_out, m_b_mu, m_b_w_in, m_b_w0, m_b_w1, m_b_w2, m_b_a0, m_b_a1, m_b_a2, m_b_g1, m_b_g2, m_b_k_k, m_b_k_a, m_b_r_k, m_b_ln_g, m_b_ln_b, m_b_w_out, m_final_g, v_ada_w, v_ada_b, v_mlp_w1, v_mlp_w2, v_a_w_in, v_a_ln_g, v_a_ln_b, v_a_w_s, v_a_b_s, v_a_w_out, v_b_mu, v_b_w_in, v_b_w0, v_b_w1, v_b_w2, v_b_a0, v_b_a1, v_b_a2, v_b_g1, v_b_g2, v_b_k_k, v_b_k_a, v_b_r_k, v_b_ln_g, v_b_ln_b, v_b_w_out, v_final_g):
    given = dict(locals())
    w = {n: given[n] for n in WEIGHTS}
    bl, t, d = x.shape
    nl = ada_w.shape[0]
    nb = N_DEV * bl
    m_tok = bl * t
    me = 4 * lax.axis_index("x") + 2 * lax.axis_index("y") + lax.axis_index("c")

    bf = lambda a: a.astype(BF16)
    vec_loc = _pad_rows(jnp.concatenate([_as2d(w[n]) for n in VECTORS], axis=0), ROW_ALIGN)
    g_c, g_vec, g_a_in, g_a_out = _gather_call((c, vec_loc, bf(a_w_in[0]), bf(a_w_out[0])), "gather_first")

    c_all = g_c.reshape(nb, d)
    cols = ada_w.shape[2]
    ada_b_cols = lax.dynamic_slice(ada_b, (0, me * cols), (nl, cols)).reshape(nl, 1, cols)
    mod_cols = _ada_fwd(c_all, ada_w, ada_b_cols)
    mod_full = jnp.moveaxis(_gather_call((mod_cols,), "gather_mod")[0], 0, 2).reshape(nl, nb, 6 * d)
    mod_mine = lax.dynamic_slice(mod_full, (0, me * bl, 0), (nl, bl, 6 * d)).reshape(nl, bl, 6, d)
    mod_mix = jnp.pad(mod_mine[:, :, 0:3], ((0, 0), (0, 0), (0, 5), (0, 0)))
    mod_mlp = jnp.pad(mod_mine[:, :, 3:6], ((0, 0), (0, 0), (0, 5), (0, 0)))

    def unshard(g, ax):
        g = jnp.moveaxis(g, 0, ax)
        return g.reshape(g.shape[:ax] + (g.shape[ax] * g.shape[ax + 1],) + g.shape[ax + 2:])

    lora_names = ['b_w1', 'b_a1', 'b_g1', 'b_w2', 'b_a2', 'b_g2']
    lora_pack = jnp.concatenate([bf(w[n]).reshape(-1) for n in lora_names]).reshape(-1, 128)
    full = {'a_w_in': unshard(g_a_in, 1), 'a_w_out': unshard(g_a_out, 0)}
    n_vec_rows = sum(_as2d(w[n]).shape[0] for n in VECTORS)
    vec = jnp.moveaxis(g_vec, 0, 1).reshape(N_VEC, d)
    vec = vec.at[n_vec_rows].set(b_r_k.reshape(d))

    e_ind, et_ind = _head_indicators(d)
    gd = d // SGU_GROUPS
    group_ind = (jnp.arange(SGU_GROUPS)[:, None] == jnp.arange(d)[None, :] // gd).astype(BF16)
    bias_full = jnp.repeat(a_b_s[0].T, gd, axis=1)
    pad_c = lambda a, n: jnp.pad(a, ((0, 0), (0, n - a.shape[1])))
    pad_r = lambda a, n: jnp.pad(a, ((0, n - a.shape[0]), (0, 0)))
    sgu_args = (full['a_w_in'], a_ln_g, a_ln_b, a_w_s[0], bias_full, full['a_w_out'])

    x0 = x
    (x1, mix_a), (g_w1_0, g_w2_0) = _sgu_fwd(x0, mod_mix[0], *sgu_args, gather=(bf(mlp_w1[0]), bf(mlp_w2[0])))
    w1_full = [unshard(g_w1_0, 1), None]
    w2_full = [unshard(g_w2_0, 0), None]
    (x2, ff0), (g_w1_1, g_w2_1, g_b_in, g_b_out, g_lora) = _mlp_fwd(
        x1, mod_mlp[0], w1_full[0], w2_full[0],
        gather=(bf(mlp_w1[1]), bf(mlp_w2[1]), bf(b_w_in[0]), bf(b_w_out[0]), lora_pack))
    w1_full[1], w2_full[1] = unshard(g_w1_1, 1), unshard(g_w2_1, 0)
    full['b_w_in'], full['b_w_out'] = unshard(g_b_in, 1), unshard(g_b_out, 0)
    lora_flat, lo = g_lora.reshape(N_DEV, -1), 0
    for n, ax in zip(lora_names, (0, 0, 0, 1, 1, 1)):
        loc = w[n].shape[1:]
        full[n] = unshard(lora_flat[:, lo:lo + w[n].size].reshape((N_DEV,) + loc), ax)
        lo += w[n].size
    rwkv_w = (full['b_w_in'], pad_c(full['b_w1'], LORA_PAD), pad_c(full['b_a1'], LORA_PAD),
              pad_c(full['b_g1'], GATE_PAD), pad_r(full['b_w2'], LORA_PAD), pad_r(full['b_a2'], LORA_PAD),
              pad_r(full['b_g2'], GATE_PAD))
    r, ld, k2, v, a_s, b_s, gate = _rwkv_pre_fwd(x2, mod_mix[1], vec, e_ind, et_ind, rwkv_w)
    y, s0, tinv = _wkv_fwd(r, ld, k2, v, a_s, b_s)
    x3, mix_b = _rwkv_post_fwd(x2, y, r, k2, v, gate, mod_mix[1], vec, e_ind, et_ind, full['b_w_out'])
    (x4, ff1), _ = _mlp_fwd(x3, mod_mlp[1], w1_full[1], w2_full[1])
    dx4, loss_blk, dfinal = _final(x4, loss_target, final_g.reshape(1, d))
    loss = lax.psum(loss_blk[0, 0], ("x", "y", "c"))

    tok = lambda a: a.reshape(m_tok, a.shape[-1])
    shard_rows = lambda g: g.reshape((N_DEV, g.shape[0] // N_DEV) + g.shape[1:])
    (dx3, dmod_mlp1, h_b, dff_b, q_b, dp_b), _ = _mlp_bwd(x3, dx4, ff1, mod_mlp[1], w1_full[1], w2_full[1])
    gw1_1 = _matmul_tn(tok(h_b), tok(dp_b), "grad_mlp_w1_l1", col_shards=N_DEV)
    gw2_1 = shard_rows(_matmul_tn(tok(q_b), tok(dff_b), "grad_mlp_w2_l1"))
    (dy, dr_p, dk2_p, dv_p, dgate_act, o_b, dmix_b, dgate_b, small_post), (rw1_1, rw2_1) = _rwkv_post_bwd(
        dx3, mix_b, y, r, k2, v, gate, mod_mix[1], vec, e_ind, et_ind, full['b_w_out'], scatter=(gw1_1, gw2_1))
    g_b_w_out = shard_rows(_matmul_tn(tok(o_b), tok(dmix_b), "grad_b_w_out"))
    dr_s, dld, dk2_s, dv_s, das, dbs = _wkv_bwd(r, ld, k2, v, a_s, b_s, s0, tinv, dy)
    (dh, dhp, xr_b, xw_b, xk_b, xv_b, xa_b, xg_b, dr_b, dk_b, dv_b, dtw_b, dta_b, dtg_b, tw2_b, ta_b, sg_b,
     dlw_b, dla_b, dg_b, small_pre) = _rwkv_pre_bwd(x2, mod_mix[1], vec, e_ind, et_ind, rwkv_w,
                                                    dr_p, dk2_p, dv_p, dgate_act, dr_s, dld, dk2_s, dv_s, das, dbs)
    g_b_w_in = jnp.concatenate([_matmul_tn(tok(xr_b), tok(dr_b), "grad_b_w_r"),
                                _matmul_tn(tok(xk_b), tok(dk_b), "grad_b_w_k"),
                                _matmul_tn(tok(xv_b), tok(dv_b), "grad_b_w_v")], axis=1)
    shard_cols = lambda g: jnp.moveaxis(g.reshape(g.shape[0], N_DEV, g.shape[1] // N_DEV), 1, 0)
    g_b_w_in = shard_cols(g_b_w_in)
    lw_, lg_ = b_w1.shape[2], b_g1.shape[2]
    small_names = ['b_w1', 'b_a1', 'b_g1', 'b_w2', 'b_a2', 'b_g2'] + VECTORS
    small_parts = [
        shard_rows(_matmul_tn(tok(xw_b), tok(dtw_b), "grad_b_w1")[:, :lw_]),
        shard_rows(_matmul_tn(tok(xa_b), tok(dta_b), "grad_b_a1")[:, :lw_]),
        shard_rows(_matmul_tn(tok(xg_b), tok(dtg_b), "grad_b_g1")[:, :lg_]),
        shard_cols(_matmul_tn(tok(tw2_b), tok(dlw_b), "grad_b_w2")[:lw_]),
        shard_cols(_matmul_tn(tok(ta_b), tok(dla_b), "grad_b_a2")[:lw_]),
        shard_cols(_matmul_tn(tok(sg_b), tok(dg_b), "grad_b_g2")[:lg_]),
        shard_cols(jnp.concatenate([small_pre[0:10], small_post[0:2]], axis=0).astype(BF16)),
    ]
    small_flat = jnp.concatenate([p.reshape(N_DEV, -1) for p in small_parts], axis=1)
    lane = 128
    small_rows = -(-small_flat.shape[1] // (lane * ROW_ALIGN)) * ROW_ALIGN
    small_pack = jnp.pad(small_flat, ((0, 0), (0, small_rows * lane - small_flat.shape[1]))).reshape(
        N_DEV, small_rows, lane)
    dx2, dmod_mix1 = _norm_bwd(x2, dx3, dh, dhp, mod_mix[1], dgate_b)
    (dx1, dmod_mlp0, h_b, dff_b, q_b, dp_b), (r_b_w_in, r_b_w_out, r_small) = _mlp_bwd(
        x1, dx2, ff0, mod_mlp[0], w1_full[0], w2_full[0], scatter=(g_b_w_in, g_b_w_out, small_pack))
    gw1_0 = _matmul_tn(tok(h_b), tok(dp_b), "grad_mlp_w1_l0", col_shards=N_DEV)
    gw2_0 = shard_rows(_matmul_tn(tok(q_b), tok(dff_b), "grad_mlp_w2_l0"))
    (dx0, dmod_mix0, h_b, dpre_b, z_b, dmix_b, small_sgu, d_ws, d_bs), (rw1_0, rw2_0) = _sgu_bwd(
        x0, dx1, mix_a, mod_mix[0], *sgu_args, group_ind, scatter=(gw1_0, gw2_0))
    dmod_mine = jnp.stack([jnp.concatenate([dmod_mix0[:, 0:3], dmod_mlp0[:, 0:3]], axis=1),
                           jnp.concatenate([dmod_mix1[:, 0:3], dmod_mlp1[:, 0:3]], axis=1)], axis=1)
    rep_g = {'a_ln_g': small_sgu[0:1], 'a_ln_b': small_sgu[1:2], 'a_w_s': d_ws.reshape(-1, d), 'a_b_s': d_bs.reshape(1, d),
             'b_r_k': small_post[2:3], 'final_g': dfinal[0:1]}
    rep_rows = [rep_g[n].shape[0] for n in REPLICATED]
    rep_pack = _pad_rows(jnp.concatenate([rep_g[n] for n in REPLICATED], axis=0), 8)
    g_a_w_in, (dmod_all, rep_all) = _matmul_tn(tok(h_b), tok(dpre_b), "grad_a_w_in", col_shards=N_DEV,
                                               gather=(dmod_mine.reshape(bl, nl * 6 * d), rep_pack))
    g_a_w_out = shard_rows(_matmul_tn(tok(z_b), tok(dmix_b), "grad_a_w_out"))
    r_a_w_in, r_a_w_out = _scatter_call((g_a_w_in, g_a_w_out), "scatter_sgu_grads")

    dmod_all = jnp.moveaxis(dmod_all.reshape(nb, nl, 6 * d), 0, 1)
    dmod_cols = lax.dynamic_slice(dmod_all, (0, 0, me * cols), (nl, nb, cols))
    g_ada_w, g_ada_b = _ada_bwd(c_all, dmod_cols, dmod_all)

    mom = {n: given['m_' + n] for n in WEIGHTS}
    var = {n: given['v_' + n] for n in WEIGHTS}
    out = {}
    as3d = lambda a: a.reshape((-1,) + a.shape[-2:])
    for n, parts in (('mlp_w1', [rw1_0, rw1_1]), ('mlp_w2', [rw2_0, rw2_1]), ('a_w_in', [r_a_w_in]),
                     ('a_w_out', [r_a_w_out]), ('b_w_in', [r_b_w_in]), ('b_w_out', [r_b_w_out]),
                     ('ada_w', list(g_ada_w))):
        res = _adamw_layers(as3d(w[n]), as3d(mom[n]), as3d(var[n]), parts, "adamw_" + n)
        out[n] = tuple(a.reshape(w[n].shape) for a in res)

    items, names = [], []

    def add(n, part):
        s2 = _as2d(w[n]).shape
        items.append((_as2d(w[n]), _as2d(mom[n]), _as2d(var[n]), part.reshape((part.shape[0],) + s2)))
        names.append(n)

    sflat = r_small.reshape(N_DEV, -1)
    so = 0
    for n in small_names:
        sz = w[n].size
        add(n, sflat[:, so:so + sz])
        so += sz
    ro = 0
    for n, nr in zip(REPLICATED, rep_rows):
        add(n, rep_all[:, ro:ro + nr])
        ro += nr
    add('ada_b', g_ada_b[None])
    for n, res in zip(names, _adamw_small(items, "adamw_small")):
        out[n] = tuple(a.reshape(w[n].shape) for a in res)

    return (loss, dx0, *[out[n][0] for n in WEIGHTS], *[out[n][1] for n in WEIGHTS],
            *[out[n][2] for n in WEIGHTS], *[out[n][3] for n in WEIGHTS])
```

```python
import functools

import jax
import jax.numpy as jnp
from jax import lax
from jax.experimental import pallas as pl
from jax.experimental.pallas import tpu as pltpu

F32 = jnp.float32
BF16 = jnp.bfloat16

N_DEV = 8
RMS_EPS = 1e-6
LN_EPS = 1e-5
HEAD = 64
GN_EPS = HEAD * 1e-5
L2_EPS = 1e-12
SGU_CHUNK = 128
SGU_GROUPS = 8
WKV_CHUNK = 64
WKV_HEADS_PER_STEP = 16
WKV_EXAMPLES_PER_STEP = 2
LORA_PAD = 128
GATE_PAD = 256
ADAM_LR, ADAM_B1, ADAM_B2, ADAM_EPS, ADAM_WD, ADAM_STEP = 0.001, 0.9, 0.999, 1e-08, 0.01, 10
VMEM_LIMIT = 56 * 1024 * 1024


def _cparams(sem=None, **kw):
    if sem is not None:
        kw["dimension_semantics"] = sem
    return pltpu.CompilerParams(vmem_limit_bytes=VMEM_LIMIT, **kw)


def _dot(a, b):
    return jnp.dot(a.astype(BF16), b.astype(BF16), preferred_element_type=F32)


def _dot_nt(a, b):
    return lax.dot_general(a.astype(BF16), b.astype(BF16), (((1,), (1,)), ((), ())), preferred_element_type=F32)


def _dot_tn(a, b):
    return lax.dot_general(a.astype(BF16), b.astype(BF16), (((0,), (0,)), ((), ())), preferred_element_type=F32)


def _bdot(a, b, dims):
    return lax.dot_general(a.astype(BF16), b.astype(BF16), (dims, ((0,), (0,))), preferred_element_type=F32)


@jax.custom_vjp
def _tri_sum(tri, tri_t, x):
    hi = x.astype(BF16)
    lo = (x - hi.astype(F32)).astype(BF16)
    dn = (((2,), (1,)), ((0,), (0,)))
    return (lax.dot_general(tri, hi, dn, preferred_element_type=F32)
            + lax.dot_general(tri, lo, dn, preferred_element_type=F32))


_tri_sum.defvjp(lambda tri, tri_t, x: (_tri_sum(tri, tri_t, x), (tri, tri_t)),
                lambda res, g: (jnp.zeros_like(res[0]), jnp.zeros_like(res[1]), _tri_sum(res[1], res[0], g)))


@jax.custom_vjp
def _bmm_nn(a, b):
    return _bdot(a, b, ((2,), (1,)))


@jax.custom_vjp
def _bmm_nt(a, b):
    return _bdot(a, b, ((2,), (2,)))


@jax.custom_vjp
def _bmm_tn(a, b):
    return _bdot(a, b, ((1,), (1,)))


_bmm_nn.defvjp(lambda a, b: (_bmm_nn(a, b), (a, b)), lambda res, g: (_bmm_nt(g, res[1]), _bmm_tn(res[0], g)))
_bmm_nt.defvjp(lambda a, b: (_bmm_nt(a, b), (a, b)), lambda res, g: (_bmm_nn(g, res[1]), _bmm_tn(g, res[0])))
_bmm_tn.defvjp(lambda a, b: (_bmm_tn(a, b), (a, b)), lambda res, g: (_bmm_nt(res[1], g), _bmm_nn(res[0], g)))


def _tri_inverse(p):
    n = p.shape[1]
    row = lax.broadcasted_iota(jnp.int32, (n, n), 0)
    col = lax.broadcasted_iota(jnp.int32, (n, n), 1)
    tinv = jnp.where(row == col, 1.0, 0.0).astype(F32)[None] + p
    for _ in range(max(1, (n - 1).bit_length()) - 1):
        p = _bmm_nn(p, p)
        tinv = tinv + _bmm_nn(tinv, p)
    return tinv.astype(BF16)


def _tri_solve_fwd(tinv, p, rhs):
    u = _bmm_nn(tinv, rhs)
    return u, (tinv, u)


def _tri_solve_bwd(res, du):
    tinv, u = res
    drhs = _bmm_tn(tinv, du)
    return jnp.zeros_like(tinv), _bmm_nt(drhs, u), drhs


@jax.custom_vjp
def _tri_solve(tinv, p, rhs):
    return _tri_solve_fwd(tinv, p, rhs)[0]


_tri_solve.defvjp(_tri_solve_fwd, _tri_solve_bwd)


def _wkv_chunk(s0, r, ld, k, v, a, b, tinv=None):
    nh, n, _ = r.shape
    row = lax.broadcasted_iota(jnp.int32, (n, n), 0)
    col = lax.broadcasted_iota(jnp.int32, (n, n), 1)
    incl = row >= col
    strict = row > col
    lower = jnp.broadcast_to(jnp.where(incl, 1.0, 0.0).astype(BF16), (nh, n, n))
    upper = jnp.broadcast_to(jnp.where(row <= col, 1.0, 0.0).astype(BF16), (nh, n, n))
    c = _tri_sum(lower, upper, ld)
    c_end = c[:, n - 1:n, :]
    ec, enc, ecx, eend = jnp.exp(c), jnp.exp(-c), jnp.exp(c - ld), jnp.exp(c_end - c)
    ar = jnp.concatenate([a * ecx, r * ec], axis=1)
    mask = jnp.concatenate([strict, incl], axis=0)[None]
    m_b = jnp.where(mask, _bmm_nt(ar, b * enc), 0.0)
    m_k = jnp.where(mask, _bmm_nt(ar, k * enc), 0.0)
    a_ab, a_rb = m_b[:, :n], m_b[:, n:]
    base = _bmm_nt(ar, s0) + _bmm_nn(m_k, v)
    if tinv is None:
        tinv = lax.stop_gradient(_tri_inverse(a_ab))
    u = _tri_solve(tinv, a_ab, base[:, :n])
    y = base[:, n:] + _bmm_nn(a_rb, u)
    s1 = s0 * jnp.exp(c_end) + _bmm_tn(jnp.concatenate([u, v], axis=1), jnp.concatenate([b * eend, k * eend], axis=1))
    return y, s1, tinv


def _wkv_specs(bl, nh, t):
    eb, hb, lc = min(bl, WKV_EXAMPLES_PER_STEP), min(nh, WKV_HEADS_PER_STEP), WKV_CHUNK
    return eb, hb, lc, (bl // eb, nh // hb, t // lc)


def _wkv_fwd(r, ld, k, v, a, b):
    bl, nh, t, n = r.shape
    eb, hb, lc, grid = _wkv_specs(bl, nh, t)
    nc = t // lc
    nb = eb * hb

    def body(r_ref, ld_ref, k_ref, v_ref, a_ref, b_ref, y_ref, s0_ref, tinv_ref, s_scr):
        @pl.when(pl.program_id(2) == 0)
        def _():
            s_scr[...] = jnp.zeros_like(s_scr)

        s0 = s_scr[...]
        s0_ref[:, :, 0] = s0.reshape(eb, hb, n, n)
        y, s1, tinv = _wkv_chunk(
            s0, *(ref[...].reshape(nb, lc, n) for ref in (r_ref, ld_ref, k_ref, v_ref, a_ref, b_ref)))
        y_ref[...] = y.reshape(eb, hb, lc, n)
        tinv_ref[:, :, 0] = tinv.reshape(eb, hb, lc, lc)
        s_scr[...] = s1

    seq = pl.BlockSpec((eb, hb, lc, n), lambda e, h, c: (e, h, c, 0))
    return pl.pallas_call(
        body, name="wkv_fwd", grid=grid,
        in_specs=[seq] * 6,
        out_specs=(seq, pl.BlockSpec((eb, hb, 1, n, n), lambda e, h, c: (e, h, c, 0, 0)),
                   pl.BlockSpec((eb, hb, 1, lc, lc), lambda e, h, c: (e, h, c, 0, 0))),
        out_shape=(jax.ShapeDtypeStruct((bl, nh, t, n), F32), jax.ShapeDtypeStruct((bl, nh, nc, n, n), F32),
                   jax.ShapeDtypeStruct((bl, nh, nc, lc, lc), BF16)),
        scratch_shapes=[pltpu.VMEM((nb, n, n), F32)],
        compiler_params=_cparams(("arbitrary", "arbitrary", "arbitrary")),
    )(r, ld, k, v, a, b)


def _wkv_bwd(r, ld, k, v, a, b, s0_all, tinv_all, dy):
    bl, nh, t, n = r.shape
    eb, hb, lc, grid = _wkv_specs(bl, nh, t)
    nc = t // lc
    nb = eb * hb

    def body(r_ref, ld_ref, k_ref, v_ref, a_ref, b_ref, s0_ref, tinv_ref, dy_ref,
             dr_ref, dld_ref, dk_ref, dv_ref, da_ref, db_ref, ds_scr):
        @pl.when(pl.program_id(2) == 0)
        def _():
            ds_scr[...] = jnp.zeros_like(ds_scr)

        args = (s0_ref[:, :, 0].reshape(nb, n, n),) + tuple(
            ref[...].reshape(nb, lc, n) for ref in (r_ref, ld_ref, k_ref, v_ref, a_ref, b_ref))
        tinv = tinv_ref[:, :, 0].reshape(nb, lc, lc)
        _, vjp = jax.vjp(lambda *xs: _wkv_chunk(*xs, tinv=tinv)[:2], *args)
        ds0, *dseq = vjp((dy_ref[...].reshape(nb, lc, n), ds_scr[...]))
        ds_scr[...] = ds0
        for ref, val in zip((dr_ref, dld_ref, dk_ref, dv_ref, da_ref, db_ref), dseq):
            ref[...] = val.reshape(eb, hb, lc, n)

    seq = pl.BlockSpec((eb, hb, lc, n), lambda e, h, c: (e, h, nc - 1 - c, 0))
    st = pl.BlockSpec((eb, hb, 1, n, n), lambda e, h, c: (e, h, nc - 1 - c, 0, 0))
    ti = pl.BlockSpec((eb, hb, 1, lc, lc), lambda e, h, c: (e, h, nc - 1 - c, 0, 0))
    out = jax.ShapeDtypeStruct((bl, nh, t, n), F32)
    return pl.pallas_call(
        body, name="wkv_bwd", grid=grid,
        in_specs=[seq] * 6 + [st, ti, seq],
        out_specs=(seq,) * 6, out_shape=(out,) * 6,
        scratch_shapes=[pltpu.VMEM((nb, n, n), F32)],
        compiler_params=_cparams(("arbitrary", "arbitrary", "arbitrary")),
    )(r, ld, k, v, a, b, s0_all, tinv_all, dy)


def _scatter_copies(x_refs, o_refs, send_sems, recv_sems, local_sems):
    pos = (lax.axis_index("x"), lax.axis_index("y"), lax.axis_index("c"))
    me = 4 * pos[0] + 2 * pos[1] + pos[2]

    def descriptors():
        sends, arrivals, local = [], [], []
        for i, (x_ref, o_ref) in enumerate(zip(x_refs, o_refs)):
            for m in range(1, N_DEV):
                p = tuple(1 - pos[a] if (m >> (2 - a)) & 1 else pos[a] for a in range(3))
                pidx = 4 * p[0] + 2 * p[1] + p[2]
                k = (N_DEV - 1) * i + m - 1
                for dst, out in ((o_ref.at[me], sends), (o_ref.at[pidx], arrivals)):
                    out.append(pltpu.make_async_remote_copy(
                        src_ref=x_ref.at[pidx], dst_ref=dst, send_sem=send_sems.at[k], recv_sem=recv_sems.at[k],
                        device_id=p, device_id_type=pl.DeviceIdType.MESH))
            local.append(pltpu.make_async_copy(x_ref.at[me], o_ref.at[me], local_sems.at[i]))
        return sends, arrivals, local

    def start():
        sends, _, local = descriptors()
        for cp in local + sends:
            cp.start()

    def finish():
        sends, arrivals, local = descriptors()
        for cp in arrivals:
            cp.wait_recv()
        for cp in sends:
            cp.wait_send()
        for cp in local:
            cp.wait()

    return start, finish


def _gather_copies(x_refs, o_refs, send_sems, recv_sems, local_sems):
    pos = (lax.axis_index("x"), lax.axis_index("y"), lax.axis_index("c"))
    me = 4 * pos[0] + 2 * pos[1] + pos[2]
    far = (2, 4, 6)

    def peer_of(m):
        p = tuple(1 - pos[a] if (m >> (2 - a)) & 1 else pos[a] for a in range(3))
        return p, 4 * p[0] + 2 * p[1] + p[2]

    sibling, _ = peer_of(1)

    def copy(i, k, src, slot, to):
        return pltpu.make_async_remote_copy(
            src_ref=src, dst_ref=o_refs[i].at[slot], send_sem=send_sems.at[(N_DEV - 1) * i + k],
            recv_sem=recv_sems.at[(N_DEV - 1) * i + k], device_id=to, device_id_type=pl.DeviceIdType.MESH)

    def direct(i):
        return [copy(i, m - 1, x_refs[i], me, peer_of(m)[0]) for m in (1,) + far]

    def local(i):
        return pltpu.make_async_copy(x_refs[i], o_refs[i].at[me], local_sems.at[i])

    def start():
        for i in range(len(x_refs)):
            local(i).start()
            for cp in direct(i):
                cp.start()

    def finish():
        n = len(x_refs)
        relays = []
        for i in range(n):
            for m in far:
                origin = peer_of(m)[1]
                copy(i, m - 1, x_refs[i], origin, sibling).wait_recv()
                fwd = copy(i, m, o_refs[i].at[origin], origin, sibling)
                fwd.start()
                relays.append(fwd)
        for i in range(n):
            copy(i, 0, x_refs[i], peer_of(1)[1], sibling).wait_recv()
            for m in far:
                copy(i, m, x_refs[i], peer_of(m ^ 1)[1], sibling).wait_recv()
        for i in range(n):
            for cp in direct(i):
                cp.wait_send()
            local(i).wait()
        for cp in relays:
            cp.wait_send()

    return start, finish


def _scatter_scratch(n):
    return [pltpu.SemaphoreType.DMA(((N_DEV - 1) * n,)), pltpu.SemaphoreType.DMA(((N_DEV - 1) * n,)),
            pltpu.SemaphoreType.DMA((n,))]


_ANY = pl.BlockSpec(memory_space=pl.ANY)


def _scatter_call(arrays, name):
    n = len(arrays)

    def body(*refs):
        start, finish = _scatter_copies(refs[:n], refs[n:2 * n], *refs[2 * n:])
        start()
        finish()

    return pl.pallas_call(
        body, name=name, in_specs=[_ANY] * n, out_specs=(_ANY,) * n,
        out_shape=tuple(_sds(a.shape, a.dtype) for a in arrays), scratch_shapes=_scatter_scratch(n),
    )(*arrays)


def _gather_call(arrays, name):
    n = len(arrays)

    def body(*refs):
        start, finish = _gather_copies(refs[:n], refs[n:2 * n], *refs[2 * n:])
        start()
        finish()

    return pl.pallas_call(
        body, name=name, in_specs=[_ANY] * n, out_specs=(_ANY,) * n,
        out_shape=tuple(_sds((N_DEV,) + a.shape, a.dtype) for a in arrays), scratch_shapes=_scatter_scratch(n),
    )(*arrays)


def _call_with_scatter(body, *, name, grid, in_specs, out_specs, out_shape, scratch_shapes, operands,
                       scatter=(), gather=()):
    assert not (scatter and gather)
    carried = tuple(scatter) or tuple(gather)
    copies = _scatter_copies if scatter else _gather_copies
    recv_shapes = tuple(_sds(a.shape if scatter else (N_DEV,) + a.shape, a.dtype) for a in carried)
    nc, n_in, n_out, n_scr = len(carried), len(in_specs), len(out_specs), len(scratch_shapes)
    if nc == 0:
        return pl.pallas_call(
            body, name=name, grid=grid, in_specs=list(in_specs), out_specs=tuple(out_specs),
            out_shape=tuple(out_shape), scratch_shapes=list(scratch_shapes),
            compiler_params=_cparams(("arbitrary",) * len(grid)))(*operands), ()

    def wrapped(*refs):
        ins, refs = refs[:n_in], refs[n_in:]
        c_in, refs = refs[:nc], refs[nc:]
        outs, refs = refs[:n_out], refs[n_out:]
        c_out, refs = refs[:nc], refs[nc:]
        scr, sems = refs[:n_scr], refs[n_scr:]
        ids = [pl.program_id(a) for a in range(len(grid))]
        first = functools.reduce(jnp.logical_and, [i == 0 for i in ids])
        last = functools.reduce(jnp.logical_and, [i == g - 1 for i, g in zip(ids, grid)])
        start, finish = copies(c_in, c_out, *sems)
        pl.when(first)(start)
        body(*ins, *outs, *scr)
        pl.when(last)(finish)

    res = pl.pallas_call(
        wrapped, name=name, grid=grid,
        in_specs=list(in_specs) + [_ANY] * nc, out_specs=tuple(out_specs) + (_ANY,) * nc,
        out_shape=tuple(out_shape) + recv_shapes,
        scratch_shapes=list(scratch_shapes) + _scatter_scratch(nc),
        compiler_params=_cparams(("arbitrary",) * len(grid)),
    )(*operands, *carried)
    return res[:n_out], res[n_out:]


def _rms(x):
    inv = lax.rsqrt(jnp.mean(x * x, axis=-1, keepdims=True) + RMS_EPS)
    return x * inv, inv


def _rms_bwd(xn, inv, dxn):
    return inv * (dxn - xn * jnp.mean(dxn * xn, axis=-1, keepdims=True))


def _colsum(x):
    return jnp.sum(x, axis=0, keepdims=True)


def _sigmoid(x):
    return 0.5 * (jnp.tanh(0.5 * x) + 1.0)


def _split_bf16(x):
    hi = x.astype(BF16)
    return hi, (x - hi.astype(F32)).astype(BF16)


def _dot_split(x, e):
    hi, lo = _split_bf16(x)
    return jnp.dot(hi, e, preferred_element_type=F32) + jnp.dot(lo, e, preferred_element_type=F32)


@jax.custom_vjp
def _headsum(x, e, et):
    return _dot_split(_dot_split(x, e), et)


_headsum.defvjp(lambda x, e, et: (_headsum(x, e, et), (e, et)),
                lambda res, g: (_headsum(g, *res), jnp.zeros_like(res[0]), jnp.zeros_like(res[1])))


def _make_headsum(e, et):
    return lambda x: _headsum(x, e, et)


def _head_indicators(d):
    e = (jnp.arange(d)[:, None] // HEAD == jnp.arange(128)[None, :]).astype(BF16)
    return e, e.T


def _rwkv_elem(r, k, lw, la, w0, a0, k_k, k_a, headsum):
    z = w0 + lw
    w_log = -(jnp.maximum(-z, 0.0) + jnp.log(1.0 + jnp.exp(-jnp.abs(z)))) - 0.5
    ld = -jnp.exp(w_log)
    a = _sigmoid(a0 + la)
    kkp = k * k_k
    kk = kkp * lax.rsqrt(jnp.maximum(headsum(kkp * kkp), L2_EPS * L2_EPS))
    k2 = k * (1.0 + (a - 1.0) * k_a)
    del r
    return ld, k2, -kk, kk * a


def _rwkv_post(y, r, k2, v, g, ln_g, ln_b, r_k, headsum):
    m = headsum(y) * (1.0 / HEAD)
    yc = y - m
    var = headsum(yc * yc) * (1.0 / HEAD)
    yn = yc * lax.rsqrt(var + GN_EPS)
    bonus = headsum(r * k2 * r_k) * v
    return (yn * ln_g + ln_b + bonus) * g


def _shift_down(h, first_row):
    rolled = pltpu.roll(h, 1, 0)
    row = lax.broadcasted_iota(jnp.int32, h.shape, 0)
    return jnp.where(row == 0, first_row, rolled)


def _shift_up(h, last_row):
    n = h.shape[0]
    rolled = pltpu.roll(h, n - 1, 0)
    row = lax.broadcasted_iota(jnp.int32, h.shape, 0)
    return jnp.where(row == n - 1, last_row, rolled)


def _gelu(p):
    return 0.5 * p * (1.0 + lax.erf(p * 0.7071067811865476))


def _gelu_grad(p):
    return 0.5 * (1.0 + lax.erf(p * 0.7071067811865476)) + p * jnp.exp(-0.5 * p * p) * 0.3989422804014327


def _tok(tm, d):
    return pl.BlockSpec((1, tm, d), lambda e, t, *_: (e, t, 0))


def _per_example(rows, d):
    return pl.BlockSpec((1, rows, d), lambda e, t, *_: (e, 0, 0))


def _whole(shape):
    nd = len(shape)
    return pl.BlockSpec(tuple(shape), lambda *_: (0,) * nd)


def _heads(nh, tm):
    return pl.BlockSpec((1, nh, tm, HEAD), lambda e, t, *_: (e, 0, t, 0))


def _sds(shape, dtype=F32):
    return jax.ShapeDtypeStruct(tuple(shape), dtype)


def _add_rows(ref, first, rows):
    @pl.when(first)
    def _():
        ref[0] = jnp.zeros(ref.shape[1:], ref.dtype)

    for i, r in enumerate(rows):
        ref[0, i:i + 1] += r


def _first(e, t):
    return jnp.logical_and(e == 0, t == 0)


def _ada_fwd(c_all, ada_w, ada_b_cols):
    nl, d, cols = ada_w.shape
    nb = c_all.shape[0]

    def body(c_ref, w_ref, b_ref, o_ref):
        c = c_ref[...]
        cond = c * _sigmoid(c)
        for i in range(nl):
            o_ref[i] = _dot(cond, w_ref[i]) + b_ref[i]

    return pl.pallas_call(
        body, name="ada_fwd", out_shape=_sds((nl, nb, cols)),
        compiler_params=_cparams(),
    )(c_all, ada_w, ada_b_cols)


def _ada_bwd(c_all, dmod_cols, dmod_full):
    nl, nb, cols = dmod_cols.shape
    d = c_all.shape[1]

    def body(c_ref, g_ref, f_ref, b_ref, *o_refs):
        c = c_ref[...]
        cond = c * _sigmoid(c)
        for i in range(nl):
            o_refs[i][0] = _dot_tn(cond, g_ref[i])
            b_ref[i:i + 1] = jnp.sum(f_ref[i], axis=0, keepdims=True)

    res = pl.pallas_call(
        body, name="ada_bwd", out_shape=(_sds((nl, dmod_full.shape[2])),) + (_sds((1, d, cols)),) * nl,
        compiler_params=_cparams(),
    )(c_all, dmod_cols, dmod_full)
    return res[1:], res[0]


def _matmul_tn(a, b, name, col_shards=None, gather=()):
    m, ka = a.shape
    n = b.shape[1]
    tm = min(m, 2048)
    tk = min(ka, 1024)
    tn = min(n, 1024)
    steps = m // tm
    if col_shards:
        cs = n // col_shards
        spt = tn // cs
        out_spec = pl.BlockSpec((spt, tk, cs), lambda i, j, s: (j, i, 0))
        out_shape = _sds((col_shards, ka, cs), BF16)
    else:
        out_spec = pl.BlockSpec((tk, tn), lambda i, j, s: (i, j))
        out_shape = _sds((ka, n), BF16)

    def body(a_ref, b_ref, o_ref, acc):
        s = pl.program_id(2)

        @pl.when(s == 0)
        def _():
            acc[...] = jnp.zeros_like(acc)

        acc[...] += _dot_tn(a_ref[...], b_ref[...])

        @pl.when(s == steps - 1)
        def _():
            if col_shards:
                for q in range(spt):
                    o_ref[q] = acc[:, q * cs:(q + 1) * cs].astype(BF16)
            else:
                o_ref[...] = acc[...].astype(BF16)

    res, got = _call_with_scatter(
        body, name=name, grid=(ka // tk, n // tn, steps),
        in_specs=[pl.BlockSpec((tm, tk), lambda i, j, s: (s, i)), pl.BlockSpec((tm, tn), lambda i, j, s: (s, j))],
        out_specs=(out_spec,), out_shape=(out_shape,),
        scratch_shapes=[pltpu.VMEM((tk, tn), F32)], operands=(a, b), gather=gather)
    return (res[0], got) if gather else res[0]


MLP_FWD_TM = 1024
MLP_FJ = 1024
MLP_BWD_TM = 512
MLP_BWD_FJ = 1024


def _mlp_fwd(x, mod, w1, w2, gather=()):
    bl, t, d = x.shape
    f = w1.shape[1]
    tm, fj = min(t, MLP_FWD_TM), min(f, MLP_FJ)
    nj = f // fj

    def body(x_ref, mod_ref, w1_ref, w2_ref, xo_ref, ff_ref, q_ref, h_scr, acc):
        j = pl.program_id(2)

        @pl.when(j == 0)
        def _():
            xn, _ = _rms(x_ref[0])
            h_scr[...] = (xn * (1.0 + mod_ref[0, 1:2]) + mod_ref[0, 0:1]).astype(BF16)
            acc[...] = jnp.zeros_like(acc)

        p = jnp.dot(h_scr[...], w1_ref[...], preferred_element_type=F32)
        q = jnp.square(jnp.maximum(p, 0.0)).astype(BF16)
        q_ref[0] = q
        acc[...] += jnp.dot(q, w2_ref[...], preferred_element_type=F32)

        @pl.when(j == nj - 1)
        def _():
            ff_ref[0] = acc[...]
            xo_ref[0] = x_ref[0] + mod_ref[0, 2:3] * acc[...]

    return _call_with_scatter(
        body, name="mlp_fwd", grid=(bl, t // tm, nj),
        in_specs=[_tok(tm, d), _per_example(8, d),
                  pl.BlockSpec((d, fj), lambda e, i, j: (0, j)), pl.BlockSpec((fj, d), lambda e, i, j: (j, 0))],
        out_specs=(_tok(tm, d), _tok(tm, d), pl.BlockSpec((1, tm, fj), lambda e, i, j: (e, i, j))),
        out_shape=(_sds(x.shape), _sds(x.shape), _sds((bl, t, f), BF16)),
        scratch_shapes=[pltpu.VMEM((tm, d), BF16), pltpu.VMEM((tm, d), F32)],
        operands=(x, mod, w1, w2), gather=gather)


def _mlp_bwd(x, dxo, ff, q, mod, w1, w2, scatter=()):
    bl, t, d = x.shape
    f = w1.shape[1]
    tm, fj = min(t, MLP_BWD_TM), min(f, MLP_BWD_FJ)
    nj = f // fj

    def body(x_ref, dxo_ref, ff_ref, q_ref, mod_ref, w1_ref, w2_ref,
             dx_ref, dmod_ref, h_ref, dff_ref, dp_ref, acc):
        ti, j = pl.program_id(1), pl.program_id(2)

        @pl.when(j == 0)
        def _():
            xn, _ = _rms(x_ref[0])
            h_ref[0] = (xn * (1.0 + mod_ref[0, 1:2]) + mod_ref[0, 0:1]).astype(BF16)
            dff_ref[0] = (mod_ref[0, 2:3] * dxo_ref[0]).astype(BF16)
            acc[...] = jnp.zeros_like(acc)

        rl = jnp.sqrt(q_ref[0].astype(F32))
        dp = (_dot_nt(dff_ref[0], w2_ref[...]) * (2.0 * rl)).astype(BF16)
        dp_ref[0] = dp
        acc[...] += _dot_nt(dp, w1_ref[...])

        @pl.when(j == nj - 1)
        def _():
            xn, inv = _rms(x_ref[0])
            dh = acc[...]
            dx_ref[0] = dxo_ref[0] + _rms_bwd(xn, inv, dh * (1.0 + mod_ref[0, 1:2]))
            _add_rows(dmod_ref, ti == 0, [_colsum(dh), _colsum(dh * xn), _colsum(dxo_ref[0] * ff_ref[0])])

    big = lambda: pl.BlockSpec((1, tm, fj), lambda e, i, j: (e, i, j))
    return _call_with_scatter(
        body, name="mlp_bwd", grid=(bl, t // tm, nj),
        in_specs=[_tok(tm, d), _tok(tm, d), _tok(tm, d), big(), _per_example(8, d),
                  pl.BlockSpec((d, fj), lambda e, i, j: (0, j)), pl.BlockSpec((fj, d), lambda e, i, j: (j, 0))],
        out_specs=(_tok(tm, d), _per_example(8, d), _tok(tm, d), _tok(tm, d), big()),
        out_shape=(_sds(x.shape), _sds((bl, 8, d)), _sds(x.shape, BF16), _sds(x.shape, BF16),
                   _sds((bl, t, f), BF16)),
        scratch_shapes=[pltpu.VMEM((tm, d), F32)],
        operands=(x, dxo, ff, q, mod, w1, w2), scatter=scatter)


SGU_TM = 256


def _sgu_core(x, mod_ref, win_ref, lng, lnb, ws_ref, bias_ref):
    tm, d = x.shape
    xn, inv = _rms(x)
    h = (xn * (1.0 + mod_ref[0, 1:2]) + mod_ref[0, 0:1]).astype(BF16)
    pre = jnp.dot(h, win_ref[...], preferred_element_type=F32)
    uv = _gelu(pre)
    u, v = uv[:, :d], uv[:, d:]
    mu = jnp.mean(v, axis=-1, keepdims=True)
    vc = v - mu
    rstd = lax.rsqrt(jnp.mean(vc * vc, axis=-1, keepdims=True) + LN_EPS)
    vhat = vc * rstd
    vln = vhat * lng + lnb
    gd = d // SGU_GROUPS
    rows = []
    for c in range(tm // SGU_CHUNK):
        cols = []
        for g in range(SGU_GROUPS):
            cols.append(_dot(ws_ref[g], vln[c * SGU_CHUNK:(c + 1) * SGU_CHUNK, g * gd:(g + 1) * gd]))
        rows.append(jnp.concatenate(cols, axis=1) + bias_ref[...])
    sv = jnp.concatenate(rows, axis=0)
    return xn, inv, h, pre, u, vhat, rstd, vln, sv


def _sgu_masked(ws_ref, wm_scr):
    row = lax.broadcasted_iota(jnp.int32, (SGU_CHUNK, SGU_CHUNK), 0)
    col = lax.broadcasted_iota(jnp.int32, (SGU_CHUNK, SGU_CHUNK), 1)
    for g in range(SGU_GROUPS):
        wm_scr[g] = jnp.where(row >= col, ws_ref[g], 0.0).astype(BF16)


def _sgu_fwd(x, mod, w_in, ln_g, ln_b, w_s, bias_full, w_out, gather=()):
    bl, t, d = x.shape
    tm = min(t, SGU_TM)

    def body(x_ref, mod_ref, win_ref, lng_ref, lnb_ref, ws_ref, bias_ref, wout_ref, xo_ref, mix_ref, wm_scr):
        _sgu_masked(ws_ref, wm_scr)
        xt = x_ref[0]
        *_, u, _, _, _, sv = _sgu_core(xt, mod_ref, win_ref, lng_ref[...], lnb_ref[...], wm_scr, bias_ref)
        mix = _dot(u * sv, wout_ref[...])
        mix_ref[0] = mix
        xo_ref[0] = xt + mod_ref[0, 2:3] * mix

    return _call_with_scatter(
        body, name="sgu_fwd", grid=(bl, t // tm),
        in_specs=[_tok(tm, d), _per_example(8, d), _whole(w_in.shape), _whole(ln_g.shape), _whole(ln_b.shape),
                  _whole(w_s.shape), _whole(bias_full.shape), _whole(w_out.shape)],
        out_specs=(_tok(tm, d), _tok(tm, d)),
        out_shape=(_sds(x.shape), _sds(x.shape)),
        scratch_shapes=[pltpu.VMEM(w_s.shape, BF16)],
        operands=(x, mod, w_in, ln_g, ln_b, w_s, bias_full, w_out), gather=gather)


def _sgu_bwd(x, dxo, mix, mod, w_in, ln_g, ln_b, w_s, bias_full, w_out, group_ind, scatter=()):
    bl, t, d = x.shape
    tm = min(t, SGU_TM)
    gd = d // SGU_GROUPS

    def body(x_ref, dxo_ref, mix_ref, mod_ref, win_ref, lng_ref, lnb_ref, ws_ref, bias_ref, wout_ref, ind_ref,
             dx_ref, dmod_ref, h_ref, dpre_ref, z_ref, dmix_ref, small_ref, dws_ref, dbs_ref, wm_scr, dbias_scr):
        e, ti = pl.program_id(0), pl.program_id(1)
        _sgu_masked(ws_ref, wm_scr)
        xt, dxo = x_ref[0], dxo_ref[0]
        lng = lng_ref[...]
        xn, inv, h, pre, u, vhat, rstd, vln, sv = _sgu_core(xt, mod_ref, win_ref, lng, lnb_ref[...], wm_scr, bias_ref)
        h_ref[0] = h
        z_ref[0] = (u * sv).astype(BF16)
        dmix = mod_ref[0, 2:3] * dxo
        dmix_ref[0] = dmix.astype(BF16)
        dz = _dot_nt(dmix, wout_ref[...])
        du, dsv = dz * sv, dz * u

        @pl.when(_first(e, ti))
        def _():
            dws_ref[...] = jnp.zeros_like(dws_ref)
            dbias_scr[...] = jnp.zeros_like(dbias_scr)
            small_ref[...] = jnp.zeros_like(small_ref)

        row = lax.broadcasted_iota(jnp.int32, (SGU_CHUNK, SGU_CHUNK), 0)
        col = lax.broadcasted_iota(jnp.int32, (SGU_CHUNK, SGU_CHUNK), 1)
        rows = []
        for c in range(tm // SGU_CHUNK):
            rs = slice(c * SGU_CHUNK, (c + 1) * SGU_CHUNK)
            dbias_scr[...] += dsv[rs]
            cols = []
            for g in range(SGU_GROUPS):
                cs = slice(g * gd, (g + 1) * gd)
                cols.append(_dot_tn(wm_scr[g], dsv[rs, cs]))
                dws_ref[g] += jnp.where(row >= col, _dot_nt(dsv[rs, cs], vln[rs, cs]), 0.0)
            rows.append(jnp.concatenate(cols, axis=1))
        dvln = jnp.concatenate(rows, axis=0)
        small_ref[0:1] += _colsum(dvln * vhat)
        small_ref[1:2] += _colsum(dvln)
        dvhat = dvln * lng
        dv = rstd * (dvhat - jnp.mean(dvhat, axis=-1, keepdims=True)
                     - vhat * jnp.mean(dvhat * vhat, axis=-1, keepdims=True))
        dpre = (jnp.concatenate([du, dv], axis=1) * _gelu_grad(pre)).astype(BF16)
        dpre_ref[0] = dpre
        dh = _dot_nt(dpre, win_ref[...])
        dx_ref[0] = dxo + _rms_bwd(xn, inv, dh * (1.0 + mod_ref[0, 1:2]))
        _add_rows(dmod_ref, ti == 0, [_colsum(dh), _colsum(dh * xn), _colsum(dxo * mix_ref[0])])

        @pl.when(jnp.logical_and(e == bl - 1, ti == t // tm - 1))
        def _():
            hi, lo = _split_bf16(dbias_scr[...])
            ind = ind_ref[...]
            dbs_ref[...] = (lax.dot_general(ind, hi, (((1,), (1,)), ((), ())), preferred_element_type=F32)
                            + lax.dot_general(ind, lo, (((1,), (1,)), ((), ())), preferred_element_type=F32))

    return _call_with_scatter(
        body, name="sgu_bwd", grid=(bl, t // tm),
        in_specs=[_tok(tm, d), _tok(tm, d), _tok(tm, d), _per_example(8, d), _whole(w_in.shape), _whole(ln_g.shape),
                  _whole(ln_b.shape), _whole(w_s.shape), _whole(bias_full.shape), _whole(w_out.shape),
                  _whole(group_ind.shape)],
        out_specs=(_tok(tm, d), _per_example(8, d), _tok(tm, d), _tok(tm, 2 * d), _tok(tm, d), _tok(tm, d),
                   _whole((8, d)), _whole(w_s.shape), _whole((SGU_GROUPS, SGU_CHUNK))),
        out_shape=(_sds(x.shape), _sds((bl, 8, d)), _sds(x.shape, BF16), _sds((bl, t, 2 * d), BF16),
                   _sds(x.shape, BF16), _sds(x.shape, BF16), _sds((8, d)), _sds(w_s.shape),
                   _sds((SGU_GROUPS, SGU_CHUNK))),
        scratch_shapes=[pltpu.VMEM(w_s.shape, BF16), pltpu.VMEM((SGU_CHUNK, d), F32)],
        operands=(x, dxo, mix, mod, w_in, ln_g, ln_b, w_s, bias_full, w_out, group_ind), scatter=scatter)


RWKV_TM = 256
N_VEC = 16


def _rwkv_pre_core(x_ref, halo_ref, mod_ref, vec_ref, ti):
    xn, inv = _rms(x_ref[0])
    scale1, shift = 1.0 + mod_ref[0, 1:2], mod_ref[0, 0:1]
    h = xn * scale1 + shift
    hn, _ = _rms(halo_ref[0])
    hh = hn * scale1 + shift
    first = jnp.where(ti == 0, 0.0, hh[7:8])
    xx = _shift_down(h, first) - h
    xs = [h + xx * vec_ref[i:i + 1] for i in range(6)]
    return xn, inv, xx, xs


def _rwkv_proj(xs, wrkv_ref, w1_ref, a1_ref, g1_ref, w2_ref, a2_ref, g2_ref):
    d = xs[0].shape[1]
    xr, xw, xk, xv, xa, xg = [z.astype(BF16) for z in xs]
    r = jnp.dot(xr, wrkv_ref[:, 0:d], preferred_element_type=F32)
    k = jnp.dot(xk, wrkv_ref[:, d:2 * d], preferred_element_type=F32)
    v = jnp.dot(xv, wrkv_ref[:, 2 * d:3 * d], preferred_element_type=F32)
    tw2 = jnp.tanh(jnp.dot(xw, w1_ref[...], preferred_element_type=F32))
    ta = jnp.dot(xa, a1_ref[...], preferred_element_type=F32)
    sg = _sigmoid(jnp.dot(xg, g1_ref[...], preferred_element_type=F32))
    lw, la, g = _dot(tw2, w2_ref[...]), _dot(ta, a2_ref[...]), _dot(sg, g2_ref[...])
    return (xr, xw, xk, xv, xa, xg), r, k, v, tw2, ta, sg, lw, la, g


def _to_heads(ref, val, nh):
    for hd in range(nh):
        ref[0, hd] = val[:, hd * HEAD:(hd + 1) * HEAD]


def _from_heads(ref, scr, nh):
    for hd in range(nh):
        scr[:, hd * HEAD:(hd + 1) * HEAD] = ref[0, hd]
    return scr[...]


def _rwkv_weight_specs(ws):
    return [_whole(w.shape) for w in ws]


def _rwkv_pre_fwd(x, mod, vec, e_ind, et_ind, weights):
    bl, t, d = x.shape
    tm = min(t, RWKV_TM)
    nh = d // HEAD
    hb = tm // 8

    def body(x_ref, halo_ref, mod_ref, vec_ref, e_ref, et_ref, wrkv, w1, a1, g1, w2, a2, g2,
             r_ref, ld_ref, k2_ref, v_ref, as_ref, bs_ref, g_ref):
        ti = pl.program_id(1)
        _, _, _, xs = _rwkv_pre_core(x_ref, halo_ref, mod_ref, vec_ref, ti)
        _, r, k, v, _, _, _, lw, la, g = _rwkv_proj(xs, wrkv, w1, a1, g1, w2, a2, g2)
        headsum = _make_headsum(e_ref[...], et_ref[...])
        ld, k2, a_s, b_s = _rwkv_elem(r, k, lw, la, vec_ref[6:7], vec_ref[7:8], vec_ref[8:9], vec_ref[9:10], headsum)
        g_ref[0] = g
        for ref, val in ((r_ref, r), (ld_ref, ld), (k2_ref, k2), (v_ref, v), (as_ref, a_s), (bs_ref, b_s)):
            _to_heads(ref, val, nh)

    halo = pl.BlockSpec((1, 8, d), lambda e, i: (e, jnp.maximum(i * hb - 1, 0), 0))
    hs = _sds((bl, nh, t, HEAD))
    return pl.pallas_call(
        body, name="rwkv_pre_fwd", grid=(bl, t // tm),
        in_specs=[_tok(tm, d), halo, _per_example(8, d), _whole(vec.shape), _whole(e_ind.shape), _whole(et_ind.shape)]
        + _rwkv_weight_specs(weights),
        out_specs=(_heads(nh, tm),) * 6 + (_tok(tm, d),),
        out_shape=(hs,) * 6 + (_sds(x.shape),),
        compiler_params=_cparams(("arbitrary", "arbitrary")),
    )(x, x, mod, vec, e_ind, et_ind, *weights)


def _rwkv_post_fwd(x, y, r, k2, v, g, mod, vec, e_ind, et_ind, w_out):
    bl, t, d = x.shape
    tm = min(t, RWKV_TM)
    nh = d // HEAD

    def body(x_ref, y_ref, r_ref, k2_ref, v_ref, g_ref, mod_ref, vec_ref, e_ref, et_ref, wout_ref,
             xo_ref, mix_ref, s0, s1, s2, s3):
        headsum = _make_headsum(e_ref[...], et_ref[...])
        yv, rv, kv, vv = (_from_heads(ref, scr, nh) for ref, scr in
                          ((y_ref, s0), (r_ref, s1), (k2_ref, s2), (v_ref, s3)))
        o = _rwkv_post(yv, rv, kv, vv, g_ref[0], vec_ref[10:11], vec_ref[11:12], vec_ref[12:13], headsum)
        mix = _dot(o, wout_ref[...])
        mix_ref[0] = mix
        xo_ref[0] = x_ref[0] + mod_ref[0, 2:3] * mix

    return pl.pallas_call(
        body, name="rwkv_post_fwd", grid=(bl, t // tm),
        in_specs=[_tok(tm, d)] + [_heads(nh, tm)] * 4 + [_tok(tm, d), _per_example(8, d), _whole(vec.shape),
                                                         _whole(e_ind.shape), _whole(et_ind.shape), _whole(w_out.shape)],
        out_specs=(_tok(tm, d), _tok(tm, d)),
        out_shape=(_sds(x.shape), _sds(x.shape)),
        scratch_shapes=[pltpu.VMEM((tm, d), F32)] * 4,
        compiler_params=_cparams(("arbitrary", "arbitrary")),
    )(x, y, r, k2, v, g, mod, vec, e_ind, et_ind, w_out)


def _rwkv_post_bwd(dxo, mix, y, r, k2, v, g, mod, vec, e_ind, et_ind, w_out, scatter=()):
    bl, t, d = dxo.shape
    tm = min(t, RWKV_TM)
    nh = d // HEAD

    def body(dxo_ref, mix_ref, y_ref, r_ref, k2_ref, v_ref, g_ref, mod_ref, vec_ref, e_ref, et_ref, wout_ref,
             dy_ref, dr_ref, dk2_ref, dv_ref, dg_ref, o_ref, dmix_ref, dgate_ref, small_ref, s0, s1, s2, s3):
        e, ti = pl.program_id(0), pl.program_id(1)
        headsum = _make_headsum(e_ref[...], et_ref[...])
        yv, rv, kv, vv = (_from_heads(ref, scr, nh) for ref, scr in
                          ((y_ref, s0), (r_ref, s1), (k2_ref, s2), (v_ref, s3)))
        dxo = dxo_ref[0]
        dmix = mod_ref[0, 2:3] * dxo
        dmix_ref[0] = dmix.astype(BF16)
        do = _dot_nt(dmix, wout_ref[...])
        post = functools.partial(_rwkv_post, headsum=headsum)
        o, vjp = jax.vjp(post, yv, rv, kv, vv, g_ref[0], vec_ref[10:11], vec_ref[11:12], vec_ref[12:13])
        o_ref[0] = o.astype(BF16)
        dy, dr, dk2, dv, dg, dlng, dlnb, drk = vjp(do)
        _to_heads(dy_ref, dy, nh)
        dr_ref[0], dk2_ref[0], dv_ref[0], dg_ref[0] = dr, dk2, dv, dg
        zero = jnp.zeros((1, d), F32)
        _add_rows(dgate_ref, ti == 0, [zero, zero, _colsum(dxo * mix_ref[0])])

        @pl.when(_first(e, ti))
        def _():
            small_ref[...] = jnp.zeros_like(small_ref)

        small_ref[0:1] += dlng
        small_ref[1:2] += dlnb
        small_ref[2:3] += drk

    return _call_with_scatter(
        body, name="rwkv_post_bwd", grid=(bl, t // tm),
        in_specs=[_tok(tm, d), _tok(tm, d)] + [_heads(nh, tm)] * 4
        + [_tok(tm, d), _per_example(8, d), _whole(vec.shape), _whole(e_ind.shape), _whole(et_ind.shape),
           _whole(w_out.shape)],
        out_specs=(_heads(nh, tm),) + (_tok(tm, d),) * 6 + (_per_example(8, d), _whole((8, d))),
        out_shape=(_sds((bl, nh, t, HEAD)),) + (_sds(dxo.shape),) * 4 + (_sds(dxo.shape, BF16),) * 2
        + (_sds((bl, 8, d)), _sds((8, d))),
        scratch_shapes=[pltpu.VMEM((tm, d), F32)] * 4,
        operands=(dxo, mix, y, r, k2, v, g, mod, vec, e_ind, et_ind, w_out), scatter=scatter)


RWKV_BWD_TM = 128


def _rwkv_pre_bwd(x, mod, vec, e_ind, et_ind, weights, dr_p, dk2_p, dv_p, dg, dr_s, dld, dk2_s, dv_s, das, dbs):
    bl, t, d = x.shape
    tm = min(t, RWKV_BWD_TM)
    nh = d // HEAD
    hb = tm // 8
    lp, gp = LORA_PAD, GATE_PAD

    def body(x_ref, halo_ref, mod_ref, vec_ref, e_ref, et_ref, wrkv, w1, a1, g1, w2, a2, g2,
             drp_ref, dk2p_ref, dvp_ref, dg_ref, drs_ref, dld_ref, dk2s_ref, dvs_ref, das_ref, dbs_ref,
             dh_ref, dhp_ref, xr_ref, xw_ref, xk_ref, xv_ref, xa_ref, xg_ref, dr_ref, dk_ref, dv_ref,
             dtw_ref, dta_ref, dtg_ref, tw2_ref, ta_ref, sg_ref, dlw_ref, dla_ref, dgb_ref, small_ref,
             s0, s1, s2, s3, s4, s5):
        e, ti = pl.program_id(0), pl.program_id(1)
        _, _, xx, xs = _rwkv_pre_core(x_ref, halo_ref, mod_ref, vec_ref, ti)
        xb, r, k, v, tw2, ta, sg, lw, la, _ = _rwkv_proj(xs, wrkv, w1, a1, g1, w2, a2, g2)
        for ref, val in zip((xr_ref, xw_ref, xk_ref, xv_ref, xa_ref, xg_ref), xb):
            ref[0] = val
        headsum = _make_headsum(e_ref[...], et_ref[...])
        drs, dld, dk2s, dvs, das, dbs_ = (_from_heads(ref, scr, nh) for ref, scr in
                                          ((drs_ref, s0), (dld_ref, s1), (dk2s_ref, s2), (dvs_ref, s3),
                                           (das_ref, s4), (dbs_ref, s5)))
        elem = functools.partial(_rwkv_elem, r, headsum=headsum)
        _, vjp = jax.vjp(elem, k, lw, la, vec_ref[6:7], vec_ref[7:8], vec_ref[8:9], vec_ref[9:10])
        dk, dlw, dla, dw0, da0, dkk, dka = vjp((dld, dk2p_ref[0] + dk2s, das, dbs_))
        dr = drp_ref[0] + drs
        dv = dvp_ref[0] + dvs
        dgv = dg_ref[0]
        dtg = _dot_nt(dgv, g2[...]) * sg * (1.0 - sg)
        dtw = _dot_nt(dlw, w2[...]) * (1.0 - tw2 * tw2)
        dta = _dot_nt(dla, a2[...])
        dr_ref[0], dk_ref[0], dv_ref[0] = dr.astype(BF16), dk.astype(BF16), dv.astype(BF16)
        dtw_ref[0], dta_ref[0], dtg_ref[0] = dtw.astype(BF16), dta.astype(BF16), dtg.astype(BF16)
        tw2_ref[0], ta_ref[0], sg_ref[0] = tw2.astype(BF16), ta.astype(BF16), sg.astype(BF16)
        dlw_ref[0], dla_ref[0], dgb_ref[0] = dlw.astype(BF16), dla.astype(BF16), dgv.astype(BF16)
        dxs = (_dot_nt(dr, wrkv[:, 0:d]), _dot_nt(dtw, w1[...]), _dot_nt(dk, wrkv[:, d:2 * d]),
               _dot_nt(dv, wrkv[:, 2 * d:3 * d]), _dot_nt(dta, a1[...]), _dot_nt(dtg, g1[...]))

        @pl.when(_first(e, ti))
        def _():
            small_ref[...] = jnp.zeros_like(small_ref)

        total = jnp.zeros((tm, d), F32)
        dhp = jnp.zeros((tm, d), F32)
        for i, dxi in enumerate(dxs):
            total += dxi
            dhp += dxi * vec_ref[i:i + 1]
            small_ref[i:i + 1] += _colsum(dxi * xx)
        dh_ref[0], dhp_ref[0] = total - dhp, dhp
        small_ref[6:7] += dw0
        small_ref[7:8] += da0
        small_ref[8:9] += dkk
        small_ref[9:10] += dka

    halo = pl.BlockSpec((1, 8, d), lambda e, i: (e, jnp.maximum(i * hb - 1, 0), 0))
    tokd, tokl, tokg = _tok(tm, d), _tok(tm, lp), _tok(tm, gp)
    bf = lambda w: _sds((bl, t, w), BF16)
    return pl.pallas_call(
        body, name="rwkv_pre_bwd", grid=(bl, t // tm),
        in_specs=[tokd, halo, _per_example(8, d), _whole(vec.shape), _whole(e_ind.shape), _whole(et_ind.shape)]
        + _rwkv_weight_specs(weights) + [tokd] * 4 + [_heads(nh, tm)] * 6,
        out_specs=(tokd, tokd) + (tokd,) * 6 + (tokd,) * 3 + (tokl, tokl, tokg, tokl, tokl, tokg)
        + (tokd, tokd, tokd, _whole((N_VEC, d))),
        out_shape=(_sds(x.shape), _sds(x.shape)) + (bf(d),) * 9 + (bf(lp), bf(lp), bf(gp), bf(lp), bf(lp), bf(gp))
        + (bf(d), bf(d), bf(d), _sds((N_VEC, d))),
        scratch_shapes=[pltpu.VMEM((tm, d), F32)] * 6,
        compiler_params=_cparams(("arbitrary", "arbitrary")),
    )(x, x, mod, vec, e_ind, et_ind, *weights, dr_p, dk2_p, dv_p, dg, dr_s, dld, dk2_s, dv_s, das, dbs)


def _norm_bwd(x, dxo, dh, dhprev, mod, dgate):
    bl, t, d = x.shape
    tm = min(t, RWKV_TM)
    hb = tm // 8
    last_blk = t // 8 - 1

    def body(x_ref, dxo_ref, dh_ref, dhp_ref, nxt_ref, mod_ref, dgate_ref, dx_ref, dmod_ref):
        ti = pl.program_id(1)
        xn, inv = _rms(x_ref[0])
        last = jnp.where(ti == t // tm - 1, 0.0, nxt_ref[0, 0:1])
        dh = dh_ref[0] + _shift_up(dhp_ref[0], last)
        dx_ref[0] = dxo_ref[0] + _rms_bwd(xn, inv, dh * (1.0 + mod_ref[0, 1:2]))

        @pl.when(ti == 0)
        def _():
            dmod_ref[0] = dgate_ref[0]

        dmod_ref[0, 0:1] += _colsum(dh)
        dmod_ref[0, 1:2] += _colsum(dh * xn)

    nxt = pl.BlockSpec((1, 8, d), lambda e, i: (e, jnp.minimum((i + 1) * hb, last_blk), 0))
    return pl.pallas_call(
        body, name="norm_bwd", grid=(bl, t // tm),
        in_specs=[_tok(tm, d)] * 4 + [nxt, _per_example(8, d), _per_example(8, d)],
        out_specs=(_tok(tm, d), _per_example(8, d)),
        out_shape=(_sds(x.shape), _sds((bl, 8, d))),
        compiler_params=_cparams(("arbitrary", "arbitrary")),
    )(x, dxo, dh, dhprev, dhprev, mod, dgate)


def _final(x, target, final_g):
    bl, t, d = x.shape
    tm = min(t, 512)

    def body(x_ref, tgt_ref, g_ref, dx_ref, loss_ref, dg_ref):
        e, ti = pl.program_id(0), pl.program_id(1)

        @pl.when(_first(e, ti))
        def _():
            loss_ref[...] = jnp.zeros_like(loss_ref)
            dg_ref[...] = jnp.zeros_like(dg_ref)

        xn, inv = _rms(x_ref[0])
        err = xn * g_ref[...] - tgt_ref[0]
        loss_ref[...] += (0.5 / d) * jnp.sum(err * err)
        dy = err * (1.0 / d)
        dg_ref[0:1] += _colsum(dy * xn)
        dx_ref[0] = _rms_bwd(xn, inv, dy * g_ref[...])

    return pl.pallas_call(
        body, name="final_loss", grid=(bl, t // tm),
        in_specs=[_tok(tm, d), _tok(tm, d), _whole(final_g.shape)],
        out_specs=(_tok(tm, d), _whole((8, 128)), _whole((8, d))),
        out_shape=(_sds(x.shape), _sds((8, 128)), _sds((8, d))),
        compiler_params=_cparams(("arbitrary", "arbitrary")),
    )(x, target, final_g)


def _adamw_math(w, g, m, v):
    m = ADAM_B1 * m + (1.0 - ADAM_B1) * g
    v = ADAM_B2 * v + (1.0 - ADAM_B2) * jnp.square(g)
    m_hat = m / (1.0 - ADAM_B1 ** ADAM_STEP)
    v_hat = v / (1.0 - ADAM_B2 ** ADAM_STEP)
    return -ADAM_LR * (m_hat / (jnp.sqrt(v_hat) + ADAM_EPS) + ADAM_WD * w), m, v


def _sum_parts(ref, n):
    g = ref[0].astype(F32)
    for s in range(1, n):
        g = g + ref[s].astype(F32)
    return g


def _adamw_layers(w, m, v, parts, name):
    nl, rows, c = w.shape
    tr = min(rows, 128)

    def body(w_ref, m_ref, v_ref, *refs):
        p_refs, (g_ref, d_ref, mo_ref, vo_ref) = refs[:nl], refs[nl:]
        for layer in range(nl):
            @pl.when(pl.program_id(0) == layer)
            def _(p_ref=p_refs[layer]):
                g = _sum_parts(p_ref, p_ref.shape[0])
                g_ref[...] = g
                d_ref[...], mo_ref[...], vo_ref[...] = _adamw_math(w_ref[...], g, m_ref[...], v_ref[...])

    row = pl.BlockSpec((None, tr, c), lambda l, i: (l, i, 0))
    return pl.pallas_call(
        body, name=name, grid=(nl, rows // tr),
        in_specs=[row, row, row] + [pl.BlockSpec((p.shape[0], tr, c), lambda l, i: (0, i, 0)) for p in parts],
        out_specs=(row,) * 4, out_shape=(_sds(w.shape),) * 4,
        compiler_params=_cparams(("arbitrary", "arbitrary")),
    )(w, m, v, *parts)


def _adamw_small(items, name):
    k = len(items)
    ns = [it[3].shape[0] for it in items]

    def body(*refs):
        ins, outs = refs[:4 * k], refs[4 * k:]
        for i in range(k):
            w_ref, m_ref, v_ref, p_ref = ins[4 * i:4 * i + 4]
            g = _sum_parts(p_ref, ns[i])
            outs[4 * i][...] = g
            outs[4 * i + 1][...], outs[4 * i + 2][...], outs[4 * i + 3][...] = _adamw_math(
                w_ref[...], g, m_ref[...], v_ref[...])

    flat = [a for it in items for a in it]
    res = pl.pallas_call(
        body, name=name,
        out_shape=tuple(_sds(it[0].shape) for it in items for _ in range(4)),
        compiler_params=_cparams(),
    )(*flat)
    return [tuple(res[4 * i:4 * i + 4]) for i in range(k)]


WEIGHTS = ['ada_w', 'ada_b', 'mlp_w1', 'mlp_w2', 'a_w_in', 'a_ln_g', 'a_ln_b', 'a_w_s', 'a_b_s', 'a_w_out', 'b_mu',
           'b_w_in', 'b_w0', 'b_w1', 'b_w2', 'b_a0', 'b_a1', 'b_a2', 'b_g1', 'b_g2', 'b_k_k', 'b_k_a', 'b_r_k',
           'b_ln_g', 'b_ln_b', 'b_w_out', 'final_g']
VECTORS = ['b_mu', 'b_w0', 'b_a0', 'b_k_k', 'b_k_a', 'b_ln_g', 'b_ln_b']
REPLICATED = ['a_ln_g', 'a_ln_b', 'a_w_s', 'a_b_s', 'b_r_k', 'final_g']
ROW_ALIGN = 16


def _pad_rows(a, mult):
    pad = (-a.shape[-2]) % mult
    return jnp.pad(a, [(0, 0)] * (a.ndim - 2) + [(0, pad), (0, 0)]) if pad else a


def _as2d(a):
    if a.ndim == 1:
        return a.reshape(1, -1)
    lead = 1
    for s in a.shape[:-1]:
        lead *= s
    return a.reshape(lead, a.shape[-1])


def kernel(x, c, ada_w, ada_b, mlp_w1, mlp_w2, a_w_in, a_ln_g, a_ln_b, a_w_s, a_b_s, a_w_out, b_mu, b_w_in, b_w0, b_w1, b_w2, b_a0, b_a1, b_a2, b_g1, b_g2, b_k_k, b_k_a, b_r_k, b_ln_g, b_ln_b, b_w_out, final_g, loss_target, m_ada_w, m_ada_b, m_mlp_w1, m_mlp_w2, m_a_w_in, m_a_ln_g, m_a_ln_b, m_a_w_s, m_a_b_s, m_a_w_out, m_b_mu, m_b_w_in, m_b_w0, m_b_w1, m_b_w2, m_b_a0, m_b_a1, m_b_a2, m_b_g1, m_b_g2, m_b_k_k, m_b_k_a, m_b_r_k, m_b_ln_g, m_b_ln_b, m_b_w_out, m_final_g, v_ada_w, v_ada_b, v_mlp_w1, v_mlp_w2, v_a_w_in, v_a_ln_g, v_a_ln_b, v_a_w_s, v_a_b_s, v_a_w_out, v_b_mu, v_b_w_in, v_b_w0, v_b_w1, v_b_w2, v_b_a0, v_b_a1, v_b_a2, v_b_g1, v_b_g2, v_b_k_k, v_b_k_a, v_b_r_k, v_b_ln_g, v_b_ln_b, v_b_w_out, v_final_g):
    given = dict(locals())
    w = {n: given[n] for n in WEIGHTS}
    bl, t, d = x.shape
    nl = ada_w.shape[0]
    nb = N_DEV * bl
    m_tok = bl * t
    me = 4 * lax.axis_index("x") + 2 * lax.axis_index("y") + lax.axis_index("c")

    bf = lambda a: a.astype(BF16)
    vec_loc = _pad_rows(jnp.concatenate([_as2d(w[n]) for n in VECTORS], axis=0), ROW_ALIGN)
    g_c, g_vec, g_a_in, g_a_out = _gather_call((c, vec_loc, bf(a_w_in[0]), bf(a_w_out[0])), "gather_first")

    c_all = g_c.reshape(nb, d)
    cols = ada_w.shape[2]
    ada_b_cols = lax.dynamic_slice(ada_b, (0, me * cols), (nl, cols)).reshape(nl, 1, cols)
    mod_cols = _ada_fwd(c_all, ada_w, ada_b_cols)
    mod_full = jnp.moveaxis(_gather_call((mod_cols,), "gather_mod")[0], 0, 2).reshape(nl, nb, 6 * d)
    mod_mine = lax.dynamic_slice(mod_full, (0, me * bl, 0), (nl, bl, 6 * d)).reshape(nl, bl, 6, d)
    mod_mix = jnp.pad(mod_mine[:, :, 0:3], ((0, 0), (0, 0), (0, 5), (0, 0)))
    mod_mlp = jnp.pad(mod_mine[:, :, 3:6], ((0, 0), (0, 0), (0, 5), (0, 0)))

    def unshard(g, ax):
        g = jnp.moveaxis(g, 0, ax)
        return g.reshape(g.shape[:ax] + (g.shape[ax] * g.shape[ax + 1],) + g.shape[ax + 2:])

    lora_names = ['b_w1', 'b_a1', 'b_g1', 'b_w2', 'b_a2', 'b_g2']
    lora_pack = jnp.concatenate([bf(w[n]).reshape(-1) for n in lora_names]).reshape(-1, 128)
    full = {'a_w_in': unshard(g_a_in, 1), 'a_w_out': unshard(g_a_out, 0)}
    n_vec_rows = sum(_as2d(w[n]).shape[0] for n in VECTORS)
    vec = jnp.moveaxis(g_vec, 0, 1).reshape(N_VEC, d)
    vec = vec.at[n_vec_rows].set(b_r_k.reshape(d))

    e_ind, et_ind = _head_indicators(d)
    gd = d // SGU_GROUPS
    group_ind = (jnp.arange(SGU_GROUPS)[:, None] == jnp.arange(d)[None, :] // gd).astype(BF16)
    bias_full = jnp.repeat(a_b_s[0].T, gd, axis=1)
    pad_c = lambda a, n: jnp.pad(a, ((0, 0), (0, n - a.shape[1])))
    pad_r = lambda a, n: jnp.pad(a, ((0, n - a.shape[0]), (0, 0)))
    sgu_args = (full['a_w_in'], a_ln_g, a_ln_b, a_w_s[0], bias_full, full['a_w_out'])

    x0 = x
    (x1, mix_a), (g_w1_0, g_w2_0) = _sgu_fwd(x0, mod_mix[0], *sgu_args, gather=(bf(mlp_w1[0]), bf(mlp_w2[0])))
    w1_full = [unshard(g_w1_0, 1), None]
    w2_full = [unshard(g_w2_0, 0), None]
    (x2, ff0, q0), (g_w1_1, g_w2_1, g_b_in, g_b_out, g_lora) = _mlp_fwd(
        x1, mod_mlp[0], w1_full[0], w2_full[0],
        gather=(bf(mlp_w1[1]), bf(mlp_w2[1]), bf(b_w_in[0]), bf(b_w_out[0]), lora_pack))
    w1_full[1], w2_full[1] = unshard(g_w1_1, 1), unshard(g_w2_1, 0)
    full['b_w_in'], full['b_w_out'] = unshard(g_b_in, 1), unshard(g_b_out, 0)
    lora_flat, lo = g_lora.reshape(N_DEV, -1), 0
    for n, ax in zip(lora_names, (0, 0, 0, 1, 1, 1)):
        loc = w[n].shape[1:]
        full[n] = unshard(lora_flat[:, lo:lo + w[n].size].reshape((N_DEV,) + loc), ax)
        lo += w[n].size
    rwkv_w = (full['b_w_in'], pad_c(full['b_w1'], LORA_PAD), pad_c(full['b_a1'], LORA_PAD),
              pad_c(full['b_g1'], GATE_PAD), pad_r(full['b_w2'], LORA_PAD), pad_r(full['b_a2'], LORA_PAD),
              pad_r(full['b_g2'], GATE_PAD))
    r, ld, k2, v, a_s, b_s, gate = _rwkv_pre_fwd(x2, mod_mix[1], vec, e_ind, et_ind, rwkv_w)
    y, s0, tinv = _wkv_fwd(r, ld, k2, v, a_s, b_s)
    x3, mix_b = _rwkv_post_fwd(x2, y, r, k2, v, gate, mod_mix[1], vec, e_ind, et_ind, full['b_w_out'])
    (x4, ff1, q1), _ = _mlp_fwd(x3, mod_mlp[1], w1_full[1], w2_full[1])
    dx4, loss_blk, dfinal = _final(x4, loss_target, final_g.reshape(1, d))
    loss = lax.psum(loss_blk[0, 0], ("x", "y", "c"))

    tok = lambda a: a.reshape(m_tok, a.shape[-1])
    shard_rows = lambda g: g.reshape((N_DEV, g.shape[0] // N_DEV) + g.shape[1:])
    (dx3, dmod_mlp1, h_b, dff_b, dp_b), _ = _mlp_bwd(x3, dx4, ff1, q1, mod_mlp[1], w1_full[1], w2_full[1])
    gw1_1 = _matmul_tn(tok(h_b), tok(dp_b), "grad_mlp_w1_l1", col_shards=N_DEV)
    gw2_1 = shard_rows(_matmul_tn(tok(q1), tok(dff_b), "grad_mlp_w2_l1"))
    (dy, dr_p, dk2_p, dv_p, dgate_act, o_b, dmix_b, dgate_b, small_post), (rw1_1, rw2_1) = _rwkv_post_bwd(
        dx3, mix_b, y, r, k2, v, gate, mod_mix[1], vec, e_ind, et_ind, full['b_w_out'], scatter=(gw1_1, gw2_1))
    g_b_w_out = shard_rows(_matmul_tn(tok(o_b), tok(dmix_b), "grad_b_w_out"))
    dr_s, dld, dk2_s, dv_s, das, dbs = _wkv_bwd(r, ld, k2, v, a_s, b_s, s0, tinv, dy)
    (dh, dhp, xr_b, xw_b, xk_b, xv_b, xa_b, xg_b, dr_b, dk_b, dv_b, dtw_b, dta_b, dtg_b, tw2_b, ta_b, sg_b,
     dlw_b, dla_b, dg_b, small_pre) = _rwkv_pre_bwd(x2, mod_mix[1], vec, e_ind, et_ind, rwkv_w,
                                                    dr_p, dk2_p, dv_p, dgate_act, dr_s, dld, dk2_s, dv_s, das, dbs)
    g_b_w_in = jnp.concatenate([_matmul_tn(tok(xr_b), tok(dr_b), "grad_b_w_r"),
                                _matmul_tn(tok(xk_b), tok(dk_b), "grad_b_w_k"),
                                _matmul_tn(tok(xv_b), tok(dv_b), "grad_b_w_v")], axis=1)
    shard_cols = lambda g: jnp.moveaxis(g.reshape(g.shape[0], N_DEV, g.shape[1] // N_DEV), 1, 0)
    g_b_w_in = shard_cols(g_b_w_in)
    lw_, lg_ = b_w1.shape[2], b_g1.shape[2]
    small_names = ['b_w1', 'b_a1', 'b_g1', 'b_w2', 'b_a2', 'b_g2'] + VECTORS
    small_parts = [
        shard_rows(_matmul_tn(tok(xw_b), tok(dtw_b), "grad_b_w1")[:, :lw_]),
        shard_rows(_matmul_tn(tok(xa_b), tok(dta_b), "grad_b_a1")[:, :lw_]),
        shard_rows(_matmul_tn(tok(xg_b), tok(dtg_b), "grad_b_g1")[:, :lg_]),
        shard_cols(_matmul_tn(tok(tw2_b), tok(dlw_b), "grad_b_w2")[:lw_]),
        shard_cols(_matmul_tn(tok(ta_b), tok(dla_b), "grad_b_a2")[:lw_]),
        shard_cols(_matmul_tn(tok(sg_b), tok(dg_b), "grad_b_g2")[:lg_]),
        shard_cols(jnp.concatenate([small_pre[0:10], small_post[0:2]], axis=0).astype(BF16)),
    ]
    small_flat = jnp.concatenate([p.reshape(N_DEV, -1) for p in small_parts], axis=1)
    lane = 128
    small_rows = -(-small_flat.shape[1] // (lane * ROW_ALIGN)) * ROW_ALIGN
    small_pack = jnp.pad(small_flat, ((0, 0), (0, small_rows * lane - small_flat.shape[1]))).reshape(
        N_DEV, small_rows, lane)
    dx2, dmod_mix1 = _norm_bwd(x2, dx3, dh, dhp, mod_mix[1], dgate_b)
    (dx1, dmod_mlp0, h_b, dff_b, dp_b), (r_b_w_in, r_b_w_out, r_small) = _mlp_bwd(
        x1, dx2, ff0, q0, mod_mlp[0], w1_full[0], w2_full[0], scatter=(g_b_w_in, g_b_w_out, small_pack))
    gw1_0 = _matmul_tn(tok(h_b), tok(dp_b), "grad_mlp_w1_l0", col_shards=N_DEV)
    gw2_0 = shard_rows(_matmul_tn(tok(q0), tok(dff_b), "grad_mlp_w2_l0"))
    (dx0, dmod_mix0, h_b, dpre_b, z_b, dmix_b, small_sgu, d_ws, d_bs), (rw1_0, rw2_0) = _sgu_bwd(
        x0, dx1, mix_a, mod_mix[0], *sgu_args, group_ind, scatter=(gw1_0, gw2_0))
    dmod_mine = jnp.stack([jnp.concatenate([dmod_mix0[:, 0:3], dmod_mlp0[:, 0:3]], axis=1),
                           jnp.concatenate([dmod_mix1[:, 0:3], dmod_mlp1[:, 0:3]], axis=1)], axis=1)
    rep_g = {'a_ln_g': small_sgu[0:1], 'a_ln_b': small_sgu[1:2], 'a_w_s': d_ws.reshape(-1, d), 'a_b_s': d_bs.reshape(1, d),
             'b_r_k': small_post[2:3], 'final_g': dfinal[0:1]}
    rep_rows = [rep_g[n].shape[0] for n in REPLICATED]
    rep_pack = _pad_rows(jnp.concatenate([rep_g[n] for n in REPLICATED], axis=0), 8)
    g_a_w_in, (dmod_all, rep_all) = _matmul_tn(tok(h_b), tok(dpre_b), "grad_a_w_in", col_shards=N_DEV,
                                               gather=(dmod_mine.reshape(bl, nl * 6 * d), rep_pack))
    g_a_w_out = shard_rows(_matmul_tn(tok(z_b), tok(dmix_b), "grad_a_w_out"))
    r_a_w_in, r_a_w_out = _scatter_call((g_a_w_in, g_a_w_out), "scatter_sgu_grads")

    dmod_all = jnp.moveaxis(dmod_all.reshape(nb, nl, 6 * d), 0, 1)
    dmod_cols = lax.dynamic_slice(dmod_all, (0, 0, me * cols), (nl, nb, cols))
    g_ada_w, g_ada_b = _ada_bwd(c_all, dmod_cols, dmod_all)

    mom = {n: given['m_' + n] for n in WEIGHTS}
    var = {n: given['v_' + n] for n in WEIGHTS}
    out = {}
    as3d = lambda a: a.reshape((-1,) + a.shape[-2:])
    for n, parts in (('mlp_w1', [rw1_0, rw1_1]), ('mlp_w2', [rw2_0, rw2_1]), ('a_w_in', [r_a_w_in]),
                     ('a_w_out', [r_a_w_out]), ('b_w_in', [r_b_w_in]), ('b_w_out', [r_b_w_out]),
                     ('ada_w', list(g_ada_w))):
        res = _adamw_layers(as3d(w[n]), as3d(mom[n]), as3d(var[n]), parts, "adamw_" + n)
        out[n] = tuple(a.reshape(w[n].shape) for a in res)

    items, names = [], []

    def add(n, part):
        s2 = _as2d(w[n]).shape
        items.append((_as2d(w[n]), _as2d(mom[n]), _as2d(var[n]), part.reshape((part.shape[0],) + s2)))
        names.append(n)

    sflat = r_small.reshape(N_DEV, -1)
    so = 0
    for n in small_names:
        sz = w[n].size
        add(n, sflat[:, so:so + sz])
        so += sz
    ro = 0
    for n, nr in zip(REPLICATED, rep_rows):
        add(n, rep_all[:, ro:ro + nr])
        ro += nr
    add('ada_b', g_ada_b[None])
    for n, res in zip(names, _adamw_small(items, "adamw_small")):
        out[n] = tuple(a.reshape(w[n].shape) for a in res)

    return (loss, dx0, *[out[n][0] for n in WEIGHTS], *[out[n][1] for n in WEIGHTS],
            *[out[n][2] for n in WEIGHTS], *[out[n][3] for n in WEIGHTS])
```

```python
import functools

import jax
import jax.numpy as jnp
from jax import lax
from jax.experimental import pallas as pl
from jax.experimental.pallas import tpu as pltpu

F32 = jnp.float32
BF16 = jnp.bfloat16

N_DEV = 8
RMS_EPS = 1e-6
LN_EPS = 1e-5
HEAD = 64
GN_EPS = HEAD * 1e-5
L2_EPS = 1e-12
SGU_CHUNK = 128
SGU_GROUPS = 8
WKV_CHUNK = 64
WKV_HEADS_PER_STEP = 16
WKV_EXAMPLES_PER_STEP = 2
LORA_PAD = 128
GATE_PAD = 256
ADAM_LR, ADAM_B1, ADAM_B2, ADAM_EPS, ADAM_WD, ADAM_STEP = 0.001, 0.9, 0.999, 1e-08, 0.01, 10
VMEM_LIMIT = 60 * 1024 * 1024


def _cparams(sem=None, **kw):
    if sem is not None:
        kw["dimension_semantics"] = sem
    return pltpu.CompilerParams(vmem_limit_bytes=VMEM_LIMIT, **kw)


def _dot(a, b):
    return jnp.dot(a.astype(BF16), b.astype(BF16), preferred_element_type=F32)


def _dot_nt(a, b):
    return lax.dot_general(a.astype(BF16), b.astype(BF16), (((1,), (1,)), ((), ())), preferred_element_type=F32)


def _dot_tn(a, b):
    return lax.dot_general(a.astype(BF16), b.astype(BF16), (((0,), (0,)), ((), ())), preferred_element_type=F32)


def _bdot(a, b, dims):
    return lax.dot_general(a.astype(BF16), b.astype(BF16), (dims, ((0,), (0,))), preferred_element_type=F32)


@jax.custom_vjp
def _tri_sum(tri, tri_t, x):
    hi = x.astype(BF16)
    lo = (x - hi.astype(F32)).astype(BF16)
    dn = (((2,), (1,)), ((0,), (0,)))
    return (lax.dot_general(tri, hi, dn, preferred_element_type=F32)
            + lax.dot_general(tri, lo, dn, preferred_element_type=F32))


_tri_sum.defvjp(lambda tri, tri_t, x: (_tri_sum(tri, tri_t, x), (tri, tri_t)),
                lambda res, g: (jnp.zeros_like(res[0]), jnp.zeros_like(res[1]), _tri_sum(res[1], res[0], g)))


@jax.custom_vjp
def _bmm_nn(a, b):
    return _bdot(a, b, ((2,), (1,)))


@jax.custom_vjp
def _bmm_nt(a, b):
    return _bdot(a, b, ((2,), (2,)))


@jax.custom_vjp
def _bmm_tn(a, b):
    return _bdot(a, b, ((1,), (1,)))


_bmm_nn.defvjp(lambda a, b: (_bmm_nn(a, b), (a, b)), lambda res, g: (_bmm_nt(g, res[1]), _bmm_tn(res[0], g)))
_bmm_nt.defvjp(lambda a, b: (_bmm_nt(a, b), (a, b)), lambda res, g: (_bmm_nn(g, res[1]), _bmm_tn(g, res[0])))
_bmm_tn.defvjp(lambda a, b: (_bmm_tn(a, b), (a, b)), lambda res, g: (_bmm_nt(res[1], g), _bmm_nn(res[0], g)))


def _tri_inverse(p):
    n = p.shape[1]
    row = lax.broadcasted_iota(jnp.int32, (n, n), 0)
    col = lax.broadcasted_iota(jnp.int32, (n, n), 1)
    tinv = jnp.where(row == col, 1.0, 0.0).astype(F32)[None] + p
    for _ in range(max(1, (n - 1).bit_length()) - 1):
        p = _bmm_nn(p, p)
        tinv = tinv + _bmm_nn(tinv, p)
    return tinv.astype(BF16)


def _tri_solve_fwd(tinv, p, rhs):
    u = _bmm_nn(tinv, rhs)
    return u, (tinv, u)


def _tri_solve_bwd(res, du):
    tinv, u = res
    drhs = _bmm_tn(tinv, du)
    return jnp.zeros_like(tinv), _bmm_nt(drhs, u), drhs


@jax.custom_vjp
def _tri_solve(tinv, p, rhs):
    return _tri_solve_fwd(tinv, p, rhs)[0]


_tri_solve.defvjp(_tri_solve_fwd, _tri_solve_bwd)


def _wkv_chunk(s0, r, ld, k, v, a, b, tinv=None):
    nh, n, _ = r.shape
    row = lax.broadcasted_iota(jnp.int32, (n, n), 0)
    col = lax.broadcasted_iota(jnp.int32, (n, n), 1)
    incl = row >= col
    strict = row > col
    lower = jnp.broadcast_to(jnp.where(incl, 1.0, 0.0).astype(BF16), (nh, n, n))
    upper = jnp.broadcast_to(jnp.where(row <= col, 1.0, 0.0).astype(BF16), (nh, n, n))
    c = _tri_sum(lower, upper, ld)
    c_end = c[:, n - 1:n, :]
    ec, enc, ecx, eend = jnp.exp(c), jnp.exp(-c), jnp.exp(c - ld), jnp.exp(c_end - c)
    ar = jnp.concatenate([a * ecx, r * ec], axis=1)
    mask = jnp.concatenate([strict, incl], axis=0)[None]
    m_b = jnp.where(mask, _bmm_nt(ar, b * enc), 0.0)
    m_k = jnp.where(mask, _bmm_nt(ar, k * enc), 0.0)
    a_ab, a_rb = m_b[:, :n], m_b[:, n:]
    base = _bmm_nt(ar, s0) + _bmm_nn(m_k, v)
    if tinv is None:
        tinv = lax.stop_gradient(_tri_inverse(a_ab))
    u = _tri_solve(tinv, a_ab, base[:, :n])
    y = base[:, n:] + _bmm_nn(a_rb, u)
    s1 = s0 * jnp.exp(c_end) + _bmm_tn(jnp.concatenate([u, v], axis=1), jnp.concatenate([b * eend, k * eend], axis=1))
    return y, s1, tinv


def _wkv_specs(bl, nh, t):
    eb, hb, lc = min(bl, WKV_EXAMPLES_PER_STEP), min(nh, WKV_HEADS_PER_STEP), WKV_CHUNK
    return eb, hb, lc, (bl // eb, nh // hb, t // lc)


def _wkv_fwd(r, ld, k, v, a, b):
    bl, nh, t, n = r.shape
    eb, hb, lc, grid = _wkv_specs(bl, nh, t)
    nc = t // lc
    nb = eb * hb

    def body(r_ref, ld_ref, k_ref, v_ref, a_ref, b_ref, y_ref, s0_ref, tinv_ref, s_scr):
        @pl.when(pl.program_id(2) == 0)
        def _():
            s_scr[...] = jnp.zeros_like(s_scr)

        s0 = s_scr[...]
        s0_ref[:, :, 0] = s0.reshape(eb, hb, n, n)
        y, s1, tinv = _wkv_chunk(
            s0, *(ref[...].reshape(nb, lc, n) for ref in (r_ref, ld_ref, k_ref, v_ref, a_ref, b_ref)))
        y_ref[...] = y.reshape(eb, hb, lc, n)
        tinv_ref[:, :, 0] = tinv.reshape(eb, hb, lc, lc)
        s_scr[...] = s1

    seq = pl.BlockSpec((eb, hb, lc, n), lambda e, h, c: (e, h, c, 0))
    return pl.pallas_call(
        body, name="wkv_fwd", grid=grid,
        in_specs=[seq] * 6,
        out_specs=(seq, pl.BlockSpec((eb, hb, 1, n, n), lambda e, h, c: (e, h, c, 0, 0)),
                   pl.BlockSpec((eb, hb, 1, lc, lc), lambda e, h, c: (e, h, c, 0, 0))),
        out_shape=(jax.ShapeDtypeStruct((bl, nh, t, n), F32), jax.ShapeDtypeStruct((bl, nh, nc, n, n), F32),
                   jax.ShapeDtypeStruct((bl, nh, nc, lc, lc), BF16)),
        scratch_shapes=[pltpu.VMEM((nb, n, n), F32)],
        compiler_params=_cparams(("arbitrary", "arbitrary", "arbitrary")),
    )(r, ld, k, v, a, b)


def _wkv_bwd(r, ld, k, v, a, b, s0_all, tinv_all, dy):
    bl, nh, t, n = r.shape
    eb, hb, lc, grid = _wkv_specs(bl, nh, t)
    nc = t // lc
    nb = eb * hb

    def body(r_ref, ld_ref, k_ref, v_ref, a_ref, b_ref, s0_ref, tinv_ref, dy_ref,
             dr_ref, dld_ref, dk_ref, dv_ref, da_ref, db_ref, ds_scr):
        @pl.when(pl.program_id(2) == 0)
        def _():
            ds_scr[...] = jnp.zeros_like(ds_scr)

        args = (s0_ref[:, :, 0].reshape(nb, n, n),) + tuple(
            ref[...].reshape(nb, lc, n) for ref in (r_ref, ld_ref, k_ref, v_ref, a_ref, b_ref))
        tinv = tinv_ref[:, :, 0].reshape(nb, lc, lc)
        _, vjp = jax.vjp(lambda *xs: _wkv_chunk(*xs, tinv=tinv)[:2], *args)
        ds0, *dseq = vjp((dy_ref[...].reshape(nb, lc, n), ds_scr[...]))
        ds_scr[...] = ds0
        for ref, val in zip((dr_ref, dld_ref, dk_ref, dv_ref, da_ref, db_ref), dseq):
            ref[...] = val.reshape(eb, hb, lc, n)

    seq = pl.BlockSpec((eb, hb, lc, n), lambda e, h, c: (e, h, nc - 1 - c, 0))
    st = pl.BlockSpec((eb, hb, 1, n, n), lambda e, h, c: (e, h, nc - 1 - c, 0, 0))
    ti = pl.BlockSpec((eb, hb, 1, lc, lc), lambda e, h, c: (e, h, nc - 1 - c, 0, 0))
    out = jax.ShapeDtypeStruct((bl, nh, t, n), F32)
    return pl.pallas_call(
        body, name="wkv_bwd", grid=grid,
        in_specs=[seq] * 6 + [st, ti, seq],
        out_specs=(seq,) * 6, out_shape=(out,) * 6,
        scratch_shapes=[pltpu.VMEM((nb, n, n), F32)],
        compiler_params=_cparams(("arbitrary", "arbitrary", "arbitrary")),
    )(r, ld, k, v, a, b, s0_all, tinv_all, dy)


def _scatter_copies(x_refs, o_refs, send_sems, recv_sems, local_sems):
    pos = (lax.axis_index("x"), lax.axis_index("y"), lax.axis_index("c"))
    me = 4 * pos[0] + 2 * pos[1] + pos[2]

    def descriptors():
        sends, arrivals, local = [], [], []
        for i, (x_ref, o_ref) in enumerate(zip(x_refs, o_refs)):
            for m in range(1, N_DEV):
                p = tuple(1 - pos[a] if (m >> (2 - a)) & 1 else pos[a] for a in range(3))
                pidx = 4 * p[0] + 2 * p[1] + p[2]
                k = (N_DEV - 1) * i + m - 1
                for dst, out in ((o_ref.at[me], sends), (o_ref.at[pidx], arrivals)):
                    out.append(pltpu.make_async_remote_copy(
                        src_ref=x_ref.at[pidx], dst_ref=dst, send_sem=send_sems.at[k], recv_sem=recv_sems.at[k],
                        device_id=p, device_id_type=pl.DeviceIdType.MESH))
            local.append(pltpu.make_async_copy(x_ref.at[me], o_ref.at[me], local_sems.at[i]))
        return sends, arrivals, local

    def start():
        sends, _, local = descriptors()
        for cp in local + sends:
            cp.start()

    def finish():
        sends, arrivals, local = descriptors()
        for cp in arrivals:
            cp.wait_recv()
        for cp in sends:
            cp.wait_send()
        for cp in local:
            cp.wait()

    return start, finish


def _gather_copies(x_refs, o_refs, send_sems, recv_sems, local_sems):
    pos = (lax.axis_index("x"), lax.axis_index("y"), lax.axis_index("c"))
    me = 4 * pos[0] + 2 * pos[1] + pos[2]
    far = (2, 4, 6)

    def peer_of(m):
        p = tuple(1 - pos[a] if (m >> (2 - a)) & 1 else pos[a] for a in range(3))
        return p, 4 * p[0] + 2 * p[1] + p[2]

    sibling, _ = peer_of(1)

    def copy(i, k, src, slot, to):
        return pltpu.make_async_remote_copy(
            src_ref=src, dst_ref=o_refs[i].at[slot], send_sem=send_sems.at[(N_DEV - 1) * i + k],
            recv_sem=recv_sems.at[(N_DEV - 1) * i + k], device_id=to, device_id_type=pl.DeviceIdType.MESH)

    def direct(i):
        return [copy(i, m - 1, x_refs[i], me, peer_of(m)[0]) for m in (1,) + far]

    def local(i):
        return pltpu.make_async_copy(x_refs[i], o_refs[i].at[me], local_sems.at[i])

    def start():
        for i in range(len(x_refs)):
            local(i).start()
            for cp in direct(i):
                cp.start()

    def finish():
        n = len(x_refs)
        relays = []
        for i in range(n):
            for m in far:
                origin = peer_of(m)[1]
                copy(i, m - 1, x_refs[i], origin, sibling).wait_recv()
                fwd = copy(i, m, o_refs[i].at[origin], origin, sibling)
                fwd.start()
                relays.append(fwd)
        for i in range(n):
            copy(i, 0, x_refs[i], peer_of(1)[1], sibling).wait_recv()
            for m in far:
                copy(i, m, x_refs[i], peer_of(m ^ 1)[1], sibling).wait_recv()
        for i in range(n):
            for cp in direct(i):
                cp.wait_send()
            local(i).wait()
        for cp in relays:
            cp.wait_send()

    return start, finish


def _scatter_scratch(n):
    return [pltpu.SemaphoreType.DMA(((N_DEV - 1) * n,)), pltpu.SemaphoreType.DMA(((N_DEV - 1) * n,)),
            pltpu.SemaphoreType.DMA((n,))]


_ANY = pl.BlockSpec(memory_space=pl.ANY)


def _scatter_call(arrays, name):
    n = len(arrays)

    def body(*refs):
        start, finish = _scatter_copies(refs[:n], refs[n:2 * n], *refs[2 * n:])
        start()
        finish()

    return pl.pallas_call(
        body, name=name, in_specs=[_ANY] * n, out_specs=(_ANY,) * n,
        out_shape=tuple(_sds(a.shape, a.dtype) for a in arrays), scratch_shapes=_scatter_scratch(n),
    )(*arrays)


def _gather_call(arrays, name):
    n = len(arrays)

    def body(*refs):
        start, finish = _gather_copies(refs[:n], refs[n:2 * n], *refs[2 * n:])
        start()
        finish()

    return pl.pallas_call(
        body, name=name, in_specs=[_ANY] * n, out_specs=(_ANY,) * n,
        out_shape=tuple(_sds((N_DEV,) + a.shape, a.dtype) for a in arrays), scratch_shapes=_scatter_scratch(n),
    )(*arrays)


def _call_with_scatter(body, *, name, grid, in_specs, out_specs, out_shape, scratch_shapes, operands,
                       scatter=(), gather=()):
    assert not (scatter and gather)
    carried = tuple(scatter) or tuple(gather)
    copies = _scatter_copies if scatter else _gather_copies
    recv_shapes = tuple(_sds(a.shape if scatter else (N_DEV,) + a.shape, a.dtype) for a in carried)
    nc, n_in, n_out, n_scr = len(carried), len(in_specs), len(out_specs), len(scratch_shapes)
    if nc == 0:
        return pl.pallas_call(
            body, name=name, grid=grid, in_specs=list(in_specs), out_specs=tuple(out_specs),
            out_shape=tuple(out_shape), scratch_shapes=list(scratch_shapes),
            compiler_params=_cparams(("arbitrary",) * len(grid)))(*operands), ()

    def wrapped(*refs):
        ins, refs = refs[:n_in], refs[n_in:]
        c_in, refs = refs[:nc], refs[nc:]
        outs, refs = refs[:n_out], refs[n_out:]
        c_out, refs = refs[:nc], refs[nc:]
        scr, sems = refs[:n_scr], refs[n_scr:]
        ids = [pl.program_id(a) for a in range(len(grid))]
        first = functools.reduce(jnp.logical_and, [i == 0 for i in ids])
        last = functools.reduce(jnp.logical_and, [i == g - 1 for i, g in zip(ids, grid)])
        start, finish = copies(c_in, c_out, *sems)
        pl.when(first)(start)
        body(*ins, *outs, *scr)
        pl.when(last)(finish)

    res = pl.pallas_call(
        wrapped, name=name, grid=grid,
        in_specs=list(in_specs) + [_ANY] * nc, out_specs=tuple(out_specs) + (_ANY,) * nc,
        out_shape=tuple(out_shape) + recv_shapes,
        scratch_shapes=list(scratch_shapes) + _scatter_scratch(nc),
        compiler_params=_cparams(("arbitrary",) * len(grid)),
    )(*operands, *carried)
    return res[:n_out], res[n_out:]


def _rms(x):
    inv = lax.rsqrt(jnp.mean(x * x, axis=-1, keepdims=True) + RMS_EPS)
    return x * inv, inv


def _rms_bwd(xn, inv, dxn):
    return inv * (dxn - xn * jnp.mean(dxn * xn, axis=-1, keepdims=True))


def _colsum(x):
    return jnp.sum(x, axis=0, keepdims=True)


def _sigmoid(x):
    return 0.5 * (jnp.tanh(0.5 * x) + 1.0)


def _split_bf16(x):
    hi = x.astype(BF16)
    return hi, (x - hi.astype(F32)).astype(BF16)


def _dot_split(x, e):
    hi, lo = _split_bf16(x)
    return jnp.dot(hi, e, preferred_element_type=F32) + jnp.dot(lo, e, preferred_element_type=F32)


@jax.custom_vjp
def _headsum(x, e, et):
    return _dot_split(_dot_split(x, e), et)


_headsum.defvjp(lambda x, e, et: (_headsum(x, e, et), (e, et)),
                lambda res, g: (_headsum(g, *res), jnp.zeros_like(res[0]), jnp.zeros_like(res[1])))


def _make_headsum(e, et):
    return lambda x: _headsum(x, e, et)


def _head_indicators(d):
    e = (jnp.arange(d)[:, None] // HEAD == jnp.arange(128)[None, :]).astype(BF16)
    return e, e.T


def _rwkv_elem(r, k, lw, la, w0, a0, k_k, k_a, headsum):
    z = w0 + lw
    w_log = -(jnp.maximum(-z, 0.0) + jnp.log(1.0 + jnp.exp(-jnp.abs(z)))) - 0.5
    ld = -jnp.exp(w_log)
    a = _sigmoid(a0 + la)
    kkp = k * k_k
    kk = kkp * lax.rsqrt(jnp.maximum(headsum(kkp * kkp), L2_EPS * L2_EPS))
    k2 = k * (1.0 + (a - 1.0) * k_a)
    del r
    return ld, k2, -kk, kk * a


def _rwkv_post(y, r, k2, v, g, ln_g, ln_b, r_k, headsum):
    m = headsum(y) * (1.0 / HEAD)
    yc = y - m
    var = headsum(yc * yc) * (1.0 / HEAD)
    yn = yc * lax.rsqrt(var + GN_EPS)
    bonus = headsum(r * k2 * r_k) * v
    return (yn * ln_g + ln_b + bonus) * g


def _shift_down(h, first_row):
    rolled = pltpu.roll(h, 1, 0)
    row = lax.broadcasted_iota(jnp.int32, h.shape, 0)
    return jnp.where(row == 0, first_row, rolled)


def _shift_up(h, last_row):
    n = h.shape[0]
    rolled = pltpu.roll(h, n - 1, 0)
    row = lax.broadcasted_iota(jnp.int32, h.shape, 0)
    return jnp.where(row == n - 1, last_row, rolled)


def _gelu(p):
    return 0.5 * p * (1.0 + lax.erf(p * 0.7071067811865476))


def _gelu_grad(p):
    return 0.5 * (1.0 + lax.erf(p * 0.7071067811865476)) + p * jnp.exp(-0.5 * p * p) * 0.3989422804014327


def _tok(tm, d):
    return pl.BlockSpec((1, tm, d), lambda e, t, *_: (e, t, 0))


def _per_example(rows, d):
    return pl.BlockSpec((1, rows, d), lambda e, t, *_: (e, 0, 0))


def _whole(shape):
    nd = len(shape)
    return pl.BlockSpec(tuple(shape), lambda *_: (0,) * nd)


def _heads(nh, tm):
    return pl.BlockSpec((1, nh, tm, HEAD), lambda e, t, *_: (e, 0, t, 0))


def _sds(shape, dtype=F32):
    return jax.ShapeDtypeStruct(tuple(shape), dtype)


def _add_rows(ref, first, rows):
    @pl.when(first)
    def _():
        ref[0] = jnp.zeros(ref.shape[1:], ref.dtype)

    for i, r in enumerate(rows):
        ref[0, i:i + 1] += r


def _first(e, t):
    return jnp.logical_and(e == 0, t == 0)


def _ada_fwd(c_all, ada_w, ada_b_cols):
    nl, d, cols = ada_w.shape
    nb = c_all.shape[0]

    def body(c_ref, w_ref, b_ref, o_ref):
        c = c_ref[...]
        cond = c * _sigmoid(c)
        for i in range(nl):
            o_ref[i] = _dot(cond, w_ref[i]) + b_ref[i]

    return pl.pallas_call(
        body, name="ada_fwd", out_shape=_sds((nl, nb, cols)),
        compiler_params=_cparams(),
    )(c_all, ada_w, ada_b_cols)


def _ada_bwd(c_all, dmod_cols, dmod_full):
    nl, nb, cols = dmod_cols.shape
    d = c_all.shape[1]

    def body(c_ref, g_ref, f_ref, b_ref, *o_refs):
        c = c_ref[...]
        cond = c * _sigmoid(c)
        for i in range(nl):
            o_refs[i][0] = _dot_tn(cond, g_ref[i])
            b_ref[i:i + 1] = jnp.sum(f_ref[i], axis=0, keepdims=True)

    res = pl.pallas_call(
        body, name="ada_bwd", out_shape=(_sds((nl, dmod_full.shape[2])),) + (_sds((1, d, cols)),) * nl,
        compiler_params=_cparams(),
    )(c_all, dmod_cols, dmod_full)
    return res[1:], res[0]


def _matmul_tn(a, b, name, col_shards=None, gather=()):
    m, ka = a.shape
    n = b.shape[1]
    tm = min(m, 2048)
    tk = min(ka, 1024)
    tn = min(n, 1024)
    steps = m // tm
    if col_shards:
        cs = n // col_shards
        spt = tn // cs
        out_spec = pl.BlockSpec((spt, tk, cs), lambda i, j, s: (j, i, 0))
        out_shape = _sds((col_shards, ka, cs), BF16)
    else:
        out_spec = pl.BlockSpec((tk, tn), lambda i, j, s: (i, j))
        out_shape = _sds((ka, n), BF16)

    def body(a_ref, b_ref, o_ref, acc):
        s = pl.program_id(2)

        @pl.when(s == 0)
        def _():
            acc[...] = jnp.zeros_like(acc)

        acc[...] += _dot_tn(a_ref[...], b_ref[...])

        @pl.when(s == steps - 1)
        def _():
            if col_shards:
                for q in range(spt):
                    o_ref[q] = acc[:, q * cs:(q + 1) * cs].astype(BF16)
            else:
                o_ref[...] = acc[...].astype(BF16)

    res, got = _call_with_scatter(
        body, name=name, grid=(ka // tk, n // tn, steps),
        in_specs=[pl.BlockSpec((tm, tk), lambda i, j, s: (s, i)), pl.BlockSpec((tm, tn), lambda i, j, s: (s, j))],
        out_specs=(out_spec,), out_shape=(out_shape,),
        scratch_shapes=[pltpu.VMEM((tk, tn), F32)], operands=(a, b), gather=gather)
    return (res[0], got) if gather else res[0]


MLP_FWD_TM = 1024
MLP_FJ = 1024
MLP_BWD_TM = 1024
MLP_BWD_FJ = 512


def _mlp_fwd(x, mod, w1, w2, gather=()):
    bl, t, d = x.shape
    f = w1.shape[1]
    tm, fj = min(t, MLP_FWD_TM), min(f, MLP_FJ)
    nj = f // fj

    def body(x_ref, mod_ref, w1_ref, w2_ref, xo_ref, ff_ref, q_ref, h_scr, acc):
        j = pl.program_id(2)

        @pl.when(j == 0)
        def _():
            xn, _ = _rms(x_ref[0])
            h_scr[...] = (xn * (1.0 + mod_ref[0, 1:2]) + mod_ref[0, 0:1]).astype(BF16)
            acc[...] = jnp.zeros_like(acc)

        p = jnp.dot(h_scr[...], w1_ref[...], preferred_element_type=F32)
        q = jnp.square(jnp.maximum(p, 0.0)).astype(BF16)
        q_ref[0] = q
        acc[...] += jnp.dot(q, w2_ref[...], preferred_element_type=F32)

        @pl.when(j == nj - 1)
        def _():
            ff_ref[0] = acc[...]
            xo_ref[0] = x_ref[0] + mod_ref[0, 2:3] * acc[...]

    return _call_with_scatter(
        body, name="mlp_fwd", grid=(bl, t // tm, nj),
        in_specs=[_tok(tm, d), _per_example(8, d),
                  pl.BlockSpec((d, fj), lambda e, i, j: (0, j)), pl.BlockSpec((fj, d), lambda e, i, j: (j, 0))],
        out_specs=(_tok(tm, d), _tok(tm, d), pl.BlockSpec((1, tm, fj), lambda e, i, j: (e, i, j))),
        out_shape=(_sds(x.shape), _sds(x.shape), _sds((bl, t, f), BF16)),
        scratch_shapes=[pltpu.VMEM((tm, d), BF16), pltpu.VMEM((tm, d), F32)],
        operands=(x, mod, w1, w2), gather=gather)


def _mlp_bwd(x, dxo, ff, q, mod, w1, w2, scatter=()):
    bl, t, d = x.shape
    f = w1.shape[1]
    tm, fj = min(t, MLP_BWD_TM), min(f, MLP_BWD_FJ)
    nj = f // fj

    def body(x_ref, dxo_ref, ff_ref, q_ref, mod_ref, w1_ref, w2_ref,
             dx_ref, dmod_ref, h_ref, dff_ref, dp_ref, acc):
        ti, j = pl.program_id(1), pl.program_id(2)

        @pl.when(j == 0)
        def _():
            xn, _ = _rms(x_ref[0])
            h_ref[0] = (xn * (1.0 + mod_ref[0, 1:2]) + mod_ref[0, 0:1]).astype(BF16)
            dff_ref[0] = (mod_ref[0, 2:3] * dxo_ref[0]).astype(BF16)
            acc[...] = jnp.zeros_like(acc)

        rl = jnp.sqrt(q_ref[0].astype(F32))
        dp = (_dot_nt(dff_ref[0], w2_ref[...]) * (2.0 * rl)).astype(BF16)
        dp_ref[0] = dp
        acc[...] += _dot_nt(dp, w1_ref[...])

        @pl.when(j == nj - 1)
        def _():
            xn, inv = _rms(x_ref[0])
            dh = acc[...]
            dx_ref[0] = dxo_ref[0] + _rms_bwd(xn, inv, dh * (1.0 + mod_ref[0, 1:2]))
            _add_rows(dmod_ref, ti == 0, [_colsum(dh), _colsum(dh * xn), _colsum(dxo_ref[0] * ff_ref[0])])

    big = lambda: pl.BlockSpec((1, tm, fj), lambda e, i, j: (e, i, j))
    return _call_with_scatter(
        body, name="mlp_bwd", grid=(bl, t // tm, nj),
        in_specs=[_tok(tm, d), _tok(tm, d), _tok(tm, d), big(), _per_example(8, d),
                  pl.BlockSpec((d, fj), lambda e, i, j: (0, j)), pl.BlockSpec((fj, d), lambda e, i, j: (j, 0))],
        out_specs=(_tok(tm, d), _per_example(8, d), _tok(tm, d), _tok(tm, d), big()),
        out_shape=(_sds(x.shape), _sds((bl, 8, d)), _sds(x.shape, BF16), _sds(x.shape, BF16),
                   _sds((bl, t, f), BF16)),
        scratch_shapes=[pltpu.VMEM((tm, d), F32)],
        operands=(x, dxo, ff, q, mod, w1, w2), scatter=scatter)


SGU_TM = 256


def _sgu_core(x, mod_ref, win_ref, lng, lnb, ws_ref, bias_ref):
    tm, d = x.shape
    xn, inv = _rms(x)
    h = (xn * (1.0 + mod_ref[0, 1:2]) + mod_ref[0, 0:1]).astype(BF16)
    pre = jnp.dot(h, win_ref[...], preferred_element_type=F32)
    uv = _gelu(pre)
    u, v = uv[:, :d], uv[:, d:]
    mu = jnp.mean(v, axis=-1, keepdims=True)
    vc = v - mu
    rstd = lax.rsqrt(jnp.mean(vc * vc, axis=-1, keepdims=True) + LN_EPS)
    vhat = vc * rstd
    vln = vhat * lng + lnb
    gd = d // SGU_GROUPS
    rows = []
    for c in range(tm // SGU_CHUNK):
        cols = []
        for g in range(SGU_GROUPS):
            cols.append(_dot(ws_ref[g], vln[c * SGU_CHUNK:(c + 1) * SGU_CHUNK, g * gd:(g + 1) * gd]))
        rows.append(jnp.concatenate(cols, axis=1) + bias_ref[...])
    sv = jnp.concatenate(rows, axis=0)
    return xn, inv, h, pre, u, vhat, rstd, vln, sv


def _sgu_masked(ws_ref, wm_scr):
    row = lax.broadcasted_iota(jnp.int32, (SGU_CHUNK, SGU_CHUNK), 0)
    col = lax.broadcasted_iota(jnp.int32, (SGU_CHUNK, SGU_CHUNK), 1)
    for g in range(SGU_GROUPS):
        wm_scr[g] = jnp.where(row >= col, ws_ref[g], 0.0).astype(BF16)


def _sgu_fwd(x, mod, w_in, ln_g, ln_b, w_s, bias_full, w_out, gather=()):
    bl, t, d = x.shape
    tm = min(t, SGU_TM)

    def body(x_ref, mod_ref, win_ref, lng_ref, lnb_ref, ws_ref, bias_ref, wout_ref, xo_ref, mix_ref, wm_scr):
        _sgu_masked(ws_ref, wm_scr)
        xt = x_ref[0]
        *_, u, _, _, _, sv = _sgu_core(xt, mod_ref, win_ref, lng_ref[...], lnb_ref[...], wm_scr, bias_ref)
        mix = _dot(u * sv, wout_ref[...])
        mix_ref[0] = mix
        xo_ref[0] = xt + mod_ref[0, 2:3] * mix

    return _call_with_scatter(
        body, name="sgu_fwd", grid=(bl, t // tm),
        in_specs=[_tok(tm, d), _per_example(8, d), _whole(w_in.shape), _whole(ln_g.shape), _whole(ln_b.shape),
                  _whole(w_s.shape), _whole(bias_full.shape), _whole(w_out.shape)],
        out_specs=(_tok(tm, d), _tok(tm, d)),
        out_shape=(_sds(x.shape), _sds(x.shape)),
        scratch_shapes=[pltpu.VMEM(w_s.shape, BF16)],
        operands=(x, mod, w_in, ln_g, ln_b, w_s, bias_full, w_out), gather=gather)


def _sgu_bwd(x, dxo, mix, mod, w_in, ln_g, ln_b, w_s, bias_full, w_out, group_ind, scatter=()):
    bl, t, d = x.shape
    tm = min(t, SGU_TM)
    gd = d // SGU_GROUPS

    def body(x_ref, dxo_ref, mix_ref, mod_ref, win_ref, lng_ref, lnb_ref, ws_ref, bias_ref, wout_ref, ind_ref,
             dx_ref, dmod_ref, h_ref, dpre_ref, z_ref, dmix_ref, small_ref, dws_ref, dbs_ref, wm_scr, dbias_scr):
        e, ti = pl.program_id(0), pl.program_id(1)
        _sgu_masked(ws_ref, wm_scr)
        xt, dxo = x_ref[0], dxo_ref[0]
        lng = lng_ref[...]
        xn, inv, h, pre, u, vhat, rstd, vln, sv = _sgu_core(xt, mod_ref, win_ref, lng, lnb_ref[...], wm_scr, bias_ref)
        h_ref[0] = h
        z_ref[0] = (u * sv).astype(BF16)
        dmix = mod_ref[0, 2:3] * dxo
        dmix_ref[0] = dmix.astype(BF16)
        dz = _dot_nt(dmix, wout_ref[...])
        du, dsv = dz * sv, dz * u

        @pl.when(_first(e, ti))
        def _():
            dws_ref[...] = jnp.zeros_like(dws_ref)
            dbias_scr[...] = jnp.zeros_like(dbias_scr)
            small_ref[...] = jnp.zeros_like(small_ref)

        row = lax.broadcasted_iota(jnp.int32, (SGU_CHUNK, SGU_CHUNK), 0)
        col = lax.broadcasted_iota(jnp.int32, (SGU_CHUNK, SGU_CHUNK), 1)
        rows = []
        for c in range(tm // SGU_CHUNK):
            rs = slice(c * SGU_CHUNK, (c + 1) * SGU_CHUNK)
            dbias_scr[...] += dsv[rs]
            cols = []
            for g in range(SGU_GROUPS):
                cs = slice(g * gd, (g + 1) * gd)
                cols.append(_dot_tn(wm_scr[g], dsv[rs, cs]))
                dws_ref[g] += jnp.where(row >= col, _dot_nt(dsv[rs, cs], vln[rs, cs]), 0.0)
            rows.append(jnp.concatenate(cols, axis=1))
        dvln = jnp.concatenate(rows, axis=0)
        small_ref[0:1] += _colsum(dvln * vhat)
        small_ref[1:2] += _colsum(dvln)
        dvhat = dvln * lng
        dv = rstd * (dvhat - jnp.mean(dvhat, axis=-1, keepdims=True)
                     - vhat * jnp.mean(dvhat * vhat, axis=-1, keepdims=True))
        dpre = (jnp.concatenate([du, dv], axis=1) * _gelu_grad(pre)).astype(BF16)
        dpre_ref[0] = dpre
        dh = _dot_nt(dpre, win_ref[...])
        dx_ref[0] = dxo + _rms_bwd(xn, inv, dh * (1.0 + mod_ref[0, 1:2]))
        _add_rows(dmod_ref, ti == 0, [_colsum(dh), _colsum(dh * xn), _colsum(dxo * mix_ref[0])])

        @pl.when(jnp.logical_and(e == bl - 1, ti == t // tm - 1))
        def _():
            hi, lo = _split_bf16(dbias_scr[...])
            ind = ind_ref[...]
            dbs_ref[...] = (lax.dot_general(ind, hi, (((1,), (1,)), ((), ())), preferred_element_type=F32)
                            + lax.dot_general(ind, lo, (((1,), (1,)), ((), ())), preferred_element_type=F32))

    return _call_with_scatter(
        body, name="sgu_bwd", grid=(bl, t // tm),
        in_specs=[_tok(tm, d), _tok(tm, d), _tok(tm, d), _per_example(8, d), _whole(w_in.shape), _whole(ln_g.shape),
                  _whole(ln_b.shape), _whole(w_s.shape), _whole(bias_full.shape), _whole(w_out.shape),
                  _whole(group_ind.shape)],
        out_specs=(_tok(tm, d), _per_example(8, d), _tok(tm, d), _tok(tm, 2 * d), _tok(tm, d), _tok(tm, d),
                   _whole((8, d)), _whole(w_s.shape), _whole((SGU_GROUPS, SGU_CHUNK))),
        out_shape=(_sds(x.shape), _sds((bl, 8, d)), _sds(x.shape, BF16), _sds((bl, t, 2 * d), BF16),
                   _sds(x.shape, BF16), _sds(x.shape, BF16), _sds((8, d)), _sds(w_s.shape),
                   _sds((SGU_GROUPS, SGU_CHUNK))),
        scratch_shapes=[pltpu.VMEM(w_s.shape, BF16), pltpu.VMEM((SGU_CHUNK, d), F32)],
        operands=(x, dxo, mix, mod, w_in, ln_g, ln_b, w_s, bias_full, w_out, group_ind), scatter=scatter)


RWKV_TM = 256
N_VEC = 16


def _rwkv_pre_core(x_ref, halo_ref, mod_ref, vec_ref, ti):
    xn, inv = _rms(x_ref[0])
    scale1, shift = 1.0 + mod_ref[0, 1:2], mod_ref[0, 0:1]
    h = xn * scale1 + shift
    hn, _ = _rms(halo_ref[0])
    hh = hn * scale1 + shift
    first = jnp.where(ti == 0, 0.0, hh[7:8])
    xx = _shift_down(h, first) - h
    xs = [h + xx * vec_ref[i:i + 1] for i in range(6)]
    return xn, inv, xx, xs


def _rwkv_proj(xs, wrkv_ref, w1_ref, a1_ref, g1_ref, w2_ref, a2_ref, g2_ref):
    d = xs[0].shape[1]
    xr, xw, xk, xv, xa, xg = [z.astype(BF16) for z in xs]
    r = jnp.dot(xr, wrkv_ref[:, 0:d], preferred_element_type=F32)
    k = jnp.dot(xk, wrkv_ref[:, d:2 * d], preferred_element_type=F32)
    v = jnp.dot(xv, wrkv_ref[:, 2 * d:3 * d], preferred_element_type=F32)
    tw2 = jnp.tanh(jnp.dot(xw, w1_ref[...], preferred_element_type=F32))
    ta = jnp.dot(xa, a1_ref[...], preferred_element_type=F32)
    sg = _sigmoid(jnp.dot(xg, g1_ref[...], preferred_element_type=F32))
    lw, la, g = _dot(tw2, w2_ref[...]), _dot(ta, a2_ref[...]), _dot(sg, g2_ref[...])
    return (xr, xw, xk, xv, xa, xg), r, k, v, tw2, ta, sg, lw, la, g


def _to_heads(ref, val, nh):
    for hd in range(nh):
        ref[0, hd] = val[:, hd * HEAD:(hd + 1) * HEAD]


def _from_heads(ref, scr, nh):
    for hd in range(nh):
        scr[:, hd * HEAD:(hd + 1) * HEAD] = ref[0, hd]
    return scr[...]


def _rwkv_weight_specs(ws):
    return [_whole(w.shape) for w in ws]


def _rwkv_pre_fwd(x, mod, vec, e_ind, et_ind, weights):
    bl, t, d = x.shape
    tm = min(t, RWKV_TM)
    nh = d // HEAD
    hb = tm // 8

    def body(x_ref, halo_ref, mod_ref, vec_ref, e_ref, et_ref, wrkv, w1, a1, g1, w2, a2, g2,
             r_ref, ld_ref, k2_ref, v_ref, as_ref, bs_ref, g_ref):
        ti = pl.program_id(1)
        _, _, _, xs = _rwkv_pre_core(x_ref, halo_ref, mod_ref, vec_ref, ti)
        _, r, k, v, _, _, _, lw, la, g = _rwkv_proj(xs, wrkv, w1, a1, g1, w2, a2, g2)
        headsum = _make_headsum(e_ref[...], et_ref[...])
        ld, k2, a_s, b_s = _rwkv_elem(r, k, lw, la, vec_ref[6:7], vec_ref[7:8], vec_ref[8:9], vec_ref[9:10], headsum)
        g_ref[0] = g
        for ref, val in ((r_ref, r), (ld_ref, ld), (k2_ref, k2), (v_ref, v), (as_ref, a_s), (bs_ref, b_s)):
            _to_heads(ref, val, nh)

    halo = pl.BlockSpec((1, 8, d), lambda e, i: (e, jnp.maximum(i * hb - 1, 0), 0))
    hs = _sds((bl, nh, t, HEAD))
    return pl.pallas_call(
        body, name="rwkv_pre_fwd", grid=(bl, t // tm),
        in_specs=[_tok(tm, d), halo, _per_example(8, d), _whole(vec.shape), _whole(e_ind.shape), _whole(et_ind.shape)]
        + _rwkv_weight_specs(weights),
        out_specs=(_heads(nh, tm),) * 6 + (_tok(tm, d),),
        out_shape=(hs,) * 6 + (_sds(x.shape),),
        compiler_params=_cparams(("arbitrary", "arbitrary")),
    )(x, x, mod, vec, e_ind, et_ind, *weights)


def _rwkv_post_fwd(x, y, r, k2, v, g, mod, vec, e_ind, et_ind, w_out):
    bl, t, d = x.shape
    tm = min(t, RWKV_TM)
    nh = d // HEAD

    def body(x_ref, y_ref, r_ref, k2_ref, v_ref, g_ref, mod_ref, vec_ref, e_ref, et_ref, wout_ref,
             xo_ref, mix_ref, s0, s1, s2, s3):
        headsum = _make_headsum(e_ref[...], et_ref[...])
        yv, rv, kv, vv = (_from_heads(ref, scr, nh) for ref, scr in
                          ((y_ref, s0), (r_ref, s1), (k2_ref, s2), (v_ref, s3)))
        o = _rwkv_post(yv, rv, kv, vv, g_ref[0], vec_ref[10:11], vec_ref[11:12], vec_ref[12:13], headsum)
        mix = _dot(o, wout_ref[...])
        mix_ref[0] = mix
        xo_ref[0] = x_ref[0] + mod_ref[0, 2:3] * mix

    return pl.pallas_call(
        body, name="rwkv_post_fwd", grid=(bl, t // tm),
        in_specs=[_tok(tm, d)] + [_heads(nh, tm)] * 4 + [_tok(tm, d), _per_example(8, d), _whole(vec.shape),
                                                         _whole(e_ind.shape), _whole(et_ind.shape), _whole(w_out.shape)],
        out_specs=(_tok(tm, d), _tok(tm, d)),
        out_shape=(_sds(x.shape), _sds(x.shape)),
        scratch_shapes=[pltpu.VMEM((tm, d), F32)] * 4,
        compiler_params=_cparams(("arbitrary", "arbitrary")),
    )(x, y, r, k2, v, g, mod, vec, e_ind, et_ind, w_out)


def _rwkv_post_bwd(dxo, mix, y, r, k2, v, g, mod, vec, e_ind, et_ind, w_out, scatter=()):
    bl, t, d = dxo.shape
    tm = min(t, RWKV_TM)
    nh = d // HEAD

    def body(dxo_ref, mix_ref, y_ref, r_ref, k2_ref, v_ref, g_ref, mod_ref, vec_ref, e_ref, et_ref, wout_ref,
             dy_ref, dr_ref, dk2_ref, dv_ref, dg_ref, o_ref, dmix_ref, dgate_ref, small_ref, s0, s1, s2, s3):
        e, ti = pl.program_id(0), pl.program_id(1)
        headsum = _make_headsum(e_ref[...], et_ref[...])
        yv, rv, kv, vv = (_from_heads(ref, scr, nh) for ref, scr in
                          ((y_ref, s0), (r_ref, s1), (k2_ref, s2), (v_ref, s3)))
        dxo = dxo_ref[0]
        dmix = mod_ref[0, 2:3] * dxo
        dmix_ref[0] = dmix.astype(BF16)
        do = _dot_nt(dmix, wout_ref[...])
        post = functools.partial(_rwkv_post, headsum=headsum)
        o, vjp = jax.vjp(post, yv, rv, kv, vv, g_ref[0], vec_ref[10:11], vec_ref[11:12], vec_ref[12:13])
        o_ref[0] = o.astype(BF16)
        dy, dr, dk2, dv, dg, dlng, dlnb, drk = vjp(do)
        _to_heads(dy_ref, dy, nh)
        dr_ref[0], dk2_ref[0], dv_ref[0], dg_ref[0] = dr, dk2, dv, dg
        zero = jnp.zeros((1, d), F32)
        _add_rows(dgate_ref, ti == 0, [zero, zero, _colsum(dxo * mix_ref[0])])

        @pl.when(_first(e, ti))
        def _():
            small_ref[...] = jnp.zeros_like(small_ref)

        small_ref[0:1] += dlng
        small_ref[1:2] += dlnb
        small_ref[2:3] += drk

    return _call_with_scatter(
        body, name="rwkv_post_bwd", grid=(bl, t // tm),
        in_specs=[_tok(tm, d), _tok(tm, d)] + [_heads(nh, tm)] * 4
        + [_tok(tm, d), _per_example(8, d), _whole(vec.shape), _whole(e_ind.shape), _whole(et_ind.shape),
           _whole(w_out.shape)],
        out_specs=(_heads(nh, tm),) + (_tok(tm, d),) * 6 + (_per_example(8, d), _whole((8, d))),
        out_shape=(_sds((bl, nh, t, HEAD)),) + (_sds(dxo.shape),) * 4 + (_sds(dxo.shape, BF16),) * 2
        + (_sds((bl, 8, d)), _sds((8, d))),
        scratch_shapes=[pltpu.VMEM((tm, d), F32)] * 4,
        operands=(dxo, mix, y, r, k2, v, g, mod, vec, e_ind, et_ind, w_out), scatter=scatter)


RWKV_BWD_TM = 128


def _rwkv_pre_bwd(x, mod, vec, e_ind, et_ind, weights, dr_p, dk2_p, dv_p, dg, dr_s, dld, dk2_s, dv_s, das, dbs):
    bl, t, d = x.shape
    tm = min(t, RWKV_BWD_TM)
    nh = d // HEAD
    hb = tm // 8
    lp, gp = LORA_PAD, GATE_PAD

    def body(x_ref, halo_ref, mod_ref, vec_ref, e_ref, et_ref, wrkv, w1, a1, g1, w2, a2, g2,
             drp_ref, dk2p_ref, dvp_ref, dg_ref, drs_ref, dld_ref, dk2s_ref, dvs_ref, das_ref, dbs_ref,
             dh_ref, dhp_ref, xr_ref, xw_ref, xk_ref, xv_ref, xa_ref, xg_ref, dr_ref, dk_ref, dv_ref,
             dtw_ref, dta_ref, dtg_ref, tw2_ref, ta_ref, sg_ref, dlw_ref, dla_ref, dgb_ref, small_ref,
             s0, s1, s2, s3, s4, s5):
        e, ti = pl.program_id(0), pl.program_id(1)
        _, _, xx, xs = _rwkv_pre_core(x_ref, halo_ref, mod_ref, vec_ref, ti)
        xb, r, k, v, tw2, ta, sg, lw, la, _ = _rwkv_proj(xs, wrkv, w1, a1, g1, w2, a2, g2)
        for ref, val in zip((xr_ref, xw_ref, xk_ref, xv_ref, xa_ref, xg_ref), xb):
            ref[0] = val
        headsum = _make_headsum(e_ref[...], et_ref[...])
        drs, dld, dk2s, dvs, das, dbs_ = (_from_heads(ref, scr, nh) for ref, scr in
                                          ((drs_ref, s0), (dld_ref, s1), (dk2s_ref, s2), (dvs_ref, s3),
                                           (das_ref, s4), (dbs_ref, s5)))
        elem = functools.partial(_rwkv_elem, r, headsum=headsum)
        _, vjp = jax.vjp(elem, k, lw, la, vec_ref[6:7], vec_ref[7:8], vec_ref[8:9], vec_ref[9:10])
        dk, dlw, dla, dw0, da0, dkk, dka = vjp((dld, dk2p_ref[0] + dk2s, das, dbs_))
        dr = drp_ref[0] + drs
        dv = dvp_ref[0] + dvs
        dgv = dg_ref[0]
        dtg = _dot_nt(dgv, g2[...]) * sg * (1.0 - sg)
        dtw = _dot_nt(dlw, w2[...]) * (1.0 - tw2 * tw2)
        dta = _dot_nt(dla, a2[...])
        dr_ref[0], dk_ref[0], dv_ref[0] = dr.astype(BF16), dk.astype(BF16), dv.astype(BF16)
        dtw_ref[0], dta_ref[0], dtg_ref[0] = dtw.astype(BF16), dta.astype(BF16), dtg.astype(BF16)
        tw2_ref[0], ta_ref[0], sg_ref[0] = tw2.astype(BF16), ta.astype(BF16), sg.astype(BF16)
        dlw_ref[0], dla_ref[0], dgb_ref[0] = dlw.astype(BF16), dla.astype(BF16), dgv.astype(BF16)
        dxs = (_dot_nt(dr, wrkv[:, 0:d]), _dot_nt(dtw, w1[...]), _dot_nt(dk, wrkv[:, d:2 * d]),
               _dot_nt(dv, wrkv[:, 2 * d:3 * d]), _dot_nt(dta, a1[...]), _dot_nt(dtg, g1[...]))

        @pl.when(_first(e, ti))
        def _():
            small_ref[...] = jnp.zeros_like(small_ref)

        total = jnp.zeros((tm, d), F32)
        dhp = jnp.zeros((tm, d), F32)
        for i, dxi in enumerate(dxs):
            total += dxi
            dhp += dxi * vec_ref[i:i + 1]
            small_ref[i:i + 1] += _colsum(dxi * xx)
        dh_ref[0], dhp_ref[0] = total - dhp, dhp
        small_ref[6:7] += dw0
        small_ref[7:8] += da0
        small_ref[8:9] += dkk
        small_ref[9:10] += dka

    halo = pl.BlockSpec((1, 8, d), lambda e, i: (e, jnp.maximum(i * hb - 1, 0), 0))
    tokd, tokl, tokg = _tok(tm, d), _tok(tm, lp), _tok(tm, gp)
    bf = lambda w: _sds((bl, t, w), BF16)
    return pl.pallas_call(
        body, name="rwkv_pre_bwd", grid=(bl, t // tm),
        in_specs=[tokd, halo, _per_example(8, d), _whole(vec.shape), _whole(e_ind.shape), _whole(et_ind.shape)]
        + _rwkv_weight_specs(weights) + [tokd] * 4 + [_heads(nh, tm)] * 6,
        out_specs=(tokd, tokd) + (tokd,) * 6 + (tokd,) * 3 + (tokl, tokl, tokg, tokl, tokl, tokg)
        + (tokd, tokd, tokd, _whole((N_VEC, d))),
        out_shape=(_sds(x.shape), _sds(x.shape)) + (bf(d),) * 9 + (bf(lp), bf(lp), bf(gp), bf(lp), bf(lp), bf(gp))
        + (bf(d), bf(d), bf(d), _sds((N_VEC, d))),
        scratch_shapes=[pltpu.VMEM((tm, d), F32)] * 6,
        compiler_params=_cparams(("arbitrary", "arbitrary")),
    )(x, x, mod, vec, e_ind, et_ind, *weights, dr_p, dk2_p, dv_p, dg, dr_s, dld, dk2_s, dv_s, das, dbs)


def _norm_bwd(x, dxo, dh, dhprev, mod, dgate):
    bl, t, d = x.shape
    tm = min(t, RWKV_TM)
    hb = tm // 8
    last_blk = t // 8 - 1

    def body(x_ref, dxo_ref, dh_ref, dhp_ref, nxt_ref, mod_ref, dgate_ref, dx_ref, dmod_ref):
        ti = pl.program_id(1)
        xn, inv = _rms(x_ref[0])
        last = jnp.where(ti == t // tm - 1, 0.0, nxt_ref[0, 0:1])
        dh = dh_ref[0] + _shift_up(dhp_ref[0], last)
        dx_ref[0] = dxo_ref[0] + _rms_bwd(xn, inv, dh * (1.0 + mod_ref[0, 1:2]))

        @pl.when(ti == 0)
        def _():
            dmod_ref[0] = dgate_ref[0]

        dmod_ref[0, 0:1] += _colsum(dh)
        dmod_ref[0, 1:2] += _colsum(dh * xn)

    nxt = pl.BlockSpec((1, 8, d), lambda e, i: (e, jnp.minimum((i + 1) * hb, last_blk), 0))
    return pl.pallas_call(
        body, name="norm_bwd", grid=(bl, t // tm),
        in_specs=[_tok(tm, d)] * 4 + [nxt, _per_example(8, d), _per_example(8, d)],
        out_specs=(_tok(tm, d), _per_example(8, d)),
        out_shape=(_sds(x.shape), _sds((bl, 8, d))),
        compiler_params=_cparams(("arbitrary", "arbitrary")),
    )(x, dxo, dh, dhprev, dhprev, mod, dgate)


def _final(x, target, final_g):
    bl, t, d = x.shape
    tm = min(t, 512)

    def body(x_ref, tgt_ref, g_ref, dx_ref, loss_ref, dg_ref):
        e, ti = pl.program_id(0), pl.program_id(1)

        @pl.when(_first(e, ti))
        def _():
            loss_ref[...] = jnp.zeros_like(loss_ref)
            dg_ref[...] = jnp.zeros_like(dg_ref)

        xn, inv = _rms(x_ref[0])
        err = xn * g_ref[...] - tgt_ref[0]
        loss_ref[...] += (0.5 / d) * jnp.sum(err * err)
        dy = err * (1.0 / d)
        dg_ref[0:1] += _colsum(dy * xn)
        dx_ref[0] = _rms_bwd(xn, inv, dy * g_ref[...])

    return pl.pallas_call(
        body, name="final_loss", grid=(bl, t // tm),
        in_specs=[_tok(tm, d), _tok(tm, d), _whole(final_g.shape)],
        out_specs=(_tok(tm, d), _whole((8, 128)), _whole((8, d))),
        out_shape=(_sds(x.shape), _sds((8, 128)), _sds((8, d))),
        compiler_params=_cparams(("arbitrary", "arbitrary")),
    )(x, target, final_g)


def _adamw_math(w, g, m, v):
    m = ADAM_B1 * m + (1.0 - ADAM_B1) * g
    v = ADAM_B2 * v + (1.0 - ADAM_B2) * jnp.square(g)
    m_hat = m / (1.0 - ADAM_B1 ** ADAM_STEP)
    v_hat = v / (1.0 - ADAM_B2 ** ADAM_STEP)
    return -ADAM_LR * (m_hat / (jnp.sqrt(v_hat) + ADAM_EPS) + ADAM_WD * w), m, v


def _sum_parts(ref, n):
    g = ref[0].astype(F32)
    for s in range(1, n):
        g = g + ref[s].astype(F32)
    return g


def _adamw_layers(w, m, v, parts, name):
    nl, rows, c = w.shape
    tr = min(rows, 128)

    def body(w_ref, m_ref, v_ref, *refs):
        p_refs, (g_ref, d_ref, mo_ref, vo_ref) = refs[:nl], refs[nl:]
        for layer in range(nl):
            @pl.when(pl.program_id(0) == layer)
            def _(p_ref=p_refs[layer]):
                g = _sum_parts(p_ref, p_ref.shape[0])
                g_ref[...] = g
                d_ref[...], mo_ref[...], vo_ref[...] = _adamw_math(w_ref[...], g, m_ref[...], v_ref[...])

    row = pl.BlockSpec((None, tr, c), lambda l, i: (l, i, 0))
    return pl.pallas_call(
        body, name=name, grid=(nl, rows // tr),
        in_specs=[row, row, row] + [pl.BlockSpec((p.shape[0], tr, c), lambda l, i, k=k: (0, jnp.where(l == k, i, 0), 0))
                                    for k, p in enumerate(parts)],
        out_specs=(row,) * 4, out_shape=(_sds(w.shape),) * 4,
        compiler_params=_cparams(("arbitrary", "arbitrary")),
    )(w, m, v, *parts)


def _adamw_small(items, name):
    k = len(items)
    ns = [it[3].shape[0] for it in items]

    def body(*refs):
        ins, outs = refs[:4 * k], refs[4 * k:]
        for i in range(k):
            w_ref, m_ref, v_ref, p_ref = ins[4 * i:4 * i + 4]
            g = _sum_parts(p_ref, ns[i])
            outs[4 * i][...] = g
            outs[4 * i + 1][...], outs[4 * i + 2][...], outs[4 * i + 3][...] = _adamw_math(
                w_ref[...], g, m_ref[...], v_ref[...])

    flat = [a for it in items for a in it]
    res = pl.pallas_call(
        body, name=name,
        out_shape=tuple(_sds(it[0].shape) for it in items for _ in range(4)),
        compiler_params=_cparams(),
    )(*flat)
    return [tuple(res[4 * i:4 * i + 4]) for i in range(k)]


WEIGHTS = ['ada_w', 'ada_b', 'mlp_w1', 'mlp_w2', 'a_w_in', 'a_ln_g', 'a_ln_b', 'a_w_s', 'a_b_s', 'a_w_out', 'b_mu',
           'b_w_in', 'b_w0', 'b_w1', 'b_w2', 'b_a0', 'b_a1', 'b_a2', 'b_g1', 'b_g2', 'b_k_k', 'b_k_a', 'b_r_k',
           'b_ln_g', 'b_ln_b', 'b_w_out', 'final_g']
VECTORS = ['b_mu', 'b_w0', 'b_a0', 'b_k_k', 'b_k_a', 'b_ln_g', 'b_ln_b']
REPLICATED = ['a_ln_g', 'a_ln_b', 'a_w_s', 'a_b_s', 'b_r_k', 'final_g']
ROW_ALIGN = 16


def _pad_rows(a, mult):
    pad = (-a.shape[-2]) % mult
    return jnp.pad(a, [(0, 0)] * (a.ndim - 2) + [(0, pad), (0, 0)]) if pad else a


def _as2d(a):
    if a.ndim == 1:
        return a.reshape(1, -1)
    lead = 1
    for s in a.shape[:-1]:
        lead *= s
    return a.reshape(lead, a.shape[-1])


def kernel(x, c, ada_w, ada_b, mlp_w1, mlp_w2, a_w_in, a_ln_g, a_ln_b, a_w_s, a_b_s, a_w_out, b_mu, b_w_in, b_w0, b_w1, b_w2, b_a0, b_a1, b_a2, b_g1, b_g2, b_k_k, b_k_a, b_r_k, b_ln_g, b_ln_b, b_w_out, final_g, loss_target, m_ada_w, m_ada_b, m_mlp_w1, m_mlp_w2, m_a_w_in, m_a_ln_g, m_a_ln_b, m_a_w_s, m_a_b_s, m_a_w_out, m_b_mu, m_b_w_in, m_b_w0, m_b_w1, m_b_w2, m_b_a0, m_b_a1, m_b_a2, m_b_g1, m_b_g2, m_b_k_k, m_b_k_a, m_b_r_k, m_b_ln_g, m_b_ln_b, m_b_w_out, m_final_g, v_ada_w, v_ada_b, v_mlp_w1, v_mlp_w2, v_a_w_in, v_a_ln_g, v_a_ln_b, v_a_w_s, v_a_b_s, v_a_w_out, v_b_mu, v_b_w_in, v_b_w0, v_b_w1, v_b_w2, v_b_a0, v_b_a1, v_b_a2, v_b_g1, v_b_g2, v_b_k_k, v_b_k_a, v_b_r_k, v_b_ln_g, v_b_ln_b, v_b_w_out, v_final_g):
    given = dict(locals())
    w = {n: given[n] for n in WEIGHTS}
    bl, t, d = x.shape
    nl = ada_w.shape[0]
    nb = N_DEV * bl
    m_tok = bl * t
    me = 4 * lax.axis_index("x") + 2 * lax.axis_index("y") + lax.axis_index("c")

    bf = lambda a: a.astype(BF16)
    vec_loc = _pad_rows(jnp.concatenate([_as2d(w[n]) for n in VECTORS], axis=0), ROW_ALIGN)
    g_c, g_vec, g_a_in, g_a_out = _gather_call((c, vec_loc, bf(a_w_in[0]), bf(a_w_out[0])), "gather_first")

    c_all = g_c.reshape(nb, d)
    cols = ada_w.shape[2]
    ada_b_cols = lax.dynamic_slice(ada_b, (0, me * cols), (nl, cols)).reshape(nl, 1, cols)
    mod_cols = _ada_fwd(c_all, ada_w, ada_b_cols)
    mod_full = jnp.moveaxis(_gather_call((mod_cols,), "gather_mod")[0], 0, 2).reshape(nl, nb, 6 * d)
    mod_mine = lax.dynamic_slice(mod_full, (0, me * bl, 0), (nl, bl, 6 * d)).reshape(nl, bl, 6, d)
    mod_mix = jnp.pad(mod_mine[:, :, 0:3], ((0, 0), (0, 0), (0, 5), (0, 0)))
    mod_mlp = jnp.pad(mod_mine[:, :, 3:6], ((0, 0), (0, 0), (0, 5), (0, 0)))

    def unshard(g, ax):
        g = jnp.moveaxis(g, 0, ax)
        return g.reshape(g.shape[:ax] + (g.shape[ax] * g.shape[ax + 1],) + g.shape[ax + 2:])

    lora_names = ['b_w1', 'b_a1', 'b_g1', 'b_w2', 'b_a2', 'b_g2']
    lora_pack = jnp.concatenate([bf(w[n]).reshape(-1) for n in lora_names]).reshape(-1, 128)
    full = {'a_w_in': unshard(g_a_in, 1), 'a_w_out': unshard(g_a_out, 0)}
    n_vec_rows = sum(_as2d(w[n]).shape[0] for n in VECTORS)
    vec = jnp.moveaxis(g_vec, 0, 1).reshape(N_VEC, d)
    vec = vec.at[n_vec_rows].set(b_r_k.reshape(d))

    e_ind, et_ind = _head_indicators(d)
    gd = d // SGU_GROUPS
    group_ind = (jnp.arange(SGU_GROUPS)[:, None] == jnp.arange(d)[None, :] // gd).astype(BF16)
    bias_full = jnp.repeat(a_b_s[0].T, gd, axis=1)
    pad_c = lambda a, n: jnp.pad(a, ((0, 0), (0, n - a.shape[1])))
    pad_r = lambda a, n: jnp.pad(a, ((0, n - a.shape[0]), (0, 0)))
    sgu_args = (full['a_w_in'], a_ln_g, a_ln_b, a_w_s[0], bias_full, full['a_w_out'])

    x0 = x
    (x1, mix_a), (g_w1_0, g_w2_0) = _sgu_fwd(x0, mod_mix[0], *sgu_args, gather=(bf(mlp_w1[0]), bf(mlp_w2[0])))
    w1_full = [unshard(g_w1_0, 1), None]
    w2_full = [unshard(g_w2_0, 0), None]
    (x2, ff0, q0), (g_w1_1, g_w2_1, g_b_in, g_b_out, g_lora) = _mlp_fwd(
        x1, mod_mlp[0], w1_full[0], w2_full[0],
        gather=(bf(mlp_w1[1]), bf(mlp_w2[1]), bf(b_w_in[0]), bf(b_w_out[0]), lora_pack))
    w1_full[1], w2_full[1] = unshard(g_w1_1, 1), unshard(g_w2_1, 0)
    full['b_w_in'], full['b_w_out'] = unshard(g_b_in, 1), unshard(g_b_out, 0)
    lora_flat, lo = g_lora.reshape(N_DEV, -1), 0
    for n, ax in zip(lora_names, (0, 0, 0, 1, 1, 1)):
        loc = w[n].shape[1:]
        full[n] = unshard(lora_flat[:, lo:lo + w[n].size].reshape((N_DEV,) + loc), ax)
        lo += w[n].size
    rwkv_w = (full['b_w_in'], pad_c(full['b_w1'], LORA_PAD), pad_c(full['b_a1'], LORA_PAD),
              pad_c(full['b_g1'], GATE_PAD), pad_r(full['b_w2'], LORA_PAD), pad_r(full['b_a2'], LORA_PAD),
              pad_r(full['b_g2'], GATE_PAD))
    r, ld, k2, v, a_s, b_s, gate = _rwkv_pre_fwd(x2, mod_mix[1], vec, e_ind, et_ind, rwkv_w)
    y, s0, tinv = _wkv_fwd(r, ld, k2, v, a_s, b_s)
    x3, mix_b = _rwkv_post_fwd(x2, y, r, k2, v, gate, mod_mix[1], vec, e_ind, et_ind, full['b_w_out'])
    (x4, ff1, q1), _ = _mlp_fwd(x3, mod_mlp[1], w1_full[1], w2_full[1])
    dx4, loss_blk, dfinal = _final(x4, loss_target, final_g.reshape(1, d))
    loss = lax.psum(loss_blk[0, 0], ("x", "y", "c"))

    tok = lambda a: a.reshape(m_tok, a.shape[-1])
    shard_rows = lambda g: g.reshape((N_DEV, g.shape[0] // N_DEV) + g.shape[1:])
    (dx3, dmod_mlp1, h_b, dff_b, dp_b), _ = _mlp_bwd(x3, dx4, ff1, q1, mod_mlp[1], w1_full[1], w2_full[1])
    gw1_1 = _matmul_tn(tok(h_b), tok(dp_b), "grad_mlp_w1_l1", col_shards=N_DEV)
    gw2_1 = shard_rows(_matmul_tn(tok(q1), tok(dff_b), "grad_mlp_w2_l1"))
    (dy, dr_p, dk2_p, dv_p, dgate_act, o_b, dmix_b, dgate_b, small_post), (rw1_1, rw2_1) = _rwkv_post_bwd(
        dx3, mix_b, y, r, k2, v, gate, mod_mix[1], vec, e_ind, et_ind, full['b_w_out'], scatter=(gw1_1, gw2_1))
    g_b_w_out = shard_rows(_matmul_tn(tok(o_b), tok(dmix_b), "grad_b_w_out"))
    dr_s, dld, dk2_s, dv_s, das, dbs = _wkv_bwd(r, ld, k2, v, a_s, b_s, s0, tinv, dy)
    (dh, dhp, xr_b, xw_b, xk_b, xv_b, xa_b, xg_b, dr_b, dk_b, dv_b, dtw_b, dta_b, dtg_b, tw2_b, ta_b, sg_b,
     dlw_b, dla_b, dg_b, small_pre) = _rwkv_pre_bwd(x2, mod_mix[1], vec, e_ind, et_ind, rwkv_w,
                                                    dr_p, dk2_p, dv_p, dgate_act, dr_s, dld, dk2_s, dv_s, das, dbs)
    g_b_w_in = jnp.concatenate([_matmul_tn(tok(xr_b), tok(dr_b), "grad_b_w_r"),
                                _matmul_tn(tok(xk_b), tok(dk_b), "grad_b_w_k"),
                                _matmul_tn(tok(xv_b), tok(dv_b), "grad_b_w_v")], axis=1)
    shard_cols = lambda g: jnp.moveaxis(g.reshape(g.shape[0], N_DEV, g.shape[1] // N_DEV), 1, 0)
    g_b_w_in = shard_cols(g_b_w_in)
    lw_, lg_ = b_w1.shape[2], b_g1.shape[2]
    small_names = ['b_w1', 'b_a1', 'b_g1', 'b_w2', 'b_a2', 'b_g2'] + VECTORS
    small_parts = [
        shard_rows(_matmul_tn(tok(xw_b), tok(dtw_b), "grad_b_w1")[:, :lw_]),
        shard_rows(_matmul_tn(tok(xa_b), tok(dta_b), "grad_b_a1")[:, :lw_]),
        shard_rows(_matmul_tn(tok(xg_b), tok(dtg_b), "grad_b_g1")[:, :lg_]),
        shard_cols(_matmul_tn(tok(tw2_b), tok(dlw_b), "grad_b_w2")[:lw_]),
        shard_cols(_matmul_tn(tok(ta_b), tok(dla_b), "grad_b_a2")[:lw_]),
        shard_cols(_matmul_tn(tok(sg_b), tok(dg_b), "grad_b_g2")[:lg_]),
        shard_cols(jnp.concatenate([small_pre[0:10], small_post[0:2]], axis=0).astype(BF16)),
    ]
    small_flat = jnp.concatenate([p.reshape(N_DEV, -1) for p in small_parts], axis=1)
    lane = 128
    small_rows = -(-small_flat.shape[1] // (lane * ROW_ALIGN)) * ROW_ALIGN
    small_pack = jnp.pad(small_flat, ((0, 0), (0, small_rows * lane - small_flat.shape[1]))).reshape(
        N_DEV, small_rows, lane)
    dx2, dmod_mix1 = _norm_bwd(x2, dx3, dh, dhp, mod_mix[1], dgate_b)
    (dx1, dmod_mlp0, h_b, dff_b, dp_b), (r_b_w_in, r_b_w_out, r_small) = _mlp_bwd(
        x1, dx2, ff0, q0, mod_mlp[0], w1_full[0], w2_full[0], scatter=(g_b_w_in, g_b_w_out, small_pack))
    gw1_0 = _matmul_tn(tok(h_b), tok(dp_b), "grad_mlp_w1_l0", col_shards=N_DEV)
    gw2_0 = shard_rows(_matmul_tn(tok(q0), tok(dff_b), "grad_mlp_w2_l0"))
    (dx0, dmod_mix0, h_b, dpre_b, z_b, dmix_b, small_sgu, d_ws, d_bs), (rw1_0, rw2_0) = _sgu_bwd(
        x0, dx1, mix_a, mod_mix[0], *sgu_args, group_ind, scatter=(gw1_0, gw2_0))
    dmod_mine = jnp.stack([jnp.concatenate([dmod_mix0[:, 0:3], dmod_mlp0[:, 0:3]], axis=1),
                           jnp.concatenate([dmod_mix1[:, 0:3], dmod_mlp1[:, 0:3]], axis=1)], axis=1)
    rep_g = {'a_ln_g': small_sgu[0:1], 'a_ln_b': small_sgu[1:2], 'a_w_s': d_ws.reshape(-1, d), 'a_b_s': d_bs.reshape(1, d),
             'b_r_k': small_post[2:3], 'final_g': dfinal[0:1]}
    rep_rows = [rep_g[n].shape[0] for n in REPLICATED]
    rep_pack = _pad_rows(jnp.concatenate([rep_g[n] for n in REPLICATED], axis=0), 8)
    g_a_w_in, (dmod_all, rep_all) = _matmul_tn(tok(h_b), tok(dpre_b), "grad_a_w_in", col_shards=N_DEV,
                                               gather=(dmod_mine.reshape(bl, nl * 6 * d), rep_pack))
    g_a_w_out = shard_rows(_matmul_tn(tok(z_b), tok(dmix_b), "grad_a_w_out"))
    r_a_w_in, r_a_w_out = _scatter_call((g_a_w_in, g_a_w_out), "scatter_sgu_grads")

    dmod_all = jnp.moveaxis(dmod_all.reshape(nb, nl, 6 * d), 0, 1)
    dmod_cols = lax.dynamic_slice(dmod_all, (0, 0, me * cols), (nl, nb, cols))
    g_ada_w, g_ada_b = _ada_bwd(c_all, dmod_cols, dmod_all)

    mom = {n: given['m_' + n] for n in WEIGHTS}
    var = {n: given['v_' + n] for n in WEIGHTS}
    out = {}
    as3d = lambda a: a.reshape((-1,) + a.shape[-2:])
    for n, parts in (('mlp_w1', [rw1_0, rw1_1]), ('mlp_w2', [rw2_0, rw2_1]), ('a_w_in', [r_a_w_in]),
                     ('a_w_out', [r_a_w_out]), ('b_w_in', [r_b_w_in]), ('b_w_out', [r_b_w_out]),
                     ('ada_w', list(g_ada_w))):
        res = _adamw_layers(as3d(w[n]), as3d(mom[n]), as3d(var[n]), parts, "adamw_" + n)
        out[n] = tuple(a.reshape(w[n].shape) for a in res)

    items, names = [], []

    def add(n, part):
        s2 = _as2d(w[n]).shape
        items.append((_as2d(w[n]), _as2d(mom[n]), _as2d(var[n]), part.reshape((part.shape[0],) + s2)))
        names.append(n)

    sflat = r_small.reshape(N_DEV, -1)
    so = 0
    for n in small_names:
        sz = w[n].size
        add(n, sflat[:, so:so + sz])
        so += sz
    ro = 0
    for n, nr in zip(REPLICATED, rep_rows):
        add(n, rep_all[:, ro:ro + nr])
        ro += nr
    add('ada_b', g_ada_b[None])
    for n, res in zip(names, _adamw_small(items, "adamw_small")):
        out[n] = tuple(a.reshape(w[n].shape) for a in res)

    return (loss, dx0, *[out[n][0] for n in WEIGHTS], *[out[n][1] for n in WEIGHTS],
            *[out[n][2] for n in WEIGHTS], *[out[n][3] for n in WEIGHTS])
```

```python
import functools

import jax
import jax.numpy as jnp
from jax import lax
from jax.experimental import pallas as pl
from jax.experimental.pallas import tpu as pltpu

F32 = jnp.float32
BF16 = jnp.bfloat16

N_DEV = 8
RMS_EPS = 1e-6
LN_EPS = 1e-5
HEAD = 64
GN_EPS = HEAD * 1e-5
L2_EPS = 1e-12
DECAY_SCALE = 0.6065306597126334
SGU_CHUNK = 128
SGU_GROUPS = 8
WKV_CHUNK = 64
WKV_HEADS_PER_STEP = 16
WKV_EXAMPLES_PER_STEP = 2
LORA_PAD = 128
GATE_PAD = 256
ADAM_LR, ADAM_B1, ADAM_B2, ADAM_EPS, ADAM_WD, ADAM_STEP = 0.001, 0.9, 0.999, 1e-08, 0.01, 10
VMEM_LIMIT = 56 * 1024 * 1024


def _cparams(sem=None, **kw):
    if sem is not None:
        kw["dimension_semantics"] = sem
    return pltpu.CompilerParams(vmem_limit_bytes=VMEM_LIMIT, **kw)


def _dot(a, b):
    return jnp.dot(a.astype(BF16), b.astype(BF16), preferred_element_type=F32)


def _dot_nt(a, b):
    return lax.dot_general(a.astype(BF16), b.astype(BF16), (((1,), (1,)), ((), ())), preferred_element_type=F32)


def _dot_tn(a, b):
    return lax.dot_general(a.astype(BF16), b.astype(BF16), (((0,), (0,)), ((), ())), preferred_element_type=F32)


def _bdot(a, b, dims):
    return lax.dot_general(a.astype(BF16), b.astype(BF16), (dims, ((0,), (0,))), preferred_element_type=F32)


@jax.custom_vjp
def _tri_sum(tri, tri_t, x):
    hi = x.astype(BF16)
    lo = (x - hi.astype(F32)).astype(BF16)
    dn = (((2,), (1,)), ((0,), (0,)))
    return (lax.dot_general(tri, hi, dn, preferred_element_type=F32)
            + lax.dot_general(tri, lo, dn, preferred_element_type=F32))


_tri_sum.defvjp(lambda tri, tri_t, x: (_tri_sum(tri, tri_t, x), (tri, tri_t)),
                lambda res, g: (jnp.zeros_like(res[0]), jnp.zeros_like(res[1]), _tri_sum(res[1], res[0], g)))


@jax.custom_vjp
def _bmm_nn(a, b):
    return _bdot(a, b, ((2,), (1,)))


@jax.custom_vjp
def _bmm_nt(a, b):
    return _bdot(a, b, ((2,), (2,)))


@jax.custom_vjp
def _bmm_tn(a, b):
    return _bdot(a, b, ((1,), (1,)))


_bmm_nn.defvjp(lambda a, b: (_bmm_nn(a, b), (a, b)), lambda res, g: (_bmm_nt(g, res[1]), _bmm_tn(res[0], g)))
_bmm_nt.defvjp(lambda a, b: (_bmm_nt(a, b), (a, b)), lambda res, g: (_bmm_nn(g, res[1]), _bmm_tn(g, res[0])))
_bmm_tn.defvjp(lambda a, b: (_bmm_tn(a, b), (a, b)), lambda res, g: (_bmm_nt(res[1], g), _bmm_nn(res[0], g)))


def _tri_inverse(p):
    n = p.shape[1]
    row = lax.broadcasted_iota(jnp.int32, (n, n), 0)
    col = lax.broadcasted_iota(jnp.int32, (n, n), 1)
    tinv = jnp.where(row == col, 1.0, 0.0).astype(F32)[None] + p
    for _ in range(max(1, (n - 1).bit_length()) - 1):
        p = _bmm_nn(p, p)
        tinv = tinv + _bmm_nn(tinv, p)
    return tinv.astype(BF16)


def _tri_solve_fwd(tinv, p, rhs):
    u = _bmm_nn(tinv, rhs)
    return u, (tinv, u)


def _tri_solve_bwd(res, du):
    tinv, u = res
    drhs = _bmm_tn(tinv, du)
    return jnp.zeros_like(tinv), _bmm_nt(drhs, u), drhs


@jax.custom_vjp
def _tri_solve(tinv, p, rhs):
    return _tri_solve_fwd(tinv, p, rhs)[0]


_tri_solve.defvjp(_tri_solve_fwd, _tri_solve_bwd)


def _wkv_chunk(s0, r, ld, k, v, a, b, tinv=None):
    nh, n, _ = r.shape
    row = lax.broadcasted_iota(jnp.int32, (n, n), 0)
    col = lax.broadcasted_iota(jnp.int32, (n, n), 1)
    incl = row >= col
    strict = row > col
    lower = jnp.broadcast_to(jnp.where(incl, 1.0, 0.0).astype(BF16), (nh, n, n))
    upper = jnp.broadcast_to(jnp.where(row <= col, 1.0, 0.0).astype(BF16), (nh, n, n))
    c = _tri_sum(lower, upper, ld)
    c_end = c[:, n - 1:n, :]
    ec, enc, ecx, eend = jnp.exp(c), jnp.exp(-c), jnp.exp(c - ld), jnp.exp(c_end - c)
    ar = jnp.concatenate([a * ecx, r * ec], axis=1)
    mask = jnp.concatenate([strict, incl], axis=0)[None]
    m_b = jnp.where(mask, _bmm_nt(ar, b * enc), 0.0)
    m_k = jnp.where(mask, _bmm_nt(ar, k * enc), 0.0)
    a_ab, a_rb = m_b[:, :n], m_b[:, n:]
    base = _bmm_nt(ar, s0) + _bmm_nn(m_k, v)
    if tinv is None:
        tinv = lax.stop_gradient(_tri_inverse(a_ab))
    u = _tri_solve(tinv, a_ab, base[:, :n])
    y = base[:, n:] + _bmm_nn(a_rb, u)
    s1 = s0 * jnp.exp(c_end) + _bmm_tn(jnp.concatenate([u, v], axis=1), jnp.concatenate([b * eend, k * eend], axis=1))
    return y, s1, tinv


def _wkv_specs(bl, nh, t):
    eb, hb, lc = min(bl, WKV_EXAMPLES_PER_STEP), min(nh, WKV_HEADS_PER_STEP), WKV_CHUNK
    return eb, hb, lc, (bl // eb, nh // hb, t // lc)


def _wkv_fwd(r, ld, k, v, a, b):
    bl, nh, t, n = r.shape
    eb, hb, lc, grid = _wkv_specs(bl, nh, t)
    nc = t // lc
    nb = eb * hb

    def body(r_ref, ld_ref, k_ref, v_ref, a_ref, b_ref, y_ref, s0_ref, tinv_ref, s_scr):
        @pl.when(pl.program_id(2) == 0)
        def _():
            s_scr[...] = jnp.zeros_like(s_scr)

        s0 = s_scr[...]
        s0_ref[:, :, 0] = s0.reshape(eb, hb, n, n)
        y, s1, tinv = _wkv_chunk(
            s0, *(ref[...].reshape(nb, lc, n) for ref in (r_ref, ld_ref, k_ref, v_ref, a_ref, b_ref)))
        y_ref[...] = y.reshape(eb, hb, lc, n)
        tinv_ref[:, :, 0] = tinv.reshape(eb, hb, lc, lc)
        s_scr[...] = s1

    seq = pl.BlockSpec((eb, hb, lc, n), lambda e, h, c: (e, h, c, 0))
    return pl.pallas_call(
        body, name="wkv_fwd", grid=grid,
        in_specs=[seq] * 6,
        out_specs=(seq, pl.BlockSpec((eb, hb, 1, n, n), lambda e, h, c: (e, h, c, 0, 0)),
                   pl.BlockSpec((eb, hb, 1, lc, lc), lambda e, h, c: (e, h, c, 0, 0))),
        out_shape=(jax.ShapeDtypeStruct((bl, nh, t, n), F32), jax.ShapeDtypeStruct((bl, nh, nc, n, n), F32),
                   jax.ShapeDtypeStruct((bl, nh, nc, lc, lc), BF16)),
        scratch_shapes=[pltpu.VMEM((nb, n, n), F32)],
        compiler_params=_cparams(("arbitrary", "arbitrary", "arbitrary")),
    )(r, ld, k, v, a, b)


def _wkv_bwd(r, ld, k, v, a, b, s0_all, tinv_all, dy):
    bl, nh, t, n = r.shape
    eb, hb, lc, grid = _wkv_specs(bl, nh, t)
    nc = t // lc
    nb = eb * hb

    def body(r_ref, ld_ref, k_ref, v_ref, a_ref, b_ref, s0_ref, tinv_ref, dy_ref,
             dr_ref, dld_ref, dk_ref, dv_ref, da_ref, db_ref, ds_scr):
        @pl.when(pl.program_id(2) == 0)
        def _():
            ds_scr[...] = jnp.zeros_like(ds_scr)

        args = (s0_ref[:, :, 0].reshape(nb, n, n),) + tuple(
            ref[...].reshape(nb, lc, n) for ref in (r_ref, ld_ref, k_ref, v_ref, a_ref, b_ref))
        tinv = tinv_ref[:, :, 0].reshape(nb, lc, lc)
        _, vjp = jax.vjp(lambda *xs: _wkv_chunk(*xs, tinv=tinv)[:2], *args)
        ds0, *dseq = vjp((dy_ref[...].reshape(nb, lc, n), ds_scr[...]))
        ds_scr[...] = ds0
        for ref, val in zip((dr_ref, dld_ref, dk_ref, dv_ref, da_ref, db_ref), dseq):
            ref[...] = val.reshape(eb, hb, lc, n)

    seq = pl.BlockSpec((eb, hb, lc, n), lambda e, h, c: (e, h, nc - 1 - c, 0))
    st = pl.BlockSpec((eb, hb, 1, n, n), lambda e, h, c: (e, h, nc - 1 - c, 0, 0))
    ti = pl.BlockSpec((eb, hb, 1, lc, lc), lambda e, h, c: (e, h, nc - 1 - c, 0, 0))
    out = jax.ShapeDtypeStruct((bl, nh, t, n), F32)
    return pl.pallas_call(
        body, name="wkv_bwd", grid=grid,
        in_specs=[seq] * 6 + [st, ti, seq],
        out_specs=(seq,) * 6, out_shape=(out,) * 6,
        scratch_shapes=[pltpu.VMEM((nb, n, n), F32)],
        compiler_params=_cparams(("arbitrary", "arbitrary", "arbitrary")),
    )(r, ld, k, v, a, b, s0_all, tinv_all, dy)


def _scatter_copies(x_refs, o_refs, send_sems, recv_sems, local_sems):
    pos = (lax.axis_index("x"), lax.axis_index("y"), lax.axis_index("c"))
    me = 4 * pos[0] + 2 * pos[1] + pos[2]

    def descriptors():
        sends, arrivals, local = [], [], []
        for i, (x_ref, o_ref) in enumerate(zip(x_refs, o_refs)):
            for m in range(1, N_DEV):
                p = tuple(1 - pos[a] if (m >> (2 - a)) & 1 else pos[a] for a in range(3))
                pidx = 4 * p[0] + 2 * p[1] + p[2]
                k = (N_DEV - 1) * i + m - 1
                for dst, out in ((o_ref.at[me], sends), (o_ref.at[pidx], arrivals)):
                    out.append(pltpu.make_async_remote_copy(
                        src_ref=x_ref.at[pidx], dst_ref=dst, send_sem=send_sems.at[k], recv_sem=recv_sems.at[k],
                        device_id=p, device_id_type=pl.DeviceIdType.MESH))
            local.append(pltpu.make_async_copy(x_ref.at[me], o_ref.at[me], local_sems.at[i]))
        return sends, arrivals, local

    def start():
        sends, _, local = descriptors()
        for cp in local + sends:
            cp.start()

    def finish():
        sends, arrivals, local = descriptors()
        for cp in arrivals:
            cp.wait_recv()
        for cp in sends:
            cp.wait_send()
        for cp in local:
            cp.wait()

    return start, finish


def _gather_copies(x_refs, o_refs, send_sems, recv_sems, local_sems):
    pos = (lax.axis_index("x"), lax.axis_index("y"), lax.axis_index("c"))
    me = 4 * pos[0] + 2 * pos[1] + pos[2]
    far = (2, 4, 6)

    def peer_of(m):
        p = tuple(1 - pos[a] if (m >> (2 - a)) & 1 else pos[a] for a in range(3))
        return p, 4 * p[0] + 2 * p[1] + p[2]

    sibling, _ = peer_of(1)

    def copy(i, k, src, slot, to):
        return pltpu.make_async_remote_copy(
            src_ref=src, dst_ref=o_refs[i].at[slot], send_sem=send_sems.at[(N_DEV - 1) * i + k],
            recv_sem=recv_sems.at[(N_DEV - 1) * i + k], device_id=to, device_id_type=pl.DeviceIdType.MESH)

    def direct(i):
        return [copy(i, m - 1, x_refs[i], me, peer_of(m)[0]) for m in (1,) + far]

    def local(i):
        return pltpu.make_async_copy(x_refs[i], o_refs[i].at[me], local_sems.at[i])

    def start():
        for i in range(len(x_refs)):
            local(i).start()
            for cp in direct(i):
                cp.start()

    def finish():
        n = len(x_refs)
        relays = []
        for i in range(n):
            for m in far:
                origin = peer_of(m)[1]
                copy(i, m - 1, x_refs[i], origin, sibling).wait_recv()
                fwd = copy(i, m, o_refs[i].at[origin], origin, sibling)
                fwd.start()
                relays.append(fwd)
        for i in range(n):
            copy(i, 0, x_refs[i], peer_of(1)[1], sibling).wait_recv()
            for m in far:
                copy(i, m, x_refs[i], peer_of(m ^ 1)[1], sibling).wait_recv()
        for i in range(n):
            for cp in direct(i):
                cp.wait_send()
            local(i).wait()
        for cp in relays:
            cp.wait_send()

    return start, finish


def _scatter_scratch(n):
    return [pltpu.SemaphoreType.DMA(((N_DEV - 1) * n,)), pltpu.SemaphoreType.DMA(((N_DEV - 1) * n,)),
            pltpu.SemaphoreType.DMA((n,))]


_ANY = pl.BlockSpec(memory_space=pl.ANY)


def _scatter_call(arrays, name):
    n = len(arrays)

    def body(*refs):
        start, finish = _scatter_copies(refs[:n], refs[n:2 * n], *refs[2 * n:])
        start()
        finish()

    return pl.pallas_call(
        body, name=name, in_specs=[_ANY] * n, out_specs=(_ANY,) * n,
        out_shape=tuple(_sds(a.shape, a.dtype) for a in arrays), scratch_shapes=_scatter_scratch(n),
    )(*arrays)


def _gather_call(arrays, name):
    n = len(arrays)

    def body(*refs):
        start, finish = _gather_copies(refs[:n], refs[n:2 * n], *refs[2 * n:])
        start()
        finish()

    return pl.pallas_call(
        body, name=name, in_specs=[_ANY] * n, out_specs=(_ANY,) * n,
        out_shape=tuple(_sds((N_DEV,) + a.shape, a.dtype) for a in arrays), scratch_shapes=_scatter_scratch(n),
    )(*arrays)


def _call_with_scatter(body, *, name, grid, in_specs, out_specs, out_shape, scratch_shapes, operands,
                       scatter=(), gather=()):
    assert not (scatter and gather)
    carried = tuple(scatter) or tuple(gather)
    copies = _scatter_copies if scatter else _gather_copies
    recv_shapes = tuple(_sds(a.shape if scatter else (N_DEV,) + a.shape, a.dtype) for a in carried)
    nc, n_in, n_out, n_scr = len(carried), len(in_specs), len(out_specs), len(scratch_shapes)
    if nc == 0:
        return pl.pallas_call(
            body, name=name, grid=grid, in_specs=list(in_specs), out_specs=tuple(out_specs),
            out_shape=tuple(out_shape), scratch_shapes=list(scratch_shapes),
            compiler_params=_cparams(("arbitrary",) * len(grid)))(*operands), ()

    def wrapped(*refs):
        ins, refs = refs[:n_in], refs[n_in:]
        c_in, refs = refs[:nc], refs[nc:]
        outs, refs = refs[:n_out], refs[n_out:]
        c_out, refs = refs[:nc], refs[nc:]
        scr, sems = refs[:n_scr], refs[n_scr:]
        ids = [pl.program_id(a) for a in range(len(grid))]
        first = functools.reduce(jnp.logical_and, [i == 0 for i in ids])
        last = functools.reduce(jnp.logical_and, [i == g - 1 for i, g in zip(ids, grid)])
        start, finish = copies(c_in, c_out, *sems)
        pl.when(first)(start)
        body(*ins, *outs, *scr)
        pl.when(last)(finish)

    res = pl.pallas_call(
        wrapped, name=name, grid=grid,
        in_specs=list(in_specs) + [_ANY] * nc, out_specs=tuple(out_specs) + (_ANY,) * nc,
        out_shape=tuple(out_shape) + recv_shapes,
        scratch_shapes=list(scratch_shapes) + _scatter_scratch(nc),
        compiler_params=_cparams(("arbitrary",) * len(grid)),
    )(*operands, *carried)
    return res[:n_out], res[n_out:]


def _rms(x):
    inv = lax.rsqrt(jnp.mean(x * x, axis=-1, keepdims=True) + RMS_EPS)
    return x * inv, inv


def _rms_bwd(xn, inv, dxn):
    return inv * (dxn - xn * jnp.mean(dxn * xn, axis=-1, keepdims=True))


def _colsum(x):
    return jnp.sum(x, axis=0, keepdims=True)


def _sigmoid(x):
    return 0.5 * (jnp.tanh(0.5 * x) + 1.0)


def _split_bf16(x):
    hi = x.astype(BF16)
    return hi, (x - hi.astype(F32)).astype(BF16)


def _dot_split(x, e):
    hi, lo = _split_bf16(x)
    return jnp.dot(hi, e, preferred_element_type=F32) + jnp.dot(lo, e, preferred_element_type=F32)


@jax.custom_vjp
def _headsum(x, e, et):
    return _dot_split(_dot_split(x, e), et)


_headsum.defvjp(lambda x, e, et: (_headsum(x, e, et), (e, et)),
                lambda res, g: (_headsum(g, *res), jnp.zeros_like(res[0]), jnp.zeros_like(res[1])))


def _make_headsum(e, et):
    return lambda x: _headsum(x, e, et)


def _head_indicators(d):
    e = (jnp.arange(d)[:, None] // HEAD == jnp.arange(128)[None, :]).astype(BF16)
    return e, e.T


def _rwkv_elem(r, k, lw, la, w0, a0, k_k, k_a, headsum):
    ld = -DECAY_SCALE * _sigmoid(w0 + lw)
    a = _sigmoid(a0 + la)
    kkp = k * k_k
    kk = kkp * lax.rsqrt(jnp.maximum(headsum(kkp * kkp), L2_EPS * L2_EPS))
    k2 = k * (1.0 + (a - 1.0) * k_a)
    del r
    return ld, k2, -kk, kk * a


def _rwkv_post(y, r, k2, v, g, ln_g, ln_b, r_k, headsum):
    m = headsum(y) * (1.0 / HEAD)
    yc = y - m
    var = headsum(yc * yc) * (1.0 / HEAD)
    yn = yc * lax.rsqrt(var + GN_EPS)
    bonus = headsum(r * k2 * r_k) * v
    return (yn * ln_g + ln_b + bonus) * g


def _shift_down(h, first_row):
    rolled = pltpu.roll(h, 1, 0)
    row = lax.broadcasted_iota(jnp.int32, h.shape, 0)
    return jnp.where(row == 0, first_row, rolled)


def _shift_up(h, last_row):
    n = h.shape[0]
    rolled = pltpu.roll(h, n - 1, 0)
    row = lax.broadcasted_iota(jnp.int32, h.shape, 0)
    return jnp.where(row == n - 1, last_row, rolled)


def _gelu(p):
    return 0.5 * p * (1.0 + lax.erf(p * 0.7071067811865476))


def _gelu_grad(p):
    return 0.5 * (1.0 + lax.erf(p * 0.7071067811865476)) + p * jnp.exp(-0.5 * p * p) * 0.3989422804014327


def _tok(tm, d):
    return pl.BlockSpec((1, tm, d), lambda e, t, *_: (e, t, 0))


def _per_example(rows, d):
    return pl.BlockSpec((1, rows, d), lambda e, t, *_: (e, 0, 0))


def _whole(shape):
    nd = len(shape)
    return pl.BlockSpec(tuple(shape), lambda *_: (0,) * nd)


def _heads(nh, tm):
    return pl.BlockSpec((1, nh, tm, HEAD), lambda e, t, *_: (e, 0, t, 0))


def _sds(shape, dtype=F32):
    return jax.ShapeDtypeStruct(tuple(shape), dtype)


def _add_rows(ref, first, rows):
    @pl.when(first)
    def _():
        ref[0] = jnp.zeros(ref.shape[1:], ref.dtype)

    for i, r in enumerate(rows):
        ref[0, i:i + 1] += r


def _first(e, t):
    return jnp.logical_and(e == 0, t == 0)


def _ada_fwd(c_all, ada_w, ada_b_cols):
    nl, d, cols = ada_w.shape
    nb = c_all.shape[0]

    def body(c_ref, w_ref, b_ref, o_ref):
        c = c_ref[...]
        cond = c * _sigmoid(c)
        for i in range(nl):
            o_ref[i] = _dot(cond, w_ref[i]) + b_ref[i]

    return pl.pallas_call(
        body, name="ada_fwd", out_shape=_sds((nl, nb, cols)),
        compiler_params=_cparams(),
    )(c_all, ada_w, ada_b_cols)


def _ada_bwd(c_all, dmod_cols, dmod_full):
    nl, nb, cols = dmod_cols.shape
    d = c_all.shape[1]

    def body(c_ref, g_ref, f_ref, b_ref, *o_refs):
        c = c_ref[...]
        cond = c * _sigmoid(c)
        for i in range(nl):
            o_refs[i][0] = _dot_tn(cond, g_ref[i])
            b_ref[i:i + 1] = jnp.sum(f_ref[i], axis=0, keepdims=True)

    res = pl.pallas_call(
        body, name="ada_bwd", out_shape=(_sds((nl, dmod_full.shape[2])),) + (_sds((1, d, cols)),) * nl,
        compiler_params=_cparams(),
    )(c_all, dmod_cols, dmod_full)
    return res[1:], res[0]


def _matmul_tn(a, b, name, col_shards=None, gather=()):
    m, ka = a.shape
    n = b.shape[1]
    tm = min(m, 2048)
    tk = min(ka, 1024)
    tn = min(n, 1024)
    steps = m // tm
    if col_shards:
        cs = n // col_shards
        spt = tn // cs
        out_spec = pl.BlockSpec((spt, tk, cs), lambda i, j, s: (j, i, 0))
        out_shape = _sds((col_shards, ka, cs), BF16)
    else:
        out_spec = pl.BlockSpec((tk, tn), lambda i, j, s: (i, j))
        out_shape = _sds((ka, n), BF16)

    def body(a_ref, b_ref, o_ref, acc):
        s = pl.program_id(2)

        @pl.when(s == 0)
        def _():
            acc[...] = jnp.zeros_like(acc)

        acc[...] += _dot_tn(a_ref[...], b_ref[...])

        @pl.when(s == steps - 1)
        def _():
            if col_shards:
                for q in range(spt):
                    o_ref[q] = acc[:, q * cs:(q + 1) * cs].astype(BF16)
            else:
                o_ref[...] = acc[...].astype(BF16)

    res, got = _call_with_scatter(
        body, name=name, grid=(ka // tk, n // tn, steps),
        in_specs=[pl.BlockSpec((tm, tk), lambda i, j, s: (s, i)), pl.BlockSpec((tm, tn), lambda i, j, s: (s, j))],
        out_specs=(out_spec,), out_shape=(out_shape,),
        scratch_shapes=[pltpu.VMEM((tk, tn), F32)], operands=(a, b), gather=gather)
    return (res[0], got) if gather else res[0]


MLP_FWD_TM = 1024
MLP_FJ = 1024
MLP_BWD_TM = 512
MLP_BWD_FJ = 1024


def _mlp_fwd(x, mod, w1, w2, gather=()):
    bl, t, d = x.shape
    f = w1.shape[1]
    tm, fj = min(t, MLP_FWD_TM), min(f, MLP_FJ)
    nj = f // fj

    def body(x_ref, mod_ref, w1_ref, w2_ref, xo_ref, ff_ref, q_ref, h_scr, acc):
        j = pl.program_id(2)

        @pl.when(j == 0)
        def _():
            xn, _ = _rms(x_ref[0])
            h_scr[...] = (xn * (1.0 + mod_ref[0, 1:2]) + mod_ref[0, 0:1]).astype(BF16)
            acc[...] = jnp.zeros_like(acc)

        p = jnp.dot(h_scr[...], w1_ref[...], preferred_element_type=F32)
        q = jnp.square(jnp.maximum(p, 0.0)).astype(BF16)
        q_ref[0] = q
        acc[...] += jnp.dot(q, w2_ref[...], preferred_element_type=F32)

        @pl.when(j == nj - 1)
        def _():
            ff_ref[0] = acc[...]
            xo_ref[0] = x_ref[0] + mod_ref[0, 2:3] * acc[...]

    return _call_with_scatter(
        body, name="mlp_fwd", grid=(bl, t // tm, nj),
        in_specs=[_tok(tm, d), _per_example(8, d),
                  pl.BlockSpec((d, fj), lambda e, i, j: (0, j)), pl.BlockSpec((fj, d), lambda e, i, j: (j, 0))],
        out_specs=(_tok(tm, d), _tok(tm, d), pl.BlockSpec((1, tm, fj), lambda e, i, j: (e, i, j))),
        out_shape=(_sds(x.shape), _sds(x.shape), _sds((bl, t, f), BF16)),
        scratch_shapes=[pltpu.VMEM((tm, d), BF16), pltpu.VMEM((tm, d), F32)],
        operands=(x, mod, w1, w2), gather=gather)


def _mlp_bwd(x, dxo, ff, q, mod, w1, w2, scatter=()):
    bl, t, d = x.shape
    f = w1.shape[1]
    tm, fj = min(t, MLP_BWD_TM), min(f, MLP_BWD_FJ)
    nj = f // fj

    def body(x_ref, dxo_ref, ff_ref, q_ref, mod_ref, w1_ref, w2_ref,
             dx_ref, dmod_ref, h_ref, dff_ref, dp_ref, acc):
        ti, j = pl.program_id(1), pl.program_id(2)

        @pl.when(j == 0)
        def _():
            xn, _ = _rms(x_ref[0])
            h_ref[0] = (xn * (1.0 + mod_ref[0, 1:2]) + mod_ref[0, 0:1]).astype(BF16)
            dff_ref[0] = (mod_ref[0, 2:3] * dxo_ref[0]).astype(BF16)
            acc[...] = jnp.zeros_like(acc)

        rl = jnp.sqrt(q_ref[0].astype(F32))
        dp = (_dot_nt(dff_ref[0], w2_ref[...]) * (2.0 * rl)).astype(BF16)
        dp_ref[0] = dp
        acc[...] += _dot_nt(dp, w1_ref[...])

        @pl.when(j == nj - 1)
        def _():
            xn, inv = _rms(x_ref[0])
            dh = acc[...]
            dx_ref[0] = dxo_ref[0] + _rms_bwd(xn, inv, dh * (1.0 + mod_ref[0, 1:2]))
            _add_rows(dmod_ref, ti == 0, [_colsum(dh), _colsum(dh * xn), _colsum(dxo_ref[0] * ff_ref[0])])

    big = lambda: pl.BlockSpec((1, tm, fj), lambda e, i, j: (e, i, j))
    return _call_with_scatter(
        body, name="mlp_bwd", grid=(bl, t // tm, nj),
        in_specs=[_tok(tm, d), _tok(tm, d), _tok(tm, d), big(), _per_example(8, d),
                  pl.BlockSpec((d, fj), lambda e, i, j: (0, j)), pl.BlockSpec((fj, d), lambda e, i, j: (j, 0))],
        out_specs=(_tok(tm, d), _per_example(8, d), _tok(tm, d), _tok(tm, d), big()),
        out_shape=(_sds(x.shape), _sds((bl, 8, d)), _sds(x.shape, BF16), _sds(x.shape, BF16),
                   _sds((bl, t, f), BF16)),
        scratch_shapes=[pltpu.VMEM((tm, d), F32)],
        operands=(x, dxo, ff, q, mod, w1, w2), scatter=scatter)


SGU_TM = 256


def _sgu_core(x, mod_ref, win_ref, lng, lnb, ws_ref, bias_ref):
    tm, d = x.shape
    xn, inv = _rms(x)
    h = (xn * (1.0 + mod_ref[0, 1:2]) + mod_ref[0, 0:1]).astype(BF16)
    pre = jnp.dot(h, win_ref[...], preferred_element_type=F32)
    uv = _gelu(pre)
    u, v = uv[:, :d], uv[:, d:]
    mu = jnp.mean(v, axis=-1, keepdims=True)
    vc = v - mu
    rstd = lax.rsqrt(jnp.mean(vc * vc, axis=-1, keepdims=True) + LN_EPS)
    vhat = vc * rstd
    vln = vhat * lng + lnb
    gd = d // SGU_GROUPS
    rows = []
    for c in range(tm // SGU_CHUNK):
        cols = []
        for g in range(SGU_GROUPS):
            cols.append(_dot(ws_ref[g], vln[c * SGU_CHUNK:(c + 1) * SGU_CHUNK, g * gd:(g + 1) * gd]))
        rows.append(jnp.concatenate(cols, axis=1) + bias_ref[...])
    sv = jnp.concatenate(rows, axis=0)
    return xn, inv, h, pre, u, vhat, rstd, vln, sv


def _sgu_masked(ws_ref, wm_scr):
    row = lax.broadcasted_iota(jnp.int32, (SGU_CHUNK, SGU_CHUNK), 0)
    col = lax.broadcasted_iota(jnp.int32, (SGU_CHUNK, SGU_CHUNK), 1)
    for g in range(SGU_GROUPS):
        wm_scr[g] = jnp.where(row >= col, ws_ref[g], 0.0).astype(BF16)


def _sgu_fwd(x, mod, w_in, ln_g, ln_b, w_s, bias_full, w_out, gather=()):
    bl, t, d = x.shape
    tm = min(t, SGU_TM)

    def body(x_ref, mod_ref, win_ref, lng_ref, lnb_ref, ws_ref, bias_ref, wout_ref, xo_ref, mix_ref, wm_scr):
        _sgu_masked(ws_ref, wm_scr)
        xt = x_ref[0]
        *_, u, _, _, _, sv = _sgu_core(xt, mod_ref, win_ref, lng_ref[...], lnb_ref[...], wm_scr, bias_ref)
        mix = _dot(u * sv, wout_ref[...])
        mix_ref[0] = mix
        xo_ref[0] = xt + mod_ref[0, 2:3] * mix

    return _call_with_scatter(
        body, name="sgu_fwd", grid=(bl, t // tm),
        in_specs=[_tok(tm, d), _per_example(8, d), _whole(w_in.shape), _whole(ln_g.shape), _whole(ln_b.shape),
                  _whole(w_s.shape), _whole(bias_full.shape), _whole(w_out.shape)],
        out_specs=(_tok(tm, d), _tok(tm, d)),
        out_shape=(_sds(x.shape), _sds(x.shape)),
        scratch_shapes=[pltpu.VMEM(w_s.shape, BF16)],
        operands=(x, mod, w_in, ln_g, ln_b, w_s, bias_full, w_out), gather=gather)


def _sgu_bwd(x, dxo, mix, mod, w_in, ln_g, ln_b, w_s, bias_full, w_out, group_ind, scatter=()):
    bl, t, d = x.shape
    tm = min(t, SGU_TM)
    gd = d // SGU_GROUPS

    def body(x_ref, dxo_ref, mix_ref, mod_ref, win_ref, lng_ref, lnb_ref, ws_ref, bias_ref, wout_ref, ind_ref,
             dx_ref, dmod_ref, h_ref, dpre_ref, z_ref, dmix_ref, small_ref, dws_ref, dbs_ref, wm_scr, dbias_scr):
        e, ti = pl.program_id(0), pl.program_id(1)
        _sgu_masked(ws_ref, wm_scr)
        xt, dxo = x_ref[0], dxo_ref[0]
        lng = lng_ref[...]
        xn, inv, h, pre, u, vhat, rstd, vln, sv = _sgu_core(xt, mod_ref, win_ref, lng, lnb_ref[...], wm_scr, bias_ref)
        h_ref[0] = h
        z_ref[0] = (u * sv).astype(BF16)
        dmix = mod_ref[0, 2:3] * dxo
        dmix_ref[0] = dmix.astype(BF16)
        dz = _dot_nt(dmix, wout_ref[...])
        du, dsv = dz * sv, dz * u

        @pl.when(_first(e, ti))
        def _():
            dws_ref[...] = jnp.zeros_like(dws_ref)
            dbias_scr[...] = jnp.zeros_like(dbias_scr)
            small_ref[...] = jnp.zeros_like(small_ref)

        row = lax.broadcasted_iota(jnp.int32, (SGU_CHUNK, SGU_CHUNK), 0)
        col = lax.broadcasted_iota(jnp.int32, (SGU_CHUNK, SGU_CHUNK), 1)
        rows = []
        for c in range(tm // SGU_CHUNK):
            rs = slice(c * SGU_CHUNK, (c + 1) * SGU_CHUNK)
            dbias_scr[...] += dsv[rs]
            cols = []
            for g in range(SGU_GROUPS):
                cs = slice(g * gd, (g + 1) * gd)
                cols.append(_dot_tn(wm_scr[g], dsv[rs, cs]))
                dws_ref[g] += jnp.where(row >= col, _dot_nt(dsv[rs, cs], vln[rs, cs]), 0.0)
            rows.append(jnp.concatenate(cols, axis=1))
        dvln = jnp.concatenate(rows, axis=0)
        small_ref[0:1] += _colsum(dvln * vhat)
        small_ref[1:2] += _colsum(dvln)
        dvhat = dvln * lng
        dv = rstd * (dvhat - jnp.mean(dvhat, axis=-1, keepdims=True)
                     - vhat * jnp.mean(dvhat * vhat, axis=-1, keepdims=True))
        dpre = (jnp.concatenate([du, dv], axis=1) * _gelu_grad(pre)).astype(BF16)
        dpre_ref[0] = dpre
        dh = _dot_nt(dpre, win_ref[...])
        dx_ref[0] = dxo + _rms_bwd(xn, inv, dh * (1.0 + mod_ref[0, 1:2]))
        _add_rows(dmod_ref, ti == 0, [_colsum(dh), _colsum(dh * xn), _colsum(dxo * mix_ref[0])])

        @pl.when(jnp.logical_and(e == bl - 1, ti == t // tm - 1))
        def _():
            hi, lo = _split_bf16(dbias_scr[...])
            ind = ind_ref[...]
            dbs_ref[...] = (lax.dot_general(ind, hi, (((1,), (1,)), ((), ())), preferred_element_type=F32)
                            + lax.dot_general(ind, lo, (((1,), (1,)), ((), ())), preferred_element_type=F32))

    return _call_with_scatter(
        body, name="sgu_bwd", grid=(bl, t // tm),
        in_specs=[_tok(tm, d), _tok(tm, d), _tok(tm, d), _per_example(8, d), _whole(w_in.shape), _whole(ln_g.shape),
                  _whole(ln_b.shape), _whole(w_s.shape), _whole(bias_full.shape), _whole(w_out.shape),
                  _whole(group_ind.shape)],
        out_specs=(_tok(tm, d), _per_example(8, d), _tok(tm, d), _tok(tm, 2 * d), _tok(tm, d), _tok(tm, d),
                   _whole((8, d)), _whole(w_s.shape), _whole((SGU_GROUPS, SGU_CHUNK))),
        out_shape=(_sds(x.shape), _sds((bl, 8, d)), _sds(x.shape, BF16), _sds((bl, t, 2 * d), BF16),
                   _sds(x.shape, BF16), _sds(x.shape, BF16), _sds((8, d)), _sds(w_s.shape),
                   _sds((SGU_GROUPS, SGU_CHUNK))),
        scratch_shapes=[pltpu.VMEM(w_s.shape, BF16), pltpu.VMEM((SGU_CHUNK, d), F32)],
        operands=(x, dxo, mix, mod, w_in, ln_g, ln_b, w_s, bias_full, w_out, group_ind), scatter=scatter)


RWKV_TM = 256
N_VEC = 16


def _rwkv_pre_core(x_ref, halo_ref, mod_ref, vec_ref, ti):
    xn, inv = _rms(x_ref[0])
    scale1, shift = 1.0 + mod_ref[0, 1:2], mod_ref[0, 0:1]
    h = xn * scale1 + shift
    hn, _ = _rms(halo_ref[0])
    hh = hn * scale1 + shift
    first = jnp.where(ti == 0, 0.0, hh[7:8])
    xx = _shift_down(h, first) - h
    xs = [h + xx * vec_ref[i:i + 1] for i in range(6)]
    return xn, inv, xx, xs


def _rwkv_proj(xs, wrkv_ref, w1_ref, a1_ref, g1_ref, w2_ref, a2_ref, g2_ref):
    d = xs[0].shape[1]
    xr, xw, xk, xv, xa, xg = [z.astype(BF16) for z in xs]
    r = jnp.dot(xr, wrkv_ref[:, 0:d], preferred_element_type=F32)
    k = jnp.dot(xk, wrkv_ref[:, d:2 * d], preferred_element_type=F32)
    v = jnp.dot(xv, wrkv_ref[:, 2 * d:3 * d], preferred_element_type=F32)
    tw2 = jnp.tanh(jnp.dot(xw, w1_ref[...], preferred_element_type=F32))
    ta = jnp.dot(xa, a1_ref[...], preferred_element_type=F32)
    sg = _sigmoid(jnp.dot(xg, g1_ref[...], preferred_element_type=F32))
    lw, la, g = _dot(tw2, w2_ref[...]), _dot(ta, a2_ref[...]), _dot(sg, g2_ref[...])
    return (xr, xw, xk, xv, xa, xg), r, k, v, tw2, ta, sg, lw, la, g


def _to_heads(ref, val, nh):
    for hd in range(nh):
        ref[0, hd] = val[:, hd * HEAD:(hd + 1) * HEAD]


def _from_heads(ref, scr, nh):
    for hd in range(nh):
        scr[:, hd * HEAD:(hd + 1) * HEAD] = ref[0, hd]
    return scr[...]


def _rwkv_weight_specs(ws):
    return [_whole(w.shape) for w in ws]


def _rwkv_pre_fwd(x, mod, vec, e_ind, et_ind, weights):
    bl, t, d = x.shape
    tm = min(t, RWKV_TM)
    nh = d // HEAD
    hb = tm // 8

    def body(x_ref, halo_ref, mod_ref, vec_ref, e_ref, et_ref, wrkv, w1, a1, g1, w2, a2, g2,
             r_ref, ld_ref, k2_ref, v_ref, as_ref, bs_ref, g_ref):
        ti = pl.program_id(1)
        _, _, _, xs = _rwkv_pre_core(x_ref, halo_ref, mod_ref, vec_ref, ti)
        _, r, k, v, _, _, _, lw, la, g = _rwkv_proj(xs, wrkv, w1, a1, g1, w2, a2, g2)
        headsum = _make_headsum(e_ref[...], et_ref[...])
        ld, k2, a_s, b_s = _rwkv_elem(r, k, lw, la, vec_ref[6:7], vec_ref[7:8], vec_ref[8:9], vec_ref[9:10], headsum)
        g_ref[0] = g
        for ref, val in ((r_ref, r), (ld_ref, ld), (k2_ref, k2), (v_ref, v), (as_ref, a_s), (bs_ref, b_s)):
            _to_heads(ref, val, nh)

    halo = pl.BlockSpec((1, 8, d), lambda e, i: (e, jnp.maximum(i * hb - 1, 0), 0))
    hs = _sds((bl, nh, t, HEAD))
    return pl.pallas_call(
        body, name="rwkv_pre_fwd", grid=(bl, t // tm),
        in_specs=[_tok(tm, d), halo, _per_example(8, d), _whole(vec.shape), _whole(e_ind.shape), _whole(et_ind.shape)]
        + _rwkv_weight_specs(weights),
        out_specs=(_heads(nh, tm),) * 6 + (_tok(tm, d),),
        out_shape=(hs,) * 6 + (_sds(x.shape),),
        compiler_params=_cparams(("arbitrary", "arbitrary")),
    )(x, x, mod, vec, e_ind, et_ind, *weights)


def _rwkv_post_fwd(x, y, r, k2, v, g, mod, vec, e_ind, et_ind, w_out):
    bl, t, d = x.shape
    tm = min(t, RWKV_TM)
    nh = d // HEAD

    def body(x_ref, y_ref, r_ref, k2_ref, v_ref, g_ref, mod_ref, vec_ref, e_ref, et_ref, wout_ref,
             xo_ref, mix_ref, s0, s1, s2, s3):
        headsum = _make_headsum(e_ref[...], et_ref[...])
        yv, rv, kv, vv = (_from_heads(ref, scr, nh) for ref, scr in
                          ((y_ref, s0), (r_ref, s1), (k2_ref, s2), (v_ref, s3)))
        o = _rwkv_post(yv, rv, kv, vv, g_ref[0], vec_ref[10:11], vec_ref[11:12], vec_ref[12:13], headsum)
        mix = _dot(o, wout_ref[...])
        mix_ref[0] = mix
        xo_ref[0] = x_ref[0] + mod_ref[0, 2:3] * mix

    return pl.pallas_call(
        body, name="rwkv_post_fwd", grid=(bl, t // tm),
        in_specs=[_tok(tm, d)] + [_heads(nh, tm)] * 4 + [_tok(tm, d), _per_example(8, d), _whole(vec.shape),
                                                         _whole(e_ind.shape), _whole(et_ind.shape), _whole(w_out.shape)],
        out_specs=(_tok(tm, d), _tok(tm, d)),
        out_shape=(_sds(x.shape), _sds(x.shape)),
        scratch_shapes=[pltpu.VMEM((tm, d), F32)] * 4,
        compiler_params=_cparams(("arbitrary", "arbitrary")),
    )(x, y, r, k2, v, g, mod, vec, e_ind, et_ind, w_out)


def _rwkv_post_bwd(dxo, mix, y, r, k2, v, g, mod, vec, e_ind, et_ind, w_out, scatter=()):
    bl, t, d = dxo.shape
    tm = min(t, RWKV_TM)
    nh = d // HEAD

    def body(dxo_ref, mix_ref, y_ref, r_ref, k2_ref, v_ref, g_ref, mod_ref, vec_ref, e_ref, et_ref, wout_ref,
             dy_ref, dr_ref, dk2_ref, dv_ref, dg_ref, o_ref, dmix_ref, dgate_ref, small_ref, s0, s1, s2, s3):
        e, ti = pl.program_id(0), pl.program_id(1)
        headsum = _make_headsum(e_ref[...], et_ref[...])
        yv, rv, kv, vv = (_from_heads(ref, scr, nh) for ref, scr in
                          ((y_ref, s0), (r_ref, s1), (k2_ref, s2), (v_ref, s3)))
        dxo = dxo_ref[0]
        dmix = mod_ref[0, 2:3] * dxo
        dmix_ref[0] = dmix.astype(BF16)
        do = _dot_nt(dmix, wout_ref[...])
        post = functools.partial(_rwkv_post, headsum=headsum)
        o, vjp = jax.vjp(post, yv, rv, kv, vv, g_ref[0], vec_ref[10:11], vec_ref[11:12], vec_ref[12:13])
        o_ref[0] = o.astype(BF16)
        dy, dr, dk2, dv, dg, dlng, dlnb, drk = vjp(do)
        _to_heads(dy_ref, dy, nh)
        dr_ref[0], dk2_ref[0], dv_ref[0], dg_ref[0] = dr, dk2, dv, dg
        zero = jnp.zeros((1, d), F32)
        _add_rows(dgate_ref, ti == 0, [zero, zero, _colsum(dxo * mix_ref[0])])

        @pl.when(_first(e, ti))
        def _():
            small_ref[...] = jnp.zeros_like(small_ref)

        small_ref[0:1] += dlng
        small_ref[1:2] += dlnb
        small_ref[2:3] += drk

    return _call_with_scatter(
        body, name="rwkv_post_bwd", grid=(bl, t // tm),
        in_specs=[_tok(tm, d), _tok(tm, d)] + [_heads(nh, tm)] * 4
        + [_tok(tm, d), _per_example(8, d), _whole(vec.shape), _whole(e_ind.shape), _whole(et_ind.shape),
           _whole(w_out.shape)],
        out_specs=(_heads(nh, tm),) + (_tok(tm, d),) * 6 + (_per_example(8, d), _whole((8, d))),
        out_shape=(_sds((bl, nh, t, HEAD)),) + (_sds(dxo.shape),) * 4 + (_sds(dxo.shape, BF16),) * 2
        + (_sds((bl, 8, d)), _sds((8, d))),
        scratch_shapes=[pltpu.VMEM((tm, d), F32)] * 4,
        operands=(dxo, mix, y, r, k2, v, g, mod, vec, e_ind, et_ind, w_out), scatter=scatter)


RWKV_BWD_TM = 128


def _rwkv_pre_bwd(x, mod, vec, e_ind, et_ind, weights, dr_p, dk2_p, dv_p, dg, dr_s, dld, dk2_s, dv_s, das, dbs):
    bl, t, d = x.shape
    tm = min(t, RWKV_BWD_TM)
    nh = d // HEAD
    hb = tm // 8
    lp, gp = LORA_PAD, GATE_PAD

    def body(x_ref, halo_ref, mod_ref, vec_ref, e_ref, et_ref, wrkv, w1, a1, g1, w2, a2, g2,
             drp_ref, dk2p_ref, dvp_ref, dg_ref, drs_ref, dld_ref, dk2s_ref, dvs_ref, das_ref, dbs_ref,
             dh_ref, dhp_ref, xr_ref, xw_ref, xk_ref, xv_ref, xa_ref, xg_ref, dr_ref, dk_ref, dv_ref,
             dtw_ref, dta_ref, dtg_ref, tw2_ref, ta_ref, sg_ref, dlw_ref, dla_ref, dgb_ref, small_ref,
             s0, s1, s2, s3, s4, s5):
        e, ti = pl.program_id(0), pl.program_id(1)
        _, _, xx, xs = _rwkv_pre_core(x_ref, halo_ref, mod_ref, vec_ref, ti)
        xb, r, k, v, tw2, ta, sg, lw, la, _ = _rwkv_proj(xs, wrkv, w1, a1, g1, w2, a2, g2)
        for ref, val in zip((xr_ref, xw_ref, xk_ref, xv_ref, xa_ref, xg_ref), xb):
            ref[0] = val
        headsum = _make_headsum(e_ref[...], et_ref[...])
        drs, dld, dk2s, dvs, das, dbs_ = (_from_heads(ref, scr, nh) for ref, scr in
                                          ((drs_ref, s0), (dld_ref, s1), (dk2s_ref, s2), (dvs_ref, s3),
                                           (das_ref, s4), (dbs_ref, s5)))
        elem = functools.partial(_rwkv_elem, r, headsum=headsum)
        _, vjp = jax.vjp(elem, k, lw, la, vec_ref[6:7], vec_ref[7:8], vec_ref[8:9], vec_ref[9:10])
        dk, dlw, dla, dw0, da0, dkk, dka = vjp((dld, dk2p_ref[0] + dk2s, das, dbs_))
        dr = drp_ref[0] + drs
        dv = dvp_ref[0] + dvs
        dgv = dg_ref[0]
        dtg = _dot_nt(dgv, g2[...]) * sg * (1.0 - sg)
        dtw = _dot_nt(dlw, w2[...]) * (1.0 - tw2 * tw2)
        dta = _dot_nt(dla, a2[...])
        dr_ref[0], dk_ref[0], dv_ref[0] = dr.astype(BF16), dk.astype(BF16), dv.astype(BF16)
        dtw_ref[0], dta_ref[0], dtg_ref[0] = dtw.astype(BF16), dta.astype(BF16), dtg.astype(BF16)
        tw2_ref[0], ta_ref[0], sg_ref[0] = tw2.astype(BF16), ta.astype(BF16), sg.astype(BF16)
        dlw_ref[0], dla_ref[0], dgb_ref[0] = dlw.astype(BF16), dla.astype(BF16), dgv.astype(BF16)
        dxs = (_dot_nt(dr, wrkv[:, 0:d]), _dot_nt(dtw, w1[...]), _dot_nt(dk, wrkv[:, d:2 * d]),
               _dot_nt(dv, wrkv[:, 2 * d:3 * d]), _dot_nt(dta, a1[...]), _dot_nt(dtg, g1[...]))

        @pl.when(_first(e, ti))
        def _():
            small_ref[...] = jnp.zeros_like(small_ref)

        total = jnp.zeros((tm, d), F32)
        dhp = jnp.zeros((tm, d), F32)
        for i, dxi in enumerate(dxs):
            total += dxi
            dhp += dxi * vec_ref[i:i + 1]
            small_ref[i:i + 1] += _colsum(dxi * xx)
        dh_ref[0], dhp_ref[0] = total - dhp, dhp
        small_ref[6:7] += dw0
        small_ref[7:8] += da0
        small_ref[8:9] += dkk
        small_ref[9:10] += dka

    halo = pl.BlockSpec((1, 8, d), lambda e, i: (e, jnp.maximum(i * hb - 1, 0), 0))
    tokd, tokl, tokg = _tok(tm, d), _tok(tm, lp), _tok(tm, gp)
    bf = lambda w: _sds((bl, t, w), BF16)
    return pl.pallas_call(
        body, name="rwkv_pre_bwd", grid=(bl, t // tm),
        in_specs=[tokd, halo, _per_example(8, d), _whole(vec.shape), _whole(e_ind.shape), _whole(et_ind.shape)]
        + _rwkv_weight_specs(weights) + [tokd] * 4 + [_heads(nh, tm)] * 6,
        out_specs=(tokd, tokd) + (tokd,) * 6 + (tokd,) * 3 + (tokl, tokl, tokg, tokl, tokl, tokg)
        + (tokd, tokd, tokd, _whole((N_VEC, d))),
        out_shape=(_sds(x.shape), _sds(x.shape)) + (bf(d),) * 9 + (bf(lp), bf(lp), bf(gp), bf(lp), bf(lp), bf(gp))
        + (bf(d), bf(d), bf(d), _sds((N_VEC, d))),
        scratch_shapes=[pltpu.VMEM((tm, d), F32)] * 6,
        compiler_params=_cparams(("arbitrary", "arbitrary")),
    )(x, x, mod, vec, e_ind, et_ind, *weights, dr_p, dk2_p, dv_p, dg, dr_s, dld, dk2_s, dv_s, das, dbs)


def _norm_bwd(x, dxo, dh, dhprev, mod, dgate):
    bl, t, d = x.shape
    tm = min(t, RWKV_TM)
    hb = tm // 8
    last_blk = t // 8 - 1

    def body(x_ref, dxo_ref, dh_ref, dhp_ref, nxt_ref, mod_ref, dgate_ref, dx_ref, dmod_ref):
        ti = pl.program_id(1)
        xn, inv = _rms(x_ref[0])
        last = jnp.where(ti == t // tm - 1, 0.0, nxt_ref[0, 0:1])
        dh = dh_ref[0] + _shift_up(dhp_ref[0], last)
        dx_ref[0] = dxo_ref[0] + _rms_bwd(xn, inv, dh * (1.0 + mod_ref[0, 1:2]))

        @pl.when(ti == 0)
        def _():
            dmod_ref[0] = dgate_ref[0]

        dmod_ref[0, 0:1] += _colsum(dh)
        dmod_ref[0, 1:2] += _colsum(dh * xn)

    nxt = pl.BlockSpec((1, 8, d), lambda e, i: (e, jnp.minimum((i + 1) * hb, last_blk), 0))
    return pl.pallas_call(
        body, name="norm_bwd", grid=(bl, t // tm),
        in_specs=[_tok(tm, d)] * 4 + [nxt, _per_example(8, d), _per_example(8, d)],
        out_specs=(_tok(tm, d), _per_example(8, d)),
        out_shape=(_sds(x.shape), _sds((bl, 8, d))),
        compiler_params=_cparams(("arbitrary", "arbitrary")),
    )(x, dxo, dh, dhprev, dhprev, mod, dgate)


def _final(x, target, final_g):
    bl, t, d = x.shape
    tm = min(t, 512)

    def body(x_ref, tgt_ref, g_ref, dx_ref, loss_ref, dg_ref):
        e, ti = pl.program_id(0), pl.program_id(1)

        @pl.when(_first(e, ti))
        def _():
            loss_ref[...] = jnp.zeros_like(loss_ref)
            dg_ref[...] = jnp.zeros_like(dg_ref)

        xn, inv = _rms(x_ref[0])
        err = xn * g_ref[...] - tgt_ref[0]
        loss_ref[...] += (0.5 / d) * jnp.sum(err * err)
        dy = err * (1.0 / d)
        dg_ref[0:1] += _colsum(dy * xn)
        dx_ref[0] = _rms_bwd(xn, inv, dy * g_ref[...])

    return pl.pallas_call(
        body, name="final_loss", grid=(bl, t // tm),
        in_specs=[_tok(tm, d), _tok(tm, d), _whole(final_g.shape)],
        out_specs=(_tok(tm, d), _whole((8, 128)), _whole((8, d))),
        out_shape=(_sds(x.shape), _sds((8, 128)), _sds((8, d))),
        compiler_params=_cparams(("arbitrary", "arbitrary")),
    )(x, target, final_g)


def _adamw_math(w, g, m, v):
    m = ADAM_B1 * m + (1.0 - ADAM_B1) * g
    v = ADAM_B2 * v + (1.0 - ADAM_B2) * jnp.square(g)
    m_hat = m / (1.0 - ADAM_B1 ** ADAM_STEP)
    v_hat = v / (1.0 - ADAM_B2 ** ADAM_STEP)
    return -ADAM_LR * (m_hat / (jnp.sqrt(v_hat) + ADAM_EPS) + ADAM_WD * w), m, v


def _sum_parts(ref, n):
    g = ref[0].astype(F32)
    for s in range(1, n):
        g = g + ref[s].astype(F32)
    return g


def _adamw_layers(w, m, v, parts, name):
    nl, rows, c = w.shape
    tr = min(rows, 128)

    def body(w_ref, m_ref, v_ref, *refs):
        p_refs, (g_ref, d_ref, mo_ref, vo_ref) = refs[:nl], refs[nl:]
        for layer in range(nl):
            @pl.when(pl.program_id(0) == layer)
            def _(p_ref=p_refs[layer]):
                g = _sum_parts(p_ref, p_ref.shape[0])
                g_ref[...] = g
                d_ref[...], mo_ref[...], vo_ref[...] = _adamw_math(w_ref[...], g, m_ref[...], v_ref[...])

    row = pl.BlockSpec((None, tr, c), lambda l, i: (l, i, 0))
    return pl.pallas_call(
        body, name=name, grid=(nl, rows // tr),
        in_specs=[row, row, row] + [pl.BlockSpec((p.shape[0], tr, c), lambda l, i, k=k: (0, jnp.where(l == k, i, 0), 0))
                                    for k, p in enumerate(parts)],
        out_specs=(row,) * 4, out_shape=(_sds(w.shape),) * 4,
        compiler_params=_cparams(("arbitrary", "arbitrary")),
    )(w, m, v, *parts)


def _adamw_small(items, name):
    k = len(items)
    ns = [it[3].shape[0] for it in items]

    def body(*refs):
        ins, outs = refs[:4 * k], refs[4 * k:]
        for i in range(k):
            w_ref, m_ref, v_ref, p_ref = ins[4 * i:4 * i + 4]
            g = _sum_parts(p_ref, ns[i])
            outs[4 * i][...] = g
            outs[4 * i + 1][...], outs[4 * i + 2][...], outs[4 * i + 3][...] = _adamw_math(
                w_ref[...], g, m_ref[...], v_ref[...])

    flat = [a for it in items for a in it]
    res = pl.pallas_call(
        body, name=name,
        out_shape=tuple(_sds(it[0].shape) for it in items for _ in range(4)),
        compiler_params=_cparams(),
    )(*flat)
    return [tuple(res[4 * i:4 * i + 4]) for i in range(k)]


WEIGHTS = ['ada_w', 'ada_b', 'mlp_w1', 'mlp_w2', 'a_w_in', 'a_ln_g', 'a_ln_b', 'a_w_s', 'a_b_s', 'a_w_out', 'b_mu',
           'b_w_in', 'b_w0', 'b_w1', 'b_w2', 'b_a0', 'b_a1', 'b_a2', 'b_g1', 'b_g2', 'b_k_k', 'b_k_a', 'b_r_k',
           'b_ln_g', 'b_ln_b', 'b_w_out', 'final_g']
VECTORS = ['b_mu', 'b_w0', 'b_a0', 'b_k_k', 'b_k_a', 'b_ln_g', 'b_ln_b']
REPLICATED = ['a_ln_g', 'a_ln_b', 'a_w_s', 'a_b_s', 'b_r_k', 'final_g']
ROW_ALIGN = 16


def _pad_rows(a, mult):
    pad = (-a.shape[-2]) % mult
    return jnp.pad(a, [(0, 0)] * (a.ndim - 2) + [(0, pad), (0, 0)]) if pad else a


def _as2d(a):
    if a.ndim == 1:
        return a.reshape(1, -1)
    lead = 1
    for s in a.shape[:-1]:
        lead *= s
    return a.reshape(lead, a.shape[-1])


def kernel(x, c, ada_w, ada_b, mlp_w1, mlp_w2, a_w_in, a_ln_g, a_ln_b, a_w_s, a_b_s, a_w_out, b_mu, b_w_in, b_w0, b_w1, b_w2, b_a0, b_a1, b_a2, b_g1, b_g2, b_k_k, b_k_a, b_r_k, b_ln_g, b_ln_b, b_w_out, final_g, loss_target, m_ada_w, m_ada_b, m_mlp_w1, m_mlp_w2, m_a_w_in, m_a_ln_g, m_a_ln_b, m_a_w_s, m_a_b_s, m_a_w_out, m_b_mu, m_b_w_in, m_b_w0, m_b_w1, m_b_w2, m_b_a0, m_b_a1, m_b_a2, m_b_g1, m_b_g2, m_b_k_k, m_b_k_a, m_b_r_k, m_b_ln_g, m_b_ln_b, m_b_w_out, m_final_g, v_ada_w, v_ada_b, v_mlp_w1, v_mlp_w2, v_a_w_in, v_a_ln_g, v_a_ln_b, v_a_w_s, v_a_b_s, v_a_w_out, v_b_mu, v_b_w_in, v_b_w0, v_b_w1, v_b_w2, v_b_a0, v_b_a1, v_b_a2, v_b_g1, v_b_g2, v_b_k_k, v_b_k_a, v_b_r_k, v_b_ln_g, v_b_ln_b, v_b_w_out, v_final_g):
    given = dict(locals())
    w = {n: given[n] for n in WEIGHTS}
    bl, t, d = x.shape
    nl = ada_w.shape[0]
    nb = N_DEV * bl
    m_tok = bl * t
    me = 4 * lax.axis_index("x") + 2 * lax.axis_index("y") + lax.axis_index("c")

    bf = lambda a: a.astype(BF16)
    vec_loc = _pad_rows(jnp.concatenate([_as2d(w[n]) for n in VECTORS], axis=0), ROW_ALIGN)
    g_c, g_vec, g_a_in, g_a_out = _gather_call((c, vec_loc, bf(a_w_in[0]), bf(a_w_out[0])), "gather_first")

    c_all = g_c.reshape(nb, d)
    cols = ada_w.shape[2]
    ada_b_cols = lax.dynamic_slice(ada_b, (0, me * cols), (nl, cols)).reshape(nl, 1, cols)
    mod_cols = _ada_fwd(c_all, ada_w, ada_b_cols)
    mod_full = jnp.moveaxis(_gather_call((mod_cols,), "gather_mod")[0], 0, 2).reshape(nl, nb, 6 * d)
    mod_mine = lax.dynamic_slice(mod_full, (0, me * bl, 0), (nl, bl, 6 * d)).reshape(nl, bl, 6, d)
    mod_mix = jnp.pad(mod_mine[:, :, 0:3], ((0, 0), (0, 0), (0, 5), (0, 0)))
    mod_mlp = jnp.pad(mod_mine[:, :, 3:6], ((0, 0), (0, 0), (0, 5), (0, 0)))

    def unshard(g, ax):
        g = jnp.moveaxis(g, 0, ax)
        return g.reshape(g.shape[:ax] + (g.shape[ax] * g.shape[ax + 1],) + g.shape[ax + 2:])

    lora_names = ['b_w1', 'b_a1', 'b_g1', 'b_w2', 'b_a2', 'b_g2']
    lora_pack = jnp.concatenate([bf(w[n]).reshape(-1) for n in lora_names]).reshape(-1, 128)
    full = {'a_w_in': unshard(g_a_in, 1), 'a_w_out': unshard(g_a_out, 0)}
    n_vec_rows = sum(_as2d(w[n]).shape[0] for n in VECTORS)
    vec = jnp.moveaxis(g_vec, 0, 1).reshape(N_VEC, d)
    vec = vec.at[n_vec_rows].set(b_r_k.reshape(d))

    e_ind, et_ind = _head_indicators(d)
    gd = d // SGU_GROUPS
    group_ind = (jnp.arange(SGU_GROUPS)[:, None] == jnp.arange(d)[None, :] // gd).astype(BF16)
    bias_full = jnp.repeat(a_b_s[0].T, gd, axis=1)
    pad_c = lambda a, n: jnp.pad(a, ((0, 0), (0, n - a.shape[1])))
    pad_r = lambda a, n: jnp.pad(a, ((0, n - a.shape[0]), (0, 0)))
    sgu_args = (full['a_w_in'], a_ln_g, a_ln_b, a_w_s[0], bias_full, full['a_w_out'])

    x0 = x
    (x1, mix_a), (g_w1_0, g_w2_0) = _sgu_fwd(x0, mod_mix[0], *sgu_args, gather=(bf(mlp_w1[0]), bf(mlp_w2[0])))
    w1_full = [unshard(g_w1_0, 1), None]
    w2_full = [unshard(g_w2_0, 0), None]
    (x2, ff0, q0), (g_w1_1, g_w2_1, g_b_in, g_b_out, g_lora) = _mlp_fwd(
        x1, mod_mlp[0], w1_full[0], w2_full[0],
        gather=(bf(mlp_w1[1]), bf(mlp_w2[1]), bf(b_w_in[0]), bf(b_w_out[0]), lora_pack))
    w1_full[1], w2_full[1] = unshard(g_w1_1, 1), unshard(g_w2_1, 0)
    full['b_w_in'], full['b_w_out'] = unshard(g_b_in, 1), unshard(g_b_out, 0)
    lora_flat, lo = g_lora.reshape(N_DEV, -1), 0
    for n, ax in zip(lora_names, (0, 0, 0, 1, 1, 1)):
        loc = w[n].shape[1:]
        full[n] = unshard(lora_flat[:, lo:lo + w[n].size].reshape((N_DEV,) + loc), ax)
        lo += w[n].size
    rwkv_w = (full['b_w_in'], pad_c(full['b_w1'], LORA_PAD), pad_c(full['b_a1'], LORA_PAD),
              pad_c(full['b_g1'], GATE_PAD), pad_r(full['b_w2'], LORA_PAD), pad_r(full['b_a2'], LORA_PAD),
              pad_r(full['b_g2'], GATE_PAD))
    r, ld, k2, v, a_s, b_s, gate = _rwkv_pre_fwd(x2, mod_mix[1], vec, e_ind, et_ind, rwkv_w)
    y, s0, tinv = _wkv_fwd(r, ld, k2, v, a_s, b_s)
    x3, mix_b = _rwkv_post_fwd(x2, y, r, k2, v, gate, mod_mix[1], vec, e_ind, et_ind, full['b_w_out'])
    (x4, ff1, q1), _ = _mlp_fwd(x3, mod_mlp[1], w1_full[1], w2_full[1])
    dx4, loss_blk, dfinal = _final(x4, loss_target, final_g.reshape(1, d))
    loss = lax.psum(loss_blk[0, 0], ("x", "y", "c"))

    tok = lambda a: a.reshape(m_tok, a.shape[-1])
    shard_rows = lambda g: g.reshape((N_DEV, g.shape[0] // N_DEV) + g.shape[1:])
    (dx3, dmod_mlp1, h_b, dff_b, dp_b), _ = _mlp_bwd(x3, dx4, ff1, q1, mod_mlp[1], w1_full[1], w2_full[1])
    gw1_1 = _matmul_tn(tok(h_b), tok(dp_b), "grad_mlp_w1_l1", col_shards=N_DEV)
    gw2_1 = shard_rows(_matmul_tn(tok(q1), tok(dff_b), "grad_mlp_w2_l1"))
    (dy, dr_p, dk2_p, dv_p, dgate_act, o_b, dmix_b, dgate_b, small_post), (rw1_1, rw2_1) = _rwkv_post_bwd(
        dx3, mix_b, y, r, k2, v, gate, mod_mix[1], vec, e_ind, et_ind, full['b_w_out'], scatter=(gw1_1, gw2_1))
    g_b_w_out = shard_rows(_matmul_tn(tok(o_b), tok(dmix_b), "grad_b_w_out"))
    dr_s, dld, dk2_s, dv_s, das, dbs = _wkv_bwd(r, ld, k2, v, a_s, b_s, s0, tinv, dy)
    (dh, dhp, xr_b, xw_b, xk_b, xv_b, xa_b, xg_b, dr_b, dk_b, dv_b, dtw_b, dta_b, dtg_b, tw2_b, ta_b, sg_b,
     dlw_b, dla_b, dg_b, small_pre) = _rwkv_pre_bwd(x2, mod_mix[1], vec, e_ind, et_ind, rwkv_w,
                                                    dr_p, dk2_p, dv_p, dgate_act, dr_s, dld, dk2_s, dv_s, das, dbs)
    g_b_w_in = jnp.concatenate([_matmul_tn(tok(xr_b), tok(dr_b), "grad_b_w_r"),
                                _matmul_tn(tok(xk_b), tok(dk_b), "grad_b_w_k"),
                                _matmul_tn(tok(xv_b), tok(dv_b), "grad_b_w_v")], axis=1)
    shard_cols = lambda g: jnp.moveaxis(g.reshape(g.shape[0], N_DEV, g.shape[1] // N_DEV), 1, 0)
    g_b_w_in = shard_cols(g_b_w_in)
    lw_, lg_ = b_w1.shape[2], b_g1.shape[2]
    small_names = ['b_w1', 'b_a1', 'b_g1', 'b_w2', 'b_a2', 'b_g2'] + VECTORS
    small_parts = [
        shard_rows(_matmul_tn(tok(xw_b), tok(dtw_b), "grad_b_w1")[:, :lw_]),
        shard_rows(_matmul_tn(tok(xa_b), tok(dta_b), "grad_b_a1")[:, :lw_]),
        shard_rows(_matmul_tn(tok(xg_b), tok(dtg_b), "grad_b_g1")[:, :lg_]),
        shard_cols(_matmul_tn(tok(tw2_b), tok(dlw_b), "grad_b_w2")[:lw_]),
        shard_cols(_matmul_tn(tok(ta_b), tok(dla_b), "grad_b_a2")[:lw_]),
        shard_cols(_matmul_tn(tok(sg_b), tok(dg_b), "grad_b_g2")[:lg_]),
        shard_cols(jnp.concatenate([small_pre[0:10], small_post[0:2]], axis=0).astype(BF16)),
    ]
    small_flat = jnp.concatenate([p.reshape(N_DEV, -1) for p in small_parts], axis=1)
    lane = 128
    small_rows = -(-small_flat.shape[1] // (lane * ROW_ALIGN)) * ROW_ALIGN
    small_pack = jnp.pad(small_flat, ((0, 0), (0, small_rows * lane - small_flat.shape[1]))).reshape(
        N_DEV, small_rows, lane)
    dx2, dmod_mix1 = _norm_bwd(x2, dx3, dh, dhp, mod_mix[1], dgate_b)
    (dx1, dmod_mlp0, h_b, dff_b, dp_b), (r_b_w_in, r_b_w_out, r_small) = _mlp_bwd(
        x1, dx2, ff0, q0, mod_mlp[0], w1_full[0], w2_full[0], scatter=(g_b_w_in, g_b_w_out, small_pack))
    gw1_0 = _matmul_tn(tok(h_b), tok(dp_b), "grad_mlp_w1_l0", col_shards=N_DEV)
    gw2_0 = shard_rows(_matmul_tn(tok(q0), tok(dff_b), "grad_mlp_w2_l0"))
    (dx0, dmod_mix0, h_b, dpre_b, z_b, dmix_b, small_sgu, d_ws, d_bs), (rw1_0, rw2_0) = _sgu_bwd(
        x0, dx1, mix_a, mod_mix[0], *sgu_args, group_ind, scatter=(gw1_0, gw2_0))
    dmod_mine = jnp.stack([jnp.concatenate([dmod_mix0[:, 0:3], dmod_mlp0[:, 0:3]], axis=1),
                           jnp.concatenate([dmod_mix1[:, 0:3], dmod_mlp1[:, 0:3]], axis=1)], axis=1)
    rep_g = {'a_ln_g': small_sgu[0:1], 'a_ln_b': small_sgu[1:2], 'a_w_s': d_ws.reshape(-1, d), 'a_b_s': d_bs.reshape(1, d),
             'b_r_k': small_post[2:3], 'final_g': dfinal[0:1]}
    rep_rows = [rep_g[n].shape[0] for n in REPLICATED]
    rep_pack = _pad_rows(jnp.concatenate([rep_g[n] for n in REPLICATED], axis=0), 8)
    g_a_w_in, (dmod_all, rep_all) = _matmul_tn(tok(h_b), tok(dpre_b), "grad_a_w_in", col_shards=N_DEV,
                                               gather=(dmod_mine.reshape(bl, nl * 6 * d), rep_pack))
    g_a_w_out = shard_rows(_matmul_tn(tok(z_b), tok(dmix_b), "grad_a_w_out"))
    r_a_w_in, r_a_w_out = _scatter_call((g_a_w_in, g_a_w_out), "scatter_sgu_grads")

    dmod_all = jnp.moveaxis(dmod_all.reshape(nb, nl, 6 * d), 0, 1)
    dmod_cols = lax.dynamic_slice(dmod_all, (0, 0, me * cols), (nl, nb, cols))
    g_ada_w, g_ada_b = _ada_bwd(c_all, dmod_cols, dmod_all)

    mom = {n: given['m_' + n] for n in WEIGHTS}
    var = {n: given['v_' + n] for n in WEIGHTS}
    out = {}
    as3d = lambda a: a.reshape((-1,) + a.shape[-2:])
    for n, parts in (('mlp_w1', [rw1_0, rw1_1]), ('mlp_w2', [rw2_0, rw2_1]), ('a_w_in', [r_a_w_in]),
                     ('a_w_out', [r_a_w_out]), ('b_w_in', [r_b_w_in]), ('b_w_out', [r_b_w_out]),
                     ('ada_w', list(g_ada_w))):
        res = _adamw_layers(as3d(w[n]), as3d(mom[n]), as3d(var[n]), parts, "adamw_" + n)
        out[n] = tuple(a.reshape(w[n].shape) for a in res)

    items, names = [], []

    def add(n, part):
        s2 = _as2d(w[n]).shape
        items.append((_as2d(w[n]), _as2d(mom[n]), _as2d(var[n]), part.reshape((part.shape[0],) + s2)))
        names.append(n)

    sflat = r_small.reshape(N_DEV, -1)
    so = 0
    for n in small_names:
        sz = w[n].size
        add(n, sflat[:, so:so + sz])
        so += sz
    ro = 0
    for n, nr in zip(REPLICATED, rep_rows):
        add(n, rep_all[:, ro:ro + nr])
        ro += nr
    add('ada_b', g_ada_b[None])
    for n, res in zip(names, _adamw_small(items, "adamw_small")):
        out[n] = tuple(a.reshape(w[n].shape) for a in res)

    return (loss, dx0, *[out[n][0] for n in WEIGHTS], *[out[n][1] for n in WEIGHTS],
            *[out[n][2] for n in WEIGHTS], *[out[n][3] for n in WEIGHTS])
```

```python
import functools

import jax
import jax.numpy as jnp
from jax import lax
from jax.experimental import pallas as pl
from jax.experimental.pallas import tpu as pltpu

F32 = jnp.float32
BF16 = jnp.bfloat16

N_DEV = 8
RMS_EPS = 1e-6
LN_EPS = 1e-5
HEAD = 64
GN_EPS = HEAD * 1e-5
L2_EPS = 1e-12
DECAY_SCALE = 0.6065306597126334
SGU_CHUNK = 128
SGU_GROUPS = 8
WKV_CHUNK = 64
WKV_HEADS_PER_STEP = 16
WKV_EXAMPLES_PER_STEP = 2
LORA_PAD = 128
GATE_PAD = 256
ADAM_LR, ADAM_B1, ADAM_B2, ADAM_EPS, ADAM_WD, ADAM_STEP = 0.001, 0.9, 0.999, 1e-08, 0.01, 10
VMEM_LIMIT = 56 * 1024 * 1024


def _cparams(sem=None, **kw):
    if sem is not None:
        kw["dimension_semantics"] = sem
    return pltpu.CompilerParams(vmem_limit_bytes=VMEM_LIMIT, **kw)


def _dot(a, b):
    return jnp.dot(a.astype(BF16), b.astype(BF16), preferred_element_type=F32)


def _dot_nt(a, b):
    return lax.dot_general(a.astype(BF16), b.astype(BF16), (((1,), (1,)), ((), ())), preferred_element_type=F32)


def _dot_tn(a, b):
    return lax.dot_general(a.astype(BF16), b.astype(BF16), (((0,), (0,)), ((), ())), preferred_element_type=F32)


def _bdot(a, b, dims):
    return lax.dot_general(a.astype(BF16), b.astype(BF16), (dims, ((0,), (0,))), preferred_element_type=F32)


@jax.custom_vjp
def _tri_sum(tri, tri_t, x):
    hi = x.astype(BF16)
    lo = (x - hi.astype(F32)).astype(BF16)
    dn = (((2,), (1,)), ((0,), (0,)))
    return (lax.dot_general(tri, hi, dn, preferred_element_type=F32)
            + lax.dot_general(tri, lo, dn, preferred_element_type=F32))


_tri_sum.defvjp(lambda tri, tri_t, x: (_tri_sum(tri, tri_t, x), (tri, tri_t)),
                lambda res, g: (jnp.zeros_like(res[0]), jnp.zeros_like(res[1]), _tri_sum(res[1], res[0], g)))


@jax.custom_vjp
def _bmm_nn(a, b):
    return _bdot(a, b, ((2,), (1,)))


@jax.custom_vjp
def _bmm_nt(a, b):
    return _bdot(a, b, ((2,), (2,)))


@jax.custom_vjp
def _bmm_tn(a, b):
    return _bdot(a, b, ((1,), (1,)))


_bmm_nn.defvjp(lambda a, b: (_bmm_nn(a, b), (a, b)), lambda res, g: (_bmm_nt(g, res[1]), _bmm_tn(res[0], g)))
_bmm_nt.defvjp(lambda a, b: (_bmm_nt(a, b), (a, b)), lambda res, g: (_bmm_nn(g, res[1]), _bmm_tn(g, res[0])))
_bmm_tn.defvjp(lambda a, b: (_bmm_tn(a, b), (a, b)), lambda res, g: (_bmm_nt(res[1], g), _bmm_nn(res[0], g)))


def _tri_inverse(p):
    n = p.shape[1]
    row = lax.broadcasted_iota(jnp.int32, (n, n), 0)
    col = lax.broadcasted_iota(jnp.int32, (n, n), 1)
    tinv = jnp.where(row == col, 1.0, 0.0).astype(F32)[None] + p
    for _ in range(max(1, (n - 1).bit_length()) - 1):
        p = _bmm_nn(p, p)
        tinv = tinv + _bmm_nn(tinv, p)
    return tinv.astype(BF16)


def _tri_solve_fwd(tinv, p, rhs):
    u = _bmm_nn(tinv, rhs)
    return u, (tinv, u)


def _tri_solve_bwd(res, du):
    tinv, u = res
    drhs = _bmm_tn(tinv, du)
    return jnp.zeros_like(tinv), _bmm_nt(drhs, u), drhs


@jax.custom_vjp
def _tri_solve(tinv, p, rhs):
    return _tri_solve_fwd(tinv, p, rhs)[0]


_tri_solve.defvjp(_tri_solve_fwd, _tri_solve_bwd)


def _wkv_chunk(s0, r, ld, k, v, a, b, tinv=None):
    nh, n, _ = r.shape
    row = lax.broadcasted_iota(jnp.int32, (n, n), 0)
    col = lax.broadcasted_iota(jnp.int32, (n, n), 1)
    incl = row >= col
    strict = row > col
    lower = jnp.broadcast_to(jnp.where(incl, 1.0, 0.0).astype(BF16), (nh, n, n))
    upper = jnp.broadcast_to(jnp.where(row <= col, 1.0, 0.0).astype(BF16), (nh, n, n))
    c = _tri_sum(lower, upper, ld)
    c_end = c[:, n - 1:n, :]
    ec, enc, ecx, eend = jnp.exp(c), jnp.exp(-c), jnp.exp(c - ld), jnp.exp(c_end - c)
    ar = jnp.concatenate([a * ecx, r * ec], axis=1)
    mask = jnp.concatenate([strict, incl], axis=0)[None]
    m_b = jnp.where(mask, _bmm_nt(ar, b * enc), 0.0)
    m_k = jnp.where(mask, _bmm_nt(ar, k * enc), 0.0)
    a_ab, a_rb = m_b[:, :n], m_b[:, n:]
    base = _bmm_nt(ar, s0) + _bmm_nn(m_k, v)
    if tinv is None:
        tinv = lax.stop_gradient(_tri_inverse(a_ab))
    u = _tri_solve(tinv, a_ab, base[:, :n])
    y = base[:, n:] + _bmm_nn(a_rb, u)
    s1 = s0 * jnp.exp(c_end) + _bmm_tn(jnp.concatenate([u, v], axis=1), jnp.concatenate([b * eend, k * eend], axis=1))
    return y, s1, tinv


def _wkv_specs(bl, nh, t):
    eb, hb, lc = min(bl, WKV_EXAMPLES_PER_STEP), min(nh, WKV_HEADS_PER_STEP), WKV_CHUNK
    return eb, hb, lc, (bl // eb, nh // hb, t // lc)


def _wkv_fwd(r, ld, k, v, a, b):
    bl, nh, t, n = r.shape
    eb, hb, lc, grid = _wkv_specs(bl, nh, t)
    nc = t // lc
    nb = eb * hb

    def body(r_ref, ld_ref, k_ref, v_ref, a_ref, b_ref, y_ref, s0_ref, tinv_ref, s_scr):
        @pl.when(pl.program_id(2) == 0)
        def _():
            s_scr[...] = jnp.zeros_like(s_scr)

        s0 = s_scr[...]
        s0_ref[:, :, 0] = s0.reshape(eb, hb, n, n)
        y, s1, tinv = _wkv_chunk(
            s0, *(ref[...].reshape(nb, lc, n) for ref in (r_ref, ld_ref, k_ref, v_ref, a_ref, b_ref)))
        y_ref[...] = y.reshape(eb, hb, lc, n)
        tinv_ref[:, :, 0] = tinv.reshape(eb, hb, lc, lc)
        s_scr[...] = s1

    seq = pl.BlockSpec((eb, hb, lc, n), lambda e, h, c: (e, h, c, 0))
    return pl.pallas_call(
        body, name="wkv_fwd", grid=grid,
        in_specs=[seq] * 6,
        out_specs=(seq, pl.BlockSpec((eb, hb, 1, n, n), lambda e, h, c: (e, h, c, 0, 0)),
                   pl.BlockSpec((eb, hb, 1, lc, lc), lambda e, h, c: (e, h, c, 0, 0))),
        out_shape=(jax.ShapeDtypeStruct((bl, nh, t, n), F32), jax.ShapeDtypeStruct((bl, nh, nc, n, n), F32),
                   jax.ShapeDtypeStruct((bl, nh, nc, lc, lc), BF16)),
        scratch_shapes=[pltpu.VMEM((nb, n, n), F32)],
        compiler_params=_cparams(("arbitrary", "arbitrary", "arbitrary")),
    )(r, ld, k, v, a, b)


def _wkv_bwd(r, ld, k, v, a, b, s0_all, tinv_all, dy):
    bl, nh, t, n = r.shape
    eb, hb, lc, grid = _wkv_specs(bl, nh, t)
    nc = t // lc
    nb = eb * hb

    def body(r_ref, ld_ref, k_ref, v_ref, a_ref, b_ref, s0_ref, tinv_ref, dy_ref,
             dr_ref, dld_ref, dk_ref, dv_ref, da_ref, db_ref, ds_scr):
        @pl.when(pl.program_id(2) == 0)
        def _():
            ds_scr[...] = jnp.zeros_like(ds_scr)

        args = (s0_ref[:, :, 0].reshape(nb, n, n),) + tuple(
            ref[...].reshape(nb, lc, n) for ref in (r_ref, ld_ref, k_ref, v_ref, a_ref, b_ref))
        tinv = tinv_ref[:, :, 0].reshape(nb, lc, lc)
        _, vjp = jax.vjp(lambda *xs: _wkv_chunk(*xs, tinv=tinv)[:2], *args)
        ds0, *dseq = vjp((dy_ref[...].reshape(nb, lc, n), ds_scr[...]))
        ds_scr[...] = ds0
        for ref, val in zip((dr_ref, dld_ref, dk_ref, dv_ref, da_ref, db_ref), dseq):
            ref[...] = val.reshape(eb, hb, lc, n)

    seq = pl.BlockSpec((eb, hb, lc, n), lambda e, h, c: (e, h, nc - 1 - c, 0))
    st = pl.BlockSpec((eb, hb, 1, n, n), lambda e, h, c: (e, h, nc - 1 - c, 0, 0))
    ti = pl.BlockSpec((eb, hb, 1, lc, lc), lambda e, h, c: (e, h, nc - 1 - c, 0, 0))
    out = jax.ShapeDtypeStruct((bl, nh, t, n), F32)
    return pl.pallas_call(
        body, name="wkv_bwd", grid=grid,
        in_specs=[seq] * 6 + [st, ti, seq],
        out_specs=(seq,) * 6, out_shape=(out,) * 6,
        scratch_shapes=[pltpu.VMEM((nb, n, n), F32)],
        compiler_params=_cparams(("arbitrary", "arbitrary", "arbitrary")),
    )(r, ld, k, v, a, b, s0_all, tinv_all, dy)


def _scatter_copies(x_refs, o_refs, send_sems, recv_sems, local_sems):
    pos = (lax.axis_index("x"), lax.axis_index("y"), lax.axis_index("c"))
    me = 4 * pos[0] + 2 * pos[1] + pos[2]

    def descriptors():
        sends, arrivals, local = [], [], []
        for i, (x_ref, o_ref) in enumerate(zip(x_refs, o_refs)):
            for m in range(1, N_DEV):
                p = tuple(1 - pos[a] if (m >> (2 - a)) & 1 else pos[a] for a in range(3))
                pidx = 4 * p[0] + 2 * p[1] + p[2]
                k = (N_DEV - 1) * i + m - 1
                for dst, out in ((o_ref.at[me], sends), (o_ref.at[pidx], arrivals)):
                    out.append(pltpu.make_async_remote_copy(
                        src_ref=x_ref.at[pidx], dst_ref=dst, send_sem=send_sems.at[k], recv_sem=recv_sems.at[k],
                        device_id=p, device_id_type=pl.DeviceIdType.MESH))
            local.append(pltpu.make_async_copy(x_ref.at[me], o_ref.at[me], local_sems.at[i]))
        return sends, arrivals, local

    def start():
        sends, _, local = descriptors()
        for cp in local + sends:
            cp.start()

    def finish():
        sends, arrivals, local = descriptors()
        for cp in arrivals:
            cp.wait_recv()
        for cp in sends:
            cp.wait_send()
        for cp in local:
            cp.wait()

    return start, finish


def _gather_copies(x_refs, o_refs, send_sems, recv_sems, local_sems):
    pos = (lax.axis_index("x"), lax.axis_index("y"), lax.axis_index("c"))
    me = 4 * pos[0] + 2 * pos[1] + pos[2]
    far = (2, 4, 6)

    def peer_of(m):
        p = tuple(1 - pos[a] if (m >> (2 - a)) & 1 else pos[a] for a in range(3))
        return p, 4 * p[0] + 2 * p[1] + p[2]

    sibling, _ = peer_of(1)

    def copy(i, k, src, slot, to):
        return pltpu.make_async_remote_copy(
            src_ref=src, dst_ref=o_refs[i].at[slot], send_sem=send_sems.at[(N_DEV - 1) * i + k],
            recv_sem=recv_sems.at[(N_DEV - 1) * i + k], device_id=to, device_id_type=pl.DeviceIdType.MESH)

    def direct(i):
        return [copy(i, m - 1, x_refs[i], me, peer_of(m)[0]) for m in (1,) + far]

    def local(i):
        return pltpu.make_async_copy(x_refs[i], o_refs[i].at[me], local_sems.at[i])

    def start():
        for i in range(len(x_refs)):
            local(i).start()
            for cp in direct(i):
                cp.start()

    def finish():
        n = len(x_refs)
        relays = []
        for i in range(n):
            for m in far:
                origin = peer_of(m)[1]
                copy(i, m - 1, x_refs[i], origin, sibling).wait_recv()
                fwd = copy(i, m, o_refs[i].at[origin], origin, sibling)
                fwd.start()
                relays.append(fwd)
        for i in range(n):
            copy(i, 0, x_refs[i], peer_of(1)[1], sibling).wait_recv()
            for m in far:
                copy(i, m, x_refs[i], peer_of(m ^ 1)[1], sibling).wait_recv()
        for i in range(n):
            for cp in direct(i):
                cp.wait_send()
            local(i).wait()
        for cp in relays:
            cp.wait_send()

    return start, finish


def _scatter_scratch(n):
    return [pltpu.SemaphoreType.DMA(((N_DEV - 1) * n,)), pltpu.SemaphoreType.DMA(((N_DEV - 1) * n,)),
            pltpu.SemaphoreType.DMA((n,))]


_ANY = pl.BlockSpec(memory_space=pl.ANY)


def _scatter_call(arrays, name):
    n = len(arrays)

    def body(*refs):
        start, finish = _scatter_copies(refs[:n], refs[n:2 * n], *refs[2 * n:])
        start()
        finish()

    return pl.pallas_call(
        body, name=name, in_specs=[_ANY] * n, out_specs=(_ANY,) * n,
        out_shape=tuple(_sds(a.shape, a.dtype) for a in arrays), scratch_shapes=_scatter_scratch(n),
    )(*arrays)


def _gather_call(arrays, name):
    n = len(arrays)

    def body(*refs):
        start, finish = _gather_copies(refs[:n], refs[n:2 * n], *refs[2 * n:])
        start()
        finish()

    return pl.pallas_call(
        body, name=name, in_specs=[_ANY] * n, out_specs=(_ANY,) * n,
        out_shape=tuple(_sds((N_DEV,) + a.shape, a.dtype) for a in arrays), scratch_shapes=_scatter_scratch(n),
    )(*arrays)


def _call_with_scatter(body, *, name, grid, in_specs, out_specs, out_shape, scratch_shapes, operands,
                       scatter=(), gather=()):
    assert not (scatter and gather)
    carried = tuple(scatter) or tuple(gather)
    copies = _scatter_copies if scatter else _gather_copies
    recv_shapes = tuple(_sds(a.shape if scatter else (N_DEV,) + a.shape, a.dtype) for a in carried)
    nc, n_in, n_out, n_scr = len(carried), len(in_specs), len(out_specs), len(scratch_shapes)
    if nc == 0:
        return pl.pallas_call(
            body, name=name, grid=grid, in_specs=list(in_specs), out_specs=tuple(out_specs),
            out_shape=tuple(out_shape), scratch_shapes=list(scratch_shapes),
            compiler_params=_cparams(("arbitrary",) * len(grid)))(*operands), ()

    def wrapped(*refs):
        ins, refs = refs[:n_in], refs[n_in:]
        c_in, refs = refs[:nc], refs[nc:]
        outs, refs = refs[:n_out], refs[n_out:]
        c_out, refs = refs[:nc], refs[nc:]
        scr, sems = refs[:n_scr], refs[n_scr:]
        ids = [pl.program_id(a) for a in range(len(grid))]
        first = functools.reduce(jnp.logical_and, [i == 0 for i in ids])
        last = functools.reduce(jnp.logical_and, [i == g - 1 for i, g in zip(ids, grid)])
        start, finish = copies(c_in, c_out, *sems)
        pl.when(first)(start)
        body(*ins, *outs, *scr)
        pl.when(last)(finish)

    res = pl.pallas_call(
        wrapped, name=name, grid=grid,
        in_specs=list(in_specs) + [_ANY] * nc, out_specs=tuple(out_specs) + (_ANY,) * nc,
        out_shape=tuple(out_shape) + recv_shapes,
        scratch_shapes=list(scratch_shapes) + _scatter_scratch(nc),
        compiler_params=_cparams(("arbitrary",) * len(grid)),
    )(*operands, *carried)
    return res[:n_out], res[n_out:]


def _rms(x):
    inv = lax.rsqrt(jnp.mean(x * x, axis=-1, keepdims=True) + RMS_EPS)
    return x * inv, inv


def _rms_bwd(xn, inv, dxn):
    return inv * (dxn - xn * jnp.mean(dxn * xn, axis=-1, keepdims=True))


def _colsum(x):
    return jnp.sum(x, axis=0, keepdims=True)


def _sigmoid(x):
    return 0.5 * (jnp.tanh(0.5 * x) + 1.0)


def _split_bf16(x):
    hi = x.astype(BF16)
    return hi, (x - hi.astype(F32)).astype(BF16)


def _dot_split(x, e):
    hi, lo = _split_bf16(x)
    return jnp.dot(hi, e, preferred_element_type=F32) + jnp.dot(lo, e, preferred_element_type=F32)


@jax.custom_vjp
def _headsum(x, e, et):
    return _dot_split(_dot_split(x, e), et)


_headsum.defvjp(lambda x, e, et: (_headsum(x, e, et), (e, et)),
                lambda res, g: (_headsum(g, *res), jnp.zeros_like(res[0]), jnp.zeros_like(res[1])))


def _make_headsum(e, et):
    return lambda x: _headsum(x, e, et)


def _head_indicators(d):
    e = (jnp.arange(d)[:, None] // HEAD == jnp.arange(128)[None, :]).astype(BF16)
    return e, e.T


def _rwkv_elem(r, k, lw, la, w0, a0, k_k, k_a, headsum):
    ld = -DECAY_SCALE * _sigmoid(w0 + lw)
    a = _sigmoid(a0 + la)
    kkp = k * k_k
    kk = kkp * lax.rsqrt(jnp.maximum(headsum(kkp * kkp), L2_EPS * L2_EPS))
    k2 = k * (1.0 + (a - 1.0) * k_a)
    del r
    return ld, k2, -kk, kk * a


def _rwkv_post(y, r, k2, v, g, ln_g, ln_b, r_k, headsum):
    m = headsum(y) * (1.0 / HEAD)
    yc = y - m
    var = headsum(yc * yc) * (1.0 / HEAD)
    yn = yc * lax.rsqrt(var + GN_EPS)
    bonus = headsum(r * k2 * r_k) * v
    return (yn * ln_g + ln_b + bonus) * g


def _shift_down(h, first_row):
    rolled = pltpu.roll(h, 1, 0)
    row = lax.broadcasted_iota(jnp.int32, h.shape, 0)
    return jnp.where(row == 0, first_row, rolled)


def _shift_up(h, last_row):
    n = h.shape[0]
    rolled = pltpu.roll(h, n - 1, 0)
    row = lax.broadcasted_iota(jnp.int32, h.shape, 0)
    return jnp.where(row == n - 1, last_row, rolled)


def _gelu(p):
    return 0.5 * p * (1.0 + lax.erf(p * 0.7071067811865476))


def _gelu_grad(p):
    return 0.5 * (1.0 + lax.erf(p * 0.7071067811865476)) + p * jnp.exp(-0.5 * p * p) * 0.3989422804014327


def _tok(tm, d):
    return pl.BlockSpec((1, tm, d), lambda e, t, *_: (e, t, 0))


def _per_example(rows, d):
    return pl.BlockSpec((1, rows, d), lambda e, t, *_: (e, 0, 0))


def _whole(shape):
    nd = len(shape)
    return pl.BlockSpec(tuple(shape), lambda *_: (0,) * nd)


def _heads(nh, tm):
    return pl.BlockSpec((1, nh, tm, HEAD), lambda e, t, *_: (e, 0, t, 0))


def _sds(shape, dtype=F32):
    return jax.ShapeDtypeStruct(tuple(shape), dtype)


def _add_rows(ref, first, rows):
    @pl.when(first)
    def _():
        ref[0] = jnp.zeros(ref.shape[1:], ref.dtype)

    for i, r in enumerate(rows):
        ref[0, i:i + 1] += r


def _first(e, t):
    return jnp.logical_and(e == 0, t == 0)


def _ada_fwd(c_all, ada_w, ada_b_cols):
    nl, d, cols = ada_w.shape
    nb = c_all.shape[0]

    def body(c_ref, w_ref, b_ref, o_ref):
        c = c_ref[...]
        cond = c * _sigmoid(c)
        for i in range(nl):
            o_ref[i] = _dot(cond, w_ref[i]) + b_ref[i]

    return pl.pallas_call(
        body, name="ada_fwd", out_shape=_sds((nl, nb, cols)),
        compiler_params=_cparams(),
    )(c_all, ada_w, ada_b_cols)


def _ada_bwd(c_all, dmod_cols, dmod_full):
    nl, nb, cols = dmod_cols.shape
    d = c_all.shape[1]

    def body(c_ref, g_ref, f_ref, b_ref, *o_refs):
        c = c_ref[...]
        cond = c * _sigmoid(c)
        for i in range(nl):
            o_refs[i][0] = _dot_tn(cond, g_ref[i])
            b_ref[i:i + 1] = jnp.sum(f_ref[i], axis=0, keepdims=True)

    res = pl.pallas_call(
        body, name="ada_bwd", out_shape=(_sds((nl, dmod_full.shape[2])),) + (_sds((1, d, cols)),) * nl,
        compiler_params=_cparams(),
    )(c_all, dmod_cols, dmod_full)
    return res[1:], res[0]


def _matmul_tn(a, b, name, col_shards=None, gather=()):
    m, ka = a.shape
    n = b.shape[1]
    tm = min(m, 4096)
    tk = min(ka, 1024)
    tn = min(n, 1024)
    steps = m // tm
    if col_shards:
        cs = n // col_shards
        spt = tn // cs
        out_spec = pl.BlockSpec((spt, tk, cs), lambda i, j, s: (j, i, 0))
        out_shape = _sds((col_shards, ka, cs), BF16)
    else:
        out_spec = pl.BlockSpec((tk, tn), lambda i, j, s: (i, j))
        out_shape = _sds((ka, n), BF16)

    def body(a_ref, b_ref, o_ref, acc):
        s = pl.program_id(2)

        @pl.when(s == 0)
        def _():
            acc[...] = jnp.zeros_like(acc)

        acc[...] += _dot_tn(a_ref[...], b_ref[...])

        @pl.when(s == steps - 1)
        def _():
            if col_shards:
                for q in range(spt):
                    o_ref[q] = acc[:, q * cs:(q + 1) * cs].astype(BF16)
            else:
                o_ref[...] = acc[...].astype(BF16)

    res, got = _call_with_scatter(
        body, name=name, grid=(ka // tk, n // tn, steps),
        in_specs=[pl.BlockSpec((tm, tk), lambda i, j, s: (s, i)), pl.BlockSpec((tm, tn), lambda i, j, s: (s, j))],
        out_specs=(out_spec,), out_shape=(out_shape,),
        scratch_shapes=[pltpu.VMEM((tk, tn), F32)], operands=(a, b), gather=gather)
    return (res[0], got) if gather else res[0]


MLP_FWD_TM = 1024
MLP_FJ = 1024
MLP_BWD_TM = 512
MLP_BWD_FJ = 1024


def _mlp_fwd(x, mod, w1, w2, gather=()):
    bl, t, d = x.shape
    f = w1.shape[1]
    tm, fj = min(t, MLP_FWD_TM), min(f, MLP_FJ)
    nj = f // fj

    def body(x_ref, mod_ref, w1_ref, w2_ref, xo_ref, ff_ref, q_ref, h_scr, acc):
        j = pl.program_id(2)

        @pl.when(j == 0)
        def _():
            xn, _ = _rms(x_ref[0])
            h_scr[...] = (xn * (1.0 + mod_ref[0, 1:2]) + mod_ref[0, 0:1]).astype(BF16)
            acc[...] = jnp.zeros_like(acc)

        p = jnp.dot(h_scr[...], w1_ref[...], preferred_element_type=F32)
        q = jnp.square(jnp.maximum(p, 0.0)).astype(BF16)
        q_ref[0] = q
        acc[...] += jnp.dot(q, w2_ref[...], preferred_element_type=F32)

        @pl.when(j == nj - 1)
        def _():
            ff_ref[0] = acc[...]
            xo_ref[0] = x_ref[0] + mod_ref[0, 2:3] * acc[...]

    return _call_with_scatter(
        body, name="mlp_fwd", grid=(bl, t // tm, nj),
        in_specs=[_tok(tm, d), _per_example(8, d),
                  pl.BlockSpec((d, fj), lambda e, i, j: (0, j)), pl.BlockSpec((fj, d), lambda e, i, j: (j, 0))],
        out_specs=(_tok(tm, d), _tok(tm, d), pl.BlockSpec((1, tm, fj), lambda e, i, j: (e, i, j))),
        out_shape=(_sds(x.shape), _sds(x.shape), _sds((bl, t, f), BF16)),
        scratch_shapes=[pltpu.VMEM((tm, d), BF16), pltpu.VMEM((tm, d), F32)],
        operands=(x, mod, w1, w2), gather=gather)


def _mlp_bwd(x, dxo, ff, q, mod, w1, w2, scatter=()):
    bl, t, d = x.shape
    f = w1.shape[1]
    tm, fj = min(t, MLP_BWD_TM), min(f, MLP_BWD_FJ)
    nj = f // fj

    def body(x_ref, dxo_ref, ff_ref, q_ref, mod_ref, w1_ref, w2_ref,
             dx_ref, dmod_ref, h_ref, dff_ref, dp_ref, acc):
        ti, j = pl.program_id(1), pl.program_id(2)

        @pl.when(j == 0)
        def _():
            xn, _ = _rms(x_ref[0])
            h_ref[0] = (xn * (1.0 + mod_ref[0, 1:2]) + mod_ref[0, 0:1]).astype(BF16)
            dff_ref[0] = (mod_ref[0, 2:3] * dxo_ref[0]).astype(BF16)
            acc[...] = jnp.zeros_like(acc)

        rl = jnp.sqrt(q_ref[0].astype(F32))
        dp = (_dot_nt(dff_ref[0], w2_ref[...]) * (2.0 * rl)).astype(BF16)
        dp_ref[0] = dp
        acc[...] += _dot_nt(dp, w1_ref[...])

        @pl.when(j == nj - 1)
        def _():
            xn, inv = _rms(x_ref[0])
            dh = acc[...]
            dx_ref[0] = dxo_ref[0] + _rms_bwd(xn, inv, dh * (1.0 + mod_ref[0, 1:2]))
            _add_rows(dmod_ref, ti == 0, [_colsum(dh), _colsum(dh * xn), _colsum(dxo_ref[0] * ff_ref[0])])

    big = lambda: pl.BlockSpec((1, tm, fj), lambda e, i, j: (e, i, j))
    return _call_with_scatter(
        body, name="mlp_bwd", grid=(bl, t // tm, nj),
        in_specs=[_tok(tm, d), _tok(tm, d), _tok(tm, d), big(), _per_example(8, d),
                  pl.BlockSpec((d, fj), lambda e, i, j: (0, j)), pl.BlockSpec((fj, d), lambda e, i, j: (j, 0))],
        out_specs=(_tok(tm, d), _per_example(8, d), _tok(tm, d), _tok(tm, d), big()),
        out_shape=(_sds(x.shape), _sds((bl, 8, d)), _sds(x.shape, BF16), _sds(x.shape, BF16),
                   _sds((bl, t, f), BF16)),
        scratch_shapes=[pltpu.VMEM((tm, d), F32)],
        operands=(x, dxo, ff, q, mod, w1, w2), scatter=scatter)


SGU_TM = 512


def _sgu_core(x, mod_ref, win_ref, lng, lnb, ws_ref, bias_ref):
    tm, d = x.shape
    xn, inv = _rms(x)
    h = (xn * (1.0 + mod_ref[0, 1:2]) + mod_ref[0, 0:1]).astype(BF16)
    pre = jnp.dot(h, win_ref[...], preferred_element_type=F32)
    uv = _gelu(pre)
    u, v = uv[:, :d], uv[:, d:]
    mu = jnp.mean(v, axis=-1, keepdims=True)
    vc = v - mu
    rstd = lax.rsqrt(jnp.mean(vc * vc, axis=-1, keepdims=True) + LN_EPS)
    vhat = vc * rstd
    vln = vhat * lng + lnb
    gd = d // SGU_GROUPS
    rows = []
    for c in range(tm // SGU_CHUNK):
        cols = []
        for g in range(SGU_GROUPS):
            cols.append(_dot(ws_ref[g], vln[c * SGU_CHUNK:(c + 1) * SGU_CHUNK, g * gd:(g + 1) * gd]))
        rows.append(jnp.concatenate(cols, axis=1) + bias_ref[...])
    sv = jnp.concatenate(rows, axis=0)
    return xn, inv, h, pre, u, vhat, rstd, vln, sv


def _sgu_masked(ws_ref, wm_scr):
    row = lax.broadcasted_iota(jnp.int32, (SGU_CHUNK, SGU_CHUNK), 0)
    col = lax.broadcasted_iota(jnp.int32, (SGU_CHUNK, SGU_CHUNK), 1)
    for g in range(SGU_GROUPS):
        wm_scr[g] = jnp.where(row >= col, ws_ref[g], 0.0).astype(BF16)


def _sgu_fwd(x, mod, w_in, ln_g, ln_b, w_s, bias_full, w_out, gather=()):
    bl, t, d = x.shape
    tm = min(t, SGU_TM)

    def body(x_ref, mod_ref, win_ref, lng_ref, lnb_ref, ws_ref, bias_ref, wout_ref, xo_ref, mix_ref, wm_scr):
        _sgu_masked(ws_ref, wm_scr)
        xt = x_ref[0]
        *_, u, _, _, _, sv = _sgu_core(xt, mod_ref, win_ref, lng_ref[...], lnb_ref[...], wm_scr, bias_ref)
        mix = _dot(u * sv, wout_ref[...])
        mix_ref[0] = mix
        xo_ref[0] = xt + mod_ref[0, 2:3] * mix

    return _call_with_scatter(
        body, name="sgu_fwd", grid=(bl, t // tm),
        in_specs=[_tok(tm, d), _per_example(8, d), _whole(w_in.shape), _whole(ln_g.shape), _whole(ln_b.shape),
                  _whole(w_s.shape), _whole(bias_full.shape), _whole(w_out.shape)],
        out_specs=(_tok(tm, d), _tok(tm, d)),
        out_shape=(_sds(x.shape), _sds(x.shape)),
        scratch_shapes=[pltpu.VMEM(w_s.shape, BF16)],
        operands=(x, mod, w_in, ln_g, ln_b, w_s, bias_full, w_out), gather=gather)


def _sgu_bwd(x, dxo, mix, mod, w_in, ln_g, ln_b, w_s, bias_full, w_out, group_ind, scatter=()):
    bl, t, d = x.shape
    tm = min(t, SGU_TM)
    gd = d // SGU_GROUPS

    def body(x_ref, dxo_ref, mix_ref, mod_ref, win_ref, lng_ref, lnb_ref, ws_ref, bias_ref, wout_ref, ind_ref,
             dx_ref, dmod_ref, h_ref, dpre_ref, z_ref, dmix_ref, small_ref, dws_ref, dbs_ref, wm_scr, dbias_scr):
        e, ti = pl.program_id(0), pl.program_id(1)
        _sgu_masked(ws_ref, wm_scr)
        xt, dxo = x_ref[0], dxo_ref[0]
        lng = lng_ref[...]
        xn, inv, h, pre, u, vhat, rstd, vln, sv = _sgu_core(xt, mod_ref, win_ref, lng, lnb_ref[...], wm_scr, bias_ref)
        h_ref[0] = h
        z_ref[0] = (u * sv).astype(BF16)
        dmix = mod_ref[0, 2:3] * dxo
        dmix_ref[0] = dmix.astype(BF16)
        dz = _dot_nt(dmix, wout_ref[...])
        du, dsv = dz * sv, dz * u

        @pl.when(_first(e, ti))
        def _():
            dws_ref[...] = jnp.zeros_like(dws_ref)
            dbias_scr[...] = jnp.zeros_like(dbias_scr)
            small_ref[...] = jnp.zeros_like(small_ref)

        row = lax.broadcasted_iota(jnp.int32, (SGU_CHUNK, SGU_CHUNK), 0)
        col = lax.broadcasted_iota(jnp.int32, (SGU_CHUNK, SGU_CHUNK), 1)
        rows = []
        for c in range(tm // SGU_CHUNK):
            rs = slice(c * SGU_CHUNK, (c + 1) * SGU_CHUNK)
            dbias_scr[...] += dsv[rs]
            cols = []
            for g in range(SGU_GROUPS):
                cs = slice(g * gd, (g + 1) * gd)
                cols.append(_dot_tn(wm_scr[g], dsv[rs, cs]))
                dws_ref[g] += jnp.where(row >= col, _dot_nt(dsv[rs, cs], vln[rs, cs]), 0.0)
            rows.append(jnp.concatenate(cols, axis=1))
        dvln = jnp.concatenate(rows, axis=0)
        small_ref[0:1] += _colsum(dvln * vhat)
        small_ref[1:2] += _colsum(dvln)
        dvhat = dvln * lng
        dv = rstd * (dvhat - jnp.mean(dvhat, axis=-1, keepdims=True)
                     - vhat * jnp.mean(dvhat * vhat, axis=-1, keepdims=True))
        dpre = (jnp.concatenate([du, dv], axis=1) * _gelu_grad(pre)).astype(BF16)
        dpre_ref[0] = dpre
        dh = _dot_nt(dpre, win_ref[...])
        dx_ref[0] = dxo + _rms_bwd(xn, inv, dh * (1.0 + mod_ref[0, 1:2]))
        _add_rows(dmod_ref, ti == 0, [_colsum(dh), _colsum(dh * xn), _colsum(dxo * mix_ref[0])])

        @pl.when(jnp.logical_and(e == bl - 1, ti == t // tm - 1))
        def _():
            hi, lo = _split_bf16(dbias_scr[...])
            ind = ind_ref[...]
            dbs_ref[...] = (lax.dot_general(ind, hi, (((1,), (1,)), ((), ())), preferred_element_type=F32)
                            + lax.dot_general(ind, lo, (((1,), (1,)), ((), ())), preferred_element_type=F32))

    return _call_with_scatter(
        body, name="sgu_bwd", grid=(bl, t // tm),
        in_specs=[_tok(tm, d), _tok(tm, d), _tok(tm, d), _per_example(8, d), _whole(w_in.shape), _whole(ln_g.shape),
                  _whole(ln_b.shape), _whole(w_s.shape), _whole(bias_full.shape), _whole(w_out.shape),
                  _whole(group_ind.shape)],
        out_specs=(_tok(tm, d), _per_example(8, d), _tok(tm, d), _tok(tm, 2 * d), _tok(tm, d), _tok(tm, d),
                   _whole((8, d)), _whole(w_s.shape), _whole((SGU_GROUPS, SGU_CHUNK))),
        out_shape=(_sds(x.shape), _sds((bl, 8, d)), _sds(x.shape, BF16), _sds((bl, t, 2 * d), BF16),
                   _sds(x.shape, BF16), _sds(x.shape, BF16), _sds((8, d)), _sds(w_s.shape),
                   _sds((SGU_GROUPS, SGU_CHUNK))),
        scratch_shapes=[pltpu.VMEM(w_s.shape, BF16), pltpu.VMEM((SGU_CHUNK, d), F32)],
        operands=(x, dxo, mix, mod, w_in, ln_g, ln_b, w_s, bias_full, w_out, group_ind), scatter=scatter)


RWKV_TM = 256
N_VEC = 16


def _rwkv_pre_core(x_ref, halo_ref, mod_ref, vec_ref, ti):
    xn, inv = _rms(x_ref[0])
    scale1, shift = 1.0 + mod_ref[0, 1:2], mod_ref[0, 0:1]
    h = xn * scale1 + shift
    hn, _ = _rms(halo_ref[0])
    hh = hn * scale1 + shift
    first = jnp.where(ti == 0, 0.0, hh[7:8])
    xx = _shift_down(h, first) - h
    xs = [h + xx * vec_ref[i:i + 1] for i in range(6)]
    return xn, inv, xx, xs


def _rwkv_proj(xs, wrkv_ref, w1_ref, a1_ref, g1_ref, w2_ref, a2_ref, g2_ref):
    d = xs[0].shape[1]
    xr, xw, xk, xv, xa, xg = [z.astype(BF16) for z in xs]
    r = jnp.dot(xr, wrkv_ref[:, 0:d], preferred_element_type=F32)
    k = jnp.dot(xk, wrkv_ref[:, d:2 * d], preferred_element_type=F32)
    v = jnp.dot(xv, wrkv_ref[:, 2 * d:3 * d], preferred_element_type=F32)
    tw2 = jnp.tanh(jnp.dot(xw, w1_ref[...], preferred_element_type=F32))
    ta = jnp.dot(xa, a1_ref[...], preferred_element_type=F32)
    sg = _sigmoid(jnp.dot(xg, g1_ref[...], preferred_element_type=F32))
    lw, la, g = _dot(tw2, w2_ref[...]), _dot(ta, a2_ref[...]), _dot(sg, g2_ref[...])
    return (xr, xw, xk, xv, xa, xg), r, k, v, tw2, ta, sg, lw, la, g


def _to_heads(ref, val, nh):
    for hd in range(nh):
        ref[0, hd] = val[:, hd * HEAD:(hd + 1) * HEAD]


def _from_heads(ref, scr, nh):
    for hd in range(nh):
        scr[:, hd * HEAD:(hd + 1) * HEAD] = ref[0, hd]
    return scr[...]


def _rwkv_weight_specs(ws):
    return [_whole(w.shape) for w in ws]


def _rwkv_pre_fwd(x, mod, vec, e_ind, et_ind, weights):
    bl, t, d = x.shape
    tm = min(t, RWKV_TM)
    nh = d // HEAD
    hb = tm // 8

    def body(x_ref, halo_ref, mod_ref, vec_ref, e_ref, et_ref, wrkv, w1, a1, g1, w2, a2, g2,
             r_ref, ld_ref, k2_ref, v_ref, as_ref, bs_ref, g_ref):
        ti = pl.program_id(1)
        _, _, _, xs = _rwkv_pre_core(x_ref, halo_ref, mod_ref, vec_ref, ti)
        _, r, k, v, _, _, _, lw, la, g = _rwkv_proj(xs, wrkv, w1, a1, g1, w2, a2, g2)
        headsum = _make_headsum(e_ref[...], et_ref[...])
        ld, k2, a_s, b_s = _rwkv_elem(r, k, lw, la, vec_ref[6:7], vec_ref[7:8], vec_ref[8:9], vec_ref[9:10], headsum)
        g_ref[0] = g
        for ref, val in ((r_ref, r), (ld_ref, ld), (k2_ref, k2), (v_ref, v), (as_ref, a_s), (bs_ref, b_s)):
            _to_heads(ref, val, nh)

    halo = pl.BlockSpec((1, 8, d), lambda e, i: (e, jnp.maximum(i * hb - 1, 0), 0))
    hs = _sds((bl, nh, t, HEAD))
    return pl.pallas_call(
        body, name="rwkv_pre_fwd", grid=(bl, t // tm),
        in_specs=[_tok(tm, d), halo, _per_example(8, d), _whole(vec.shape), _whole(e_ind.shape), _whole(et_ind.shape)]
        + _rwkv_weight_specs(weights),
        out_specs=(_heads(nh, tm),) * 6 + (_tok(tm, d),),
        out_shape=(hs,) * 6 + (_sds(x.shape),),
        compiler_params=_cparams(("arbitrary", "arbitrary")),
    )(x, x, mod, vec, e_ind, et_ind, *weights)


def _rwkv_post_fwd(x, y, r, k2, v, g, mod, vec, e_ind, et_ind, w_out):
    bl, t, d = x.shape
    tm = min(t, RWKV_TM)
    nh = d // HEAD

    def body(x_ref, y_ref, r_ref, k2_ref, v_ref, g_ref, mod_ref, vec_ref, e_ref, et_ref, wout_ref,
             xo_ref, mix_ref, s0, s1, s2, s3):
        headsum = _make_headsum(e_ref[...], et_ref[...])
        yv, rv, kv, vv = (_from_heads(ref, scr, nh) for ref, scr in
                          ((y_ref, s0), (r_ref, s1), (k2_ref, s2), (v_ref, s3)))
        o = _rwkv_post(yv, rv, kv, vv, g_ref[0], vec_ref[10:11], vec_ref[11:12], vec_ref[12:13], headsum)
        mix = _dot(o, wout_ref[...])
        mix_ref[0] = mix
        xo_ref[0] = x_ref[0] + mod_ref[0, 2:3] * mix

    return pl.pallas_call(
        body, name="rwkv_post_fwd", grid=(bl, t // tm),
        in_specs=[_tok(tm, d)] + [_heads(nh, tm)] * 4 + [_tok(tm, d), _per_example(8, d), _whole(vec.shape),
                                                         _whole(e_ind.shape), _whole(et_ind.shape), _whole(w_out.shape)],
        out_specs=(_tok(tm, d), _tok(tm, d)),
        out_shape=(_sds(x.shape), _sds(x.shape)),
        scratch_shapes=[pltpu.VMEM((tm, d), F32)] * 4,
        compiler_params=_cparams(("arbitrary", "arbitrary")),
    )(x, y, r, k2, v, g, mod, vec, e_ind, et_ind, w_out)


def _rwkv_post_bwd(dxo, mix, y, r, k2, v, g, mod, vec, e_ind, et_ind, w_out, scatter=()):
    bl, t, d = dxo.shape
    tm = min(t, RWKV_TM)
    nh = d // HEAD

    def body(dxo_ref, mix_ref, y_ref, r_ref, k2_ref, v_ref, g_ref, mod_ref, vec_ref, e_ref, et_ref, wout_ref,
             dy_ref, dr_ref, dk2_ref, dv_ref, dg_ref, o_ref, dmix_ref, dgate_ref, small_ref, s0, s1, s2, s3):
        e, ti = pl.program_id(0), pl.program_id(1)
        headsum = _make_headsum(e_ref[...], et_ref[...])
        yv, rv, kv, vv = (_from_heads(ref, scr, nh) for ref, scr in
                          ((y_ref, s0), (r_ref, s1), (k2_ref, s2), (v_ref, s3)))
        dxo = dxo_ref[0]
        dmix = mod_ref[0, 2:3] * dxo
        dmix_ref[0] = dmix.astype(BF16)
        do = _dot_nt(dmix, wout_ref[...])
        post = functools.partial(_rwkv_post, headsum=headsum)
        o, vjp = jax.vjp(post, yv, rv, kv, vv, g_ref[0], vec_ref[10:11], vec_ref[11:12], vec_ref[12:13])
        o_ref[0] = o.astype(BF16)
        dy, dr, dk2, dv, dg, dlng, dlnb, drk = vjp(do)
        _to_heads(dy_ref, dy, nh)
        dr_ref[0], dk2_ref[0], dv_ref[0], dg_ref[0] = dr, dk2, dv, dg
        zero = jnp.zeros((1, d), F32)
        _add_rows(dgate_ref, ti == 0, [zero, zero, _colsum(dxo * mix_ref[0])])

        @pl.when(_first(e, ti))
        def _():
            small_ref[...] = jnp.zeros_like(small_ref)

        small_ref[0:1] += dlng
        small_ref[1:2] += dlnb
        small_ref[2:3] += drk

    return _call_with_scatter(
        body, name="rwkv_post_bwd", grid=(bl, t // tm),
        in_specs=[_tok(tm, d), _tok(tm, d)] + [_heads(nh, tm)] * 4
        + [_tok(tm, d), _per_example(8, d), _whole(vec.shape), _whole(e_ind.shape), _whole(et_ind.shape),
           _whole(w_out.shape)],
        out_specs=(_heads(nh, tm),) + (_tok(tm, d),) * 6 + (_per_example(8, d), _whole((8, d))),
        out_shape=(_sds((bl, nh, t, HEAD)),) + (_sds(dxo.shape),) * 4 + (_sds(dxo.shape, BF16),) * 2
        + (_sds((bl, 8, d)), _sds((8, d))),
        scratch_shapes=[pltpu.VMEM((tm, d), F32)] * 4,
        operands=(dxo, mix, y, r, k2, v, g, mod, vec, e_ind, et_ind, w_out), scatter=scatter)


RWKV_BWD_TM = 128


def _rwkv_pre_bwd(x, mod, vec, e_ind, et_ind, weights, dr_p, dk2_p, dv_p, dg, dr_s, dld, dk2_s, dv_s, das, dbs):
    bl, t, d = x.shape
    tm = min(t, RWKV_BWD_TM)
    nh = d // HEAD
    hb = tm // 8
    lp, gp = LORA_PAD, GATE_PAD

    def body(x_ref, halo_ref, mod_ref, vec_ref, e_ref, et_ref, wrkv, w1, a1, g1, w2, a2, g2,
             drp_ref, dk2p_ref, dvp_ref, dg_ref, drs_ref, dld_ref, dk2s_ref, dvs_ref, das_ref, dbs_ref,
             dh_ref, dhp_ref, xr_ref, xw_ref, xk_ref, xv_ref, xa_ref, xg_ref, dr_ref, dk_ref, dv_ref,
             dtw_ref, dta_ref, dtg_ref, tw2_ref, ta_ref, sg_ref, dlw_ref, dla_ref, dgb_ref, small_ref,
             s0, s1, s2, s3, s4, s5):
        e, ti = pl.program_id(0), pl.program_id(1)
        _, _, xx, xs = _rwkv_pre_core(x_ref, halo_ref, mod_ref, vec_ref, ti)
        xb, r, k, v, tw2, ta, sg, lw, la, _ = _rwkv_proj(xs, wrkv, w1, a1, g1, w2, a2, g2)
        for ref, val in zip((xr_ref, xw_ref, xk_ref, xv_ref, xa_ref, xg_ref), xb):
            ref[0] = val
        headsum = _make_headsum(e_ref[...], et_ref[...])
        drs, dld, dk2s, dvs, das, dbs_ = (_from_heads(ref, scr, nh) for ref, scr in
                                          ((drs_ref, s0), (dld_ref, s1), (dk2s_ref, s2), (dvs_ref, s3),
                                           (das_ref, s4), (dbs_ref, s5)))
        elem = functools.partial(_rwkv_elem, r, headsum=headsum)
        _, vjp = jax.vjp(elem, k, lw, la, vec_ref[6:7], vec_ref[7:8], vec_ref[8:9], vec_ref[9:10])
        dk, dlw, dla, dw0, da0, dkk, dka = vjp((dld, dk2p_ref[0] + dk2s, das, dbs_))
        dr = drp_ref[0] + drs
        dv = dvp_ref[0] + dvs
        dgv = dg_ref[0]
        dtg = _dot_nt(dgv, g2[...]) * sg * (1.0 - sg)
        dtw = _dot_nt(dlw, w2[...]) * (1.0 - tw2 * tw2)
        dta = _dot_nt(dla, a2[...])
        dr_ref[0], dk_ref[0], dv_ref[0] = dr.astype(BF16), dk.astype(BF16), dv.astype(BF16)
        dtw_ref[0], dta_ref[0], dtg_ref[0] = dtw.astype(BF16), dta.astype(BF16), dtg.astype(BF16)
        tw2_ref[0], ta_ref[0], sg_ref[0] = tw2.astype(BF16), ta.astype(BF16), sg.astype(BF16)
        dlw_ref[0], dla_ref[0], dgb_ref[0] = dlw.astype(BF16), dla.astype(BF16), dgv.astype(BF16)
        dxs = (_dot_nt(dr, wrkv[:, 0:d]), _dot_nt(dtw, w1[...]), _dot_nt(dk, wrkv[:, d:2 * d]),
               _dot_nt(dv, wrkv[:, 2 * d:3 * d]), _dot_nt(dta, a1[...]), _dot_nt(dtg, g1[...]))

        @pl.when(_first(e, ti))
        def _():
            small_ref[...] = jnp.zeros_like(small_ref)

        total = jnp.zeros((tm, d), F32)
        dhp = jnp.zeros((tm, d), F32)
        for i, dxi in enumerate(dxs):
            total += dxi
            dhp += dxi * vec_ref[i:i + 1]
            small_ref[i:i + 1] += _colsum(dxi * xx)
        dh_ref[0], dhp_ref[0] = total - dhp, dhp
        small_ref[6:7] += dw0
        small_ref[7:8] += da0
        small_ref[8:9] += dkk
        small_ref[9:10] += dka

    halo = pl.BlockSpec((1, 8, d), lambda e, i: (e, jnp.maximum(i * hb - 1, 0), 0))
    tokd, tokl, tokg = _tok(tm, d), _tok(tm, lp), _tok(tm, gp)
    bf = lambda w: _sds((bl, t, w), BF16)
    return pl.pallas_call(
        body, name="rwkv_pre_bwd", grid=(bl, t // tm),
        in_specs=[tokd, halo, _per_example(8, d), _whole(vec.shape), _whole(e_ind.shape), _whole(et_ind.shape)]
        + _rwkv_weight_specs(weights) + [tokd] * 4 + [_heads(nh, tm)] * 6,
        out_specs=(tokd, tokd) + (tokd,) * 6 + (tokd,) * 3 + (tokl, tokl, tokg, tokl, tokl, tokg)
        + (tokd, tokd, tokd, _whole((N_VEC, d))),
        out_shape=(_sds(x.shape), _sds(x.shape)) + (bf(d),) * 9 + (bf(lp), bf(lp), bf(gp), bf(lp), bf(lp), bf(gp))
        + (bf(d), bf(d), bf(d), _sds((N_VEC, d))),
        scratch_shapes=[pltpu.VMEM((tm, d), F32)] * 6,
        compiler_params=_cparams(("arbitrary", "arbitrary")),
    )(x, x, mod, vec, e_ind, et_ind, *weights, dr_p, dk2_p, dv_p, dg, dr_s, dld, dk2_s, dv_s, das, dbs)


NORM_BWD_TM = 512
FINAL_TM = 1024


def _norm_bwd(x, dxo, dh, dhprev, mod, dgate):
    bl, t, d = x.shape
    tm = min(t, NORM_BWD_TM)
    hb = tm // 8
    last_blk = t // 8 - 1

    def body(x_ref, dxo_ref, dh_ref, dhp_ref, nxt_ref, mod_ref, dgate_ref, dx_ref, dmod_ref):
        ti = pl.program_id(1)
        xn, inv = _rms(x_ref[0])
        last = jnp.where(ti == t // tm - 1, 0.0, nxt_ref[0, 0:1])
        dh = dh_ref[0] + _shift_up(dhp_ref[0], last)
        dx_ref[0] = dxo_ref[0] + _rms_bwd(xn, inv, dh * (1.0 + mod_ref[0, 1:2]))

        @pl.when(ti == 0)
        def _():
            dmod_ref[0] = dgate_ref[0]

        dmod_ref[0, 0:1] += _colsum(dh)
        dmod_ref[0, 1:2] += _colsum(dh * xn)

    nxt = pl.BlockSpec((1, 8, d), lambda e, i: (e, jnp.minimum((i + 1) * hb, last_blk), 0))
    return pl.pallas_call(
        body, name="norm_bwd", grid=(bl, t // tm),
        in_specs=[_tok(tm, d)] * 4 + [nxt, _per_example(8, d), _per_example(8, d)],
        out_specs=(_tok(tm, d), _per_example(8, d)),
        out_shape=(_sds(x.shape), _sds((bl, 8, d))),
        compiler_params=_cparams(("arbitrary", "arbitrary")),
    )(x, dxo, dh, dhprev, dhprev, mod, dgate)


def _final(x, target, final_g):
    bl, t, d = x.shape
    tm = min(t, FINAL_TM)

    def body(x_ref, tgt_ref, g_ref, dx_ref, loss_ref, dg_ref):
        e, ti = pl.program_id(0), pl.program_id(1)

        @pl.when(_first(e, ti))
        def _():
            loss_ref[...] = jnp.zeros_like(loss_ref)
            dg_ref[...] = jnp.zeros_like(dg_ref)

        xn, inv = _rms(x_ref[0])
        err = xn * g_ref[...] - tgt_ref[0]
        loss_ref[...] += (0.5 / d) * jnp.sum(err * err)
        dy = err * (1.0 / d)
        dg_ref[0:1] += _colsum(dy * xn)
        dx_ref[0] = _rms_bwd(xn, inv, dy * g_ref[...])

    return pl.pallas_call(
        body, name="final_loss", grid=(bl, t // tm),
        in_specs=[_tok(tm, d), _tok(tm, d), _whole(final_g.shape)],
        out_specs=(_tok(tm, d), _whole((8, 128)), _whole((8, d))),
        out_shape=(_sds(x.shape), _sds((8, 128)), _sds((8, d))),
        compiler_params=_cparams(("arbitrary", "arbitrary")),
    )(x, target, final_g)


def _adamw_math(w, g, m, v):
    m = ADAM_B1 * m + (1.0 - ADAM_B1) * g
    v = ADAM_B2 * v + (1.0 - ADAM_B2) * jnp.square(g)
    m_hat = m / (1.0 - ADAM_B1 ** ADAM_STEP)
    v_hat = v / (1.0 - ADAM_B2 ** ADAM_STEP)
    return -ADAM_LR * (m_hat / (jnp.sqrt(v_hat) + ADAM_EPS) + ADAM_WD * w), m, v


def _sum_parts(ref, n):
    g = ref[0].astype(F32)
    for s in range(1, n):
        g = g + ref[s].astype(F32)
    return g


def _adamw_layers(w, m, v, parts, name):
    nl, rows, c = w.shape
    tr = min(rows, 256)

    def body(w_ref, m_ref, v_ref, *refs):
        p_refs, (g_ref, d_ref, mo_ref, vo_ref) = refs[:nl], refs[nl:]
        for layer in range(nl):
            @pl.when(pl.program_id(0) == layer)
            def _(p_ref=p_refs[layer]):
                g = _sum_parts(p_ref, p_ref.shape[0])
                g_ref[...] = g
                d_ref[...], mo_ref[...], vo_ref[...] = _adamw_math(w_ref[...], g, m_ref[...], v_ref[...])

    row = pl.BlockSpec((None, tr, c), lambda l, i: (l, i, 0))
    return pl.pallas_call(
        body, name=name, grid=(nl, rows // tr),
        in_specs=[row, row, row] + [pl.BlockSpec((p.shape[0], tr, c), lambda l, i, k=k: (0, jnp.where(l == k, i, 0), 0))
                                    for k, p in enumerate(parts)],
        out_specs=(row,) * 4, out_shape=(_sds(w.shape),) * 4,
        compiler_params=_cparams(("arbitrary", "arbitrary")),
    )(w, m, v, *parts)


def _adamw_small(items, name):
    k = len(items)
    ns = [it[3].shape[0] for it in items]

    def body(*refs):
        ins, outs = refs[:4 * k], refs[4 * k:]
        for i in range(k):
            w_ref, m_ref, v_ref, p_ref = ins[4 * i:4 * i + 4]
            g = _sum_parts(p_ref, ns[i])
            outs[4 * i][...] = g
            outs[4 * i + 1][...], outs[4 * i + 2][...], outs[4 * i + 3][...] = _adamw_math(
                w_ref[...], g, m_ref[...], v_ref[...])

    flat = [a for it in items for a in it]
    res = pl.pallas_call(
        body, name=name,
        out_shape=tuple(_sds(it[0].shape) for it in items for _ in range(4)),
        compiler_params=_cparams(),
    )(*flat)
    return [tuple(res[4 * i:4 * i + 4]) for i in range(k)]


WEIGHTS = ['ada_w', 'ada_b', 'mlp_w1', 'mlp_w2', 'a_w_in', 'a_ln_g', 'a_ln_b', 'a_w_s', 'a_b_s', 'a_w_out', 'b_mu',
           'b_w_in', 'b_w0', 'b_w1', 'b_w2', 'b_a0', 'b_a1', 'b_a2', 'b_g1', 'b_g2', 'b_k_k', 'b_k_a', 'b_r_k',
           'b_ln_g', 'b_ln_b', 'b_w_out', 'final_g']
VECTORS = ['b_mu', 'b_w0', 'b_a0', 'b_k_k', 'b_k_a', 'b_ln_g', 'b_ln_b']
REPLICATED = ['a_ln_g', 'a_ln_b', 'a_w_s', 'a_b_s', 'b_r_k', 'final_g']
ROW_ALIGN = 16


def _pad_rows(a, mult):
    pad = (-a.shape[-2]) % mult
    return jnp.pad(a, [(0, 0)] * (a.ndim - 2) + [(0, pad), (0, 0)]) if pad else a


def _as2d(a):
    if a.ndim == 1:
        return a.reshape(1, -1)
    lead = 1
    for s in a.shape[:-1]:
        lead *= s
    return a.reshape(lead, a.shape[-1])


def kernel(x, c, ada_w, ada_b, mlp_w1, mlp_w2, a_w_in, a_ln_g, a_ln_b, a_w_s, a_b_s, a_w_out, b_mu, b_w_in, b_w0, b_w1, b_w2, b_a0, b_a1, b_a2, b_g1, b_g2, b_k_k, b_k_a, b_r_k, b_ln_g, b_ln_b, b_w_out, final_g, loss_target, m_ada_w, m_ada_b, m_mlp_w1, m_mlp_w2, m_a_w_in, m_a_ln_g, m_a_ln_b, m_a_w_s, m_a_b_s, m_a_w_out, m_b_mu, m_b_w_in, m_b_w0, m_b_w1, m_b_w2, m_b_a0, m_b_a1, m_b_a2, m_b_g1, m_b_g2, m_b_k_k, m_b_k_a, m_b_r_k, m_b_ln_g, m_b_ln_b, m_b_w_out, m_final_g, v_ada_w, v_ada_b, v_mlp_w1, v_mlp_w2, v_a_w_in, v_a_ln_g, v_a_ln_b, v_a_w_s, v_a_b_s, v_a_w_out, v_b_mu, v_b_w_in, v_b_w0, v_b_w1, v_b_w2, v_b_a0, v_b_a1, v_b_a2, v_b_g1, v_b_g2, v_b_k_k, v_b_k_a, v_b_r_k, v_b_ln_g, v_b_ln_b, v_b_w_out, v_final_g):
    given = dict(locals())
    w = {n: given[n] for n in WEIGHTS}
    bl, t, d = x.shape
    nl = ada_w.shape[0]
    nb = N_DEV * bl
    m_tok = bl * t
    me = 4 * lax.axis_index("x") + 2 * lax.axis_index("y") + lax.axis_index("c")

    bf = lambda a: a.astype(BF16)
    vec_loc = _pad_rows(jnp.concatenate([_as2d(w[n]) for n in VECTORS], axis=0), ROW_ALIGN)
    g_c, g_vec, g_a_in, g_a_out = _gather_call((c, vec_loc, bf(a_w_in[0]), bf(a_w_out[0])), "gather_first")

    c_all = g_c.reshape(nb, d)
    cols = ada_w.shape[2]
    ada_b_cols = lax.dynamic_slice(ada_b, (0, me * cols), (nl, cols)).reshape(nl, 1, cols)
    mod_cols = _ada_fwd(c_all, ada_w, ada_b_cols)
    mod_full = jnp.moveaxis(_gather_call((mod_cols,), "gather_mod")[0], 0, 2).reshape(nl, nb, 6 * d)
    mod_mine = lax.dynamic_slice(mod_full, (0, me * bl, 0), (nl, bl, 6 * d)).reshape(nl, bl, 6, d)
    mod_mix = jnp.pad(mod_mine[:, :, 0:3], ((0, 0), (0, 0), (0, 5), (0, 0)))
    mod_mlp = jnp.pad(mod_mine[:, :, 3:6], ((0, 0), (0, 0), (0, 5), (0, 0)))

    def unshard(g, ax):
        g = jnp.moveaxis(g, 0, ax)
        return g.reshape(g.shape[:ax] + (g.shape[ax] * g.shape[ax + 1],) + g.shape[ax + 2:])

    lora_names = ['b_w1', 'b_a1', 'b_g1', 'b_w2', 'b_a2', 'b_g2']
    lora_pack = jnp.concatenate([bf(w[n]).reshape(-1) for n in lora_names]).reshape(-1, 128)
    full = {'a_w_in': unshard(g_a_in, 1), 'a_w_out': unshard(g_a_out, 0)}
    n_vec_rows = sum(_as2d(w[n]).shape[0] for n in VECTORS)
    vec = jnp.moveaxis(g_vec, 0, 1).reshape(N_VEC, d)
    vec = vec.at[n_vec_rows].set(b_r_k.reshape(d))

    e_ind, et_ind = _head_indicators(d)
    gd = d // SGU_GROUPS
    group_ind = (jnp.arange(SGU_GROUPS)[:, None] == jnp.arange(d)[None, :] // gd).astype(BF16)
    bias_full = jnp.repeat(a_b_s[0].T, gd, axis=1)
    pad_c = lambda a, n: jnp.pad(a, ((0, 0), (0, n - a.shape[1])))
    pad_r = lambda a, n: jnp.pad(a, ((0, n - a.shape[0]), (0, 0)))
    sgu_args = (full['a_w_in'], a_ln_g, a_ln_b, a_w_s[0], bias_full, full['a_w_out'])

    x0 = x
    (x1, mix_a), (g_w1_0, g_w2_0) = _sgu_fwd(x0, mod_mix[0], *sgu_args, gather=(bf(mlp_w1[0]), bf(mlp_w2[0])))
    w1_full = [unshard(g_w1_0, 1), None]
    w2_full = [unshard(g_w2_0, 0), None]
    (x2, ff0, q0), (g_w1_1, g_w2_1, g_b_in, g_b_out, g_lora) = _mlp_fwd(
        x1, mod_mlp[0], w1_full[0], w2_full[0],
        gather=(bf(mlp_w1[1]), bf(mlp_w2[1]), bf(b_w_in[0]), bf(b_w_out[0]), lora_pack))
    w1_full[1], w2_full[1] = unshard(g_w1_1, 1), unshard(g_w2_1, 0)
    full['b_w_in'], full['b_w_out'] = unshard(g_b_in, 1), unshard(g_b_out, 0)
    lora_flat, lo = g_lora.reshape(N_DEV, -1), 0
    for n, ax in zip(lora_names, (0, 0, 0, 1, 1, 1)):
        loc = w[n].shape[1:]
        full[n] = unshard(lora_flat[:, lo:lo + w[n].size].reshape((N_DEV,) + loc), ax)
        lo += w[n].size
    rwkv_w = (full['b_w_in'], pad_c(full['b_w1'], LORA_PAD), pad_c(full['b_a1'], LORA_PAD),
              pad_c(full['b_g1'], GATE_PAD), pad_r(full['b_w2'], LORA_PAD), pad_r(full['b_a2'], LORA_PAD),
              pad_r(full['b_g2'], GATE_PAD))
    r, ld, k2, v, a_s, b_s, gate = _rwkv_pre_fwd(x2, mod_mix[1], vec, e_ind, et_ind, rwkv_w)
    y, s0, tinv = _wkv_fwd(r, ld, k2, v, a_s, b_s)
    x3, mix_b = _rwkv_post_fwd(x2, y, r, k2, v, gate, mod_mix[1], vec, e_ind, et_ind, full['b_w_out'])
    (x4, ff1, q1), _ = _mlp_fwd(x3, mod_mlp[1], w1_full[1], w2_full[1])
    dx4, loss_blk, dfinal = _final(x4, loss_target, final_g.reshape(1, d))
    loss = lax.psum(loss_blk[0, 0], ("x", "y", "c"))

    tok = lambda a: a.reshape(m_tok, a.shape[-1])
    shard_rows = lambda g: g.reshape((N_DEV, g.shape[0] // N_DEV) + g.shape[1:])
    (dx3, dmod_mlp1, h_b, dff_b, dp_b), _ = _mlp_bwd(x3, dx4, ff1, q1, mod_mlp[1], w1_full[1], w2_full[1])
    gw1_1 = _matmul_tn(tok(h_b), tok(dp_b), "grad_mlp_w1_l1", col_shards=N_DEV)
    gw2_1 = shard_rows(_matmul_tn(tok(q1), tok(dff_b), "grad_mlp_w2_l1"))
    (dy, dr_p, dk2_p, dv_p, dgate_act, o_b, dmix_b, dgate_b, small_post), (rw1_1, rw2_1) = _rwkv_post_bwd(
        dx3, mix_b, y, r, k2, v, gate, mod_mix[1], vec, e_ind, et_ind, full['b_w_out'], scatter=(gw1_1, gw2_1))
    g_b_w_out = shard_rows(_matmul_tn(tok(o_b), tok(dmix_b), "grad_b_w_out"))
    dr_s, dld, dk2_s, dv_s, das, dbs = _wkv_bwd(r, ld, k2, v, a_s, b_s, s0, tinv, dy)
    (dh, dhp, xr_b, xw_b, xk_b, xv_b, xa_b, xg_b, dr_b, dk_b, dv_b, dtw_b, dta_b, dtg_b, tw2_b, ta_b, sg_b,
     dlw_b, dla_b, dg_b, small_pre) = _rwkv_pre_bwd(x2, mod_mix[1], vec, e_ind, et_ind, rwkv_w,
                                                    dr_p, dk2_p, dv_p, dgate_act, dr_s, dld, dk2_s, dv_s, das, dbs)
    g_b_w_in = jnp.concatenate([_matmul_tn(tok(xr_b), tok(dr_b), "grad_b_w_r"),
                                _matmul_tn(tok(xk_b), tok(dk_b), "grad_b_w_k"),
                                _matmul_tn(tok(xv_b), tok(dv_b), "grad_b_w_v")], axis=1)
    shard_cols = lambda g: jnp.moveaxis(g.reshape(g.shape[0], N_DEV, g.shape[1] // N_DEV), 1, 0)
    g_b_w_in = shard_cols(g_b_w_in)
    lw_, lg_ = b_w1.shape[2], b_g1.shape[2]
    small_names = ['b_w1', 'b_a1', 'b_g1', 'b_w2', 'b_a2', 'b_g2'] + VECTORS
    small_parts = [
        shard_rows(_matmul_tn(tok(xw_b), tok(dtw_b), "grad_b_w1")[:, :lw_]),
        shard_rows(_matmul_tn(tok(xa_b), tok(dta_b), "grad_b_a1")[:, :lw_]),
        shard_rows(_matmul_tn(tok(xg_b), tok(dtg_b), "grad_b_g1")[:, :lg_]),
        shard_cols(_matmul_tn(tok(tw2_b), tok(dlw_b), "grad_b_w2")[:lw_]),
        shard_cols(_matmul_tn(tok(ta_b), tok(dla_b), "grad_b_a2")[:lw_]),
        shard_cols(_matmul_tn(tok(sg_b), tok(dg_b), "grad_b_g2")[:lg_]),
        shard_cols(jnp.concatenate([small_pre[0:10], small_post[0:2]], axis=0).astype(BF16)),
    ]
    small_flat = jnp.concatenate([p.reshape(N_DEV, -1) for p in small_parts], axis=1)
    lane = 128
    small_rows = -(-small_flat.shape[1] // (lane * ROW_ALIGN)) * ROW_ALIGN
    small_pack = jnp.pad(small_flat, ((0, 0), (0, small_rows * lane - small_flat.shape[1]))).reshape(
        N_DEV, small_rows, lane)
    dx2, dmod_mix1 = _norm_bwd(x2, dx3, dh, dhp, mod_mix[1], dgate_b)
    (dx1, dmod_mlp0, h_b, dff_b, dp_b), (r_b_w_in, r_b_w_out, r_small) = _mlp_bwd(
        x1, dx2, ff0, q0, mod_mlp[0], w1_full[0], w2_full[0], scatter=(g_b_w_in, g_b_w_out, small_pack))
    gw1_0 = _matmul_tn(tok(h_b), tok(dp_b), "grad_mlp_w1_l0", col_shards=N_DEV)
    gw2_0 = shard_rows(_matmul_tn(tok(q0), tok(dff_b), "grad_mlp_w2_l0"))
    (dx0, dmod_mix0, h_b, dpre_b, z_b, dmix_b, small_sgu, d_ws, d_bs), (rw1_0, rw2_0) = _sgu_bwd(
        x0, dx1, mix_a, mod_mix[0], *sgu_args, group_ind, scatter=(gw1_0, gw2_0))
    dmod_mine = jnp.stack([jnp.concatenate([dmod_mix0[:, 0:3], dmod_mlp0[:, 0:3]], axis=1),
                           jnp.concatenate([dmod_mix1[:, 0:3], dmod_mlp1[:, 0:3]], axis=1)], axis=1)
    rep_g = {'a_ln_g': small_sgu[0:1], 'a_ln_b': small_sgu[1:2], 'a_w_s': d_ws.reshape(-1, d), 'a_b_s': d_bs.reshape(1, d),
             'b_r_k': small_post[2:3], 'final_g': dfinal[0:1]}
    rep_rows = [rep_g[n].shape[0] for n in REPLICATED]
    rep_pack = _pad_rows(jnp.concatenate([rep_g[n] for n in REPLICATED], axis=0), 8)
    g_a_w_in, (dmod_all, rep_all) = _matmul_tn(tok(h_b), tok(dpre_b), "grad_a_w_in", col_shards=N_DEV,
                                               gather=(dmod_mine.reshape(bl, nl * 6 * d), rep_pack))
    g_a_w_out = shard_rows(_matmul_tn(tok(z_b), tok(dmix_b), "grad_a_w_out"))
    r_a_w_in, r_a_w_out = _scatter_call((g_a_w_in, g_a_w_out), "scatter_sgu_grads")

    dmod_all = jnp.moveaxis(dmod_all.reshape(nb, nl, 6 * d), 0, 1)
    dmod_cols = lax.dynamic_slice(dmod_all, (0, 0, me * cols), (nl, nb, cols))
    g_ada_w, g_ada_b = _ada_bwd(c_all, dmod_cols, dmod_all)

    mom = {n: given['m_' + n] for n in WEIGHTS}
    var = {n: given['v_' + n] for n in WEIGHTS}
    out = {}
    as3d = lambda a: a.reshape((-1,) + a.shape[-2:])
    for n, parts in (('mlp_w1', [rw1_0, rw1_1]), ('mlp_w2', [rw2_0, rw2_1]), ('a_w_in', [r_a_w_in]),
                     ('a_w_out', [r_a_w_out]), ('b_w_in', [r_b_w_in]), ('b_w_out', [r_b_w_out]),
                     ('ada_w', list(g_ada_w))):
        res = _adamw_layers(as3d(w[n]), as3d(mom[n]), as3d(var[n]), parts, "adamw_" + n)
        out[n] = tuple(a.reshape(w[n].shape) for a in res)

    items, names = [], []

    def add(n, part):
        s2 = _as2d(w[n]).shape
        items.append((_as2d(w[n]), _as2d(mom[n]), _as2d(var[n]), part.reshape((part.shape[0],) + s2)))
        names.append(n)

    sflat = r_small.reshape(N_DEV, -1)
    so = 0
    for n in small_names:
        sz = w[n].size
        add(n, sflat[:, so:so + sz])
        so += sz
    ro = 0
    for n, nr in zip(REPLICATED, rep_rows):
        add(n, rep_all[:, ro:ro + nr])
        ro += nr
    add('ada_b', g_ada_b[None])
    for n, res in zip(names, _adamw_small(items, "adamw_small")):
        out[n] = tuple(a.reshape(w[n].shape) for a in res)

    return (loss, dx0, *[out[n][0] for n in WEIGHTS], *[out[n][1] for n in WEIGHTS],
            *[out[n][2] for n in WEIGHTS], *[out[n][3] for n in WEIGHTS])
```

```python
import functools

import jax
import jax.numpy as jnp
from jax import lax
from jax.experimental import pallas as pl
from jax.experimental.pallas import tpu as pltpu

F32 = jnp.float32
BF16 = jnp.bfloat16

N_DEV = 8
RMS_EPS = 1e-6
LN_EPS = 1e-5
HEAD = 64
GN_EPS = HEAD * 1e-5
L2_EPS = 1e-12
DECAY_SCALE = 0.6065306597126334
SGU_CHUNK = 128
SGU_GROUPS = 8
WKV_CHUNK = 64
WKV_HEADS_PER_STEP = 16
WKV_EXAMPLES_PER_STEP = 2
LORA_PAD = 128
GATE_PAD = 256
ADAM_LR, ADAM_B1, ADAM_B2, ADAM_EPS, ADAM_WD, ADAM_STEP = 0.001, 0.9, 0.999, 1e-08, 0.01, 10
VMEM_LIMIT = 56 * 1024 * 1024


def _cparams(sem=None, **kw):
    if sem is not None:
        kw["dimension_semantics"] = sem
    return pltpu.CompilerParams(vmem_limit_bytes=VMEM_LIMIT, **kw)


def _dot(a, b):
    return jnp.dot(a.astype(BF16), b.astype(BF16), preferred_element_type=F32)


def _dot_nt(a, b):
    return lax.dot_general(a.astype(BF16), b.astype(BF16), (((1,), (1,)), ((), ())), preferred_element_type=F32)


def _dot_tn(a, b):
    return lax.dot_general(a.astype(BF16), b.astype(BF16), (((0,), (0,)), ((), ())), preferred_element_type=F32)


def _bdot(a, b, dims):
    return lax.dot_general(a.astype(BF16), b.astype(BF16), (dims, ((0,), (0,))), preferred_element_type=F32)


@jax.custom_vjp
def _tri_sum(tri, tri_t, x):
    hi = x.astype(BF16)
    lo = (x - hi.astype(F32)).astype(BF16)
    dn = (((2,), (1,)), ((0,), (0,)))
    return (lax.dot_general(tri, hi, dn, preferred_element_type=F32)
            + lax.dot_general(tri, lo, dn, preferred_element_type=F32))


_tri_sum.defvjp(lambda tri, tri_t, x: (_tri_sum(tri, tri_t, x), (tri, tri_t)),
                lambda res, g: (jnp.zeros_like(res[0]), jnp.zeros_like(res[1]), _tri_sum(res[1], res[0], g)))


@jax.custom_vjp
def _bmm_nn(a, b):
    return _bdot(a, b, ((2,), (1,)))


@jax.custom_vjp
def _bmm_nt(a, b):
    return _bdot(a, b, ((2,), (2,)))


@jax.custom_vjp
def _bmm_tn(a, b):
    return _bdot(a, b, ((1,), (1,)))


_bmm_nn.defvjp(lambda a, b: (_bmm_nn(a, b), (a, b)), lambda res, g: (_bmm_nt(g, res[1]), _bmm_tn(res[0], g)))
_bmm_nt.defvjp(lambda a, b: (_bmm_nt(a, b), (a, b)), lambda res, g: (_bmm_nn(g, res[1]), _bmm_tn(g, res[0])))
_bmm_tn.defvjp(lambda a, b: (_bmm_tn(a, b), (a, b)), lambda res, g: (_bmm_nt(res[1], g), _bmm_nn(res[0], g)))


def _tri_inverse(p):
    n = p.shape[1]
    row = lax.broadcasted_iota(jnp.int32, (n, n), 0)
    col = lax.broadcasted_iota(jnp.int32, (n, n), 1)
    tinv = jnp.where(row == col, 1.0, 0.0).astype(F32)[None] + p
    for _ in range(max(1, (n - 1).bit_length()) - 1):
        p = _bmm_nn(p, p)
        tinv = tinv + _bmm_nn(tinv, p)
    return tinv.astype(BF16)


def _tri_solve_fwd(tinv, p, rhs):
    u = _bmm_nn(tinv, rhs)
    return u, (tinv, u)


def _tri_solve_bwd(res, du):
    tinv, u = res
    drhs = _bmm_tn(tinv, du)
    return jnp.zeros_like(tinv), _bmm_nt(drhs, u), drhs


@jax.custom_vjp
def _tri_solve(tinv, p, rhs):
    return _tri_solve_fwd(tinv, p, rhs)[0]


_tri_solve.defvjp(_tri_solve_fwd, _tri_solve_bwd)


def _wkv_chunk(s0, r, ld, k, v, a, b, tinv=None):
    nh, n, _ = r.shape
    row = lax.broadcasted_iota(jnp.int32, (n, n), 0)
    col = lax.broadcasted_iota(jnp.int32, (n, n), 1)
    incl = row >= col
    strict = row > col
    lower = jnp.broadcast_to(jnp.where(incl, 1.0, 0.0).astype(BF16), (nh, n, n))
    upper = jnp.broadcast_to(jnp.where(row <= col, 1.0, 0.0).astype(BF16), (nh, n, n))
    c = _tri_sum(lower, upper, ld)
    c_end = c[:, n - 1:n, :]
    ec, enc, ecx, eend = jnp.exp(c), jnp.exp(-c), jnp.exp(c - ld), jnp.exp(c_end - c)
    ar = jnp.concatenate([a * ecx, r * ec], axis=1)
    mask = jnp.concatenate([strict, incl], axis=0)[None]
    m_b = jnp.where(mask, _bmm_nt(ar, b * enc), 0.0)
    m_k = jnp.where(mask, _bmm_nt(ar, k * enc), 0.0)
    a_ab, a_rb = m_b[:, :n], m_b[:, n:]
    base = _bmm_nt(ar, s0) + _bmm_nn(m_k, v)
    if tinv is None:
        tinv = lax.stop_gradient(_tri_inverse(a_ab))
    u = _tri_solve(tinv, a_ab, base[:, :n])
    y = base[:, n:] + _bmm_nn(a_rb, u)
    s1 = s0 * jnp.exp(c_end) + _bmm_tn(jnp.concatenate([u, v], axis=1), jnp.concatenate([b * eend, k * eend], axis=1))
    return y, s1, tinv


def _wkv_specs(bl, nh, t):
    eb, hb, lc = min(bl, WKV_EXAMPLES_PER_STEP), min(nh, WKV_HEADS_PER_STEP), WKV_CHUNK
    return eb, hb, lc, (bl // eb, nh // hb, t // lc)


def _wkv_fwd(r, ld, k, v, a, b):
    bl, nh, t, n = r.shape
    eb, hb, lc, grid = _wkv_specs(bl, nh, t)
    nc = t // lc
    nb = eb * hb

    def body(r_ref, ld_ref, k_ref, v_ref, a_ref, b_ref, y_ref, s0_ref, tinv_ref, s_scr):
        @pl.when(pl.program_id(2) == 0)
        def _():
            s_scr[...] = jnp.zeros_like(s_scr)

        s0 = s_scr[...]
        s0_ref[:, :, 0] = s0.reshape(eb, hb, n, n)
        y, s1, tinv = _wkv_chunk(
            s0, *(ref[...].reshape(nb, lc, n) for ref in (r_ref, ld_ref, k_ref, v_ref, a_ref, b_ref)))
        y_ref[...] = y.reshape(eb, hb, lc, n)
        tinv_ref[:, :, 0] = tinv.reshape(eb, hb, lc, lc)
        s_scr[...] = s1

    seq = pl.BlockSpec((eb, hb, lc, n), lambda e, h, c: (e, h, c, 0))
    return pl.pallas_call(
        body, name="wkv_fwd", grid=grid,
        in_specs=[seq] * 6,
        out_specs=(seq, pl.BlockSpec((eb, hb, 1, n, n), lambda e, h, c: (e, h, c, 0, 0)),
                   pl.BlockSpec((eb, hb, 1, lc, lc), lambda e, h, c: (e, h, c, 0, 0))),
        out_shape=(jax.ShapeDtypeStruct((bl, nh, t, n), F32), jax.ShapeDtypeStruct((bl, nh, nc, n, n), F32),
                   jax.ShapeDtypeStruct((bl, nh, nc, lc, lc), BF16)),
        scratch_shapes=[pltpu.VMEM((nb, n, n), F32)],
        compiler_params=_cparams(("arbitrary", "arbitrary", "arbitrary")),
    )(r, ld, k, v, a, b)


def _wkv_bwd(r, ld, k, v, a, b, s0_all, tinv_all, dy):
    bl, nh, t, n = r.shape
    eb, hb, lc, grid = _wkv_specs(bl, nh, t)
    nc = t // lc
    nb = eb * hb

    def body(r_ref, ld_ref, k_ref, v_ref, a_ref, b_ref, s0_ref, tinv_ref, dy_ref,
             dr_ref, dld_ref, dk_ref, dv_ref, da_ref, db_ref, ds_scr):
        @pl.when(pl.program_id(2) == 0)
        def _():
            ds_scr[...] = jnp.zeros_like(ds_scr)

        args = (s0_ref[:, :, 0].reshape(nb, n, n),) + tuple(
            ref[...].reshape(nb, lc, n) for ref in (r_ref, ld_ref, k_ref, v_ref, a_ref, b_ref))
        tinv = tinv_ref[:, :, 0].reshape(nb, lc, lc)
        _, vjp = jax.vjp(lambda *xs: _wkv_chunk(*xs, tinv=tinv)[:2], *args)
        ds0, *dseq = vjp((dy_ref[...].reshape(nb, lc, n), ds_scr[...]))
        ds_scr[...] = ds0
        for ref, val in zip((dr_ref, dld_ref, dk_ref, dv_ref, da_ref, db_ref), dseq):
            ref[...] = val.reshape(eb, hb, lc, n)

    seq = pl.BlockSpec((eb, hb, lc, n), lambda e, h, c: (e, h, nc - 1 - c, 0))
    st = pl.BlockSpec((eb, hb, 1, n, n), lambda e, h, c: (e, h, nc - 1 - c, 0, 0))
    ti = pl.BlockSpec((eb, hb, 1, lc, lc), lambda e, h, c: (e, h, nc - 1 - c, 0, 0))
    out = jax.ShapeDtypeStruct((bl, nh, t, n), F32)
    return pl.pallas_call(
        body, name="wkv_bwd", grid=grid,
        in_specs=[seq] * 6 + [st, ti, seq],
        out_specs=(seq,) * 6, out_shape=(out,) * 6,
        scratch_shapes=[pltpu.VMEM((nb, n, n), F32)],
        compiler_params=_cparams(("arbitrary", "arbitrary", "arbitrary")),
    )(r, ld, k, v, a, b, s0_all, tinv_all, dy)


def _scatter_copies(x_refs, o_refs, send_sems, recv_sems, local_sems):
    pos = (lax.axis_index("x"), lax.axis_index("y"), lax.axis_index("c"))
    me = 4 * pos[0] + 2 * pos[1] + pos[2]

    def descriptors():
        sends, arrivals, local = [], [], []
        for i, (x_ref, o_ref) in enumerate(zip(x_refs, o_refs)):
            for m in range(1, N_DEV):
                p = tuple(1 - pos[a] if (m >> (2 - a)) & 1 else pos[a] for a in range(3))
                pidx = 4 * p[0] + 2 * p[1] + p[2]
                k = (N_DEV - 1) * i + m - 1
                for dst, out in ((o_ref.at[me], sends), (o_ref.at[pidx], arrivals)):
                    out.append(pltpu.make_async_remote_copy(
                        src_ref=x_ref.at[pidx], dst_ref=dst, send_sem=send_sems.at[k], recv_sem=recv_sems.at[k],
                        device_id=p, device_id_type=pl.DeviceIdType.MESH))
            local.append(pltpu.make_async_copy(x_ref.at[me], o_ref.at[me], local_sems.at[i]))
        return sends, arrivals, local

    def start():
        sends, _, local = descriptors()
        for cp in local + sends:
            cp.start()

    def finish():
        sends, arrivals, local = descriptors()
        for cp in arrivals:
            cp.wait_recv()
        for cp in sends:
            cp.wait_send()
        for cp in local:
            cp.wait()

    return start, finish


def _gather_copies(x_refs, o_refs, send_sems, recv_sems, local_sems):
    pos = (lax.axis_index("x"), lax.axis_index("y"), lax.axis_index("c"))
    me = 4 * pos[0] + 2 * pos[1] + pos[2]
    far = (2, 4, 6)

    def peer_of(m):
        p = tuple(1 - pos[a] if (m >> (2 - a)) & 1 else pos[a] for a in range(3))
        return p, 4 * p[0] + 2 * p[1] + p[2]

    sibling, _ = peer_of(1)

    def copy(i, k, src, slot, to):
        return pltpu.make_async_remote_copy(
            src_ref=src, dst_ref=o_refs[i].at[slot], send_sem=send_sems.at[(N_DEV - 1) * i + k],
            recv_sem=recv_sems.at[(N_DEV - 1) * i + k], device_id=to, device_id_type=pl.DeviceIdType.MESH)

    def direct(i):
        return [copy(i, m - 1, x_refs[i], me, peer_of(m)[0]) for m in (1,) + far]

    def local(i):
        return pltpu.make_async_copy(x_refs[i], o_refs[i].at[me], local_sems.at[i])

    def start():
        for i in range(len(x_refs)):
            local(i).start()
            for cp in direct(i):
                cp.start()

    def relays():
        return [copy(i, m, o_refs[i].at[peer_of(m)[1]], peer_of(m)[1], sibling)
                for i in range(len(x_refs)) for m in far]

    def relay():
        for i in range(len(x_refs)):
            for m in far:
                copy(i, m - 1, x_refs[i], peer_of(m)[1], sibling).wait_recv()
        for fwd in relays():
            fwd.start()

    def finish():
        n = len(x_refs)
        for i in range(n):
            copy(i, 0, x_refs[i], peer_of(1)[1], sibling).wait_recv()
            for m in far:
                copy(i, m, x_refs[i], peer_of(m ^ 1)[1], sibling).wait_recv()
        for i in range(n):
            for cp in direct(i):
                cp.wait_send()
            local(i).wait()
        for fwd in relays():
            fwd.wait_send()

    return start, relay, finish


def _scatter_scratch(n):
    return [pltpu.SemaphoreType.DMA(((N_DEV - 1) * n,)), pltpu.SemaphoreType.DMA(((N_DEV - 1) * n,)),
            pltpu.SemaphoreType.DMA((n,))]


_ANY = pl.BlockSpec(memory_space=pl.ANY)


def _scatter_call(arrays, name):
    n = len(arrays)

    def body(*refs):
        start, finish = _scatter_copies(refs[:n], refs[n:2 * n], *refs[2 * n:])
        start()
        finish()

    return pl.pallas_call(
        body, name=name, in_specs=[_ANY] * n, out_specs=(_ANY,) * n,
        out_shape=tuple(_sds(a.shape, a.dtype) for a in arrays), scratch_shapes=_scatter_scratch(n),
    )(*arrays)


def _gather_call(arrays, name):
    n = len(arrays)

    def body(*refs):
        start, relay, finish = _gather_copies(refs[:n], refs[n:2 * n], *refs[2 * n:])
        start()
        relay()
        finish()

    return pl.pallas_call(
        body, name=name, in_specs=[_ANY] * n, out_specs=(_ANY,) * n,
        out_shape=tuple(_sds((N_DEV,) + a.shape, a.dtype) for a in arrays), scratch_shapes=_scatter_scratch(n),
    )(*arrays)


def _call_with_scatter(body, *, name, grid, in_specs, out_specs, out_shape, scratch_shapes, operands,
                       scatter=(), gather=()):
    assert not (scatter and gather)
    carried = tuple(scatter) or tuple(gather)
    copies = _scatter_copies if scatter else _gather_copies
    recv_shapes = tuple(_sds(a.shape if scatter else (N_DEV,) + a.shape, a.dtype) for a in carried)
    nc, n_in, n_out, n_scr = len(carried), len(in_specs), len(out_specs), len(scratch_shapes)
    total_steps = functools.reduce(lambda a, b: a * b, grid, 1)
    if nc == 0:
        return pl.pallas_call(
            body, name=name, grid=grid, in_specs=list(in_specs), out_specs=tuple(out_specs),
            out_shape=tuple(out_shape), scratch_shapes=list(scratch_shapes),
            compiler_params=_cparams(("arbitrary",) * len(grid)))(*operands), ()

    def wrapped(*refs):
        ins, refs = refs[:n_in], refs[n_in:]
        c_in, refs = refs[:nc], refs[nc:]
        outs, refs = refs[:n_out], refs[n_out:]
        c_out, refs = refs[:nc], refs[nc:]
        scr, sems = refs[:n_scr], refs[n_scr:]
        ids = [pl.program_id(a) for a in range(len(grid))]
        first = functools.reduce(jnp.logical_and, [i == 0 for i in ids])
        last = functools.reduce(jnp.logical_and, [i == g - 1 for i, g in zip(ids, grid)])
        start, *relay, finish = copies(c_in, c_out, *sems)
        pl.when(first)(start)
        if relay:
            step = functools.reduce(lambda acc, ig: acc * ig[1] + ig[0], zip(ids, grid), 0)
            pl.when(step == (3 * total_steps) // 4)(relay[0])
        body(*ins, *outs, *scr)
        pl.when(last)(finish)

    res = pl.pallas_call(
        wrapped, name=name, grid=grid,
        in_specs=list(in_specs) + [_ANY] * nc, out_specs=tuple(out_specs) + (_ANY,) * nc,
        out_shape=tuple(out_shape) + recv_shapes,
        scratch_shapes=list(scratch_shapes) + _scatter_scratch(nc),
        compiler_params=_cparams(("arbitrary",) * len(grid)),
    )(*operands, *carried)
    return res[:n_out], res[n_out:]


def _rms(x):
    inv = lax.rsqrt(jnp.mean(x * x, axis=-1, keepdims=True) + RMS_EPS)
    return x * inv, inv


def _rms_bwd(xn, inv, dxn):
    return inv * (dxn - xn * jnp.mean(dxn * xn, axis=-1, keepdims=True))


def _colsum(x):
    return jnp.sum(x, axis=0, keepdims=True)


def _sigmoid(x):
    return 0.5 * (jnp.tanh(0.5 * x) + 1.0)


def _split_bf16(x):
    hi = x.astype(BF16)
    return hi, (x - hi.astype(F32)).astype(BF16)


def _dot_split(x, e):
    hi, lo = _split_bf16(x)
    return jnp.dot(hi, e, preferred_element_type=F32) + jnp.dot(lo, e, preferred_element_type=F32)


@jax.custom_vjp
def _headsum(x, e, et):
    return _dot_split(_dot_split(x, e), et)


_headsum.defvjp(lambda x, e, et: (_headsum(x, e, et), (e, et)),
                lambda res, g: (_headsum(g, *res), jnp.zeros_like(res[0]), jnp.zeros_like(res[1])))


def _make_headsum(e, et):
    return lambda x: _headsum(x, e, et)


def _head_indicators(d):
    e = (jnp.arange(d)[:, None] // HEAD == jnp.arange(128)[None, :]).astype(BF16)
    return e, e.T


def _rwkv_elem(r, k, lw, la, w0, a0, k_k, k_a, headsum):
    ld = -DECAY_SCALE * _sigmoid(w0 + lw)
    a = _sigmoid(a0 + la)
    kkp = k * k_k
    kk = kkp * lax.rsqrt(jnp.maximum(headsum(kkp * kkp), L2_EPS * L2_EPS))
    k2 = k * (1.0 + (a - 1.0) * k_a)
    del r
    return ld, k2, -kk, kk * a


def _rwkv_post(y, r, k2, v, g, ln_g, ln_b, r_k, headsum):
    m = headsum(y) * (1.0 / HEAD)
    yc = y - m
    var = headsum(yc * yc) * (1.0 / HEAD)
    yn = yc * lax.rsqrt(var + GN_EPS)
    bonus = headsum(r * k2 * r_k) * v
    return (yn * ln_g + ln_b + bonus) * g


def _shift_down(h, first_row):
    rolled = pltpu.roll(h, 1, 0)
    row = lax.broadcasted_iota(jnp.int32, h.shape, 0)
    return jnp.where(row == 0, first_row, rolled)


def _shift_up(h, last_row):
    n = h.shape[0]
    rolled = pltpu.roll(h, n - 1, 0)
    row = lax.broadcasted_iota(jnp.int32, h.shape, 0)
    return jnp.where(row == n - 1, last_row, rolled)


def _gelu(p):
    return 0.5 * p * (1.0 + lax.erf(p * 0.7071067811865476))


def _gelu_grad(p):
    return 0.5 * (1.0 + lax.erf(p * 0.7071067811865476)) + p * jnp.exp(-0.5 * p * p) * 0.3989422804014327


def _tok(tm, d):
    return pl.BlockSpec((1, tm, d), lambda e, t, *_: (e, t, 0))


def _per_example(rows, d):
    return pl.BlockSpec((1, rows, d), lambda e, t, *_: (e, 0, 0))


def _whole(shape):
    nd = len(shape)
    return pl.BlockSpec(tuple(shape), lambda *_: (0,) * nd)


def _heads(nh, tm):
    return pl.BlockSpec((1, nh, tm, HEAD), lambda e, t, *_: (e, 0, t, 0))


def _sds(shape, dtype=F32):
    return jax.ShapeDtypeStruct(tuple(shape), dtype)


def _add_rows(ref, first, rows):
    @pl.when(first)
    def _():
        ref[0] = jnp.zeros(ref.shape[1:], ref.dtype)

    for i, r in enumerate(rows):
        ref[0, i:i + 1] += r


def _first(e, t):
    return jnp.logical_and(e == 0, t == 0)


def _ada_fwd(c_all, ada_w, ada_b_cols):
    nl, d, cols = ada_w.shape
    nb = c_all.shape[0]

    def body(c_ref, w_ref, b_ref, o_ref):
        c = c_ref[...]
        cond = c * _sigmoid(c)
        for i in range(nl):
            o_ref[i] = _dot(cond, w_ref[i]) + b_ref[i]

    return pl.pallas_call(
        body, name="ada_fwd", out_shape=_sds((nl, nb, cols)),
        compiler_params=_cparams(),
    )(c_all, ada_w, ada_b_cols)


def _ada_bwd(c_all, dmod_cols, dmod_full):
    nl, nb, cols = dmod_cols.shape
    d = c_all.shape[1]

    def body(c_ref, g_ref, f_ref, b_ref, *o_refs):
        c = c_ref[...]
        cond = c * _sigmoid(c)
        for i in range(nl):
            o_refs[i][0] = _dot_tn(cond, g_ref[i])
            b_ref[i:i + 1] = jnp.sum(f_ref[i], axis=0, keepdims=True)

    res = pl.pallas_call(
        body, name="ada_bwd", out_shape=(_sds((nl, dmod_full.shape[2])),) + (_sds((1, d, cols)),) * nl,
        compiler_params=_cparams(),
    )(c_all, dmod_cols, dmod_full)
    return res[1:], res[0]


def _matmul_tn(a, b, name, col_shards=None, gather=()):
    m, ka = a.shape
    n = b.shape[1]
    tm = min(m, 2048)
    tk = min(ka, 1024)
    tn = min(n, 1024)
    steps = m // tm
    if col_shards:
        cs = n // col_shards
        spt = tn // cs
        out_spec = pl.BlockSpec((spt, tk, cs), lambda i, j, s: (j, i, 0))
        out_shape = _sds((col_shards, ka, cs), BF16)
    else:
        out_spec = pl.BlockSpec((tk, tn), lambda i, j, s: (i, j))
        out_shape = _sds((ka, n), BF16)

    def body(a_ref, b_ref, o_ref, acc):
        s = pl.program_id(2)

        @pl.when(s == 0)
        def _():
            acc[...] = jnp.zeros_like(acc)

        acc[...] += _dot_tn(a_ref[...], b_ref[...])

        @pl.when(s == steps - 1)
        def _():
            if col_shards:
                for q in range(spt):
                    o_ref[q] = acc[:, q * cs:(q + 1) * cs].astype(BF16)
            else:
                o_ref[...] = acc[...].astype(BF16)

    res, got = _call_with_scatter(
        body, name=name, grid=(ka // tk, n // tn, steps),
        in_specs=[pl.BlockSpec((tm, tk), lambda i, j, s: (s, i)), pl.BlockSpec((tm, tn), lambda i, j, s: (s, j))],
        out_specs=(out_spec,), out_shape=(out_shape,),
        scratch_shapes=[pltpu.VMEM((tk, tn), F32)], operands=(a, b), gather=gather)
    return (res[0], got) if gather else res[0]


MLP_FWD_TM = 1024
MLP_FJ = 1024
MLP_BWD_TM = 512
MLP_BWD_FJ = 1024


def _mlp_fwd(x, mod, w1, w2, gather=()):
    bl, t, d = x.shape
    f = w1.shape[1]
    tm, fj = min(t, MLP_FWD_TM), min(f, MLP_FJ)
    nj = f // fj

    def body(x_ref, mod_ref, w1_ref, w2_ref, xo_ref, ff_ref, q_ref, h_scr, acc):
        j = pl.program_id(2)

        @pl.when(j == 0)
        def _():
            xn, _ = _rms(x_ref[0])
            h_scr[...] = (xn * (1.0 + mod_ref[0, 1:2]) + mod_ref[0, 0:1]).astype(BF16)
            acc[...] = jnp.zeros_like(acc)

        p = jnp.dot(h_scr[...], w1_ref[...], preferred_element_type=F32)
        q = jnp.square(jnp.maximum(p, 0.0)).astype(BF16)
        q_ref[0] = q
        acc[...] += jnp.dot(q, w2_ref[...], preferred_element_type=F32)

        @pl.when(j == nj - 1)
        def _():
            ff_ref[0] = acc[...]
            xo_ref[0] = x_ref[0] + mod_ref[0, 2:3] * acc[...]

    return _call_with_scatter(
        body, name="mlp_fwd", grid=(bl, t // tm, nj),
        in_specs=[_tok(tm, d), _per_example(8, d),
                  pl.BlockSpec((d, fj), lambda e, i, j: (0, j)), pl.BlockSpec((fj, d), lambda e, i, j: (j, 0))],
        out_specs=(_tok(tm, d), _tok(tm, d), pl.BlockSpec((1, tm, fj), lambda e, i, j: (e, i, j))),
        out_shape=(_sds(x.shape), _sds(x.shape), _sds((bl, t, f), BF16)),
        scratch_shapes=[pltpu.VMEM((tm, d), BF16), pltpu.VMEM((tm, d), F32)],
        operands=(x, mod, w1, w2), gather=gather)


def _mlp_bwd(x, dxo, ff, q, mod, w1, w2, scatter=()):
    bl, t, d = x.shape
    f = w1.shape[1]
    tm, fj = min(t, MLP_BWD_TM), min(f, MLP_BWD_FJ)
    nj = f // fj

    def body(x_ref, dxo_ref, ff_ref, q_ref, mod_ref, w1_ref, w2_ref,
             dx_ref, dmod_ref, h_ref, dff_ref, dp_ref, acc):
        ti, j = pl.program_id(1), pl.program_id(2)

        @pl.when(j == 0)
        def _():
            xn, _ = _rms(x_ref[0])
            h_ref[0] = (xn * (1.0 + mod_ref[0, 1:2]) + mod_ref[0, 0:1]).astype(BF16)
            dff_ref[0] = (mod_ref[0, 2:3] * dxo_ref[0]).astype(BF16)
            acc[...] = jnp.zeros_like(acc)

        rl = jnp.sqrt(q_ref[0].astype(F32))
        dp = (_dot_nt(dff_ref[0], w2_ref[...]) * (2.0 * rl)).astype(BF16)
        dp_ref[0] = dp
        acc[...] += _dot_nt(dp, w1_ref[...])

        @pl.when(j == nj - 1)
        def _():
            xn, inv = _rms(x_ref[0])
            dh = acc[...]
            dx_ref[0] = dxo_ref[0] + _rms_bwd(xn, inv, dh * (1.0 + mod_ref[0, 1:2]))
            _add_rows(dmod_ref, ti == 0, [_colsum(dh), _colsum(dh * xn), _colsum(dxo_ref[0] * ff_ref[0])])

    big = lambda: pl.BlockSpec((1, tm, fj), lambda e, i, j: (e, i, j))
    return _call_with_scatter(
        body, name="mlp_bwd", grid=(bl, t // tm, nj),
        in_specs=[_tok(tm, d), _tok(tm, d), _tok(tm, d), big(), _per_example(8, d),
                  pl.BlockSpec((d, fj), lambda e, i, j: (0, j)), pl.BlockSpec((fj, d), lambda e, i, j: (j, 0))],
        out_specs=(_tok(tm, d), _per_example(8, d), _tok(tm, d), _tok(tm, d), big()),
        out_shape=(_sds(x.shape), _sds((bl, 8, d)), _sds(x.shape, BF16), _sds(x.shape, BF16),
                   _sds((bl, t, f), BF16)),
        scratch_shapes=[pltpu.VMEM((tm, d), F32)],
        operands=(x, dxo, ff, q, mod, w1, w2), scatter=scatter)


SGU_TM = 512


def _sgu_core(x, mod_ref, win_ref, lng, lnb, ws_ref, bias_ref):
    tm, d = x.shape
    xn, inv = _rms(x)
    h = (xn * (1.0 + mod_ref[0, 1:2]) + mod_ref[0, 0:1]).astype(BF16)
    pre = jnp.dot(h, win_ref[...], preferred_element_type=F32)
    uv = _gelu(pre)
    u, v = uv[:, :d], uv[:, d:]
    mu = jnp.mean(v, axis=-1, keepdims=True)
    vc = v - mu
    rstd = lax.rsqrt(jnp.mean(vc * vc, axis=-1, keepdims=True) + LN_EPS)
    vhat = vc * rstd
    vln = vhat * lng + lnb
    gd = d // SGU_GROUPS
    rows = []
    for c in range(tm // SGU_CHUNK):
        cols = []
        for g in range(SGU_GROUPS):
            cols.append(_dot(ws_ref[g], vln[c * SGU_CHUNK:(c + 1) * SGU_CHUNK, g * gd:(g + 1) * gd]))
        rows.append(jnp.concatenate(cols, axis=1) + bias_ref[...])
    sv = jnp.concatenate(rows, axis=0)
    return xn, inv, h, pre, u, vhat, rstd, vln, sv


def _sgu_masked(ws_ref, wm_scr):
    row = lax.broadcasted_iota(jnp.int32, (SGU_CHUNK, SGU_CHUNK), 0)
    col = lax.broadcasted_iota(jnp.int32, (SGU_CHUNK, SGU_CHUNK), 1)
    for g in range(SGU_GROUPS):
        wm_scr[g] = jnp.where(row >= col, ws_ref[g], 0.0).astype(BF16)


def _sgu_fwd(x, mod, w_in, ln_g, ln_b, w_s, bias_full, w_out, gather=()):
    bl, t, d = x.shape
    tm = min(t, SGU_TM)

    def body(x_ref, mod_ref, win_ref, lng_ref, lnb_ref, ws_ref, bias_ref, wout_ref, xo_ref, mix_ref, wm_scr):
        _sgu_masked(ws_ref, wm_scr)
        xt = x_ref[0]
        *_, u, _, _, _, sv = _sgu_core(xt, mod_ref, win_ref, lng_ref[...], lnb_ref[...], wm_scr, bias_ref)
        mix = _dot(u * sv, wout_ref[...])
        mix_ref[0] = mix
        xo_ref[0] = xt + mod_ref[0, 2:3] * mix

    return _call_with_scatter(
        body, name="sgu_fwd", grid=(bl, t // tm),
        in_specs=[_tok(tm, d), _per_example(8, d), _whole(w_in.shape), _whole(ln_g.shape), _whole(ln_b.shape),
                  _whole(w_s.shape), _whole(bias_full.shape), _whole(w_out.shape)],
        out_specs=(_tok(tm, d), _tok(tm, d)),
        out_shape=(_sds(x.shape), _sds(x.shape)),
        scratch_shapes=[pltpu.VMEM(w_s.shape, BF16)],
        operands=(x, mod, w_in, ln_g, ln_b, w_s, bias_full, w_out), gather=gather)


def _sgu_bwd(x, dxo, mix, mod, w_in, ln_g, ln_b, w_s, bias_full, w_out, group_ind, scatter=()):
    bl, t, d = x.shape
    tm = min(t, SGU_TM)
    gd = d // SGU_GROUPS

    def body(x_ref, dxo_ref, mix_ref, mod_ref, win_ref, lng_ref, lnb_ref, ws_ref, bias_ref, wout_ref, ind_ref,
             dx_ref, dmod_ref, h_ref, dpre_ref, z_ref, dmix_ref, small_ref, dws_ref, dbs_ref, wm_scr, dbias_scr):
        e, ti = pl.program_id(0), pl.program_id(1)
        _sgu_masked(ws_ref, wm_scr)
        xt, dxo = x_ref[0], dxo_ref[0]
        lng = lng_ref[...]
        xn, inv, h, pre, u, vhat, rstd, vln, sv = _sgu_core(xt, mod_ref, win_ref, lng, lnb_ref[...], wm_scr, bias_ref)
        h_ref[0] = h
        z_ref[0] = (u * sv).astype(BF16)
        dmix = mod_ref[0, 2:3] * dxo
        dmix_ref[0] = dmix.astype(BF16)
        dz = _dot_nt(dmix, wout_ref[...])
        du, dsv = dz * sv, dz * u

        @pl.when(_first(e, ti))
        def _():
            dws_ref[...] = jnp.zeros_like(dws_ref)
            dbias_scr[...] = jnp.zeros_like(dbias_scr)
            small_ref[...] = jnp.zeros_like(small_ref)

        row = lax.broadcasted_iota(jnp.int32, (SGU_CHUNK, SGU_CHUNK), 0)
        col = lax.broadcasted_iota(jnp.int32, (SGU_CHUNK, SGU_CHUNK), 1)
        rows = []
        for c in range(tm // SGU_CHUNK):
            rs = slice(c * SGU_CHUNK, (c + 1) * SGU_CHUNK)
            dbias_scr[...] += dsv[rs]
            cols = []
            for g in range(SGU_GROUPS):
                cs = slice(g * gd, (g + 1) * gd)
                cols.append(_dot_tn(wm_scr[g], dsv[rs, cs]))
                dws_ref[g] += jnp.where(row >= col, _dot_nt(dsv[rs, cs], vln[rs, cs]), 0.0)
            rows.append(jnp.concatenate(cols, axis=1))
        dvln = jnp.concatenate(rows, axis=0)
        small_ref[0:1] += _colsum(dvln * vhat)
        small_ref[1:2] += _colsum(dvln)
        dvhat = dvln * lng
        dv = rstd * (dvhat - jnp.mean(dvhat, axis=-1, keepdims=True)
                     - vhat * jnp.mean(dvhat * vhat, axis=-1, keepdims=True))
        dpre = (jnp.concatenate([du, dv], axis=1) * _gelu_grad(pre)).astype(BF16)
        dpre_ref[0] = dpre
        dh = _dot_nt(dpre, win_ref[...])
        dx_ref[0] = dxo + _rms_bwd(xn, inv, dh * (1.0 + mod_ref[0, 1:2]))
        _add_rows(dmod_ref, ti == 0, [_colsum(dh), _colsum(dh * xn), _colsum(dxo * mix_ref[0])])

        @pl.when(jnp.logical_and(e == bl - 1, ti == t // tm - 1))
        def _():
            hi, lo = _split_bf16(dbias_scr[...])
            ind = ind_ref[...]
            dbs_ref[...] = (lax.dot_general(ind, hi, (((1,), (1,)), ((), ())), preferred_element_type=F32)
                            + lax.dot_general(ind, lo, (((1,), (1,)), ((), ())), preferred_element_type=F32))

    return _call_with_scatter(
        body, name="sgu_bwd", grid=(bl, t // tm),
        in_specs=[_tok(tm, d), _tok(tm, d), _tok(tm, d), _per_example(8, d), _whole(w_in.shape), _whole(ln_g.shape),
                  _whole(ln_b.shape), _whole(w_s.shape), _whole(bias_full.shape), _whole(w_out.shape),
                  _whole(group_ind.shape)],
        out_specs=(_tok(tm, d), _per_example(8, d), _tok(tm, d), _tok(tm, 2 * d), _tok(tm, d), _tok(tm, d),
                   _whole((8, d)), _whole(w_s.shape), _whole((SGU_GROUPS, SGU_CHUNK))),
        out_shape=(_sds(x.shape), _sds((bl, 8, d)), _sds(x.shape, BF16), _sds((bl, t, 2 * d), BF16),
                   _sds(x.shape, BF16), _sds(x.shape, BF16), _sds((8, d)), _sds(w_s.shape),
                   _sds((SGU_GROUPS, SGU_CHUNK))),
        scratch_shapes=[pltpu.VMEM(w_s.shape, BF16), pltpu.VMEM((SGU_CHUNK, d), F32)],
        operands=(x, dxo, mix, mod, w_in, ln_g, ln_b, w_s, bias_full, w_out, group_ind), scatter=scatter)


RWKV_TM = 256
N_VEC = 16


def _rwkv_pre_core(x_ref, halo_ref, mod_ref, vec_ref, ti):
    xn, inv = _rms(x_ref[0])
    scale1, shift = 1.0 + mod_ref[0, 1:2], mod_ref[0, 0:1]
    h = xn * scale1 + shift
    hn, _ = _rms(halo_ref[0])
    hh = hn * scale1 + shift
    first = jnp.where(ti == 0, 0.0, hh[7:8])
    xx = _shift_down(h, first) - h
    xs = [h + xx * vec_ref[i:i + 1] for i in range(6)]
    return xn, inv, xx, xs


def _rwkv_proj(xs, wrkv_ref, w1_ref, a1_ref, g1_ref, w2_ref, a2_ref, g2_ref):
    d = xs[0].shape[1]
    xr, xw, xk, xv, xa, xg = [z.astype(BF16) for z in xs]
    r = jnp.dot(xr, wrkv_ref[:, 0:d], preferred_element_type=F32)
    k = jnp.dot(xk, wrkv_ref[:, d:2 * d], preferred_element_type=F32)
    v = jnp.dot(xv, wrkv_ref[:, 2 * d:3 * d], preferred_element_type=F32)
    tw2 = jnp.tanh(jnp.dot(xw, w1_ref[...], preferred_element_type=F32))
    ta = jnp.dot(xa, a1_ref[...], preferred_element_type=F32)
    sg = _sigmoid(jnp.dot(xg, g1_ref[...], preferred_element_type=F32))
    lw, la, g = _dot(tw2, w2_ref[...]), _dot(ta, a2_ref[...]), _dot(sg, g2_ref[...])
    return (xr, xw, xk, xv, xa, xg), r, k, v, tw2, ta, sg, lw, la, g


def _to_heads(ref, val, nh):
    for hd in range(nh):
        ref[0, hd] = val[:, hd * HEAD:(hd + 1) * HEAD]


def _from_heads(ref, scr, nh):
    for hd in range(nh):
        scr[:, hd * HEAD:(hd + 1) * HEAD] = ref[0, hd]
    return scr[...]


def _rwkv_weight_specs(ws):
    return [_whole(w.shape) for w in ws]


def _rwkv_pre_fwd(x, mod, vec, e_ind, et_ind, weights):
    bl, t, d = x.shape
    tm = min(t, RWKV_TM)
    nh = d // HEAD
    hb = tm // 8

    def body(x_ref, halo_ref, mod_ref, vec_ref, e_ref, et_ref, wrkv, w1, a1, g1, w2, a2, g2,
             r_ref, ld_ref, k2_ref, v_ref, as_ref, bs_ref, g_ref):
        ti = pl.program_id(1)
        _, _, _, xs = _rwkv_pre_core(x_ref, halo_ref, mod_ref, vec_ref, ti)
        _, r, k, v, _, _, _, lw, la, g = _rwkv_proj(xs, wrkv, w1, a1, g1, w2, a2, g2)
        headsum = _make_headsum(e_ref[...], et_ref[...])
        ld, k2, a_s, b_s = _rwkv_elem(r, k, lw, la, vec_ref[6:7], vec_ref[7:8], vec_ref[8:9], vec_ref[9:10], headsum)
        g_ref[0] = g
        for ref, val in ((r_ref, r), (ld_ref, ld), (k2_ref, k2), (v_ref, v), (as_ref, a_s), (bs_ref, b_s)):
            _to_heads(ref, val, nh)

    halo = pl.BlockSpec((1, 8, d), lambda e, i: (e, jnp.maximum(i * hb - 1, 0), 0))
    hs = _sds((bl, nh, t, HEAD))
    return pl.pallas_call(
        body, name="rwkv_pre_fwd", grid=(bl, t // tm),
        in_specs=[_tok(tm, d), halo, _per_example(8, d), _whole(vec.shape), _whole(e_ind.shape), _whole(et_ind.shape)]
        + _rwkv_weight_specs(weights),
        out_specs=(_heads(nh, tm),) * 6 + (_tok(tm, d),),
        out_shape=(hs,) * 6 + (_sds(x.shape),),
        compiler_params=_cparams(("arbitrary", "arbitrary")),
    )(x, x, mod, vec, e_ind, et_ind, *weights)


def _rwkv_post_fwd(x, y, r, k2, v, g, mod, vec, e_ind, et_ind, w_out):
    bl, t, d = x.shape
    tm = min(t, RWKV_TM)
    nh = d // HEAD

    def body(x_ref, y_ref, r_ref, k2_ref, v_ref, g_ref, mod_ref, vec_ref, e_ref, et_ref, wout_ref,
             xo_ref, mix_ref, s0, s1, s2, s3):
        headsum = _make_headsum(e_ref[...], et_ref[...])
        yv, rv, kv, vv = (_from_heads(ref, scr, nh) for ref, scr in
                          ((y_ref, s0), (r_ref, s1), (k2_ref, s2), (v_ref, s3)))
        o = _rwkv_post(yv, rv, kv, vv, g_ref[0], vec_ref[10:11], vec_ref[11:12], vec_ref[12:13], headsum)
        mix = _dot(o, wout_ref[...])
        mix_ref[0] = mix
        xo_ref[0] = x_ref[0] + mod_ref[0, 2:3] * mix

    return pl.pallas_call(
        body, name="rwkv_post_fwd", grid=(bl, t // tm),
        in_specs=[_tok(tm, d)] + [_heads(nh, tm)] * 4 + [_tok(tm, d), _per_example(8, d), _whole(vec.shape),
                                                         _whole(e_ind.shape), _whole(et_ind.shape), _whole(w_out.shape)],
        out_specs=(_tok(tm, d), _tok(tm, d)),
        out_shape=(_sds(x.shape), _sds(x.shape)),
        scratch_shapes=[pltpu.VMEM((tm, d), F32)] * 4,
        compiler_params=_cparams(("arbitrary", "arbitrary")),
    )(x, y, r, k2, v, g, mod, vec, e_ind, et_ind, w_out)


def _rwkv_post_bwd(dxo, mix, y, r, k2, v, g, mod, vec, e_ind, et_ind, w_out, scatter=()):
    bl, t, d = dxo.shape
    tm = min(t, RWKV_TM)
    nh = d // HEAD

    def body(dxo_ref, mix_ref, y_ref, r_ref, k2_ref, v_ref, g_ref, mod_ref, vec_ref, e_ref, et_ref, wout_ref,
             dy_ref, dr_ref, dk2_ref, dv_ref, dg_ref, o_ref, dmix_ref, dgate_ref, small_ref, s0, s1, s2, s3):
        e, ti = pl.program_id(0), pl.program_id(1)
        headsum = _make_headsum(e_ref[...], et_ref[...])
        yv, rv, kv, vv = (_from_heads(ref, scr, nh) for ref, scr in
                          ((y_ref, s0), (r_ref, s1), (k2_ref, s2), (v_ref, s3)))
        dxo = dxo_ref[0]
        dmix = mod_ref[0, 2:3] * dxo
        dmix_ref[0] = dmix.astype(BF16)
        do = _dot_nt(dmix, wout_ref[...])
        post = functools.partial(_rwkv_post, headsum=headsum)
        o, vjp = jax.vjp(post, yv, rv, kv, vv, g_ref[0], vec_ref[10:11], vec_ref[11:12], vec_ref[12:13])
        o_ref[0] = o.astype(BF16)
        dy, dr, dk2, dv, dg, dlng, dlnb, drk = vjp(do)
        _to_heads(dy_ref, dy, nh)
        dr_ref[0], dk2_ref[0], dv_ref[0], dg_ref[0] = dr, dk2, dv, dg
        zero = jnp.zeros((1, d), F32)
        _add_rows(dgate_ref, ti == 0, [zero, zero, _colsum(dxo * mix_ref[0])])

        @pl.when(_first(e, ti))
        def _():
            small_ref[...] = jnp.zeros_like(small_ref)

        small_ref[0:1] += dlng
        small_ref[1:2] += dlnb
        small_ref[2:3] += drk

    return _call_with_scatter(
        body, name="rwkv_post_bwd", grid=(bl, t // tm),
        in_specs=[_tok(tm, d), _tok(tm, d)] + [_heads(nh, tm)] * 4
        + [_tok(tm, d), _per_example(8, d), _whole(vec.shape), _whole(e_ind.shape), _whole(et_ind.shape),
           _whole(w_out.shape)],
        out_specs=(_heads(nh, tm),) + (_tok(tm, d),) * 6 + (_per_example(8, d), _whole((8, d))),
        out_shape=(_sds((bl, nh, t, HEAD)),) + (_sds(dxo.shape),) * 4 + (_sds(dxo.shape, BF16),) * 2
        + (_sds((bl, 8, d)), _sds((8, d))),
        scratch_shapes=[pltpu.VMEM((tm, d), F32)] * 4,
        operands=(dxo, mix, y, r, k2, v, g, mod, vec, e_ind, et_ind, w_out), scatter=scatter)


RWKV_BWD_TM = 128


def _rwkv_pre_bwd(x, mod, vec, e_ind, et_ind, weights, dr_p, dk2_p, dv_p, dg, dr_s, dld, dk2_s, dv_s, das, dbs):
    bl, t, d = x.shape
    tm = min(t, RWKV_BWD_TM)
    nh = d // HEAD
    hb = tm // 8
    lp, gp = LORA_PAD, GATE_PAD

    def body(x_ref, halo_ref, mod_ref, vec_ref, e_ref, et_ref, wrkv, w1, a1, g1, w2, a2, g2,
             drp_ref, dk2p_ref, dvp_ref, dg_ref, drs_ref, dld_ref, dk2s_ref, dvs_ref, das_ref, dbs_ref,
             dh_ref, dhp_ref, xr_ref, xw_ref, xk_ref, xv_ref, xa_ref, xg_ref, dr_ref, dk_ref, dv_ref,
             dtw_ref, dta_ref, dtg_ref, tw2_ref, ta_ref, sg_ref, dlw_ref, dla_ref, dgb_ref, small_ref,
             s0, s1, s2, s3, s4, s5):
        e, ti = pl.program_id(0), pl.program_id(1)
        _, _, xx, xs = _rwkv_pre_core(x_ref, halo_ref, mod_ref, vec_ref, ti)
        xb, r, k, v, tw2, ta, sg, lw, la, _ = _rwkv_proj(xs, wrkv, w1, a1, g1, w2, a2, g2)
        for ref, val in zip((xr_ref, xw_ref, xk_ref, xv_ref, xa_ref, xg_ref), xb):
            ref[0] = val
        headsum = _make_headsum(e_ref[...], et_ref[...])
        drs, dld, dk2s, dvs, das, dbs_ = (_from_heads(ref, scr, nh) for ref, scr in
                                          ((drs_ref, s0), (dld_ref, s1), (dk2s_ref, s2), (dvs_ref, s3),
                                           (das_ref, s4), (dbs_ref, s5)))
        elem = functools.partial(_rwkv_elem, r, headsum=headsum)
        _, vjp = jax.vjp(elem, k, lw, la, vec_ref[6:7], vec_ref[7:8], vec_ref[8:9], vec_ref[9:10])
        dk, dlw, dla, dw0, da0, dkk, dka = vjp((dld, dk2p_ref[0] + dk2s, das, dbs_))
        dr = drp_ref[0] + drs
        dv = dvp_ref[0] + dvs
        dgv = dg_ref[0]
        dtg = _dot_nt(dgv, g2[...]) * sg * (1.0 - sg)
        dtw = _dot_nt(dlw, w2[...]) * (1.0 - tw2 * tw2)
        dta = _dot_nt(dla, a2[...])
        dr_ref[0], dk_ref[0], dv_ref[0] = dr.astype(BF16), dk.astype(BF16), dv.astype(BF16)
        dtw_ref[0], dta_ref[0], dtg_ref[0] = dtw.astype(BF16), dta.astype(BF16), dtg.astype(BF16)
        tw2_ref[0], ta_ref[0], sg_ref[0] = tw2.astype(BF16), ta.astype(BF16), sg.astype(BF16)
        dlw_ref[0], dla_ref[0], dgb_ref[0] = dlw.astype(BF16), dla.astype(BF16), dgv.astype(BF16)
        dxs = (_dot_nt(dr, wrkv[:, 0:d]), _dot_nt(dtw, w1[...]), _dot_nt(dk, wrkv[:, d:2 * d]),
               _dot_nt(dv, wrkv[:, 2 * d:3 * d]), _dot_nt(dta, a1[...]), _dot_nt(dtg, g1[...]))

        @pl.when(_first(e, ti))
        def _():
            small_ref[...] = jnp.zeros_like(small_ref)

        total = jnp.zeros((tm, d), F32)
        dhp = jnp.zeros((tm, d), F32)
        for i, dxi in enumerate(dxs):
            total += dxi
            dhp += dxi * vec_ref[i:i + 1]
            small_ref[i:i + 1] += _colsum(dxi * xx)
        dh_ref[0], dhp_ref[0] = total - dhp, dhp
        small_ref[6:7] += dw0
        small_ref[7:8] += da0
        small_ref[8:9] += dkk
        small_ref[9:10] += dka

    halo = pl.BlockSpec((1, 8, d), lambda e, i: (e, jnp.maximum(i * hb - 1, 0), 0))
    tokd, tokl, tokg = _tok(tm, d), _tok(tm, lp), _tok(tm, gp)
    bf = lambda w: _sds((bl, t, w), BF16)
    return pl.pallas_call(
        body, name="rwkv_pre_bwd", grid=(bl, t // tm),
        in_specs=[tokd, halo, _per_example(8, d), _whole(vec.shape), _whole(e_ind.shape), _whole(et_ind.shape)]
        + _rwkv_weight_specs(weights) + [tokd] * 4 + [_heads(nh, tm)] * 6,
        out_specs=(tokd, tokd) + (tokd,) * 6 + (tokd,) * 3 + (tokl, tokl, tokg, tokl, tokl, tokg)
        + (tokd, tokd, tokd, _whole((N_VEC, d))),
        out_shape=(_sds(x.shape), _sds(x.shape)) + (bf(d),) * 9 + (bf(lp), bf(lp), bf(gp), bf(lp), bf(lp), bf(gp))
        + (bf(d), bf(d), bf(d), _sds((N_VEC, d))),
        scratch_shapes=[pltpu.VMEM((tm, d), F32)] * 6,
        compiler_params=_cparams(("arbitrary", "arbitrary")),
    )(x, x, mod, vec, e_ind, et_ind, *weights, dr_p, dk2_p, dv_p, dg, dr_s, dld, dk2_s, dv_s, das, dbs)


NORM_BWD_TM = 512
FINAL_TM = 1024


def _norm_bwd(x, dxo, dh, dhprev, mod, dgate):
    bl, t, d = x.shape
    tm = min(t, NORM_BWD_TM)
    hb = tm // 8
    last_blk = t // 8 - 1

    def body(x_ref, dxo_ref, dh_ref, dhp_ref, nxt_ref, mod_ref, dgate_ref, dx_ref, dmod_ref):
        ti = pl.program_id(1)
        xn, inv = _rms(x_ref[0])
        last = jnp.where(ti == t // tm - 1, 0.0, nxt_ref[0, 0:1])
        dh = dh_ref[0] + _shift_up(dhp_ref[0], last)
        dx_ref[0] = dxo_ref[0] + _rms_bwd(xn, inv, dh * (1.0 + mod_ref[0, 1:2]))

        @pl.when(ti == 0)
        def _():
            dmod_ref[0] = dgate_ref[0]

        dmod_ref[0, 0:1] += _colsum(dh)
        dmod_ref[0, 1:2] += _colsum(dh * xn)

    nxt = pl.BlockSpec((1, 8, d), lambda e, i: (e, jnp.minimum((i + 1) * hb, last_blk), 0))
    return pl.pallas_call(
        body, name="norm_bwd", grid=(bl, t // tm),
        in_specs=[_tok(tm, d)] * 4 + [nxt, _per_example(8, d), _per_example(8, d)],
        out_specs=(_tok(tm, d), _per_example(8, d)),
        out_shape=(_sds(x.shape), _sds((bl, 8, d))),
        compiler_params=_cparams(("arbitrary", "arbitrary")),
    )(x, dxo, dh, dhprev, dhprev, mod, dgate)


def _final(x, target, final_g):
    bl, t, d = x.shape
    tm = min(t, FINAL_TM)

    def body(x_ref, tgt_ref, g_ref, dx_ref, loss_ref, dg_ref):
        e, ti = pl.program_id(0), pl.program_id(1)

        @pl.when(_first(e, ti))
        def _():
            loss_ref[...] = jnp.zeros_like(loss_ref)
            dg_ref[...] = jnp.zeros_like(dg_ref)

        xn, inv = _rms(x_ref[0])
        err = xn * g_ref[...] - tgt_ref[0]
        loss_ref[...] += (0.5 / d) * jnp.sum(err * err)
        dy = err * (1.0 / d)
        dg_ref[0:1] += _colsum(dy * xn)
        dx_ref[0] = _rms_bwd(xn, inv, dy * g_ref[...])

    return pl.pallas_call(
        body, name="final_loss", grid=(bl, t // tm),
        in_specs=[_tok(tm, d), _tok(tm, d), _whole(final_g.shape)],
        out_specs=(_tok(tm, d), _whole((8, 128)), _whole((8, d))),
        out_shape=(_sds(x.shape), _sds((8, 128)), _sds((8, d))),
        compiler_params=_cparams(("arbitrary", "arbitrary")),
    )(x, target, final_g)


def _adamw_math(w, g, m, v):
    m = ADAM_B1 * m + (1.0 - ADAM_B1) * g
    v = ADAM_B2 * v + (1.0 - ADAM_B2) * jnp.square(g)
    m_hat = m / (1.0 - ADAM_B1 ** ADAM_STEP)
    v_hat = v / (1.0 - ADAM_B2 ** ADAM_STEP)
    return -ADAM_LR * (m_hat / (jnp.sqrt(v_hat) + ADAM_EPS) + ADAM_WD * w), m, v


def _sum_parts(ref, n):
    g = ref[0].astype(F32)
    for s in range(1, n):
        g = g + ref[s].astype(F32)
    return g


def _adamw_layers(w, m, v, parts, name):
    nl, rows, c = w.shape
    tr = min(rows, 256)

    def body(w_ref, m_ref, v_ref, *refs):
        p_refs, (g_ref, d_ref, mo_ref, vo_ref) = refs[:nl], refs[nl:]
        for layer in range(nl):
            @pl.when(pl.program_id(0) == layer)
            def _(p_ref=p_refs[layer]):
                g = _sum_parts(p_ref, p_ref.shape[0])
                g_ref[...] = g
                d_ref[...], mo_ref[...], vo_ref[...] = _adamw_math(w_ref[...], g, m_ref[...], v_ref[...])

    row = pl.BlockSpec((None, tr, c), lambda l, i: (l, i, 0))
    return pl.pallas_call(
        body, name=name, grid=(nl, rows // tr),
        in_specs=[row, row, row] + [pl.BlockSpec((p.shape[0], tr, c), lambda l, i, k=k: (0, jnp.where(l == k, i, 0), 0))
                                    for k, p in enumerate(parts)],
        out_specs=(row,) * 4, out_shape=(_sds(w.shape),) * 4,
        compiler_params=_cparams(("arbitrary", "arbitrary")),
    )(w, m, v, *parts)


def _adamw_small(items, name):
    k = len(items)
    ns = [it[3].shape[0] for it in items]

    def body(*refs):
        ins, outs = refs[:4 * k], refs[4 * k:]
        for i in range(k):
            w_ref, m_ref, v_ref, p_ref = ins[4 * i:4 * i + 4]
            g = _sum_parts(p_ref, ns[i])
            outs[4 * i][...] = g
            outs[4 * i + 1][...], outs[4 * i + 2][...], outs[4 * i + 3][...] = _adamw_math(
                w_ref[...], g, m_ref[...], v_ref[...])

    flat = [a for it in items for a in it]
    res = pl.pallas_call(
        body, name=name,
        out_shape=tuple(_sds(it[0].shape) for it in items for _ in range(4)),
        compiler_params=_cparams(),
    )(*flat)
    return [tuple(res[4 * i:4 * i + 4]) for i in range(k)]


WEIGHTS = ['ada_w', 'ada_b', 'mlp_w1', 'mlp_w2', 'a_w_in', 'a_ln_g', 'a_ln_b', 'a_w_s', 'a_b_s', 'a_w_out', 'b_mu',
           'b_w_in', 'b_w0', 'b_w1', 'b_w2', 'b_a0', 'b_a1', 'b_a2', 'b_g1', 'b_g2', 'b_k_k', 'b_k_a', 'b_r_k',
           'b_ln_g', 'b_ln_b', 'b_w_out', 'final_g']
VECTORS = ['b_mu', 'b_w0', 'b_a0', 'b_k_k', 'b_k_a', 'b_ln_g', 'b_ln_b']
REPLICATED = ['a_ln_g', 'a_ln_b', 'a_w_s', 'a_b_s', 'b_r_k', 'final_g']
ROW_ALIGN = 16


def _pad_rows(a, mult):
    pad = (-a.shape[-2]) % mult
    return jnp.pad(a, [(0, 0)] * (a.ndim - 2) + [(0, pad), (0, 0)]) if pad else a


def _as2d(a):
    if a.ndim == 1:
        return a.reshape(1, -1)
    lead = 1
    for s in a.shape[:-1]:
        lead *= s
    return a.reshape(lead, a.shape[-1])


def kernel(x, c, ada_w, ada_b, mlp_w1, mlp_w2, a_w_in, a_ln_g, a_ln_b, a_w_s, a_b_s, a_w_out, b_mu, b_w_in, b_w0, b_w1, b_w2, b_a0, b_a1, b_a2, b_g1, b_g2, b_k_k, b_k_a, b_r_k, b_ln_g, b_ln_b, b_w_out, final_g, loss_target, m_ada_w, m_ada_b, m_mlp_w1, m_mlp_w2, m_a_w_in, m_a_ln_g, m_a_ln_b, m_a_w_s, m_a_b_s, m_a_w_out, m_b_mu, m_b_w_in, m_b_w0, m_b_w1, m_b_w2, m_b_a0, m_b_a1, m_b_a2, m_b_g1, m_b_g2, m_b_k_k, m_b_k_a, m_b_r_k, m_b_ln_g, m_b_ln_b, m_b_w_out, m_final_g, v_ada_w, v_ada_b, v_mlp_w1, v_mlp_w2, v_a_w_in, v_a_ln_g, v_a_ln_b, v_a_w_s, v_a_b_s, v_a_w_out, v_b_mu, v_b_w_in, v_b_w0, v_b_w1, v_b_w2, v_b_a0, v_b_a1, v_b_a2, v_b_g1, v_b_g2, v_b_k_k, v_b_k_a, v_b_r_k, v_b_ln_g, v_b_ln_b, v_b_w_out, v_final_g):
    given = dict(locals())
    w = {n: given[n] for n in WEIGHTS}
    bl, t, d = x.shape
    nl = ada_w.shape[0]
    nb = N_DEV * bl
    m_tok = bl * t
    me = 4 * lax.axis_index("x") + 2 * lax.axis_index("y") + lax.axis_index("c")

    bf = lambda a: a.astype(BF16)
    vec_loc = _pad_rows(jnp.concatenate([_as2d(w[n]) for n in VECTORS], axis=0), ROW_ALIGN)
    g_c, g_vec, g_a_in, g_a_out = _gather_call((c, vec_loc, bf(a_w_in[0]), bf(a_w_out[0])), "gather_first")

    c_all = g_c.reshape(nb, d)
    cols = ada_w.shape[2]
    ada_b_cols = lax.dynamic_slice(ada_b, (0, me * cols), (nl, cols)).reshape(nl, 1, cols)
    mod_cols = _ada_fwd(c_all, ada_w, ada_b_cols)
    mod_full = jnp.moveaxis(_gather_call((mod_cols,), "gather_mod")[0], 0, 2).reshape(nl, nb, 6 * d)
    mod_mine = lax.dynamic_slice(mod_full, (0, me * bl, 0), (nl, bl, 6 * d)).reshape(nl, bl, 6, d)
    mod_mix = jnp.pad(mod_mine[:, :, 0:3], ((0, 0), (0, 0), (0, 5), (0, 0)))
    mod_mlp = jnp.pad(mod_mine[:, :, 3:6], ((0, 0), (0, 0), (0, 5), (0, 0)))

    def unshard(g, ax):
        g = jnp.moveaxis(g, 0, ax)
        return g.reshape(g.shape[:ax] + (g.shape[ax] * g.shape[ax + 1],) + g.shape[ax + 2:])

    lora_names = ['b_w1', 'b_a1', 'b_g1', 'b_w2', 'b_a2', 'b_g2']
    lora_pack = jnp.concatenate([bf(w[n]).reshape(-1) for n in lora_names]).reshape(-1, 128)
    full = {'a_w_in': unshard(g_a_in, 1), 'a_w_out': unshard(g_a_out, 0)}
    n_vec_rows = sum(_as2d(w[n]).shape[0] for n in VECTORS)
    vec = jnp.moveaxis(g_vec, 0, 1).reshape(N_VEC, d)
    vec = vec.at[n_vec_rows].set(b_r_k.reshape(d))

    e_ind, et_ind = _head_indicators(d)
    gd = d // SGU_GROUPS
    group_ind = (jnp.arange(SGU_GROUPS)[:, None] == jnp.arange(d)[None, :] // gd).astype(BF16)
    bias_full = jnp.repeat(a_b_s[0].T, gd, axis=1)
    pad_c = lambda a, n: jnp.pad(a, ((0, 0), (0, n - a.shape[1])))
    pad_r = lambda a, n: jnp.pad(a, ((0, n - a.shape[0]), (0, 0)))
    sgu_args = (full['a_w_in'], a_ln_g, a_ln_b, a_w_s[0], bias_full, full['a_w_out'])

    x0 = x
    (x1, mix_a), (g_w1_0, g_w2_0) = _sgu_fwd(x0, mod_mix[0], *sgu_args, gather=(bf(mlp_w1[0]), bf(mlp_w2[0])))
    w1_full = [unshard(g_w1_0, 1), None]
    w2_full = [unshard(g_w2_0, 0), None]
    (x2, ff0, q0), (g_w1_1, g_w2_1, g_b_in, g_b_out, g_lora) = _mlp_fwd(
        x1, mod_mlp[0], w1_full[0], w2_full[0],
        gather=(bf(mlp_w1[1]), bf(mlp_w2[1]), bf(b_w_in[0]), bf(b_w_out[0]), lora_pack))
    w1_full[1], w2_full[1] = unshard(g_w1_1, 1), unshard(g_w2_1, 0)
    full['b_w_in'], full['b_w_out'] = unshard(g_b_in, 1), unshard(g_b_out, 0)
    lora_flat, lo = g_lora.reshape(N_DEV, -1), 0
    for n, ax in zip(lora_names, (0, 0, 0, 1, 1, 1)):
        loc = w[n].shape[1:]
        full[n] = unshard(lora_flat[:, lo:lo + w[n].size].reshape((N_DEV,) + loc), ax)
        lo += w[n].size
    rwkv_w = (full['b_w_in'], pad_c(full['b_w1'], LORA_PAD), pad_c(full['b_a1'], LORA_PAD),
              pad_c(full['b_g1'], GATE_PAD), pad_r(full['b_w2'], LORA_PAD), pad_r(full['b_a2'], LORA_PAD),
              pad_r(full['b_g2'], GATE_PAD))
    r, ld, k2, v, a_s, b_s, gate = _rwkv_pre_fwd(x2, mod_mix[1], vec, e_ind, et_ind, rwkv_w)
    y, s0, tinv = _wkv_fwd(r, ld, k2, v, a_s, b_s)
    x3, mix_b = _rwkv_post_fwd(x2, y, r, k2, v, gate, mod_mix[1], vec, e_ind, et_ind, full['b_w_out'])
    (x4, ff1, q1), _ = _mlp_fwd(x3, mod_mlp[1], w1_full[1], w2_full[1])
    dx4, loss_blk, dfinal = _final(x4, loss_target, final_g.reshape(1, d))
    loss = lax.psum(loss_blk[0, 0], ("x", "y", "c"))

    tok = lambda a: a.reshape(m_tok, a.shape[-1])
    shard_rows = lambda g: g.reshape((N_DEV, g.shape[0] // N_DEV) + g.shape[1:])
    (dx3, dmod_mlp1, h_b, dff_b, dp_b), _ = _mlp_bwd(x3, dx4, ff1, q1, mod_mlp[1], w1_full[1], w2_full[1])
    gw1_1 = _matmul_tn(tok(h_b), tok(dp_b), "grad_mlp_w1_l1", col_shards=N_DEV)
    gw2_1 = shard_rows(_matmul_tn(tok(q1), tok(dff_b), "grad_mlp_w2_l1"))
    (dy, dr_p, dk2_p, dv_p, dgate_act, o_b, dmix_b, dgate_b, small_post), (rw1_1, rw2_1) = _rwkv_post_bwd(
        dx3, mix_b, y, r, k2, v, gate, mod_mix[1], vec, e_ind, et_ind, full['b_w_out'], scatter=(gw1_1, gw2_1))
    g_b_w_out = shard_rows(_matmul_tn(tok(o_b), tok(dmix_b), "grad_b_w_out"))
    dr_s, dld, dk2_s, dv_s, das, dbs = _wkv_bwd(r, ld, k2, v, a_s, b_s, s0, tinv, dy)
    (dh, dhp, xr_b, xw_b, xk_b, xv_b, xa_b, xg_b, dr_b, dk_b, dv_b, dtw_b, dta_b, dtg_b, tw2_b, ta_b, sg_b,
     dlw_b, dla_b, dg_b, small_pre) = _rwkv_pre_bwd(x2, mod_mix[1], vec, e_ind, et_ind, rwkv_w,
                                                    dr_p, dk2_p, dv_p, dgate_act, dr_s, dld, dk2_s, dv_s, das, dbs)
    g_b_w_in = jnp.concatenate([_matmul_tn(tok(xr_b), tok(dr_b), "grad_b_w_r"),
                                _matmul_tn(tok(xk_b), tok(dk_b), "grad_b_w_k"),
                                _matmul_tn(tok(xv_b), tok(dv_b), "grad_b_w_v")], axis=1)
    shard_cols = lambda g: jnp.moveaxis(g.reshape(g.shape[0], N_DEV, g.shape[1] // N_DEV), 1, 0)
    g_b_w_in = shard_cols(g_b_w_in)
    lw_, lg_ = b_w1.shape[2], b_g1.shape[2]
    small_names = ['b_w1', 'b_a1', 'b_g1', 'b_w2', 'b_a2', 'b_g2'] + VECTORS
    small_parts = [
        shard_rows(_matmul_tn(tok(xw_b), tok(dtw_b), "grad_b_w1")[:, :lw_]),
        shard_rows(_matmul_tn(tok(xa_b), tok(dta_b), "grad_b_a1")[:, :lw_]),
        shard_rows(_matmul_tn(tok(xg_b), tok(dtg_b), "grad_b_g1")[:, :lg_]),
        shard_cols(_matmul_tn(tok(tw2_b), tok(dlw_b), "grad_b_w2")[:lw_]),
        shard_cols(_matmul_tn(tok(ta_b), tok(dla_b), "grad_b_a2")[:lw_]),
        shard_cols(_matmul_tn(tok(sg_b), tok(dg_b), "grad_b_g2")[:lg_]),
        shard_cols(jnp.concatenate([small_pre[0:10], small_post[0:2]], axis=0).astype(BF16)),
    ]
    small_flat = jnp.concatenate([p.reshape(N_DEV, -1) for p in small_parts], axis=1)
    lane = 128
    small_rows = -(-small_flat.shape[1] // (lane * ROW_ALIGN)) * ROW_ALIGN
    small_pack = jnp.pad(small_flat, ((0, 0), (0, small_rows * lane - small_flat.shape[1]))).reshape(
        N_DEV, small_rows, lane)
    dx2, dmod_mix1 = _norm_bwd(x2, dx3, dh, dhp, mod_mix[1], dgate_b)
    (dx1, dmod_mlp0, h_b, dff_b, dp_b), (r_b_w_in, r_b_w_out, r_small) = _mlp_bwd(
        x1, dx2, ff0, q0, mod_mlp[0], w1_full[0], w2_full[0], scatter=(g_b_w_in, g_b_w_out, small_pack))
    gw1_0 = _matmul_tn(tok(h_b), tok(dp_b), "grad_mlp_w1_l0", col_shards=N_DEV)
    gw2_0 = shard_rows(_matmul_tn(tok(q0), tok(dff_b), "grad_mlp_w2_l0"))
    (dx0, dmod_mix0, h_b, dpre_b, z_b, dmix_b, small_sgu, d_ws, d_bs), (rw1_0, rw2_0) = _sgu_bwd(
        x0, dx1, mix_a, mod_mix[0], *sgu_args, group_ind, scatter=(gw1_0, gw2_0))
    dmod_mine = jnp.stack([jnp.concatenate([dmod_mix0[:, 0:3], dmod_mlp0[:, 0:3]], axis=1),
                           jnp.concatenate([dmod_mix1[:, 0:3], dmod_mlp1[:, 0:3]], axis=1)], axis=1)
    rep_g = {'a_ln_g': small_sgu[0:1], 'a_ln_b': small_sgu[1:2], 'a_w_s': d_ws.reshape(-1, d), 'a_b_s': d_bs.reshape(1, d),
             'b_r_k': small_post[2:3], 'final_g': dfinal[0:1]}
    rep_rows = [rep_g[n].shape[0] for n in REPLICATED]
    rep_pack = _pad_rows(jnp.concatenate([rep_g[n] for n in REPLICATED], axis=0), 8)
    g_a_w_in, (dmod_all, rep_all) = _matmul_tn(tok(h_b), tok(dpre_b), "grad_a_w_in", col_shards=N_DEV,
                                               gather=(dmod_mine.reshape(bl, nl * 6 * d), rep_pack))
    g_a_w_out = shard_rows(_matmul_tn(tok(z_b), tok(dmix_b), "grad_a_w_out"))
    r_a_w_in, r_a_w_out = _scatter_call((g_a_w_in, g_a_w_out), "scatter_sgu_grads")

    dmod_all = jnp.moveaxis(dmod_all.reshape(nb, nl, 6 * d), 0, 1)
    dmod_cols = lax.dynamic_slice(dmod_all, (0, 0, me * cols), (nl, nb, cols))
    g_ada_w, g_ada_b = _ada_bwd(c_all, dmod_cols, dmod_all)

    mom = {n: given['m_' + n] for n in WEIGHTS}
    var = {n: given['v_' + n] for n in WEIGHTS}
    out = {}
    as3d = lambda a: a.reshape((-1,) + a.shape[-2:])
    for n, parts in (('mlp_w1', [rw1_0, rw1_1]), ('mlp_w2', [rw2_0, rw2_1]), ('a_w_in', [r_a_w_in]),
                     ('a_w_out', [r_a_w_out]), ('b_w_in', [r_b_w_in]), ('b_w_out', [r_b_w_out]),
                     ('ada_w', list(g_ada_w))):
        res = _adamw_layers(as3d(w[n]), as3d(mom[n]), as3d(var[n]), parts, "adamw_" + n)
        out[n] = tuple(a.reshape(w[n].shape) for a in res)

    items, names = [], []

    def add(n, part):
        s2 = _as2d(w[n]).shape
        items.append((_as2d(w[n]), _as2d(mom[n]), _as2d(var[n]), part.reshape((part.shape[0],) + s2)))
        names.append(n)

    sflat = r_small.reshape(N_DEV, -1)
    so = 0
    for n in small_names:
        sz = w[n].size
        add(n, sflat[:, so:so + sz])
        so += sz
    ro = 0
    for n, nr in zip(REPLICATED, rep_rows):
        add(n, rep_all[:, ro:ro + nr])
        ro += nr
    add('ada_b', g_ada_b[None])
    for n, res in zip(names, _adamw_small(items, "adamw_small")):
        out[n] = tuple(a.reshape(w[n].shape) for a in res)

    return (loss, dx0, *[out[n][0] for n in WEIGHTS], *[out[n][1] for n in WEIGHTS],
            *[out[n][2] for n in WEIGHTS], *[out[n][3] for n in WEIGHTS])
```

```python
import functools

import jax
import jax.numpy as jnp
from jax import lax
from jax.experimental import pallas as pl
from jax.experimental.pallas import tpu as pltpu

F32 = jnp.float32
BF16 = jnp.bfloat16

N_DEV = 8
RMS_EPS = 1e-6
LN_EPS = 1e-5
HEAD = 64
GN_EPS = HEAD * 1e-5
L2_EPS = 1e-12
DECAY_SCALE = 0.6065306597126334
SGU_CHUNK = 128
SGU_GROUPS = 8
WKV_CHUNK = 64
WKV_HEADS_PER_STEP = 16
WKV_EXAMPLES_PER_STEP = 2
LORA_PAD = 128
GATE_PAD = 256
ADAM_LR, ADAM_B1, ADAM_B2, ADAM_EPS, ADAM_WD, ADAM_STEP = 0.001, 0.9, 0.999, 1e-08, 0.01, 10
VMEM_LIMIT = 56 * 1024 * 1024


def _cparams(sem=None, **kw):
    if sem is not None:
        kw["dimension_semantics"] = sem
    return pltpu.CompilerParams(vmem_limit_bytes=VMEM_LIMIT, **kw)


def _dot(a, b):
    return jnp.dot(a.astype(BF16), b.astype(BF16), preferred_element_type=F32)


def _dot_nt(a, b):
    return lax.dot_general(a.astype(BF16), b.astype(BF16), (((1,), (1,)), ((), ())), preferred_element_type=F32)


def _dot_tn(a, b):
    return lax.dot_general(a.astype(BF16), b.astype(BF16), (((0,), (0,)), ((), ())), preferred_element_type=F32)


def _bdot(a, b, dims):
    return lax.dot_general(a.astype(BF16), b.astype(BF16), (dims, ((0,), (0,))), preferred_element_type=F32)


@jax.custom_vjp
def _tri_sum(tri, tri_t, x):
    hi = x.astype(BF16)
    lo = (x - hi.astype(F32)).astype(BF16)
    dn = (((2,), (1,)), ((0,), (0,)))
    return (lax.dot_general(tri, hi, dn, preferred_element_type=F32)
            + lax.dot_general(tri, lo, dn, preferred_element_type=F32))


_tri_sum.defvjp(lambda tri, tri_t, x: (_tri_sum(tri, tri_t, x), (tri, tri_t)),
                lambda res, g: (jnp.zeros_like(res[0]), jnp.zeros_like(res[1]), _tri_sum(res[1], res[0], g)))


@jax.custom_vjp
def _bmm_nn(a, b):
    return _bdot(a, b, ((2,), (1,)))


@jax.custom_vjp
def _bmm_nt(a, b):
    return _bdot(a, b, ((2,), (2,)))


@jax.custom_vjp
def _bmm_tn(a, b):
    return _bdot(a, b, ((1,), (1,)))


_bmm_nn.defvjp(lambda a, b: (_bmm_nn(a, b), (a, b)), lambda res, g: (_bmm_nt(g, res[1]), _bmm_tn(res[0], g)))
_bmm_nt.defvjp(lambda a, b: (_bmm_nt(a, b), (a, b)), lambda res, g: (_bmm_nn(g, res[1]), _bmm_tn(g, res[0])))
_bmm_tn.defvjp(lambda a, b: (_bmm_tn(a, b), (a, b)), lambda res, g: (_bmm_nt(res[1], g), _bmm_nn(res[0], g)))


def _tri_inverse(p):
    n = p.shape[1]
    row = lax.broadcasted_iota(jnp.int32, (n, n), 0)
    col = lax.broadcasted_iota(jnp.int32, (n, n), 1)
    tinv = jnp.where(row == col, 1.0, 0.0).astype(F32)[None] + p
    for _ in range(max(1, (n - 1).bit_length()) - 1):
        p = _bmm_nn(p, p)
        tinv = tinv + _bmm_nn(tinv, p)
    return tinv.astype(BF16)


def _tri_solve_fwd(tinv, p, rhs):
    u = _bmm_nn(tinv, rhs)
    return u, (tinv, u)


def _tri_solve_bwd(res, du):
    tinv, u = res
    drhs = _bmm_tn(tinv, du)
    return jnp.zeros_like(tinv), _bmm_nt(drhs, u), drhs


@jax.custom_vjp
def _tri_solve(tinv, p, rhs):
    return _tri_solve_fwd(tinv, p, rhs)[0]


_tri_solve.defvjp(_tri_solve_fwd, _tri_solve_bwd)


def _wkv_chunk(s0, r, ld, k, v, a, b, tinv=None):
    nh, n, _ = r.shape
    row = lax.broadcasted_iota(jnp.int32, (n, n), 0)
    col = lax.broadcasted_iota(jnp.int32, (n, n), 1)
    incl = row >= col
    strict = row > col
    lower = jnp.broadcast_to(jnp.where(incl, 1.0, 0.0).astype(BF16), (nh, n, n))
    upper = jnp.broadcast_to(jnp.where(row <= col, 1.0, 0.0).astype(BF16), (nh, n, n))
    c = _tri_sum(lower, upper, ld)
    c_end = c[:, n - 1:n, :]
    ec, enc, ecx, eend = jnp.exp(c), jnp.exp(-c), jnp.exp(c - ld), jnp.exp(c_end - c)
    ar = jnp.concatenate([a * ecx, r * ec], axis=1)
    mask = jnp.concatenate([strict, incl], axis=0)[None]
    m_b = jnp.where(mask, _bmm_nt(ar, b * enc), 0.0)
    m_k = jnp.where(mask, _bmm_nt(ar, k * enc), 0.0)
    a_ab, a_rb = m_b[:, :n], m_b[:, n:]
    base = _bmm_nt(ar, s0) + _bmm_nn(m_k, v)
    if tinv is None:
        tinv = lax.stop_gradient(_tri_inverse(a_ab))
    u = _tri_solve(tinv, a_ab, base[:, :n])
    y = base[:, n:] + _bmm_nn(a_rb, u)
    s1 = s0 * jnp.exp(c_end) + _bmm_tn(jnp.concatenate([u, v], axis=1), jnp.concatenate([b * eend, k * eend], axis=1))
    return y, s1, tinv


def _wkv_specs(bl, nh, t):
    eb, hb, lc = min(bl, WKV_EXAMPLES_PER_STEP), min(nh, WKV_HEADS_PER_STEP), WKV_CHUNK
    return eb, hb, lc, (bl // eb, nh // hb, t // lc)


def _wkv_fwd(r, ld, k, v, a, b):
    bl, nh, t, n = r.shape
    eb, hb, lc, grid = _wkv_specs(bl, nh, t)
    nc = t // lc
    nb = eb * hb

    def body(r_ref, ld_ref, k_ref, v_ref, a_ref, b_ref, y_ref, s0_ref, tinv_ref, s_scr):
        @pl.when(pl.program_id(2) == 0)
        def _():
            s_scr[...] = jnp.zeros_like(s_scr)

        s0 = s_scr[...]
        s0_ref[:, :, 0] = s0.reshape(eb, hb, n, n)
        y, s1, tinv = _wkv_chunk(
            s0, *(ref[...].reshape(nb, lc, n) for ref in (r_ref, ld_ref, k_ref, v_ref, a_ref, b_ref)))
        y_ref[...] = y.reshape(eb, hb, lc, n)
        tinv_ref[:, :, 0] = tinv.reshape(eb, hb, lc, lc)
        s_scr[...] = s1

    seq = pl.BlockSpec((eb, hb, lc, n), lambda e, h, c: (e, h, c, 0))
    return pl.pallas_call(
        body, name="wkv_fwd", grid=grid,
        in_specs=[seq] * 6,
        out_specs=(seq, pl.BlockSpec((eb, hb, 1, n, n), lambda e, h, c: (e, h, c, 0, 0)),
                   pl.BlockSpec((eb, hb, 1, lc, lc), lambda e, h, c: (e, h, c, 0, 0))),
        out_shape=(jax.ShapeDtypeStruct((bl, nh, t, n), F32), jax.ShapeDtypeStruct((bl, nh, nc, n, n), F32),
                   jax.ShapeDtypeStruct((bl, nh, nc, lc, lc), BF16)),
        scratch_shapes=[pltpu.VMEM((nb, n, n), F32)],
        compiler_params=_cparams(("arbitrary", "arbitrary", "arbitrary")),
    )(r, ld, k, v, a, b)


def _wkv_bwd(r, ld, k, v, a, b, s0_all, tinv_all, dy):
    bl, nh, t, n = r.shape
    eb, hb, lc, grid = _wkv_specs(bl, nh, t)
    nc = t // lc
    nb = eb * hb

    def body(r_ref, ld_ref, k_ref, v_ref, a_ref, b_ref, s0_ref, tinv_ref, dy_ref,
             dr_ref, dld_ref, dk_ref, dv_ref, da_ref, db_ref, ds_scr):
        @pl.when(pl.program_id(2) == 0)
        def _():
            ds_scr[...] = jnp.zeros_like(ds_scr)

        args = (s0_ref[:, :, 0].reshape(nb, n, n),) + tuple(
            ref[...].reshape(nb, lc, n) for ref in (r_ref, ld_ref, k_ref, v_ref, a_ref, b_ref))
        tinv = tinv_ref[:, :, 0].reshape(nb, lc, lc)
        _, vjp = jax.vjp(lambda *xs: _wkv_chunk(*xs, tinv=tinv)[:2], *args)
        ds0, *dseq = vjp((dy_ref[...].reshape(nb, lc, n), ds_scr[...]))
        ds_scr[...] = ds0
        for ref, val in zip((dr_ref, dld_ref, dk_ref, dv_ref, da_ref, db_ref), dseq):
            ref[...] = val.reshape(eb, hb, lc, n)

    seq = pl.BlockSpec((eb, hb, lc, n), lambda e, h, c: (e, h, nc - 1 - c, 0))
    st = pl.BlockSpec((eb, hb, 1, n, n), lambda e, h, c: (e, h, nc - 1 - c, 0, 0))
    ti = pl.BlockSpec((eb, hb, 1, lc, lc), lambda e, h, c: (e, h, nc - 1 - c, 0, 0))
    out = jax.ShapeDtypeStruct((bl, nh, t, n), F32)
    return pl.pallas_call(
        body, name="wkv_bwd", grid=grid,
        in_specs=[seq] * 6 + [st, ti, seq],
        out_specs=(seq,) * 6, out_shape=(out,) * 6,
        scratch_shapes=[pltpu.VMEM((nb, n, n), F32)],
        compiler_params=_cparams(("arbitrary", "arbitrary", "arbitrary")),
    )(r, ld, k, v, a, b, s0_all, tinv_all, dy)


def _scatter_copies(x_refs, o_refs, send_sems, recv_sems, local_sems):
    pos = (lax.axis_index("x"), lax.axis_index("y"), lax.axis_index("c"))
    me = 4 * pos[0] + 2 * pos[1] + pos[2]

    def descriptors():
        sends, arrivals, local = [], [], []
        for i, (x_ref, o_ref) in enumerate(zip(x_refs, o_refs)):
            for m in range(1, N_DEV):
                p = tuple(1 - pos[a] if (m >> (2 - a)) & 1 else pos[a] for a in range(3))
                pidx = 4 * p[0] + 2 * p[1] + p[2]
                k = (N_DEV - 1) * i + m - 1
                for dst, out in ((o_ref.at[me], sends), (o_ref.at[pidx], arrivals)):
                    out.append(pltpu.make_async_remote_copy(
                        src_ref=x_ref.at[pidx], dst_ref=dst, send_sem=send_sems.at[k], recv_sem=recv_sems.at[k],
                        device_id=p, device_id_type=pl.DeviceIdType.MESH))
            local.append(pltpu.make_async_copy(x_ref.at[me], o_ref.at[me], local_sems.at[i]))
        return sends, arrivals, local

    def start():
        sends, _, local = descriptors()
        for cp in local + sends:
            cp.start()

    def finish():
        sends, arrivals, local = descriptors()
        for cp in arrivals:
            cp.wait_recv()
        for cp in sends:
            cp.wait_send()
        for cp in local:
            cp.wait()

    return start, finish


def _gather_copies(x_refs, o_refs, send_sems, recv_sems, local_sems):
    pos = (lax.axis_index("x"), lax.axis_index("y"), lax.axis_index("c"))
    me = 4 * pos[0] + 2 * pos[1] + pos[2]
    far = (2, 4, 6)

    def peer_of(m):
        p = tuple(1 - pos[a] if (m >> (2 - a)) & 1 else pos[a] for a in range(3))
        return p, 4 * p[0] + 2 * p[1] + p[2]

    sibling, _ = peer_of(1)

    def copy(i, k, src, slot, to):
        return pltpu.make_async_remote_copy(
            src_ref=src, dst_ref=o_refs[i].at[slot], send_sem=send_sems.at[(N_DEV - 1) * i + k],
            recv_sem=recv_sems.at[(N_DEV - 1) * i + k], device_id=to, device_id_type=pl.DeviceIdType.MESH)

    def direct(i):
        return [copy(i, m - 1, x_refs[i], me, peer_of(m)[0]) for m in (1,) + far]

    def local(i):
        return pltpu.make_async_copy(x_refs[i], o_refs[i].at[me], local_sems.at[i])

    def start():
        for i in range(len(x_refs)):
            local(i).start()
            for cp in direct(i):
                cp.start()

    def relays():
        return [copy(i, m, o_refs[i].at[peer_of(m)[1]], peer_of(m)[1], sibling)
                for i in range(len(x_refs)) for m in far]

    def relay():
        for i in range(len(x_refs)):
            for m in far:
                copy(i, m - 1, x_refs[i], peer_of(m)[1], sibling).wait_recv()
        for fwd in relays():
            fwd.start()

    def finish():
        n = len(x_refs)
        for i in range(n):
            copy(i, 0, x_refs[i], peer_of(1)[1], sibling).wait_recv()
            for m in far:
                copy(i, m, x_refs[i], peer_of(m ^ 1)[1], sibling).wait_recv()
        for i in range(n):
            for cp in direct(i):
                cp.wait_send()
            local(i).wait()
        for fwd in relays():
            fwd.wait_send()

    return start, relay, finish


def _scatter_scratch(n):
    return [pltpu.SemaphoreType.DMA(((N_DEV - 1) * n,)), pltpu.SemaphoreType.DMA(((N_DEV - 1) * n,)),
            pltpu.SemaphoreType.DMA((n,))]


_ANY = pl.BlockSpec(memory_space=pl.ANY)


def _scatter_call(arrays, name):
    n = len(arrays)

    def body(*refs):
        start, finish = _scatter_copies(refs[:n], refs[n:2 * n], *refs[2 * n:])
        start()
        finish()

    return pl.pallas_call(
        body, name=name, in_specs=[_ANY] * n, out_specs=(_ANY,) * n,
        out_shape=tuple(_sds(a.shape, a.dtype) for a in arrays), scratch_shapes=_scatter_scratch(n),
    )(*arrays)


def _gather_call(arrays, name):
    n = len(arrays)

    def body(*refs):
        start, relay, finish = _gather_copies(refs[:n], refs[n:2 * n], *refs[2 * n:])
        start()
        relay()
        finish()

    return pl.pallas_call(
        body, name=name, in_specs=[_ANY] * n, out_specs=(_ANY,) * n,
        out_shape=tuple(_sds((N_DEV,) + a.shape, a.dtype) for a in arrays), scratch_shapes=_scatter_scratch(n),
    )(*arrays)


def _call_with_scatter(body, *, name, grid, in_specs, out_specs, out_shape, scratch_shapes, operands,
                       scatter=(), gather=()):
    assert not (scatter and gather)
    carried = tuple(scatter) or tuple(gather)
    copies = _scatter_copies if scatter else _gather_copies
    recv_shapes = tuple(_sds(a.shape if scatter else (N_DEV,) + a.shape, a.dtype) for a in carried)
    nc, n_in, n_out, n_scr = len(carried), len(in_specs), len(out_specs), len(scratch_shapes)
    if nc == 0:
        return pl.pallas_call(
            body, name=name, grid=grid, in_specs=list(in_specs), out_specs=tuple(out_specs),
            out_shape=tuple(out_shape), scratch_shapes=list(scratch_shapes),
            compiler_params=_cparams(("arbitrary",) * len(grid)))(*operands), ()

    def wrapped(*refs):
        ins, refs = refs[:n_in], refs[n_in:]
        c_in, refs = refs[:nc], refs[nc:]
        outs, refs = refs[:n_out], refs[n_out:]
        c_out, refs = refs[:nc], refs[nc:]
        scr, sems = refs[:n_scr], refs[n_scr:]
        ids = [pl.program_id(a) for a in range(len(grid))]
        first = functools.reduce(jnp.logical_and, [i == 0 for i in ids])
        last = functools.reduce(jnp.logical_and, [i == g - 1 for i, g in zip(ids, grid)])
        start, *relay, finish = copies(c_in, c_out, *sems)
        pl.when(first)(start)
        if relay:
            pl.when(last)(relay[0])
        body(*ins, *outs, *scr)
        pl.when(last)(finish)

    res = pl.pallas_call(
        wrapped, name=name, grid=grid,
        in_specs=list(in_specs) + [_ANY] * nc, out_specs=tuple(out_specs) + (_ANY,) * nc,
        out_shape=tuple(out_shape) + recv_shapes,
        scratch_shapes=list(scratch_shapes) + _scatter_scratch(nc),
        compiler_params=_cparams(("arbitrary",) * len(grid)),
    )(*operands, *carried)
    return res[:n_out], res[n_out:]


def _rms(x):
    inv = lax.rsqrt(jnp.mean(x * x, axis=-1, keepdims=True) + RMS_EPS)
    return x * inv, inv


def _rms_bwd(xn, inv, dxn):
    return inv * (dxn - xn * jnp.mean(dxn * xn, axis=-1, keepdims=True))


def _colsum(x):
    return jnp.sum(x, axis=0, keepdims=True)


def _sigmoid(x):
    return 0.5 * (jnp.tanh(0.5 * x) + 1.0)


def _split_bf16(x):
    hi = x.astype(BF16)
    return hi, (x - hi.astype(F32)).astype(BF16)


def _dot_split(x, e):
    hi, lo = _split_bf16(x)
    return jnp.dot(hi, e, preferred_element_type=F32) + jnp.dot(lo, e, preferred_element_type=F32)


@jax.custom_vjp
def _headsum(x, e, et):
    return _dot_split(_dot_split(x, e), et)


_headsum.defvjp(lambda x, e, et: (_headsum(x, e, et), (e, et)),
                lambda res, g: (_headsum(g, *res), jnp.zeros_like(res[0]), jnp.zeros_like(res[1])))


def _make_headsum(e, et):
    return lambda x: _headsum(x, e, et)


def _head_indicators(d):
    e = (jnp.arange(d)[:, None] // HEAD == jnp.arange(128)[None, :]).astype(BF16)
    return e, e.T


def _rwkv_elem(r, k, lw, la, w0, a0, k_k, k_a, headsum):
    ld = -DECAY_SCALE * _sigmoid(w0 + lw)
    a = _sigmoid(a0 + la)
    kkp = k * k_k
    kk = kkp * lax.rsqrt(jnp.maximum(headsum(kkp * kkp), L2_EPS * L2_EPS))
    k2 = k * (1.0 + (a - 1.0) * k_a)
    del r
    return ld, k2, -kk, kk * a


def _rwkv_post(y, r, k2, v, g, ln_g, ln_b, r_k, headsum):
    m = headsum(y) * (1.0 / HEAD)
    yc = y - m
    var = headsum(yc * yc) * (1.0 / HEAD)
    yn = yc * lax.rsqrt(var + GN_EPS)
    bonus = headsum(r * k2 * r_k) * v
    return (yn * ln_g + ln_b + bonus) * g


def _shift_down(h, first_row):
    rolled = pltpu.roll(h, 1, 0)
    row = lax.broadcasted_iota(jnp.int32, h.shape, 0)
    return jnp.where(row == 0, first_row, rolled)


def _shift_up(h, last_row):
    n = h.shape[0]
    rolled = pltpu.roll(h, n - 1, 0)
    row = lax.broadcasted_iota(jnp.int32, h.shape, 0)
    return jnp.where(row == n - 1, last_row, rolled)


def _gelu(p):
    return 0.5 * p * (1.0 + lax.erf(p * 0.7071067811865476))


def _gelu_grad(p):
    return 0.5 * (1.0 + lax.erf(p * 0.7071067811865476)) + p * jnp.exp(-0.5 * p * p) * 0.3989422804014327


def _tok(tm, d):
    return pl.BlockSpec((1, tm, d), lambda e, t, *_: (e, t, 0))


def _per_example(rows, d):
    return pl.BlockSpec((1, rows, d), lambda e, t, *_: (e, 0, 0))


def _whole(shape):
    nd = len(shape)
    return pl.BlockSpec(tuple(shape), lambda *_: (0,) * nd)


def _heads(nh, tm):
    return pl.BlockSpec((1, nh, tm, HEAD), lambda e, t, *_: (e, 0, t, 0))


def _sds(shape, dtype=F32):
    return jax.ShapeDtypeStruct(tuple(shape), dtype)


def _add_rows(ref, first, rows):
    @pl.when(first)
    def _():
        ref[0] = jnp.zeros(ref.shape[1:], ref.dtype)

    for i, r in enumerate(rows):
        ref[0, i:i + 1] += r


def _first(e, t):
    return jnp.logical_and(e == 0, t == 0)


def _ada_fwd(c_all, ada_w, ada_b_cols):
    nl, d, cols = ada_w.shape
    nb = c_all.shape[0]

    def body(c_ref, w_ref, b_ref, o_ref):
        c = c_ref[...]
        cond = c * _sigmoid(c)
        for i in range(nl):
            o_ref[i] = _dot(cond, w_ref[i]) + b_ref[i]

    return pl.pallas_call(
        body, name="ada_fwd", out_shape=_sds((nl, nb, cols)),
        compiler_params=_cparams(),
    )(c_all, ada_w, ada_b_cols)


def _ada_bwd(c_all, dmod_cols, dmod_full):
    nl, nb, cols = dmod_cols.shape
    d = c_all.shape[1]

    def body(c_ref, g_ref, f_ref, b_ref, *o_refs):
        c = c_ref[...]
        cond = c * _sigmoid(c)
        for i in range(nl):
            o_refs[i][0] = _dot_tn(cond, g_ref[i])
            b_ref[i:i + 1] = jnp.sum(f_ref[i], axis=0, keepdims=True)

    res = pl.pallas_call(
        body, name="ada_bwd", out_shape=(_sds((nl, dmod_full.shape[2])),) + (_sds((1, d, cols)),) * nl,
        compiler_params=_cparams(),
    )(c_all, dmod_cols, dmod_full)
    return res[1:], res[0]


def _matmul_tn(a, b, name, col_shards=None, gather=()):
    m, ka = a.shape
    n = b.shape[1]
    tm = min(m, 2048)
    tk = min(ka, 1024)
    tn = min(n, 1024)
    steps = m // tm
    if col_shards:
        cs = n // col_shards
        spt = tn // cs
        out_spec = pl.BlockSpec((spt, tk, cs), lambda i, j, s: (j, i, 0))
        out_shape = _sds((col_shards, ka, cs), BF16)
    else:
        out_spec = pl.BlockSpec((tk, tn), lambda i, j, s: (i, j))
        out_shape = _sds((ka, n), BF16)

    def body(a_ref, b_ref, o_ref, acc):
        s = pl.program_id(2)

        @pl.when(s == 0)
        def _():
            acc[...] = jnp.zeros_like(acc)

        acc[...] += _dot_tn(a_ref[...], b_ref[...])

        @pl.when(s == steps - 1)
        def _():
            if col_shards:
                for q in range(spt):
                    o_ref[q] = acc[:, q * cs:(q + 1) * cs].astype(BF16)
            else:
                o_ref[...] = acc[...].astype(BF16)

    res, got = _call_with_scatter(
        body, name=name, grid=(ka // tk, n // tn, steps),
        in_specs=[pl.BlockSpec((tm, tk), lambda i, j, s: (s, i)), pl.BlockSpec((tm, tn), lambda i, j, s: (s, j))],
        out_specs=(out_spec,), out_shape=(out_shape,),
        scratch_shapes=[pltpu.VMEM((tk, tn), F32)], operands=(a, b), gather=gather)
    return (res[0], got) if gather else res[0]


MLP_FWD_TM = 1024
MLP_FJ = 1024
MLP_BWD_TM = 512
MLP_BWD_FJ = 1024


def _mlp_fwd(x, mod, w1, w2, gather=()):
    bl, t, d = x.shape
    f = w1.shape[1]
    tm, fj = min(t, MLP_FWD_TM), min(f, MLP_FJ)
    nj = f // fj

    def body(x_ref, mod_ref, w1_ref, w2_ref, xo_ref, ff_ref, q_ref, h_scr, acc):
        j = pl.program_id(2)

        @pl.when(j == 0)
        def _():
            xn, _ = _rms(x_ref[0])
            h_scr[...] = (xn * (1.0 + mod_ref[0, 1:2]) + mod_ref[0, 0:1]).astype(BF16)
            acc[...] = jnp.zeros_like(acc)

        p = jnp.dot(h_scr[...], w1_ref[...], preferred_element_type=F32)
        q = jnp.square(jnp.maximum(p, 0.0)).astype(BF16)
        q_ref[0] = q
        acc[...] += jnp.dot(q, w2_ref[...], preferred_element_type=F32)

        @pl.when(j == nj - 1)
        def _():
            ff_ref[0] = acc[...]
            xo_ref[0] = x_ref[0] + mod_ref[0, 2:3] * acc[...]

    return _call_with_scatter(
        body, name="mlp_fwd", grid=(bl, t // tm, nj),
        in_specs=[_tok(tm, d), _per_example(8, d),
                  pl.BlockSpec((d, fj), lambda e, i, j: (0, j)), pl.BlockSpec((fj, d), lambda e, i, j: (j, 0))],
        out_specs=(_tok(tm, d), _tok(tm, d), pl.BlockSpec((1, tm, fj), lambda e, i, j: (e, i, j))),
        out_shape=(_sds(x.shape), _sds(x.shape), _sds((bl, t, f), BF16)),
        scratch_shapes=[pltpu.VMEM((tm, d), BF16), pltpu.VMEM((tm, d), F32)],
        operands=(x, mod, w1, w2), gather=gather)


def _mlp_bwd(x, dxo, ff, q, mod, w1, w2, scatter=()):
    bl, t, d = x.shape
    f = w1.shape[1]
    tm, fj = min(t, MLP_BWD_TM), min(f, MLP_BWD_FJ)
    nj = f // fj

    def body(x_ref, dxo_ref, ff_ref, q_ref, mod_ref, w1_ref, w2_ref,
             dx_ref, dmod_ref, h_ref, dff_ref, dp_ref, acc):
        ti, j = pl.program_id(1), pl.program_id(2)

        @pl.when(j == 0)
        def _():
            xn, _ = _rms(x_ref[0])
            h_ref[0] = (xn * (1.0 + mod_ref[0, 1:2]) + mod_ref[0, 0:1]).astype(BF16)
            dff_ref[0] = (mod_ref[0, 2:3] * dxo_ref[0]).astype(BF16)
            acc[...] = jnp.zeros_like(acc)

        rl = jnp.sqrt(q_ref[0].astype(F32))
        dp = (_dot_nt(dff_ref[0], w2_ref[...]) * (2.0 * rl)).astype(BF16)
        dp_ref[0] = dp
        acc[...] += _dot_nt(dp, w1_ref[...])

        @pl.when(j == nj - 1)
        def _():
            xn, inv = _rms(x_ref[0])
            dh = acc[...]
            dx_ref[0] = dxo_ref[0] + _rms_bwd(xn, inv, dh * (1.0 + mod_ref[0, 1:2]))
            _add_rows(dmod_ref, ti == 0, [_colsum(dh), _colsum(dh * xn), _colsum(dxo_ref[0] * ff_ref[0])])

    big = lambda: pl.BlockSpec((1, tm, fj), lambda e, i, j: (e, i, j))
    return _call_with_scatter(
        body, name="mlp_bwd", grid=(bl, t // tm, nj),
        in_specs=[_tok(tm, d), _tok(tm, d), _tok(tm, d), big(), _per_example(8, d),
                  pl.BlockSpec((d, fj), lambda e, i, j: (0, j)), pl.BlockSpec((fj, d), lambda e, i, j: (j, 0))],
        out_specs=(_tok(tm, d), _per_example(8, d), _tok(tm, d), _tok(tm, d), big()),
        out_shape=(_sds(x.shape), _sds((bl, 8, d)), _sds(x.shape, BF16), _sds(x.shape, BF16),
                   _sds((bl, t, f), BF16)),
        scratch_shapes=[pltpu.VMEM((tm, d), F32)],
        operands=(x, dxo, ff, q, mod, w1, w2), scatter=scatter)


SGU_TM = 512


def _sgu_core(x, mod_ref, win_ref, lng, lnb, ws_ref, bias_ref):
    tm, d = x.shape
    xn, inv = _rms(x)
    h = (xn * (1.0 + mod_ref[0, 1:2]) + mod_ref[0, 0:1]).astype(BF16)
    pre = jnp.dot(h, win_ref[...], preferred_element_type=F32)
    uv = _gelu(pre)
    u, v = uv[:, :d], uv[:, d:]
    mu = jnp.mean(v, axis=-1, keepdims=True)
    vc = v - mu
    rstd = lax.rsqrt(jnp.mean(vc * vc, axis=-1, keepdims=True) + LN_EPS)
    vhat = vc * rstd
    vln = vhat * lng + lnb
    gd = d // SGU_GROUPS
    rows = []
    for c in range(tm // SGU_CHUNK):
        cols = []
        for g in range(SGU_GROUPS):
            cols.append(_dot(ws_ref[g], vln[c * SGU_CHUNK:(c + 1) * SGU_CHUNK, g * gd:(g + 1) * gd]))
        rows.append(jnp.concatenate(cols, axis=1) + bias_ref[...])
    sv = jnp.concatenate(rows, axis=0)
    return xn, inv, h, pre, u, vhat, rstd, vln, sv


def _sgu_masked(ws_ref, wm_scr):
    row = lax.broadcasted_iota(jnp.int32, (SGU_CHUNK, SGU_CHUNK), 0)
    col = lax.broadcasted_iota(jnp.int32, (SGU_CHUNK, SGU_CHUNK), 1)
    for g in range(SGU_GROUPS):
        wm_scr[g] = jnp.where(row >= col, ws_ref[g], 0.0).astype(BF16)


def _sgu_fwd(x, mod, w_in, ln_g, ln_b, w_s, bias_full, w_out, gather=()):
    bl, t, d = x.shape
    tm = min(t, SGU_TM)

    def body(x_ref, mod_ref, win_ref, lng_ref, lnb_ref, ws_ref, bias_ref, wout_ref, xo_ref, mix_ref, wm_scr):
        _sgu_masked(ws_ref, wm_scr)
        xt = x_ref[0]
        *_, u, _, _, _, sv = _sgu_core(xt, mod_ref, win_ref, lng_ref[...], lnb_ref[...], wm_scr, bias_ref)
        mix = _dot(u * sv, wout_ref[...])
        mix_ref[0] = mix
        xo_ref[0] = xt + mod_ref[0, 2:3] * mix

    return _call_with_scatter(
        body, name="sgu_fwd", grid=(bl, t // tm),
        in_specs=[_tok(tm, d), _per_example(8, d), _whole(w_in.shape), _whole(ln_g.shape), _whole(ln_b.shape),
                  _whole(w_s.shape), _whole(bias_full.shape), _whole(w_out.shape)],
        out_specs=(_tok(tm, d), _tok(tm, d)),
        out_shape=(_sds(x.shape), _sds(x.shape)),
        scratch_shapes=[pltpu.VMEM(w_s.shape, BF16)],
        operands=(x, mod, w_in, ln_g, ln_b, w_s, bias_full, w_out), gather=gather)


def _sgu_bwd(x, dxo, mix, mod, w_in, ln_g, ln_b, w_s, bias_full, w_out, group_ind, scatter=()):
    bl, t, d = x.shape
    tm = min(t, SGU_TM)
    gd = d // SGU_GROUPS

    def body(x_ref, dxo_ref, mix_ref, mod_ref, win_ref, lng_ref, lnb_ref, ws_ref, bias_ref, wout_ref, ind_ref,
             dx_ref, dmod_ref, h_ref, dpre_ref, z_ref, dmix_ref, small_ref, dws_ref, dbs_ref, wm_scr, dbias_scr):
        e, ti = pl.program_id(0), pl.program_id(1)
        _sgu_masked(ws_ref, wm_scr)
        xt, dxo = x_ref[0], dxo_ref[0]
        lng = lng_ref[...]
        xn, inv, h, pre, u, vhat, rstd, vln, sv = _sgu_core(xt, mod_ref, win_ref, lng, lnb_ref[...], wm_scr, bias_ref)
        h_ref[0] = h
        z_ref[0] = (u * sv).astype(BF16)
        dmix = mod_ref[0, 2:3] * dxo
        dmix_ref[0] = dmix.astype(BF16)
        dz = _dot_nt(dmix, wout_ref[...])
        du, dsv = dz * sv, dz * u

        @pl.when(_first(e, ti))
        def _():
            dws_ref[...] = jnp.zeros_like(dws_ref)
            dbias_scr[...] = jnp.zeros_like(dbias_scr)
            small_ref[...] = jnp.zeros_like(small_ref)

        row = lax.broadcasted_iota(jnp.int32, (SGU_CHUNK, SGU_CHUNK), 0)
        col = lax.broadcasted_iota(jnp.int32, (SGU_CHUNK, SGU_CHUNK), 1)
        rows = []
        for c in range(tm // SGU_CHUNK):
            rs = slice(c * SGU_CHUNK, (c + 1) * SGU_CHUNK)
            dbias_scr[...] += dsv[rs]
            cols = []
            for g in range(SGU_GROUPS):
                cs = slice(g * gd, (g + 1) * gd)
                cols.append(_dot_tn(wm_scr[g], dsv[rs, cs]))
                dws_ref[g] += jnp.where(row >= col, _dot_nt(dsv[rs, cs], vln[rs, cs]), 0.0)
            rows.append(jnp.concatenate(cols, axis=1))
        dvln = jnp.concatenate(rows, axis=0)
        small_ref[0:1] += _colsum(dvln * vhat)
        small_ref[1:2] += _colsum(dvln)
        dvhat = dvln * lng
        dv = rstd * (dvhat - jnp.mean(dvhat, axis=-1, keepdims=True)
                     - vhat * jnp.mean(dvhat * vhat, axis=-1, keepdims=True))
        dpre = (jnp.concatenate([du, dv], axis=1) * _gelu_grad(pre)).astype(BF16)
        dpre_ref[0] = dpre
        dh = _dot_nt(dpre, win_ref[...])
        dx_ref[0] = dxo + _rms_bwd(xn, inv, dh * (1.0 + mod_ref[0, 1:2]))
        _add_rows(dmod_ref, ti == 0, [_colsum(dh), _colsum(dh * xn), _colsum(dxo * mix_ref[0])])

        @pl.when(jnp.logical_and(e == bl - 1, ti == t // tm - 1))
        def _():
            hi, lo = _split_bf16(dbias_scr[...])
            ind = ind_ref[...]
            dbs_ref[...] = (lax.dot_general(ind, hi, (((1,), (1,)), ((), ())), preferred_element_type=F32)
                            + lax.dot_general(ind, lo, (((1,), (1,)), ((), ())), preferred_element_type=F32))

    return _call_with_scatter(
        body, name="sgu_bwd", grid=(bl, t // tm),
        in_specs=[_tok(tm, d), _tok(tm, d), _tok(tm, d), _per_example(8, d), _whole(w_in.shape), _whole(ln_g.shape),
                  _whole(ln_b.shape), _whole(w_s.shape), _whole(bias_full.shape), _whole(w_out.shape),
                  _whole(group_ind.shape)],
        out_specs=(_tok(tm, d), _per_example(8, d), _tok(tm, d), _tok(tm, 2 * d), _tok(tm, d), _tok(tm, d),
                   _whole((8, d)), _whole(w_s.shape), _whole((SGU_GROUPS, SGU_CHUNK))),
        out_shape=(_sds(x.shape), _sds((bl, 8, d)), _sds(x.shape, BF16), _sds((bl, t, 2 * d), BF16),
                   _sds(x.shape, BF16), _sds(x.shape, BF16), _sds((8, d)), _sds(w_s.shape),
                   _sds((SGU_GROUPS, SGU_CHUNK))),
        scratch_shapes=[pltpu.VMEM(w_s.shape, BF16), pltpu.VMEM((SGU_CHUNK, d), F32)],
        operands=(x, dxo, mix, mod, w_in, ln_g, ln_b, w_s, bias_full, w_out, group_ind), scatter=scatter)


RWKV_TM = 256
N_VEC = 16


def _rwkv_pre_core(x_ref, halo_ref, mod_ref, vec_ref, ti):
    xn, inv = _rms(x_ref[0])
    scale1, shift = 1.0 + mod_ref[0, 1:2], mod_ref[0, 0:1]
    h = xn * scale1 + shift
    hn, _ = _rms(halo_ref[0])
    hh = hn * scale1 + shift
    first = jnp.where(ti == 0, 0.0, hh[7:8])
    xx = _shift_down(h, first) - h
    xs = [h + xx * vec_ref[i:i + 1] for i in range(6)]
    return xn, inv, xx, xs


def _rwkv_proj(xs, wrkv_ref, w1_ref, a1_ref, g1_ref, w2_ref, a2_ref, g2_ref):
    d = xs[0].shape[1]
    xr, xw, xk, xv, xa, xg = [z.astype(BF16) for z in xs]
    r = jnp.dot(xr, wrkv_ref[:, 0:d], preferred_element_type=F32)
    k = jnp.dot(xk, wrkv_ref[:, d:2 * d], preferred_element_type=F32)
    v = jnp.dot(xv, wrkv_ref[:, 2 * d:3 * d], preferred_element_type=F32)
    tw2 = jnp.tanh(jnp.dot(xw, w1_ref[...], preferred_element_type=F32))
    ta = jnp.dot(xa, a1_ref[...], preferred_element_type=F32)
    sg = _sigmoid(jnp.dot(xg, g1_ref[...], preferred_element_type=F32))
    lw, la, g = _dot(tw2, w2_ref[...]), _dot(ta, a2_ref[...]), _dot(sg, g2_ref[...])
    return (xr, xw, xk, xv, xa, xg), r, k, v, tw2, ta, sg, lw, la, g


def _to_heads(ref, val, nh):
    for hd in range(nh):
        ref[0, hd] = val[:, hd * HEAD:(hd + 1) * HEAD]


def _from_heads(ref, scr, nh):
    for hd in range(nh):
        scr[:, hd * HEAD:(hd + 1) * HEAD] = ref[0, hd]
    return scr[...]


def _rwkv_weight_specs(ws):
    return [_whole(w.shape) for w in ws]


def _rwkv_pre_fwd(x, mod, vec, e_ind, et_ind, weights):
    bl, t, d = x.shape
    tm = min(t, RWKV_TM)
    nh = d // HEAD
    hb = tm // 8

    def body(x_ref, halo_ref, mod_ref, vec_ref, e_ref, et_ref, wrkv, w1, a1, g1, w2, a2, g2,
             r_ref, ld_ref, k2_ref, v_ref, as_ref, bs_ref, g_ref):
        ti = pl.program_id(1)
        _, _, _, xs = _rwkv_pre_core(x_ref, halo_ref, mod_ref, vec_ref, ti)
        _, r, k, v, _, _, _, lw, la, g = _rwkv_proj(xs, wrkv, w1, a1, g1, w2, a2, g2)
        headsum = _make_headsum(e_ref[...], et_ref[...])
        ld, k2, a_s, b_s = _rwkv_elem(r, k, lw, la, vec_ref[6:7], vec_ref[7:8], vec_ref[8:9], vec_ref[9:10], headsum)
        g_ref[0] = g
        for ref, val in ((r_ref, r), (ld_ref, ld), (k2_ref, k2), (v_ref, v), (as_ref, a_s), (bs_ref, b_s)):
            _to_heads(ref, val, nh)

    halo = pl.BlockSpec((1, 8, d), lambda e, i: (e, jnp.maximum(i * hb - 1, 0), 0))
    hs = _sds((bl, nh, t, HEAD))
    return pl.pallas_call(
        body, name="rwkv_pre_fwd", grid=(bl, t // tm),
        in_specs=[_tok(tm, d), halo, _per_example(8, d), _whole(vec.shape), _whole(e_ind.shape), _whole(et_ind.shape)]
        + _rwkv_weight_specs(weights),
        out_specs=(_heads(nh, tm),) * 6 + (_tok(tm, d),),
        out_shape=(hs,) * 6 + (_sds(x.shape),),
        compiler_params=_cparams(("arbitrary", "arbitrary")),
    )(x, x, mod, vec, e_ind, et_ind, *weights)


def _rwkv_post_fwd(x, y, r, k2, v, g, mod, vec, e_ind, et_ind, w_out):
    bl, t, d = x.shape
    tm = min(t, RWKV_TM)
    nh = d // HEAD

    def body(x_ref, y_ref, r_ref, k2_ref, v_ref, g_ref, mod_ref, vec_ref, e_ref, et_ref, wout_ref,
             xo_ref, mix_ref, s0, s1, s2, s3):
        headsum = _make_headsum(e_ref[...], et_ref[...])
        yv, rv, kv, vv = (_from_heads(ref, scr, nh) for ref, scr in
                          ((y_ref, s0), (r_ref, s1), (k2_ref, s2), (v_ref, s3)))
        o = _rwkv_post(yv, rv, kv, vv, g_ref[0], vec_ref[10:11], vec_ref[11:12], vec_ref[12:13], headsum)
        mix = _dot(o, wout_ref[...])
        mix_ref[0] = mix
        xo_ref[0] = x_ref[0] + mod_ref[0, 2:3] * mix

    return pl.pallas_call(
        body, name="rwkv_post_fwd", grid=(bl, t // tm),
        in_specs=[_tok(tm, d)] + [_heads(nh, tm)] * 4 + [_tok(tm, d), _per_example(8, d), _whole(vec.shape),
                                                         _whole(e_ind.shape), _whole(et_ind.shape), _whole(w_out.shape)],
        out_specs=(_tok(tm, d), _tok(tm, d)),
        out_shape=(_sds(x.shape), _sds(x.shape)),
        scratch_shapes=[pltpu.VMEM((tm, d), F32)] * 4,
        compiler_params=_cparams(("arbitrary", "arbitrary")),
    )(x, y, r, k2, v, g, mod, vec, e_ind, et_ind, w_out)


def _rwkv_post_bwd(dxo, mix, y, r, k2, v, g, mod, vec, e_ind, et_ind, w_out, scatter=()):
    bl, t, d = dxo.shape
    tm = min(t, RWKV_TM)
    nh = d // HEAD

    def body(dxo_ref, mix_ref, y_ref, r_ref, k2_ref, v_ref, g_ref, mod_ref, vec_ref, e_ref, et_ref, wout_ref,
             dy_ref, dr_ref, dk2_ref, dv_ref, dg_ref, o_ref, dmix_ref, dgate_ref, small_ref, s0, s1, s2, s3):
        e, ti = pl.program_id(0), pl.program_id(1)
        headsum = _make_headsum(e_ref[...], et_ref[...])
        yv, rv, kv, vv = (_from_heads(ref, scr, nh) for ref, scr in
                          ((y_ref, s0), (r_ref, s1), (k2_ref, s2), (v_ref, s3)))
        dxo = dxo_ref[0]
        dmix = mod_ref[0, 2:3] * dxo
        dmix_ref[0] = dmix.astype(BF16)
        do = _dot_nt(dmix, wout_ref[...])
        post = functools.partial(_rwkv_post, headsum=headsum)
        o, vjp = jax.vjp(post, yv, rv, kv, vv, g_ref[0], vec_ref[10:11], vec_ref[11:12], vec_ref[12:13])
        o_ref[0] = o.astype(BF16)
        dy, dr, dk2, dv, dg, dlng, dlnb, drk = vjp(do)
        _to_heads(dy_ref, dy, nh)
        dr_ref[0], dk2_ref[0], dv_ref[0], dg_ref[0] = dr, dk2, dv, dg
        zero = jnp.zeros((1, d), F32)
        _add_rows(dgate_ref, ti == 0, [zero, zero, _colsum(dxo * mix_ref[0])])

        @pl.when(_first(e, ti))
        def _():
            small_ref[...] = jnp.zeros_like(small_ref)

        small_ref[0:1] += dlng
        small_ref[1:2] += dlnb
        small_ref[2:3] += drk

    return _call_with_scatter(
        body, name="rwkv_post_bwd", grid=(bl, t // tm),
        in_specs=[_tok(tm, d), _tok(tm, d)] + [_heads(nh, tm)] * 4
        + [_tok(tm, d), _per_example(8, d), _whole(vec.shape), _whole(e_ind.shape), _whole(et_ind.shape),
           _whole(w_out.shape)],
        out_specs=(_heads(nh, tm),) + (_tok(tm, d),) * 6 + (_per_example(8, d), _whole((8, d))),
        out_shape=(_sds((bl, nh, t, HEAD)),) + (_sds(dxo.shape),) * 4 + (_sds(dxo.shape, BF16),) * 2
        + (_sds((bl, 8, d)), _sds((8, d))),
        scratch_shapes=[pltpu.VMEM((tm, d), F32)] * 4,
        operands=(dxo, mix, y, r, k2, v, g, mod, vec, e_ind, et_ind, w_out), scatter=scatter)


RWKV_BWD_TM = 128


def _rwkv_pre_bwd(x, mod, vec, e_ind, et_ind, weights, dr_p, dk2_p, dv_p, dg, dr_s, dld, dk2_s, dv_s, das, dbs):
    bl, t, d = x.shape
    tm = min(t, RWKV_BWD_TM)
    nh = d // HEAD
    hb = tm // 8
    lp, gp = LORA_PAD, GATE_PAD

    def body(x_ref, halo_ref, mod_ref, vec_ref, e_ref, et_ref, wrkv, w1, a1, g1, w2, a2, g2,
             drp_ref, dk2p_ref, dvp_ref, dg_ref, drs_ref, dld_ref, dk2s_ref, dvs_ref, das_ref, dbs_ref,
             dh_ref, dhp_ref, xr_ref, xw_ref, xk_ref, xv_ref, xa_ref, xg_ref, dr_ref, dk_ref, dv_ref,
             dtw_ref, dta_ref, dtg_ref, tw2_ref, ta_ref, sg_ref, dlw_ref, dla_ref, dgb_ref, small_ref,
             s0, s1, s2, s3, s4, s5):
        e, ti = pl.program_id(0), pl.program_id(1)
        _, _, xx, xs = _rwkv_pre_core(x_ref, halo_ref, mod_ref, vec_ref, ti)
        xb, r, k, v, tw2, ta, sg, lw, la, _ = _rwkv_proj(xs, wrkv, w1, a1, g1, w2, a2, g2)
        for ref, val in zip((xr_ref, xw_ref, xk_ref, xv_ref, xa_ref, xg_ref), xb):
            ref[0] = val
        headsum = _make_headsum(e_ref[...], et_ref[...])
        drs, dld, dk2s, dvs, das, dbs_ = (_from_heads(ref, scr, nh) for ref, scr in
                                          ((drs_ref, s0), (dld_ref, s1), (dk2s_ref, s2), (dvs_ref, s3),
                                           (das_ref, s4), (dbs_ref, s5)))
        elem = functools.partial(_rwkv_elem, r, headsum=headsum)
        _, vjp = jax.vjp(elem, k, lw, la, vec_ref[6:7], vec_ref[7:8], vec_ref[8:9], vec_ref[9:10])
        dk, dlw, dla, dw0, da0, dkk, dka = vjp((dld, dk2p_ref[0] + dk2s, das, dbs_))
        dr = drp_ref[0] + drs
        dv = dvp_ref[0] + dvs
        dgv = dg_ref[0]
        dtg = _dot_nt(dgv, g2[...]) * sg * (1.0 - sg)
        dtw = _dot_nt(dlw, w2[...]) * (1.0 - tw2 * tw2)
        dta = _dot_nt(dla, a2[...])
        dr_ref[0], dk_ref[0], dv_ref[0] = dr.astype(BF16), dk.astype(BF16), dv.astype(BF16)
        dtw_ref[0], dta_ref[0], dtg_ref[0] = dtw.astype(BF16), dta.astype(BF16), dtg.astype(BF16)
        tw2_ref[0], ta_ref[0], sg_ref[0] = tw2.astype(BF16), ta.astype(BF16), sg.astype(BF16)
        dlw_ref[0], dla_ref[0], dgb_ref[0] = dlw.astype(BF16), dla.astype(BF16), dgv.astype(BF16)
        dxs = (_dot_nt(dr, wrkv[:, 0:d]), _dot_nt(dtw, w1[...]), _dot_nt(dk, wrkv[:, d:2 * d]),
               _dot_nt(dv, wrkv[:, 2 * d:3 * d]), _dot_nt(dta, a1[...]), _dot_nt(dtg, g1[...]))

        @pl.when(_first(e, ti))
        def _():
            small_ref[...] = jnp.zeros_like(small_ref)

        total = jnp.zeros((tm, d), F32)
        dhp = jnp.zeros((tm, d), F32)
        for i, dxi in enumerate(dxs):
            total += dxi
            dhp += dxi * vec_ref[i:i + 1]
            small_ref[i:i + 1] += _colsum(dxi * xx)
        dh_ref[0], dhp_ref[0] = total - dhp, dhp
        small_ref[6:7] += dw0
        small_ref[7:8] += da0
        small_ref[8:9] += dkk
        small_ref[9:10] += dka

    halo = pl.BlockSpec((1, 8, d), lambda e, i: (e, jnp.maximum(i * hb - 1, 0), 0))
    tokd, tokl, tokg = _tok(tm, d), _tok(tm, lp), _tok(tm, gp)
    bf = lambda w: _sds((bl, t, w), BF16)
    return pl.pallas_call(
        body, name="rwkv_pre_bwd", grid=(bl, t // tm),
        in_specs=[tokd, halo, _per_example(8, d), _whole(vec.shape), _whole(e_ind.shape), _whole(et_ind.shape)]
        + _rwkv_weight_specs(weights) + [tokd] * 4 + [_heads(nh, tm)] * 6,
        out_specs=(tokd, tokd) + (tokd,) * 6 + (tokd,) * 3 + (tokl, tokl, tokg, tokl, tokl, tokg)
        + (tokd, tokd, tokd, _whole((N_VEC, d))),
        out_shape=(_sds(x.shape), _sds(x.shape)) + (bf(d),) * 9 + (bf(lp), bf(lp), bf(gp), bf(lp), bf(lp), bf(gp))
        + (bf(d), bf(d), bf(d), _sds((N_VEC, d))),
        scratch_shapes=[pltpu.VMEM((tm, d), F32)] * 6,
        compiler_params=_cparams(("arbitrary", "arbitrary")),
    )(x, x, mod, vec, e_ind, et_ind, *weights, dr_p, dk2_p, dv_p, dg, dr_s, dld, dk2_s, dv_s, das, dbs)


NORM_BWD_TM = 512
FINAL_TM = 1024


def _norm_bwd(x, dxo, dh, dhprev, mod, dgate):
    bl, t, d = x.shape
    tm = min(t, NORM_BWD_TM)
    hb = tm // 8
    last_blk = t // 8 - 1

    def body(x_ref, dxo_ref, dh_ref, dhp_ref, nxt_ref, mod_ref, dgate_ref, dx_ref, dmod_ref):
        ti = pl.program_id(1)
        xn, inv = _rms(x_ref[0])
        last = jnp.where(ti == t // tm - 1, 0.0, nxt_ref[0, 0:1])
        dh = dh_ref[0] + _shift_up(dhp_ref[0], last)
        dx_ref[0] = dxo_ref[0] + _rms_bwd(xn, inv, dh * (1.0 + mod_ref[0, 1:2]))

        @pl.when(ti == 0)
        def _():
            dmod_ref[0] = dgate_ref[0]

        dmod_ref[0, 0:1] += _colsum(dh)
        dmod_ref[0, 1:2] += _colsum(dh * xn)

    nxt = pl.BlockSpec((1, 8, d), lambda e, i: (e, jnp.minimum((i + 1) * hb, last_blk), 0))
    return pl.pallas_call(
        body, name="norm_bwd", grid=(bl, t // tm),
        in_specs=[_tok(tm, d)] * 4 + [nxt, _per_example(8, d), _per_example(8, d)],
        out_specs=(_tok(tm, d), _per_example(8, d)),
        out_shape=(_sds(x.shape), _sds((bl, 8, d))),
        compiler_params=_cparams(("arbitrary", "arbitrary")),
    )(x, dxo, dh, dhprev, dhprev, mod, dgate)


def _final(x, target, final_g):
    bl, t, d = x.shape
    tm = min(t, FINAL_TM)

    def body(x_ref, tgt_ref, g_ref, dx_ref, loss_ref, dg_ref):
        e, ti = pl.program_id(0), pl.program_id(1)

        @pl.when(_first(e, ti))
        def _():
            loss_ref[...] = jnp.zeros_like(loss_ref)
            dg_ref[...] = jnp.zeros_like(dg_ref)

        xn, inv = _rms(x_ref[0])
        err = xn * g_ref[...] - tgt_ref[0]
        loss_ref[...] += (0.5 / d) * jnp.sum(err * err)
        dy = err * (1.0 / d)
        dg_ref[0:1] += _colsum(dy * xn)
        dx_ref[0] = _rms_bwd(xn, inv, dy * g_ref[...])

    return pl.pallas_call(
        body, name="final_loss", grid=(bl, t // tm),
        in_specs=[_tok(tm, d), _tok(tm, d), _whole(final_g.shape)],
        out_specs=(_tok(tm, d), _whole((8, 128)), _whole((8, d))),
        out_shape=(_sds(x.shape), _sds((8, 128)), _sds((8, d))),
        compiler_params=_cparams(("arbitrary", "arbitrary")),
    )(x, target, final_g)


def _adamw_math(w, g, m, v):
    m = ADAM_B1 * m + (1.0 - ADAM_B1) * g
    v = ADAM_B2 * v + (1.0 - ADAM_B2) * jnp.square(g)
    m_hat = m / (1.0 - ADAM_B1 ** ADAM_STEP)
    v_hat = v / (1.0 - ADAM_B2 ** ADAM_STEP)
    return -ADAM_LR * (m_hat / (jnp.sqrt(v_hat) + ADAM_EPS) + ADAM_WD * w), m, v


def _sum_parts(ref, n):
    g = ref[0].astype(F32)
    for s in range(1, n):
        g = g + ref[s].astype(F32)
    return g


def _adamw_layers(w, m, v, parts, name):
    nl, rows, c = w.shape
    tr = min(rows, 256)

    def body(w_ref, m_ref, v_ref, *refs):
        p_refs, (g_ref, d_ref, mo_ref, vo_ref) = refs[:nl], refs[nl:]
        for layer in range(nl):
            @pl.when(pl.program_id(0) == layer)
            def _(p_ref=p_refs[layer]):
                g = _sum_parts(p_ref, p_ref.shape[0])
                g_ref[...] = g
                d_ref[...], mo_ref[...], vo_ref[...] = _adamw_math(w_ref[...], g, m_ref[...], v_ref[...])

    row = pl.BlockSpec((None, tr, c), lambda l, i: (l, i, 0))
    return pl.pallas_call(
        body, name=name, grid=(nl, rows // tr),
        in_specs=[row, row, row] + [pl.BlockSpec((p.shape[0], tr, c), lambda l, i, k=k: (0, jnp.where(l == k, i, 0), 0))
                                    for k, p in enumerate(parts)],
        out_specs=(row,) * 4, out_shape=(_sds(w.shape),) * 4,
        compiler_params=_cparams(("arbitrary", "arbitrary")),
    )(w, m, v, *parts)


def _adamw_small(items, name):
    k = len(items)
    ns = [it[3].shape[0] for it in items]

    def body(*refs):
        ins, outs = refs[:4 * k], refs[4 * k:]
        for i in range(k):
            w_ref, m_ref, v_ref, p_ref = ins[4 * i:4 * i + 4]
            g = _sum_parts(p_ref, ns[i])
            outs[4 * i][...] = g
            outs[4 * i + 1][...], outs[4 * i + 2][...], outs[4 * i + 3][...] = _adamw_math(
                w_ref[...], g, m_ref[...], v_ref[...])

    flat = [a for it in items for a in it]
    res = pl.pallas_call(
        body, name=name,
        out_shape=tuple(_sds(it[0].shape) for it in items for _ in range(4)),
        compiler_params=_cparams(),
    )(*flat)
    return [tuple(res[4 * i:4 * i + 4]) for i in range(k)]


WEIGHTS = ['ada_w', 'ada_b', 'mlp_w1', 'mlp_w2', 'a_w_in', 'a_ln_g', 'a_ln_b', 'a_w_s', 'a_b_s', 'a_w_out', 'b_mu',
           'b_w_in', 'b_w0', 'b_w1', 'b_w2', 'b_a0', 'b_a1', 'b_a2', 'b_g1', 'b_g2', 'b_k_k', 'b_k_a', 'b_r_k',
           'b_ln_g', 'b_ln_b', 'b_w_out', 'final_g']
VECTORS = ['b_mu', 'b_w0', 'b_a0', 'b_k_k', 'b_k_a', 'b_ln_g', 'b_ln_b']
REPLICATED = ['a_ln_g', 'a_ln_b', 'a_w_s', 'a_b_s', 'b_r_k', 'final_g']
ROW_ALIGN = 16


def _pad_rows(a, mult):
    pad = (-a.shape[-2]) % mult
    return jnp.pad(a, [(0, 0)] * (a.ndim - 2) + [(0, pad), (0, 0)]) if pad else a


def _as2d(a):
    if a.ndim == 1:
        return a.reshape(1, -1)
    lead = 1
    for s in a.shape[:-1]:
        lead *= s
    return a.reshape(lead, a.shape[-1])


def kernel(x, c, ada_w, ada_b, mlp_w1, mlp_w2, a_w_in, a_ln_g, a_ln_b, a_w_s, a_b_s, a_w_out, b_mu, b_w_in, b_w0, b_w1, b_w2, b_a0, b_a1, b_a2, b_g1, b_g2, b_k_k, b_k_a, b_r_k, b_ln_g, b_ln_b, b_w_out, final_g, loss_target, m_ada_w, m_ada_b, m_mlp_w1, m_mlp_w2, m_a_w_in, m_a_ln_g, m_a_ln_b, m_a_w_s, m_a_b_s, m_a_w_out, m_b_mu, m_b_w_in, m_b_w0, m_b_w1, m_b_w2, m_b_a0, m_b_a1, m_b_a2, m_b_g1, m_b_g2, m_b_k_k, m_b_k_a, m_b_r_k, m_b_ln_g, m_b_ln_b, m_b_w_out, m_final_g, v_ada_w, v_ada_b, v_mlp_w1, v_mlp_w2, v_a_w_in, v_a_ln_g, v_a_ln_b, v_a_w_s, v_a_b_s, v_a_w_out, v_b_mu, v_b_w_in, v_b_w0, v_b_w1, v_b_w2, v_b_a0, v_b_a1, v_b_a2, v_b_g1, v_b_g2, v_b_k_k, v_b_k_a, v_b_r_k, v_b_ln_g, v_b_ln_b, v_b_w_out, v_final_g):
    given = dict(locals())
    w = {n: given[n] for n in WEIGHTS}
    bl, t, d = x.shape
    nl = ada_w.shape[0]
    nb = N_DEV * bl
    m_tok = bl * t
    me = 4 * lax.axis_index("x") + 2 * lax.axis_index("y") + lax.axis_index("c")

    bf = lambda a: a.astype(BF16)
    vec_loc = _pad_rows(jnp.concatenate([_as2d(w[n]) for n in VECTORS], axis=0), ROW_ALIGN)
    g_c, g_vec, g_a_in, g_a_out = _gather_call((c, vec_loc, bf(a_w_in[0]), bf(a_w_out[0])), "gather_first")

    c_all = g_c.reshape(nb, d)
    cols = ada_w.shape[2]
    ada_b_cols = lax.dynamic_slice(ada_b, (0, me * cols), (nl, cols)).reshape(nl, 1, cols)
    mod_cols = _ada_fwd(c_all, ada_w, ada_b_cols)
    mod_full = jnp.moveaxis(_gather_call((mod_cols,), "gather_mod")[0], 0, 2).reshape(nl, nb, 6 * d)
    mod_mine = lax.dynamic_slice(mod_full, (0, me * bl, 0), (nl, bl, 6 * d)).reshape(nl, bl, 6, d)
    mod_mix = jnp.pad(mod_mine[:, :, 0:3], ((0, 0), (0, 0), (0, 5), (0, 0)))
    mod_mlp = jnp.pad(mod_mine[:, :, 3:6], ((0, 0), (0, 0), (0, 5), (0, 0)))

    def unshard(g, ax):
        g = jnp.moveaxis(g, 0, ax)
        return g.reshape(g.shape[:ax] + (g.shape[ax] * g.shape[ax + 1],) + g.shape[ax + 2:])

    lora_names = ['b_w1', 'b_a1', 'b_g1', 'b_w2', 'b_a2', 'b_g2']
    lora_pack = jnp.concatenate([bf(w[n]).reshape(-1) for n in lora_names]).reshape(-1, 128)
    full = {'a_w_in': unshard(g_a_in, 1), 'a_w_out': unshard(g_a_out, 0)}
    n_vec_rows = sum(_as2d(w[n]).shape[0] for n in VECTORS)
    vec = jnp.moveaxis(g_vec, 0, 1).reshape(N_VEC, d)
    vec = vec.at[n_vec_rows].set(b_r_k.reshape(d))

    e_ind, et_ind = _head_indicators(d)
    gd = d // SGU_GROUPS
    group_ind = (jnp.arange(SGU_GROUPS)[:, None] == jnp.arange(d)[None, :] // gd).astype(BF16)
    bias_full = jnp.repeat(a_b_s[0].T, gd, axis=1)
    pad_c = lambda a, n: jnp.pad(a, ((0, 0), (0, n - a.shape[1])))
    pad_r = lambda a, n: jnp.pad(a, ((0, n - a.shape[0]), (0, 0)))
    sgu_args = (full['a_w_in'], a_ln_g, a_ln_b, a_w_s[0], bias_full, full['a_w_out'])

    x0 = x
    (x1, mix_a), (g_w1_0, g_w2_0) = _sgu_fwd(x0, mod_mix[0], *sgu_args, gather=(bf(mlp_w1[0]), bf(mlp_w2[0])))
    w1_full = [unshard(g_w1_0, 1), None]
    w2_full = [unshard(g_w2_0, 0), None]
    (x2, ff0, q0), (g_w1_1, g_w2_1, g_b_in, g_b_out, g_lora) = _mlp_fwd(
        x1, mod_mlp[0], w1_full[0], w2_full[0],
        gather=(bf(mlp_w1[1]), bf(mlp_w2[1]), bf(b_w_in[0]), bf(b_w_out[0]), lora_pack))
    w1_full[1], w2_full[1] = unshard(g_w1_1, 1), unshard(g_w2_1, 0)
    full['b_w_in'], full['b_w_out'] = unshard(g_b_in, 1), unshard(g_b_out, 0)
    lora_flat, lo = g_lora.reshape(N_DEV, -1), 0
    for n, ax in zip(lora_names, (0, 0, 0, 1, 1, 1)):
        loc = w[n].shape[1:]
        full[n] = unshard(lora_flat[:, lo:lo + w[n].size].reshape((N_DEV,) + loc), ax)
        lo += w[n].size
    rwkv_w = (full['b_w_in'], pad_c(full['b_w1'], LORA_PAD), pad_c(full['b_a1'], LORA_PAD),
              pad_c(full['b_g1'], GATE_PAD), pad_r(full['b_w2'], LORA_PAD), pad_r(full['b_a2'], LORA_PAD),
              pad_r(full['b_g2'], GATE_PAD))
    r, ld, k2, v, a_s, b_s, gate = _rwkv_pre_fwd(x2, mod_mix[1], vec, e_ind, et_ind, rwkv_w)
    y, s0, tinv = _wkv_fwd(r, ld, k2, v, a_s, b_s)
    x3, mix_b = _rwkv_post_fwd(x2, y, r, k2, v, gate, mod_mix[1], vec, e_ind, et_ind, full['b_w_out'])
    (x4, ff1, q1), _ = _mlp_fwd(x3, mod_mlp[1], w1_full[1], w2_full[1])
    dx4, loss_blk, dfinal = _final(x4, loss_target, final_g.reshape(1, d))
    loss = lax.psum(loss_blk[0, 0], ("x", "y", "c"))

    tok = lambda a: a.reshape(m_tok, a.shape[-1])
    shard_rows = lambda g: g.reshape((N_DEV, g.shape[0] // N_DEV) + g.shape[1:])
    (dx3, dmod_mlp1, h_b, dff_b, dp_b), _ = _mlp_bwd(x3, dx4, ff1, q1, mod_mlp[1], w1_full[1], w2_full[1])
    gw1_1 = _matmul_tn(tok(h_b), tok(dp_b), "grad_mlp_w1_l1", col_shards=N_DEV)
    gw2_1 = shard_rows(_matmul_tn(tok(q1), tok(dff_b), "grad_mlp_w2_l1"))
    (dy, dr_p, dk2_p, dv_p, dgate_act, o_b, dmix_b, dgate_b, small_post), (rw1_1, rw2_1) = _rwkv_post_bwd(
        dx3, mix_b, y, r, k2, v, gate, mod_mix[1], vec, e_ind, et_ind, full['b_w_out'], scatter=(gw1_1, gw2_1))
    g_b_w_out = shard_rows(_matmul_tn(tok(o_b), tok(dmix_b), "grad_b_w_out"))
    dr_s, dld, dk2_s, dv_s, das, dbs = _wkv_bwd(r, ld, k2, v, a_s, b_s, s0, tinv, dy)
    (dh, dhp, xr_b, xw_b, xk_b, xv_b, xa_b, xg_b, dr_b, dk_b, dv_b, dtw_b, dta_b, dtg_b, tw2_b, ta_b, sg_b,
     dlw_b, dla_b, dg_b, small_pre) = _rwkv_pre_bwd(x2, mod_mix[1], vec, e_ind, et_ind, rwkv_w,
                                                    dr_p, dk2_p, dv_p, dgate_act, dr_s, dld, dk2_s, dv_s, das, dbs)
    g_b_w_in = jnp.concatenate([_matmul_tn(tok(xr_b), tok(dr_b), "grad_b_w_r"),
                                _matmul_tn(tok(xk_b), tok(dk_b), "grad_b_w_k"),
                                _matmul_tn(tok(xv_b), tok(dv_b), "grad_b_w_v")], axis=1)
    shard_cols = lambda g: jnp.moveaxis(g.reshape(g.shape[0], N_DEV, g.shape[1] // N_DEV), 1, 0)
    g_b_w_in = shard_cols(g_b_w_in)
    lw_, lg_ = b_w1.shape[2], b_g1.shape[2]
    small_names = ['b_w1', 'b_a1', 'b_g1', 'b_w2', 'b_a2', 'b_g2'] + VECTORS
    small_parts = [
        shard_rows(_matmul_tn(tok(xw_b), tok(dtw_b), "grad_b_w1")[:, :lw_]),
        shard_rows(_matmul_tn(tok(xa_b), tok(dta_b), "grad_b_a1")[:, :lw_]),
        shard_rows(_matmul_tn(tok(xg_b), tok(dtg_b), "grad_b_g1")[:, :lg_]),
        shard_cols(_matmul_tn(tok(tw2_b), tok(dlw_b), "grad_b_w2")[:lw_]),
        shard_cols(_matmul_tn(tok(ta_b), tok(dla_b), "grad_b_a2")[:lw_]),
        shard_cols(_matmul_tn(tok(sg_b), tok(dg_b), "grad_b_g2")[:lg_]),
        shard_cols(jnp.concatenate([small_pre[0:10], small_post[0:2]], axis=0).astype(BF16)),
    ]
    small_flat = jnp.concatenate([p.reshape(N_DEV, -1) for p in small_parts], axis=1)
    lane = 128
    small_rows = -(-small_flat.shape[1] // (lane * ROW_ALIGN)) * ROW_ALIGN
    small_pack = jnp.pad(small_flat, ((0, 0), (0, small_rows * lane - small_flat.shape[1]))).reshape(
        N_DEV, small_rows, lane)
    dx2, dmod_mix1 = _norm_bwd(x2, dx3, dh, dhp, mod_mix[1], dgate_b)
    (dx1, dmod_mlp0, h_b, dff_b, dp_b), (r_b_w_in, r_b_w_out, r_small) = _mlp_bwd(
        x1, dx2, ff0, q0, mod_mlp[0], w1_full[0], w2_full[0], scatter=(g_b_w_in, g_b_w_out, small_pack))
    gw1_0 = _matmul_tn(tok(h_b), tok(dp_b), "grad_mlp_w1_l0", col_shards=N_DEV)
    gw2_0 = shard_rows(_matmul_tn(tok(q0), tok(dff_b), "grad_mlp_w2_l0"))
    (dx0, dmod_mix0, h_b, dpre_b, z_b, dmix_b, small_sgu, d_ws, d_bs), (rw1_0, rw2_0) = _sgu_bwd(
        x0, dx1, mix_a, mod_mix[0], *sgu_args, group_ind, scatter=(gw1_0, gw2_0))
    dmod_mine = jnp.stack([jnp.concatenate([dmod_mix0[:, 0:3], dmod_mlp0[:, 0:3]], axis=1),
                           jnp.concatenate([dmod_mix1[:, 0:3], dmod_mlp1[:, 0:3]], axis=1)], axis=1)
    rep_g = {'a_ln_g': small_sgu[0:1], 'a_ln_b': small_sgu[1:2], 'a_w_s': d_ws.reshape(-1, d), 'a_b_s': d_bs.reshape(1, d),
             'b_r_k': small_post[2:3], 'final_g': dfinal[0:1]}
    rep_rows = [rep_g[n].shape[0] for n in REPLICATED]
    rep_pack = _pad_rows(jnp.concatenate([rep_g[n] for n in REPLICATED], axis=0), 8)
    g_a_w_in, (dmod_all, rep_all) = _matmul_tn(tok(h_b), tok(dpre_b), "grad_a_w_in", col_shards=N_DEV,
                                               gather=(dmod_mine.reshape(bl, nl * 6 * d), rep_pack))
    g_a_w_out = shard_rows(_matmul_tn(tok(z_b), tok(dmix_b), "grad_a_w_out"))
    r_a_w_in, r_a_w_out = _scatter_call((g_a_w_in, g_a_w_out), "scatter_sgu_grads")

    dmod_all = jnp.moveaxis(dmod_all.reshape(nb, nl, 6 * d), 0, 1)
    dmod_cols = lax.dynamic_slice(dmod_all, (0, 0, me * cols), (nl, nb, cols))
    g_ada_w, g_ada_b = _ada_bwd(c_all, dmod_cols, dmod_all)

    mom = {n: given['m_' + n] for n in WEIGHTS}
    var = {n: given['v_' + n] for n in WEIGHTS}
    out = {}
    as3d = lambda a: a.reshape((-1,) + a.shape[-2:])
    for n, parts in (('mlp_w1', [rw1_0, rw1_1]), ('mlp_w2', [rw2_0, rw2_1]), ('a_w_in', [r_a_w_in]),
                     ('a_w_out', [r_a_w_out]), ('b_w_in', [r_b_w_in]), ('b_w_out', [r_b_w_out]),
                     ('ada_w', list(g_ada_w))):
        res = _adamw_layers(as3d(w[n]), as3d(mom[n]), as3d(var[n]), parts, "adamw_" + n)
        out[n] = tuple(a.reshape(w[n].shape) for a in res)

    items, names = [], []

    def add(n, part):
        s2 = _as2d(w[n]).shape
        items.append((_as2d(w[n]), _as2d(mom[n]), _as2d(var[n]), part.reshape((part.shape[0],) + s2)))
        names.append(n)

    sflat = r_small.reshape(N_DEV, -1)
    so = 0
    for n in small_names:
        sz = w[n].size
        add(n, sflat[:, so:so + sz])
        so += sz
    ro = 0
    for n, nr in zip(REPLICATED, rep_rows):
        add(n, rep_all[:, ro:ro + nr])
        ro += nr
    add('ada_b', g_ada_b[None])
    for n, res in zip(names, _adamw_small(items, "adamw_small")):
        out[n] = tuple(a.reshape(w[n].shape) for a in res)

    return (loss, dx0, *[out[n][0] for n in WEIGHTS], *[out[n][1] for n in WEIGHTS],
            *[out[n][2] for n in WEIGHTS], *[out[n][3] for n in WEIGHTS])
```

```python
import functools

import jax
import jax.numpy as jnp
from jax import lax
from jax.experimental import pallas as pl
from jax.experimental.pallas import tpu as pltpu

F32 = jnp.float32
BF16 = jnp.bfloat16

N_DEV = 8
RMS_EPS = 1e-6
LN_EPS = 1e-5
HEAD = 64
GN_EPS = HEAD * 1e-5
L2_EPS = 1e-12
DECAY_SCALE = 0.6065306597126334
SGU_CHUNK = 128
SGU_GROUPS = 8
WKV_CHUNK = 64
WKV_HEADS_PER_STEP = 16
WKV_EXAMPLES_PER_STEP = 2
LORA_PAD = 128
GATE_PAD = 256
ADAM_LR, ADAM_B1, ADAM_B2, ADAM_EPS, ADAM_WD, ADAM_STEP = 0.001, 0.9, 0.999, 1e-08, 0.01, 10
VMEM_LIMIT = 60 * 1024 * 1024


def _cparams(sem=None, **kw):
    if sem is not None:
        kw["dimension_semantics"] = sem
    return pltpu.CompilerParams(vmem_limit_bytes=VMEM_LIMIT, **kw)


def _dot(a, b):
    return jnp.dot(a.astype(BF16), b.astype(BF16), preferred_element_type=F32)


def _dot_nt(a, b):
    return lax.dot_general(a.astype(BF16), b.astype(BF16), (((1,), (1,)), ((), ())), preferred_element_type=F32)


def _dot_tn(a, b):
    return lax.dot_general(a.astype(BF16), b.astype(BF16), (((0,), (0,)), ((), ())), preferred_element_type=F32)


def _bdot(a, b, dims):
    return lax.dot_general(a.astype(BF16), b.astype(BF16), (dims, ((0,), (0,))), preferred_element_type=F32)


@jax.custom_vjp
def _tri_sum(tri, tri_t, x):
    hi = x.astype(BF16)
    lo = (x - hi.astype(F32)).astype(BF16)
    dn = (((2,), (1,)), ((0,), (0,)))
    return (lax.dot_general(tri, hi, dn, preferred_element_type=F32)
            + lax.dot_general(tri, lo, dn, preferred_element_type=F32))


_tri_sum.defvjp(lambda tri, tri_t, x: (_tri_sum(tri, tri_t, x), (tri, tri_t)),
                lambda res, g: (jnp.zeros_like(res[0]), jnp.zeros_like(res[1]), _tri_sum(res[1], res[0], g)))


@jax.custom_vjp
def _bmm_nn(a, b):
    return _bdot(a, b, ((2,), (1,)))


@jax.custom_vjp
def _bmm_nt(a, b):
    return _bdot(a, b, ((2,), (2,)))


@jax.custom_vjp
def _bmm_tn(a, b):
    return _bdot(a, b, ((1,), (1,)))


_bmm_nn.defvjp(lambda a, b: (_bmm_nn(a, b), (a, b)), lambda res, g: (_bmm_nt(g, res[1]), _bmm_tn(res[0], g)))
_bmm_nt.defvjp(lambda a, b: (_bmm_nt(a, b), (a, b)), lambda res, g: (_bmm_nn(g, res[1]), _bmm_tn(g, res[0])))
_bmm_tn.defvjp(lambda a, b: (_bmm_tn(a, b), (a, b)), lambda res, g: (_bmm_nt(res[1], g), _bmm_nn(res[0], g)))


def _tri_inverse(p):
    n = p.shape[1]
    row = lax.broadcasted_iota(jnp.int32, (n, n), 0)
    col = lax.broadcasted_iota(jnp.int32, (n, n), 1)
    tinv = jnp.where(row == col, 1.0, 0.0).astype(F32)[None] + p
    for _ in range(max(1, (n - 1).bit_length()) - 1):
        p = _bmm_nn(p, p)
        tinv = tinv + _bmm_nn(tinv, p)
    return tinv.astype(BF16)


def _tri_solve_fwd(tinv, p, rhs):
    u = _bmm_nn(tinv, rhs)
    return u, (tinv, u)


def _tri_solve_bwd(res, du):
    tinv, u = res
    drhs = _bmm_tn(tinv, du)
    return jnp.zeros_like(tinv), _bmm_nt(drhs, u), drhs


@jax.custom_vjp
def _tri_solve(tinv, p, rhs):
    return _tri_solve_fwd(tinv, p, rhs)[0]


_tri_solve.defvjp(_tri_solve_fwd, _tri_solve_bwd)


def _wkv_chunk(s0, r, ld, k, v, a, b, tinv=None):
    nh, n, _ = r.shape
    row = lax.broadcasted_iota(jnp.int32, (n, n), 0)
    col = lax.broadcasted_iota(jnp.int32, (n, n), 1)
    incl = row >= col
    strict = row > col
    lower = jnp.broadcast_to(jnp.where(incl, 1.0, 0.0).astype(BF16), (nh, n, n))
    upper = jnp.broadcast_to(jnp.where(row <= col, 1.0, 0.0).astype(BF16), (nh, n, n))
    c = _tri_sum(lower, upper, ld)
    c_end = c[:, n - 1:n, :]
    ec, enc, ecx, eend = jnp.exp(c), jnp.exp(-c), jnp.exp(c - ld), jnp.exp(c_end - c)
    ar = jnp.concatenate([a * ecx, r * ec], axis=1)
    mask = jnp.concatenate([strict, incl], axis=0)[None]
    m_b = jnp.where(mask, _bmm_nt(ar, b * enc), 0.0)
    m_k = jnp.where(mask, _bmm_nt(ar, k * enc), 0.0)
    a_ab, a_rb = m_b[:, :n], m_b[:, n:]
    base = _bmm_nt(ar, s0) + _bmm_nn(m_k, v)
    if tinv is None:
        tinv = lax.stop_gradient(_tri_inverse(a_ab))
    u = _tri_solve(tinv, a_ab, base[:, :n])
    y = base[:, n:] + _bmm_nn(a_rb, u)
    s1 = s0 * jnp.exp(c_end) + _bmm_tn(jnp.concatenate([u, v], axis=1), jnp.concatenate([b * eend, k * eend], axis=1))
    return y, s1, tinv


def _wkv_specs(bl, nh, t):
    eb, hb, lc = min(bl, WKV_EXAMPLES_PER_STEP), min(nh, WKV_HEADS_PER_STEP), WKV_CHUNK
    return eb, hb, lc, (bl // eb, nh // hb, t // lc)


def _wkv_fwd(r, ld, k, v, a, b):
    bl, nh, t, n = r.shape
    eb, hb, lc, grid = _wkv_specs(bl, nh, t)
    nc = t // lc
    nb = eb * hb

    def body(r_ref, ld_ref, k_ref, v_ref, a_ref, b_ref, y_ref, s0_ref, tinv_ref, s_scr):
        @pl.when(pl.program_id(2) == 0)
        def _():
            s_scr[...] = jnp.zeros_like(s_scr)

        s0 = s_scr[...]
        s0_ref[:, :, 0] = s0.reshape(eb, hb, n, n)
        y, s1, tinv = _wkv_chunk(
            s0, *(ref[...].reshape(nb, lc, n) for ref in (r_ref, ld_ref, k_ref, v_ref, a_ref, b_ref)))
        y_ref[...] = y.reshape(eb, hb, lc, n)
        tinv_ref[:, :, 0] = tinv.reshape(eb, hb, lc, lc)
        s_scr[...] = s1

    seq = pl.BlockSpec((eb, hb, lc, n), lambda e, h, c: (e, h, c, 0))
    return pl.pallas_call(
        body, name="wkv_fwd", grid=grid,
        in_specs=[seq] * 6,
        out_specs=(seq, pl.BlockSpec((eb, hb, 1, n, n), lambda e, h, c: (e, h, c, 0, 0)),
                   pl.BlockSpec((eb, hb, 1, lc, lc), lambda e, h, c: (e, h, c, 0, 0))),
        out_shape=(jax.ShapeDtypeStruct((bl, nh, t, n), F32), jax.ShapeDtypeStruct((bl, nh, nc, n, n), F32),
                   jax.ShapeDtypeStruct((bl, nh, nc, lc, lc), BF16)),
        scratch_shapes=[pltpu.VMEM((nb, n, n), F32)],
        compiler_params=_cparams(("arbitrary", "arbitrary", "arbitrary")),
    )(r, ld, k, v, a, b)


def _wkv_bwd(r, ld, k, v, a, b, s0_all, tinv_all, dy):
    bl, nh, t, n = r.shape
    eb, hb, lc, grid = _wkv_specs(bl, nh, t)
    nc = t // lc
    nb = eb * hb

    def body(r_ref, ld_ref, k_ref, v_ref, a_ref, b_ref, s0_ref, tinv_ref, dy_ref,
             dr_ref, dld_ref, dk_ref, dv_ref, da_ref, db_ref, ds_scr):
        @pl.when(pl.program_id(2) == 0)
        def _():
            ds_scr[...] = jnp.zeros_like(ds_scr)

        args = (s0_ref[:, :, 0].reshape(nb, n, n),) + tuple(
            ref[...].reshape(nb, lc, n) for ref in (r_ref, ld_ref, k_ref, v_ref, a_ref, b_ref))
        tinv = tinv_ref[:, :, 0].reshape(nb, lc, lc)
        _, vjp = jax.vjp(lambda *xs: _wkv_chunk(*xs, tinv=tinv)[:2], *args)
        ds0, *dseq = vjp((dy_ref[...].reshape(nb, lc, n), ds_scr[...]))
        ds_scr[...] = ds0
        for ref, val in zip((dr_ref, dld_ref, dk_ref, dv_ref, da_ref, db_ref), dseq):
            ref[...] = val.reshape(eb, hb, lc, n)

    seq = pl.BlockSpec((eb, hb, lc, n), lambda e, h, c: (e, h, nc - 1 - c, 0))
    st = pl.BlockSpec((eb, hb, 1, n, n), lambda e, h, c: (e, h, nc - 1 - c, 0, 0))
    ti = pl.BlockSpec((eb, hb, 1, lc, lc), lambda e, h, c: (e, h, nc - 1 - c, 0, 0))
    out = jax.ShapeDtypeStruct((bl, nh, t, n), F32)
    return pl.pallas_call(
        body, name="wkv_bwd", grid=grid,
        in_specs=[seq] * 6 + [st, ti, seq],
        out_specs=(seq,) * 6, out_shape=(out,) * 6,
        scratch_shapes=[pltpu.VMEM((nb, n, n), F32)],
        compiler_params=_cparams(("arbitrary", "arbitrary", "arbitrary")),
    )(r, ld, k, v, a, b, s0_all, tinv_all, dy)


def _scatter_copies(x_refs, o_refs, send_sems, recv_sems, local_sems):
    pos = (lax.axis_index("x"), lax.axis_index("y"), lax.axis_index("c"))
    me = 4 * pos[0] + 2 * pos[1] + pos[2]

    def descriptors():
        sends, arrivals, local = [], [], []
        for i, (x_ref, o_ref) in enumerate(zip(x_refs, o_refs)):
            for m in range(1, N_DEV):
                p = tuple(1 - pos[a] if (m >> (2 - a)) & 1 else pos[a] for a in range(3))
                pidx = 4 * p[0] + 2 * p[1] + p[2]
                k = (N_DEV - 1) * i + m - 1
                for dst, out in ((o_ref.at[me], sends), (o_ref.at[pidx], arrivals)):
                    out.append(pltpu.make_async_remote_copy(
                        src_ref=x_ref.at[pidx], dst_ref=dst, send_sem=send_sems.at[k], recv_sem=recv_sems.at[k],
                        device_id=p, device_id_type=pl.DeviceIdType.MESH))
            local.append(pltpu.make_async_copy(x_ref.at[me], o_ref.at[me], local_sems.at[i]))
        return sends, arrivals, local

    def start():
        sends, _, local = descriptors()
        for cp in local + sends:
            cp.start()

    def finish():
        sends, arrivals, local = descriptors()
        for cp in arrivals:
            cp.wait_recv()
        for cp in sends:
            cp.wait_send()
        for cp in local:
            cp.wait()

    return start, finish


def _gather_copies(x_refs, o_refs, send_sems, recv_sems, local_sems):
    pos = (lax.axis_index("x"), lax.axis_index("y"), lax.axis_index("c"))
    me = 4 * pos[0] + 2 * pos[1] + pos[2]
    far = (2, 4, 6)

    def peer_of(m):
        p = tuple(1 - pos[a] if (m >> (2 - a)) & 1 else pos[a] for a in range(3))
        return p, 4 * p[0] + 2 * p[1] + p[2]

    sibling, _ = peer_of(1)

    def copy(i, k, src, slot, to):
        return pltpu.make_async_remote_copy(
            src_ref=src, dst_ref=o_refs[i].at[slot], send_sem=send_sems.at[(N_DEV - 1) * i + k],
            recv_sem=recv_sems.at[(N_DEV - 1) * i + k], device_id=to, device_id_type=pl.DeviceIdType.MESH)

    def direct(i):
        return [copy(i, m - 1, x_refs[i], me, peer_of(m)[0]) for m in (1,) + far]

    def local(i):
        return pltpu.make_async_copy(x_refs[i], o_refs[i].at[me], local_sems.at[i])

    def start():
        for i in range(len(x_refs)):
            local(i).start()
            for cp in direct(i):
                cp.start()

    def relays():
        return [copy(i, m, o_refs[i].at[peer_of(m)[1]], peer_of(m)[1], sibling)
                for i in range(len(x_refs)) for m in far]

    def relay():
        for i in range(len(x_refs)):
            for m in far:
                copy(i, m - 1, x_refs[i], peer_of(m)[1], sibling).wait_recv()
        for fwd in relays():
            fwd.start()

    def finish():
        n = len(x_refs)
        for i in range(n):
            copy(i, 0, x_refs[i], peer_of(1)[1], sibling).wait_recv()
            for m in far:
                copy(i, m, x_refs[i], peer_of(m ^ 1)[1], sibling).wait_recv()
        for i in range(n):
            for cp in direct(i):
                cp.wait_send()
            local(i).wait()
        for fwd in relays():
            fwd.wait_send()

    return start, relay, finish


def _scatter_scratch(n):
    return [pltpu.SemaphoreType.DMA(((N_DEV - 1) * n,)), pltpu.SemaphoreType.DMA(((N_DEV - 1) * n,)),
            pltpu.SemaphoreType.DMA((n,))]


_ANY = pl.BlockSpec(memory_space=pl.ANY)


def _scatter_call(arrays, name):
    n = len(arrays)

    def body(*refs):
        start, finish = _scatter_copies(refs[:n], refs[n:2 * n], *refs[2 * n:])
        start()
        finish()

    return pl.pallas_call(
        body, name=name, in_specs=[_ANY] * n, out_specs=(_ANY,) * n,
        out_shape=tuple(_sds(a.shape, a.dtype) for a in arrays), scratch_shapes=_scatter_scratch(n),
    )(*arrays)


def _gather_call(arrays, name):
    n = len(arrays)

    def body(*refs):
        start, relay, finish = _gather_copies(refs[:n], refs[n:2 * n], *refs[2 * n:])
        start()
        relay()
        finish()

    return pl.pallas_call(
        body, name=name, in_specs=[_ANY] * n, out_specs=(_ANY,) * n,
        out_shape=tuple(_sds((N_DEV,) + a.shape, a.dtype) for a in arrays), scratch_shapes=_scatter_scratch(n),
    )(*arrays)


def _call_with_scatter(body, *, name, grid, in_specs, out_specs, out_shape, scratch_shapes, operands,
                       scatter=(), gather=()):
    assert not (scatter and gather)
    carried = tuple(scatter) or tuple(gather)
    copies = _scatter_copies if scatter else _gather_copies
    recv_shapes = tuple(_sds(a.shape if scatter else (N_DEV,) + a.shape, a.dtype) for a in carried)
    nc, n_in, n_out, n_scr = len(carried), len(in_specs), len(out_specs), len(scratch_shapes)
    if nc == 0:
        return pl.pallas_call(
            body, name=name, grid=grid, in_specs=list(in_specs), out_specs=tuple(out_specs),
            out_shape=tuple(out_shape), scratch_shapes=list(scratch_shapes),
            compiler_params=_cparams(("arbitrary",) * len(grid)))(*operands), ()

    def wrapped(*refs):
        ins, refs = refs[:n_in], refs[n_in:]
        c_in, refs = refs[:nc], refs[nc:]
        outs, refs = refs[:n_out], refs[n_out:]
        c_out, refs = refs[:nc], refs[nc:]
        scr, sems = refs[:n_scr], refs[n_scr:]
        ids = [pl.program_id(a) for a in range(len(grid))]
        first = functools.reduce(jnp.logical_and, [i == 0 for i in ids])
        last = functools.reduce(jnp.logical_and, [i == g - 1 for i, g in zip(ids, grid)])
        start, *relay, finish = copies(c_in, c_out, *sems)
        pl.when(first)(start)
        if relay:
            pl.when(last)(relay[0])
        body(*ins, *outs, *scr)
        pl.when(last)(finish)

    res = pl.pallas_call(
        wrapped, name=name, grid=grid,
        in_specs=list(in_specs) + [_ANY] * nc, out_specs=tuple(out_specs) + (_ANY,) * nc,
        out_shape=tuple(out_shape) + recv_shapes,
        scratch_shapes=list(scratch_shapes) + _scatter_scratch(nc),
        compiler_params=_cparams(("arbitrary",) * len(grid)),
    )(*operands, *carried)
    return res[:n_out], res[n_out:]


def _rms(x):
    inv = lax.rsqrt(jnp.mean(x * x, axis=-1, keepdims=True) + RMS_EPS)
    return x * inv, inv


def _rms_bwd(xn, inv, dxn):
    return inv * (dxn - xn * jnp.mean(dxn * xn, axis=-1, keepdims=True))


def _colsum(x):
    return jnp.sum(x, axis=0, keepdims=True)


def _sigmoid(x):
    return 0.5 * (jnp.tanh(0.5 * x) + 1.0)


def _split_bf16(x):
    hi = x.astype(BF16)
    return hi, (x - hi.astype(F32)).astype(BF16)


def _dot_split(x, e):
    hi, lo = _split_bf16(x)
    return jnp.dot(hi, e, preferred_element_type=F32) + jnp.dot(lo, e, preferred_element_type=F32)


@jax.custom_vjp
def _headsum(x, e, et):
    return _dot_split(_dot_split(x, e), et)


_headsum.defvjp(lambda x, e, et: (_headsum(x, e, et), (e, et)),
                lambda res, g: (_headsum(g, *res), jnp.zeros_like(res[0]), jnp.zeros_like(res[1])))


def _make_headsum(e, et):
    return lambda x: _headsum(x, e, et)


def _head_indicators(d):
    e = (jnp.arange(d)[:, None] // HEAD == jnp.arange(128)[None, :]).astype(BF16)
    return e, e.T


def _rwkv_elem(r, k, lw, la, w0, a0, k_k, k_a, headsum):
    ld = -DECAY_SCALE * _sigmoid(w0 + lw)
    a = _sigmoid(a0 + la)
    kkp = k * k_k
    kk = kkp * lax.rsqrt(jnp.maximum(headsum(kkp * kkp), L2_EPS * L2_EPS))
    k2 = k * (1.0 + (a - 1.0) * k_a)
    del r
    return ld, k2, -kk, kk * a


def _rwkv_post(y, r, k2, v, g, ln_g, ln_b, r_k, headsum):
    m = headsum(y) * (1.0 / HEAD)
    yc = y - m
    var = headsum(yc * yc) * (1.0 / HEAD)
    yn = yc * lax.rsqrt(var + GN_EPS)
    bonus = headsum(r * k2 * r_k) * v
    return (yn * ln_g + ln_b + bonus) * g


def _shift_down(h, first_row):
    rolled = pltpu.roll(h, 1, 0)
    row = lax.broadcasted_iota(jnp.int32, h.shape, 0)
    return jnp.where(row == 0, first_row, rolled)


def _shift_up(h, last_row):
    n = h.shape[0]
    rolled = pltpu.roll(h, n - 1, 0)
    row = lax.broadcasted_iota(jnp.int32, h.shape, 0)
    return jnp.where(row == n - 1, last_row, rolled)


def _gelu(p):
    return 0.5 * p * (1.0 + lax.erf(p * 0.7071067811865476))


def _gelu_grad(p):
    return 0.5 * (1.0 + lax.erf(p * 0.7071067811865476)) + p * jnp.exp(-0.5 * p * p) * 0.3989422804014327


def _tok(tm, d):
    return pl.BlockSpec((1, tm, d), lambda e, t, *_: (e, t, 0))


def _per_example(rows, d):
    return pl.BlockSpec((1, rows, d), lambda e, t, *_: (e, 0, 0))


def _whole(shape):
    nd = len(shape)
    return pl.BlockSpec(tuple(shape), lambda *_: (0,) * nd)


def _heads(nh, tm):
    return pl.BlockSpec((1, nh, tm, HEAD), lambda e, t, *_: (e, 0, t, 0))


def _sds(shape, dtype=F32):
    return jax.ShapeDtypeStruct(tuple(shape), dtype)


def _add_rows(ref, first, rows):
    @pl.when(first)
    def _():
        ref[0] = jnp.zeros(ref.shape[1:], ref.dtype)

    for i, r in enumerate(rows):
        ref[0, i:i + 1] += r


def _first(e, t):
    return jnp.logical_and(e == 0, t == 0)


def _ada_fwd(c_all, ada_w, ada_b_cols):
    nl, d, cols = ada_w.shape
    nb = c_all.shape[0]

    def body(c_ref, w_ref, b_ref, o_ref):
        c = c_ref[...]
        cond = c * _sigmoid(c)
        for i in range(nl):
            o_ref[i] = _dot(cond, w_ref[i]) + b_ref[i]

    return pl.pallas_call(
        body, name="ada_fwd", out_shape=_sds((nl, nb, cols)),
        compiler_params=_cparams(),
    )(c_all, ada_w, ada_b_cols)


def _ada_bwd(c_all, dmod_cols, dmod_full):
    nl, nb, cols = dmod_cols.shape
    d = c_all.shape[1]

    def body(c_ref, g_ref, f_ref, b_ref, *o_refs):
        c = c_ref[...]
        cond = c * _sigmoid(c)
        for i in range(nl):
            o_refs[i][0] = _dot_tn(cond, g_ref[i])
            b_ref[i:i + 1] = jnp.sum(f_ref[i], axis=0, keepdims=True)

    res = pl.pallas_call(
        body, name="ada_bwd", out_shape=(_sds((nl, dmod_full.shape[2])),) + (_sds((1, d, cols)),) * nl,
        compiler_params=_cparams(),
    )(c_all, dmod_cols, dmod_full)
    return res[1:], res[0]


def _matmul_tn(a, b, name, col_shards=None, gather=()):
    m, ka = a.shape
    n = b.shape[1]
    tm = min(m, 2048)
    tk = min(ka, 1024)
    tn = min(n, 1024)
    steps = m // tm
    if col_shards:
        cs = n // col_shards
        spt = tn // cs
        out_spec = pl.BlockSpec((spt, tk, cs), lambda i, j, s: (j, i, 0))
        out_shape = _sds((col_shards, ka, cs), BF16)
    else:
        out_spec = pl.BlockSpec((tk, tn), lambda i, j, s: (i, j))
        out_shape = _sds((ka, n), BF16)

    def body(a_ref, b_ref, o_ref, acc):
        s = pl.program_id(2)

        @pl.when(s == 0)
        def _():
            acc[...] = jnp.zeros_like(acc)

        acc[...] += _dot_tn(a_ref[...], b_ref[...])

        @pl.when(s == steps - 1)
        def _():
            if col_shards:
                for q in range(spt):
                    o_ref[q] = acc[:, q * cs:(q + 1) * cs].astype(BF16)
            else:
                o_ref[...] = acc[...].astype(BF16)

    res, got = _call_with_scatter(
        body, name=name, grid=(ka // tk, n // tn, steps),
        in_specs=[pl.BlockSpec((tm, tk), lambda i, j, s: (s, i)), pl.BlockSpec((tm, tn), lambda i, j, s: (s, j))],
        out_specs=(out_spec,), out_shape=(out_shape,),
        scratch_shapes=[pltpu.VMEM((tk, tn), F32)], operands=(a, b), gather=gather)
    return (res[0], got) if gather else res[0]


MLP_FWD_TM = 1024
MLP_FJ = 1024
MLP_BWD_TM = 512
MLP_BWD_FJ = 1024


def _mlp_fwd(x, mod, w1, w2, gather=()):
    bl, t, d = x.shape
    f = w1.shape[1]
    tm, fj = min(t, MLP_FWD_TM), min(f, MLP_FJ)
    nj = f // fj

    def body(x_ref, mod_ref, w1_ref, w2_ref, xo_ref, ff_ref, q_ref, h_scr, acc):
        j = pl.program_id(2)

        @pl.when(j == 0)
        def _():
            xn, _ = _rms(x_ref[0])
            h_scr[...] = (xn * (1.0 + mod_ref[0, 1:2]) + mod_ref[0, 0:1]).astype(BF16)
            acc[...] = jnp.zeros_like(acc)

        p = jnp.dot(h_scr[...], w1_ref[...], preferred_element_type=F32)
        q = jnp.square(jnp.maximum(p, 0.0)).astype(BF16)
        q_ref[0] = q
        acc[...] += jnp.dot(q, w2_ref[...], preferred_element_type=F32)

        @pl.when(j == nj - 1)
        def _():
            ff_ref[0] = acc[...]
            xo_ref[0] = x_ref[0] + mod_ref[0, 2:3] * acc[...]

    return _call_with_scatter(
        body, name="mlp_fwd", grid=(bl, t // tm, nj),
        in_specs=[_tok(tm, d), _per_example(8, d),
                  pl.BlockSpec((d, fj), lambda e, i, j: (0, j)), pl.BlockSpec((fj, d), lambda e, i, j: (j, 0))],
        out_specs=(_tok(tm, d), _tok(tm, d), pl.BlockSpec((1, tm, fj), lambda e, i, j: (e, i, j))),
        out_shape=(_sds(x.shape), _sds(x.shape), _sds((bl, t, f), BF16)),
        scratch_shapes=[pltpu.VMEM((tm, d), BF16), pltpu.VMEM((tm, d), F32)],
        operands=(x, mod, w1, w2), gather=gather)


def _mlp_bwd(x, dxo, ff, q, mod, w1, w2, scatter=(), norm=None):
    bl, t, d = x.shape
    f = w1.shape[1]
    tm, fj = min(t, MLP_BWD_TM), min(f, MLP_BWD_FJ)
    nj = f // fj
    nt = t // tm
    n_in = 7

    def body(*refs):
        x_ref, dxo_ref, ff_ref, q_ref, mod_ref, w1_ref, w2_ref = refs[:n_in]
        refs = refs[n_in:]
        if norm is not None:
            (xm_ref, dhm_ref, dhp_ref, nxt_ref, modm_ref, dgate_ref), refs = refs[:6], refs[6:]
            dx_ref, dmod_ref, h_ref, dff_ref, dp_ref, dmodm_ref, acc, dxo_scr = refs
        else:
            dx_ref, dmod_ref, h_ref, dff_ref, dp_ref, acc = refs
        ti, j = pl.program_id(1), pl.program_id(2)
        dxo = (lambda: dxo_scr[...]) if norm is not None else (lambda: dxo_ref[0])

        @pl.when(j == 0)
        def _():
            if norm is not None:
                xn_m, inv_m = _rms(xm_ref[0])
                last = jnp.where(ti == nt - 1, 0.0, nxt_ref[0, 0:1])
                dh_m = dhm_ref[0] + _shift_up(dhp_ref[0], last)
                dxo_scr[...] = dxo_ref[0] + _rms_bwd(xn_m, inv_m, dh_m * (1.0 + modm_ref[0, 1:2]))

                @pl.when(ti == 0)
                def _():
                    dmodm_ref[0] = dgate_ref[0]

                dmodm_ref[0, 0:1] += _colsum(dh_m)
                dmodm_ref[0, 1:2] += _colsum(dh_m * xn_m)
            xn, _ = _rms(x_ref[0])
            h_ref[0] = (xn * (1.0 + mod_ref[0, 1:2]) + mod_ref[0, 0:1]).astype(BF16)
            dff_ref[0] = (mod_ref[0, 2:3] * dxo()).astype(BF16)
            acc[...] = jnp.zeros_like(acc)

        rl = jnp.sqrt(q_ref[0].astype(F32))
        dp = (_dot_nt(dff_ref[0], w2_ref[...]) * (2.0 * rl)).astype(BF16)
        dp_ref[0] = dp
        acc[...] += _dot_nt(dp, w1_ref[...])

        @pl.when(j == nj - 1)
        def _():
            xn, inv = _rms(x_ref[0])
            dh = acc[...]
            dx_ref[0] = dxo() + _rms_bwd(xn, inv, dh * (1.0 + mod_ref[0, 1:2]))
            _add_rows(dmod_ref, ti == 0, [_colsum(dh), _colsum(dh * xn), _colsum(dxo() * ff_ref[0])])

    big = lambda: pl.BlockSpec((1, tm, fj), lambda e, i, j: (e, i, j))
    in_specs = [_tok(tm, d), _tok(tm, d), _tok(tm, d), big(), _per_example(8, d),
                pl.BlockSpec((d, fj), lambda e, i, j: (0, j)), pl.BlockSpec((fj, d), lambda e, i, j: (j, 0))]
    out_specs = [_tok(tm, d), _per_example(8, d), _tok(tm, d), _tok(tm, d), big()]
    out_shape = [_sds(x.shape), _sds((bl, 8, d)), _sds(x.shape, BF16), _sds(x.shape, BF16), _sds((bl, t, f), BF16)]
    scratch = [pltpu.VMEM((tm, d), F32)]
    operands = (x, dxo, ff, q, mod, w1, w2)
    if norm is not None:
        x_mix, dh_mix, dhprev, mod_mix, dgate = norm
        hb, last_blk = tm // 8, t // 8 - 1
        nxt = pl.BlockSpec((1, 8, d), lambda e, i, j: (e, jnp.minimum((i + 1) * hb, last_blk), 0))
        in_specs += [_tok(tm, d), _tok(tm, d), _tok(tm, d), nxt, _per_example(8, d), _per_example(8, d)]
        out_specs.append(_per_example(8, d))
        out_shape.append(_sds((bl, 8, d)))
        scratch.append(pltpu.VMEM((tm, d), F32))
        operands += (x_mix, dh_mix, dhprev, dhprev, mod_mix, dgate)
    return _call_with_scatter(
        body, name="mlp_bwd", grid=(bl, nt, nj), in_specs=in_specs, out_specs=tuple(out_specs),
        out_shape=tuple(out_shape), scratch_shapes=scratch, operands=operands, scatter=scatter)


SGU_TM = 512


def _sgu_core(x, mod_ref, win_ref, lng, lnb, ws_ref, bias_ref):
    tm, d = x.shape
    xn, inv = _rms(x)
    h = (xn * (1.0 + mod_ref[0, 1:2]) + mod_ref[0, 0:1]).astype(BF16)
    pre = jnp.dot(h, win_ref[...], preferred_element_type=F32)
    uv = _gelu(pre)
    u, v = uv[:, :d], uv[:, d:]
    mu = jnp.mean(v, axis=-1, keepdims=True)
    vc = v - mu
    rstd = lax.rsqrt(jnp.mean(vc * vc, axis=-1, keepdims=True) + LN_EPS)
    vhat = vc * rstd
    vln = vhat * lng + lnb
    gd = d // SGU_GROUPS
    rows = []
    for c in range(tm // SGU_CHUNK):
        cols = []
        for g in range(SGU_GROUPS):
            cols.append(_dot(ws_ref[g], vln[c * SGU_CHUNK:(c + 1) * SGU_CHUNK, g * gd:(g + 1) * gd]))
        rows.append(jnp.concatenate(cols, axis=1) + bias_ref[...])
    sv = jnp.concatenate(rows, axis=0)
    return xn, inv, h, pre, u, vhat, rstd, vln, sv


def _sgu_masked(ws_ref, wm_scr):
    row = lax.broadcasted_iota(jnp.int32, (SGU_CHUNK, SGU_CHUNK), 0)
    col = lax.broadcasted_iota(jnp.int32, (SGU_CHUNK, SGU_CHUNK), 1)
    for g in range(SGU_GROUPS):
        wm_scr[g] = jnp.where(row >= col, ws_ref[g], 0.0).astype(BF16)


def _sgu_fwd(x, mod, w_in, ln_g, ln_b, w_s, bias_full, w_out, gather=()):
    bl, t, d = x.shape
    tm = min(t, SGU_TM)

    def body(x_ref, mod_ref, win_ref, lng_ref, lnb_ref, ws_ref, bias_ref, wout_ref, xo_ref, mix_ref, wm_scr):
        _sgu_masked(ws_ref, wm_scr)
        xt = x_ref[0]
        *_, u, _, _, _, sv = _sgu_core(xt, mod_ref, win_ref, lng_ref[...], lnb_ref[...], wm_scr, bias_ref)
        mix = _dot(u * sv, wout_ref[...])
        mix_ref[0] = mix
        xo_ref[0] = xt + mod_ref[0, 2:3] * mix

    return _call_with_scatter(
        body, name="sgu_fwd", grid=(bl, t // tm),
        in_specs=[_tok(tm, d), _per_example(8, d), _whole(w_in.shape), _whole(ln_g.shape), _whole(ln_b.shape),
                  _whole(w_s.shape), _whole(bias_full.shape), _whole(w_out.shape)],
        out_specs=(_tok(tm, d), _tok(tm, d)),
        out_shape=(_sds(x.shape), _sds(x.shape)),
        scratch_shapes=[pltpu.VMEM(w_s.shape, BF16)],
        operands=(x, mod, w_in, ln_g, ln_b, w_s, bias_full, w_out), gather=gather)


def _sgu_bwd(x, dxo, mix, mod, w_in, ln_g, ln_b, w_s, bias_full, w_out, group_ind, scatter=()):
    bl, t, d = x.shape
    tm = min(t, SGU_TM)
    gd = d // SGU_GROUPS

    def body(x_ref, dxo_ref, mix_ref, mod_ref, win_ref, lng_ref, lnb_ref, ws_ref, bias_ref, wout_ref, ind_ref,
             dx_ref, dmod_ref, h_ref, dpre_ref, z_ref, dmix_ref, small_ref, dws_ref, dbs_ref, wm_scr, dbias_scr):
        e, ti = pl.program_id(0), pl.program_id(1)
        _sgu_masked(ws_ref, wm_scr)
        xt, dxo = x_ref[0], dxo_ref[0]
        lng = lng_ref[...]
        xn, inv, h, pre, u, vhat, rstd, vln, sv = _sgu_core(xt, mod_ref, win_ref, lng, lnb_ref[...], wm_scr, bias_ref)
        h_ref[0] = h
        z_ref[0] = (u * sv).astype(BF16)
        dmix = mod_ref[0, 2:3] * dxo
        dmix_ref[0] = dmix.astype(BF16)
        dz = _dot_nt(dmix, wout_ref[...])
        du, dsv = dz * sv, dz * u

        @pl.when(_first(e, ti))
        def _():
            dws_ref[...] = jnp.zeros_like(dws_ref)
            dbias_scr[...] = jnp.zeros_like(dbias_scr)
            small_ref[...] = jnp.zeros_like(small_ref)

        row = lax.broadcasted_iota(jnp.int32, (SGU_CHUNK, SGU_CHUNK), 0)
        col = lax.broadcasted_iota(jnp.int32, (SGU_CHUNK, SGU_CHUNK), 1)
        rows = []
        for c in range(tm // SGU_CHUNK):
            rs = slice(c * SGU_CHUNK, (c + 1) * SGU_CHUNK)
            dbias_scr[...] += dsv[rs]
            cols = []
            for g in range(SGU_GROUPS):
                cs = slice(g * gd, (g + 1) * gd)
                cols.append(_dot_tn(wm_scr[g], dsv[rs, cs]))
                dws_ref[g] += jnp.where(row >= col, _dot_nt(dsv[rs, cs], vln[rs, cs]), 0.0)
            rows.append(jnp.concatenate(cols, axis=1))
        dvln = jnp.concatenate(rows, axis=0)
        small_ref[0:1] += _colsum(dvln * vhat)
        small_ref[1:2] += _colsum(dvln)
        dvhat = dvln * lng
        dv = rstd * (dvhat - jnp.mean(dvhat, axis=-1, keepdims=True)
                     - vhat * jnp.mean(dvhat * vhat, axis=-1, keepdims=True))
        dpre = (jnp.concatenate([du, dv], axis=1) * _gelu_grad(pre)).astype(BF16)
        dpre_ref[0] = dpre
        dh = _dot_nt(dpre, win_ref[...])
        dx_ref[0] = dxo + _rms_bwd(xn, inv, dh * (1.0 + mod_ref[0, 1:2]))
        _add_rows(dmod_ref, ti == 0, [_colsum(dh), _colsum(dh * xn), _colsum(dxo * mix_ref[0])])

        @pl.when(jnp.logical_and(e == bl - 1, ti == t // tm - 1))
        def _():
            hi, lo = _split_bf16(dbias_scr[...])
            ind = ind_ref[...]
            dbs_ref[...] = (lax.dot_general(ind, hi, (((1,), (1,)), ((), ())), preferred_element_type=F32)
                            + lax.dot_general(ind, lo, (((1,), (1,)), ((), ())), preferred_element_type=F32))

    return _call_with_scatter(
        body, name="sgu_bwd", grid=(bl, t // tm),
        in_specs=[_tok(tm, d), _tok(tm, d), _tok(tm, d), _per_example(8, d), _whole(w_in.shape), _whole(ln_g.shape),
                  _whole(ln_b.shape), _whole(w_s.shape), _whole(bias_full.shape), _whole(w_out.shape),
                  _whole(group_ind.shape)],
        out_specs=(_tok(tm, d), _per_example(8, d), _tok(tm, d), _tok(tm, 2 * d), _tok(tm, d), _tok(tm, d),
                   _whole((8, d)), _whole(w_s.shape), _whole((SGU_GROUPS, SGU_CHUNK))),
        out_shape=(_sds(x.shape), _sds((bl, 8, d)), _sds(x.shape, BF16), _sds((bl, t, 2 * d), BF16),
                   _sds(x.shape, BF16), _sds(x.shape, BF16), _sds((8, d)), _sds(w_s.shape),
                   _sds((SGU_GROUPS, SGU_CHUNK))),
        scratch_shapes=[pltpu.VMEM(w_s.shape, BF16), pltpu.VMEM((SGU_CHUNK, d), F32)],
        operands=(x, dxo, mix, mod, w_in, ln_g, ln_b, w_s, bias_full, w_out, group_ind), scatter=scatter)


RWKV_TM = 256
N_VEC = 16


def _rwkv_pre_core(x_ref, halo_ref, mod_ref, vec_ref, ti):
    xn, inv = _rms(x_ref[0])
    scale1, shift = 1.0 + mod_ref[0, 1:2], mod_ref[0, 0:1]
    h = xn * scale1 + shift
    hn, _ = _rms(halo_ref[0])
    hh = hn * scale1 + shift
    first = jnp.where(ti == 0, 0.0, hh[7:8])
    xx = _shift_down(h, first) - h
    xs = [h + xx * vec_ref[i:i + 1] for i in range(6)]
    return xn, inv, xx, xs


def _rwkv_proj(xs, wrkv_ref, w1_ref, a1_ref, g1_ref, w2_ref, a2_ref, g2_ref):
    d = xs[0].shape[1]
    xr, xw, xk, xv, xa, xg = [z.astype(BF16) for z in xs]
    r = jnp.dot(xr, wrkv_ref[:, 0:d], preferred_element_type=F32)
    k = jnp.dot(xk, wrkv_ref[:, d:2 * d], preferred_element_type=F32)
    v = jnp.dot(xv, wrkv_ref[:, 2 * d:3 * d], preferred_element_type=F32)
    tw2 = jnp.tanh(jnp.dot(xw, w1_ref[...], preferred_element_type=F32))
    ta = jnp.dot(xa, a1_ref[...], preferred_element_type=F32)
    sg = _sigmoid(jnp.dot(xg, g1_ref[...], preferred_element_type=F32))
    lw, la, g = _dot(tw2, w2_ref[...]), _dot(ta, a2_ref[...]), _dot(sg, g2_ref[...])
    return (xr, xw, xk, xv, xa, xg), r, k, v, tw2, ta, sg, lw, la, g


def _to_heads(ref, val, nh):
    for hd in range(nh):
        ref[0, hd] = val[:, hd * HEAD:(hd + 1) * HEAD]


def _from_heads(ref, scr, nh):
    for hd in range(nh):
        scr[:, hd * HEAD:(hd + 1) * HEAD] = ref[0, hd]
    return scr[...]


def _rwkv_weight_specs(ws):
    return [_whole(w.shape) for w in ws]


def _rwkv_pre_fwd(x, mod, vec, e_ind, et_ind, weights):
    bl, t, d = x.shape
    tm = min(t, RWKV_TM)
    nh = d // HEAD
    hb = tm // 8

    def body(x_ref, halo_ref, mod_ref, vec_ref, e_ref, et_ref, wrkv, w1, a1, g1, w2, a2, g2,
             r_ref, ld_ref, k2_ref, v_ref, as_ref, bs_ref, g_ref):
        ti = pl.program_id(1)
        _, _, _, xs = _rwkv_pre_core(x_ref, halo_ref, mod_ref, vec_ref, ti)
        _, r, k, v, _, _, _, lw, la, g = _rwkv_proj(xs, wrkv, w1, a1, g1, w2, a2, g2)
        headsum = _make_headsum(e_ref[...], et_ref[...])
        ld, k2, a_s, b_s = _rwkv_elem(r, k, lw, la, vec_ref[6:7], vec_ref[7:8], vec_ref[8:9], vec_ref[9:10], headsum)
        g_ref[0] = g
        for ref, val in ((r_ref, r), (ld_ref, ld), (k2_ref, k2), (v_ref, v), (as_ref, a_s), (bs_ref, b_s)):
            _to_heads(ref, val, nh)

    halo = pl.BlockSpec((1, 8, d), lambda e, i: (e, jnp.maximum(i * hb - 1, 0), 0))
    hs = _sds((bl, nh, t, HEAD))
    return pl.pallas_call(
        body, name="rwkv_pre_fwd", grid=(bl, t // tm),
        in_specs=[_tok(tm, d), halo, _per_example(8, d), _whole(vec.shape), _whole(e_ind.shape), _whole(et_ind.shape)]
        + _rwkv_weight_specs(weights),
        out_specs=(_heads(nh, tm),) * 6 + (_tok(tm, d),),
        out_shape=(hs,) * 6 + (_sds(x.shape),),
        compiler_params=_cparams(("arbitrary", "arbitrary")),
    )(x, x, mod, vec, e_ind, et_ind, *weights)


def _rwkv_post_fwd(x, y, r, k2, v, g, mod, vec, e_ind, et_ind, w_out):
    bl, t, d = x.shape
    tm = min(t, RWKV_TM)
    nh = d // HEAD

    def body(x_ref, y_ref, r_ref, k2_ref, v_ref, g_ref, mod_ref, vec_ref, e_ref, et_ref, wout_ref,
             xo_ref, mix_ref, s0, s1, s2, s3):
        headsum = _make_headsum(e_ref[...], et_ref[...])
        yv, rv, kv, vv = (_from_heads(ref, scr, nh) for ref, scr in
                          ((y_ref, s0), (r_ref, s1), (k2_ref, s2), (v_ref, s3)))
        o = _rwkv_post(yv, rv, kv, vv, g_ref[0], vec_ref[10:11], vec_ref[11:12], vec_ref[12:13], headsum)
        mix = _dot(o, wout_ref[...])
        mix_ref[0] = mix
        xo_ref[0] = x_ref[0] + mod_ref[0, 2:3] * mix

    return pl.pallas_call(
        body, name="rwkv_post_fwd", grid=(bl, t // tm),
        in_specs=[_tok(tm, d)] + [_heads(nh, tm)] * 4 + [_tok(tm, d), _per_example(8, d), _whole(vec.shape),
                                                         _whole(e_ind.shape), _whole(et_ind.shape), _whole(w_out.shape)],
        out_specs=(_tok(tm, d), _tok(tm, d)),
        out_shape=(_sds(x.shape), _sds(x.shape)),
        scratch_shapes=[pltpu.VMEM((tm, d), F32)] * 4,
        compiler_params=_cparams(("arbitrary", "arbitrary")),
    )(x, y, r, k2, v, g, mod, vec, e_ind, et_ind, w_out)


def _rwkv_post_bwd(dxo, mix, y, r, k2, v, g, mod, vec, e_ind, et_ind, w_out, scatter=()):
    bl, t, d = dxo.shape
    tm = min(t, RWKV_TM)
    nh = d // HEAD

    def body(dxo_ref, mix_ref, y_ref, r_ref, k2_ref, v_ref, g_ref, mod_ref, vec_ref, e_ref, et_ref, wout_ref,
             dy_ref, dr_ref, dk2_ref, dv_ref, dg_ref, o_ref, dmix_ref, dgate_ref, small_ref, s0, s1, s2, s3):
        e, ti = pl.program_id(0), pl.program_id(1)
        headsum = _make_headsum(e_ref[...], et_ref[...])
        yv, rv, kv, vv = (_from_heads(ref, scr, nh) for ref, scr in
                          ((y_ref, s0), (r_ref, s1), (k2_ref, s2), (v_ref, s3)))
        dxo = dxo_ref[0]
        dmix = mod_ref[0, 2:3] * dxo
        dmix_ref[0] = dmix.astype(BF16)
        do = _dot_nt(dmix, wout_ref[...])
        post = functools.partial(_rwkv_post, headsum=headsum)
        o, vjp = jax.vjp(post, yv, rv, kv, vv, g_ref[0], vec_ref[10:11], vec_ref[11:12], vec_ref[12:13])
        o_ref[0] = o.astype(BF16)
        dy, dr, dk2, dv, dg, dlng, dlnb, drk = vjp(do)
        _to_heads(dy_ref, dy, nh)
        dr_ref[0], dk2_ref[0], dv_ref[0], dg_ref[0] = dr, dk2, dv, dg
        zero = jnp.zeros((1, d), F32)
        _add_rows(dgate_ref, ti == 0, [zero, zero, _colsum(dxo * mix_ref[0])])

        @pl.when(_first(e, ti))
        def _():
            small_ref[...] = jnp.zeros_like(small_ref)

        small_ref[0:1] += dlng
        small_ref[1:2] += dlnb
        small_ref[2:3] += drk

    return _call_with_scatter(
        body, name="rwkv_post_bwd", grid=(bl, t // tm),
        in_specs=[_tok(tm, d), _tok(tm, d)] + [_heads(nh, tm)] * 4
        + [_tok(tm, d), _per_example(8, d), _whole(vec.shape), _whole(e_ind.shape), _whole(et_ind.shape),
           _whole(w_out.shape)],
        out_specs=(_heads(nh, tm),) + (_tok(tm, d),) * 6 + (_per_example(8, d), _whole((8, d))),
        out_shape=(_sds((bl, nh, t, HEAD)),) + (_sds(dxo.shape),) * 4 + (_sds(dxo.shape, BF16),) * 2
        + (_sds((bl, 8, d)), _sds((8, d))),
        scratch_shapes=[pltpu.VMEM((tm, d), F32)] * 4,
        operands=(dxo, mix, y, r, k2, v, g, mod, vec, e_ind, et_ind, w_out), scatter=scatter)


RWKV_BWD_TM = 128


def _rwkv_pre_bwd(x, mod, vec, e_ind, et_ind, weights, dr_p, dk2_p, dv_p, dg, dr_s, dld, dk2_s, dv_s, das, dbs):
    bl, t, d = x.shape
    tm = min(t, RWKV_BWD_TM)
    nh = d // HEAD
    hb = tm // 8
    lp, gp = LORA_PAD, GATE_PAD

    def body(x_ref, halo_ref, mod_ref, vec_ref, e_ref, et_ref, wrkv, w1, a1, g1, w2, a2, g2,
             drp_ref, dk2p_ref, dvp_ref, dg_ref, drs_ref, dld_ref, dk2s_ref, dvs_ref, das_ref, dbs_ref,
             dh_ref, dhp_ref, xr_ref, xw_ref, xk_ref, xv_ref, xa_ref, xg_ref, dr_ref, dk_ref, dv_ref,
             dtw_ref, dta_ref, dtg_ref, tw2_ref, ta_ref, sg_ref, dlw_ref, dla_ref, dgb_ref, small_ref,
             s0, s1, s2, s3, s4, s5):
        e, ti = pl.program_id(0), pl.program_id(1)
        _, _, xx, xs = _rwkv_pre_core(x_ref, halo_ref, mod_ref, vec_ref, ti)
        xb, r, k, v, tw2, ta, sg, lw, la, _ = _rwkv_proj(xs, wrkv, w1, a1, g1, w2, a2, g2)
        for ref, val in zip((xr_ref, xw_ref, xk_ref, xv_ref, xa_ref, xg_ref), xb):
            ref[0] = val
        headsum = _make_headsum(e_ref[...], et_ref[...])
        drs, dld, dk2s, dvs, das, dbs_ = (_from_heads(ref, scr, nh) for ref, scr in
                                          ((drs_ref, s0), (dld_ref, s1), (dk2s_ref, s2), (dvs_ref, s3),
                                           (das_ref, s4), (dbs_ref, s5)))
        elem = functools.partial(_rwkv_elem, r, headsum=headsum)
        _, vjp = jax.vjp(elem, k, lw, la, vec_ref[6:7], vec_ref[7:8], vec_ref[8:9], vec_ref[9:10])
        dk, dlw, dla, dw0, da0, dkk, dka = vjp((dld, dk2p_ref[0] + dk2s, das, dbs_))
        dr = drp_ref[0] + drs
        dv = dvp_ref[0] + dvs
        dgv = dg_ref[0]
        dtg = _dot_nt(dgv, g2[...]) * sg * (1.0 - sg)
        dtw = _dot_nt(dlw, w2[...]) * (1.0 - tw2 * tw2)
        dta = _dot_nt(dla, a2[...])
        dr_ref[0], dk_ref[0], dv_ref[0] = dr.astype(BF16), dk.astype(BF16), dv.astype(BF16)
        dtw_ref[0], dta_ref[0], dtg_ref[0] = dtw.astype(BF16), dta.astype(BF16), dtg.astype(BF16)
        tw2_ref[0], ta_ref[0], sg_ref[0] = tw2.astype(BF16), ta.astype(BF16), sg.astype(BF16)
        dlw_ref[0], dla_ref[0], dgb_ref[0] = dlw.astype(BF16), dla.astype(BF16), dgv.astype(BF16)
        dxs = (_dot_nt(dr, wrkv[:, 0:d]), _dot_nt(dtw, w1[...]), _dot_nt(dk, wrkv[:, d:2 * d]),
               _dot_nt(dv, wrkv[:, 2 * d:3 * d]), _dot_nt(dta, a1[...]), _dot_nt(dtg, g1[...]))

        @pl.when(_first(e, ti))
        def _():
            small_ref[...] = jnp.zeros_like(small_ref)

        total = jnp.zeros((tm, d), F32)
        dhp = jnp.zeros((tm, d), F32)
        for i, dxi in enumerate(dxs):
            total += dxi
            dhp += dxi * vec_ref[i:i + 1]
            small_ref[i:i + 1] += _colsum(dxi * xx)
        dh_ref[0], dhp_ref[0] = total - dhp, dhp
        small_ref[6:7] += dw0
        small_ref[7:8] += da0
        small_ref[8:9] += dkk
        small_ref[9:10] += dka

    halo = pl.BlockSpec((1, 8, d), lambda e, i: (e, jnp.maximum(i * hb - 1, 0), 0))
    tokd, tokl, tokg = _tok(tm, d), _tok(tm, lp), _tok(tm, gp)
    bf = lambda w: _sds((bl, t, w), BF16)
    return pl.pallas_call(
        body, name="rwkv_pre_bwd", grid=(bl, t // tm),
        in_specs=[tokd, halo, _per_example(8, d), _whole(vec.shape), _whole(e_ind.shape), _whole(et_ind.shape)]
        + _rwkv_weight_specs(weights) + [tokd] * 4 + [_heads(nh, tm)] * 6,
        out_specs=(tokd, tokd) + (tokd,) * 6 + (tokd,) * 3 + (tokl, tokl, tokg, tokl, tokl, tokg)
        + (tokd, tokd, tokd, _whole((N_VEC, d))),
        out_shape=(_sds(x.shape), _sds(x.shape)) + (bf(d),) * 9 + (bf(lp), bf(lp), bf(gp), bf(lp), bf(lp), bf(gp))
        + (bf(d), bf(d), bf(d), _sds((N_VEC, d))),
        scratch_shapes=[pltpu.VMEM((tm, d), F32)] * 6,
        compiler_params=_cparams(("arbitrary", "arbitrary")),
    )(x, x, mod, vec, e_ind, et_ind, *weights, dr_p, dk2_p, dv_p, dg, dr_s, dld, dk2_s, dv_s, das, dbs)


NORM_BWD_TM = 512
FINAL_TM = 1024


def _norm_bwd(x, dxo, dh, dhprev, mod, dgate):
    bl, t, d = x.shape
    tm = min(t, NORM_BWD_TM)
    hb = tm // 8
    last_blk = t // 8 - 1

    def body(x_ref, dxo_ref, dh_ref, dhp_ref, nxt_ref, mod_ref, dgate_ref, dx_ref, dmod_ref):
        ti = pl.program_id(1)
        xn, inv = _rms(x_ref[0])
        last = jnp.where(ti == t // tm - 1, 0.0, nxt_ref[0, 0:1])
        dh = dh_ref[0] + _shift_up(dhp_ref[0], last)
        dx_ref[0] = dxo_ref[0] + _rms_bwd(xn, inv, dh * (1.0 + mod_ref[0, 1:2]))

        @pl.when(ti == 0)
        def _():
            dmod_ref[0] = dgate_ref[0]

        dmod_ref[0, 0:1] += _colsum(dh)
        dmod_ref[0, 1:2] += _colsum(dh * xn)

    nxt = pl.BlockSpec((1, 8, d), lambda e, i: (e, jnp.minimum((i + 1) * hb, last_blk), 0))
    return pl.pallas_call(
        body, name="norm_bwd", grid=(bl, t // tm),
        in_specs=[_tok(tm, d)] * 4 + [nxt, _per_example(8, d), _per_example(8, d)],
        out_specs=(_tok(tm, d), _per_example(8, d)),
        out_shape=(_sds(x.shape), _sds((bl, 8, d))),
        compiler_params=_cparams(("arbitrary", "arbitrary")),
    )(x, dxo, dh, dhprev, dhprev, mod, dgate)


def _final(x, target, final_g):
    bl, t, d = x.shape
    tm = min(t, FINAL_TM)

    def body(x_ref, tgt_ref, g_ref, dx_ref, loss_ref, dg_ref):
        e, ti = pl.program_id(0), pl.program_id(1)

        @pl.when(_first(e, ti))
        def _():
            loss_ref[...] = jnp.zeros_like(loss_ref)
            dg_ref[...] = jnp.zeros_like(dg_ref)

        xn, inv = _rms(x_ref[0])
        err = xn * g_ref[...] - tgt_ref[0]
        loss_ref[...] += (0.5 / d) * jnp.sum(err * err)
        dy = err * (1.0 / d)
        dg_ref[0:1] += _colsum(dy * xn)
        dx_ref[0] = _rms_bwd(xn, inv, dy * g_ref[...])

    return pl.pallas_call(
        body, name="final_loss", grid=(bl, t // tm),
        in_specs=[_tok(tm, d), _tok(tm, d), _whole(final_g.shape)],
        out_specs=(_tok(tm, d), _whole((8, 128)), _whole((8, d))),
        out_shape=(_sds(x.shape), _sds((8, 128)), _sds((8, d))),
        compiler_params=_cparams(("arbitrary", "arbitrary")),
    )(x, target, final_g)


def _adamw_math(w, g, m, v):
    m = ADAM_B1 * m + (1.0 - ADAM_B1) * g
    v = ADAM_B2 * v + (1.0 - ADAM_B2) * jnp.square(g)
    m_hat = m / (1.0 - ADAM_B1 ** ADAM_STEP)
    v_hat = v / (1.0 - ADAM_B2 ** ADAM_STEP)
    return -ADAM_LR * (m_hat / (jnp.sqrt(v_hat) + ADAM_EPS) + ADAM_WD * w), m, v


def _sum_parts(ref, n):
    g = ref[0].astype(F32)
    for s in range(1, n):
        g = g + ref[s].astype(F32)
    return g


def _adamw_layers(w, m, v, parts, name):
    nl, rows, c = w.shape
    tr = min(rows, 256)

    def body(w_ref, m_ref, v_ref, *refs):
        p_refs, (g_ref, d_ref, mo_ref, vo_ref) = refs[:nl], refs[nl:]
        for layer in range(nl):
            @pl.when(pl.program_id(0) == layer)
            def _(p_ref=p_refs[layer]):
                g = _sum_parts(p_ref, p_ref.shape[0])
                g_ref[...] = g
                d_ref[...], mo_ref[...], vo_ref[...] = _adamw_math(w_ref[...], g, m_ref[...], v_ref[...])

    row = pl.BlockSpec((None, tr, c), lambda l, i: (l, i, 0))
    return pl.pallas_call(
        body, name=name, grid=(nl, rows // tr),
        in_specs=[row, row, row] + [pl.BlockSpec((p.shape[0], tr, c), lambda l, i, k=k: (0, jnp.where(l == k, i, 0), 0))
                                    for k, p in enumerate(parts)],
        out_specs=(row,) * 4, out_shape=(_sds(w.shape),) * 4,
        compiler_params=_cparams(("arbitrary", "arbitrary")),
    )(w, m, v, *parts)


def _adamw_small(items, name):
    k = len(items)
    ns = [it[3].shape[0] for it in items]

    def body(*refs):
        ins, outs = refs[:4 * k], refs[4 * k:]
        for i in range(k):
            w_ref, m_ref, v_ref, p_ref = ins[4 * i:4 * i + 4]
            g = _sum_parts(p_ref, ns[i])
            outs[4 * i][...] = g
            outs[4 * i + 1][...], outs[4 * i + 2][...], outs[4 * i + 3][...] = _adamw_math(
                w_ref[...], g, m_ref[...], v_ref[...])

    flat = [a for it in items for a in it]
    res = pl.pallas_call(
        body, name=name,
        out_shape=tuple(_sds(it[0].shape) for it in items for _ in range(4)),
        compiler_params=_cparams(),
    )(*flat)
    return [tuple(res[4 * i:4 * i + 4]) for i in range(k)]


WEIGHTS = ['ada_w', 'ada_b', 'mlp_w1', 'mlp_w2', 'a_w_in', 'a_ln_g', 'a_ln_b', 'a_w_s', 'a_b_s', 'a_w_out', 'b_mu',
           'b_w_in', 'b_w0', 'b_w1', 'b_w2', 'b_a0', 'b_a1', 'b_a2', 'b_g1', 'b_g2', 'b_k_k', 'b_k_a', 'b_r_k',
           'b_ln_g', 'b_ln_b', 'b_w_out', 'final_g']
VECTORS = ['b_mu', 'b_w0', 'b_a0', 'b_k_k', 'b_k_a', 'b_ln_g', 'b_ln_b']
REPLICATED = ['a_ln_g', 'a_ln_b', 'a_w_s', 'a_b_s', 'b_r_k', 'final_g']
ROW_ALIGN = 16


def _pad_rows(a, mult):
    pad = (-a.shape[-2]) % mult
    return jnp.pad(a, [(0, 0)] * (a.ndim - 2) + [(0, pad), (0, 0)]) if pad else a


def _as2d(a):
    if a.ndim == 1:
        return a.reshape(1, -1)
    lead = 1
    for s in a.shape[:-1]:
        lead *= s
    return a.reshape(lead, a.shape[-1])


def kernel(x, c, ada_w, ada_b, mlp_w1, mlp_w2, a_w_in, a_ln_g, a_ln_b, a_w_s, a_b_s, a_w_out, b_mu, b_w_in, b_w0, b_w1, b_w2, b_a0, b_a1, b_a2, b_g1, b_g2, b_k_k, b_k_a, b_r_k, b_ln_g, b_ln_b, b_w_out, final_g, loss_target, m_ada_w, m_ada_b, m_mlp_w1, m_mlp_w2, m_a_w_in, m_a_ln_g, m_a_ln_b, m_a_w_s, m_a_b_s, m_a_w_out, m_b_mu, m_b_w_in, m_b_w0, m_b_w1, m_b_w2, m_b_a0, m_b_a1, m_b_a2, m_b_g1, m_b_g2, m_b_k_k, m_b_k_a, m_b_r_k, m_b_ln_g, m_b_ln_b, m_b_w_out, m_final_g, v_ada_w, v_ada_b, v_mlp_w1, v_mlp_w2, v_a_w_in, v_a_ln_g, v_a_ln_b, v_a_w_s, v_a_b_s, v_a_w_out, v_b_mu, v_b_w_in, v_b_w0, v_b_w1, v_b_w2, v_b_a0, v_b_a1, v_b_a2, v_b_g1, v_b_g2, v_b_k_k, v_b_k_a, v_b_r_k, v_b_ln_g, v_b_ln_b, v_b_w_out, v_final_g):
    given = dict(locals())
    w = {n: given[n] for n in WEIGHTS}
    bl, t, d = x.shape
    nl = ada_w.shape[0]
    nb = N_DEV * bl
    m_tok = bl * t
    me = 4 * lax.axis_index("x") + 2 * lax.axis_index("y") + lax.axis_index("c")

    bf = lambda a: a.astype(BF16)
    vec_loc = _pad_rows(jnp.concatenate([_as2d(w[n]) for n in VECTORS], axis=0), ROW_ALIGN)
    g_c, g_vec, g_a_in, g_a_out = _gather_call((c, vec_loc, bf(a_w_in[0]), bf(a_w_out[0])), "gather_first")

    c_all = g_c.reshape(nb, d)
    cols = ada_w.shape[2]
    ada_b_cols = lax.dynamic_slice(ada_b, (0, me * cols), (nl, cols)).reshape(nl, 1, cols)
    mod_cols = _ada_fwd(c_all, ada_w, ada_b_cols)
    mod_full = jnp.moveaxis(_gather_call((mod_cols,), "gather_mod")[0], 0, 2).reshape(nl, nb, 6 * d)
    mod_mine = lax.dynamic_slice(mod_full, (0, me * bl, 0), (nl, bl, 6 * d)).reshape(nl, bl, 6, d)
    mod_mix = jnp.pad(mod_mine[:, :, 0:3], ((0, 0), (0, 0), (0, 5), (0, 0)))
    mod_mlp = jnp.pad(mod_mine[:, :, 3:6], ((0, 0), (0, 0), (0, 5), (0, 0)))

    def unshard(g, ax):
        g = jnp.moveaxis(g, 0, ax)
        return g.reshape(g.shape[:ax] + (g.shape[ax] * g.shape[ax + 1],) + g.shape[ax + 2:])

    lora_names = ['b_w1', 'b_a1', 'b_g1', 'b_w2', 'b_a2', 'b_g2']
    lora_pack = jnp.concatenate([bf(w[n]).reshape(-1) for n in lora_names]).reshape(-1, 128)
    full = {'a_w_in': unshard(g_a_in, 1), 'a_w_out': unshard(g_a_out, 0)}
    n_vec_rows = sum(_as2d(w[n]).shape[0] for n in VECTORS)
    vec = jnp.moveaxis(g_vec, 0, 1).reshape(N_VEC, d)
    vec = vec.at[n_vec_rows].set(b_r_k.reshape(d))

    e_ind, et_ind = _head_indicators(d)
    gd = d // SGU_GROUPS
    group_ind = (jnp.arange(SGU_GROUPS)[:, None] == jnp.arange(d)[None, :] // gd).astype(BF16)
    bias_full = jnp.repeat(a_b_s[0].T, gd, axis=1)
    pad_c = lambda a, n: jnp.pad(a, ((0, 0), (0, n - a.shape[1])))
    pad_r = lambda a, n: jnp.pad(a, ((0, n - a.shape[0]), (0, 0)))
    sgu_args = (full['a_w_in'], a_ln_g, a_ln_b, a_w_s[0], bias_full, full['a_w_out'])

    x0 = x
    (x1, mix_a), (g_w1_0, g_w2_0) = _sgu_fwd(x0, mod_mix[0], *sgu_args, gather=(bf(mlp_w1[0]), bf(mlp_w2[0])))
    w1_full = [unshard(g_w1_0, 1), None]
    w2_full = [unshard(g_w2_0, 0), None]
    (x2, ff0, q0), (g_w1_1, g_w2_1, g_b_in, g_b_out, g_lora) = _mlp_fwd(
        x1, mod_mlp[0], w1_full[0], w2_full[0],
        gather=(bf(mlp_w1[1]), bf(mlp_w2[1]), bf(b_w_in[0]), bf(b_w_out[0]), lora_pack))
    w1_full[1], w2_full[1] = unshard(g_w1_1, 1), unshard(g_w2_1, 0)
    full['b_w_in'], full['b_w_out'] = unshard(g_b_in, 1), unshard(g_b_out, 0)
    lora_flat, lo = g_lora.reshape(N_DEV, -1), 0
    for n, ax in zip(lora_names, (0, 0, 0, 1, 1, 1)):
        loc = w[n].shape[1:]
        full[n] = unshard(lora_flat[:, lo:lo + w[n].size].reshape((N_DEV,) + loc), ax)
        lo += w[n].size
    rwkv_w = (full['b_w_in'], pad_c(full['b_w1'], LORA_PAD), pad_c(full['b_a1'], LORA_PAD),
              pad_c(full['b_g1'], GATE_PAD), pad_r(full['b_w2'], LORA_PAD), pad_r(full['b_a2'], LORA_PAD),
              pad_r(full['b_g2'], GATE_PAD))
    r, ld, k2, v, a_s, b_s, gate = _rwkv_pre_fwd(x2, mod_mix[1], vec, e_ind, et_ind, rwkv_w)
    y, s0, tinv = _wkv_fwd(r, ld, k2, v, a_s, b_s)
    x3, mix_b = _rwkv_post_fwd(x2, y, r, k2, v, gate, mod_mix[1], vec, e_ind, et_ind, full['b_w_out'])
    (x4, ff1, q1), _ = _mlp_fwd(x3, mod_mlp[1], w1_full[1], w2_full[1])
    dx4, loss_blk, dfinal = _final(x4, loss_target, final_g.reshape(1, d))
    loss = lax.psum(loss_blk[0, 0], ("x", "y", "c"))

    tok = lambda a: a.reshape(m_tok, a.shape[-1])
    shard_rows = lambda g: g.reshape((N_DEV, g.shape[0] // N_DEV) + g.shape[1:])
    (dx3, dmod_mlp1, h_b, dff_b, dp_b), _ = _mlp_bwd(x3, dx4, ff1, q1, mod_mlp[1], w1_full[1], w2_full[1])
    gw1_1 = _matmul_tn(tok(h_b), tok(dp_b), "grad_mlp_w1_l1", col_shards=N_DEV)
    gw2_1 = shard_rows(_matmul_tn(tok(q1), tok(dff_b), "grad_mlp_w2_l1"))
    (dy, dr_p, dk2_p, dv_p, dgate_act, o_b, dmix_b, dgate_b, small_post), (rw1_1, rw2_1) = _rwkv_post_bwd(
        dx3, mix_b, y, r, k2, v, gate, mod_mix[1], vec, e_ind, et_ind, full['b_w_out'], scatter=(gw1_1, gw2_1))
    g_b_w_out = shard_rows(_matmul_tn(tok(o_b), tok(dmix_b), "grad_b_w_out"))
    dr_s, dld, dk2_s, dv_s, das, dbs = _wkv_bwd(r, ld, k2, v, a_s, b_s, s0, tinv, dy)
    (dh, dhp, xr_b, xw_b, xk_b, xv_b, xa_b, xg_b, dr_b, dk_b, dv_b, dtw_b, dta_b, dtg_b, tw2_b, ta_b, sg_b,
     dlw_b, dla_b, dg_b, small_pre) = _rwkv_pre_bwd(x2, mod_mix[1], vec, e_ind, et_ind, rwkv_w,
                                                    dr_p, dk2_p, dv_p, dgate_act, dr_s, dld, dk2_s, dv_s, das, dbs)
    g_b_w_in = jnp.concatenate([_matmul_tn(tok(xr_b), tok(dr_b), "grad_b_w_r"),
                                _matmul_tn(tok(xk_b), tok(dk_b), "grad_b_w_k"),
                                _matmul_tn(tok(xv_b), tok(dv_b), "grad_b_w_v")], axis=1)
    shard_cols = lambda g: jnp.moveaxis(g.reshape(g.shape[0], N_DEV, g.shape[1] // N_DEV), 1, 0)
    g_b_w_in = shard_cols(g_b_w_in)
    lw_, lg_ = b_w1.shape[2], b_g1.shape[2]
    small_names = ['b_w1', 'b_a1', 'b_g1', 'b_w2', 'b_a2', 'b_g2'] + VECTORS
    small_parts = [
        shard_rows(_matmul_tn(tok(xw_b), tok(dtw_b), "grad_b_w1")[:, :lw_]),
        shard_rows(_matmul_tn(tok(xa_b), tok(dta_b), "grad_b_a1")[:, :lw_]),
        shard_rows(_matmul_tn(tok(xg_b), tok(dtg_b), "grad_b_g1")[:, :lg_]),
        shard_cols(_matmul_tn(tok(tw2_b), tok(dlw_b), "grad_b_w2")[:lw_]),
        shard_cols(_matmul_tn(tok(ta_b), tok(dla_b), "grad_b_a2")[:lw_]),
        shard_cols(_matmul_tn(tok(sg_b), tok(dg_b), "grad_b_g2")[:lg_]),
        shard_cols(jnp.concatenate([small_pre[0:10], small_post[0:2]], axis=0).astype(BF16)),
    ]
    small_flat = jnp.concatenate([p.reshape(N_DEV, -1) for p in small_parts], axis=1)
    lane = 128
    small_rows = -(-small_flat.shape[1] // (lane * ROW_ALIGN)) * ROW_ALIGN
    small_pack = jnp.pad(small_flat, ((0, 0), (0, small_rows * lane - small_flat.shape[1]))).reshape(
        N_DEV, small_rows, lane)
    (dx1, dmod_mlp0, h_b, dff_b, dp_b, dmod_mix1), (r_b_w_in, r_b_w_out, r_small) = _mlp_bwd(
        x1, dx3, ff0, q0, mod_mlp[0], w1_full[0], w2_full[0], scatter=(g_b_w_in, g_b_w_out, small_pack),
        norm=(x2, dh, dhp, mod_mix[1], dgate_b))
    gw1_0 = _matmul_tn(tok(h_b), tok(dp_b), "grad_mlp_w1_l0", col_shards=N_DEV)
    gw2_0 = shard_rows(_matmul_tn(tok(q0), tok(dff_b), "grad_mlp_w2_l0"))
    (dx0, dmod_mix0, h_b, dpre_b, z_b, dmix_b, small_sgu, d_ws, d_bs), (rw1_0, rw2_0) = _sgu_bwd(
        x0, dx1, mix_a, mod_mix[0], *sgu_args, group_ind, scatter=(gw1_0, gw2_0))
    dmod_mine = jnp.stack([jnp.concatenate([dmod_mix0[:, 0:3], dmod_mlp0[:, 0:3]], axis=1),
                           jnp.concatenate([dmod_mix1[:, 0:3], dmod_mlp1[:, 0:3]], axis=1)], axis=1)
    rep_g = {'a_ln_g': small_sgu[0:1], 'a_ln_b': small_sgu[1:2], 'a_w_s': d_ws.reshape(-1, d), 'a_b_s': d_bs.reshape(1, d),
             'b_r_k': small_post[2:3], 'final_g': dfinal[0:1]}
    rep_rows = [rep_g[n].shape[0] for n in REPLICATED]
    rep_pack = _pad_rows(jnp.concatenate([rep_g[n] for n in REPLICATED], axis=0), 8)
    g_a_w_in, (dmod_all, rep_all) = _matmul_tn(tok(h_b), tok(dpre_b), "grad_a_w_in", col_shards=N_DEV,
                                               gather=(dmod_mine.reshape(bl, nl * 6 * d), rep_pack))
    g_a_w_out = shard_rows(_matmul_tn(tok(z_b), tok(dmix_b), "grad_a_w_out"))
    r_a_w_in, r_a_w_out = _scatter_call((g_a_w_in, g_a_w_out), "scatter_sgu_grads")

    dmod_all = jnp.moveaxis(dmod_all.reshape(nb, nl, 6 * d), 0, 1)
    dmod_cols = lax.dynamic_slice(dmod_all, (0, 0, me * cols), (nl, nb, cols))
    g_ada_w, g_ada_b = _ada_bwd(c_all, dmod_cols, dmod_all)

    mom = {n: given['m_' + n] for n in WEIGHTS}
    var = {n: given['v_' + n] for n in WEIGHTS}
    out = {}
    as3d = lambda a: a.reshape((-1,) + a.shape[-2:])
    for n, parts in (('mlp_w1', [rw1_0, rw1_1]), ('mlp_w2', [rw2_0, rw2_1]), ('a_w_in', [r_a_w_in]),
                     ('a_w_out', [r_a_w_out]), ('b_w_in', [r_b_w_in]), ('b_w_out', [r_b_w_out]),
                     ('ada_w', list(g_ada_w))):
        res = _adamw_layers(as3d(w[n]), as3d(mom[n]), as3d(var[n]), parts, "adamw_" + n)
        out[n] = tuple(a.reshape(w[n].shape) for a in res)

    items, names = [], []

    def add(n, part):
        s2 = _as2d(w[n]).shape
        items.append((_as2d(w[n]), _as2d(mom[n]), _as2d(var[n]), part.reshape((part.shape[0],) + s2)))
        names.append(n)

    sflat = r_small.reshape(N_DEV, -1)
    so = 0
    for n in small_names:
        sz = w[n].size
        add(n, sflat[:, so:so + sz])
        so += sz
    ro = 0
    for n, nr in zip(REPLICATED, rep_rows):
        add(n, rep_all[:, ro:ro + nr])
        ro += nr
    add('ada_b', g_ada_b[None])
    for n, res in zip(names, _adamw_small(items, "adamw_small")):
        out[n] = tuple(a.reshape(w[n].shape) for a in res)

    return (loss, dx0, *[out[n][0] for n in WEIGHTS], *[out[n][1] for n in WEIGHTS],
            *[out[n][2] for n in WEIGHTS], *[out[n][3] for n in WEIGHTS])
```
